```python
import math
import jax, jax.numpy as jnp
from jax import lax
import numpy as np

D_MODEL = 1024
BATCH = 16
SEQ = 2048
DEPTH = 1

HEAD_DIM = 128
HEADS_PER_GROUP = 4
DILATED_GROUPS = ((128, 1), (512, 4), (2048, 16))
N_GROUPS = 3
N_ATTN_HEADS = N_GROUPS * HEADS_PER_GROUP
QKV_WIDTH = N_ATTN_HEADS * HEAD_DIM
ATTN_WIDTH = HEADS_PER_GROUP * HEAD_DIM
CONV_WIDTH = D_MODEL
CONV_K = 3
N_BUCKETS = 32
MAX_EXACT = 16
MAX_DISTANCE = 2048
BLOCK = 128
DEEPNORM_ALPHA = (2.0 * DEPTH) ** 0.25
DEEPNORM_BETA = (8.0 * DEPTH) ** -0.25
LN_EPS = 1e-5
NEG_INF = -1e30
COL_WIDTHS = (QKV_WIDTH, QKV_WIDTH, QKV_WIDTH, ATTN_WIDTH,
              CONV_WIDTH, CONV_WIDTH, CONV_WIDTH, CONV_WIDTH,
              D_MODEL, D_MODEL)
IN_COLS = 4 * QKV_WIDTH // 4 * 3 // 3 * 1 * 3 + ATTN_WIDTH + 4 * CONV_WIDTH + 2 * D_MODEL

kernel_name = "hybrid_dilated_attn_shortconv_gated_merge"


def _split_cols(proj):
    offsets = []
    acc = 0
    for w in COL_WIDTHS[:-1]:
        acc += w
        offsets.append(acc)
    return jnp.split(proj, offsets, axis=-1)


def _t5_bucket(dist):
    n = jnp.maximum(dist, 1).astype(jnp.float32)
    large = MAX_EXACT + (jnp.log(n / MAX_EXACT) / math.log(MAX_DISTANCE / MAX_EXACT)
                         * (N_BUCKETS - MAX_EXACT)).astype(jnp.int32)
    large = jnp.minimum(large, N_BUCKETS - 1)
    return jnp.where(dist < MAX_EXACT, dist, large)


def _dilated_group_attention(q, k, v, bias_tab, window, dilation):
    bsz, seq, n_h, e = q.shape
    sub_len = seq // dilation
    n_blk = -(-sub_len // BLOCK)
    sub_pad = n_blk * BLOCK
    n_steps = window // dilation

    def to_sub(t):
        t = t.reshape(bsz, sub_len, dilation, n_h, e).transpose(0, 2, 3, 1, 4)
        return jnp.pad(t, ((0, 0), (0, 0), (0, 0), (0, sub_pad - sub_len), (0, 0)))

    def band(t):
        t = jnp.pad(t, ((0, 0), (0, 0), (0, 0), (BLOCK, 0), (0, 0)))
        t = t.reshape(bsz, dilation, n_h, n_blk + 1, BLOCK, e)
        return jnp.concatenate([t[:, :, :, :-1], t[:, :, :, 1:]], axis=4)

    qb = to_sub(q).reshape(bsz, dilation, n_h, n_blk, BLOCK, e)
    kb = band(to_sub(k))
    vb = band(to_sub(v))

    a_idx = jnp.arange(BLOCK)[:, None]
    b_idx = jnp.arange(2 * BLOCK)[None, :]
    steps = a_idx + BLOCK - b_idx
    key_sub = jnp.arange(n_blk)[:, None, None] * BLOCK - BLOCK + b_idx[None]
    valid = ((steps >= 0) & (steps <= n_steps))[None] & (key_sub >= 0)
    bucket = _t5_bucket(jnp.maximum(steps, 0) * dilation)
    bias = bias_tab[bucket].transpose(2, 0, 1).astype(jnp.float32)

    s = jnp.einsum('bdhnqe,bdhnke->bdhnqk', qb, kb).astype(jnp.float32) * (HEAD_DIM ** -0.5)
    s = s + bias[None, None, :, None]
    s = jnp.where(valid[None, None, None], s, NEG_INF)
    lse = jax.nn.logsumexp(s, axis=-1)
    p = jnp.exp(s - lse[..., None]).astype(v.dtype)
    o = jnp.einsum('bdhnqk,bdhnke->bdhnqe', p, vb)
    o = o.reshape(bsz, dilation, n_h, sub_pad, e)[:, :, :, :sub_len]
    lse = lse.reshape(bsz, dilation, n_h, sub_pad)[..., :sub_len]
    o = o.transpose(0, 3, 1, 2, 4).reshape(bsz, seq, n_h, e)
    lse = lse.transpose(0, 3, 1, 2).reshape(bsz, seq, n_h)
    return o, lse


def _layer_norm(x, g, b):
    xf = x.astype(jnp.float32)
    mu = jnp.mean(xf, axis=-1, keepdims=True)
    var = jnp.mean(jnp.square(xf - mu), axis=-1, keepdims=True)
    return ((xf - mu) * lax.rsqrt(var + LN_EPS) * g + b).astype(x.dtype)


def _fwd_setup_inputs(seed: int = 0) -> dict:
    key = jax.random.key(seed)
    ks = jax.random.split(key, 14)
    f32 = jnp.float32
    x = jax.random.normal(ks[0], (BATCH, SEQ, D_MODEL), f32)
    c = jax.random.normal(ks[1], (BATCH, D_MODEL), f32)
    w_ada = jax.random.normal(ks[2], (DEPTH, D_MODEL, 3 * D_MODEL), f32) * (0.1 * D_MODEL ** -0.5)
    b_ada = jax.random.normal(ks[3], (DEPTH, 3 * D_MODEL), f32) * 0.01
    n_cols = sum(COL_WIDTHS)
    col_scale = np.ones((n_cols,), np.float32)
    col_scale[2 * QKV_WIDTH:3 * QKV_WIDTH] = DEEPNORM_BETA
    w_in = jax.random.normal(ks[4], (DEPTH, D_MODEL, n_cols), f32) * (D_MODEL ** -0.5) * jnp.asarray(col_scale)
    conv_w = jax.random.normal(ks[5], (DEPTH, CONV_K, CONV_WIDTH), f32) * (CONV_K ** -0.5)
    conv_b = jax.random.normal(ks[6], (DEPTH, CONV_WIDTH), f32) * 0.01
    rel_bias = jax.random.normal(ks[7], (N_BUCKETS, N_ATTN_HEADS), f32) * 0.5
    w_attn_out = jax.random.normal(ks[8], (DEPTH, ATTN_WIDTH, D_MODEL), f32) * (ATTN_WIDTH ** -0.5) * DEEPNORM_BETA
    w_conv_out = jax.random.normal(ks[9], (DEPTH, CONV_WIDTH, D_MODEL), f32) * (CONV_WIDTH ** -0.5) * DEEPNORM_BETA
    w_o = jax.random.normal(ks[10], (DEPTH, D_MODEL, D_MODEL), f32) * (D_MODEL ** -0.5) * DEEPNORM_BETA
    ln_g = 1.0 + jax.random.normal(ks[11], (DEPTH, D_MODEL), f32) * 0.01
    ln_b = jax.random.normal(ks[12], (DEPTH, D_MODEL), f32) * 0.01
    return {"x": x, "c": c, "w_ada": w_ada, "b_ada": b_ada, "w_in": w_in,
            "conv_w": conv_w, "conv_b": conv_b, "rel_bias": rel_bias,
            "w_attn_out": w_attn_out, "w_conv_out": w_conv_out, "w_o": w_o,
            "ln_g": ln_g, "ln_b": ln_b}


def _fwd_reference(x, c, w_ada, b_ada, w_in, conv_w, conv_b, rel_bias,
              w_attn_out, w_conv_out, w_o, ln_g, ln_b):
    bsz, seq, _ = x.shape
    for layer in range(DEPTH):
        mod = jax.nn.silu(c) @ w_ada[layer] + b_ada[layer]
        shift, scale, gate = jnp.split(mod, 3, axis=-1)
        h = x * (1.0 + scale[:, None]) + shift[:, None]

        proj = h @ w_in[layer]
        q, k, v, g_attn, u, b_gate, c_gate, g_conv, m_attn, m_conv = _split_cols(proj)

        q = q.reshape(bsz, seq, N_ATTN_HEADS, HEAD_DIM)
        k = k.reshape(bsz, seq, N_ATTN_HEADS, HEAD_DIM)
        v = v.reshape(bsz, seq, N_ATTN_HEADS, HEAD_DIM)
        outs, lses = [], []
        for gi, (window, dilation) in enumerate(DILATED_GROUPS):
            hs = slice(gi * HEADS_PER_GROUP, (gi + 1) * HEADS_PER_GROUP)
            o_g, lse_g = _dilated_group_attention(q[:, :, hs], k[:, :, hs], v[:, :, hs],
                                                  rel_bias[:, hs], window, dilation)
            outs.append(o_g)
            lses.append(lse_g)
        o_all = jnp.stack(outs, axis=0)
        wts = jax.nn.softmax(jnp.stack(lses, axis=0), axis=0)
        o = jnp.sum(wts[..., None].astype(o_all.dtype) * o_all, axis=0).reshape(bsz, seq, ATTN_WIDTH)
        a_out = (o * jax.nn.silu(g_attn)) @ w_attn_out[layer]

        z = c_gate * u
        zp = jnp.pad(z, ((0, 0), (CONV_K - 1, 0), (0, 0)))
        cw = conv_w[layer]
        y_conv = (cw[0] * zp[:, :-2] + cw[1] * zp[:, 1:-1] + cw[2] * zp[:, 2:]) + conv_b[layer]
        s_out = (b_gate * y_conv * jax.nn.silu(g_conv)) @ w_conv_out[layer]

        merged = jax.nn.sigmoid(m_attn) * a_out + jax.nn.sigmoid(m_conv) * s_out
        y = merged @ w_o[layer]

        x = _layer_norm(DEEPNORM_ALPHA * x + (1.0 + gate[:, None]) * y, ln_g[layer], ln_b[layer])
    return x


import jax as _jax
import jax.numpy as _jnp

TWIN_FORMAT = 'train_step'
FWD_PARAMS = ['x', 'c', 'w_ada', 'b_ada', 'w_in', 'conv_w', 'conv_b', 'rel_bias', 'w_attn_out', 'w_conv_out', 'w_o', 'ln_g', 'ln_b']
TWIN_WEIGHTS = ['w_ada', 'b_ada', 'w_in', 'conv_w', 'conv_b', 'rel_bias', 'w_attn_out', 'w_conv_out', 'w_o', 'ln_g', 'ln_b']
TWIN_DIFF_INPUT = 'x'
TWIN_INPUTS = ['x', 'c', 'w_ada', 'b_ada', 'w_in', 'conv_w', 'conv_b', 'rel_bias', 'w_attn_out', 'w_conv_out', 'w_o', 'ln_g', 'ln_b', 'loss_target', 'm_w_ada', 'm_b_ada', 'm_w_in', 'm_conv_w', 'm_conv_b', 'm_rel_bias', 'm_w_attn_out', 'm_w_conv_out', 'm_w_o', 'm_ln_g', 'm_ln_b', 'v_w_ada', 'v_b_ada', 'v_w_in', 'v_conv_w', 'v_conv_b', 'v_rel_bias', 'v_w_attn_out', 'v_w_conv_out', 'v_w_o', 'v_ln_g', 'v_ln_b']
TWIN_OUTPUTS = ['loss', 'grad_x', 'grad_w_ada', 'grad_b_ada', 'grad_w_in', 'grad_conv_w', 'grad_conv_b', 'grad_rel_bias', 'grad_w_attn_out', 'grad_w_conv_out', 'grad_w_o', 'grad_ln_g', 'grad_ln_b', 'delta_w_ada', 'delta_b_ada', 'delta_w_in', 'delta_conv_w', 'delta_conv_b', 'delta_rel_bias', 'delta_w_attn_out', 'delta_w_conv_out', 'delta_w_o', 'delta_ln_g', 'delta_ln_b', 'new_m_w_ada', 'new_m_b_ada', 'new_m_w_in', 'new_m_conv_w', 'new_m_conv_b', 'new_m_rel_bias', 'new_m_w_attn_out', 'new_m_w_conv_out', 'new_m_w_o', 'new_m_ln_g', 'new_m_ln_b', 'new_v_w_ada', 'new_v_b_ada', 'new_v_w_in', 'new_v_conv_w', 'new_v_conv_b', 'new_v_rel_bias', 'new_v_w_attn_out', 'new_v_w_conv_out', 'new_v_w_o', 'new_v_ln_g', 'new_v_ln_b']
TWIN_LEAF_KINDS = {'loss': 'loss', 'grad_x': 'grad_x', 'grad_w_ada': 'grad_w', 'grad_b_ada': 'grad_w', 'grad_w_in': 'grad_w', 'grad_conv_w': 'grad_w', 'grad_conv_b': 'grad_w', 'grad_rel_bias': 'grad_w', 'grad_w_attn_out': 'grad_w', 'grad_w_conv_out': 'grad_w', 'grad_w_o': 'grad_w', 'grad_ln_g': 'grad_w', 'grad_ln_b': 'grad_w', 'delta_w_ada': 'delta_w', 'delta_b_ada': 'delta_w', 'delta_w_in': 'delta_w', 'delta_conv_w': 'delta_w', 'delta_conv_b': 'delta_w', 'delta_rel_bias': 'delta_w', 'delta_w_attn_out': 'delta_w', 'delta_w_conv_out': 'delta_w', 'delta_w_o': 'delta_w', 'delta_ln_g': 'delta_w', 'delta_ln_b': 'delta_w', 'new_m_w_ada': 'new_m', 'new_m_b_ada': 'new_m', 'new_m_w_in': 'new_m', 'new_m_conv_w': 'new_m', 'new_m_conv_b': 'new_m', 'new_m_rel_bias': 'new_m', 'new_m_w_attn_out': 'new_m', 'new_m_w_conv_out': 'new_m', 'new_m_w_o': 'new_m', 'new_m_ln_g': 'new_m', 'new_m_ln_b': 'new_m', 'new_v_w_ada': 'new_v', 'new_v_b_ada': 'new_v', 'new_v_w_in': 'new_v', 'new_v_conv_w': 'new_v', 'new_v_conv_b': 'new_v', 'new_v_rel_bias': 'new_v', 'new_v_w_attn_out': 'new_v', 'new_v_w_conv_out': 'new_v', 'new_v_w_o': 'new_v', 'new_v_ln_g': 'new_v', 'new_v_ln_b': 'new_v'}


def _forward(args):
    return _fwd_reference(*[args[k] for k in FWD_PARAMS])


def _output_shape():
    out = _jax.eval_shape(lambda: _forward(_fwd_setup_inputs(0)))
    return out.shape, out.dtype

N_MICROBATCH = 1
ADAM_LR = 0.001
ADAM_B1 = 0.9
ADAM_B2 = 0.999
ADAM_EPS = 1e-08
ADAM_WD = 0.01
ADAM_STEP = 10
PER_EXAMPLE_BATCH_AXIS = {'x': 0, 'c': 0, 'loss_target': 0}
SHARED_INPUTS = []
_WEIGHT_DTYPES = {'w_ada': _jnp.float32, 'b_ada': _jnp.float32, 'w_in': _jnp.float32, 'conv_w': _jnp.float32, 'conv_b': _jnp.float32, 'rel_bias': _jnp.float32, 'w_attn_out': _jnp.float32, 'w_conv_out': _jnp.float32, 'w_o': _jnp.float32, 'ln_g': _jnp.float32, 'ln_b': _jnp.float32}
MOMENT_SCALE = {'w_ada': 1.942260e-02, 'b_ada': 3.281208e-02, 'w_in': 1.074743e-02, 'conv_w': 1.758595e-02, 'conv_b': 1.686709e-02, 'rel_bias': 2.638223e-03, 'w_attn_out': 3.632913e-03, 'w_conv_out': 2.888306e-02, 'w_o': 2.916022e-02, 'ln_g': 3.197153e+01, 'ln_b': 3.091025e-01}


def _to_microbatches(a, axis):
    t = _jnp.moveaxis(a, axis, 0)
    t = t.reshape((N_MICROBATCH, t.shape[0] // N_MICROBATCH) + t.shape[1:])
    return _jnp.moveaxis(t, 1, axis + 1)


def setup_inputs(seed: int = 0) -> dict:
    inp = _fwd_setup_inputs(seed)
    key = _jax.random.fold_in(_jax.random.key(seed), 7919)
    shape, _ = _output_shape()
    out = dict(inp)
    out["loss_target"] = _jax.random.normal(_jax.random.fold_in(key, 0), shape, _jnp.float32)
    for i, name in enumerate(TWIN_WEIGHTS):
        w = inp[name].astype(_jnp.float32)
        if MOMENT_SCALE is None:
            s = _jnp.sqrt(_jnp.mean(_jnp.square(w)) + 1e-30)
        else:
            s = MOMENT_SCALE[name]
        km, kv = _jax.random.split(_jax.random.fold_in(key, i + 1))
        out[name] = w
        out["m_" + name] = s * _jax.random.normal(km, w.shape, _jnp.float32)
        out["v_" + name] = (s * s) * _jax.random.uniform(kv, w.shape, _jnp.float32, 0.5, 1.5)
    if N_MICROBATCH > 1:
        for name, axis in PER_EXAMPLE_BATCH_AXIS.items():
            out[name] = _to_microbatches(out[name], axis)
    return {'x': out['x'], 'c': out['c'], 'w_ada': out['w_ada'], 'b_ada': out['b_ada'], 'w_in': out['w_in'], 'conv_w': out['conv_w'], 'conv_b': out['conv_b'], 'rel_bias': out['rel_bias'], 'w_attn_out': out['w_attn_out'], 'w_conv_out': out['w_conv_out'], 'w_o': out['w_o'], 'ln_g': out['ln_g'], 'ln_b': out['ln_b'], 'loss_target': out['loss_target'], 'm_w_ada': out['m_w_ada'], 'm_b_ada': out['m_b_ada'], 'm_w_in': out['m_w_in'], 'm_conv_w': out['m_conv_w'], 'm_conv_b': out['m_conv_b'], 'm_rel_bias': out['m_rel_bias'], 'm_w_attn_out': out['m_w_attn_out'], 'm_w_conv_out': out['m_w_conv_out'], 'm_w_o': out['m_w_o'], 'm_ln_g': out['m_ln_g'], 'm_ln_b': out['m_ln_b'], 'v_w_ada': out['v_w_ada'], 'v_b_ada': out['v_b_ada'], 'v_w_in': out['v_w_in'], 'v_conv_w': out['v_conv_w'], 'v_conv_b': out['v_conv_b'], 'v_rel_bias': out['v_rel_bias'], 'v_w_attn_out': out['v_w_attn_out'], 'v_w_conv_out': out['v_w_conv_out'], 'v_w_o': out['v_w_o'], 'v_ln_g': out['v_ln_g'], 'v_ln_b': out['v_ln_b']}


def _loss(weights, diff, rest, loss_target):
    with _jax.named_scope("forward"):
        args = {**rest, TWIN_DIFF_INPUT: diff, **{k: w.astype(_WEIGHT_DTYPES[k]) for k, w in weights.items()}}
        y = _forward(args)
    with _jax.named_scope("loss_head"):
        err = _jnp.square(y.astype(_jnp.float32) - loss_target)
        return 0.5 * _jnp.sum(_jnp.mean(err, axis=-1)) if err.ndim else 0.5 * err


def _adamw(w, g, m, v):
    m = ADAM_B1 * m + (1.0 - ADAM_B1) * g
    v = ADAM_B2 * v + (1.0 - ADAM_B2) * _jnp.square(g)
    m_hat = m / (1.0 - ADAM_B1 ** ADAM_STEP)
    v_hat = v / (1.0 - ADAM_B2 ** ADAM_STEP)
    delta = -ADAM_LR * (m_hat / (_jnp.sqrt(v_hat) + ADAM_EPS) + ADAM_WD * w)
    return delta, m, v


def reference(x, c, w_ada, b_ada, w_in, conv_w, conv_b, rel_bias, w_attn_out, w_conv_out, w_o, ln_g, ln_b, loss_target, m_w_ada, m_b_ada, m_w_in, m_conv_w, m_conv_b, m_rel_bias, m_w_attn_out, m_w_conv_out, m_w_o, m_ln_g, m_ln_b, v_w_ada, v_b_ada, v_w_in, v_conv_w, v_conv_b, v_rel_bias, v_w_attn_out, v_w_conv_out, v_w_o, v_ln_g, v_ln_b):
    given = dict(x=x, c=c, w_ada=w_ada, b_ada=b_ada, w_in=w_in, conv_w=conv_w, conv_b=conv_b, rel_bias=rel_bias, w_attn_out=w_attn_out, w_conv_out=w_conv_out, w_o=w_o, ln_g=ln_g, ln_b=ln_b, loss_target=loss_target, m_w_ada=m_w_ada, m_b_ada=m_b_ada, m_w_in=m_w_in, m_conv_w=m_conv_w, m_conv_b=m_conv_b, m_rel_bias=m_rel_bias, m_w_attn_out=m_w_attn_out, m_w_conv_out=m_w_conv_out, m_w_o=m_w_o, m_ln_g=m_ln_g, m_ln_b=m_ln_b, v_w_ada=v_w_ada, v_b_ada=v_b_ada, v_w_in=v_w_in, v_conv_w=v_conv_w, v_conv_b=v_conv_b, v_rel_bias=v_rel_bias, v_w_attn_out=v_w_attn_out, v_w_conv_out=v_w_conv_out, v_w_o=v_w_o, v_ln_g=v_ln_g, v_ln_b=v_ln_b)
    weights = {n: given[n] for n in TWIN_WEIGHTS}
    shared = {n: given[n] for n in SHARED_INPUTS}
    per_example = {n: given[n] for n in ['x', 'c']}
    grad_fn = _jax.value_and_grad(_loss, argnums=(0, 1))

    def one_microbatch(ex, loss_target):
        ex = dict(ex)
        diff = ex.pop(TWIN_DIFF_INPUT)
        return grad_fn(weights, diff, {**shared, **ex}, loss_target)

    if N_MICROBATCH == 1:
        loss, (grad_w, grad_x) = one_microbatch(per_example, given["loss_target"])
    else:
        def body(carry, xs):
            loss_sum, grad_sum = carry
            l_k, (gw_k, gx_k) = one_microbatch(xs[0], xs[1])
            with _jax.named_scope("update"):
                return (loss_sum + l_k, _jax.tree.map(_jnp.add, grad_sum, gw_k)), gx_k

        init = (_jnp.zeros((), _jnp.float32), _jax.tree.map(_jnp.zeros_like, weights))
        (loss, grad_w), grad_x = _jax.lax.scan(body, init, (per_example, given["loss_target"]))
    with _jax.named_scope("update"):
        delta_w, new_m, new_v = {}, {}, {}
        for n in TWIN_WEIGHTS:
            delta_w[n], new_m[n], new_v[n] = _adamw(weights[n], grad_w[n], given["m_" + n], given["v_" + n])
    return (loss, grad_x, *[grad_w[n] for n in TWIN_WEIGHTS], *[delta_w[n] for n in TWIN_WEIGHTS],
            *[new_m[n] for n in TWIN_WEIGHTS], *[new_v[n] for n in TWIN_WEIGHTS])
```

```python
import math

import jax
import jax.numpy as jnp
from jax import lax
from jax.experimental import pallas as pl
from jax.experimental.pallas import tpu as pltpu

F32 = jnp.float32
BF16 = jnp.bfloat16
SDS = jax.ShapeDtypeStruct
MESH = pl.DeviceIdType.MESH
ANY = pl.BlockSpec(memory_space=pl.ANY)
VMEM_SPEC = pl.BlockSpec(memory_space=pltpu.VMEM)

D = 1024
HD = 128
BLK = 128
QW = 1536
AW = 512
NGATE = 6656
NCOL = 3 * QW + NGATE
TN = 512
NQT = QW // TN
NPT = NCOL // TN
DILATIONS = (1, 4, 16)
N_BUCKETS, MAX_EXACT, MAX_DISTANCE = 32, 16, 2048
ALPHA = 2.0 ** 0.25
LN_EPS = 1e-5
NEG = -1e30
SCALE = HD ** -0.5
LR, B1, B2, EPS, WD, STEP = 0.001, 0.9, 0.999, 1e-08, 0.01, 10
NCHIP = 4
VMEM_CAP = 60 * 2 ** 20


def _cp(sem=None, vmem=None, side=False):
    return pltpu.CompilerParams(dimension_semantics=sem, vmem_limit_bytes=vmem, has_side_effects=side)


def _dot(a, b):
    return jnp.dot(a, b, preferred_element_type=F32)


def _dot_nt(a, b):
    return lax.dot_general(a, b, (((1,), (1,)), ((), ())), preferred_element_type=F32)


def _dot_tn(a, b):
    return lax.dot_general(a, b, (((0,), (0,)), ((), ())), preferred_element_type=F32)


def _sig(x):
    return 1.0 / (1.0 + jnp.exp(-x))


def _place():
    x, y, c = lax.axis_index("x"), lax.axis_index("y"), lax.axis_index("c")
    return x, y, c


def _all_gather8(v, name):
    r, cdim = v.shape

    def body(v_ref, out_ref, send_sems, recv_sems, local_sem):
        x, y, c = _place()
        me = 4 * x + 2 * y + c
        peers = [(x, y, 1 - c), (1 - x, y, c), (x, 1 - y, c), (1 - x, 1 - y, c),
                 (1 - x, y, 1 - c), (x, 1 - y, 1 - c), (1 - x, 1 - y, 1 - c)]
        mine = pltpu.make_async_copy(v_ref, out_ref.at[me], local_sem)
        mine.start()

        def copy(k, block, to):
            return pltpu.make_async_remote_copy(src_ref=v_ref, dst_ref=out_ref.at[block], send_sem=send_sems.at[k],
                                                recv_sem=recv_sems.at[k], device_id=to, device_id_type=MESH)

        sends = [copy(k, me, p) for k, p in enumerate(peers)]
        for cp in sends:
            cp.start()
        for k, (px, py, pc) in enumerate(peers):
            copy(k, 4 * px + 2 * py + pc, (px, py, pc)).wait_recv()
        for cp in sends:
            cp.wait_send()
        mine.wait()

    return pl.pallas_call(
        body, name=name, out_shape=SDS((8, r, cdim), v.dtype), in_specs=[VMEM_SPEC], out_specs=VMEM_SPEC,
        scratch_shapes=[pltpu.SemaphoreType.DMA((7,)), pltpu.SemaphoreType.DMA((7,)), pltpu.SemaphoreType.DMA(())],
        compiler_params=_cp(side=True),
    )(v)


W_CUTS = (("col", D, NCOL // NCHIP), ("col", AW, D // NCHIP), ("row", D // NCHIP, D), ("row", D // NCHIP, D))
W_FULL = ((D, NCOL), (AW, D), (D, D), (D, D))


def _shard_window(ref, cut, k, half):
    kind, nr, nc = cut
    hr = nr // 2
    if kind == "col":
        rows = pl.ds(0, nr) if half is None else pl.ds(pl.multiple_of(half * hr, 16), hr)
        return ref.at[rows, pl.ds(pl.multiple_of(k * nc, 128), nc)]
    if half is None:
        return ref.at[pl.ds(pl.multiple_of(k * nr, 16), nr), :]
    return ref.at[pl.ds(pl.multiple_of(k * nr + half * hr, 16), hr), :]


def _half_rows(ref, cut, half):
    hr = cut[1] // 2
    return ref.at[pl.ds(pl.multiple_of(half * hr, 16), hr), :]


def _gather_weights(shards):
    n = len(shards)

    def body(*refs):
        src, full = refs[:n], refs[n:2 * n]
        send_sems, recv_sems, local_sems = refs[2 * n:]
        x, y, c = _place()
        me = 2 * x + y
        chips = [(1 - x, y), (x, 1 - y), (1 - x, 1 - y)]
        sibling = (x, y, 1 - c)
        locals_ = []
        for a in range(n):
            cp = pltpu.make_async_copy(src[a], _shard_window(full[a], W_CUTS[a], me, None), local_sems.at[a])
            cp.start()
            locals_.append(cp)

        def remote(a, k, src_ref, chip, half, to):
            return pltpu.make_async_remote_copy(src_ref=src_ref, dst_ref=_shard_window(full[a], W_CUTS[a], chip, half),
                                                send_sem=send_sems.at[k], recv_sem=recv_sems.at[k], device_id=to, device_id_type=MESH)

        sends = []
        for a in range(n):
            for j, (px, py) in enumerate(chips):
                cp = remote(a, 6 * a + j, _half_rows(src[a], W_CUTS[a], c), me, c, (px, py, c))
                cp.start()
                sends.append(cp)
        for a in range(n):
            for j, (px, py) in enumerate(chips):
                chip = 2 * px + py
                landed = _shard_window(full[a], W_CUTS[a], chip, c)
                remote(a, 6 * a + j, landed, chip, c, (px, py, c)).wait_recv()
                cp = remote(a, 6 * a + 3 + j, landed, chip, c, sibling)
                cp.start()
                sends.append(cp)
        for a in range(n):
            for j, (px, py) in enumerate(chips):
                chip = 2 * px + py
                remote(a, 6 * a + 3 + j, _shard_window(full[a], W_CUTS[a], chip, 1 - c), chip, 1 - c, sibling).wait_recv()
        for cp in sends:
            cp.wait_send()
        for cp in locals_:
            cp.wait()

    return pl.pallas_call(
        body, name="gather_weights", out_shape=[SDS(s, BF16) for s in W_FULL], in_specs=[ANY] * n, out_specs=[ANY] * n,
        scratch_shapes=[pltpu.SemaphoreType.DMA((6 * n,)), pltpu.SemaphoreType.DMA((6 * n,)), pltpu.SemaphoreType.DMA((n,))],
        compiler_params=_cp(side=True),
    )(*shards)


def _swap_halves(grads):
    n = len(grads)
    shapes = []
    for a in range(n):
        kind, nr, nc = W_CUTS[a]
        shapes.append((W_FULL[a][0] // 2, W_FULL[a][1]) if kind == "col" else (NCHIP, nr // 2, nc))

    def pieces(a, ref, land, half):
        kind, nr, nc = W_CUTS[a]
        if kind == "col":
            hr = nr // 2
            return [(ref.at[pl.ds(pl.multiple_of(half * hr, 16), hr), :], land)]
        return [(_shard_window(ref, W_CUTS[a], k, half), land.at[k]) for k in range(NCHIP)]

    def body(*refs):
        src, land = refs[:n], refs[n:2 * n]
        send_sems, recv_sems = refs[2 * n:]
        x, y, c = _place()
        sibling = (x, y, 1 - c)
        sends = []
        k = 0
        for a in range(n):
            for s, d in pieces(a, src[a], land[a], 1 - c):
                cp = pltpu.make_async_remote_copy(src_ref=s, dst_ref=d, send_sem=send_sems.at[k], recv_sem=recv_sems.at[k],
                                                  device_id=sibling, device_id_type=MESH)
                cp.start()
                sends.append(cp)
                k += 1
        for cp in sends:
            cp.wait()

    n_sems = sum(1 if W_CUTS[a][0] == "col" else NCHIP for a in range(n))
    return pl.pallas_call(
        body, name="swap_grad_halves", out_shape=[SDS(s, F32) for s in shapes], in_specs=[ANY] * n, out_specs=[ANY] * n,
        scratch_shapes=[pltpu.SemaphoreType.DMA((n_sems,)), pltpu.SemaphoreType.DMA((n_sems,))],
        compiler_params=_cp(side=True),
    )(*grads)


def _half_shapes(a):
    kind, nr, nc = W_CUTS[a]
    return (nr // 2, nc)


def _chip_sum(a, grad, got, name):
    kind, nr, nc = W_CUTS[a]
    hr = nr // 2
    c = lax.axis_index("c")
    cidx = jnp.reshape(c, (1,)).astype(jnp.int32)

    def body(c_ref, g_ref, r_ref, f_ref, b_ref):
        s = g_ref[...] + r_ref[...]
        f_ref[...] = s.reshape(f_ref.shape)
        b_ref[...] = s.astype(BF16).reshape(b_ref.shape)

    if kind == "col":
        in_specs = [pl.BlockSpec((hr, nc), lambda k, cr: (cr[0], k)), pl.BlockSpec((hr, nc), lambda k, cr: (0, k))]
    else:
        grad = grad.reshape(NCHIP, 2, hr, nc)
        in_specs = [pl.BlockSpec((1, 1, hr, nc), lambda k, cr: (k, cr[0], 0, 0)), pl.BlockSpec((1, hr, nc), lambda k, cr: (k, 0, 0))]
    out_specs = [pl.BlockSpec((1, hr, nc), lambda k, cr: (k, 0, 0))] * 2
    return pl.pallas_call(
        body, name=name, out_shape=[SDS((NCHIP, hr, nc), F32), SDS((NCHIP, hr, nc), BF16)],
        grid_spec=pltpu.PrefetchScalarGridSpec(num_scalar_prefetch=1, grid=(NCHIP,), in_specs=in_specs, out_specs=out_specs),
        compiler_params=_cp(("arbitrary",), VMEM_CAP),
    )(cidx, grad, got)


def _scatter_chip_sums(sums):
    n = len(sums)

    def body(*refs):
        src, land = refs[:n], refs[n:2 * n]
        send_sems, recv_sems = refs[2 * n:]
        x, y, c = _place()
        chips = [(1 - x, y), (x, 1 - y), (1 - x, 1 - y)]
        sends = []
        for a in range(n):
            for j, (px, py) in enumerate(chips):
                cp = pltpu.make_async_remote_copy(src_ref=src[a].at[2 * px + py], dst_ref=land[a].at[j], send_sem=send_sems.at[3 * a + j],
                                                  recv_sem=recv_sems.at[3 * a + j], device_id=(px, py, c), device_id_type=MESH)
                cp.start()
                sends.append(cp)
        for cp in sends:
            cp.wait()

    return pl.pallas_call(
        body, name="scatter_chip_sums", out_shape=[SDS((3,) + s.shape[1:], BF16) for s in sums], in_specs=[ANY] * n, out_specs=[ANY] * n,
        scratch_shapes=[pltpu.SemaphoreType.DMA((3 * n,)), pltpu.SemaphoreType.DMA((3 * n,))],
        compiler_params=_cp(side=True),
    )(*sums)


def _reduce_mine(a, mine_f32, got, name):
    kind, nr, nc = W_CUTS[a]
    hr = nr // 2
    x, y, _ = _place()
    me = jnp.reshape(2 * x + y, (1,)).astype(jnp.int32)
    tr = min(hr, 256)

    def body(me_ref, m_ref, g_ref, o_ref):
        o_ref[...] = ((m_ref[0] + g_ref[0].astype(F32)) + g_ref[1].astype(F32)) + g_ref[2].astype(F32)

    return pl.pallas_call(
        body, name=name, out_shape=SDS((hr, nc), F32),
        grid_spec=pltpu.PrefetchScalarGridSpec(
            num_scalar_prefetch=1, grid=(hr // tr,),
            in_specs=[pl.BlockSpec((1, tr, nc), lambda i, mr: (mr[0], i, 0)), pl.BlockSpec((3, tr, nc), lambda i, mr: (0, i, 0))],
            out_specs=pl.BlockSpec((tr, nc), lambda i, mr: (i, 0))),
        compiler_params=_cp(("arbitrary",), VMEM_CAP),
    )(me, mine_f32, got)


def _join_halves(halves):
    n = len(halves)

    def body(*refs):
        src, full = refs[:n], refs[n:2 * n]
        send_sems, recv_sems, local_sems = refs[2 * n:]
        x, y, c = _place()
        sibling = (x, y, 1 - c)
        cps = []
        for a in range(n):
            mine = pltpu.make_async_copy(src[a], _half_rows(full[a], W_CUTS[a], c), local_sems.at[a])
            mine.start()
            cp = pltpu.make_async_remote_copy(src_ref=src[a], dst_ref=_half_rows(full[a], W_CUTS[a], c), send_sem=send_sems.at[a],
                                              recv_sem=recv_sems.at[a], device_id=sibling, device_id_type=MESH)
            cp.start()
            cps.append((mine, cp))
        for a, (mine, cp) in enumerate(cps):
            cp.wait_send()
            pltpu.make_async_remote_copy(src_ref=src[a], dst_ref=_half_rows(full[a], W_CUTS[a], 1 - c), send_sem=send_sems.at[a],
                                         recv_sem=recv_sems.at[a], device_id=sibling, device_id_type=MESH).wait_recv()
            mine.wait()

    return pl.pallas_call(
        body, name="join_grad_halves", out_shape=[SDS((W_CUTS[a][1], W_CUTS[a][2]), F32) for a in range(n)],
        in_specs=[ANY] * n, out_specs=[ANY] * n,
        scratch_shapes=[pltpu.SemaphoreType.DMA((n,)), pltpu.SemaphoreType.DMA((n,)), pltpu.SemaphoreType.DMA((n,))],
        compiler_params=_cp(side=True),
    )(*halves)


def _to_bf16(w, name, tr=256):
    r, cdim = w.shape
    tr = min(tr, r)

    def body(w_ref, o_ref):
        o_ref[...] = w_ref[...].astype(BF16)

    return pl.pallas_call(
        body, name=name, out_shape=SDS((r, cdim), BF16), grid=(r // tr,),
        in_specs=[pl.BlockSpec((tr, cdim), lambda i: (i, 0))], out_specs=pl.BlockSpec((tr, cdim), lambda i: (i, 0)),
        compiler_params=_cp(("parallel",)),
    )(w)


def _ada_forward(c_all, w_ada, b_cols):
    nb, nc = c_all.shape[0], w_ada.shape[1]

    def body(c_ref, w_ref, b_ref, o_ref):
        cv = c_ref[...]
        sc = (cv * _sig(cv)).astype(BF16)
        o_ref[...] = _dot(sc, w_ref[...].astype(BF16)) + b_ref[...]

    return pl.pallas_call(body, name="ada_forward", out_shape=SDS((nb, nc), F32), compiler_params=_cp(vmem=VMEM_CAP // 2))(c_all, w_ada, b_cols)


def _ada_backward(c_all, dmod_cols, dmod_all):
    nb, nc = dmod_cols.shape

    def body(c_ref, d_ref, a_ref, gw_ref, gb_ref):
        cv = c_ref[...]
        sc = (cv * _sig(cv)).astype(BF16)
        gw_ref[...] = _dot_tn(sc, d_ref[...].astype(BF16))
        gb_ref[...] = jnp.sum(a_ref[...], axis=0, keepdims=True)

    return pl.pallas_call(body, name="ada_backward", out_shape=[SDS((D, nc), F32), SDS((1, dmod_all.shape[1]), F32)],
                          compiler_params=_cp(vmem=VMEM_CAP // 2))(c_all, dmod_cols, dmod_all)


def _modulate(x2, sc1p, shift, seq, tm=256):
    t = x2.shape[0]
    spt = seq // tm

    def body(x_ref, sc_ref, sh_ref, h_ref, ht_ref):
        h = x_ref[...] * sc_ref[0] + sh_ref[0]
        h_ref[...] = h.astype(BF16)
        ht_ref[...] = h.T.astype(BF16)

    per_seq = pl.BlockSpec((1, 1, D), lambda i: (i // spt, 0, 0))
    return pl.pallas_call(
        body, name="modulate", out_shape=[SDS((t, D), BF16), SDS((D, t), BF16)], grid=(t // tm,),
        in_specs=[pl.BlockSpec((tm, D), lambda i: (i, 0)), per_seq, per_seq],
        out_specs=[pl.BlockSpec((tm, D), lambda i: (i, 0)), pl.BlockSpec((D, tm), lambda i: (0, i))],
        compiler_params=_cp(("parallel",)),
    )(x2, sc1p, shift)


def _project(h, w, tm=1024):
    t = h.shape[0]

    def body(h_ref, w_ref, q_ref, k_ref, v_ref, g_ref):
        j = pl.program_id(1)
        acc = _dot(h_ref[...], w_ref[...])
        for n, ref in enumerate((q_ref, k_ref, v_ref)):
            @pl.when((j >= n * NQT) & (j < (n + 1) * NQT))
            def _(ref=ref):
                ref[...] = acc.astype(BF16)

        @pl.when(j >= 3 * NQT)
        def _():
            g_ref[...] = acc

    def part(n):
        return pl.BlockSpec((tm, TN), lambda i, j: (i, jnp.clip(j - n * NQT, 0, NQT - 1)))

    return pl.pallas_call(
        body, name="project", out_shape=[SDS((t, QW), BF16)] * 3 + [SDS((t, NGATE), F32)], grid=(t // tm, NPT),
        in_specs=[pl.BlockSpec((tm, D), lambda i, j: (i, 0)), pl.BlockSpec((D, TN), lambda i, j: (0, j))],
        out_specs=[part(0), part(1), part(2), pl.BlockSpec((tm, TN), lambda i, j: (i, jnp.maximum(j - 3 * NQT, 0)))],
        compiler_params=_cp(("arbitrary", "arbitrary"), VMEM_CAP // 2),
    )(h, w)


def _bias_tables(rel_bias, buckets):
    def body(tab_ref, bk_ref, o_ref):
        a = lax.broadcasted_iota(jnp.int32, (BLK, 2 * BLK), 0)
        b = lax.broadcasted_iota(jnp.int32, (BLK, 2 * BLK), 1)
        steps = a + BLK - b
        valid = (steps >= 0) & (steps <= BLK)
        for g in range(3):
            bk = bk_ref[g]
            for j in range(4):
                def pick(kk, acc, bk=bk, col=4 * g + j):
                    return jnp.where(bk == kk, tab_ref[kk, col], acc)

                acc = lax.fori_loop(0, N_BUCKETS, pick, jnp.zeros((BLK, 2 * BLK), F32))
                o_ref[g, j] = jnp.where(valid, acc, NEG)

    return pl.pallas_call(
        body, name="bias_tables", out_shape=SDS((3, 4, BLK, 2 * BLK), F32),
        in_specs=[pl.BlockSpec(memory_space=pltpu.SMEM), VMEM_SPEC], out_specs=VMEM_SPEC,
    )(rel_bias, buckets)


def _bias_grad(ds_sum, buckets):
    def body(ds_ref, bk_ref, o_ref):
        lane = lax.broadcasted_iota(jnp.int32, (1, 128), 1)
        for g in range(3):
            def bucket(kk, carry, g=g):
                row = jnp.zeros((1, 128), F32)
                for j in range(4):
                    v = jnp.where(bk_ref[g] == kk, ds_ref[g, j], 0.0)
                    s = jnp.sum(jnp.sum(v, axis=1, keepdims=True), axis=0, keepdims=True)
                    row = jnp.where(lane == j, s, row)
                o_ref[g, pl.ds(kk, 1), :] = row
                return carry

            lax.fori_loop(0, N_BUCKETS, bucket, 0)

    return pl.pallas_call(body, name="bias_grad", out_shape=SDS((3, N_BUCKETS, 128), F32), in_specs=[VMEM_SPEC, VMEM_SPEC],
                          out_specs=VMEM_SPEC)(ds_sum, buckets)


def _attn_views(g, seq, nh):
    d = DILATIONS[g]
    ln = seq // d
    w = HD * nh
    per_pos_q, per_pos_o = QW // w, AW // w
    qkv = pl.BlockSpec((1, ln, w), lambda b, r, hh: (b, 0, r * per_pos_q + g * (4 // nh) + hh))
    out = pl.BlockSpec((1, ln, w), lambda b, r, hh: (b, 0, r * per_pos_o + hh))
    return d, ln, qkv, out


def _attn_forward(g, q, k, v, bias, bsz, seq, nh):
    d, ln, qkv_spec, out_spec = _attn_views(g, seq, nh)
    nblk = ln // BLK
    view = lambda a: a.reshape(bsz, ln, d * a.shape[1])

    def body(q_ref, k_ref, v_ref, b_ref, o_ref, l_ref):
        def block(n, carry):
            r0 = pl.multiple_of(n * BLK, BLK)
            p0 = pl.multiple_of(jnp.maximum(n - 1, 0) * BLK, BLK)
            for hh in range(nh):
                cs = slice(hh * HD, (hh + 1) * HD)
                hs = pl.program_id(2) * nh + hh
                qb = q_ref[0, pl.ds(r0, BLK), cs]
                kc, vc = k_ref[0, pl.ds(r0, BLK), cs], v_ref[0, pl.ds(r0, BLK), cs]
                s_c = _dot_nt(qb, kc) * SCALE + b_ref[hs, :, BLK:]
                m = jnp.max(s_c, axis=1, keepdims=True)
                if nblk > 1:
                    kp, vp = k_ref[0, pl.ds(p0, BLK), cs], v_ref[0, pl.ds(p0, BLK), cs]
                    s_p = _dot_nt(qb, kp) * SCALE + b_ref[hs, :, :BLK]
                    s_p = jnp.where(n > 0, s_p, NEG)
                    m = jnp.maximum(m, jnp.max(s_p, axis=1, keepdims=True))
                p_c = jnp.exp(s_c - m)
                den = jnp.sum(p_c, axis=1, keepdims=True)
                acc = _dot(p_c.astype(BF16), vc)
                if nblk > 1:
                    p_p = jnp.exp(s_p - m)
                    den = den + jnp.sum(p_p, axis=1, keepdims=True)
                    acc = acc + _dot(p_p.astype(BF16), vp)
                o_ref[0, pl.ds(r0, BLK), cs] = acc / den
                l_ref[0, pl.ds(r0, BLK), cs] = jnp.broadcast_to(m + jnp.log(den), (BLK, HD))
            return carry

        lax.fori_loop(0, nblk, block, 0)

    o, lse = pl.pallas_call(
        body, name=f"attn_forward_{g}", out_shape=[SDS((bsz, ln, d * AW), F32)] * 2, grid=(bsz, d, 4 // nh),
        in_specs=[qkv_spec, qkv_spec, qkv_spec, pl.BlockSpec((4, BLK, 2 * BLK), lambda b, r, hh: (0, 0, 0))],
        out_specs=[out_spec, out_spec],
        compiler_params=_cp(("parallel", "parallel", "parallel"), VMEM_CAP),
    )(view(q), view(k), view(v), bias)
    return o.reshape(bsz * seq, AW), lse.reshape(bsz * seq, AW)


def _attn_backward(g, q, k, v, do, dl, bias, prev, bsz, seq, nh):
    d, ln, qkv_spec, out_spec = _attn_views(g, seq, nh)
    nblk = ln // BLK
    w = HD * nh
    view = lambda a: a.reshape(bsz, ln, d * a.shape[1])

    def body(q_ref, k_ref, v_ref, do_ref, dl_ref, b_ref, *rest):
        dq_ref, dk_ref, dv_ref, db_ref, dk_acc, dv_acc = rest[-6:]
        first = (pl.program_id(0) == 0) & (pl.program_id(1) == 0) & (pl.program_id(2) == 0)

        @pl.when(first)
        def _():
            db_ref[...] = jnp.zeros_like(db_ref)

        dk_acc[...] = jnp.zeros_like(dk_acc)
        dv_acc[...] = jnp.zeros_like(dv_acc)

        def block(n, carry):
            r0 = pl.multiple_of(n * BLK, BLK)
            p0 = pl.multiple_of(jnp.maximum(n - 1, 0) * BLK, BLK)
            for hh in range(nh):
                cs = slice(hh * HD, (hh + 1) * HD)
                hs = pl.program_id(2) * nh + hh
                qb = q_ref[0, pl.ds(r0, BLK), cs]
                dob = do_ref[0, pl.ds(r0, BLK), cs]
                both = dl_ref[0, pl.ds(r0, BLK), cs]
                lse, delta = both[:, 0:1], both[:, 64:65]
                dq = jnp.zeros((BLK, HD), F32)
                parts = [(r0, slice(BLK, 2 * BLK), None)]
                if nblk > 1:
                    parts.append((p0, slice(0, BLK), n > 0))
                for rows, band, live in parts:
                    kb, vb = k_ref[0, pl.ds(rows, BLK), cs], v_ref[0, pl.ds(rows, BLK), cs]
                    s = _dot_nt(qb, kb) * SCALE + b_ref[hs, :, band]
                    if live is not None:
                        s = jnp.where(live, s, NEG)
                    p = jnp.exp(s - lse)
                    dp = _dot_nt(dob, vb)
                    ds = p * (dp - delta)
                    dsb = ds.astype(BF16)
                    dv_acc[pl.ds(rows, BLK), cs] += _dot_tn(p.astype(BF16), dob)
                    dk_acc[pl.ds(rows, BLK), cs] += _dot_tn(dsb, qb) * SCALE
                    dq = dq + _dot(dsb, kb) * SCALE
                    db_ref[hs, :, band] += ds
                dq_ref[0, pl.ds(r0, BLK), cs] = dq.astype(BF16)
            return carry

        lax.fori_loop(0, nblk, block, 0)
        dk_ref[0] = dk_acc[...].astype(BF16)
        dv_ref[0] = dv_acc[...].astype(BF16)

    shape = SDS((bsz, ln, d * QW), BF16)
    ins = [view(q), view(k), view(v), view(do), view(dl), bias]
    in_specs = [qkv_spec, qkv_spec, qkv_spec, out_spec, out_spec, pl.BlockSpec((4, BLK, 2 * BLK), lambda b, r, hh: (0, 0, 0))]
    aliases = {}
    if prev is not None:
        ins += [view(p) for p in prev]
        in_specs += [ANY] * 3
        aliases = {6: 0, 7: 1, 8: 2}
    dq, dk, dv, db = pl.pallas_call(
        body, name=f"attn_backward_{g}", out_shape=[shape] * 3 + [SDS((4, BLK, 2 * BLK), F32)], grid=(bsz, d, 4 // nh),
        in_specs=in_specs, out_specs=[qkv_spec] * 3 + [pl.BlockSpec((4, BLK, 2 * BLK), lambda b, r, hh: (0, 0, 0))],
        scratch_shapes=[pltpu.VMEM((ln, w), F32), pltpu.VMEM((ln, w), F32)], input_output_aliases=aliases,
        compiler_params=_cp(("arbitrary", "arbitrary", "arbitrary"), VMEM_CAP),
    )(*ins)
    flat = lambda a: a.reshape(bsz * seq, QW)
    return (flat(dq), flat(dk), flat(dv)), db


def _mix_forward(gates, og, lg, x2, tgt, gate, w_ao, w_co, w_o, conv_w, conv_b, ln_g, ln_b, bsz, seq, tm=256):
    t = x2.shape[0]
    spt = seq // tm

    def body(g_ref, o1, o2, o3, l1, l2, l3, x_ref, t_ref, gate_ref, wao_ref, wco_ref, wo_ref, cw_ref, cb_ref, lng_ref, lnb_ref,
             ain_ref, sin_ref, mrg_ref, dy_ref, aout_ref, sout_ref, yc_ref, o_ref, lj_ref, dxr_ref, vec_ref, dgate_ref, zc_ref):
        b, i = pl.program_id(0), pl.program_id(1)

        @pl.when((b == 0) & (i == 0))
        def _():
            vec_ref[...] = jnp.zeros_like(vec_ref)

        @pl.when(i == 0)
        def _():
            zc_ref[...] = jnp.zeros_like(zc_ref)
            dgate_ref[...] = jnp.zeros_like(dgate_ref)

        g_attn, u, bg = g_ref[:, 0:512], g_ref[:, 512:1536], g_ref[:, 1536:2560]
        cg, g_conv = g_ref[:, 2560:3584], g_ref[:, 3584:4608]
        m_attn, m_conv = g_ref[:, 4608:5632], g_ref[:, 5632:6656]
        la, lb, lc = l1[...], l2[...], l3[...]
        mx = jnp.maximum(la, jnp.maximum(lb, lc))
        ea, eb, ec = jnp.exp(la - mx), jnp.exp(lb - mx), jnp.exp(lc - mx)
        den = ea + eb + ec
        o = (ea * o1[...] + eb * o2[...] + ec * o3[...]) / den
        o_ref[...] = o
        lj_ref[...] = mx + jnp.log(den)
        a_in = o * (g_attn * _sig(g_attn))
        ain_ref[...] = a_in.astype(BF16)
        a_out = _dot(a_in.astype(BF16), wao_ref[...])
        aout_ref[...] = a_out
        z = cg * u
        rows = lax.broadcasted_iota(jnp.int32, (tm, D), 0)
        c6, c7 = zc_ref[6:7, :], zc_ref[7:8, :]
        z1 = jnp.where(rows == 0, c7, pltpu.roll(z, 1, 0))
        z2 = jnp.where(rows == 0, c6, jnp.where(rows == 1, c7, pltpu.roll(z, 2, 0)))
        zc_ref[...] = z[tm - 8:tm, :]
        y_conv = (cw_ref[0:1, :] * z2 + cw_ref[1:2, :] * z1 + cw_ref[2:3, :] * z) + cb_ref[...]
        yc_ref[...] = y_conv
        s_in = bg * y_conv * (g_conv * _sig(g_conv))
        sin_ref[...] = s_in.astype(BF16)
        s_out = _dot(s_in.astype(BF16), wco_ref[...])
        sout_ref[...] = s_out
        merged = _sig(m_attn) * a_out + _sig(m_conv) * s_out
        mrg_ref[...] = merged.astype(BF16)
        y = _dot(merged.astype(BF16), wo_ref[...])
        gate1 = 1.0 + gate_ref[0]
        r = ALPHA * x_ref[...] + gate1 * y
        mu = jnp.mean(r, axis=1, keepdims=True)
        rc = r - mu
        rstd = lax.rsqrt(jnp.mean(rc * rc, axis=1, keepdims=True) + LN_EPS)
        xhat = rc * rstd
        diff = (xhat * lng_ref[...] + lnb_ref[...]) - t_ref[...]
        dout = diff * (1.0 / D)
        vec_ref[0:1, :] += jnp.sum(dout * xhat, axis=0, keepdims=True)
        vec_ref[1:2, :] += jnp.sum(dout, axis=0, keepdims=True)
        vec_ref[2:3, :] += jnp.sum(diff * diff, axis=0, keepdims=True)
        dxh = dout * lng_ref[...]
        dr = rstd * (dxh - jnp.mean(dxh, axis=1, keepdims=True) - xhat * jnp.mean(dxh * xhat, axis=1, keepdims=True))
        dxr_ref[...] = ALPHA * dr
        dy_ref[...] = (dr * gate1).astype(BF16)
        dgate_ref[0] += jnp.sum(dr * y, axis=0, keepdims=True)

    tok = lambda w: pl.BlockSpec((tm, w), lambda b, i: (b * spt + i, 0))
    const = lambda s: pl.BlockSpec(s, lambda b, i: (0,) * len(s))
    per_seq = pl.BlockSpec((1, 1, D), lambda b, i: (b, 0, 0))
    outs = pl.pallas_call(
        body, name="mix_forward", grid=(bsz, spt),
        out_shape=[SDS((t, AW), BF16), SDS((t, D), BF16), SDS((t, D), BF16), SDS((t, D), BF16), SDS((t, D), F32), SDS((t, D), F32),
                   SDS((t, D), F32), SDS((t, AW), F32), SDS((t, AW), F32), SDS((t, D), F32), SDS((8, D), F32), SDS((bsz, 1, D), F32)],
        in_specs=[tok(NGATE)] + [tok(AW)] * 6 + [tok(D), tok(D), per_seq, const((AW, D)), const((D, D)), const((D, D)),
                                                 const((3, D)), const((1, D)), const((1, D)), const((1, D))],
        out_specs=[tok(AW), tok(D), tok(D), tok(D), tok(D), tok(D), tok(D), tok(AW), tok(AW), tok(D), const((8, D)), per_seq],
        scratch_shapes=[pltpu.VMEM((8, D), F32)],
        compiler_params=_cp(("arbitrary", "arbitrary"), VMEM_CAP),
    )(gates, *og, *lg, x2, tgt, gate, w_ao, w_co, w_o, conv_w, conv_b, ln_g, ln_b)
    return outs


def _mix_backward(gates, dy, a_out, s_out, y_conv, o, lj, w_ao, w_co, w_o, conv_w, bsz, seq, tm=256):
    t = dy.shape[0]
    spt = seq // tm

    def body(g_ref, dy_ref, aout_ref, sout_ref, yc_ref, o_ref, lj_ref, wao_ref, wco_ref, wo_ref, cw_ref,
             dg_ref, do_ref, dl_ref, daout_ref, dsout_ref, vec_ref, car_ref):
        b, i = pl.program_id(0), pl.program_id(1)

        @pl.when((b == 0) & (i == 0))
        def _():
            vec_ref[...] = jnp.zeros_like(vec_ref)

        @pl.when(i == 0)
        def _():
            car_ref[...] = jnp.zeros_like(car_ref)

        g_attn, u, bg = g_ref[:, 0:512], g_ref[:, 512:1536], g_ref[:, 1536:2560]
        cg, g_conv = g_ref[:, 2560:3584], g_ref[:, 3584:4608]
        m_attn, m_conv = g_ref[:, 4608:5632], g_ref[:, 5632:6656]
        dmerged = _dot_nt(dy_ref[...], wo_ref[...])
        sa, sc = _sig(m_attn), _sig(m_conv)
        da_out = (dmerged * sa).astype(BF16)
        ds_out = (dmerged * sc).astype(BF16)
        daout_ref[...] = da_out
        dsout_ref[...] = ds_out
        dg_ref[:, 4608:5632] = (dmerged * aout_ref[...] * (sa * (1.0 - sa))).astype(BF16)
        dg_ref[:, 5632:6656] = (dmerged * sout_ref[...] * (sc * (1.0 - sc))).astype(BF16)
        da_in = _dot_nt(da_out, wao_ref[...])
        ds_in = _dot_nt(ds_out, wco_ref[...])
        sga = _sig(g_attn)
        o = o_ref[...]
        do = da_in * (g_attn * sga)
        do_ref[...] = do.astype(BF16)
        dg_ref[:, 0:512] = (da_in * o * (sga * (1.0 + g_attn * (1.0 - sga)))).astype(BF16)
        prod = do * o
        lane = lax.broadcasted_iota(jnp.int32, (tm, HD), 1)
        for j in range(4):
            cs = slice(j * HD, (j + 1) * HD)
            delta = jnp.sum(prod[:, cs], axis=1, keepdims=True)
            dl_ref[:, cs] = jnp.where(lane < 64, lj_ref[:, cs], delta)
        sgc = _sig(g_conv)
        silu_c = g_conv * sgc
        yc = yc_ref[...]
        dg_ref[:, 1536:2560] = (ds_in * yc * silu_c).astype(BF16)
        dg_ref[:, 3584:4608] = (ds_in * bg * yc * (sgc * (1.0 + g_conv * (1.0 - sgc)))).astype(BF16)
        dyc = ds_in * bg * silu_c
        rows = lax.broadcasted_iota(jnp.int32, (tm, D), 0)
        c0, c1 = car_ref[0:1, :], car_ref[1:2, :]
        n1 = jnp.where(rows == tm - 1, c0, pltpu.roll(dyc, tm - 1, 0))
        n2 = jnp.where(rows == tm - 2, c0, jnp.where(rows == tm - 1, c1, pltpu.roll(dyc, tm - 2, 0)))
        car_ref[...] = dyc[0:8, :]
        dz = cw_ref[2:3, :] * dyc + cw_ref[1:2, :] * n1 + cw_ref[0:1, :] * n2
        z = cg * u
        dg_ref[:, 512:1536] = (dz * cg).astype(BF16)
        dg_ref[:, 2560:3584] = (dz * u).astype(BF16)
        vec_ref[0:1, :] += jnp.sum(n2 * z, axis=0, keepdims=True)
        vec_ref[1:2, :] += jnp.sum(n1 * z, axis=0, keepdims=True)
        vec_ref[2:3, :] += jnp.sum(dyc * z, axis=0, keepdims=True)
        vec_ref[3:4, :] += jnp.sum(dyc, axis=0, keepdims=True)

    tok = lambda w: pl.BlockSpec((tm, w), lambda b, i: (b * spt + (spt - 1 - i), 0))
    const = lambda s: pl.BlockSpec(s, lambda b, i: (0,) * len(s))
    return pl.pallas_call(
        body, name="mix_backward", grid=(bsz, spt),
        out_shape=[SDS((t, NGATE), BF16), SDS((t, AW), BF16), SDS((t, AW), F32), SDS((t, D), BF16), SDS((t, D), BF16), SDS((8, D), F32)],
        in_specs=[tok(NGATE), tok(D), tok(D), tok(D), tok(D), tok(AW), tok(AW), const((AW, D)), const((D, D)), const((D, D)), const((3, D))],
        out_specs=[tok(NGATE), tok(AW), tok(AW), tok(D), tok(D), const((8, D))],
        scratch_shapes=[pltpu.VMEM((8, D), F32)],
        compiler_params=_cp(("arbitrary", "arbitrary"), VMEM_CAP),
    )(gates, dy, a_out, s_out, y_conv, o, lj, w_ao, w_co, w_o, conv_w)


def _out_weight_grads(a_in, da_out, s_in, ds_out, merged, dy, tk=512):
    t = dy.shape[0]

    def body(ain_ref, da_ref, sin_ref, ds_ref, m_ref, dy_ref, gao_ref, gco_ref, go_ref):
        @pl.when(pl.program_id(0) == 0)
        def _():
            gao_ref[...] = jnp.zeros_like(gao_ref)
            gco_ref[...] = jnp.zeros_like(gco_ref)
            go_ref[...] = jnp.zeros_like(go_ref)

        gao_ref[...] += _dot_tn(ain_ref[...], da_ref[...])
        gco_ref[...] += _dot_tn(sin_ref[...], ds_ref[...])
        go_ref[...] += _dot_tn(m_ref[...], dy_ref[...])

    tok = lambda w: pl.BlockSpec((tk, w), lambda i: (i, 0))
    const = lambda s: pl.BlockSpec(s, lambda i: (0, 0))
    return pl.pallas_call(
        body, name="out_weight_grads", grid=(t // tk,), out_shape=[SDS((AW, D), F32), SDS((D, D), F32), SDS((D, D), F32)],
        in_specs=[tok(AW), tok(D), tok(D), tok(D), tok(D), tok(D)], out_specs=[const((AW, D)), const((D, D)), const((D, D))],
        compiler_params=_cp(("arbitrary",), VMEM_CAP),
    )(a_in, da_out, s_in, ds_out, merged, dy)


def _input_grad(dq, dk, dv, dgates, w, x2, dxr, sc1p, seq, tm=1024):
    t = x2.shape[0]
    spt = seq // tm
    bsz = t // seq

    def body(dq_ref, dk_ref, dv_ref, dg_ref, w_ref, x_ref, dxr_ref, sc_ref, dx_ref, dsh_ref, dsc_ref, acc_ref):
        i, j = pl.program_id(0), pl.program_id(1)

        @pl.when(j == 0)
        def _():
            acc_ref[...] = jnp.zeros_like(acc_ref)

        for n, ref in enumerate((dq_ref, dk_ref, dv_ref)):
            @pl.when((j >= n * NQT) & (j < (n + 1) * NQT))
            def _(ref=ref):
                acc_ref[...] += _dot_nt(ref[...], w_ref[...])

        @pl.when(j >= 3 * NQT)
        def _():
            acc_ref[...] += _dot_nt(dg_ref[...], w_ref[...])

        @pl.when(j == NPT - 1)
        def _():
            dh = acc_ref[...]
            dx_ref[...] = dh * sc_ref[0] + dxr_ref[...]

            @pl.when(i % spt == 0)
            def _():
                dsh_ref[...] = jnp.zeros_like(dsh_ref)
                dsc_ref[...] = jnp.zeros_like(dsc_ref)

            dsh_ref[0] += jnp.sum(dh, axis=0, keepdims=True)
            dsc_ref[0] += jnp.sum(dh * x_ref[...], axis=0, keepdims=True)

    def part(n):
        return pl.BlockSpec((tm, TN), lambda i, j: (i, jnp.clip(j - n * NQT, 0, NQT - 1)))

    row = pl.BlockSpec((tm, D), lambda i, j: (i, 0))
    per_seq = pl.BlockSpec((1, 1, D), lambda i, j: (i // spt, 0, 0))
    return pl.pallas_call(
        body, name="input_grad", grid=(t // tm, NPT), out_shape=[SDS((t, D), F32), SDS((bsz, 1, D), F32), SDS((bsz, 1, D), F32)],
        in_specs=[part(0), part(1), part(2), pl.BlockSpec((tm, TN), lambda i, j: (i, jnp.maximum(j - 3 * NQT, 0))),
                  pl.BlockSpec((D, TN), lambda i, j: (0, j)), row, row, per_seq],
        out_specs=[row, per_seq, per_seq], scratch_shapes=[pltpu.VMEM((tm, D), F32)],
        compiler_params=_cp(("arbitrary", "arbitrary"), VMEM_CAP),
    )(dq, dk, dv, dgates, w, x2, dxr, sc1p)


def _in_weight_grad(ht, src, col0, prev, name, tk=1024):
    t = ht.shape[1]
    ncols = src.shape[1] // TN

    def body(ht_ref, s_ref, *rest):
        o_ref = rest[-1]

        @pl.when(pl.program_id(1) == 0)
        def _():
            o_ref[...] = jnp.zeros_like(o_ref)

        o_ref[...] += _dot(ht_ref[...], s_ref[...])

    ins = [ht, src]
    in_specs = [pl.BlockSpec((D, tk), lambda j, i: (0, i)), pl.BlockSpec((tk, TN), lambda j, i: (i, j))]
    aliases = {}
    if prev is not None:
        ins.append(prev)
        in_specs.append(ANY)
        aliases = {2: 0}
    return pl.pallas_call(
        body, name=name, grid=(ncols, t // tk), out_shape=SDS((D, NCOL), F32), in_specs=in_specs,
        out_specs=pl.BlockSpec((D, TN), lambda j, i: (0, col0 + j)), input_output_aliases=aliases,
        compiler_params=_cp(("arbitrary", "arbitrary"), VMEM_CAP // 2),
    )(*ins)


def _sum_partials(gathered):
    def body(g_ref, o_ref):
        acc = g_ref[0]
        for k in range(1, 8):
            acc = acc + g_ref[k]
        o_ref[...] = acc

    return pl.pallas_call(body, name="sum_partials", out_shape=SDS(gathered.shape[1:], F32), in_specs=[VMEM_SPEC], out_specs=VMEM_SPEC)(gathered)


def _adamw(w, g, m, v, name, tr=256):
    r, cdim = w.shape
    tr = tr if cdim <= D else tr // 2
    tr = tr if (r % tr == 0 and r > tr) else r

    def body(w_ref, g_ref, m_ref, v_ref, d_ref, nm_ref, nv_ref):
        gv = g_ref[...]
        nm = B1 * m_ref[...] + (1.0 - B1) * gv
        nv = B2 * v_ref[...] + (1.0 - B2) * (gv * gv)
        m_hat = nm / (1.0 - B1 ** STEP)
        v_hat = nv / (1.0 - B2 ** STEP)
        d_ref[...] = -LR * (m_hat / (jnp.sqrt(v_hat) + EPS) + WD * w_ref[...])
        nm_ref[...] = nm
        nv_ref[...] = nv

    spec = pl.BlockSpec((tr, cdim), lambda i: (i, 0))
    return pl.pallas_call(
        body, name=name, grid=(r // tr,), out_shape=[SDS((r, cdim), F32)] * 3, in_specs=[spec] * 4, out_specs=[spec] * 3,
        compiler_params=_cp(("parallel",), VMEM_CAP // 2),
    )(w, g, m, v)


def _t5_bucket(dist):
    n = jnp.maximum(dist, 1).astype(F32)
    large = MAX_EXACT + (jnp.log(n / MAX_EXACT) / math.log(MAX_DISTANCE / MAX_EXACT) * (N_BUCKETS - MAX_EXACT)).astype(jnp.int32)
    large = jnp.minimum(large, N_BUCKETS - 1)
    return jnp.where(dist < MAX_EXACT, dist, large)


def _band_buckets():
    a = jnp.arange(BLK)[:, None]
    b = jnp.arange(2 * BLK)[None, :]
    steps = jnp.maximum(a + BLK - b, 0)
    return jnp.stack([_t5_bucket(steps * d) for d in DILATIONS]).astype(jnp.int32)


def _pad_rows(a, rows=8):
    return jnp.pad(a, ((0, rows - a.shape[0]), (0, 0)))


def kernel(x, c, w_ada, b_ada, w_in, conv_w, conv_b, rel_bias, w_attn_out, w_conv_out, w_o, ln_g, ln_b, loss_target, m_w_ada, m_b_ada, m_w_in, m_conv_w, m_conv_b, m_rel_bias, m_w_attn_out, m_w_conv_out, m_w_o, m_ln_g, m_ln_b, v_w_ada, v_b_ada, v_w_in, v_conv_w, v_conv_b, v_rel_bias, v_w_attn_out, v_w_conv_out, v_w_o, v_ln_g, v_ln_b):
    bsz, seq, _ = x.shape
    t = bsz * seq
    mx, my, mc = _place()
    chip = 2 * mx + my
    dev = 4 * mx + 2 * my + mc
    x2 = x.reshape(t, D)
    tgt = loss_target.reshape(t, D)

    shards = [_to_bf16(w_in[0], "w_in_bf16"), _to_bf16(w_attn_out[0], "w_attn_out_bf16"),
              _to_bf16(w_conv_out[0], "w_conv_out_bf16"), _to_bf16(w_o[0], "w_o_bf16")]
    w_in_f, w_ao_f, w_co_f, w_o_f = _gather_weights(shards)

    n_ada = w_ada.shape[2]
    c_all = _all_gather8(c, "gather_c").reshape(8 * bsz, D)
    b_cols = lax.dynamic_slice(b_ada, (0, chip * n_ada), (1, n_ada))
    mod_part = _ada_forward(c_all, w_ada[0], b_cols)
    mod_parts = _all_gather8(mod_part, "gather_mod")
    mod_all = mod_parts[0::2].transpose(1, 0, 2).reshape(8 * bsz, 3 * D)
    mod = lax.dynamic_slice(mod_all, (dev * bsz, 0), (bsz, 3 * D))
    shift = mod[:, 0:D].reshape(bsz, 1, D)
    sc1p = 1.0 + mod[:, D:2 * D].reshape(bsz, 1, D)
    gate = mod[:, 2 * D:].reshape(bsz, 1, D)
    cw_parts = _all_gather8(_pad_rows(conv_w[0]), "gather_conv_w")
    conv_w_f = cw_parts[0::2].transpose(1, 0, 2).reshape(8, D)[0:3]

    h, ht = _modulate(x2, sc1p, shift, seq)
    q, k, v, gates = _project(h, w_in_f)
    buckets = _band_buckets()
    bias = _bias_tables(rel_bias, buckets)
    heads = (1, 4, 4)
    og, lg = [], []
    for g in range(3):
        o_g, l_g = _attn_forward(g, q, k, v, bias[g], bsz, seq, heads[g])
        og.append(o_g)
        lg.append(l_g)
    (a_in, s_in, merged, dy, a_out, s_out, y_conv, o, lj, dxr, vec_f, dgate) = _mix_forward(
        gates, og, lg, x2, tgt, gate, w_ao_f, w_co_f, w_o_f, conv_w_f, conv_b, ln_g, ln_b, bsz, seq)

    dgates, do, dl, da_out, ds_out, vec_b = _mix_backward(gates, dy, a_out, s_out, y_conv, o, lj, w_ao_f, w_co_f, w_o_f, conv_w_f, bsz, seq)
    g_ao, g_co, g_o = _out_weight_grads(a_in, da_out, s_in, ds_out, merged, dy)
    dqkv, dbs = None, []
    for g in range(3):
        dqkv, db = _attn_backward(g, q, k, v, do, dl, bias[g], dqkv, bsz, seq, heads[g])
        dbs.append(db)
    dq, dk, dv = dqkv
    drb = _bias_grad(jnp.stack(dbs), buckets)
    drb = drb[:, :, 0:4].transpose(1, 0, 2).reshape(N_BUCKETS, 12)
    grad_x, dshift, dscale = _input_grad(dq, dk, dv, dgates, w_in_f, x2, dxr, sc1p, seq)
    g_in = None
    for n, src in enumerate((dq, dk, dv, dgates)):
        g_in = _in_weight_grad(ht, src, n * NQT, g_in, f"in_weight_grad_{n}")

    grads = [g_in, g_ao, g_co, g_o]
    got = _swap_halves(grads)
    sums = [_chip_sum(a, grads[a], got[a], f"chip_sum_{a}") for a in range(4)]
    landed = _scatter_chip_sums([s[1] for s in sums])
    halves = [_reduce_mine(a, sums[a][0], landed[a], f"reduce_mine_{a}") for a in range(4)]
    gw_in, gw_ao, gw_co, gw_o = _join_halves(halves)

    dmod = jnp.concatenate([dshift, dscale, dgate], axis=2).reshape(bsz * 3, D)
    drb_row = jnp.pad(drb.reshape(1, N_BUCKETS * 12), ((0, 0), (0, D - N_BUCKETS * 12)))
    packed = jnp.concatenate([vec_f, vec_b, _pad_rows(dmod), _pad_rows(drb_row)], axis=0)
    gathered = _all_gather8(packed, "gather_small")
    small = _sum_partials(gathered)
    g_ln_g, g_ln_b, loss_lanes = small[0:1], small[1:2], small[2:3]
    g_conv_w_full, g_conv_b = small[8:11], small[11:12]
    g_rel_bias = small[24, 0:N_BUCKETS * 12].reshape(N_BUCKETS, 12)
    loss = 0.5 / D * jnp.sum(loss_lanes)
    dmod_all = gathered[:, 16:16 + 3 * bsz, :].reshape(8 * bsz, 3 * D)
    dmod_cols = lax.dynamic_slice(dmod_all, (0, chip * n_ada), (8 * bsz, n_ada))
    gw_ada, gb_ada = _ada_backward(c_all, dmod_cols, dmod_all)
    n_cw = conv_w.shape[2]
    g_conv_w = lax.dynamic_slice(g_conv_w_full, (0, chip * n_cw), (3, n_cw))

    names = ["w_ada", "b_ada", "w_in", "conv_w", "conv_b", "rel_bias", "w_attn_out", "w_conv_out", "w_o", "ln_g", "ln_b"]
    two_d = lambda a: a.reshape(a.shape[-2:]) if a.ndim == 3 else a
    weights = dict(zip(names, map(two_d, (w_ada, b_ada, w_in, conv_w, conv_b, rel_bias, w_attn_out, w_conv_out, w_o, ln_g, ln_b))))
    ms = dict(zip(names, map(two_d, (m_w_ada, m_b_ada, m_w_in, m_conv_w, m_conv_b, m_rel_bias, m_w_attn_out, m_w_conv_out, m_w_o, m_ln_g, m_ln_b))))
    vs = dict(zip(names, map(two_d, (v_w_ada, v_b_ada, v_w_in, v_conv_w, v_conv_b, v_rel_bias, v_w_attn_out, v_w_conv_out, v_w_o, v_ln_g, v_ln_b))))
    grads = dict(zip(names, (gw_ada, gb_ada, gw_in, g_conv_w, g_conv_b, g_rel_bias, gw_ao, gw_co, gw_o, g_ln_g, g_ln_b)))
    shapes = dict(zip(names, (w_ada, b_ada, w_in, conv_w, conv_b, rel_bias, w_attn_out, w_conv_out, w_o, ln_g, ln_b)))
    deltas, new_m, new_v = {}, {}, {}
    for n in names:
        deltas[n], new_m[n], new_v[n] = _adamw(weights[n], grads[n], ms[n], vs[n], f"adamw_{n}")
    shaped = lambda d: [d[n].reshape(shapes[n].shape) for n in names]
    return (loss, grad_x.reshape(bsz, seq, D), *shaped(grads), *shaped(deltas), *shaped(new_m), *shaped(new_v))
```

```python
import math

import jax
import jax.numpy as jnp
from jax import lax
from jax.experimental import pallas as pl
from jax.experimental.pallas import tpu as pltpu

F32 = jnp.float32
BF16 = jnp.bfloat16
SDS = jax.ShapeDtypeStruct
MESH = pl.DeviceIdType.MESH
ANY = pl.BlockSpec(memory_space=pl.ANY)
VMEM_SPEC = pl.BlockSpec(memory_space=pltpu.VMEM)

D = 1024
HD = 128
BLK = 128
QW = 1536
AW = 512
NGATE = 6656
NCOL = 3 * QW + NGATE
TN = 512
NQT = QW // TN
NPT = NCOL // TN
DILATIONS = (1, 4, 16)
N_BUCKETS, MAX_EXACT, MAX_DISTANCE = 32, 16, 2048
ALPHA = 2.0 ** 0.25
LN_EPS = 1e-5
NEG = -1e30
SCALE = HD ** -0.5
LR, B1, B2, EPS, WD, STEP = 0.001, 0.9, 0.999, 1e-08, 0.01, 10
NCHIP = 4
VMEM_CAP = 60 * 2 ** 20


def _cp(sem=None, vmem=None, side=False):
    return pltpu.CompilerParams(dimension_semantics=sem, vmem_limit_bytes=vmem, has_side_effects=side)


def _dot(a, b):
    return jnp.dot(a, b, preferred_element_type=F32)


def _dot_nt(a, b):
    return lax.dot_general(a, b, (((1,), (1,)), ((), ())), preferred_element_type=F32)


def _dot_tn(a, b):
    return lax.dot_general(a, b, (((0,), (0,)), ((), ())), preferred_element_type=F32)


def _sig(x):
    return 1.0 / (1.0 + jnp.exp(-x))


def _place():
    x, y, c = lax.axis_index("x"), lax.axis_index("y"), lax.axis_index("c")
    return x, y, c


def _all_gather8(v, name):
    r, cdim = v.shape

    def body(v_ref, out_ref, send_sems, recv_sems, local_sem):
        x, y, c = _place()
        me = 4 * x + 2 * y + c
        peers = [(x, y, 1 - c), (1 - x, y, c), (x, 1 - y, c), (1 - x, 1 - y, c),
                 (1 - x, y, 1 - c), (x, 1 - y, 1 - c), (1 - x, 1 - y, 1 - c)]
        mine = pltpu.make_async_copy(v_ref, out_ref.at[me], local_sem)
        mine.start()

        def copy(k, block, to):
            return pltpu.make_async_remote_copy(src_ref=v_ref, dst_ref=out_ref.at[block], send_sem=send_sems.at[k],
                                                recv_sem=recv_sems.at[k], device_id=to, device_id_type=MESH)

        sends = [copy(k, me, p) for k, p in enumerate(peers)]
        for cp in sends:
            cp.start()
        for k, (px, py, pc) in enumerate(peers):
            copy(k, 4 * px + 2 * py + pc, (px, py, pc)).wait_recv()
        for cp in sends:
            cp.wait_send()
        mine.wait()

    return pl.pallas_call(
        body, name=name, out_shape=SDS((8, r, cdim), v.dtype), in_specs=[VMEM_SPEC], out_specs=VMEM_SPEC,
        scratch_shapes=[pltpu.SemaphoreType.DMA((7,)), pltpu.SemaphoreType.DMA((7,)), pltpu.SemaphoreType.DMA(())],
        compiler_params=_cp(side=True),
    )(v)


W_CUTS = (("col", D, NCOL // NCHIP), ("col", AW, D // NCHIP), ("row", D // NCHIP, D), ("row", D // NCHIP, D))
W_FULL = ((D, NCOL), (AW, D), (D, D), (D, D))


def _shard_window(ref, cut, k, half):
    kind, nr, nc = cut
    hr = nr // 2
    if kind == "col":
        rows = pl.ds(0, nr) if half is None else pl.ds(pl.multiple_of(half * hr, 16), hr)
        return ref.at[rows, pl.ds(pl.multiple_of(k * nc, 128), nc)]
    if half is None:
        return ref.at[pl.ds(pl.multiple_of(k * nr, 16), nr), :]
    return ref.at[pl.ds(pl.multiple_of(k * nr + half * hr, 16), hr), :]


def _half_rows(ref, cut, half):
    hr = cut[1] // 2
    return ref.at[pl.ds(pl.multiple_of(half * hr, 16), hr), :]


def _to_bf16_window(a, w, name):
    kind, nr, nc = W_CUTS[a]
    x, y, _ = _place()
    chip = jnp.reshape(2 * x + y, (1,)).astype(jnp.int32)
    tr = min(nr, 256)

    def body(c_ref, w_ref, o_ref):
        o_ref[...] = w_ref[...].astype(BF16)

    out_map = (lambda i, cr: (i, cr[0])) if kind == "col" else (lambda i, cr: (cr[0] * (nr // tr) + i, 0))
    return pl.pallas_call(
        body, name=name, out_shape=SDS(W_FULL[a], BF16),
        grid_spec=pltpu.PrefetchScalarGridSpec(num_scalar_prefetch=1, grid=(nr // tr,),
                                               in_specs=[pl.BlockSpec((tr, nc), lambda i, cr: (i, 0))], out_specs=pl.BlockSpec((tr, nc), out_map)),
        compiler_params=_cp(("arbitrary",)),
    )(chip, w)


def _gather_weights(fulls):
    n = len(fulls)

    def body(*refs):
        full = refs[n:2 * n]
        send_sems, recv_sems = refs[2 * n:]
        x, y, c = _place()
        me = 2 * x + y
        chips = [(1 - x, y), (x, 1 - y), (1 - x, 1 - y)]
        sibling = (x, y, 1 - c)

        def remote(a, k, chip, half, to):
            window = _shard_window(full[a], W_CUTS[a], chip, half)
            return pltpu.make_async_remote_copy(src_ref=window, dst_ref=window, send_sem=send_sems.at[k], recv_sem=recv_sems.at[k],
                                                device_id=to, device_id_type=MESH)

        sends = []
        for a in range(n):
            for j, (px, py) in enumerate(chips):
                cp = remote(a, 6 * a + j, me, c, (px, py, c))
                cp.start()
                sends.append(cp)
        for a in range(n):
            for j, (px, py) in enumerate(chips):
                chip = 2 * px + py
                remote(a, 6 * a + j, chip, c, (px, py, c)).wait_recv()
                cp = remote(a, 6 * a + 3 + j, chip, c, sibling)
                cp.start()
                sends.append(cp)
        for a in range(n):
            for j, (px, py) in enumerate(chips):
                remote(a, 6 * a + 3 + j, 2 * px + py, 1 - c, sibling).wait_recv()
        for cp in sends:
            cp.wait_send()

    return pl.pallas_call(
        body, name="gather_weights", out_shape=[SDS(s, BF16) for s in W_FULL], in_specs=[ANY] * n, out_specs=[ANY] * n,
        scratch_shapes=[pltpu.SemaphoreType.DMA((6 * n,)), pltpu.SemaphoreType.DMA((6 * n,))],
        input_output_aliases={a: a for a in range(n)}, compiler_params=_cp(side=True),
    )(*fulls)


def _swap_halves(grads):
    n = len(grads)
    shapes = []
    for a in range(n):
        kind, nr, nc = W_CUTS[a]
        shapes.append((W_FULL[a][0] // 2, W_FULL[a][1]) if kind == "col" else (NCHIP, nr // 2, nc))

    def pieces(a, ref, land, half):
        kind, nr, nc = W_CUTS[a]
        if kind == "col":
            hr = nr // 2
            return [(ref.at[pl.ds(pl.multiple_of(half * hr, 16), hr), :], land)]
        return [(_shard_window(ref, W_CUTS[a], k, half), land.at[k]) for k in range(NCHIP)]

    def body(*refs):
        src, land = refs[:n], refs[n:2 * n]
        send_sems, recv_sems = refs[2 * n:]
        x, y, c = _place()
        sibling = (x, y, 1 - c)
        sends = []
        k = 0
        for a in range(n):
            for s, d in pieces(a, src[a], land[a], 1 - c):
                cp = pltpu.make_async_remote_copy(src_ref=s, dst_ref=d, send_sem=send_sems.at[k], recv_sem=recv_sems.at[k],
                                                  device_id=sibling, device_id_type=MESH)
                cp.start()
                sends.append(cp)
                k += 1
        for cp in sends:
            cp.wait()

    n_sems = sum(1 if W_CUTS[a][0] == "col" else NCHIP for a in range(n))
    return pl.pallas_call(
        body, name="swap_grad_halves", out_shape=[SDS(s, F32) for s in shapes], in_specs=[ANY] * n, out_specs=[ANY] * n,
        scratch_shapes=[pltpu.SemaphoreType.DMA((n_sems,)), pltpu.SemaphoreType.DMA((n_sems,))],
        compiler_params=_cp(side=True),
    )(*grads)


def _chip_sum(a, grad, got, name):
    kind, nr, nc = W_CUTS[a]
    hr = nr // 2
    c = lax.axis_index("c")
    cidx = jnp.reshape(c, (1,)).astype(jnp.int32)

    def body(c_ref, g_ref, r_ref, f_ref, b_ref):
        s = g_ref[...] + r_ref[...]
        f_ref[...] = s.reshape(f_ref.shape)
        b_ref[...] = s.astype(BF16).reshape(b_ref.shape)

    if kind == "col":
        in_specs = [pl.BlockSpec((hr, nc), lambda k, cr: (cr[0], k)), pl.BlockSpec((hr, nc), lambda k, cr: (0, k))]
    else:
        grad = grad.reshape(NCHIP, 2, hr, nc)
        in_specs = [pl.BlockSpec((1, 1, hr, nc), lambda k, cr: (k, cr[0], 0, 0)), pl.BlockSpec((1, hr, nc), lambda k, cr: (k, 0, 0))]
    out_specs = [pl.BlockSpec((1, hr, nc), lambda k, cr: (k, 0, 0))] * 2
    return pl.pallas_call(
        body, name=name, out_shape=[SDS((NCHIP, hr, nc), F32), SDS((NCHIP, hr, nc), BF16)],
        grid_spec=pltpu.PrefetchScalarGridSpec(num_scalar_prefetch=1, grid=(NCHIP,), in_specs=in_specs, out_specs=out_specs),
        compiler_params=_cp(("arbitrary",), VMEM_CAP),
    )(cidx, grad, got)


def _scatter_chip_sums(sums):
    n = len(sums)

    def body(*refs):
        src, land = refs[:n], refs[n:2 * n]
        send_sems, recv_sems = refs[2 * n:]
        x, y, c = _place()
        chips = [(1 - x, y), (x, 1 - y), (1 - x, 1 - y)]
        sends = []
        for a in range(n):
            for j, (px, py) in enumerate(chips):
                cp = pltpu.make_async_remote_copy(src_ref=src[a].at[2 * px + py], dst_ref=land[a].at[j], send_sem=send_sems.at[3 * a + j],
                                                  recv_sem=recv_sems.at[3 * a + j], device_id=(px, py, c), device_id_type=MESH)
                cp.start()
                sends.append(cp)
        for cp in sends:
            cp.wait()

    return pl.pallas_call(
        body, name="scatter_chip_sums", out_shape=[SDS((3,) + s.shape[1:], BF16) for s in sums], in_specs=[ANY] * n, out_specs=[ANY] * n,
        scratch_shapes=[pltpu.SemaphoreType.DMA((3 * n,)), pltpu.SemaphoreType.DMA((3 * n,))],
        compiler_params=_cp(side=True),
    )(*sums)


def _reduce_mine(a, mine_f32, got, name):
    kind, nr, nc = W_CUTS[a]
    hr = nr // 2
    x, y, c = _place()
    where = jnp.stack([2 * x + y, c]).astype(jnp.int32)
    tr = min(hr, 256)

    def body(w_ref, m_ref, g_ref, o_ref):
        o_ref[...] = ((m_ref[0] + g_ref[0].astype(F32)) + g_ref[1].astype(F32)) + g_ref[2].astype(F32)

    return pl.pallas_call(
        body, name=name, out_shape=SDS((nr, nc), F32),
        grid_spec=pltpu.PrefetchScalarGridSpec(
            num_scalar_prefetch=1, grid=(hr // tr,),
            in_specs=[pl.BlockSpec((1, tr, nc), lambda i, wr: (wr[0], i, 0)), pl.BlockSpec((3, tr, nc), lambda i, wr: (0, i, 0))],
            out_specs=pl.BlockSpec((tr, nc), lambda i, wr: (wr[1] * (hr // tr) + i, 0))),
        compiler_params=_cp(("arbitrary",), VMEM_CAP),
    )(where, mine_f32, got)


def _join_halves(fulls):
    n = len(fulls)

    def body(*refs):
        full = refs[n:2 * n]
        send_sems, recv_sems = refs[2 * n:]
        x, y, c = _place()
        sibling = (x, y, 1 - c)

        def swap(a, half):
            rows = _half_rows(full[a], W_CUTS[a], half)
            return pltpu.make_async_remote_copy(src_ref=rows, dst_ref=rows, send_sem=send_sems.at[a], recv_sem=recv_sems.at[a],
                                                device_id=sibling, device_id_type=MESH)

        sends = [swap(a, c) for a in range(n)]
        for cp in sends:
            cp.start()
        for a, cp in enumerate(sends):
            cp.wait_send()
            swap(a, 1 - c).wait_recv()

    return pl.pallas_call(
        body, name="join_grad_halves", out_shape=[SDS((W_CUTS[a][1], W_CUTS[a][2]), F32) for a in range(n)],
        in_specs=[ANY] * n, out_specs=[ANY] * n,
        scratch_shapes=[pltpu.SemaphoreType.DMA((n,)), pltpu.SemaphoreType.DMA((n,))],
        input_output_aliases={a: a for a in range(n)}, compiler_params=_cp(side=True),
    )(*fulls)


def _ada_forward(c_all, w_ada, b_cols):
    nb, nc = c_all.shape[0], w_ada.shape[1]

    def body(c_ref, w_ref, b_ref, o_ref):
        cv = c_ref[...]
        sc = (cv * _sig(cv)).astype(BF16)
        o_ref[...] = _dot(sc, w_ref[...].astype(BF16)) + b_ref[...]

    return pl.pallas_call(body, name="ada_forward", out_shape=SDS((nb, nc), F32), compiler_params=_cp(vmem=VMEM_CAP // 2))(c_all, w_ada, b_cols)


def _ada_backward(c_all, dmod_cols, dmod_all):
    nb, nc = dmod_cols.shape

    def body(c_ref, d_ref, a_ref, gw_ref, gb_ref):
        cv = c_ref[...]
        sc = (cv * _sig(cv)).astype(BF16)
        gw_ref[...] = _dot_tn(sc, d_ref[...].astype(BF16))
        gb_ref[...] = jnp.sum(a_ref[...], axis=0, keepdims=True)

    return pl.pallas_call(body, name="ada_backward", out_shape=[SDS((D, nc), F32), SDS((1, dmod_all.shape[1]), F32)],
                          compiler_params=_cp(vmem=VMEM_CAP // 2))(c_all, dmod_cols, dmod_all)


def _modulate(x2, sc1p, shift, seq, tm=256):
    t = x2.shape[0]
    spt = seq // tm

    def body(x_ref, sc_ref, sh_ref, h_ref, ht_ref):
        h = x_ref[...] * sc_ref[0] + sh_ref[0]
        h_ref[...] = h.astype(BF16)
        ht_ref[...] = h.T.astype(BF16)

    per_seq = pl.BlockSpec((1, 1, D), lambda i: (i // spt, 0, 0))
    return pl.pallas_call(
        body, name="modulate", out_shape=[SDS((t, D), BF16), SDS((D, t), BF16)], grid=(t // tm,),
        in_specs=[pl.BlockSpec((tm, D), lambda i: (i, 0)), per_seq, per_seq],
        out_specs=[pl.BlockSpec((tm, D), lambda i: (i, 0)), pl.BlockSpec((D, tm), lambda i: (0, i))],
        compiler_params=_cp(("parallel",)),
    )(x2, sc1p, shift)


def _project(h, w, tm=1024):
    t = h.shape[0]

    def body(h_ref, w_ref, q_ref, k_ref, v_ref, g_ref):
        j = pl.program_id(1)
        acc = _dot(h_ref[...], w_ref[...])
        for n, ref in enumerate((q_ref, k_ref, v_ref)):
            @pl.when((j >= n * NQT) & (j < (n + 1) * NQT))
            def _(ref=ref):
                ref[...] = acc

        @pl.when(j >= 3 * NQT)
        def _():
            g_ref[...] = acc

    def part(n):
        return pl.BlockSpec((tm, TN), lambda i, j: (i, jnp.clip(j - n * NQT, 0, NQT - 1)))

    return pl.pallas_call(
        body, name="project", out_shape=[SDS((t, QW), F32)] * 3 + [SDS((t, NGATE), F32)], grid=(t // tm, NPT),
        in_specs=[pl.BlockSpec((tm, D), lambda i, j: (i, 0)), pl.BlockSpec((D, TN), lambda i, j: (0, j))],
        out_specs=[part(0), part(1), part(2), pl.BlockSpec((tm, TN), lambda i, j: (i, jnp.maximum(j - 3 * NQT, 0)))],
        compiler_params=_cp(("arbitrary", "arbitrary"), VMEM_CAP // 2),
    )(h, w)


def _bias_tables(rel_bias, buckets):
    def body(tab_ref, bk_ref, o_ref):
        a = lax.broadcasted_iota(jnp.int32, (BLK, 2 * BLK), 0)
        b = lax.broadcasted_iota(jnp.int32, (BLK, 2 * BLK), 1)
        steps = a + BLK - b
        valid = (steps >= 0) & (steps <= BLK)
        for g in range(3):
            bk = bk_ref[g]
            for j in range(4):
                def pick(kk, acc, bk=bk, col=4 * g + j):
                    return jnp.where(bk == kk, tab_ref[kk, col], acc)

                acc = lax.fori_loop(0, N_BUCKETS, pick, jnp.zeros((BLK, 2 * BLK), F32))
                o_ref[g, j] = jnp.where(valid, acc, NEG)

    return pl.pallas_call(
        body, name="bias_tables", out_shape=SDS((3, 4, BLK, 2 * BLK), F32),
        in_specs=[pl.BlockSpec(memory_space=pltpu.SMEM), VMEM_SPEC], out_specs=VMEM_SPEC,
    )(rel_bias, buckets)


def _bias_grad(ds_sum, buckets):
    def body(ds_ref, bk_ref, o_ref):
        lane = lax.broadcasted_iota(jnp.int32, (1, 128), 1)
        for g in range(3):
            def bucket(kk, carry, g=g):
                row = jnp.zeros((1, 128), F32)
                for j in range(4):
                    v = jnp.where(bk_ref[g] == kk, ds_ref[g, j], 0.0)
                    s = jnp.sum(jnp.sum(v, axis=1, keepdims=True), axis=0, keepdims=True)
                    row = jnp.where(lane == j, s, row)
                o_ref[g, pl.ds(kk, 1), :] = row
                return carry

            lax.fori_loop(0, N_BUCKETS, bucket, 0)

    return pl.pallas_call(body, name="bias_grad", out_shape=SDS((3, N_BUCKETS, 128), F32), in_specs=[VMEM_SPEC, VMEM_SPEC],
                          out_specs=VMEM_SPEC)(ds_sum, buckets)


def _sub_rows(d, r, n):
    return pl.ds(n * BLK * d + r, BLK) if d == 1 else pl.ds(n * BLK * d + r, BLK, stride=d)


def _attn_forward(g, q, k, v, bias, bsz, seq):
    d = DILATIONS[g]
    nblk = seq // d // BLK

    def body(q_ref, k_ref, v_ref, b_ref, o_ref, l_ref):
        hs = pl.program_id(1)
        for r in range(d):
            for n in range(nblk):
                rows = _sub_rows(d, r, n)
                qb = q_ref[rows, :].astype(BF16)
                s_c = _dot_nt(qb, k_ref[rows, :].astype(BF16)) * SCALE + b_ref[hs, :, BLK:]
                m = jnp.max(s_c, axis=1, keepdims=True)
                if n > 0:
                    prev = _sub_rows(d, r, n - 1)
                    s_p = _dot_nt(qb, k_ref[prev, :].astype(BF16)) * SCALE + b_ref[hs, :, :BLK]
                    m = jnp.maximum(m, jnp.max(s_p, axis=1, keepdims=True))
                p_c = jnp.exp(s_c - m)
                den = jnp.sum(p_c, axis=1, keepdims=True)
                acc = _dot(p_c.astype(BF16), v_ref[rows, :].astype(BF16))
                if n > 0:
                    p_p = jnp.exp(s_p - m)
                    den = den + jnp.sum(p_p, axis=1, keepdims=True)
                    acc = acc + _dot(p_p.astype(BF16), v_ref[prev, :].astype(BF16))
                o_ref[rows, :] = acc / den
                l_ref[rows, :] = jnp.broadcast_to(m + jnp.log(den), (BLK, HD))

    qkv_spec = pl.BlockSpec((seq, HD), lambda b, hh: (b, 4 * g + hh))
    out_spec = pl.BlockSpec((seq, HD), lambda b, hh: (b, hh))
    return pl.pallas_call(
        body, name=f"attn_forward_{g}", out_shape=[SDS((bsz * seq, AW), F32)] * 2, grid=(bsz, 4),
        in_specs=[qkv_spec, qkv_spec, qkv_spec, pl.BlockSpec((4, BLK, 2 * BLK), lambda b, hh: (0, 0, 0))],
        out_specs=[out_spec, out_spec],
        compiler_params=_cp(("parallel", "parallel"), VMEM_CAP // 2),
    )(q, k, v, bias)


def _attn_backward(g, q, k, v, do, dl, bias, prev_out, bsz, seq):
    d = DILATIONS[g]
    nblk = seq // d // BLK

    def body(q_ref, k_ref, v_ref, do_ref, dl_ref, b_ref, *rest):
        dq_ref, dk_ref, dv_ref, db_ref = rest[-4:]
        hs = pl.program_id(1)

        @pl.when((pl.program_id(0) == 0) & (hs == 0))
        def _():
            db_ref[...] = jnp.zeros_like(db_ref)

        dk_ref[...] = jnp.zeros_like(dk_ref)
        dv_ref[...] = jnp.zeros_like(dv_ref)
        for r in range(d):
            for n in range(nblk):
                rows = _sub_rows(d, r, n)
                qb = q_ref[rows, :].astype(BF16)
                dob = do_ref[rows, :].astype(BF16)
                both = dl_ref[rows, :]
                lse, delta = both[:, 0:1], both[:, 64:65]
                dq = jnp.zeros((BLK, HD), F32)
                parts = [(rows, slice(BLK, 2 * BLK))]
                if n > 0:
                    parts.append((_sub_rows(d, r, n - 1), slice(0, BLK)))
                for keys, band in parts:
                    kb, vb = k_ref[keys, :].astype(BF16), v_ref[keys, :].astype(BF16)
                    p = jnp.exp(_dot_nt(qb, kb) * SCALE + b_ref[hs, :, band] - lse)
                    ds = p * (_dot_nt(dob, vb) - delta)
                    dsb = ds.astype(BF16)
                    dv_ref[keys, :] += _dot_tn(p.astype(BF16), dob)
                    dk_ref[keys, :] += _dot_tn(dsb, qb) * SCALE
                    dq = dq + _dot(dsb, kb) * SCALE
                    db_ref[hs, :, band] += ds
                dq_ref[rows, :] = dq

    qkv_spec = pl.BlockSpec((seq, HD), lambda b, hh: (b, 4 * g + hh))
    out_spec = pl.BlockSpec((seq, HD), lambda b, hh: (b, hh))
    band_spec = pl.BlockSpec((4, BLK, 2 * BLK), lambda b, hh: (0, 0, 0))
    ins = [q, k, v, do, dl, bias]
    in_specs = [qkv_spec, qkv_spec, qkv_spec, out_spec, out_spec, band_spec]
    aliases = {}
    if prev_out is not None:
        ins += list(prev_out)
        in_specs += [ANY] * 3
        aliases = {6: 0, 7: 1, 8: 2}
    dq, dk, dv, db = pl.pallas_call(
        body, name=f"attn_backward_{g}", out_shape=[SDS((bsz * seq, QW), F32)] * 3 + [SDS((4, BLK, 2 * BLK), F32)], grid=(bsz, 4),
        in_specs=in_specs, out_specs=[qkv_spec] * 3 + [band_spec], input_output_aliases=aliases,
        compiler_params=_cp(("arbitrary", "arbitrary"), VMEM_CAP // 2),
    )(*ins)
    return (dq, dk, dv), db


def _mix_forward(gates, og, lg, x2, tgt, gate, w_ao, w_co, w_o, conv_w, conv_b, ln_g, ln_b, bsz, seq, tm=256):
    t = x2.shape[0]
    spt = seq // tm

    def body(g_ref, o1, o2, o3, l1, l2, l3, x_ref, t_ref, gate_ref, wao_ref, wco_ref, wo_ref, cw_ref, cb_ref, lng_ref, lnb_ref,
             ain_ref, sin_ref, mrg_ref, dy_ref, aout_ref, sout_ref, yc_ref, o_ref, lj_ref, dxr_ref, vec_ref, dgate_ref, zc_ref):
        b, i = pl.program_id(0), pl.program_id(1)

        @pl.when((b == 0) & (i == 0))
        def _():
            vec_ref[...] = jnp.zeros_like(vec_ref)

        @pl.when(i == 0)
        def _():
            zc_ref[...] = jnp.zeros_like(zc_ref)
            dgate_ref[...] = jnp.zeros_like(dgate_ref)

        g_attn, u, bg = g_ref[:, 0:512], g_ref[:, 512:1536], g_ref[:, 1536:2560]
        cg, g_conv = g_ref[:, 2560:3584], g_ref[:, 3584:4608]
        m_attn, m_conv = g_ref[:, 4608:5632], g_ref[:, 5632:6656]
        la, lb, lc = l1[...], l2[...], l3[...]
        mx = jnp.maximum(la, jnp.maximum(lb, lc))
        ea, eb, ec = jnp.exp(la - mx), jnp.exp(lb - mx), jnp.exp(lc - mx)
        den = ea + eb + ec
        o = (ea * o1[...] + eb * o2[...] + ec * o3[...]) / den
        o_ref[...] = o
        lj_ref[...] = mx + jnp.log(den)
        a_in = o * (g_attn * _sig(g_attn))
        ain_ref[...] = a_in.astype(BF16)
        a_out = _dot(a_in.astype(BF16), wao_ref[...])
        aout_ref[...] = a_out
        z = cg * u
        rows = lax.broadcasted_iota(jnp.int32, (tm, D), 0)
        c6, c7 = zc_ref[6:7, :], zc_ref[7:8, :]
        z1 = jnp.where(rows == 0, c7, pltpu.roll(z, 1, 0))
        z2 = jnp.where(rows == 0, c6, jnp.where(rows == 1, c7, pltpu.roll(z, 2, 0)))
        zc_ref[...] = z[tm - 8:tm, :]
        y_conv = (cw_ref[0:1, :] * z2 + cw_ref[1:2, :] * z1 + cw_ref[2:3, :] * z) + cb_ref[...]
        yc_ref[...] = y_conv
        s_in = bg * y_conv * (g_conv * _sig(g_conv))
        sin_ref[...] = s_in.astype(BF16)
        s_out = _dot(s_in.astype(BF16), wco_ref[...])
        sout_ref[...] = s_out
        merged = _sig(m_attn) * a_out + _sig(m_conv) * s_out
        mrg_ref[...] = merged.astype(BF16)
        y = _dot(merged.astype(BF16), wo_ref[...])
        gate1 = 1.0 + gate_ref[0]
        r = ALPHA * x_ref[...] + gate1 * y
        mu = jnp.mean(r, axis=1, keepdims=True)
        rc = r - mu
        rstd = lax.rsqrt(jnp.mean(rc * rc, axis=1, keepdims=True) + LN_EPS)
        xhat = rc * rstd
        diff = (xhat * lng_ref[...] + lnb_ref[...]) - t_ref[...]
        dout = diff * (1.0 / D)
        vec_ref[0:1, :] += jnp.sum(dout * xhat, axis=0, keepdims=True)
        vec_ref[1:2, :] += jnp.sum(dout, axis=0, keepdims=True)
        vec_ref[2:3, :] += jnp.sum(diff * diff, axis=0, keepdims=True)
        dxh = dout * lng_ref[...]
        dr = rstd * (dxh - jnp.mean(dxh, axis=1, keepdims=True) - xhat * jnp.mean(dxh * xhat, axis=1, keepdims=True))
        dxr_ref[...] = ALPHA * dr
        dy_ref[...] = (dr * gate1).astype(BF16)
        dgate_ref[0] += jnp.sum(dr * y, axis=0, keepdims=True)

    tok = lambda w: pl.BlockSpec((tm, w), lambda b, i: (b * spt + i, 0))
    const = lambda s: pl.BlockSpec(s, lambda b, i: (0,) * len(s))
    per_seq = pl.BlockSpec((1, 1, D), lambda b, i: (b, 0, 0))
    outs = pl.pallas_call(
        body, name="mix_forward", grid=(bsz, spt),
        out_shape=[SDS((t, AW), BF16), SDS((t, D), BF16), SDS((t, D), BF16), SDS((t, D), BF16), SDS((t, D), F32), SDS((t, D), F32),
                   SDS((t, D), F32), SDS((t, AW), F32), SDS((t, AW), F32), SDS((t, D), F32), SDS((8, D), F32), SDS((bsz, 1, D), F32)],
        in_specs=[tok(NGATE)] + [tok(AW)] * 6 + [tok(D), tok(D), per_seq, const((AW, D)), const((D, D)), const((D, D)),
                                                 const((3, D)), const((1, D)), const((1, D)), const((1, D))],
        out_specs=[tok(AW), tok(D), tok(D), tok(D), tok(D), tok(D), tok(D), tok(AW), tok(AW), tok(D), const((8, D)), per_seq],
        scratch_shapes=[pltpu.VMEM((8, D), F32)],
        compiler_params=_cp(("arbitrary", "arbitrary"), VMEM_CAP),
    )(gates, *og, *lg, x2, tgt, gate, w_ao, w_co, w_o, conv_w, conv_b, ln_g, ln_b)
    return outs


def _mix_backward(gates, dy, a_out, s_out, y_conv, o, lj, w_ao, w_co, w_o, conv_w, bsz, seq, tm=256):
    t = dy.shape[0]
    spt = seq // tm

    def body(g_ref, dy_ref, aout_ref, sout_ref, yc_ref, o_ref, lj_ref, wao_ref, wco_ref, wo_ref, cw_ref,
             dg_ref, do_ref, dl_ref, daout_ref, dsout_ref, vec_ref, car_ref):
        b, i = pl.program_id(0), pl.program_id(1)

        @pl.when((b == 0) & (i == 0))
        def _():
            vec_ref[...] = jnp.zeros_like(vec_ref)

        @pl.when(i == 0)
        def _():
            car_ref[...] = jnp.zeros_like(car_ref)

        g_attn, u, bg = g_ref[:, 0:512], g_ref[:, 512:1536], g_ref[:, 1536:2560]
        cg, g_conv = g_ref[:, 2560:3584], g_ref[:, 3584:4608]
        m_attn, m_conv = g_ref[:, 4608:5632], g_ref[:, 5632:6656]
        dmerged = _dot_nt(dy_ref[...], wo_ref[...])
        sa, sc = _sig(m_attn), _sig(m_conv)
        da_out = (dmerged * sa).astype(BF16)
        ds_out = (dmerged * sc).astype(BF16)
        daout_ref[...] = da_out
        dsout_ref[...] = ds_out
        dg_ref[:, 4608:5632] = (dmerged * aout_ref[...] * (sa * (1.0 - sa))).astype(BF16)
        dg_ref[:, 5632:6656] = (dmerged * sout_ref[...] * (sc * (1.0 - sc))).astype(BF16)
        da_in = _dot_nt(da_out, wao_ref[...])
        ds_in = _dot_nt(ds_out, wco_ref[...])
        sga = _sig(g_attn)
        o = o_ref[...]
        do = da_in * (g_attn * sga)
        do_ref[...] = do
        dg_ref[:, 0:512] = (da_in * o * (sga * (1.0 + g_attn * (1.0 - sga)))).astype(BF16)
        prod = do * o
        lane = lax.broadcasted_iota(jnp.int32, (tm, HD), 1)
        for j in range(4):
            cs = slice(j * HD, (j + 1) * HD)
            delta = jnp.sum(prod[:, cs], axis=1, keepdims=True)
            dl_ref[:, cs] = jnp.where(lane < 64, lj_ref[:, cs], delta)
        sgc = _sig(g_conv)
        silu_c = g_conv * sgc
        yc = yc_ref[...]
        dg_ref[:, 1536:2560] = (ds_in * yc * silu_c).astype(BF16)
        dg_ref[:, 3584:4608] = (ds_in * bg * yc * (sgc * (1.0 + g_conv * (1.0 - sgc)))).astype(BF16)
        dyc = ds_in * bg * silu_c
        rows = lax.broadcasted_iota(jnp.int32, (tm, D), 0)
        c0, c1 = car_ref[0:1, :], car_ref[1:2, :]
        n1 = jnp.where(rows == tm - 1, c0, pltpu.roll(dyc, tm - 1, 0))
        n2 = jnp.where(rows == tm - 2, c0, jnp.where(rows == tm - 1, c1, pltpu.roll(dyc, tm - 2, 0)))
        car_ref[...] = dyc[0:8, :]
        dz = cw_ref[2:3, :] * dyc + cw_ref[1:2, :] * n1 + cw_ref[0:1, :] * n2
        z = cg * u
        dg_ref[:, 512:1536] = (dz * cg).astype(BF16)
        dg_ref[:, 2560:3584] = (dz * u).astype(BF16)
        vec_ref[0:1, :] += jnp.sum(n2 * z, axis=0, keepdims=True)
        vec_ref[1:2, :] += jnp.sum(n1 * z, axis=0, keepdims=True)
        vec_ref[2:3, :] += jnp.sum(dyc * z, axis=0, keepdims=True)
        vec_ref[3:4, :] += jnp.sum(dyc, axis=0, keepdims=True)

    tok = lambda w: pl.BlockSpec((tm, w), lambda b, i: (b * spt + (spt - 1 - i), 0))
    const = lambda s: pl.BlockSpec(s, lambda b, i: (0,) * len(s))
    return pl.pallas_call(
        body, name="mix_backward", grid=(bsz, spt),
        out_shape=[SDS((t, NGATE), BF16), SDS((t, AW), F32), SDS((t, AW), F32), SDS((t, D), BF16), SDS((t, D), BF16), SDS((8, D), F32)],
        in_specs=[tok(NGATE), tok(D), tok(D), tok(D), tok(D), tok(AW), tok(AW), const((AW, D)), const((D, D)), const((D, D)), const((3, D))],
        out_specs=[tok(NGATE), tok(AW), tok(AW), tok(D), tok(D), const((8, D))],
        scratch_shapes=[pltpu.VMEM((8, D), F32)],
        compiler_params=_cp(("arbitrary", "arbitrary"), VMEM_CAP),
    )(gates, dy, a_out, s_out, y_conv, o, lj, w_ao, w_co, w_o, conv_w)


def _out_weight_grads(a_in, da_out, s_in, ds_out, merged, dy, tk=512):
    t = dy.shape[0]

    def body(ain_ref, da_ref, sin_ref, ds_ref, m_ref, dy_ref, gao_ref, gco_ref, go_ref):
        @pl.when(pl.program_id(0) == 0)
        def _():
            gao_ref[...] = jnp.zeros_like(gao_ref)
            gco_ref[...] = jnp.zeros_like(gco_ref)
            go_ref[...] = jnp.zeros_like(go_ref)

        gao_ref[...] += _dot_tn(ain_ref[...], da_ref[...])
        gco_ref[...] += _dot_tn(sin_ref[...], ds_ref[...])
        go_ref[...] += _dot_tn(m_ref[...], dy_ref[...])

    tok = lambda w: pl.BlockSpec((tk, w), lambda i: (i, 0))
    const = lambda s: pl.BlockSpec(s, lambda i: (0, 0))
    return pl.pallas_call(
        body, name="out_weight_grads", grid=(t // tk,), out_shape=[SDS((AW, D), F32), SDS((D, D), F32), SDS((D, D), F32)],
        in_specs=[tok(AW), tok(D), tok(D), tok(D), tok(D), tok(D)], out_specs=[const((AW, D)), const((D, D)), const((D, D))],
        compiler_params=_cp(("arbitrary",), VMEM_CAP),
    )(a_in, da_out, s_in, ds_out, merged, dy)


def _input_grad(dq, dk, dv, dgates, w, x2, dxr, sc1p, seq, tm=1024):
    t = x2.shape[0]
    spt = seq // tm
    bsz = t // seq

    def body(dq_ref, dk_ref, dv_ref, dg_ref, w_ref, x_ref, dxr_ref, sc_ref, dx_ref, dsh_ref, dsc_ref, acc_ref):
        i, j = pl.program_id(0), pl.program_id(1)

        @pl.when(j == 0)
        def _():
            acc_ref[...] = jnp.zeros_like(acc_ref)

        for n, ref in enumerate((dq_ref, dk_ref, dv_ref)):
            @pl.when((j >= n * NQT) & (j < (n + 1) * NQT))
            def _(ref=ref):
                acc_ref[...] += _dot_nt(ref[...].astype(BF16), w_ref[...])

        @pl.when(j >= 3 * NQT)
        def _():
            acc_ref[...] += _dot_nt(dg_ref[...], w_ref[...])

        @pl.when(j == NPT - 1)
        def _():
            dh = acc_ref[...]
            dx_ref[...] = dh * sc_ref[0] + dxr_ref[...]

            @pl.when(i % spt == 0)
            def _():
                dsh_ref[...] = jnp.zeros_like(dsh_ref)
                dsc_ref[...] = jnp.zeros_like(dsc_ref)

            dsh_ref[0] += jnp.sum(dh, axis=0, keepdims=True)
            dsc_ref[0] += jnp.sum(dh * x_ref[...], axis=0, keepdims=True)

    def part(n):
        return pl.BlockSpec((tm, TN), lambda i, j: (i, jnp.clip(j - n * NQT, 0, NQT - 1)))

    row = pl.BlockSpec((tm, D), lambda i, j: (i, 0))
    per_seq = pl.BlockSpec((1, 1, D), lambda i, j: (i // spt, 0, 0))
    return pl.pallas_call(
        body, name="input_grad", grid=(t // tm, NPT), out_shape=[SDS((t, D), F32), SDS((bsz, 1, D), F32), SDS((bsz, 1, D), F32)],
        in_specs=[part(0), part(1), part(2), pl.BlockSpec((tm, TN), lambda i, j: (i, jnp.maximum(j - 3 * NQT, 0))),
                  pl.BlockSpec((D, TN), lambda i, j: (0, j)), row, row, per_seq],
        out_specs=[row, per_seq, per_seq], scratch_shapes=[pltpu.VMEM((tm, D), F32)],
        compiler_params=_cp(("arbitrary", "arbitrary"), VMEM_CAP),
    )(dq, dk, dv, dgates, w, x2, dxr, sc1p)


def _in_weight_grad(ht, src, col0, prev, name, tk=1024):
    t = ht.shape[1]
    ncols = src.shape[1] // TN

    def body(ht_ref, s_ref, *rest):
        o_ref = rest[-1]

        @pl.when(pl.program_id(1) == 0)
        def _():
            o_ref[...] = jnp.zeros_like(o_ref)

        o_ref[...] += _dot(ht_ref[...], s_ref[...].astype(BF16))

    ins = [ht, src]
    in_specs = [pl.BlockSpec((D, tk), lambda j, i: (0, i)), pl.BlockSpec((tk, TN), lambda j, i: (i, j))]
    aliases = {}
    if prev is not None:
        ins.append(prev)
        in_specs.append(ANY)
        aliases = {2: 0}
    return pl.pallas_call(
        body, name=name, grid=(ncols, t // tk), out_shape=SDS((D, NCOL), F32), in_specs=in_specs,
        out_specs=pl.BlockSpec((D, TN), lambda j, i: (0, col0 + j)), input_output_aliases=aliases,
        compiler_params=_cp(("arbitrary", "arbitrary"), VMEM_CAP // 2),
    )(*ins)


def _sum_partials(gathered):
    def body(g_ref, o_ref):
        acc = g_ref[0]
        for k in range(1, 8):
            acc = acc + g_ref[k]
        o_ref[...] = acc

    return pl.pallas_call(body, name="sum_partials", out_shape=SDS(gathered.shape[1:], F32), in_specs=[VMEM_SPEC], out_specs=VMEM_SPEC)(gathered)


def _adamw(w, g, m, v, name, tr=256):
    r, cdim = w.shape
    tr = tr if cdim <= D else tr // 2
    tr = tr if (r % tr == 0 and r > tr) else r

    def body(w_ref, g_ref, m_ref, v_ref, d_ref, nm_ref, nv_ref):
        gv = g_ref[...]
        nm = B1 * m_ref[...] + (1.0 - B1) * gv
        nv = B2 * v_ref[...] + (1.0 - B2) * (gv * gv)
        m_hat = nm / (1.0 - B1 ** STEP)
        v_hat = nv / (1.0 - B2 ** STEP)
        d_ref[...] = -LR * (m_hat / (jnp.sqrt(v_hat) + EPS) + WD * w_ref[...])
        nm_ref[...] = nm
        nv_ref[...] = nv

    spec = pl.BlockSpec((tr, cdim), lambda i: (i, 0))
    return pl.pallas_call(
        body, name=name, grid=(r // tr,), out_shape=[SDS((r, cdim), F32)] * 3, in_specs=[spec] * 4, out_specs=[spec] * 3,
        compiler_params=_cp(("parallel",), VMEM_CAP // 2),
    )(w, g, m, v)


def _t5_bucket(dist):
    n = jnp.maximum(dist, 1).astype(F32)
    large = MAX_EXACT + (jnp.log(n / MAX_EXACT) / math.log(MAX_DISTANCE / MAX_EXACT) * (N_BUCKETS - MAX_EXACT)).astype(jnp.int32)
    large = jnp.minimum(large, N_BUCKETS - 1)
    return jnp.where(dist < MAX_EXACT, dist, large)


def _band_buckets():
    a = jnp.arange(BLK)[:, None]
    b = jnp.arange(2 * BLK)[None, :]
    steps = jnp.maximum(a + BLK - b, 0)
    return jnp.stack([_t5_bucket(steps * d) for d in DILATIONS]).astype(jnp.int32)


def _pad_rows(a, rows=8):
    return jnp.pad(a, ((0, rows - a.shape[0]), (0, 0)))


def kernel(x, c, w_ada, b_ada, w_in, conv_w, conv_b, rel_bias, w_attn_out, w_conv_out, w_o, ln_g, ln_b, loss_target, m_w_ada, m_b_ada, m_w_in, m_conv_w, m_conv_b, m_rel_bias, m_w_attn_out, m_w_conv_out, m_w_o, m_ln_g, m_ln_b, v_w_ada, v_b_ada, v_w_in, v_conv_w, v_conv_b, v_rel_bias, v_w_attn_out, v_w_conv_out, v_w_o, v_ln_g, v_ln_b):
    bsz, seq, _ = x.shape
    t = bsz * seq
    mx, my, mc = _place()
    chip = 2 * mx + my
    dev = 4 * mx + 2 * my + mc
    x2 = x.reshape(t, D)
    tgt = loss_target.reshape(t, D)

    mine = [_to_bf16_window(a, w[0], f"to_bf16_{a}") for a, w in enumerate((w_in, w_attn_out, w_conv_out, w_o))]
    w_in_f, w_ao_f, w_co_f, w_o_f = _gather_weights(mine)

    n_ada = w_ada.shape[2]
    c_all = _all_gather8(c, "gather_c").reshape(8 * bsz, D)
    b_cols = lax.dynamic_slice(b_ada, (0, chip * n_ada), (1, n_ada))
    mod_part = _ada_forward(c_all, w_ada[0], b_cols)
    mod_parts = _all_gather8(mod_part, "gather_mod")
    mod_all = mod_parts[0::2].transpose(1, 0, 2).reshape(8 * bsz, 3 * D)
    mod = lax.dynamic_slice(mod_all, (dev * bsz, 0), (bsz, 3 * D))
    shift = mod[:, 0:D].reshape(bsz, 1, D)
    sc1p = 1.0 + mod[:, D:2 * D].reshape(bsz, 1, D)
    gate = mod[:, 2 * D:].reshape(bsz, 1, D)
    cw_parts = _all_gather8(_pad_rows(conv_w[0]), "gather_conv_w")
    conv_w_f = cw_parts[0::2].transpose(1, 0, 2).reshape(8, D)[0:3]

    h, ht = _modulate(x2, sc1p, shift, seq)
    q, k, v, gates = _project(h, w_in_f)
    buckets = _band_buckets()
    bias = _bias_tables(rel_bias, buckets)
    og, lg = [], []
    for g in range(3):
        o_g, l_g = _attn_forward(g, q, k, v, bias[g], bsz, seq)
        og.append(o_g)
        lg.append(l_g)
    (a_in, s_in, merged, dy, a_out, s_out, y_conv, o, lj, dxr, vec_f, dgate) = _mix_forward(
        gates, og, lg, x2, tgt, gate, w_ao_f, w_co_f, w_o_f, conv_w_f, conv_b, ln_g, ln_b, bsz, seq)

    dgates, do, dl, da_out, ds_out, vec_b = _mix_backward(gates, dy, a_out, s_out, y_conv, o, lj, w_ao_f, w_co_f, w_o_f, conv_w_f, bsz, seq)
    g_ao, g_co, g_o = _out_weight_grads(a_in, da_out, s_in, ds_out, merged, dy)
    dqkv, dbs = None, []
    for g in range(3):
        dqkv, db = _attn_backward(g, q, k, v, do, dl, bias[g], dqkv, bsz, seq)
        dbs.append(db)
    dq, dk, dv = dqkv
    drb = _bias_grad(jnp.stack(dbs), buckets)
    drb = drb[:, :, 0:4].transpose(1, 0, 2).reshape(N_BUCKETS, 12)
    grad_x, dshift, dscale = _input_grad(dq, dk, dv, dgates, w_in_f, x2, dxr, sc1p, seq)
    g_in = None
    for n, src in enumerate((dq, dk, dv, dgates)):
        g_in = _in_weight_grad(ht, src, n * NQT, g_in, f"in_weight_grad_{n}")

    grads = [g_in, g_ao, g_co, g_o]
    got = _swap_halves(grads)
    sums = [_chip_sum(a, grads[a], got[a], f"chip_sum_{a}") for a in range(4)]
    landed = _scatter_chip_sums([s[1] for s in sums])
    halves = [_reduce_mine(a, sums[a][0], landed[a], f"reduce_mine_{a}") for a in range(4)]
    gw_in, gw_ao, gw_co, gw_o = _join_halves(halves)

    dmod = jnp.concatenate([dshift, dscale, dgate], axis=2).reshape(bsz * 3, D)
    drb_row = jnp.pad(drb.reshape(1, N_BUCKETS * 12), ((0, 0), (0, D - N_BUCKETS * 12)))
    packed = jnp.concatenate([vec_f, vec_b, _pad_rows(dmod), _pad_rows(drb_row)], axis=0)
    gathered = _all_gather8(packed, "gather_small")
    small = _sum_partials(gathered)
    g_ln_g, g_ln_b, loss_lanes = small[0:1], small[1:2], small[2:3]
    g_conv_w_full, g_conv_b = small[8:11], small[11:12]
    g_rel_bias = small[24, 0:N_BUCKETS * 12].reshape(N_BUCKETS, 12)
    loss = 0.5 / D * jnp.sum(loss_lanes)
    dmod_all = gathered[:, 16:16 + 3 * bsz, :].reshape(8 * bsz, 3 * D)
    dmod_cols = lax.dynamic_slice(dmod_all, (0, chip * n_ada), (8 * bsz, n_ada))
    gw_ada, gb_ada = _ada_backward(c_all, dmod_cols, dmod_all)
    n_cw = conv_w.shape[2]
    g_conv_w = lax.dynamic_slice(g_conv_w_full, (0, chip * n_cw), (3, n_cw))

    names = ["w_ada", "b_ada", "w_in", "conv_w", "conv_b", "rel_bias", "w_attn_out", "w_conv_out", "w_o", "ln_g", "ln_b"]
    two_d = lambda a: a.reshape(a.shape[-2:]) if a.ndim == 3 else a
    weights = dict(zip(names, map(two_d, (w_ada, b_ada, w_in, conv_w, conv_b, rel_bias, w_attn_out, w_conv_out, w_o, ln_g, ln_b))))
    ms = dict(zip(names, map(two_d, (m_w_ada, m_b_ada, m_w_in, m_conv_w, m_conv_b, m_rel_bias, m_w_attn_out, m_w_conv_out, m_w_o, m_ln_g, m_ln_b))))
    vs = dict(zip(names, map(two_d, (v_w_ada, v_b_ada, v_w_in, v_conv_w, v_conv_b, v_rel_bias, v_w_attn_out, v_w_conv_out, v_w_o, v_ln_g, v_ln_b))))
    grads = dict(zip(names, (gw_ada, gb_ada, gw_in, g_conv_w, g_conv_b, g_rel_bias, gw_ao, gw_co, gw_o, g_ln_g, g_ln_b)))
    shapes = dict(zip(names, (w_ada, b_ada, w_in, conv_w, conv_b, rel_bias, w_attn_out, w_conv_out, w_o, ln_g, ln_b)))
    deltas, new_m, new_v = {}, {}, {}
    for n in names:
        deltas[n], new_m[n], new_v[n] = _adamw(weights[n], grads[n], ms[n], vs[n], f"adamw_{n}")
    shaped = lambda d: [d[n].reshape(shapes[n].shape) for n in names]
    return (loss, grad_x.reshape(bsz, seq, D), *shaped(grads), *shaped(deltas), *shaped(new_m), *shaped(new_v))
```

```python
import math

import jax
import jax.numpy as jnp
from jax import lax
from jax.experimental import pallas as pl
from jax.experimental.pallas import tpu as pltpu

F32 = jnp.float32
BF16 = jnp.bfloat16
SDS = jax.ShapeDtypeStruct
MESH = pl.DeviceIdType.MESH
ANY = pl.BlockSpec(memory_space=pl.ANY)
VMEM_SPEC = pl.BlockSpec(memory_space=pltpu.VMEM)

D = 1024
HD = 128
BLK = 128
QW = 1536
AW = 512
NGATE = 6656
NCOL = 3 * QW + NGATE
TN = 512
NQT = QW // TN
NPT = NCOL // TN
DILATIONS = (1, 4, 16)
N_BUCKETS, MAX_EXACT, MAX_DISTANCE = 32, 16, 2048
ALPHA = 2.0 ** 0.25
LN_EPS = 1e-5
NEG = -1e30
SCALE = HD ** -0.5
LR, B1, B2, EPS, WD, STEP = 0.001, 0.9, 0.999, 1e-08, 0.01, 10
NCHIP = 4
VMEM_CAP = 60 * 2 ** 20


def _cp(sem=None, vmem=None, side=False):
    return pltpu.CompilerParams(dimension_semantics=sem, vmem_limit_bytes=vmem, has_side_effects=side)


def _dot(a, b):
    return jnp.dot(a, b, preferred_element_type=F32)


def _dot_nt(a, b):
    return lax.dot_general(a, b, (((1,), (1,)), ((), ())), preferred_element_type=F32)


def _dot_tn(a, b):
    return lax.dot_general(a, b, (((0,), (0,)), ((), ())), preferred_element_type=F32)


def _sig(x):
    return 1.0 / (1.0 + jnp.exp(-x))


def _place():
    x, y, c = lax.axis_index("x"), lax.axis_index("y"), lax.axis_index("c")
    return x, y, c


def _all_gather8(v, name):
    r, cdim = v.shape

    def body(v_ref, out_ref, send_sems, recv_sems, local_sem):
        x, y, c = _place()
        me = 4 * x + 2 * y + c
        peers = [(x, y, 1 - c), (1 - x, y, c), (x, 1 - y, c), (1 - x, 1 - y, c),
                 (1 - x, y, 1 - c), (x, 1 - y, 1 - c), (1 - x, 1 - y, 1 - c)]
        mine = pltpu.make_async_copy(v_ref, out_ref.at[me], local_sem)
        mine.start()

        def copy(k, block, to):
            return pltpu.make_async_remote_copy(src_ref=v_ref, dst_ref=out_ref.at[block], send_sem=send_sems.at[k],
                                                recv_sem=recv_sems.at[k], device_id=to, device_id_type=MESH)

        sends = [copy(k, me, p) for k, p in enumerate(peers)]
        for cp in sends:
            cp.start()
        for k, (px, py, pc) in enumerate(peers):
            copy(k, 4 * px + 2 * py + pc, (px, py, pc)).wait_recv()
        for cp in sends:
            cp.wait_send()
        mine.wait()

    return pl.pallas_call(
        body, name=name, out_shape=SDS((8, r, cdim), v.dtype), in_specs=[VMEM_SPEC], out_specs=VMEM_SPEC,
        scratch_shapes=[pltpu.SemaphoreType.DMA((7,)), pltpu.SemaphoreType.DMA((7,)), pltpu.SemaphoreType.DMA(())],
        compiler_params=_cp(side=True),
    )(v)


W_CUTS = (("col", D, NCOL // NCHIP), ("col", AW, D // NCHIP), ("row", D // NCHIP, D), ("row", D // NCHIP, D))
W_FULL = ((D, NCOL), (AW, D), (D, D), (D, D))


def _shard_window(ref, cut, k, half):
    kind, nr, nc = cut
    hr = nr // 2
    if kind == "col":
        rows = pl.ds(0, nr) if half is None else pl.ds(pl.multiple_of(half * hr, 16), hr)
        return ref.at[rows, pl.ds(pl.multiple_of(k * nc, 128), nc)]
    if half is None:
        return ref.at[pl.ds(pl.multiple_of(k * nr, 16), nr), :]
    return ref.at[pl.ds(pl.multiple_of(k * nr + half * hr, 16), hr), :]


def _half_rows(ref, cut, half):
    hr = cut[1] // 2
    return ref.at[pl.ds(pl.multiple_of(half * hr, 16), hr), :]


def _to_bf16_window(a, w, name):
    kind, nr, nc = W_CUTS[a]
    x, y, _ = _place()
    chip = jnp.reshape(2 * x + y, (1,)).astype(jnp.int32)
    tr = min(nr, 256)

    def body(c_ref, w_ref, o_ref):
        o_ref[...] = w_ref[...].astype(BF16)

    out_map = (lambda i, cr: (i, cr[0])) if kind == "col" else (lambda i, cr: (cr[0] * (nr // tr) + i, 0))
    return pl.pallas_call(
        body, name=name, out_shape=SDS(W_FULL[a], BF16),
        grid_spec=pltpu.PrefetchScalarGridSpec(num_scalar_prefetch=1, grid=(nr // tr,),
                                               in_specs=[pl.BlockSpec((tr, nc), lambda i, cr: (i, 0))], out_specs=pl.BlockSpec((tr, nc), out_map)),
        compiler_params=_cp(("arbitrary",)),
    )(chip, w)


def _gather_weights(fulls):
    n = len(fulls)

    def body(*refs):
        full = refs[n:2 * n]
        send_sems, recv_sems = refs[2 * n:]
        x, y, c = _place()
        me = 2 * x + y
        chips = [(1 - x, y), (x, 1 - y), (1 - x, 1 - y)]
        sibling = (x, y, 1 - c)

        def remote(a, k, chip, half, to):
            window = _shard_window(full[a], W_CUTS[a], chip, half)
            return pltpu.make_async_remote_copy(src_ref=window, dst_ref=window, send_sem=send_sems.at[k], recv_sem=recv_sems.at[k],
                                                device_id=to, device_id_type=MESH)

        sends = []
        for a in range(n):
            for j, (px, py) in enumerate(chips):
                cp = remote(a, 6 * a + j, me, c, (px, py, c))
                cp.start()
                sends.append(cp)
        for a in range(n):
            for j, (px, py) in enumerate(chips):
                chip = 2 * px + py
                remote(a, 6 * a + j, chip, c, (px, py, c)).wait_recv()
                cp = remote(a, 6 * a + 3 + j, chip, c, sibling)
                cp.start()
                sends.append(cp)
        for a in range(n):
            for j, (px, py) in enumerate(chips):
                remote(a, 6 * a + 3 + j, 2 * px + py, 1 - c, sibling).wait_recv()
        for cp in sends:
            cp.wait_send()

    return pl.pallas_call(
        body, name="gather_weights", out_shape=[SDS(s, BF16) for s in W_FULL], in_specs=[ANY] * n, out_specs=[ANY] * n,
        scratch_shapes=[pltpu.SemaphoreType.DMA((6 * n,)), pltpu.SemaphoreType.DMA((6 * n,))],
        input_output_aliases={a: a for a in range(n)}, compiler_params=_cp(side=True),
    )(*fulls)


def _swap_halves(grads):
    n = len(grads)
    shapes = []
    for a in range(n):
        kind, nr, nc = W_CUTS[a]
        shapes.append((W_FULL[a][0] // 2, W_FULL[a][1]) if kind == "col" else (NCHIP, nr // 2, nc))

    def pieces(a, ref, land, half):
        kind, nr, nc = W_CUTS[a]
        if kind == "col":
            hr = nr // 2
            return [(ref.at[pl.ds(pl.multiple_of(half * hr, 16), hr), :], land)]
        return [(_shard_window(ref, W_CUTS[a], k, half), land.at[k]) for k in range(NCHIP)]

    def body(*refs):
        src, land = refs[:n], refs[n:2 * n]
        send_sems, recv_sems = refs[2 * n:]
        x, y, c = _place()
        sibling = (x, y, 1 - c)
        sends = []
        k = 0
        for a in range(n):
            for s, d in pieces(a, src[a], land[a], 1 - c):
                cp = pltpu.make_async_remote_copy(src_ref=s, dst_ref=d, send_sem=send_sems.at[k], recv_sem=recv_sems.at[k],
                                                  device_id=sibling, device_id_type=MESH)
                cp.start()
                sends.append(cp)
                k += 1
        for cp in sends:
            cp.wait()

    n_sems = sum(1 if W_CUTS[a][0] == "col" else NCHIP for a in range(n))
    return pl.pallas_call(
        body, name="swap_grad_halves", out_shape=[SDS(s, F32) for s in shapes], in_specs=[ANY] * n, out_specs=[ANY] * n,
        scratch_shapes=[pltpu.SemaphoreType.DMA((n_sems,)), pltpu.SemaphoreType.DMA((n_sems,))],
        compiler_params=_cp(side=True),
    )(*grads)


def _chip_sum(a, grad, got, name):
    kind, nr, nc = W_CUTS[a]
    hr = nr // 2
    c = lax.axis_index("c")
    cidx = jnp.reshape(c, (1,)).astype(jnp.int32)

    def body(c_ref, g_ref, r_ref, f_ref, b_ref):
        s = g_ref[...] + r_ref[...]
        f_ref[...] = s.reshape(f_ref.shape)
        b_ref[...] = s.astype(BF16).reshape(b_ref.shape)

    if kind == "col":
        in_specs = [pl.BlockSpec((hr, nc), lambda k, cr: (cr[0], k)), pl.BlockSpec((hr, nc), lambda k, cr: (0, k))]
    else:
        grad = grad.reshape(NCHIP, 2, hr, nc)
        in_specs = [pl.BlockSpec((1, 1, hr, nc), lambda k, cr: (k, cr[0], 0, 0)), pl.BlockSpec((1, hr, nc), lambda k, cr: (k, 0, 0))]
    out_specs = [pl.BlockSpec((1, hr, nc), lambda k, cr: (k, 0, 0))] * 2
    return pl.pallas_call(
        body, name=name, out_shape=[SDS((NCHIP, hr, nc), F32), SDS((NCHIP, hr, nc), BF16)],
        grid_spec=pltpu.PrefetchScalarGridSpec(num_scalar_prefetch=1, grid=(NCHIP,), in_specs=in_specs, out_specs=out_specs),
        compiler_params=_cp(("arbitrary",), VMEM_CAP),
    )(cidx, grad, got)


def _reduce_mine(a, mine_f32, got, name):
    kind, nr, nc = W_CUTS[a]
    hr = nr // 2
    x, y, c = _place()
    where = jnp.stack([2 * x + y, c]).astype(jnp.int32)
    tr = min(hr, 256)

    def body(w_ref, m_ref, g_ref, o_ref):
        o_ref[...] = ((m_ref[0] + g_ref[0].astype(F32)) + g_ref[1].astype(F32)) + g_ref[2].astype(F32)

    return pl.pallas_call(
        body, name=name, out_shape=SDS((nr, nc), F32),
        grid_spec=pltpu.PrefetchScalarGridSpec(
            num_scalar_prefetch=1, grid=(hr // tr,),
            in_specs=[pl.BlockSpec((1, tr, nc), lambda i, wr: (wr[0], i, 0)), pl.BlockSpec((3, tr, nc), lambda i, wr: (0, i, 0))],
            out_specs=pl.BlockSpec((tr, nc), lambda i, wr: (wr[1] * (hr // tr) + i, 0))),
        compiler_params=_cp(("arbitrary",), VMEM_CAP),
    )(where, mine_f32, got)


def _join_halves(fulls):
    n = len(fulls)

    def body(*refs):
        full = refs[n:2 * n]
        send_sems, recv_sems = refs[2 * n:]
        x, y, c = _place()
        sibling = (x, y, 1 - c)

        def swap(a, half):
            rows = _half_rows(full[a], W_CUTS[a], half)
            return pltpu.make_async_remote_copy(src_ref=rows, dst_ref=rows, send_sem=send_sems.at[a], recv_sem=recv_sems.at[a],
                                                device_id=sibling, device_id_type=MESH)

        sends = [swap(a, c) for a in range(n)]
        for cp in sends:
            cp.start()
        for a, cp in enumerate(sends):
            cp.wait_send()
            swap(a, 1 - c).wait_recv()

    return pl.pallas_call(
        body, name="join_grad_halves", out_shape=[SDS((W_CUTS[a][1], W_CUTS[a][2]), F32) for a in range(n)],
        in_specs=[ANY] * n, out_specs=[ANY] * n,
        scratch_shapes=[pltpu.SemaphoreType.DMA((n,)), pltpu.SemaphoreType.DMA((n,))],
        input_output_aliases={a: a for a in range(n)}, compiler_params=_cp(side=True),
    )(*fulls)


def _ada_forward(c_all, w_ada, b_cols):
    nb, nc = c_all.shape[0], w_ada.shape[1]

    def body(c_ref, w_ref, b_ref, o_ref):
        cv = c_ref[...]
        sc = (cv * _sig(cv)).astype(BF16)
        o_ref[...] = _dot(sc, w_ref[...].astype(BF16)) + b_ref[...]

    return pl.pallas_call(body, name="ada_forward", out_shape=SDS((nb, nc), F32), compiler_params=_cp(vmem=VMEM_CAP // 2))(c_all, w_ada, b_cols)


def _ada_backward(c_all, dmod_cols, dmod_all):
    nb, nc = dmod_cols.shape

    def body(c_ref, d_ref, a_ref, gw_ref, gb_ref):
        cv = c_ref[...]
        sc = (cv * _sig(cv)).astype(BF16)
        gw_ref[...] = _dot_tn(sc, d_ref[...].astype(BF16))
        gb_ref[...] = jnp.sum(a_ref[...], axis=0, keepdims=True)

    return pl.pallas_call(body, name="ada_backward", out_shape=[SDS((D, nc), F32), SDS((1, dmod_all.shape[1]), F32)],
                          compiler_params=_cp(vmem=VMEM_CAP // 2))(c_all, dmod_cols, dmod_all)


def _modulate(x2, sc1p, shift, seq, tm=256):
    t = x2.shape[0]
    spt = seq // tm

    def body(x_ref, sc_ref, sh_ref, h_ref, ht_ref):
        h = x_ref[...] * sc_ref[0] + sh_ref[0]
        h_ref[...] = h.astype(BF16)
        ht_ref[...] = h.T.astype(BF16)

    per_seq = pl.BlockSpec((1, 1, D), lambda i: (i // spt, 0, 0))
    return pl.pallas_call(
        body, name="modulate", out_shape=[SDS((t, D), BF16), SDS((D, t), BF16)], grid=(t // tm,),
        in_specs=[pl.BlockSpec((tm, D), lambda i: (i, 0)), per_seq, per_seq],
        out_specs=[pl.BlockSpec((tm, D), lambda i: (i, 0)), pl.BlockSpec((D, tm), lambda i: (0, i))],
        compiler_params=_cp(("parallel",)),
    )(x2, sc1p, shift)


def _project(h, w, tm=1024):
    t = h.shape[0]

    def body(h_ref, w_ref, q_ref, k_ref, v_ref, g_ref):
        j = pl.program_id(1)
        acc = _dot(h_ref[...], w_ref[...])
        for n, ref in enumerate((q_ref, k_ref, v_ref)):
            @pl.when((j >= n * NQT) & (j < (n + 1) * NQT))
            def _(ref=ref):
                ref[...] = acc

        @pl.when(j >= 3 * NQT)
        def _():
            g_ref[...] = acc

    def part(n):
        return pl.BlockSpec((tm, TN), lambda i, j: (i, jnp.clip(j - n * NQT, 0, NQT - 1)))

    return pl.pallas_call(
        body, name="project", out_shape=[SDS((t, QW), F32)] * 3 + [SDS((t, NGATE), F32)], grid=(t // tm, NPT),
        in_specs=[pl.BlockSpec((tm, D), lambda i, j: (i, 0)), pl.BlockSpec((D, TN), lambda i, j: (0, j))],
        out_specs=[part(0), part(1), part(2), pl.BlockSpec((tm, TN), lambda i, j: (i, jnp.maximum(j - 3 * NQT, 0)))],
        compiler_params=_cp(("arbitrary", "arbitrary"), VMEM_CAP // 2),
    )(h, w)


def _bias_tables(rel_bias, buckets):
    def body(tab_ref, bk_ref, o_ref):
        a = lax.broadcasted_iota(jnp.int32, (BLK, 2 * BLK), 0)
        b = lax.broadcasted_iota(jnp.int32, (BLK, 2 * BLK), 1)
        steps = a + BLK - b
        valid = (steps >= 0) & (steps <= BLK)
        for g in range(3):
            bk = bk_ref[g]
            for j in range(4):
                def pick(kk, acc, bk=bk, col=4 * g + j):
                    return jnp.where(bk == kk, tab_ref[kk, col], acc)

                acc = lax.fori_loop(0, N_BUCKETS, pick, jnp.zeros((BLK, 2 * BLK), F32))
                o_ref[g, j] = jnp.where(valid, acc, NEG)

    return pl.pallas_call(
        body, name="bias_tables", out_shape=SDS((3, 4, BLK, 2 * BLK), F32),
        in_specs=[pl.BlockSpec(memory_space=pltpu.SMEM), VMEM_SPEC], out_specs=VMEM_SPEC,
    )(rel_bias, buckets)


def _bias_grad(ds_sum, buckets):
    def body(ds_ref, bk_ref, o_ref):
        lane = lax.broadcasted_iota(jnp.int32, (1, 128), 1)
        for g in range(3):
            def bucket(kk, carry, g=g):
                row = jnp.zeros((1, 128), F32)
                for j in range(4):
                    v = jnp.where(bk_ref[g] == kk, ds_ref[g, j], 0.0)
                    s = jnp.sum(jnp.sum(v, axis=1, keepdims=True), axis=0, keepdims=True)
                    row = jnp.where(lane == j, s, row)
                o_ref[g, pl.ds(kk, 1), :] = row
                return carry

            lax.fori_loop(0, N_BUCKETS, bucket, 0)

    return pl.pallas_call(body, name="bias_grad", out_shape=SDS((3, N_BUCKETS, 128), F32), in_specs=[VMEM_SPEC, VMEM_SPEC],
                          out_specs=VMEM_SPEC)(ds_sum, buckets)


def _sub_rows(d, r, n):
    return pl.ds(n * BLK * d + r, BLK) if d == 1 else pl.ds(n * BLK * d + r, BLK, stride=d)


def _attn_forward(g, q, k, v, bias, bsz, seq):
    d = DILATIONS[g]
    nblk = seq // d // BLK

    def body(q_ref, k_ref, v_ref, b_ref, o_ref, l_ref):
        hs = pl.program_id(1)
        for r in range(d):
            for n in range(nblk):
                rows = _sub_rows(d, r, n)
                qb = q_ref[rows, :].astype(BF16)
                s_c = _dot_nt(qb, k_ref[rows, :].astype(BF16)) * SCALE + b_ref[hs, :, BLK:]
                m = jnp.max(s_c, axis=1, keepdims=True)
                if n > 0:
                    prev = _sub_rows(d, r, n - 1)
                    s_p = _dot_nt(qb, k_ref[prev, :].astype(BF16)) * SCALE + b_ref[hs, :, :BLK]
                    m = jnp.maximum(m, jnp.max(s_p, axis=1, keepdims=True))
                p_c = jnp.exp(s_c - m)
                den = jnp.sum(p_c, axis=1, keepdims=True)
                acc = _dot(p_c.astype(BF16), v_ref[rows, :].astype(BF16))
                if n > 0:
                    p_p = jnp.exp(s_p - m)
                    den = den + jnp.sum(p_p, axis=1, keepdims=True)
                    acc = acc + _dot(p_p.astype(BF16), v_ref[prev, :].astype(BF16))
                o_ref[rows, :] = acc / den
                l_ref[rows, :] = jnp.broadcast_to(m + jnp.log(den), (BLK, HD))

    qkv_spec = pl.BlockSpec((seq, HD), lambda b, hh: (b, 4 * g + hh))
    out_spec = pl.BlockSpec((seq, HD), lambda b, hh: (b, hh))
    return pl.pallas_call(
        body, name=f"attn_forward_{g}", out_shape=[SDS((bsz * seq, AW), F32)] * 2, grid=(bsz, 4),
        in_specs=[qkv_spec, qkv_spec, qkv_spec, pl.BlockSpec((4, BLK, 2 * BLK), lambda b, hh: (0, 0, 0))],
        out_specs=[out_spec, out_spec],
        compiler_params=_cp(("parallel", "parallel"), VMEM_CAP // 2),
    )(q, k, v, bias)


def _attn_backward(g, q, k, v, do, dl, bias, prev_out, bsz, seq):
    d = DILATIONS[g]
    nblk = seq // d // BLK

    def body(q_ref, k_ref, v_ref, do_ref, dl_ref, b_ref, *rest):
        dq_ref, dk_ref, dv_ref, db_ref = rest[-4:]
        hs = pl.program_id(1)

        @pl.when((pl.program_id(0) == 0) & (hs == 0))
        def _():
            db_ref[...] = jnp.zeros_like(db_ref)

        dk_ref[...] = jnp.zeros_like(dk_ref)
        dv_ref[...] = jnp.zeros_like(dv_ref)
        for r in range(d):
            for n in range(nblk):
                rows = _sub_rows(d, r, n)
                qb = q_ref[rows, :].astype(BF16)
                dob = do_ref[rows, :].astype(BF16)
                both = dl_ref[rows, :]
                lse, delta = both[:, 0:1], both[:, 64:65]
                dq = jnp.zeros((BLK, HD), F32)
                parts = [(rows, slice(BLK, 2 * BLK))]
                if n > 0:
                    parts.append((_sub_rows(d, r, n - 1), slice(0, BLK)))
                for keys, band in parts:
                    kb, vb = k_ref[keys, :].astype(BF16), v_ref[keys, :].astype(BF16)
                    p = jnp.exp(_dot_nt(qb, kb) * SCALE + b_ref[hs, :, band] - lse)
                    ds = p * (_dot_nt(dob, vb) - delta)
                    dsb = ds.astype(BF16)
                    dv_ref[keys, :] += _dot_tn(p.astype(BF16), dob)
                    dk_ref[keys, :] += _dot_tn(dsb, qb) * SCALE
                    dq = dq + _dot(dsb, kb) * SCALE
                    db_ref[hs, :, band] += ds
                dq_ref[rows, :] = dq

    qkv_spec = pl.BlockSpec((seq, HD), lambda b, hh: (b, 4 * g + hh))
    out_spec = pl.BlockSpec((seq, HD), lambda b, hh: (b, hh))
    band_spec = pl.BlockSpec((4, BLK, 2 * BLK), lambda b, hh: (0, 0, 0))
    ins = [q, k, v, do, dl, bias]
    in_specs = [qkv_spec, qkv_spec, qkv_spec, out_spec, out_spec, band_spec]
    aliases = {}
    if prev_out is not None:
        ins += list(prev_out)
        in_specs += [ANY] * 3
        aliases = {6: 0, 7: 1, 8: 2}
    dq, dk, dv, db = pl.pallas_call(
        body, name=f"attn_backward_{g}", out_shape=[SDS((bsz * seq, QW), F32)] * 3 + [SDS((4, BLK, 2 * BLK), F32)], grid=(bsz, 4),
        in_specs=in_specs, out_specs=[qkv_spec] * 3 + [band_spec], input_output_aliases=aliases,
        compiler_params=_cp(("arbitrary", "arbitrary"), VMEM_CAP // 2),
    )(*ins)
    return (dq, dk, dv), db


def _mix_forward(gates, og, lg, x2, tgt, gate, w_ao, w_co, w_o, conv_w, conv_b, ln_g, ln_b, bsz, seq, tm=256):
    t = x2.shape[0]
    spt = seq // tm

    def body(g_ref, o1, o2, o3, l1, l2, l3, x_ref, t_ref, gate_ref, wao_ref, wco_ref, wo_ref, cw_ref, cb_ref, lng_ref, lnb_ref,
             ain_ref, sin_ref, mrg_ref, dy_ref, aout_ref, sout_ref, yc_ref, o_ref, lj_ref, dxr_ref, vec_ref, dgate_ref, zc_ref):
        b, i = pl.program_id(0), pl.program_id(1)

        @pl.when((b == 0) & (i == 0))
        def _():
            vec_ref[...] = jnp.zeros_like(vec_ref)

        @pl.when(i == 0)
        def _():
            zc_ref[...] = jnp.zeros_like(zc_ref)
            dgate_ref[...] = jnp.zeros_like(dgate_ref)

        g_attn, u, bg = g_ref[:, 0:512], g_ref[:, 512:1536], g_ref[:, 1536:2560]
        cg, g_conv = g_ref[:, 2560:3584], g_ref[:, 3584:4608]
        m_attn, m_conv = g_ref[:, 4608:5632], g_ref[:, 5632:6656]
        la, lb, lc = l1[...], l2[...], l3[...]
        mx = jnp.maximum(la, jnp.maximum(lb, lc))
        ea, eb, ec = jnp.exp(la - mx), jnp.exp(lb - mx), jnp.exp(lc - mx)
        den = ea + eb + ec
        o = (ea * o1[...] + eb * o2[...] + ec * o3[...]) / den
        o_ref[...] = o
        lj_ref[...] = mx + jnp.log(den)
        a_in = o * (g_attn * _sig(g_attn))
        ain_ref[...] = a_in.astype(BF16)
        a_out = _dot(a_in.astype(BF16), wao_ref[...])
        aout_ref[...] = a_out
        z = cg * u
        rows = lax.broadcasted_iota(jnp.int32, (tm, D), 0)
        c6, c7 = zc_ref[6:7, :], zc_ref[7:8, :]
        z1 = jnp.where(rows == 0, c7, pltpu.roll(z, 1, 0))
        z2 = jnp.where(rows == 0, c6, jnp.where(rows == 1, c7, pltpu.roll(z, 2, 0)))
        zc_ref[...] = z[tm - 8:tm, :]
        y_conv = (cw_ref[0:1, :] * z2 + cw_ref[1:2, :] * z1 + cw_ref[2:3, :] * z) + cb_ref[...]
        yc_ref[...] = y_conv
        s_in = bg * y_conv * (g_conv * _sig(g_conv))
        sin_ref[...] = s_in.astype(BF16)
        s_out = _dot(s_in.astype(BF16), wco_ref[...])
        sout_ref[...] = s_out
        merged = _sig(m_attn) * a_out + _sig(m_conv) * s_out
        mrg_ref[...] = merged.astype(BF16)
        y = _dot(merged.astype(BF16), wo_ref[...])
        gate1 = 1.0 + gate_ref[0]
        r = ALPHA * x_ref[...] + gate1 * y
        mu = jnp.mean(r, axis=1, keepdims=True)
        rc = r - mu
        rstd = lax.rsqrt(jnp.mean(rc * rc, axis=1, keepdims=True) + LN_EPS)
        xhat = rc * rstd
        diff = (xhat * lng_ref[...] + lnb_ref[...]) - t_ref[...]
        dout = diff * (1.0 / D)
        vec_ref[0:1, :] += jnp.sum(dout * xhat, axis=0, keepdims=True)
        vec_ref[1:2, :] += jnp.sum(dout, axis=0, keepdims=True)
        vec_ref[2:3, :] += jnp.sum(diff * diff, axis=0, keepdims=True)
        dxh = dout * lng_ref[...]
        dr = rstd * (dxh - jnp.mean(dxh, axis=1, keepdims=True) - xhat * jnp.mean(dxh * xhat, axis=1, keepdims=True))
        dxr_ref[...] = ALPHA * dr
        dy_ref[...] = (dr * gate1).astype(BF16)
        dgate_ref[0] += jnp.sum(dr * y, axis=0, keepdims=True)

    tok = lambda w: pl.BlockSpec((tm, w), lambda b, i: (b * spt + i, 0))
    const = lambda s: pl.BlockSpec(s, lambda b, i: (0,) * len(s))
    per_seq = pl.BlockSpec((1, 1, D), lambda b, i: (b, 0, 0))
    outs = pl.pallas_call(
        body, name="mix_forward", grid=(bsz, spt),
        out_shape=[SDS((t, AW), BF16), SDS((t, D), BF16), SDS((t, D), BF16), SDS((t, D), BF16), SDS((t, D), F32), SDS((t, D), F32),
                   SDS((t, D), F32), SDS((t, AW), F32), SDS((t, AW), F32), SDS((t, D), F32), SDS((8, D), F32), SDS((bsz, 1, D), F32)],
        in_specs=[tok(NGATE)] + [tok(AW)] * 6 + [tok(D), tok(D), per_seq, const((AW, D)), const((D, D)), const((D, D)),
                                                 const((3, D)), const((1, D)), const((1, D)), const((1, D))],
        out_specs=[tok(AW), tok(D), tok(D), tok(D), tok(D), tok(D), tok(D), tok(AW), tok(AW), tok(D), const((8, D)), per_seq],
        scratch_shapes=[pltpu.VMEM((8, D), F32)],
        compiler_params=_cp(("arbitrary", "arbitrary"), VMEM_CAP),
    )(gates, *og, *lg, x2, tgt, gate, w_ao, w_co, w_o, conv_w, conv_b, ln_g, ln_b)
    return outs


def _mix_backward(gates, dy, a_out, s_out, y_conv, o, lj, w_ao, w_co, w_o, conv_w, bsz, seq, tm=256):
    t = dy.shape[0]
    spt = seq // tm

    def body(g_ref, dy_ref, aout_ref, sout_ref, yc_ref, o_ref, lj_ref, wao_ref, wco_ref, wo_ref, cw_ref,
             dg_ref, do_ref, dl_ref, daout_ref, dsout_ref, vec_ref, car_ref):
        b, i = pl.program_id(0), pl.program_id(1)

        @pl.when((b == 0) & (i == 0))
        def _():
            vec_ref[...] = jnp.zeros_like(vec_ref)

        @pl.when(i == 0)
        def _():
            car_ref[...] = jnp.zeros_like(car_ref)

        g_attn, u, bg = g_ref[:, 0:512], g_ref[:, 512:1536], g_ref[:, 1536:2560]
        cg, g_conv = g_ref[:, 2560:3584], g_ref[:, 3584:4608]
        m_attn, m_conv = g_ref[:, 4608:5632], g_ref[:, 5632:6656]
        dmerged = _dot_nt(dy_ref[...], wo_ref[...])
        sa, sc = _sig(m_attn), _sig(m_conv)
        da_out = (dmerged * sa).astype(BF16)
        ds_out = (dmerged * sc).astype(BF16)
        daout_ref[...] = da_out
        dsout_ref[...] = ds_out
        dg_ref[:, 4608:5632] = (dmerged * aout_ref[...] * (sa * (1.0 - sa))).astype(BF16)
        dg_ref[:, 5632:6656] = (dmerged * sout_ref[...] * (sc * (1.0 - sc))).astype(BF16)
        da_in = _dot_nt(da_out, wao_ref[...])
        ds_in = _dot_nt(ds_out, wco_ref[...])
        sga = _sig(g_attn)
        o = o_ref[...]
        do = da_in * (g_attn * sga)
        do_ref[...] = do
        dg_ref[:, 0:512] = (da_in * o * (sga * (1.0 + g_attn * (1.0 - sga)))).astype(BF16)
        prod = do * o
        lane = lax.broadcasted_iota(jnp.int32, (tm, HD), 1)
        for j in range(4):
            cs = slice(j * HD, (j + 1) * HD)
            delta = jnp.sum(prod[:, cs], axis=1, keepdims=True)
            dl_ref[:, cs] = jnp.where(lane < 64, lj_ref[:, cs], delta)
        sgc = _sig(g_conv)
        silu_c = g_conv * sgc
        yc = yc_ref[...]
        dg_ref[:, 1536:2560] = (ds_in * yc * silu_c).astype(BF16)
        dg_ref[:, 3584:4608] = (ds_in * bg * yc * (sgc * (1.0 + g_conv * (1.0 - sgc)))).astype(BF16)
        dyc = ds_in * bg * silu_c
        rows = lax.broadcasted_iota(jnp.int32, (tm, D), 0)
        c0, c1 = car_ref[0:1, :], car_ref[1:2, :]
        n1 = jnp.where(rows == tm - 1, c0, pltpu.roll(dyc, tm - 1, 0))
        n2 = jnp.where(rows == tm - 2, c0, jnp.where(rows == tm - 1, c1, pltpu.roll(dyc, tm - 2, 0)))
        car_ref[...] = dyc[0:8, :]
        dz = cw_ref[2:3, :] * dyc + cw_ref[1:2, :] * n1 + cw_ref[0:1, :] * n2
        z = cg * u
        dg_ref[:, 512:1536] = (dz * cg).astype(BF16)
        dg_ref[:, 2560:3584] = (dz * u).astype(BF16)
        vec_ref[0:1, :] += jnp.sum(n2 * z, axis=0, keepdims=True)
        vec_ref[1:2, :] += jnp.sum(n1 * z, axis=0, keepdims=True)
        vec_ref[2:3, :] += jnp.sum(dyc * z, axis=0, keepdims=True)
        vec_ref[3:4, :] += jnp.sum(dyc, axis=0, keepdims=True)

    tok = lambda w: pl.BlockSpec((tm, w), lambda b, i: (b * spt + (spt - 1 - i), 0))
    const = lambda s: pl.BlockSpec(s, lambda b, i: (0,) * len(s))
    return pl.pallas_call(
        body, name="mix_backward", grid=(bsz, spt),
        out_shape=[SDS((t, NGATE), BF16), SDS((t, AW), F32), SDS((t, AW), F32), SDS((t, D), BF16), SDS((t, D), BF16), SDS((8, D), F32)],
        in_specs=[tok(NGATE), tok(D), tok(D), tok(D), tok(D), tok(AW), tok(AW), const((AW, D)), const((D, D)), const((D, D)), const((3, D))],
        out_specs=[tok(NGATE), tok(AW), tok(AW), tok(D), tok(D), const((8, D))],
        scratch_shapes=[pltpu.VMEM((8, D), F32)],
        compiler_params=_cp(("arbitrary", "arbitrary"), VMEM_CAP),
    )(gates, dy, a_out, s_out, y_conv, o, lj, w_ao, w_co, w_o, conv_w)


def _out_weight_grads(a_in, da_out, s_in, ds_out, merged, dy, tk=512):
    t = dy.shape[0]

    def body(ain_ref, da_ref, sin_ref, ds_ref, m_ref, dy_ref, gao_ref, gco_ref, go_ref):
        @pl.when(pl.program_id(0) == 0)
        def _():
            gao_ref[...] = jnp.zeros_like(gao_ref)
            gco_ref[...] = jnp.zeros_like(gco_ref)
            go_ref[...] = jnp.zeros_like(go_ref)

        gao_ref[...] += _dot_tn(ain_ref[...], da_ref[...])
        gco_ref[...] += _dot_tn(sin_ref[...], ds_ref[...])
        go_ref[...] += _dot_tn(m_ref[...], dy_ref[...])

    tok = lambda w: pl.BlockSpec((tk, w), lambda i: (i, 0))
    const = lambda s: pl.BlockSpec(s, lambda i: (0, 0))
    return pl.pallas_call(
        body, name="out_weight_grads", grid=(t // tk,), out_shape=[SDS((AW, D), F32), SDS((D, D), F32), SDS((D, D), F32)],
        in_specs=[tok(AW), tok(D), tok(D), tok(D), tok(D), tok(D)], out_specs=[const((AW, D)), const((D, D)), const((D, D))],
        compiler_params=_cp(("arbitrary",), VMEM_CAP),
    )(a_in, da_out, s_in, ds_out, merged, dy)


def _input_grad(dq, dk, dv, dgates, w, x2, dxr, sc1p, seq, sums, tm=1024):
    t = x2.shape[0]
    spt = seq // tm
    bsz = t // seq
    n = len(sums)

    def body(dq_ref, dk_ref, dv_ref, dg_ref, w_ref, x_ref, dxr_ref, sc_ref, *rest):
        src, (dx_ref, dsh_ref, dsc_ref), land = rest[:n], rest[n:n + 3], rest[n + 3:2 * n + 3]
        acc_ref, send_sems, recv_sems = rest[2 * n + 3:]
        i, j = pl.program_id(0), pl.program_id(1)
        px, py, pc = _place()
        chips = [(1 - px, py), (px, 1 - py), (1 - px, 1 - py)]
        copies = [pltpu.make_async_remote_copy(src_ref=src[a].at[2 * cx + cy], dst_ref=land[a].at[r], send_sem=send_sems.at[3 * a + r],
                                               recv_sem=recv_sems.at[3 * a + r], device_id=(cx, cy, pc), device_id_type=MESH)
                  for a in range(n) for r, (cx, cy) in enumerate(chips)]

        @pl.when((i == 0) & (j == 0))
        def _():
            for cp in copies:
                cp.start()

        @pl.when(j == 0)
        def _():
            acc_ref[...] = jnp.zeros_like(acc_ref)

        for k, ref in enumerate((dq_ref, dk_ref, dv_ref)):
            @pl.when((j >= k * NQT) & (j < (k + 1) * NQT))
            def _(ref=ref):
                acc_ref[...] += _dot_nt(ref[...].astype(BF16), w_ref[...])

        @pl.when(j >= 3 * NQT)
        def _():
            acc_ref[...] += _dot_nt(dg_ref[...], w_ref[...])

        @pl.when(j == NPT - 1)
        def _():
            dh = acc_ref[...]
            dx_ref[...] = dh * sc_ref[0] + dxr_ref[...]

            @pl.when(i % spt == 0)
            def _():
                dsh_ref[...] = jnp.zeros_like(dsh_ref)
                dsc_ref[...] = jnp.zeros_like(dsc_ref)

            dsh_ref[0] += jnp.sum(dh, axis=0, keepdims=True)
            dsc_ref[0] += jnp.sum(dh * x_ref[...], axis=0, keepdims=True)

        @pl.when((i == t // tm - 1) & (j == NPT - 1))
        def _():
            for cp in copies:
                cp.wait()

    def part(k):
        return pl.BlockSpec((tm, TN), lambda i, j: (i, jnp.clip(j - k * NQT, 0, NQT - 1)))

    row = pl.BlockSpec((tm, D), lambda i, j: (i, 0))
    per_seq = pl.BlockSpec((1, 1, D), lambda i, j: (i // spt, 0, 0))
    outs = pl.pallas_call(
        body, name="input_grad", grid=(t // tm, NPT),
        out_shape=[SDS((t, D), F32), SDS((bsz, 1, D), F32), SDS((bsz, 1, D), F32)] + [SDS((3,) + s.shape[1:], BF16) for s in sums],
        in_specs=[part(0), part(1), part(2), pl.BlockSpec((tm, TN), lambda i, j: (i, jnp.maximum(j - 3 * NQT, 0))),
                  pl.BlockSpec((D, TN), lambda i, j: (0, j)), row, row, per_seq] + [ANY] * n,
        out_specs=[row, per_seq, per_seq] + [ANY] * n,
        scratch_shapes=[pltpu.VMEM((tm, D), F32), pltpu.SemaphoreType.DMA((3 * NCHIP,)), pltpu.SemaphoreType.DMA((3 * NCHIP,))],
        compiler_params=_cp(("arbitrary", "arbitrary"), VMEM_CAP, side=True),
    )(dq, dk, dv, dgates, w, x2, dxr, sc1p, *sums)
    return outs[0], outs[1], outs[2], outs[3:]


def _in_weight_grad(ht, src, col0, prev, name, tk=1024):
    t = ht.shape[1]
    ncols = src.shape[1] // TN

    def body(ht_ref, s_ref, *rest):
        o_ref = rest[-1]

        @pl.when(pl.program_id(1) == 0)
        def _():
            o_ref[...] = jnp.zeros_like(o_ref)

        o_ref[...] += _dot(ht_ref[...], s_ref[...].astype(BF16))

    ins = [ht, src]
    in_specs = [pl.BlockSpec((D, tk), lambda j, i: (0, i)), pl.BlockSpec((tk, TN), lambda j, i: (i, j))]
    aliases = {}
    if prev is not None:
        ins.append(prev)
        in_specs.append(ANY)
        aliases = {2: 0}
    return pl.pallas_call(
        body, name=name, grid=(ncols, t // tk), out_shape=SDS((D, NCOL), F32), in_specs=in_specs,
        out_specs=pl.BlockSpec((D, TN), lambda j, i: (0, col0 + j)), input_output_aliases=aliases,
        compiler_params=_cp(("arbitrary", "arbitrary"), VMEM_CAP // 2),
    )(*ins)


def _sum_partials(gathered):
    def body(g_ref, o_ref):
        acc = g_ref[0]
        for k in range(1, 8):
            acc = acc + g_ref[k]
        o_ref[...] = acc

    return pl.pallas_call(body, name="sum_partials", out_shape=SDS(gathered.shape[1:], F32), in_specs=[VMEM_SPEC], out_specs=VMEM_SPEC)(gathered)


def _adamw(w, g, m, v, name, tr=256):
    r, cdim = w.shape
    tr = tr if cdim <= D else tr // 2
    tr = tr if (r % tr == 0 and r > tr) else r

    def body(w_ref, g_ref, m_ref, v_ref, d_ref, nm_ref, nv_ref):
        gv = g_ref[...]
        nm = B1 * m_ref[...] + (1.0 - B1) * gv
        nv = B2 * v_ref[...] + (1.0 - B2) * (gv * gv)
        m_hat = nm / (1.0 - B1 ** STEP)
        v_hat = nv / (1.0 - B2 ** STEP)
        d_ref[...] = -LR * (m_hat / (jnp.sqrt(v_hat) + EPS) + WD * w_ref[...])
        nm_ref[...] = nm
        nv_ref[...] = nv

    spec = pl.BlockSpec((tr, cdim), lambda i: (i, 0))
    return pl.pallas_call(
        body, name=name, grid=(r // tr,), out_shape=[SDS((r, cdim), F32)] * 3, in_specs=[spec] * 4, out_specs=[spec] * 3,
        compiler_params=_cp(("parallel",), VMEM_CAP // 2),
    )(w, g, m, v)


def _t5_bucket(dist):
    n = jnp.maximum(dist, 1).astype(F32)
    large = MAX_EXACT + (jnp.log(n / MAX_EXACT) / math.log(MAX_DISTANCE / MAX_EXACT) * (N_BUCKETS - MAX_EXACT)).astype(jnp.int32)
    large = jnp.minimum(large, N_BUCKETS - 1)
    return jnp.where(dist < MAX_EXACT, dist, large)


def _band_buckets():
    a = jnp.arange(BLK)[:, None]
    b = jnp.arange(2 * BLK)[None, :]
    steps = jnp.maximum(a + BLK - b, 0)
    return jnp.stack([_t5_bucket(steps * d) for d in DILATIONS]).astype(jnp.int32)


def _pad_rows(a, rows=8):
    return jnp.pad(a, ((0, rows - a.shape[0]), (0, 0)))


def kernel(x, c, w_ada, b_ada, w_in, conv_w, conv_b, rel_bias, w_attn_out, w_conv_out, w_o, ln_g, ln_b, loss_target, m_w_ada, m_b_ada, m_w_in, m_conv_w, m_conv_b, m_rel_bias, m_w_attn_out, m_w_conv_out, m_w_o, m_ln_g, m_ln_b, v_w_ada, v_b_ada, v_w_in, v_conv_w, v_conv_b, v_rel_bias, v_w_attn_out, v_w_conv_out, v_w_o, v_ln_g, v_ln_b):
    bsz, seq, _ = x.shape
    t = bsz * seq
    mx, my, mc = _place()
    chip = 2 * mx + my
    dev = 4 * mx + 2 * my + mc
    x2 = x.reshape(t, D)
    tgt = loss_target.reshape(t, D)

    mine = [_to_bf16_window(a, w[0], f"to_bf16_{a}") for a, w in enumerate((w_in, w_attn_out, w_conv_out, w_o))]
    w_in_f, w_ao_f, w_co_f, w_o_f = _gather_weights(mine)

    n_ada = w_ada.shape[2]
    c_all = _all_gather8(c, "gather_c").reshape(8 * bsz, D)
    b_cols = lax.dynamic_slice(b_ada, (0, chip * n_ada), (1, n_ada))
    mod_part = _ada_forward(c_all, w_ada[0], b_cols)
    mod_parts = _all_gather8(mod_part, "gather_mod")
    mod_all = mod_parts[0::2].transpose(1, 0, 2).reshape(8 * bsz, 3 * D)
    mod = lax.dynamic_slice(mod_all, (dev * bsz, 0), (bsz, 3 * D))
    shift = mod[:, 0:D].reshape(bsz, 1, D)
    sc1p = 1.0 + mod[:, D:2 * D].reshape(bsz, 1, D)
    gate = mod[:, 2 * D:].reshape(bsz, 1, D)
    cw_parts = _all_gather8(_pad_rows(conv_w[0]), "gather_conv_w")
    conv_w_f = cw_parts[0::2].transpose(1, 0, 2).reshape(8, D)[0:3]

    h, ht = _modulate(x2, sc1p, shift, seq)
    q, k, v, gates = _project(h, w_in_f)
    buckets = _band_buckets()
    bias = _bias_tables(rel_bias, buckets)
    og, lg = [], []
    for g in range(3):
        o_g, l_g = _attn_forward(g, q, k, v, bias[g], bsz, seq)
        og.append(o_g)
        lg.append(l_g)
    (a_in, s_in, merged, dy, a_out, s_out, y_conv, o, lj, dxr, vec_f, dgate) = _mix_forward(
        gates, og, lg, x2, tgt, gate, w_ao_f, w_co_f, w_o_f, conv_w_f, conv_b, ln_g, ln_b, bsz, seq)

    dgates, do, dl, da_out, ds_out, vec_b = _mix_backward(gates, dy, a_out, s_out, y_conv, o, lj, w_ao_f, w_co_f, w_o_f, conv_w_f, bsz, seq)
    g_ao, g_co, g_o = _out_weight_grads(a_in, da_out, s_in, ds_out, merged, dy)
    dqkv, dbs = None, []
    for g in range(3):
        dqkv, db = _attn_backward(g, q, k, v, do, dl, bias[g], dqkv, bsz, seq)
        dbs.append(db)
    dq, dk, dv = dqkv
    drb = _bias_grad(jnp.stack(dbs), buckets)
    drb = drb[:, :, 0:4].transpose(1, 0, 2).reshape(N_BUCKETS, 12)
    g_in = None
    for n, src in enumerate((dq, dk, dv, dgates)):
        g_in = _in_weight_grad(ht, src, n * NQT, g_in, f"in_weight_grad_{n}")

    grads = [g_in, g_ao, g_co, g_o]
    got = _swap_halves(grads)
    sums = [_chip_sum(a, grads[a], got[a], f"chip_sum_{a}") for a in range(4)]
    grad_x, dshift, dscale, landed = _input_grad(dq, dk, dv, dgates, w_in_f, x2, dxr, sc1p, seq, [s[1] for s in sums])
    halves = [_reduce_mine(a, sums[a][0], landed[a], f"reduce_mine_{a}") for a in range(4)]
    gw_in, gw_ao, gw_co, gw_o = _join_halves(halves)

    dmod = jnp.concatenate([dshift, dscale, dgate], axis=2).reshape(bsz * 3, D)
    drb_row = jnp.pad(drb.reshape(1, N_BUCKETS * 12), ((0, 0), (0, D - N_BUCKETS * 12)))
    packed = jnp.concatenate([vec_f, vec_b, _pad_rows(dmod), _pad_rows(drb_row)], axis=0)
    gathered = _all_gather8(packed, "gather_small")
    small = _sum_partials(gathered)
    g_ln_g, g_ln_b, loss_lanes = small[0:1], small[1:2], small[2:3]
    g_conv_w_full, g_conv_b = small[8:11], small[11:12]
    g_rel_bias = small[24, 0:N_BUCKETS * 12].reshape(N_BUCKETS, 12)
    loss = 0.5 / D * jnp.sum(loss_lanes)
    dmod_all = gathered[:, 16:16 + 3 * bsz, :].reshape(8 * bsz, 3 * D)
    dmod_cols = lax.dynamic_slice(dmod_all, (0, chip * n_ada), (8 * bsz, n_ada))
    gw_ada, gb_ada = _ada_backward(c_all, dmod_cols, dmod_all)
    n_cw = conv_w.shape[2]
    g_conv_w = lax.dynamic_slice(g_conv_w_full, (0, chip * n_cw), (3, n_cw))

    names = ["w_ada", "b_ada", "w_in", "conv_w", "conv_b", "rel_bias", "w_attn_out", "w_conv_out", "w_o", "ln_g", "ln_b"]
    two_d = lambda a: a.reshape(a.shape[-2:]) if a.ndim == 3 else a
    weights = dict(zip(names, map(two_d, (w_ada, b_ada, w_in, conv_w, conv_b, rel_bias, w_attn_out, w_conv_out, w_o, ln_g, ln_b))))
    ms = dict(zip(names, map(two_d, (m_w_ada, m_b_ada, m_w_in, m_conv_w, m_conv_b, m_rel_bias, m_w_attn_out, m_w_conv_out, m_w_o, m_ln_g, m_ln_b))))
    vs = dict(zip(names, map(two_d, (v_w_ada, v_b_ada, v_w_in, v_conv_w, v_conv_b, v_rel_bias, v_w_attn_out, v_w_conv_out, v_w_o, v_ln_g, v_ln_b))))
    grads = dict(zip(names, (gw_ada, gb_ada, gw_in, g_conv_w, g_conv_b, g_rel_bias, gw_ao, gw_co, gw_o, g_ln_g, g_ln_b)))
    shapes = dict(zip(names, (w_ada, b_ada, w_in, conv_w, conv_b, rel_bias, w_attn_out, w_conv_out, w_o, ln_g, ln_b)))
    deltas, new_m, new_v = {}, {}, {}
    for n in names:
        deltas[n], new_m[n], new_v[n] = _adamw(weights[n], grads[n], ms[n], vs[n], f"adamw_{n}")
    shaped = lambda d: [d[n].reshape(shapes[n].shape) for n in names]
    return (loss, grad_x.reshape(bsz, seq, D), *shaped(grads), *shaped(deltas), *shaped(new_m), *shaped(new_v))
```

```python
import math

import jax
import jax.numpy as jnp
from jax import lax
from jax.experimental import pallas as pl
from jax.experimental.pallas import tpu as pltpu

F32 = jnp.float32
BF16 = jnp.bfloat16
SDS = jax.ShapeDtypeStruct
MESH = pl.DeviceIdType.MESH
ANY = pl.BlockSpec(memory_space=pl.ANY)
VMEM_SPEC = pl.BlockSpec(memory_space=pltpu.VMEM)

D = 1024
HD = 128
BLK = 128
QW = 1536
AW = 512
NGATE = 6656
GATE_COLS = ((0, 512), (512, 1536), (1536, 2560), (2560, 3584), (3584, 4608), (4608, 5632), (5632, 6656))
NCOL = 3 * QW + NGATE
TN = 512
NQT = QW // TN
NPT = NCOL // TN
DILATIONS = (1, 4, 16)
N_BUCKETS, MAX_EXACT, MAX_DISTANCE = 32, 16, 2048
ALPHA = 2.0 ** 0.25
LN_EPS = 1e-5
NEG = -1e30
SCALE = HD ** -0.5
LR, B1, B2, EPS, WD, STEP = 0.001, 0.9, 0.999, 1e-08, 0.01, 10
NCHIP = 4
VMEM_CAP = 60 * 2 ** 20


def _cp(sem=None, vmem=None, side=False):
    return pltpu.CompilerParams(dimension_semantics=sem, vmem_limit_bytes=vmem, has_side_effects=side)


def _dot(a, b):
    return jnp.dot(a, b, preferred_element_type=F32)


def _dot_nt(a, b):
    return lax.dot_general(a, b, (((1,), (1,)), ((), ())), preferred_element_type=F32)


def _dot_tn(a, b):
    return lax.dot_general(a, b, (((0,), (0,)), ((), ())), preferred_element_type=F32)


def _sig(x):
    return 1.0 / (1.0 + jnp.exp(-x))


def _place():
    x, y, c = lax.axis_index("x"), lax.axis_index("y"), lax.axis_index("c")
    return x, y, c


def _all_gather8(v, name):
    r, cdim = v.shape

    def body(v_ref, out_ref, send_sems, recv_sems, local_sem):
        x, y, c = _place()
        me = 4 * x + 2 * y + c
        peers = [(x, y, 1 - c), (1 - x, y, c), (x, 1 - y, c), (1 - x, 1 - y, c),
                 (1 - x, y, 1 - c), (x, 1 - y, 1 - c), (1 - x, 1 - y, 1 - c)]
        mine = pltpu.make_async_copy(v_ref, out_ref.at[me], local_sem)
        mine.start()

        def copy(k, block, to):
            return pltpu.make_async_remote_copy(src_ref=v_ref, dst_ref=out_ref.at[block], send_sem=send_sems.at[k],
                                                recv_sem=recv_sems.at[k], device_id=to, device_id_type=MESH)

        sends = [copy(k, me, p) for k, p in enumerate(peers)]
        for cp in sends:
            cp.start()
        for k, (px, py, pc) in enumerate(peers):
            copy(k, 4 * px + 2 * py + pc, (px, py, pc)).wait_recv()
        for cp in sends:
            cp.wait_send()
        mine.wait()

    return pl.pallas_call(
        body, name=name, out_shape=SDS((8, r, cdim), v.dtype), in_specs=[VMEM_SPEC], out_specs=VMEM_SPEC,
        scratch_shapes=[pltpu.SemaphoreType.DMA((7,)), pltpu.SemaphoreType.DMA((7,)), pltpu.SemaphoreType.DMA(())],
        compiler_params=_cp(side=True),
    )(v)


W_CUTS = (("col", D, NCOL // NCHIP), ("col", AW, D // NCHIP), ("row", D // NCHIP, D), ("row", D // NCHIP, D))
W_FULL = ((D, NCOL), (AW, D), (D, D), (D, D))


def _shard_window(ref, cut, k, half):
    kind, nr, nc = cut
    hr = nr // 2
    if kind == "col":
        rows = pl.ds(0, nr) if half is None else pl.ds(pl.multiple_of(half * hr, 16), hr)
        return ref.at[rows, pl.ds(pl.multiple_of(k * nc, 128), nc)]
    if half is None:
        return ref.at[pl.ds(pl.multiple_of(k * nr, 16), nr), :]
    return ref.at[pl.ds(pl.multiple_of(k * nr + half * hr, 16), hr), :]


def _half_rows(ref, cut, half):
    hr = cut[1] // 2
    return ref.at[pl.ds(pl.multiple_of(half * hr, 16), hr), :]


def _to_bf16_window(a, w, name):
    kind, nr, nc = W_CUTS[a]
    x, y, _ = _place()
    chip = jnp.reshape(2 * x + y, (1,)).astype(jnp.int32)
    tr = min(nr, 256)

    def body(c_ref, w_ref, o_ref):
        o_ref[...] = w_ref[...].astype(BF16)

    out_map = (lambda i, cr: (i, cr[0])) if kind == "col" else (lambda i, cr: (cr[0] * (nr // tr) + i, 0))
    return pl.pallas_call(
        body, name=name, out_shape=SDS(W_FULL[a], BF16),
        grid_spec=pltpu.PrefetchScalarGridSpec(num_scalar_prefetch=1, grid=(nr // tr,),
                                               in_specs=[pl.BlockSpec((tr, nc), lambda i, cr: (i, 0))], out_specs=pl.BlockSpec((tr, nc), out_map)),
        compiler_params=_cp(("arbitrary",)),
    )(chip, w)


def _gather_weights(fulls):
    n = len(fulls)

    def body(*refs):
        full = refs[n:2 * n]
        send_sems, recv_sems = refs[2 * n:]
        x, y, c = _place()
        me = 2 * x + y
        chips = [(1 - x, y), (x, 1 - y), (1 - x, 1 - y)]
        sibling = (x, y, 1 - c)

        def remote(a, k, chip, half, to):
            window = _shard_window(full[a], W_CUTS[a], chip, half)
            return pltpu.make_async_remote_copy(src_ref=window, dst_ref=window, send_sem=send_sems.at[k], recv_sem=recv_sems.at[k],
                                                device_id=to, device_id_type=MESH)

        sends = []
        for a in range(n):
            for j, (px, py) in enumerate(chips):
                cp = remote(a, 6 * a + j, me, c, (px, py, c))
                cp.start()
                sends.append(cp)
        for a in range(n):
            for j, (px, py) in enumerate(chips):
                chip = 2 * px + py
                remote(a, 6 * a + j, chip, c, (px, py, c)).wait_recv()
                cp = remote(a, 6 * a + 3 + j, chip, c, sibling)
                cp.start()
                sends.append(cp)
        for a in range(n):
            for j, (px, py) in enumerate(chips):
                remote(a, 6 * a + 3 + j, 2 * px + py, 1 - c, sibling).wait_recv()
        for cp in sends:
            cp.wait_send()

    return pl.pallas_call(
        body, name="gather_weights", out_shape=[SDS(s, BF16) for s in W_FULL], in_specs=[ANY] * n, out_specs=[ANY] * n,
        scratch_shapes=[pltpu.SemaphoreType.DMA((6 * n,)), pltpu.SemaphoreType.DMA((6 * n,))],
        input_output_aliases={a: a for a in range(n)}, compiler_params=_cp(side=True),
    )(*fulls)


def _swap_halves(grads):
    n = len(grads)
    shapes = []
    for a in range(n):
        kind, nr, nc = W_CUTS[a]
        shapes.append((W_FULL[a][0] // 2, W_FULL[a][1]) if kind == "col" else (NCHIP, nr // 2, nc))

    def pieces(a, ref, land, half):
        kind, nr, nc = W_CUTS[a]
        if kind == "col":
            hr = nr // 2
            return [(ref.at[pl.ds(pl.multiple_of(half * hr, 16), hr), :], land)]
        return [(_shard_window(ref, W_CUTS[a], k, half), land.at[k]) for k in range(NCHIP)]

    def body(*refs):
        src, land = refs[:n], refs[n:2 * n]
        send_sems, recv_sems = refs[2 * n:]
        x, y, c = _place()
        sibling = (x, y, 1 - c)
        sends = []
        k = 0
        for a in range(n):
            for s, d in pieces(a, src[a], land[a], 1 - c):
                cp = pltpu.make_async_remote_copy(src_ref=s, dst_ref=d, send_sem=send_sems.at[k], recv_sem=recv_sems.at[k],
                                                  device_id=sibling, device_id_type=MESH)
                cp.start()
                sends.append(cp)
                k += 1
        for cp in sends:
            cp.wait()

    n_sems = sum(1 if W_CUTS[a][0] == "col" else NCHIP for a in range(n))
    return pl.pallas_call(
        body, name="swap_grad_halves", out_shape=[SDS(s, F32) for s in shapes], in_specs=[ANY] * n, out_specs=[ANY] * n,
        scratch_shapes=[pltpu.SemaphoreType.DMA((n_sems,)), pltpu.SemaphoreType.DMA((n_sems,))],
        compiler_params=_cp(side=True),
    )(*grads)


def _chip_sum(a, grad, got, name):
    kind, nr, nc = W_CUTS[a]
    hr = nr // 2
    c = lax.axis_index("c")
    cidx = jnp.reshape(c, (1,)).astype(jnp.int32)

    def body(c_ref, g_ref, r_ref, f_ref, b_ref):
        s = g_ref[...] + r_ref[...]
        f_ref[...] = s.reshape(f_ref.shape)
        b_ref[...] = s.astype(BF16).reshape(b_ref.shape)

    if kind == "col":
        in_specs = [pl.BlockSpec((hr, nc), lambda k, cr: (cr[0], k)), pl.BlockSpec((hr, nc), lambda k, cr: (0, k))]
    else:
        grad = grad.reshape(NCHIP, 2, hr, nc)
        in_specs = [pl.BlockSpec((1, 1, hr, nc), lambda k, cr: (k, cr[0], 0, 0)), pl.BlockSpec((1, hr, nc), lambda k, cr: (k, 0, 0))]
    out_specs = [pl.BlockSpec((1, hr, nc), lambda k, cr: (k, 0, 0))] * 2
    return pl.pallas_call(
        body, name=name, out_shape=[SDS((NCHIP, hr, nc), F32), SDS((NCHIP, hr, nc), BF16)],
        grid_spec=pltpu.PrefetchScalarGridSpec(num_scalar_prefetch=1, grid=(NCHIP,), in_specs=in_specs, out_specs=out_specs),
        compiler_params=_cp(("arbitrary",), VMEM_CAP),
    )(cidx, grad, got)


def _reduce_mine(a, mine_f32, got, name):
    kind, nr, nc = W_CUTS[a]
    hr = nr // 2
    x, y, c = _place()
    where = jnp.stack([2 * x + y, c]).astype(jnp.int32)
    tr = min(hr, 256)

    def body(w_ref, m_ref, g_ref, o_ref):
        o_ref[...] = ((m_ref[0] + g_ref[0].astype(F32)) + g_ref[1].astype(F32)) + g_ref[2].astype(F32)

    return pl.pallas_call(
        body, name=name, out_shape=SDS((nr, nc), F32),
        grid_spec=pltpu.PrefetchScalarGridSpec(
            num_scalar_prefetch=1, grid=(hr // tr,),
            in_specs=[pl.BlockSpec((1, tr, nc), lambda i, wr: (wr[0], i, 0)), pl.BlockSpec((3, tr, nc), lambda i, wr: (0, i, 0))],
            out_specs=pl.BlockSpec((tr, nc), lambda i, wr: (wr[1] * (hr // tr) + i, 0))),
        compiler_params=_cp(("arbitrary",), VMEM_CAP),
    )(where, mine_f32, got)


def _join_halves(fulls):
    n = len(fulls)

    def body(*refs):
        full = refs[n:2 * n]
        send_sems, recv_sems = refs[2 * n:]
        x, y, c = _place()
        sibling = (x, y, 1 - c)

        def swap(a, half):
            rows = _half_rows(full[a], W_CUTS[a], half)
            return pltpu.make_async_remote_copy(src_ref=rows, dst_ref=rows, send_sem=send_sems.at[a], recv_sem=recv_sems.at[a],
                                                device_id=sibling, device_id_type=MESH)

        sends = [swap(a, c) for a in range(n)]
        for cp in sends:
            cp.start()
        for a, cp in enumerate(sends):
            cp.wait_send()
            swap(a, 1 - c).wait_recv()

    return pl.pallas_call(
        body, name="join_grad_halves", out_shape=[SDS((W_CUTS[a][1], W_CUTS[a][2]), F32) for a in range(n)],
        in_specs=[ANY] * n, out_specs=[ANY] * n,
        scratch_shapes=[pltpu.SemaphoreType.DMA((n,)), pltpu.SemaphoreType.DMA((n,))],
        input_output_aliases={a: a for a in range(n)}, compiler_params=_cp(side=True),
    )(*fulls)


def _ada_forward(c_all, w_ada, b_cols):
    nb, nc = c_all.shape[0], w_ada.shape[1]

    def body(c_ref, w_ref, b_ref, o_ref):
        cv = c_ref[...]
        sc = (cv * _sig(cv)).astype(BF16)
        o_ref[...] = _dot(sc, w_ref[...].astype(BF16)) + b_ref[...]

    return pl.pallas_call(body, name="ada_forward", out_shape=SDS((nb, nc), F32), compiler_params=_cp(vmem=VMEM_CAP // 2))(c_all, w_ada, b_cols)


def _ada_backward(c_all, dmod_cols, dmod_all):
    nb, nc = dmod_cols.shape

    def body(c_ref, d_ref, a_ref, gw_ref, gb_ref):
        cv = c_ref[...]
        sc = (cv * _sig(cv)).astype(BF16)
        gw_ref[...] = _dot_tn(sc, d_ref[...].astype(BF16))
        gb_ref[...] = jnp.sum(a_ref[...], axis=0, keepdims=True)

    return pl.pallas_call(body, name="ada_backward", out_shape=[SDS((D, nc), F32), SDS((1, dmod_all.shape[1]), F32)],
                          compiler_params=_cp(vmem=VMEM_CAP // 2))(c_all, dmod_cols, dmod_all)


def _modulate(x2, sc1p, shift, seq, tm=256):
    t = x2.shape[0]
    spt = seq // tm

    def body(x_ref, sc_ref, sh_ref, h_ref, ht_ref):
        h = x_ref[...] * sc_ref[0] + sh_ref[0]
        h_ref[...] = h.astype(BF16)
        ht_ref[...] = h.T.astype(BF16)

    per_seq = pl.BlockSpec((1, 1, D), lambda i: (i // spt, 0, 0))
    return pl.pallas_call(
        body, name="modulate", out_shape=[SDS((t, D), BF16), SDS((D, t), BF16)], grid=(t // tm,),
        in_specs=[pl.BlockSpec((tm, D), lambda i: (i, 0)), per_seq, per_seq],
        out_specs=[pl.BlockSpec((tm, D), lambda i: (i, 0)), pl.BlockSpec((D, tm), lambda i: (0, i))],
        compiler_params=_cp(("parallel",)),
    )(x2, sc1p, shift)


def _project(h, w, tm=1024):
    t = h.shape[0]

    def body(h_ref, w_ref, q_ref, k_ref, v_ref, g_ref):
        j = pl.program_id(1)
        acc = _dot(h_ref[...], w_ref[...])
        for n, ref in enumerate((q_ref, k_ref, v_ref)):
            @pl.when((j >= n * NQT) & (j < (n + 1) * NQT))
            def _(ref=ref):
                ref[...] = acc

        @pl.when(j >= 3 * NQT)
        def _():
            g_ref[...] = acc.astype(BF16)

    def part(n):
        return pl.BlockSpec((tm, TN), lambda i, j: (i, jnp.clip(j - n * NQT, 0, NQT - 1)))

    return pl.pallas_call(
        body, name="project", out_shape=[SDS((t, QW), F32)] * 3 + [SDS((t, NGATE), BF16)], grid=(t // tm, NPT),
        in_specs=[pl.BlockSpec((tm, D), lambda i, j: (i, 0)), pl.BlockSpec((D, TN), lambda i, j: (0, j))],
        out_specs=[part(0), part(1), part(2), pl.BlockSpec((tm, TN), lambda i, j: (i, jnp.maximum(j - 3 * NQT, 0)))],
        compiler_params=_cp(("arbitrary", "arbitrary"), VMEM_CAP // 2),
    )(h, w)


def _bias_tables(rel_bias, buckets):
    def body(tab_ref, bk_ref, o_ref):
        a = lax.broadcasted_iota(jnp.int32, (BLK, 2 * BLK), 0)
        b = lax.broadcasted_iota(jnp.int32, (BLK, 2 * BLK), 1)
        steps = a + BLK - b
        valid = (steps >= 0) & (steps <= BLK)
        for g in range(3):
            bk = bk_ref[g]
            for j in range(4):
                def pick(kk, acc, bk=bk, col=4 * g + j):
                    return jnp.where(bk == kk, tab_ref[kk, col], acc)

                acc = lax.fori_loop(0, N_BUCKETS, pick, jnp.zeros((BLK, 2 * BLK), F32))
                o_ref[g, j] = jnp.where(valid, acc, NEG)

    return pl.pallas_call(
        body, name="bias_tables", out_shape=SDS((3, 4, BLK, 2 * BLK), F32),
        in_specs=[pl.BlockSpec(memory_space=pltpu.SMEM), VMEM_SPEC], out_specs=VMEM_SPEC,
    )(rel_bias, buckets)


def _bias_grad(ds_sum, buckets):
    def body(ds_ref, bk_ref, o_ref):
        lane = lax.broadcasted_iota(jnp.int32, (1, 128), 1)
        for g in range(3):
            def bucket(kk, carry, g=g):
                row = jnp.zeros((1, 128), F32)
                for j in range(4):
                    v = jnp.where(bk_ref[g] == kk, ds_ref[g, j], 0.0)
                    s = jnp.sum(jnp.sum(v, axis=1, keepdims=True), axis=0, keepdims=True)
                    row = jnp.where(lane == j, s, row)
                o_ref[g, pl.ds(kk, 1), :] = row
                return carry

            lax.fori_loop(0, N_BUCKETS, bucket, 0)

    return pl.pallas_call(body, name="bias_grad", out_shape=SDS((3, N_BUCKETS, 128), F32), in_specs=[VMEM_SPEC, VMEM_SPEC],
                          out_specs=VMEM_SPEC)(ds_sum, buckets)


def _sub_rows(d, r, n):
    return pl.ds(n * BLK * d + r, BLK) if d == 1 else pl.ds(n * BLK * d + r, BLK, stride=d)


def _attn_forward(g, q, k, v, bias, bsz, seq):
    d = DILATIONS[g]
    nblk = seq // d // BLK

    def body(q_ref, k_ref, v_ref, b_ref, o_ref, l_ref):
        hs = pl.program_id(1)
        for r in range(d):
            for n in range(nblk):
                rows = _sub_rows(d, r, n)
                qb = q_ref[rows, :].astype(BF16)
                s_c = _dot_nt(qb, k_ref[rows, :].astype(BF16)) * SCALE + b_ref[hs, :, BLK:]
                m = jnp.max(s_c, axis=1, keepdims=True)
                if n > 0:
                    prev = _sub_rows(d, r, n - 1)
                    s_p = _dot_nt(qb, k_ref[prev, :].astype(BF16)) * SCALE + b_ref[hs, :, :BLK]
                    m = jnp.maximum(m, jnp.max(s_p, axis=1, keepdims=True))
                p_c = jnp.exp(s_c - m)
                den = jnp.sum(p_c, axis=1, keepdims=True)
                acc = _dot(p_c.astype(BF16), v_ref[rows, :].astype(BF16))
                if n > 0:
                    p_p = jnp.exp(s_p - m)
                    den = den + jnp.sum(p_p, axis=1, keepdims=True)
                    acc = acc + _dot(p_p.astype(BF16), v_ref[prev, :].astype(BF16))
                o_ref[rows, :] = acc / den
                l_ref[rows, :] = jnp.broadcast_to(m + jnp.log(den), (BLK, HD))

    qkv_spec = pl.BlockSpec((seq, HD), lambda b, hh: (b, 4 * g + hh))
    out_spec = pl.BlockSpec((seq, HD), lambda b, hh: (b, hh))
    return pl.pallas_call(
        body, name=f"attn_forward_{g}", out_shape=[SDS((bsz * seq, AW), F32)] * 2, grid=(bsz, 4),
        in_specs=[qkv_spec, qkv_spec, qkv_spec, pl.BlockSpec((4, BLK, 2 * BLK), lambda b, hh: (0, 0, 0))],
        out_specs=[out_spec, out_spec],
        compiler_params=_cp(("parallel", "parallel"), VMEM_CAP // 2),
    )(q, k, v, bias)


def _attn_backward(g, q, k, v, do, dl, bias, prev_out, bsz, seq):
    d = DILATIONS[g]
    nblk = seq // d // BLK

    def body(q_ref, k_ref, v_ref, do_ref, dl_ref, b_ref, *rest):
        dq_ref, dk_ref, dv_ref, db_ref = rest[-4:]
        hs = pl.program_id(1)

        @pl.when((pl.program_id(0) == 0) & (hs == 0))
        def _():
            db_ref[...] = jnp.zeros_like(db_ref)

        dk_ref[...] = jnp.zeros_like(dk_ref)
        dv_ref[...] = jnp.zeros_like(dv_ref)
        for r in range(d):
            for n in range(nblk):
                rows = _sub_rows(d, r, n)
                qb = q_ref[rows, :].astype(BF16)
                dob = do_ref[rows, :].astype(BF16)
                both = dl_ref[rows, :]
                lse, delta = both[:, 0:1], both[:, 64:65]
                dq = jnp.zeros((BLK, HD), F32)
                parts = [(rows, slice(BLK, 2 * BLK))]
                if n > 0:
                    parts.append((_sub_rows(d, r, n - 1), slice(0, BLK)))
                for keys, band in parts:
                    kb, vb = k_ref[keys, :].astype(BF16), v_ref[keys, :].astype(BF16)
                    p = jnp.exp(_dot_nt(qb, kb) * SCALE + b_ref[hs, :, band] - lse)
                    ds = p * (_dot_nt(dob, vb) - delta)
                    dsb = ds.astype(BF16)
                    dv_ref[keys, :] += _dot_tn(p.astype(BF16), dob)
                    dk_ref[keys, :] += _dot_tn(dsb, qb) * SCALE
                    dq = dq + _dot(dsb, kb) * SCALE
                    db_ref[hs, :, band] += ds
                dq_ref[rows, :] = dq

    qkv_spec = pl.BlockSpec((seq, HD), lambda b, hh: (b, 4 * g + hh))
    out_spec = pl.BlockSpec((seq, HD), lambda b, hh: (b, hh))
    band_spec = pl.BlockSpec((4, BLK, 2 * BLK), lambda b, hh: (0, 0, 0))
    ins = [q, k, v, do, dl, bias]
    in_specs = [qkv_spec, qkv_spec, qkv_spec, out_spec, out_spec, band_spec]
    aliases = {}
    if prev_out is not None:
        ins += list(prev_out)
        in_specs += [ANY] * 3
        aliases = {6: 0, 7: 1, 8: 2}
    dq, dk, dv, db = pl.pallas_call(
        body, name=f"attn_backward_{g}", out_shape=[SDS((bsz * seq, QW), F32)] * 3 + [SDS((4, BLK, 2 * BLK), F32)], grid=(bsz, 4),
        in_specs=in_specs, out_specs=[qkv_spec] * 3 + [band_spec], input_output_aliases=aliases,
        compiler_params=_cp(("arbitrary", "arbitrary"), VMEM_CAP // 2),
    )(*ins)
    return (dq, dk, dv), db


def _mix_forward(gates, og, lg, x2, tgt, gate, w_ao, w_co, w_o, conv_w, conv_b, ln_g, ln_b, bsz, seq, tm=256):
    t = x2.shape[0]
    spt = seq // tm

    def body(g_ref, o1, o2, o3, l1, l2, l3, x_ref, t_ref, gate_ref, wao_ref, wco_ref, wo_ref, cw_ref, cb_ref, lng_ref, lnb_ref,
             ain_ref, sin_ref, mrg_ref, dy_ref, aout_ref, sout_ref, yc_ref, o_ref, lj_ref, dxr_ref, vec_ref, dgate_ref, zc_ref):
        b, i = pl.program_id(0), pl.program_id(1)

        @pl.when((b == 0) & (i == 0))
        def _():
            vec_ref[...] = jnp.zeros_like(vec_ref)

        @pl.when(i == 0)
        def _():
            zc_ref[...] = jnp.zeros_like(zc_ref)
            dgate_ref[...] = jnp.zeros_like(dgate_ref)

        g_attn, u, bg, cg, g_conv, m_attn, m_conv = (g_ref[:, lo:hi].astype(F32) for lo, hi in GATE_COLS)
        la, lb, lc = l1[...], l2[...], l3[...]
        mx = jnp.maximum(la, jnp.maximum(lb, lc))
        ea, eb, ec = jnp.exp(la - mx), jnp.exp(lb - mx), jnp.exp(lc - mx)
        den = ea + eb + ec
        o = (ea * o1[...] + eb * o2[...] + ec * o3[...]) / den
        o_ref[...] = o
        lj_ref[...] = mx + jnp.log(den)
        a_in = o * (g_attn * _sig(g_attn))
        ain_ref[...] = a_in.astype(BF16)
        a_out = _dot(a_in.astype(BF16), wao_ref[...])
        aout_ref[...] = a_out.astype(BF16)
        z = cg * u
        rows = lax.broadcasted_iota(jnp.int32, (tm, D), 0)
        c6, c7 = zc_ref[6:7, :], zc_ref[7:8, :]
        z1 = jnp.where(rows == 0, c7, pltpu.roll(z, 1, 0))
        z2 = jnp.where(rows == 0, c6, jnp.where(rows == 1, c7, pltpu.roll(z, 2, 0)))
        zc_ref[...] = z[tm - 8:tm, :]
        y_conv = (cw_ref[0:1, :] * z2 + cw_ref[1:2, :] * z1 + cw_ref[2:3, :] * z) + cb_ref[...]
        yc_ref[...] = y_conv.astype(BF16)
        s_in = bg * y_conv * (g_conv * _sig(g_conv))
        sin_ref[...] = s_in.astype(BF16)
        s_out = _dot(s_in.astype(BF16), wco_ref[...])
        sout_ref[...] = s_out.astype(BF16)
        merged = _sig(m_attn) * a_out + _sig(m_conv) * s_out
        mrg_ref[...] = merged.astype(BF16)
        y = _dot(merged.astype(BF16), wo_ref[...])
        gate1 = 1.0 + gate_ref[0]
        r = ALPHA * x_ref[...] + gate1 * y
        mu = jnp.mean(r, axis=1, keepdims=True)
        rc = r - mu
        rstd = lax.rsqrt(jnp.mean(rc * rc, axis=1, keepdims=True) + LN_EPS)
        xhat = rc * rstd
        diff = (xhat * lng_ref[...] + lnb_ref[...]) - t_ref[...]
        dout = diff * (1.0 / D)
        vec_ref[0:1, :] += jnp.sum(dout * xhat, axis=0, keepdims=True)
        vec_ref[1:2, :] += jnp.sum(dout, axis=0, keepdims=True)
        vec_ref[2:3, :] += jnp.sum(diff * diff, axis=0, keepdims=True)
        dxh = dout * lng_ref[...]
        dr = rstd * (dxh - jnp.mean(dxh, axis=1, keepdims=True) - xhat * jnp.mean(dxh * xhat, axis=1, keepdims=True))
        dxr_ref[...] = ALPHA * dr
        dy_ref[...] = (dr * gate1).astype(BF16)
        dgate_ref[0] += jnp.sum(dr * y, axis=0, keepdims=True)

    tok = lambda w: pl.BlockSpec((tm, w), lambda b, i: (b * spt + i, 0))
    const = lambda s: pl.BlockSpec(s, lambda b, i: (0,) * len(s))
    per_seq = pl.BlockSpec((1, 1, D), lambda b, i: (b, 0, 0))
    outs = pl.pallas_call(
        body, name="mix_forward", grid=(bsz, spt),
        out_shape=[SDS((t, AW), BF16), SDS((t, D), BF16), SDS((t, D), BF16), SDS((t, D), BF16), SDS((t, D), BF16), SDS((t, D), BF16),
                   SDS((t, D), BF16), SDS((t, AW), F32), SDS((t, AW), F32), SDS((t, D), F32), SDS((8, D), F32), SDS((bsz, 1, D), F32)],
        in_specs=[tok(NGATE)] + [tok(AW)] * 6 + [tok(D), tok(D), per_seq, const((AW, D)), const((D, D)), const((D, D)),
                                                 const((3, D)), const((1, D)), const((1, D)), const((1, D))],
        out_specs=[tok(AW), tok(D), tok(D), tok(D), tok(D), tok(D), tok(D), tok(AW), tok(AW), tok(D), const((8, D)), per_seq],
        scratch_shapes=[pltpu.VMEM((8, D), F32)],
        compiler_params=_cp(("arbitrary", "arbitrary"), VMEM_CAP),
    )(gates, *og, *lg, x2, tgt, gate, w_ao, w_co, w_o, conv_w, conv_b, ln_g, ln_b)
    return outs


def _mix_backward(gates, dy, a_out, s_out, y_conv, o, lj, w_ao, w_co, w_o, conv_w, bsz, seq, tm=256):
    t = dy.shape[0]
    spt = seq // tm

    def body(g_ref, dy_ref, aout_ref, sout_ref, yc_ref, o_ref, lj_ref, wao_ref, wco_ref, wo_ref, cw_ref,
             dg_ref, do_ref, dl_ref, daout_ref, dsout_ref, vec_ref, car_ref):
        b, i = pl.program_id(0), pl.program_id(1)

        @pl.when((b == 0) & (i == 0))
        def _():
            vec_ref[...] = jnp.zeros_like(vec_ref)

        @pl.when(i == 0)
        def _():
            car_ref[...] = jnp.zeros_like(car_ref)

        g_attn, u, bg, cg, g_conv, m_attn, m_conv = (g_ref[:, lo:hi].astype(F32) for lo, hi in GATE_COLS)
        dmerged = _dot_nt(dy_ref[...], wo_ref[...])
        sa, sc = _sig(m_attn), _sig(m_conv)
        da_out = (dmerged * sa).astype(BF16)
        ds_out = (dmerged * sc).astype(BF16)
        daout_ref[...] = da_out
        dsout_ref[...] = ds_out
        dg_ref[:, 4608:5632] = (dmerged * aout_ref[...].astype(F32) * (sa * (1.0 - sa))).astype(BF16)
        dg_ref[:, 5632:6656] = (dmerged * sout_ref[...].astype(F32) * (sc * (1.0 - sc))).astype(BF16)
        da_in = _dot_nt(da_out, wao_ref[...])
        ds_in = _dot_nt(ds_out, wco_ref[...])
        sga = _sig(g_attn)
        o = o_ref[...]
        do = da_in * (g_attn * sga)
        do_ref[...] = do
        dg_ref[:, 0:512] = (da_in * o * (sga * (1.0 + g_attn * (1.0 - sga)))).astype(BF16)
        prod = do * o
        lane = lax.broadcasted_iota(jnp.int32, (tm, HD), 1)
        for j in range(4):
            cs = slice(j * HD, (j + 1) * HD)
            delta = jnp.sum(prod[:, cs], axis=1, keepdims=True)
            dl_ref[:, cs] = jnp.where(lane < 64, lj_ref[:, cs], delta)
        sgc = _sig(g_conv)
        silu_c = g_conv * sgc
        yc = yc_ref[...].astype(F32)
        dg_ref[:, 1536:2560] = (ds_in * yc * silu_c).astype(BF16)
        dg_ref[:, 3584:4608] = (ds_in * bg * yc * (sgc * (1.0 + g_conv * (1.0 - sgc)))).astype(BF16)
        dyc = ds_in * bg * silu_c
        rows = lax.broadcasted_iota(jnp.int32, (tm, D), 0)
        c0, c1 = car_ref[0:1, :], car_ref[1:2, :]
        n1 = jnp.where(rows == tm - 1, c0, pltpu.roll(dyc, tm - 1, 0))
        n2 = jnp.where(rows == tm - 2, c0, jnp.where(rows == tm - 1, c1, pltpu.roll(dyc, tm - 2, 0)))
        car_ref[...] = dyc[0:8, :]
        dz = cw_ref[2:3, :] * dyc + cw_ref[1:2, :] * n1 + cw_ref[0:1, :] * n2
        z = cg * u
        dg_ref[:, 512:1536] = (dz * cg).astype(BF16)
        dg_ref[:, 2560:3584] = (dz * u).astype(BF16)
        vec_ref[0:1, :] += jnp.sum(n2 * z, axis=0, keepdims=True)
        vec_ref[1:2, :] += jnp.sum(n1 * z, axis=0, keepdims=True)
        vec_ref[2:3, :] += jnp.sum(dyc * z, axis=0, keepdims=True)
        vec_ref[3:4, :] += jnp.sum(dyc, axis=0, keepdims=True)

    tok = lambda w: pl.BlockSpec((tm, w), lambda b, i: (b * spt + (spt - 1 - i), 0))
    const = lambda s: pl.BlockSpec(s, lambda b, i: (0,) * len(s))
    return pl.pallas_call(
        body, name="mix_backward", grid=(bsz, spt),
        out_shape=[SDS((t, NGATE), BF16), SDS((t, AW), F32), SDS((t, AW), F32), SDS((t, D), BF16), SDS((t, D), BF16), SDS((8, D), F32)],
        in_specs=[tok(NGATE), tok(D), tok(D), tok(D), tok(D), tok(AW), tok(AW), const((AW, D)), const((D, D)), const((D, D)), const((3, D))],
        out_specs=[tok(NGATE), tok(AW), tok(AW), tok(D), tok(D), const((8, D))],
        scratch_shapes=[pltpu.VMEM((8, D), F32)],
        compiler_params=_cp(("arbitrary", "arbitrary"), VMEM_CAP),
    )(gates, dy, a_out, s_out, y_conv, o, lj, w_ao, w_co, w_o, conv_w)


def _out_weight_grads(a_in, da_out, s_in, ds_out, merged, dy, tk=512):
    t = dy.shape[0]

    def body(ain_ref, da_ref, sin_ref, ds_ref, m_ref, dy_ref, gao_ref, gco_ref, go_ref):
        @pl.when(pl.program_id(0) == 0)
        def _():
            gao_ref[...] = jnp.zeros_like(gao_ref)
            gco_ref[...] = jnp.zeros_like(gco_ref)
            go_ref[...] = jnp.zeros_like(go_ref)

        gao_ref[...] += _dot_tn(ain_ref[...], da_ref[...])
        gco_ref[...] += _dot_tn(sin_ref[...], ds_ref[...])
        go_ref[...] += _dot_tn(m_ref[...], dy_ref[...])

    tok = lambda w: pl.BlockSpec((tk, w), lambda i: (i, 0))
    const = lambda s: pl.BlockSpec(s, lambda i: (0, 0))
    return pl.pallas_call(
        body, name="out_weight_grads", grid=(t // tk,), out_shape=[SDS((AW, D), F32), SDS((D, D), F32), SDS((D, D), F32)],
        in_specs=[tok(AW), tok(D), tok(D), tok(D), tok(D), tok(D)], out_specs=[const((AW, D)), const((D, D)), const((D, D))],
        compiler_params=_cp(("arbitrary",), VMEM_CAP),
    )(a_in, da_out, s_in, ds_out, merged, dy)


def _input_grad(dq, dk, dv, dgates, w, x2, dxr, sc1p, seq, sums, tm=1024):
    t = x2.shape[0]
    spt = seq // tm
    bsz = t // seq
    n = len(sums)

    def body(dq_ref, dk_ref, dv_ref, dg_ref, w_ref, x_ref, dxr_ref, sc_ref, *rest):
        src, (dx_ref, dsh_ref, dsc_ref), land = rest[:n], rest[n:n + 3], rest[n + 3:2 * n + 3]
        acc_ref, send_sems, recv_sems = rest[2 * n + 3:]
        i, j = pl.program_id(0), pl.program_id(1)
        px, py, pc = _place()
        chips = [(1 - px, py), (px, 1 - py), (1 - px, 1 - py)]
        copies = [pltpu.make_async_remote_copy(src_ref=src[a].at[2 * cx + cy], dst_ref=land[a].at[r], send_sem=send_sems.at[3 * a + r],
                                               recv_sem=recv_sems.at[3 * a + r], device_id=(cx, cy, pc), device_id_type=MESH)
                  for a in range(n) for r, (cx, cy) in enumerate(chips)]

        @pl.when((i == 0) & (j == 0))
        def _():
            for cp in copies:
                cp.start()

        @pl.when(j == 0)
        def _():
            acc_ref[...] = jnp.zeros_like(acc_ref)

        for k, ref in enumerate((dq_ref, dk_ref, dv_ref)):
            @pl.when((j >= k * NQT) & (j < (k + 1) * NQT))
            def _(ref=ref):
                acc_ref[...] += _dot_nt(ref[...].astype(BF16), w_ref[...])

        @pl.when(j >= 3 * NQT)
        def _():
            acc_ref[...] += _dot_nt(dg_ref[...], w_ref[...])

        @pl.when(j == NPT - 1)
        def _():
            dh = acc_ref[...]
            dx_ref[...] = dh * sc_ref[0] + dxr_ref[...]

            @pl.when(i % spt == 0)
            def _():
                dsh_ref[...] = jnp.zeros_like(dsh_ref)
                dsc_ref[...] = jnp.zeros_like(dsc_ref)

            dsh_ref[0] += jnp.sum(dh, axis=0, keepdims=True)
            dsc_ref[0] += jnp.sum(dh * x_ref[...], axis=0, keepdims=True)

        @pl.when((i == t // tm - 1) & (j == NPT - 1))
        def _():
            for cp in copies:
                cp.wait()

    def part(k):
        return pl.BlockSpec((tm, TN), lambda i, j: (i, jnp.clip(j - k * NQT, 0, NQT - 1)))

    row = pl.BlockSpec((tm, D), lambda i, j: (i, 0))
    per_seq = pl.BlockSpec((1, 1, D), lambda i, j: (i // spt, 0, 0))
    outs = pl.pallas_call(
        body, name="input_grad", grid=(t // tm, NPT),
        out_shape=[SDS((t, D), F32), SDS((bsz, 1, D), F32), SDS((bsz, 1, D), F32)] + [SDS((3,) + s.shape[1:], BF16) for s in sums],
        in_specs=[part(0), part(1), part(2), pl.BlockSpec((tm, TN), lambda i, j: (i, jnp.maximum(j - 3 * NQT, 0))),
                  pl.BlockSpec((D, TN), lambda i, j: (0, j)), row, row, per_seq] + [ANY] * n,
        out_specs=[row, per_seq, per_seq] + [ANY] * n,
        scratch_shapes=[pltpu.VMEM((tm, D), F32), pltpu.SemaphoreType.DMA((3 * NCHIP,)), pltpu.SemaphoreType.DMA((3 * NCHIP,))],
        compiler_params=_cp(("arbitrary", "arbitrary"), VMEM_CAP, side=True),
    )(dq, dk, dv, dgates, w, x2, dxr, sc1p, *sums)
    return outs[0], outs[1], outs[2], outs[3:]


def _in_weight_grad(ht, src, col0, prev, name, tk=1024):
    t = ht.shape[1]
    ncols = src.shape[1] // TN

    def body(ht_ref, s_ref, *rest):
        o_ref = rest[-1]

        @pl.when(pl.program_id(1) == 0)
        def _():
            o_ref[...] = jnp.zeros_like(o_ref)

        o_ref[...] += _dot(ht_ref[...], s_ref[...].astype(BF16))

    ins = [ht, src]
    in_specs = [pl.BlockSpec((D, tk), lambda j, i: (0, i)), pl.BlockSpec((tk, TN), lambda j, i: (i, j))]
    aliases = {}
    if prev is not None:
        ins.append(prev)
        in_specs.append(ANY)
        aliases = {2: 0}
    return pl.pallas_call(
        body, name=name, grid=(ncols, t // tk), out_shape=SDS((D, NCOL), F32), in_specs=in_specs,
        out_specs=pl.BlockSpec((D, TN), lambda j, i: (0, col0 + j)), input_output_aliases=aliases,
        compiler_params=_cp(("arbitrary", "arbitrary"), VMEM_CAP // 2),
    )(*ins)


def _sum_partials(gathered):
    def body(g_ref, o_ref):
        acc = g_ref[0]
        for k in range(1, 8):
            acc = acc + g_ref[k]
        o_ref[...] = acc

    return pl.pallas_call(body, name="sum_partials", out_shape=SDS(gathered.shape[1:], F32), in_specs=[VMEM_SPEC], out_specs=VMEM_SPEC)(gathered)


def _adamw(w, g, m, v, name, tr=256):
    r, cdim = w.shape
    tr = tr if cdim <= D else tr // 2
    tr = tr if (r % tr == 0 and r > tr) else r

    def body(w_ref, g_ref, m_ref, v_ref, d_ref, nm_ref, nv_ref):
        gv = g_ref[...]
        nm = B1 * m_ref[...] + (1.0 - B1) * gv
        nv = B2 * v_ref[...] + (1.0 - B2) * (gv * gv)
        m_hat = nm / (1.0 - B1 ** STEP)
        v_hat = nv / (1.0 - B2 ** STEP)
        d_ref[...] = -LR * (m_hat / (jnp.sqrt(v_hat) + EPS) + WD * w_ref[...])
        nm_ref[...] = nm
        nv_ref[...] = nv

    spec = pl.BlockSpec((tr, cdim), lambda i: (i, 0))
    return pl.pallas_call(
        body, name=name, grid=(r // tr,), out_shape=[SDS((r, cdim), F32)] * 3, in_specs=[spec] * 4, out_specs=[spec] * 3,
        compiler_params=_cp(("parallel",), VMEM_CAP // 2),
    )(w, g, m, v)


def _t5_bucket(dist):
    n = jnp.maximum(dist, 1).astype(F32)
    large = MAX_EXACT + (jnp.log(n / MAX_EXACT) / math.log(MAX_DISTANCE / MAX_EXACT) * (N_BUCKETS - MAX_EXACT)).astype(jnp.int32)
    large = jnp.minimum(large, N_BUCKETS - 1)
    return jnp.where(dist < MAX_EXACT, dist, large)


def _band_buckets():
    a = jnp.arange(BLK)[:, None]
    b = jnp.arange(2 * BLK)[None, :]
    steps = jnp.maximum(a + BLK - b, 0)
    return jnp.stack([_t5_bucket(steps * d) for d in DILATIONS]).astype(jnp.int32)


def _pad_rows(a, rows=8):
    return jnp.pad(a, ((0, rows - a.shape[0]), (0, 0)))


def kernel(x, c, w_ada, b_ada, w_in, conv_w, conv_b, rel_bias, w_attn_out, w_conv_out, w_o, ln_g, ln_b, loss_target, m_w_ada, m_b_ada, m_w_in, m_conv_w, m_conv_b, m_rel_bias, m_w_attn_out, m_w_conv_out, m_w_o, m_ln_g, m_ln_b, v_w_ada, v_b_ada, v_w_in, v_conv_w, v_conv_b, v_rel_bias, v_w_attn_out, v_w_conv_out, v_w_o, v_ln_g, v_ln_b):
    bsz, seq, _ = x.shape
    t = bsz * seq
    mx, my, mc = _place()
    chip = 2 * mx + my
    dev = 4 * mx + 2 * my + mc
    x2 = x.reshape(t, D)
    tgt = loss_target.reshape(t, D)

    mine = [_to_bf16_window(a, w[0], f"to_bf16_{a}") for a, w in enumerate((w_in, w_attn_out, w_conv_out, w_o))]
    w_in_f, w_ao_f, w_co_f, w_o_f = _gather_weights(mine)

    n_ada = w_ada.shape[2]
    n_cw = conv_w.shape[2]
    c_and_cw = jnp.concatenate([_pad_rows(c), jnp.pad(conv_w[0], ((0, 5), (0, D - n_cw)))], axis=0)
    firsts = _all_gather8(c_and_cw, "gather_c_conv_w")
    c_all = firsts[:, 0:bsz, :].reshape(8 * bsz, D)
    conv_w_f = firsts[0::2, 8:11, 0:n_cw].transpose(1, 0, 2).reshape(3, D)
    b_cols = lax.dynamic_slice(b_ada, (0, chip * n_ada), (1, n_ada))
    mod_part = _ada_forward(c_all, w_ada[0], b_cols)
    mod_parts = _all_gather8(mod_part, "gather_mod")
    mod_all = mod_parts[0::2].transpose(1, 0, 2).reshape(8 * bsz, 3 * D)
    mod = lax.dynamic_slice(mod_all, (dev * bsz, 0), (bsz, 3 * D))
    shift = mod[:, 0:D].reshape(bsz, 1, D)
    sc1p = 1.0 + mod[:, D:2 * D].reshape(bsz, 1, D)
    gate = mod[:, 2 * D:].reshape(bsz, 1, D)

    h, ht = _modulate(x2, sc1p, shift, seq)
    q, k, v, gates = _project(h, w_in_f)
    buckets = _band_buckets()
    bias = _bias_tables(rel_bias, buckets)
    og, lg = [], []
    for g in range(3):
        o_g, l_g = _attn_forward(g, q, k, v, bias[g], bsz, seq)
        og.append(o_g)
        lg.append(l_g)
    (a_in, s_in, merged, dy, a_out, s_out, y_conv, o, lj, dxr, vec_f, dgate) = _mix_forward(
        gates, og, lg, x2, tgt, gate, w_ao_f, w_co_f, w_o_f, conv_w_f, conv_b, ln_g, ln_b, bsz, seq)

    dgates, do, dl, da_out, ds_out, vec_b = _mix_backward(gates, dy, a_out, s_out, y_conv, o, lj, w_ao_f, w_co_f, w_o_f, conv_w_f, bsz, seq)
    g_ao, g_co, g_o = _out_weight_grads(a_in, da_out, s_in, ds_out, merged, dy)
    dqkv, dbs = None, []
    for g in range(3):
        dqkv, db = _attn_backward(g, q, k, v, do, dl, bias[g], dqkv, bsz, seq)
        dbs.append(db)
    dq, dk, dv = dqkv
    drb = _bias_grad(jnp.stack(dbs), buckets)
    drb = drb[:, :, 0:4].transpose(1, 0, 2).reshape(N_BUCKETS, 12)
    g_in = None
    for n, src in enumerate((dq, dk, dv, dgates)):
        g_in = _in_weight_grad(ht, src, n * NQT, g_in, f"in_weight_grad_{n}")

    grads = [g_in, g_ao, g_co, g_o]
    got = _swap_halves(grads)
    sums = [_chip_sum(a, grads[a], got[a], f"chip_sum_{a}") for a in range(4)]
    grad_x, dshift, dscale, landed = _input_grad(dq, dk, dv, dgates, w_in_f, x2, dxr, sc1p, seq, [s[1] for s in sums])
    halves = [_reduce_mine(a, sums[a][0], landed[a], f"reduce_mine_{a}") for a in range(4)]
    gw_in, gw_ao, gw_co, gw_o = _join_halves(halves)

    dmod = jnp.concatenate([dshift, dscale, dgate], axis=2).reshape(bsz * 3, D)
    drb_row = jnp.pad(drb.reshape(1, N_BUCKETS * 12), ((0, 0), (0, D - N_BUCKETS * 12)))
    packed = jnp.concatenate([vec_f, vec_b, _pad_rows(dmod), _pad_rows(drb_row)], axis=0)
    gathered = _all_gather8(packed, "gather_small")
    small = _sum_partials(gathered)
    g_ln_g, g_ln_b, loss_lanes = small[0:1], small[1:2], small[2:3]
    g_conv_w_full, g_conv_b = small[8:11], small[11:12]
    g_rel_bias = small[24, 0:N_BUCKETS * 12].reshape(N_BUCKETS, 12)
    loss = 0.5 / D * jnp.sum(loss_lanes)
    dmod_all = gathered[:, 16:16 + 3 * bsz, :].reshape(8 * bsz, 3 * D)
    dmod_cols = lax.dynamic_slice(dmod_all, (0, chip * n_ada), (8 * bsz, n_ada))
    gw_ada, gb_ada = _ada_backward(c_all, dmod_cols, dmod_all)
    g_conv_w = lax.dynamic_slice(g_conv_w_full, (0, chip * n_cw), (3, n_cw))

    names = ["w_ada", "b_ada", "w_in", "conv_w", "conv_b", "rel_bias", "w_attn_out", "w_conv_out", "w_o", "ln_g", "ln_b"]
    two_d = lambda a: a.reshape(a.shape[-2:]) if a.ndim == 3 else a
    weights = dict(zip(names, map(two_d, (w_ada, b_ada, w_in, conv_w, conv_b, rel_bias, w_attn_out, w_conv_out, w_o, ln_g, ln_b))))
    ms = dict(zip(names, map(two_d, (m_w_ada, m_b_ada, m_w_in, m_conv_w, m_conv_b, m_rel_bias, m_w_attn_out, m_w_conv_out, m_w_o, m_ln_g, m_ln_b))))
    vs = dict(zip(names, map(two_d, (v_w_ada, v_b_ada, v_w_in, v_conv_w, v_conv_b, v_rel_bias, v_w_attn_out, v_w_conv_out, v_w_o, v_ln_g, v_ln_b))))
    grads = dict(zip(names, (gw_ada, gb_ada, gw_in, g_conv_w, g_conv_b, g_rel_bias, gw_ao, gw_co, gw_o, g_ln_g, g_ln_b)))
    shapes = dict(zip(names, (w_ada, b_ada, w_in, conv_w, conv_b, rel_bias, w_attn_out, w_conv_out, w_o, ln_g, ln_b)))
    deltas, new_m, new_v = {}, {}, {}
    for n in names:
        deltas[n], new_m[n], new_v[n] = _adamw(weights[n], grads[n], ms[n], vs[n], f"adamw_{n}")
    shaped = lambda d: [d[n].reshape(shapes[n].shape) for n in names]
    return (loss, grad_x.reshape(bsz, seq, D), *shaped(grads), *shaped(deltas), *shaped(new_m), *shaped(new_v))
```

```python
import math

import numpy as np
import jax
import jax.numpy as jnp
from jax import lax
from jax.experimental import pallas as pl
from jax.experimental.pallas import tpu as pltpu

F32 = jnp.float32
BF16 = jnp.bfloat16
SDS = jax.ShapeDtypeStruct
MESH = pl.DeviceIdType.MESH
ANY = pl.BlockSpec(memory_space=pl.ANY)
VMEM_SPEC = pl.BlockSpec(memory_space=pltpu.VMEM)

D = 1024
HD = 128
BLK = 128
QW = 1536
AW = 512
NGATE = 6656
GATE_COLS = ((0, 512), (512, 1536), (1536, 2560), (2560, 3584), (3584, 4608), (4608, 5632), (5632, 6656))
NCOL = 3 * QW + NGATE
TN = 512
NQT = QW // TN
NPT = NCOL // TN
DILATIONS = (1, 4, 16)
N_BUCKETS, MAX_EXACT, MAX_DISTANCE = 32, 16, 2048
ALPHA = 2.0 ** 0.25
LN_EPS = 1e-5
NEG = -1e30
SCALE = HD ** -0.5
LR, B1, B2, EPS, WD, STEP = 0.001, 0.9, 0.999, 1e-08, 0.01, 10
NCHIP = 4
VMEM_CAP = 60 * 2 ** 20


def _cp(sem=None, vmem=None, side=False):
    return pltpu.CompilerParams(dimension_semantics=sem, vmem_limit_bytes=vmem, has_side_effects=side)


def _dot(a, b):
    return jnp.dot(a, b, preferred_element_type=F32)


def _dot_nt(a, b):
    return lax.dot_general(a, b, (((1,), (1,)), ((), ())), preferred_element_type=F32)


def _dot_tn(a, b):
    return lax.dot_general(a, b, (((0,), (0,)), ((), ())), preferred_element_type=F32)


def _sig(x):
    return 1.0 / (1.0 + jnp.exp(-x))


def _place():
    x, y, c = lax.axis_index("x"), lax.axis_index("y"), lax.axis_index("c")
    return x, y, c


def _all_gather8(v, name):
    r, cdim = v.shape

    def body(v_ref, out_ref, send_sems, recv_sems, local_sem):
        x, y, c = _place()
        me = 4 * x + 2 * y + c
        peers = [(x, y, 1 - c), (1 - x, y, c), (x, 1 - y, c), (1 - x, 1 - y, c),
                 (1 - x, y, 1 - c), (x, 1 - y, 1 - c), (1 - x, 1 - y, 1 - c)]
        mine = pltpu.make_async_copy(v_ref, out_ref.at[me], local_sem)
        mine.start()

        def copy(k, block, to):
            return pltpu.make_async_remote_copy(src_ref=v_ref, dst_ref=out_ref.at[block], send_sem=send_sems.at[k],
                                                recv_sem=recv_sems.at[k], device_id=to, device_id_type=MESH)

        sends = [copy(k, me, p) for k, p in enumerate(peers)]
        for cp in sends:
            cp.start()
        for k, (px, py, pc) in enumerate(peers):
            copy(k, 4 * px + 2 * py + pc, (px, py, pc)).wait_recv()
        for cp in sends:
            cp.wait_send()
        mine.wait()

    return pl.pallas_call(
        body, name=name, out_shape=SDS((8, r, cdim), v.dtype), in_specs=[VMEM_SPEC], out_specs=VMEM_SPEC,
        scratch_shapes=[pltpu.SemaphoreType.DMA((7,)), pltpu.SemaphoreType.DMA((7,)), pltpu.SemaphoreType.DMA(())],
        compiler_params=_cp(side=True),
    )(v)


W_CUTS = (("col", D, NCOL // NCHIP), ("col", AW, D // NCHIP), ("row", D // NCHIP, D), ("row", D // NCHIP, D))
W_FULL = ((D, NCOL), (AW, D), (D, D), (D, D))


def _shard_window(ref, cut, k, half):
    kind, nr, nc = cut
    hr = nr // 2
    if kind == "col":
        rows = pl.ds(0, nr) if half is None else pl.ds(pl.multiple_of(half * hr, 16), hr)
        return ref.at[rows, pl.ds(pl.multiple_of(k * nc, 128), nc)]
    if half is None:
        return ref.at[pl.ds(pl.multiple_of(k * nr, 16), nr), :]
    return ref.at[pl.ds(pl.multiple_of(k * nr + half * hr, 16), hr), :]


def _half_rows(ref, cut, half):
    hr = cut[1] // 2
    return ref.at[pl.ds(pl.multiple_of(half * hr, 16), hr), :]


def _to_bf16_window(a, w, name):
    kind, nr, nc = W_CUTS[a]
    x, y, _ = _place()
    chip = jnp.reshape(2 * x + y, (1,)).astype(jnp.int32)
    tr = min(nr, 256)

    def body(c_ref, w_ref, o_ref):
        o_ref[...] = w_ref[...].astype(BF16)

    out_map = (lambda i, cr: (i, cr[0])) if kind == "col" else (lambda i, cr: (cr[0] * (nr // tr) + i, 0))
    return pl.pallas_call(
        body, name=name, out_shape=SDS(W_FULL[a], BF16),
        grid_spec=pltpu.PrefetchScalarGridSpec(num_scalar_prefetch=1, grid=(nr // tr,),
                                               in_specs=[pl.BlockSpec((tr, nc), lambda i, cr: (i, 0))], out_specs=pl.BlockSpec((tr, nc), out_map)),
        compiler_params=_cp(("arbitrary",)),
    )(chip, w)


def _swap_halves(grads):
    n = len(grads)
    shapes = []
    for a in range(n):
        kind, nr, nc = W_CUTS[a]
        shapes.append((W_FULL[a][0] // 2, W_FULL[a][1]) if kind == "col" else (NCHIP, nr // 2, nc))

    def pieces(a, ref, land, half):
        kind, nr, nc = W_CUTS[a]
        if kind == "col":
            hr = nr // 2
            return [(ref.at[pl.ds(pl.multiple_of(half * hr, 16), hr), :], land)]
        return [(_shard_window(ref, W_CUTS[a], k, half), land.at[k]) for k in range(NCHIP)]

    def body(*refs):
        src, land = refs[:n], refs[n:2 * n]
        send_sems, recv_sems = refs[2 * n:]
        x, y, c = _place()
        sibling = (x, y, 1 - c)
        sends = []
        k = 0
        for a in range(n):
            for s, d in pieces(a, src[a], land[a], 1 - c):
                cp = pltpu.make_async_remote_copy(src_ref=s, dst_ref=d, send_sem=send_sems.at[k], recv_sem=recv_sems.at[k],
                                                  device_id=sibling, device_id_type=MESH)
                cp.start()
                sends.append(cp)
                k += 1
        for cp in sends:
            cp.wait()

    n_sems = sum(1 if W_CUTS[a][0] == "col" else NCHIP for a in range(n))
    return pl.pallas_call(
        body, name="swap_grad_halves", out_shape=[SDS(s, F32) for s in shapes], in_specs=[ANY] * n, out_specs=[ANY] * n,
        scratch_shapes=[pltpu.SemaphoreType.DMA((n_sems,)), pltpu.SemaphoreType.DMA((n_sems,))],
        compiler_params=_cp(side=True),
    )(*grads)


def _chip_sum(a, grad, got, name):
    kind, nr, nc = W_CUTS[a]
    hr = nr // 2
    c = lax.axis_index("c")
    cidx = jnp.reshape(c, (1,)).astype(jnp.int32)

    def body(c_ref, g_ref, r_ref, f_ref, b_ref):
        s = g_ref[...] + r_ref[...]
        f_ref[...] = s.reshape(f_ref.shape)
        b_ref[...] = s.astype(BF16).reshape(b_ref.shape)

    if kind == "col":
        in_specs = [pl.BlockSpec((hr, nc), lambda k, cr: (cr[0], k)), pl.BlockSpec((hr, nc), lambda k, cr: (0, k))]
    else:
        grad = grad.reshape(NCHIP, 2, hr, nc)
        in_specs = [pl.BlockSpec((1, 1, hr, nc), lambda k, cr: (k, cr[0], 0, 0)), pl.BlockSpec((1, hr, nc), lambda k, cr: (k, 0, 0))]
    out_specs = [pl.BlockSpec((1, hr, nc), lambda k, cr: (k, 0, 0))] * 2
    return pl.pallas_call(
        body, name=name, out_shape=[SDS((NCHIP, hr, nc), F32), SDS((NCHIP, hr, nc), BF16)],
        grid_spec=pltpu.PrefetchScalarGridSpec(num_scalar_prefetch=1, grid=(NCHIP,), in_specs=in_specs, out_specs=out_specs),
        compiler_params=_cp(("arbitrary",), VMEM_CAP),
    )(cidx, grad, got)


def _reduce_mine(a, mine_f32, got, name):
    kind, nr, nc = W_CUTS[a]
    hr = nr // 2
    x, y, c = _place()
    where = jnp.stack([2 * x + y, c]).astype(jnp.int32)
    tr = min(hr, 256)

    def body(w_ref, m_ref, g_ref, o_ref):
        o_ref[...] = ((m_ref[0] + g_ref[0].astype(F32)) + g_ref[1].astype(F32)) + g_ref[2].astype(F32)

    return pl.pallas_call(
        body, name=name, out_shape=SDS((nr, nc), F32),
        grid_spec=pltpu.PrefetchScalarGridSpec(
            num_scalar_prefetch=1, grid=(hr // tr,),
            in_specs=[pl.BlockSpec((1, tr, nc), lambda i, wr: (wr[0], i, 0)), pl.BlockSpec((3, tr, nc), lambda i, wr: (0, i, 0))],
            out_specs=pl.BlockSpec((tr, nc), lambda i, wr: (wr[1] * (hr // tr) + i, 0))),
        compiler_params=_cp(("arbitrary",), VMEM_CAP),
    )(where, mine_f32, got)


def _join_halves(fulls):
    n = len(fulls)

    def body(*refs):
        full = refs[n:2 * n]
        send_sems, recv_sems = refs[2 * n:]
        x, y, c = _place()
        sibling = (x, y, 1 - c)

        def swap(a, half):
            rows = _half_rows(full[a], W_CUTS[a], half)
            return pltpu.make_async_remote_copy(src_ref=rows, dst_ref=rows, send_sem=send_sems.at[a], recv_sem=recv_sems.at[a],
                                                device_id=sibling, device_id_type=MESH)

        sends = [swap(a, c) for a in range(n)]
        for cp in sends:
            cp.start()
        for a, cp in enumerate(sends):
            cp.wait_send()
            swap(a, 1 - c).wait_recv()

    return pl.pallas_call(
        body, name="join_grad_halves", out_shape=[SDS((W_CUTS[a][1], W_CUTS[a][2]), F32) for a in range(n)],
        in_specs=[ANY] * n, out_specs=[ANY] * n,
        scratch_shapes=[pltpu.SemaphoreType.DMA((n,)), pltpu.SemaphoreType.DMA((n,))],
        input_output_aliases={a: a for a in range(n)}, compiler_params=_cp(side=True),
    )(*fulls)


def _ada_forward(c_all, w_ada, b_cols):
    nb, nc = c_all.shape[0], w_ada.shape[1]

    def body(c_ref, w_ref, b_ref, o_ref):
        cv = c_ref[...]
        sc = (cv * _sig(cv)).astype(BF16)
        o_ref[...] = _dot(sc, w_ref[...].astype(BF16)) + b_ref[...]

    return pl.pallas_call(body, name="ada_forward", out_shape=SDS((nb, nc), F32), compiler_params=_cp(vmem=VMEM_CAP // 2))(c_all, w_ada, b_cols)


def _ada_backward(c_all, dmod_cols, dmod_all):
    nb, nc = dmod_cols.shape

    def body(c_ref, d_ref, a_ref, gw_ref, gb_ref):
        cv = c_ref[...]
        sc = (cv * _sig(cv)).astype(BF16)
        gw_ref[...] = _dot_tn(sc, d_ref[...].astype(BF16))
        gb_ref[...] = jnp.sum(a_ref[...], axis=0, keepdims=True)

    return pl.pallas_call(body, name="ada_backward", out_shape=[SDS((D, nc), F32), SDS((1, dmod_all.shape[1]), F32)],
                          compiler_params=_cp(vmem=VMEM_CAP // 2))(c_all, dmod_cols, dmod_all)


def _modulate(x2, sc1p, shift, seq, tm=256):
    t = x2.shape[0]
    spt = seq // tm

    def body(x_ref, sc_ref, sh_ref, h_ref, ht_ref):
        h = x_ref[...] * sc_ref[0] + sh_ref[0]
        h_ref[...] = h.astype(BF16)
        ht_ref[...] = h.T.astype(BF16)

    per_seq = pl.BlockSpec((1, 1, D), lambda i: (i // spt, 0, 0))
    return pl.pallas_call(
        body, name="modulate", out_shape=[SDS((t, D), BF16), SDS((D, t), BF16)], grid=(t // tm,),
        in_specs=[pl.BlockSpec((tm, D), lambda i: (i, 0)), per_seq, per_seq],
        out_specs=[pl.BlockSpec((tm, D), lambda i: (i, 0)), pl.BlockSpec((D, tm), lambda i: (0, i))],
        compiler_params=_cp(("parallel",)),
    )(x2, sc1p, shift)


TW = 256
TPS = NCOL // NCHIP // TW
NT = NCOL // TW
NQKV_T = 3 * QW // TW


def _tile_tables():
    tabs = np.zeros((NCHIP, 3, NT), np.int32)
    for me in range(NCHIP):
        tiles = [TPS * (me ^ (s // TPS)) + s % TPS for s in range(NT)]
        tabs[me, 0] = tiles
        for row, (lo, hi) in enumerate(((0, NQKV_T), (NQKV_T, NT))):
            mine = [w - lo if lo <= w < hi else None for w in tiles]
            held = next(m for m in mine if m is not None)
            for s, m in enumerate(mine):
                held = held if m is None else m
                tabs[me, 1 + row, s] = held
    return tabs


def _project_gather(h, fulls, tab):
    t = h.shape[0]
    n = len(fulls)
    chunk = 1024

    def body(tab_ref, h_ref, *rest):
        qkv_ref, g_ref = rest[n], rest[n + 1]
        full = rest[n + 2:2 * n + 2]
        w_buf, tile_sems, send_sems, recv_sems = rest[2 * n + 2:]
        s = pl.program_id(0)
        x, y, c = _place()
        me = 2 * x + y
        peers = [(x, 1 - y), (1 - x, y), (1 - x, 1 - y)]
        sibling = (x, y, 1 - c)

        def hop(a, r, stage, chip, half, to):
            window = _shard_window(full[a], W_CUTS[a], chip, half)
            k = 6 * a + 2 * r + stage
            return pltpu.make_async_remote_copy(src_ref=window, dst_ref=window, send_sem=send_sems.at[k], recv_sem=recv_sems.at[k],
                                                device_id=to, device_id_type=MESH)

        def send(a, r):
            return hop(a, r, 0, me, c, (*peers[r], c))

        def arrive(a, r):
            px, py = peers[r]
            chip = 2 * px + py
            hop(a, r, 0, chip, c, (px, py, c)).wait_recv()
            hop(a, r, 1, chip, c, sibling).start()
            hop(a, r, 1, chip, 1 - c, sibling).wait_recv()

        def tile(step, slot):
            col = pl.multiple_of(tab_ref[0, step] * TW, TW)
            return pltpu.make_async_copy(full[0].at[:, pl.ds(col, TW)], w_buf.at[slot], tile_sems.at[slot])

        @pl.when(s == 0)
        def _():
            for r in range(3):
                for a in range(n):
                    send(a, r).start()
            tile(0, 0).start()

        slot = s % 2
        tile(s, slot).wait()

        @pl.when((s + 1 < NT) & ((s + 1) % TPS != 0))
        def _():
            tile(s + 1, 1 - slot).start()

        w = w_buf[slot]
        is_qkv = tab_ref[0, s] < NQKV_T
        for i in range(t // chunk):
            rows = pl.ds(i * chunk, chunk)
            acc = _dot(h_ref[rows, :], w)

            @pl.when(is_qkv)
            def _(rows=rows, acc=acc):
                qkv_ref[rows, :] = acc

            @pl.when(jnp.logical_not(is_qkv))
            def _(rows=rows, acc=acc):
                g_ref[rows, :] = acc.astype(BF16)

        for r in range(3):
            @pl.when(s + 1 == TPS * (r + 1))
            def _(r=r):
                arrive(0, r)
                tile(s + 1, 1 - slot).start()

        @pl.when(s == NT - 1)
        def _():
            for a in range(1, n):
                for r in range(3):
                    arrive(a, r)
            for a in range(n):
                for r in range(3):
                    send(a, r).wait_send()
                    px, py = peers[r]
                    hop(a, r, 1, 2 * px + py, c, sibling).wait_send()

    outs = pl.pallas_call(
        body, name="project_gather", out_shape=[SDS((t, 3 * QW), F32), SDS((t, NGATE), BF16)] + [SDS(s, BF16) for s in W_FULL],
        grid_spec=pltpu.PrefetchScalarGridSpec(
            num_scalar_prefetch=1, grid=(NT,),
            in_specs=[pl.BlockSpec((t, D), lambda s, tab: (0, 0))] + [ANY] * n,
            out_specs=[pl.BlockSpec((t, TW), lambda s, tab: (0, tab[1, s])), pl.BlockSpec((t, TW), lambda s, tab: (0, tab[2, s]))] + [ANY] * n,
            scratch_shapes=[pltpu.VMEM((2, D, TW), BF16), pltpu.SemaphoreType.DMA((2,)),
                            pltpu.SemaphoreType.DMA((6 * n,)), pltpu.SemaphoreType.DMA((6 * n,))]),
        input_output_aliases={2 + a: 2 + a for a in range(n)},
        compiler_params=_cp(("arbitrary",), VMEM_CAP, side=True),
    )(tab, h, *fulls)
    return outs[0], outs[1], outs[2:]


def _bias_tables(rel_bias, buckets):
    def body(tab_ref, bk_ref, o_ref):
        a = lax.broadcasted_iota(jnp.int32, (BLK, 2 * BLK), 0)
        b = lax.broadcasted_iota(jnp.int32, (BLK, 2 * BLK), 1)
        steps = a + BLK - b
        valid = (steps >= 0) & (steps <= BLK)
        for g in range(3):
            bk = bk_ref[g]
            for j in range(4):
                def pick(kk, acc, bk=bk, col=4 * g + j):
                    return jnp.where(bk == kk, tab_ref[kk, col], acc)

                acc = lax.fori_loop(0, N_BUCKETS, pick, jnp.zeros((BLK, 2 * BLK), F32))
                o_ref[g, j] = jnp.where(valid, acc, NEG)

    return pl.pallas_call(
        body, name="bias_tables", out_shape=SDS((3, 4, BLK, 2 * BLK), F32),
        in_specs=[pl.BlockSpec(memory_space=pltpu.SMEM), VMEM_SPEC], out_specs=VMEM_SPEC,
    )(rel_bias, buckets)


def _bias_grad(ds_sum, buckets):
    def body(ds_ref, bk_ref, o_ref):
        lane = lax.broadcasted_iota(jnp.int32, (1, 128), 1)
        for g in range(3):
            def bucket(kk, carry, g=g):
                row = jnp.zeros((1, 128), F32)
                for j in range(4):
                    v = jnp.where(bk_ref[g] == kk, ds_ref[g, j], 0.0)
                    s = jnp.sum(jnp.sum(v, axis=1, keepdims=True), axis=0, keepdims=True)
                    row = jnp.where(lane == j, s, row)
                o_ref[g, pl.ds(kk, 1), :] = row
                return carry

            lax.fori_loop(0, N_BUCKETS, bucket, 0)

    return pl.pallas_call(body, name="bias_grad", out_shape=SDS((3, N_BUCKETS, 128), F32), in_specs=[VMEM_SPEC, VMEM_SPEC],
                          out_specs=VMEM_SPEC)(ds_sum, buckets)


def _sub_rows(d, r, n):
    return pl.ds(n * BLK * d + r, BLK) if d == 1 else pl.ds(n * BLK * d + r, BLK, stride=d)


def _head_spec(seq, g, part):
    return pl.BlockSpec((seq, HD), lambda b, hh: (b, part * (QW // HD) + 4 * g + hh))


def _attn_forward(g, qkv, bias, bsz, seq):
    d = DILATIONS[g]
    nblk = seq // d // BLK

    def body(q_ref, k_ref, v_ref, b_ref, o_ref, l_ref):
        hs = pl.program_id(1)
        for r in range(d):
            for n in range(nblk):
                rows = _sub_rows(d, r, n)
                qb = q_ref[rows, :].astype(BF16)
                s_c = _dot_nt(qb, k_ref[rows, :].astype(BF16)) * SCALE + b_ref[hs, :, BLK:]
                m = jnp.max(s_c, axis=1, keepdims=True)
                if n > 0:
                    prev = _sub_rows(d, r, n - 1)
                    s_p = _dot_nt(qb, k_ref[prev, :].astype(BF16)) * SCALE + b_ref[hs, :, :BLK]
                    m = jnp.maximum(m, jnp.max(s_p, axis=1, keepdims=True))
                p_c = jnp.exp(s_c - m)
                den = jnp.sum(p_c, axis=1, keepdims=True)
                acc = _dot(p_c.astype(BF16), v_ref[rows, :].astype(BF16))
                if n > 0:
                    p_p = jnp.exp(s_p - m)
                    den = den + jnp.sum(p_p, axis=1, keepdims=True)
                    acc = acc + _dot(p_p.astype(BF16), v_ref[prev, :].astype(BF16))
                o_ref[rows, :] = acc / den
                l_ref[rows, :] = jnp.broadcast_to(m + jnp.log(den), (BLK, HD))

    out_spec = pl.BlockSpec((seq, HD), lambda b, hh: (b, hh))
    return pl.pallas_call(
        body, name=f"attn_forward_{g}", out_shape=[SDS((bsz * seq, AW), F32)] * 2, grid=(bsz, 4),
        in_specs=[_head_spec(seq, g, part) for part in range(3)] + [pl.BlockSpec((4, BLK, 2 * BLK), lambda b, hh: (0, 0, 0))],
        out_specs=[out_spec, out_spec],
        compiler_params=_cp(("parallel", "parallel"), VMEM_CAP // 2),
    )(qkv, qkv, qkv, bias)


def _attn_backward(g, qkv, do, dl, bias, prev_out, bsz, seq):
    d = DILATIONS[g]
    nblk = seq // d // BLK

    def body(q_ref, k_ref, v_ref, do_ref, dl_ref, b_ref, *rest):
        dq_ref, dk_ref, dv_ref, db_ref = rest[-4:]
        hs = pl.program_id(1)

        @pl.when((pl.program_id(0) == 0) & (hs == 0))
        def _():
            db_ref[...] = jnp.zeros_like(db_ref)

        dk_ref[...] = jnp.zeros_like(dk_ref)
        dv_ref[...] = jnp.zeros_like(dv_ref)
        for r in range(d):
            for n in range(nblk):
                rows = _sub_rows(d, r, n)
                qb = q_ref[rows, :].astype(BF16)
                dob = do_ref[rows, :].astype(BF16)
                both = dl_ref[rows, :]
                lse, delta = both[:, 0:1], both[:, 64:65]
                dq = jnp.zeros((BLK, HD), F32)
                parts = [(rows, slice(BLK, 2 * BLK))]
                if n > 0:
                    parts.append((_sub_rows(d, r, n - 1), slice(0, BLK)))
                for keys, band in parts:
                    kb, vb = k_ref[keys, :].astype(BF16), v_ref[keys, :].astype(BF16)
                    p = jnp.exp(_dot_nt(qb, kb) * SCALE + b_ref[hs, :, band] - lse)
                    ds = p * (_dot_nt(dob, vb) - delta)
                    dsb = ds.astype(BF16)
                    dv_ref[keys, :] += _dot_tn(p.astype(BF16), dob)
                    dk_ref[keys, :] += _dot_tn(dsb, qb) * SCALE
                    dq = dq + _dot(dsb, kb) * SCALE
                    db_ref[hs, :, band] += ds
                dq_ref[rows, :] = dq

    qkv_spec = _head_spec(seq, g, 0)
    out_spec = pl.BlockSpec((seq, HD), lambda b, hh: (b, hh))
    band_spec = pl.BlockSpec((4, BLK, 2 * BLK), lambda b, hh: (0, 0, 0))
    ins = [qkv, qkv, qkv, do, dl, bias]
    in_specs = [_head_spec(seq, g, part) for part in range(3)] + [out_spec, out_spec, band_spec]
    aliases = {}
    if prev_out is not None:
        ins += list(prev_out)
        in_specs += [ANY] * 3
        aliases = {6: 0, 7: 1, 8: 2}
    dq, dk, dv, db = pl.pallas_call(
        body, name=f"attn_backward_{g}", out_shape=[SDS((bsz * seq, QW), F32)] * 3 + [SDS((4, BLK, 2 * BLK), F32)], grid=(bsz, 4),
        in_specs=in_specs, out_specs=[qkv_spec] * 3 + [band_spec], input_output_aliases=aliases,
        compiler_params=_cp(("arbitrary", "arbitrary"), VMEM_CAP // 2),
    )(*ins)
    return (dq, dk, dv), db


def _mix_forward(gates, og, lg, x2, tgt, gate, w_ao, w_co, w_o, conv_w, conv_b, ln_g, ln_b, bsz, seq, tm=256):
    t = x2.shape[0]
    spt = seq // tm

    def body(g_ref, o1, o2, o3, l1, l2, l3, x_ref, t_ref, gate_ref, wao_ref, wco_ref, wo_ref, cw_ref, cb_ref, lng_ref, lnb_ref,
             ain_ref, sin_ref, mrg_ref, dy_ref, aout_ref, sout_ref, yc_ref, o_ref, lj_ref, dxr_ref, vec_ref, dgate_ref, zc_ref):
        b, i = pl.program_id(0), pl.program_id(1)

        @pl.when((b == 0) & (i == 0))
        def _():
            vec_ref[...] = jnp.zeros_like(vec_ref)

        @pl.when(i == 0)
        def _():
            zc_ref[...] = jnp.zeros_like(zc_ref)
            dgate_ref[...] = jnp.zeros_like(dgate_ref)

        g_attn, u, bg, cg, g_conv, m_attn, m_conv = (g_ref[:, lo:hi].astype(F32) for lo, hi in GATE_COLS)
        la, lb, lc = l1[...], l2[...], l3[...]
        mx = jnp.maximum(la, jnp.maximum(lb, lc))
        ea, eb, ec = jnp.exp(la - mx), jnp.exp(lb - mx), jnp.exp(lc - mx)
        den = ea + eb + ec
        o = (ea * o1[...] + eb * o2[...] + ec * o3[...]) / den
        o_ref[...] = o
        lj_ref[...] = mx + jnp.log(den)
        a_in = o * (g_attn * _sig(g_attn))
        ain_ref[...] = a_in.astype(BF16)
        a_out = _dot(a_in.astype(BF16), wao_ref[...])
        aout_ref[...] = a_out.astype(BF16)
        z = cg * u
        rows = lax.broadcasted_iota(jnp.int32, (tm, D), 0)
        c6, c7 = zc_ref[6:7, :], zc_ref[7:8, :]
        z1 = jnp.where(rows == 0, c7, pltpu.roll(z, 1, 0))
        z2 = jnp.where(rows == 0, c6, jnp.where(rows == 1, c7, pltpu.roll(z, 2, 0)))
        zc_ref[...] = z[tm - 8:tm, :]
        y_conv = (cw_ref[0:1, :] * z2 + cw_ref[1:2, :] * z1 + cw_ref[2:3, :] * z) + cb_ref[...]
        yc_ref[...] = y_conv.astype(BF16)
        s_in = bg * y_conv * (g_conv * _sig(g_conv))
        sin_ref[...] = s_in.astype(BF16)
        s_out = _dot(s_in.astype(BF16), wco_ref[...])
        sout_ref[...] = s_out.astype(BF16)
        merged = _sig(m_attn) * a_out + _sig(m_conv) * s_out
        mrg_ref[...] = merged.astype(BF16)
        y = _dot(merged.astype(BF16), wo_ref[...])
        gate1 = 1.0 + gate_ref[0]
        r = ALPHA * x_ref[...] + gate1 * y
        mu = jnp.mean(r, axis=1, keepdims=True)
        rc = r - mu
        rstd = lax.rsqrt(jnp.mean(rc * rc, axis=1, keepdims=True) + LN_EPS)
        xhat = rc * rstd
        diff = (xhat * lng_ref[...] + lnb_ref[...]) - t_ref[...]
        dout = diff * (1.0 / D)
        vec_ref[0:1, :] += jnp.sum(dout * xhat, axis=0, keepdims=True)
        vec_ref[1:2, :] += jnp.sum(dout, axis=0, keepdims=True)
        vec_ref[2:3, :] += jnp.sum(diff * diff, axis=0, keepdims=True)
        dxh = dout * lng_ref[...]
        dr = rstd * (dxh - jnp.mean(dxh, axis=1, keepdims=True) - xhat * jnp.mean(dxh * xhat, axis=1, keepdims=True))
        dxr_ref[...] = ALPHA * dr
        dy_ref[...] = (dr * gate1).astype(BF16)
        dgate_ref[0] += jnp.sum(dr * y, axis=0, keepdims=True)

    tok = lambda w: pl.BlockSpec((tm, w), lambda b, i: (b * spt + i, 0))
    const = lambda s: pl.BlockSpec(s, lambda b, i: (0,) * len(s))
    per_seq = pl.BlockSpec((1, 1, D), lambda b, i: (b, 0, 0))
    outs = pl.pallas_call(
        body, name="mix_forward", grid=(bsz, spt),
        out_shape=[SDS((t, AW), BF16), SDS((t, D), BF16), SDS((t, D), BF16), SDS((t, D), BF16), SDS((t, D), BF16), SDS((t, D), BF16),
                   SDS((t, D), BF16), SDS((t, AW), F32), SDS((t, AW), F32), SDS((t, D), F32), SDS((8, D), F32), SDS((bsz, 1, D), F32)],
        in_specs=[tok(NGATE)] + [tok(AW)] * 6 + [tok(D), tok(D), per_seq, const((AW, D)), const((D, D)), const((D, D)),
                                                 const((3, D)), const((1, D)), const((1, D)), const((1, D))],
        out_specs=[tok(AW), tok(D), tok(D), tok(D), tok(D), tok(D), tok(D), tok(AW), tok(AW), tok(D), const((8, D)), per_seq],
        scratch_shapes=[pltpu.VMEM((8, D), F32)],
        compiler_params=_cp(("arbitrary", "arbitrary"), VMEM_CAP),
    )(gates, *og, *lg, x2, tgt, gate, w_ao, w_co, w_o, conv_w, conv_b, ln_g, ln_b)
    return outs


def _mix_backward(gates, dy, a_out, s_out, y_conv, o, lj, w_ao, w_co, w_o, conv_w, bsz, seq, tm=256):
    t = dy.shape[0]
    spt = seq // tm

    def body(g_ref, dy_ref, aout_ref, sout_ref, yc_ref, o_ref, lj_ref, wao_ref, wco_ref, wo_ref, cw_ref,
             dg_ref, do_ref, dl_ref, daout_ref, dsout_ref, vec_ref, car_ref):
        b, i = pl.program_id(0), pl.program_id(1)

        @pl.when((b == 0) & (i == 0))
        def _():
            vec_ref[...] = jnp.zeros_like(vec_ref)

        @pl.when(i == 0)
        def _():
            car_ref[...] = jnp.zeros_like(car_ref)

        g_attn, u, bg, cg, g_conv, m_attn, m_conv = (g_ref[:, lo:hi].astype(F32) for lo, hi in GATE_COLS)
        dmerged = _dot_nt(dy_ref[...], wo_ref[...])
        sa, sc = _sig(m_attn), _sig(m_conv)
        da_out = (dmerged * sa).astype(BF16)
        ds_out = (dmerged * sc).astype(BF16)
        daout_ref[...] = da_out
        dsout_ref[...] = ds_out
        dg_ref[:, 4608:5632] = (dmerged * aout_ref[...].astype(F32) * (sa * (1.0 - sa))).astype(BF16)
        dg_ref[:, 5632:6656] = (dmerged * sout_ref[...].astype(F32) * (sc * (1.0 - sc))).astype(BF16)
        da_in = _dot_nt(da_out, wao_ref[...])
        ds_in = _dot_nt(ds_out, wco_ref[...])
        sga = _sig(g_attn)
        o = o_ref[...]
        do = da_in * (g_attn * sga)
        do_ref[...] = do
        dg_ref[:, 0:512] = (da_in * o * (sga * (1.0 + g_attn * (1.0 - sga)))).astype(BF16)
        prod = do * o
        lane = lax.broadcasted_iota(jnp.int32, (tm, HD), 1)
        for j in range(4):
            cs = slice(j * HD, (j + 1) * HD)
            delta = jnp.sum(prod[:, cs], axis=1, keepdims=True)
            dl_ref[:, cs] = jnp.where(lane < 64, lj_ref[:, cs], delta)
        sgc = _sig(g_conv)
        silu_c = g_conv * sgc
        yc = yc_ref[...].astype(F32)
        dg_ref[:, 1536:2560] = (ds_in * yc * silu_c).astype(BF16)
        dg_ref[:, 3584:4608] = (ds_in * bg * yc * (sgc * (1.0 + g_conv * (1.0 - sgc)))).astype(BF16)
        dyc = ds_in * bg * silu_c
        rows = lax.broadcasted_iota(jnp.int32, (tm, D), 0)
        c0, c1 = car_ref[0:1, :], car_ref[1:2, :]
        n1 = jnp.where(rows == tm - 1, c0, pltpu.roll(dyc, tm - 1, 0))
        n2 = jnp.where(rows == tm - 2, c0, jnp.where(rows == tm - 1, c1, pltpu.roll(dyc, tm - 2, 0)))
        car_ref[...] = dyc[0:8, :]
        dz = cw_ref[2:3, :] * dyc + cw_ref[1:2, :] * n1 + cw_ref[0:1, :] * n2
        z = cg * u
        dg_ref[:, 512:1536] = (dz * cg).astype(BF16)
        dg_ref[:, 2560:3584] = (dz * u).astype(BF16)
        vec_ref[0:1, :] += jnp.sum(n2 * z, axis=0, keepdims=True)
        vec_ref[1:2, :] += jnp.sum(n1 * z, axis=0, keepdims=True)
        vec_ref[2:3, :] += jnp.sum(dyc * z, axis=0, keepdims=True)
        vec_ref[3:4, :] += jnp.sum(dyc, axis=0, keepdims=True)

    tok = lambda w: pl.BlockSpec((tm, w), lambda b, i: (b * spt + (spt - 1 - i), 0))
    const = lambda s: pl.BlockSpec(s, lambda b, i: (0,) * len(s))
    return pl.pallas_call(
        body, name="mix_backward", grid=(bsz, spt),
        out_shape=[SDS((t, NGATE), BF16), SDS((t, AW), F32), SDS((t, AW), F32), SDS((t, D), BF16), SDS((t, D), BF16), SDS((8, D), F32)],
        in_specs=[tok(NGATE), tok(D), tok(D), tok(D), tok(D), tok(AW), tok(AW), const((AW, D)), const((D, D)), const((D, D)), const((3, D))],
        out_specs=[tok(NGATE), tok(AW), tok(AW), tok(D), tok(D), const((8, D))],
        scratch_shapes=[pltpu.VMEM((8, D), F32)],
        compiler_params=_cp(("arbitrary", "arbitrary"), VMEM_CAP),
    )(gates, dy, a_out, s_out, y_conv, o, lj, w_ao, w_co, w_o, conv_w)


def _out_weight_grads(a_in, da_out, s_in, ds_out, merged, dy, tk=512):
    t = dy.shape[0]

    def body(ain_ref, da_ref, sin_ref, ds_ref, m_ref, dy_ref, gao_ref, gco_ref, go_ref):
        @pl.when(pl.program_id(0) == 0)
        def _():
            gao_ref[...] = jnp.zeros_like(gao_ref)
            gco_ref[...] = jnp.zeros_like(gco_ref)
            go_ref[...] = jnp.zeros_like(go_ref)

        gao_ref[...] += _dot_tn(ain_ref[...], da_ref[...])
        gco_ref[...] += _dot_tn(sin_ref[...], ds_ref[...])
        go_ref[...] += _dot_tn(m_ref[...], dy_ref[...])

    tok = lambda w: pl.BlockSpec((tk, w), lambda i: (i, 0))
    const = lambda s: pl.BlockSpec(s, lambda i: (0, 0))
    return pl.pallas_call(
        body, name="out_weight_grads", grid=(t // tk,), out_shape=[SDS((AW, D), F32), SDS((D, D), F32), SDS((D, D), F32)],
        in_specs=[tok(AW), tok(D), tok(D), tok(D), tok(D), tok(D)], out_specs=[const((AW, D)), const((D, D)), const((D, D))],
        compiler_params=_cp(("arbitrary",), VMEM_CAP),
    )(a_in, da_out, s_in, ds_out, merged, dy)


def _input_grad(dq, dk, dv, dgates, w, x2, dxr, sc1p, seq, sums, tm=1024):
    t = x2.shape[0]
    spt = seq // tm
    bsz = t // seq
    n = len(sums)

    def body(dq_ref, dk_ref, dv_ref, dg_ref, w_ref, x_ref, dxr_ref, sc_ref, *rest):
        src, (dx_ref, dsh_ref, dsc_ref), land = rest[:n], rest[n:n + 3], rest[n + 3:2 * n + 3]
        acc_ref, send_sems, recv_sems = rest[2 * n + 3:]
        i, j = pl.program_id(0), pl.program_id(1)
        px, py, pc = _place()
        chips = [(1 - px, py), (px, 1 - py), (1 - px, 1 - py)]
        copies = [pltpu.make_async_remote_copy(src_ref=src[a].at[2 * cx + cy], dst_ref=land[a].at[r], send_sem=send_sems.at[3 * a + r],
                                               recv_sem=recv_sems.at[3 * a + r], device_id=(cx, cy, pc), device_id_type=MESH)
                  for a in range(n) for r, (cx, cy) in enumerate(chips)]

        @pl.when((i == 0) & (j == 0))
        def _():
            for cp in copies:
                cp.start()

        @pl.when(j == 0)
        def _():
            acc_ref[...] = jnp.zeros_like(acc_ref)

        for k, ref in enumerate((dq_ref, dk_ref, dv_ref)):
            @pl.when((j >= k * NQT) & (j < (k + 1) * NQT))
            def _(ref=ref):
                acc_ref[...] += _dot_nt(ref[...].astype(BF16), w_ref[...])

        @pl.when(j >= 3 * NQT)
        def _():
            acc_ref[...] += _dot_nt(dg_ref[...], w_ref[...])

        @pl.when(j == NPT - 1)
        def _():
            dh = acc_ref[...]
            dx_ref[...] = dh * sc_ref[0] + dxr_ref[...]

            @pl.when(i % spt == 0)
            def _():
                dsh_ref[...] = jnp.zeros_like(dsh_ref)
                dsc_ref[...] = jnp.zeros_like(dsc_ref)

            dsh_ref[0] += jnp.sum(dh, axis=0, keepdims=True)
            dsc_ref[0] += jnp.sum(dh * x_ref[...], axis=0, keepdims=True)

        @pl.when((i == t // tm - 1) & (j == NPT - 1))
        def _():
            for cp in copies:
                cp.wait()

    def part(k):
        return pl.BlockSpec((tm, TN), lambda i, j: (i, jnp.clip(j - k * NQT, 0, NQT - 1)))

    row = pl.BlockSpec((tm, D), lambda i, j: (i, 0))
    per_seq = pl.BlockSpec((1, 1, D), lambda i, j: (i // spt, 0, 0))
    outs = pl.pallas_call(
        body, name="input_grad", grid=(t // tm, NPT),
        out_shape=[SDS((t, D), F32), SDS((bsz, 1, D), F32), SDS((bsz, 1, D), F32)] + [SDS((3,) + s.shape[1:], BF16) for s in sums],
        in_specs=[part(0), part(1), part(2), pl.BlockSpec((tm, TN), lambda i, j: (i, jnp.maximum(j - 3 * NQT, 0))),
                  pl.BlockSpec((D, TN), lambda i, j: (0, j)), row, row, per_seq] + [ANY] * n,
        out_specs=[row, per_seq, per_seq] + [ANY] * n,
        scratch_shapes=[pltpu.VMEM((tm, D), F32), pltpu.SemaphoreType.DMA((3 * NCHIP,)), pltpu.SemaphoreType.DMA((3 * NCHIP,))],
        compiler_params=_cp(("arbitrary", "arbitrary"), VMEM_CAP, side=True),
    )(dq, dk, dv, dgates, w, x2, dxr, sc1p, *sums)
    return outs[0], outs[1], outs[2], outs[3:]


def _in_weight_grad(ht, src, col0, prev, name, tk=1024):
    t = ht.shape[1]
    ncols = src.shape[1] // TN

    def body(ht_ref, s_ref, *rest):
        o_ref = rest[-1]

        @pl.when(pl.program_id(1) == 0)
        def _():
            o_ref[...] = jnp.zeros_like(o_ref)

        o_ref[...] += _dot(ht_ref[...], s_ref[...].astype(BF16))

    ins = [ht, src]
    in_specs = [pl.BlockSpec((D, tk), lambda j, i: (0, i)), pl.BlockSpec((tk, TN), lambda j, i: (i, j))]
    aliases = {}
    if prev is not None:
        ins.append(prev)
        in_specs.append(ANY)
        aliases = {2: 0}
    return pl.pallas_call(
        body, name=name, grid=(ncols, t // tk), out_shape=SDS((D, NCOL), F32), in_specs=in_specs,
        out_specs=pl.BlockSpec((D, TN), lambda j, i: (0, col0 + j)), input_output_aliases=aliases,
        compiler_params=_cp(("arbitrary", "arbitrary"), VMEM_CAP // 2),
    )(*ins)


def _sum_partials(gathered):
    def body(g_ref, o_ref):
        acc = g_ref[0]
        for k in range(1, 8):
            acc = acc + g_ref[k]
        o_ref[...] = acc

    return pl.pallas_call(body, name="sum_partials", out_shape=SDS(gathered.shape[1:], F32), in_specs=[VMEM_SPEC], out_specs=VMEM_SPEC)(gathered)


def _adamw(w, g, m, v, name, tr=256):
    r, cdim = w.shape
    tr = tr if cdim <= D else tr // 2
    tr = tr if (r % tr == 0 and r > tr) else r

    def body(w_ref, g_ref, m_ref, v_ref, d_ref, nm_ref, nv_ref):
        gv = g_ref[...]
        nm = B1 * m_ref[...] + (1.0 - B1) * gv
        nv = B2 * v_ref[...] + (1.0 - B2) * (gv * gv)
        m_hat = nm / (1.0 - B1 ** STEP)
        v_hat = nv / (1.0 - B2 ** STEP)
        d_ref[...] = -LR * (m_hat / (jnp.sqrt(v_hat) + EPS) + WD * w_ref[...])
        nm_ref[...] = nm
        nv_ref[...] = nv

    spec = pl.BlockSpec((tr, cdim), lambda i: (i, 0))
    return pl.pallas_call(
        body, name=name, grid=(r // tr,), out_shape=[SDS((r, cdim), F32)] * 3, in_specs=[spec] * 4, out_specs=[spec] * 3,
        compiler_params=_cp(("parallel",), VMEM_CAP // 2),
    )(w, g, m, v)


def _t5_bucket(dist):
    n = jnp.maximum(dist, 1).astype(F32)
    large = MAX_EXACT + (jnp.log(n / MAX_EXACT) / math.log(MAX_DISTANCE / MAX_EXACT) * (N_BUCKETS - MAX_EXACT)).astype(jnp.int32)
    large = jnp.minimum(large, N_BUCKETS - 1)
    return jnp.where(dist < MAX_EXACT, dist, large)


def _band_buckets():
    a = jnp.arange(BLK)[:, None]
    b = jnp.arange(2 * BLK)[None, :]
    steps = jnp.maximum(a + BLK - b, 0)
    return jnp.stack([_t5_bucket(steps * d) for d in DILATIONS]).astype(jnp.int32)


def _pad_rows(a, rows=8):
    return jnp.pad(a, ((0, rows - a.shape[0]), (0, 0)))


def kernel(x, c, w_ada, b_ada, w_in, conv_w, conv_b, rel_bias, w_attn_out, w_conv_out, w_o, ln_g, ln_b, loss_target, m_w_ada, m_b_ada, m_w_in, m_conv_w, m_conv_b, m_rel_bias, m_w_attn_out, m_w_conv_out, m_w_o, m_ln_g, m_ln_b, v_w_ada, v_b_ada, v_w_in, v_conv_w, v_conv_b, v_rel_bias, v_w_attn_out, v_w_conv_out, v_w_o, v_ln_g, v_ln_b):
    bsz, seq, _ = x.shape
    t = bsz * seq
    mx, my, mc = _place()
    chip = 2 * mx + my
    dev = 4 * mx + 2 * my + mc
    x2 = x.reshape(t, D)
    tgt = loss_target.reshape(t, D)

    mine = [_to_bf16_window(a, w[0], f"to_bf16_{a}") for a, w in enumerate((w_in, w_attn_out, w_conv_out, w_o))]

    n_ada = w_ada.shape[2]
    n_cw = conv_w.shape[2]
    c_and_cw = jnp.concatenate([_pad_rows(c), jnp.pad(conv_w[0], ((0, 5), (0, D - n_cw)))], axis=0)
    firsts = _all_gather8(c_and_cw, "gather_c_conv_w")
    c_all = firsts[:, 0:bsz, :].reshape(8 * bsz, D)
    conv_w_f = firsts[0::2, 8:11, 0:n_cw].transpose(1, 0, 2).reshape(3, D)
    b_cols = lax.dynamic_slice(b_ada, (0, chip * n_ada), (1, n_ada))
    mod_part = _ada_forward(c_all, w_ada[0], b_cols)
    mod_parts = _all_gather8(mod_part, "gather_mod")
    mod_all = mod_parts[0::2].transpose(1, 0, 2).reshape(8 * bsz, 3 * D)
    mod = lax.dynamic_slice(mod_all, (dev * bsz, 0), (bsz, 3 * D))
    shift = mod[:, 0:D].reshape(bsz, 1, D)
    sc1p = 1.0 + mod[:, D:2 * D].reshape(bsz, 1, D)
    gate = mod[:, 2 * D:].reshape(bsz, 1, D)

    h, ht = _modulate(x2, sc1p, shift, seq)
    tab = lax.dynamic_index_in_dim(jnp.asarray(_tile_tables()), chip, 0, keepdims=False)
    qkv, gates, (w_in_f, w_ao_f, w_co_f, w_o_f) = _project_gather(h, mine, tab)
    buckets = _band_buckets()
    bias = _bias_tables(rel_bias, buckets)
    og, lg = [], []
    for g in range(3):
        o_g, l_g = _attn_forward(g, qkv, bias[g], bsz, seq)
        og.append(o_g)
        lg.append(l_g)
    (a_in, s_in, merged, dy, a_out, s_out, y_conv, o, lj, dxr, vec_f, dgate) = _mix_forward(
        gates, og, lg, x2, tgt, gate, w_ao_f, w_co_f, w_o_f, conv_w_f, conv_b, ln_g, ln_b, bsz, seq)

    dgates, do, dl, da_out, ds_out, vec_b = _mix_backward(gates, dy, a_out, s_out, y_conv, o, lj, w_ao_f, w_co_f, w_o_f, conv_w_f, bsz, seq)
    g_ao, g_co, g_o = _out_weight_grads(a_in, da_out, s_in, ds_out, merged, dy)
    dqkv, dbs = None, []
    for g in range(3):
        dqkv, db = _attn_backward(g, qkv, do, dl, bias[g], dqkv, bsz, seq)
        dbs.append(db)
    dq, dk, dv = dqkv
    drb = _bias_grad(jnp.stack(dbs), buckets)
    drb = drb[:, :, 0:4].transpose(1, 0, 2).reshape(N_BUCKETS, 12)
    g_in = None
    for n, src in enumerate((dq, dk, dv, dgates)):
        g_in = _in_weight_grad(ht, src, n * NQT, g_in, f"in_weight_grad_{n}")

    grads = [g_in, g_ao, g_co, g_o]
    got = _swap_halves(grads)
    sums = [_chip_sum(a, grads[a], got[a], f"chip_sum_{a}") for a in range(4)]
    grad_x, dshift, dscale, landed = _input_grad(dq, dk, dv, dgates, w_in_f, x2, dxr, sc1p, seq, [s[1] for s in sums])
    halves = [_reduce_mine(a, sums[a][0], landed[a], f"reduce_mine_{a}") for a in range(4)]
    gw_in, gw_ao, gw_co, gw_o = _join_halves(halves)

    dmod = jnp.concatenate([dshift, dscale, dgate], axis=2).reshape(bsz * 3, D)
    drb_row = jnp.pad(drb.reshape(1, N_BUCKETS * 12), ((0, 0), (0, D - N_BUCKETS * 12)))
    packed = jnp.concatenate([vec_f, vec_b, _pad_rows(dmod), _pad_rows(drb_row)], axis=0)
    gathered = _all_gather8(packed, "gather_small")
    small = _sum_partials(gathered)
    g_ln_g, g_ln_b, loss_lanes = small[0:1], small[1:2], small[2:3]
    g_conv_w_full, g_conv_b = small[8:11], small[11:12]
    g_rel_bias = small[24, 0:N_BUCKETS * 12].reshape(N_BUCKETS, 12)
    loss = 0.5 / D * jnp.sum(loss_lanes)
    dmod_all = gathered[:, 16:16 + 3 * bsz, :].reshape(8 * bsz, 3 * D)
    dmod_cols = lax.dynamic_slice(dmod_all, (0, chip * n_ada), (8 * bsz, n_ada))
    gw_ada, gb_ada = _ada_backward(c_all, dmod_cols, dmod_all)
    g_conv_w = lax.dynamic_slice(g_conv_w_full, (0, chip * n_cw), (3, n_cw))

    names = ["w_ada", "b_ada", "w_in", "conv_w", "conv_b", "rel_bias", "w_attn_out", "w_conv_out", "w_o", "ln_g", "ln_b"]
    two_d = lambda a: a.reshape(a.shape[-2:]) if a.ndim == 3 else a
    weights = dict(zip(names, map(two_d, (w_ada, b_ada, w_in, conv_w, conv_b, rel_bias, w_attn_out, w_conv_out, w_o, ln_g, ln_b))))
    ms = dict(zip(names, map(two_d, (m_w_ada, m_b_ada, m_w_in, m_conv_w, m_conv_b, m_rel_bias, m_w_attn_out, m_w_conv_out, m_w_o, m_ln_g, m_ln_b))))
    vs = dict(zip(names, map(two_d, (v_w_ada, v_b_ada, v_w_in, v_conv_w, v_conv_b, v_rel_bias, v_w_attn_out, v_w_conv_out, v_w_o, v_ln_g, v_ln_b))))
    grads = dict(zip(names, (gw_ada, gb_ada, gw_in, g_conv_w, g_conv_b, g_rel_bias, gw_ao, gw_co, gw_o, g_ln_g, g_ln_b)))
    shapes = dict(zip(names, (w_ada, b_ada, w_in, conv_w, conv_b, rel_bias, w_attn_out, w_conv_out, w_o, ln_g, ln_b)))
    deltas, new_m, new_v = {}, {}, {}
    for n in names:
        deltas[n], new_m[n], new_v[n] = _adamw(weights[n], grads[n], ms[n], vs[n], f"adamw_{n}")
    shaped = lambda d: [d[n].reshape(shapes[n].shape) for n in names]
    return (loss, grad_x.reshape(bsz, seq, D), *shaped(grads), *shaped(deltas), *shaped(new_m), *shaped(new_v))
```

```python
import math

import numpy as np
import jax
import jax.numpy as jnp
from jax import lax
from jax.experimental import pallas as pl
from jax.experimental.pallas import tpu as pltpu

F32 = jnp.float32
BF16 = jnp.bfloat16
SDS = jax.ShapeDtypeStruct
MESH = pl.DeviceIdType.MESH
ANY = pl.BlockSpec(memory_space=pl.ANY)
VMEM_SPEC = pl.BlockSpec(memory_space=pltpu.VMEM)

D = 1024
HD = 128
BLK = 128
QW = 1536
AW = 512
NGATE = 6656
GATE_COLS = ((0, 512), (512, 1536), (1536, 2560), (2560, 3584), (3584, 4608), (4608, 5632), (5632, 6656))
NCOL = 3 * QW + NGATE
TN = 512
NQT = QW // TN
NPT = NCOL // TN
DILATIONS = (1, 4, 16)
N_BUCKETS, MAX_EXACT, MAX_DISTANCE = 32, 16, 2048
ALPHA = 2.0 ** 0.25
LN_EPS = 1e-5
NEG = -1e30
SCALE = HD ** -0.5
LR, B1, B2, EPS, WD, STEP = 0.001, 0.9, 0.999, 1e-08, 0.01, 10
NCHIP = 4
VMEM_CAP = 60 * 2 ** 20


def _cp(sem=None, vmem=None, side=False):
    return pltpu.CompilerParams(dimension_semantics=sem, vmem_limit_bytes=vmem, has_side_effects=side)


def _dot(a, b):
    return jnp.dot(a, b, preferred_element_type=F32)


def _dot_nt(a, b):
    return lax.dot_general(a, b, (((1,), (1,)), ((), ())), preferred_element_type=F32)


def _dot_tn(a, b):
    return lax.dot_general(a, b, (((0,), (0,)), ((), ())), preferred_element_type=F32)


def _sig(x):
    return 1.0 / (1.0 + jnp.exp(-x))


def _place():
    x, y, c = lax.axis_index("x"), lax.axis_index("y"), lax.axis_index("c")
    return x, y, c


def _all_gather8(v, name):
    r, cdim = v.shape

    def body(v_ref, out_ref, send_sems, recv_sems, local_sem):
        x, y, c = _place()
        me = 4 * x + 2 * y + c
        peers = [(x, y, 1 - c), (1 - x, y, c), (x, 1 - y, c), (1 - x, 1 - y, c),
                 (1 - x, y, 1 - c), (x, 1 - y, 1 - c), (1 - x, 1 - y, 1 - c)]
        mine = pltpu.make_async_copy(v_ref, out_ref.at[me], local_sem)
        mine.start()

        def copy(k, block, to):
            return pltpu.make_async_remote_copy(src_ref=v_ref, dst_ref=out_ref.at[block], send_sem=send_sems.at[k],
                                                recv_sem=recv_sems.at[k], device_id=to, device_id_type=MESH)

        sends = [copy(k, me, p) for k, p in enumerate(peers)]
        for cp in sends:
            cp.start()
        for k, (px, py, pc) in enumerate(peers):
            copy(k, 4 * px + 2 * py + pc, (px, py, pc)).wait_recv()
        for cp in sends:
            cp.wait_send()
        mine.wait()

    return pl.pallas_call(
        body, name=name, out_shape=SDS((8, r, cdim), v.dtype), in_specs=[VMEM_SPEC], out_specs=VMEM_SPEC,
        scratch_shapes=[pltpu.SemaphoreType.DMA((7,)), pltpu.SemaphoreType.DMA((7,)), pltpu.SemaphoreType.DMA(())],
        compiler_params=_cp(side=True),
    )(v)


W_CUTS = (("col", D, NCOL // NCHIP), ("col", AW, D // NCHIP), ("row", D // NCHIP, D), ("row", D // NCHIP, D))
W_FULL = ((D, NCOL), (AW, D), (D, D), (D, D))


def _shard_window(ref, cut, k, half):
    kind, nr, nc = cut
    hr = nr // 2
    if kind == "col":
        rows = pl.ds(0, nr) if half is None else pl.ds(pl.multiple_of(half * hr, 16), hr)
        return ref.at[rows, pl.ds(pl.multiple_of(k * nc, 128), nc)]
    if half is None:
        return ref.at[pl.ds(pl.multiple_of(k * nr, 16), nr), :]
    return ref.at[pl.ds(pl.multiple_of(k * nr + half * hr, 16), hr), :]


def _half_rows(ref, cut, half):
    hr = cut[1] // 2
    return ref.at[pl.ds(pl.multiple_of(half * hr, 16), hr), :]


def _to_bf16_window(a, w, name):
    kind, nr, nc = W_CUTS[a]
    x, y, _ = _place()
    chip = jnp.reshape(2 * x + y, (1,)).astype(jnp.int32)
    tr = min(nr, 256)

    def body(c_ref, w_ref, o_ref):
        o_ref[...] = w_ref[...].astype(BF16)

    out_map = (lambda i, cr: (i, cr[0])) if kind == "col" else (lambda i, cr: (cr[0] * (nr // tr) + i, 0))
    return pl.pallas_call(
        body, name=name, out_shape=SDS(W_FULL[a], BF16),
        grid_spec=pltpu.PrefetchScalarGridSpec(num_scalar_prefetch=1, grid=(nr // tr,),
                                               in_specs=[pl.BlockSpec((tr, nc), lambda i, cr: (i, 0))], out_specs=pl.BlockSpec((tr, nc), out_map)),
        compiler_params=_cp(("arbitrary",)),
    )(chip, w)


def _swap_halves(grads):
    n = len(grads)
    shapes = []
    for a in range(n):
        kind, nr, nc = W_CUTS[a]
        shapes.append((W_FULL[a][0] // 2, W_FULL[a][1]) if kind == "col" else (NCHIP, nr // 2, nc))

    def pieces(a, ref, land, half):
        kind, nr, nc = W_CUTS[a]
        if kind == "col":
            hr = nr // 2
            return [(ref.at[pl.ds(pl.multiple_of(half * hr, 16), hr), :], land)]
        return [(_shard_window(ref, W_CUTS[a], k, half), land.at[k]) for k in range(NCHIP)]

    def body(*refs):
        src, land = refs[:n], refs[n:2 * n]
        send_sems, recv_sems = refs[2 * n:]
        x, y, c = _place()
        sibling = (x, y, 1 - c)
        sends = []
        k = 0
        for a in range(n):
            for s, d in pieces(a, src[a], land[a], 1 - c):
                cp = pltpu.make_async_remote_copy(src_ref=s, dst_ref=d, send_sem=send_sems.at[k], recv_sem=recv_sems.at[k],
                                                  device_id=sibling, device_id_type=MESH)
                cp.start()
                sends.append(cp)
                k += 1
        for cp in sends:
            cp.wait()

    n_sems = sum(1 if W_CUTS[a][0] == "col" else NCHIP for a in range(n))
    return pl.pallas_call(
        body, name="swap_grad_halves", out_shape=[SDS(s, F32) for s in shapes], in_specs=[ANY] * n, out_specs=[ANY] * n,
        scratch_shapes=[pltpu.SemaphoreType.DMA((n_sems,)), pltpu.SemaphoreType.DMA((n_sems,))],
        compiler_params=_cp(side=True),
    )(*grads)


def _chip_sum(a, grad, got, name):
    kind, nr, nc = W_CUTS[a]
    hr = nr // 2
    c = lax.axis_index("c")
    cidx = jnp.reshape(c, (1,)).astype(jnp.int32)

    def body(c_ref, g_ref, r_ref, f_ref, b_ref):
        s = g_ref[...] + r_ref[...]
        f_ref[...] = s.reshape(f_ref.shape)
        b_ref[...] = s.astype(BF16).reshape(b_ref.shape)

    if kind == "col":
        in_specs = [pl.BlockSpec((hr, nc), lambda k, cr: (cr[0], k)), pl.BlockSpec((hr, nc), lambda k, cr: (0, k))]
    else:
        grad = grad.reshape(NCHIP, 2, hr, nc)
        in_specs = [pl.BlockSpec((1, 1, hr, nc), lambda k, cr: (k, cr[0], 0, 0)), pl.BlockSpec((1, hr, nc), lambda k, cr: (k, 0, 0))]
    out_specs = [pl.BlockSpec((1, hr, nc), lambda k, cr: (k, 0, 0))] * 2
    return pl.pallas_call(
        body, name=name, out_shape=[SDS((NCHIP, hr, nc), F32), SDS((NCHIP, hr, nc), BF16)],
        grid_spec=pltpu.PrefetchScalarGridSpec(num_scalar_prefetch=1, grid=(NCHIP,), in_specs=in_specs, out_specs=out_specs),
        compiler_params=_cp(("arbitrary",), VMEM_CAP),
    )(cidx, grad, got)


def _reduce_mine(a, mine_f32, got, name):
    kind, nr, nc = W_CUTS[a]
    hr = nr // 2
    x, y, c = _place()
    where = jnp.stack([2 * x + y, c]).astype(jnp.int32)
    tr = min(hr, 256)

    def body(w_ref, m_ref, g_ref, o_ref):
        o_ref[...] = ((m_ref[0] + g_ref[0].astype(F32)) + g_ref[1].astype(F32)) + g_ref[2].astype(F32)

    return pl.pallas_call(
        body, name=name, out_shape=SDS((nr, nc), F32),
        grid_spec=pltpu.PrefetchScalarGridSpec(
            num_scalar_prefetch=1, grid=(hr // tr,),
            in_specs=[pl.BlockSpec((1, tr, nc), lambda i, wr: (wr[0], i, 0)), pl.BlockSpec((3, tr, nc), lambda i, wr: (0, i, 0))],
            out_specs=pl.BlockSpec((tr, nc), lambda i, wr: (wr[1] * (hr // tr) + i, 0))),
        compiler_params=_cp(("arbitrary",), VMEM_CAP),
    )(where, mine_f32, got)


def _join_halves(fulls):
    n = len(fulls)

    def body(*refs):
        full = refs[n:2 * n]
        send_sems, recv_sems = refs[2 * n:]
        x, y, c = _place()
        sibling = (x, y, 1 - c)

        def swap(a, half):
            rows = _half_rows(full[a], W_CUTS[a], half)
            return pltpu.make_async_remote_copy(src_ref=rows, dst_ref=rows, send_sem=send_sems.at[a], recv_sem=recv_sems.at[a],
                                                device_id=sibling, device_id_type=MESH)

        sends = [swap(a, c) for a in range(n)]
        for cp in sends:
            cp.start()
        for a, cp in enumerate(sends):
            cp.wait_send()
            swap(a, 1 - c).wait_recv()

    return pl.pallas_call(
        body, name="join_grad_halves", out_shape=[SDS((W_CUTS[a][1], W_CUTS[a][2]), F32) for a in range(n)],
        in_specs=[ANY] * n, out_specs=[ANY] * n,
        scratch_shapes=[pltpu.SemaphoreType.DMA((n,)), pltpu.SemaphoreType.DMA((n,))],
        input_output_aliases={a: a for a in range(n)}, compiler_params=_cp(side=True),
    )(*fulls)


def _ada_forward(c_all, w_ada, b_cols):
    nb, nc = c_all.shape[0], w_ada.shape[1]

    def body(c_ref, w_ref, b_ref, o_ref):
        cv = c_ref[...]
        sc = (cv * _sig(cv)).astype(BF16)
        o_ref[...] = _dot(sc, w_ref[...].astype(BF16)) + b_ref[...]

    return pl.pallas_call(body, name="ada_forward", out_shape=SDS((nb, nc), F32), compiler_params=_cp(vmem=VMEM_CAP // 2))(c_all, w_ada, b_cols)


def _ada_backward(c_all, dmod_cols, dmod_all):
    nb, nc = dmod_cols.shape

    def body(c_ref, d_ref, a_ref, gw_ref, gb_ref):
        cv = c_ref[...]
        sc = (cv * _sig(cv)).astype(BF16)
        gw_ref[...] = _dot_tn(sc, d_ref[...].astype(BF16))
        gb_ref[...] = jnp.sum(a_ref[...], axis=0, keepdims=True)

    return pl.pallas_call(body, name="ada_backward", out_shape=[SDS((D, nc), F32), SDS((1, dmod_all.shape[1]), F32)],
                          compiler_params=_cp(vmem=VMEM_CAP // 2))(c_all, dmod_cols, dmod_all)


def _modulate(x2, sc1p, shift, seq, tm=256):
    t = x2.shape[0]
    spt = seq // tm

    def body(x_ref, sc_ref, sh_ref, h_ref, ht_ref):
        h = x_ref[...] * sc_ref[0] + sh_ref[0]
        h_ref[...] = h.astype(BF16)
        ht_ref[...] = h.T.astype(BF16)

    per_seq = pl.BlockSpec((1, 1, D), lambda i: (i // spt, 0, 0))
    return pl.pallas_call(
        body, name="modulate", out_shape=[SDS((t, D), BF16), SDS((D, t), BF16)], grid=(t // tm,),
        in_specs=[pl.BlockSpec((tm, D), lambda i: (i, 0)), per_seq, per_seq],
        out_specs=[pl.BlockSpec((tm, D), lambda i: (i, 0)), pl.BlockSpec((D, tm), lambda i: (0, i))],
        compiler_params=_cp(("parallel",)),
    )(x2, sc1p, shift)


TW = 256
TPS = NCOL // NCHIP // TW
NT = NCOL // TW
NQKV_T = 3 * QW // TW


def _tile_tables():
    tabs = np.zeros((NCHIP, 3, NT), np.int32)
    for me in range(NCHIP):
        tiles = [TPS * (me ^ (s // TPS)) + s % TPS for s in range(NT)]
        tabs[me, 0] = tiles
        for row, (lo, hi) in enumerate(((0, NQKV_T), (NQKV_T, NT))):
            mine = [w - lo if lo <= w < hi else None for w in tiles]
            held = next(m for m in mine if m is not None)
            for s, m in enumerate(mine):
                held = held if m is None else m
                tabs[me, 1 + row, s] = held
    return tabs


def _project_gather(h, fulls, tab):
    t = h.shape[0]
    n = len(fulls)
    chunk = 1024

    def body(tab_ref, h_ref, *rest):
        qkv_ref, g_ref = rest[n], rest[n + 1]
        full = rest[n + 2:2 * n + 2]
        w_buf, tile_sems, send_sems, recv_sems = rest[2 * n + 2:]
        s = pl.program_id(0)
        x, y, c = _place()
        me = 2 * x + y
        peers = [(x, 1 - y), (1 - x, y), (1 - x, 1 - y)]
        sibling = (x, y, 1 - c)

        def hop(a, r, stage, chip, half, to):
            window = _shard_window(full[a], W_CUTS[a], chip, half)
            k = 6 * a + 2 * r + stage
            return pltpu.make_async_remote_copy(src_ref=window, dst_ref=window, send_sem=send_sems.at[k], recv_sem=recv_sems.at[k],
                                                device_id=to, device_id_type=MESH)

        def send(a, r):
            return hop(a, r, 0, me, c, (*peers[r], c))

        def arrive(a, r):
            px, py = peers[r]
            chip = 2 * px + py
            hop(a, r, 0, chip, c, (px, py, c)).wait_recv()
            hop(a, r, 1, chip, c, sibling).start()
            hop(a, r, 1, chip, 1 - c, sibling).wait_recv()

        def tile(step, slot):
            col = pl.multiple_of(tab_ref[0, step] * TW, TW)
            return pltpu.make_async_copy(full[0].at[:, pl.ds(col, TW)], w_buf.at[slot], tile_sems.at[slot])

        @pl.when(s == 0)
        def _():
            send(0, 0).start()
            send(0, 1).start()
            tile(0, 0).start()

        slot = s % 2
        tile(s, slot).wait()

        @pl.when((s + 1 < NT) & ((s + 1) % TPS != 0))
        def _():
            tile(s + 1, 1 - slot).start()

        w = w_buf[slot]
        is_qkv = tab_ref[0, s] < NQKV_T
        for i in range(t // chunk):
            rows = pl.ds(i * chunk, chunk)
            acc = _dot(h_ref[rows, :], w)

            @pl.when(is_qkv)
            def _(rows=rows, acc=acc):
                qkv_ref[rows, :] = acc

            @pl.when(jnp.logical_not(is_qkv))
            def _(rows=rows, acc=acc):
                g_ref[rows, :] = acc.astype(BF16)

        for r in range(3):
            @pl.when(s + 1 == TPS * (r + 1))
            def _(r=r):
                arrive(0, r)
                tile(s + 1, 1 - slot).start()
                if r == 0:
                    send(0, 2).start()
                    for a in range(1, n):
                        for q in range(3):
                            send(a, q).start()

        @pl.when(s == NT - 1)
        def _():
            for a in range(1, n):
                for r in range(3):
                    arrive(a, r)
            for a in range(n):
                for r in range(3):
                    send(a, r).wait_send()
                    px, py = peers[r]
                    hop(a, r, 1, 2 * px + py, c, sibling).wait_send()

    outs = pl.pallas_call(
        body, name="project_gather", out_shape=[SDS((t, 3 * QW), F32), SDS((t, NGATE), BF16)] + [SDS(s, BF16) for s in W_FULL],
        grid_spec=pltpu.PrefetchScalarGridSpec(
            num_scalar_prefetch=1, grid=(NT,),
            in_specs=[pl.BlockSpec((t, D), lambda s, tab: (0, 0))] + [ANY] * n,
            out_specs=[pl.BlockSpec((t, TW), lambda s, tab: (0, tab[1, s])), pl.BlockSpec((t, TW), lambda s, tab: (0, tab[2, s]))] + [ANY] * n,
            scratch_shapes=[pltpu.VMEM((2, D, TW), BF16), pltpu.SemaphoreType.DMA((2,)),
                            pltpu.SemaphoreType.DMA((6 * n,)), pltpu.SemaphoreType.DMA((6 * n,))]),
        input_output_aliases={2 + a: 2 + a for a in range(n)},
        compiler_params=_cp(("arbitrary",), VMEM_CAP, side=True),
    )(tab, h, *fulls)
    return outs[0], outs[1], outs[2:]


def _bias_tables(rel_bias, buckets):
    def body(tab_ref, bk_ref, o_ref):
        a = lax.broadcasted_iota(jnp.int32, (BLK, 2 * BLK), 0)
        b = lax.broadcasted_iota(jnp.int32, (BLK, 2 * BLK), 1)
        steps = a + BLK - b
        valid = (steps >= 0) & (steps <= BLK)
        for g in range(3):
            bk = bk_ref[g]
            for j in range(4):
                def pick(kk, acc, bk=bk, col=4 * g + j):
                    return jnp.where(bk == kk, tab_ref[kk, col], acc)

                acc = lax.fori_loop(0, N_BUCKETS, pick, jnp.zeros((BLK, 2 * BLK), F32))
                o_ref[g, j] = jnp.where(valid, acc, NEG)

    return pl.pallas_call(
        body, name="bias_tables", out_shape=SDS((3, 4, BLK, 2 * BLK), F32),
        in_specs=[pl.BlockSpec(memory_space=pltpu.SMEM), VMEM_SPEC], out_specs=VMEM_SPEC,
    )(rel_bias, buckets)


def _bias_grad(ds_sum, buckets):
    def body(ds_ref, bk_ref, o_ref):
        lane = lax.broadcasted_iota(jnp.int32, (1, 128), 1)
        for g in range(3):
            def bucket(kk, carry, g=g):
                row = jnp.zeros((1, 128), F32)
                for j in range(4):
                    v = jnp.where(bk_ref[g] == kk, ds_ref[g, j], 0.0)
                    s = jnp.sum(jnp.sum(v, axis=1, keepdims=True), axis=0, keepdims=True)
                    row = jnp.where(lane == j, s, row)
                o_ref[g, pl.ds(kk, 1), :] = row
                return carry

            lax.fori_loop(0, N_BUCKETS, bucket, 0)

    return pl.pallas_call(body, name="bias_grad", out_shape=SDS((3, N_BUCKETS, 128), F32), in_specs=[VMEM_SPEC, VMEM_SPEC],
                          out_specs=VMEM_SPEC)(ds_sum, buckets)


def _sub_rows(d, r, n):
    return pl.ds(n * BLK * d + r, BLK) if d == 1 else pl.ds(n * BLK * d + r, BLK, stride=d)


def _head_spec(seq, g, part):
    return pl.BlockSpec((seq, HD), lambda b, hh: (b, part * (QW // HD) + 4 * g + hh))


def _attn_forward(g, qkv, bias, bsz, seq):
    d = DILATIONS[g]
    nblk = seq // d // BLK

    def body(q_ref, k_ref, v_ref, b_ref, o_ref, l_ref):
        hs = pl.program_id(1)
        for r in range(d):
            for n in range(nblk):
                rows = _sub_rows(d, r, n)
                qb = q_ref[rows, :].astype(BF16)
                s_c = _dot_nt(qb, k_ref[rows, :].astype(BF16)) * SCALE + b_ref[hs, :, BLK:]
                m = jnp.max(s_c, axis=1, keepdims=True)
                if n > 0:
                    prev = _sub_rows(d, r, n - 1)
                    s_p = _dot_nt(qb, k_ref[prev, :].astype(BF16)) * SCALE + b_ref[hs, :, :BLK]
                    m = jnp.maximum(m, jnp.max(s_p, axis=1, keepdims=True))
                p_c = jnp.exp(s_c - m)
                den = jnp.sum(p_c, axis=1, keepdims=True)
                acc = _dot(p_c.astype(BF16), v_ref[rows, :].astype(BF16))
                if n > 0:
                    p_p = jnp.exp(s_p - m)
                    den = den + jnp.sum(p_p, axis=1, keepdims=True)
                    acc = acc + _dot(p_p.astype(BF16), v_ref[prev, :].astype(BF16))
                o_ref[rows, :] = acc / den
                l_ref[rows, :] = jnp.broadcast_to(m + jnp.log(den), (BLK, HD))

    out_spec = pl.BlockSpec((seq, HD), lambda b, hh: (b, hh))
    return pl.pallas_call(
        body, name=f"attn_forward_{g}", out_shape=[SDS((bsz * seq, AW), F32)] * 2, grid=(bsz, 4),
        in_specs=[_head_spec(seq, g, part) for part in range(3)] + [pl.BlockSpec((4, BLK, 2 * BLK), lambda b, hh: (0, 0, 0))],
        out_specs=[out_spec, out_spec],
        compiler_params=_cp(("parallel", "parallel"), VMEM_CAP // 2),
    )(qkv, qkv, qkv, bias)


def _attn_backward(g, qkv, do, dl, bias, prev_out, bsz, seq):
    d = DILATIONS[g]
    nblk = seq // d // BLK

    def body(q_ref, k_ref, v_ref, do_ref, dl_ref, b_ref, *rest):
        dq_ref, dk_ref, dv_ref, db_ref = rest[-4:]
        hs = pl.program_id(1)

        @pl.when((pl.program_id(0) == 0) & (hs == 0))
        def _():
            db_ref[...] = jnp.zeros_like(db_ref)

        dk_ref[...] = jnp.zeros_like(dk_ref)
        dv_ref[...] = jnp.zeros_like(dv_ref)
        for r in range(d):
            for n in range(nblk):
                rows = _sub_rows(d, r, n)
                qb = q_ref[rows, :].astype(BF16)
                dob = do_ref[rows, :].astype(BF16)
                both = dl_ref[rows, :]
                lse, delta = both[:, 0:1], both[:, 64:65]
                dq = jnp.zeros((BLK, HD), F32)
                parts = [(rows, slice(BLK, 2 * BLK))]
                if n > 0:
                    parts.append((_sub_rows(d, r, n - 1), slice(0, BLK)))
                for keys, band in parts:
                    kb, vb = k_ref[keys, :].astype(BF16), v_ref[keys, :].astype(BF16)
                    p = jnp.exp(_dot_nt(qb, kb) * SCALE + b_ref[hs, :, band] - lse)
                    ds = p * (_dot_nt(dob, vb) - delta)
                    dsb = ds.astype(BF16)
                    dv_ref[keys, :] += _dot_tn(p.astype(BF16), dob)
                    dk_ref[keys, :] += _dot_tn(dsb, qb) * SCALE
                    dq = dq + _dot(dsb, kb) * SCALE
                    db_ref[hs, :, band] += ds
                dq_ref[rows, :] = dq

    qkv_spec = _head_spec(seq, g, 0)
    out_spec = pl.BlockSpec((seq, HD), lambda b, hh: (b, hh))
    band_spec = pl.BlockSpec((4, BLK, 2 * BLK), lambda b, hh: (0, 0, 0))
    ins = [qkv, qkv, qkv, do, dl, bias]
    in_specs = [_head_spec(seq, g, part) for part in range(3)] + [out_spec, out_spec, band_spec]
    aliases = {}
    if prev_out is not None:
        ins += list(prev_out)
        in_specs += [ANY] * 3
        aliases = {6: 0, 7: 1, 8: 2}
    dq, dk, dv, db = pl.pallas_call(
        body, name=f"attn_backward_{g}", out_shape=[SDS((bsz * seq, QW), F32)] * 3 + [SDS((4, BLK, 2 * BLK), F32)], grid=(bsz, 4),
        in_specs=in_specs, out_specs=[qkv_spec] * 3 + [band_spec], input_output_aliases=aliases,
        compiler_params=_cp(("arbitrary", "arbitrary"), VMEM_CAP // 2),
    )(*ins)
    return (dq, dk, dv), db


def _mix_forward(gates, og, lg, x2, tgt, gate, w_ao, w_co, w_o, conv_w, conv_b, ln_g, ln_b, bsz, seq, tm=256):
    t = x2.shape[0]
    spt = seq // tm

    def body(g_ref, o1, o2, o3, l1, l2, l3, x_ref, t_ref, gate_ref, wao_ref, wco_ref, wo_ref, cw_ref, cb_ref, lng_ref, lnb_ref,
             ain_ref, sin_ref, mrg_ref, dy_ref, aout_ref, sout_ref, yc_ref, o_ref, lj_ref, dxr_ref, vec_ref, dgate_ref, zc_ref):
        b, i = pl.program_id(0), pl.program_id(1)

        @pl.when((b == 0) & (i == 0))
        def _():
            vec_ref[...] = jnp.zeros_like(vec_ref)

        @pl.when(i == 0)
        def _():
            zc_ref[...] = jnp.zeros_like(zc_ref)
            dgate_ref[...] = jnp.zeros_like(dgate_ref)

        g_attn, u, bg, cg, g_conv, m_attn, m_conv = (g_ref[:, lo:hi].astype(F32) for lo, hi in GATE_COLS)
        la, lb, lc = l1[...], l2[...], l3[...]
        mx = jnp.maximum(la, jnp.maximum(lb, lc))
        ea, eb, ec = jnp.exp(la - mx), jnp.exp(lb - mx), jnp.exp(lc - mx)
        den = ea + eb + ec
        o = (ea * o1[...] + eb * o2[...] + ec * o3[...]) / den
        o_ref[...] = o
        lj_ref[...] = mx + jnp.log(den)
        a_in = o * (g_attn * _sig(g_attn))
        ain_ref[...] = a_in.astype(BF16)
        a_out = _dot(a_in.astype(BF16), wao_ref[...])
        aout_ref[...] = a_out.astype(BF16)
        z = cg * u
        rows = lax.broadcasted_iota(jnp.int32, (tm, D), 0)
        c6, c7 = zc_ref[6:7, :], zc_ref[7:8, :]
        z1 = jnp.where(rows == 0, c7, pltpu.roll(z, 1, 0))
        z2 = jnp.where(rows == 0, c6, jnp.where(rows == 1, c7, pltpu.roll(z, 2, 0)))
        zc_ref[...] = z[tm - 8:tm, :]
        y_conv = (cw_ref[0:1, :] * z2 + cw_ref[1:2, :] * z1 + cw_ref[2:3, :] * z) + cb_ref[...]
        yc_ref[...] = y_conv.astype(BF16)
        s_in = bg * y_conv * (g_conv * _sig(g_conv))
        sin_ref[...] = s_in.astype(BF16)
        s_out = _dot(s_in.astype(BF16), wco_ref[...])
        sout_ref[...] = s_out.astype(BF16)
        merged = _sig(m_attn) * a_out + _sig(m_conv) * s_out
        mrg_ref[...] = merged.astype(BF16)
        y = _dot(merged.astype(BF16), wo_ref[...])
        gate1 = 1.0 + gate_ref[0]
        r = ALPHA * x_ref[...] + gate1 * y
        mu = jnp.mean(r, axis=1, keepdims=True)
        rc = r - mu
        rstd = lax.rsqrt(jnp.mean(rc * rc, axis=1, keepdims=True) + LN_EPS)
        xhat = rc * rstd
        diff = (xhat * lng_ref[...] + lnb_ref[...]) - t_ref[...]
        dout = diff * (1.0 / D)
        vec_ref[0:1, :] += jnp.sum(dout * xhat, axis=0, keepdims=True)
        vec_ref[1:2, :] += jnp.sum(dout, axis=0, keepdims=True)
        vec_ref[2:3, :] += jnp.sum(diff * diff, axis=0, keepdims=True)
        dxh = dout * lng_ref[...]
        dr = rstd * (dxh - jnp.mean(dxh, axis=1, keepdims=True) - xhat * jnp.mean(dxh * xhat, axis=1, keepdims=True))
        dxr_ref[...] = ALPHA * dr
        dy_ref[...] = (dr * gate1).astype(BF16)
        dgate_ref[0] += jnp.sum(dr * y, axis=0, keepdims=True)

    tok = lambda w: pl.BlockSpec((tm, w), lambda b, i: (b * spt + i, 0))
    const = lambda s: pl.BlockSpec(s, lambda b, i: (0,) * len(s))
    per_seq = pl.BlockSpec((1, 1, D), lambda b, i: (b, 0, 0))
    outs = pl.pallas_call(
        body, name="mix_forward", grid=(bsz, spt),
        out_shape=[SDS((t, AW), BF16), SDS((t, D), BF16), SDS((t, D), BF16), SDS((t, D), BF16), SDS((t, D), BF16), SDS((t, D), BF16),
                   SDS((t, D), BF16), SDS((t, AW), F32), SDS((t, AW), F32), SDS((t, D), F32), SDS((8, D), F32), SDS((bsz, 1, D), F32)],
        in_specs=[tok(NGATE)] + [tok(AW)] * 6 + [tok(D), tok(D), per_seq, const((AW, D)), const((D, D)), const((D, D)),
                                                 const((3, D)), const((1, D)), const((1, D)), const((1, D))],
        out_specs=[tok(AW), tok(D), tok(D), tok(D), tok(D), tok(D), tok(D), tok(AW), tok(AW), tok(D), const((8, D)), per_seq],
        scratch_shapes=[pltpu.VMEM((8, D), F32)],
        compiler_params=_cp(("arbitrary", "arbitrary"), VMEM_CAP),
    )(gates, *og, *lg, x2, tgt, gate, w_ao, w_co, w_o, conv_w, conv_b, ln_g, ln_b)
    return outs


def _mix_backward(gates, dy, a_out, s_out, y_conv, o, lj, w_ao, w_co, w_o, conv_w, bsz, seq, tm=256):
    t = dy.shape[0]
    spt = seq // tm

    def body(g_ref, dy_ref, aout_ref, sout_ref, yc_ref, o_ref, lj_ref, wao_ref, wco_ref, wo_ref, cw_ref,
             dg_ref, do_ref, dl_ref, daout_ref, dsout_ref, vec_ref, car_ref):
        b, i = pl.program_id(0), pl.program_id(1)

        @pl.when((b == 0) & (i == 0))
        def _():
            vec_ref[...] = jnp.zeros_like(vec_ref)

        @pl.when(i == 0)
        def _():
            car_ref[...] = jnp.zeros_like(car_ref)

        g_attn, u, bg, cg, g_conv, m_attn, m_conv = (g_ref[:, lo:hi].astype(F32) for lo, hi in GATE_COLS)
        dmerged = _dot_nt(dy_ref[...], wo_ref[...])
        sa, sc = _sig(m_attn), _sig(m_conv)
        da_out = (dmerged * sa).astype(BF16)
        ds_out = (dmerged * sc).astype(BF16)
        daout_ref[...] = da_out
        dsout_ref[...] = ds_out
        dg_ref[:, 4608:5632] = (dmerged * aout_ref[...].astype(F32) * (sa * (1.0 - sa))).astype(BF16)
        dg_ref[:, 5632:6656] = (dmerged * sout_ref[...].astype(F32) * (sc * (1.0 - sc))).astype(BF16)
        da_in = _dot_nt(da_out, wao_ref[...])
        ds_in = _dot_nt(ds_out, wco_ref[...])
        sga = _sig(g_attn)
        o = o_ref[...]
        do = da_in * (g_attn * sga)
        do_ref[...] = do
        dg_ref[:, 0:512] = (da_in * o * (sga * (1.0 + g_attn * (1.0 - sga)))).astype(BF16)
        prod = do * o
        lane = lax.broadcasted_iota(jnp.int32, (tm, HD), 1)
        for j in range(4):
            cs = slice(j * HD, (j + 1) * HD)
            delta = jnp.sum(prod[:, cs], axis=1, keepdims=True)
            dl_ref[:, cs] = jnp.where(lane < 64, lj_ref[:, cs], delta)
        sgc = _sig(g_conv)
        silu_c = g_conv * sgc
        yc = yc_ref[...].astype(F32)
        dg_ref[:, 1536:2560] = (ds_in * yc * silu_c).astype(BF16)
        dg_ref[:, 3584:4608] = (ds_in * bg * yc * (sgc * (1.0 + g_conv * (1.0 - sgc)))).astype(BF16)
        dyc = ds_in * bg * silu_c
        rows = lax.broadcasted_iota(jnp.int32, (tm, D), 0)
        c0, c1 = car_ref[0:1, :], car_ref[1:2, :]
        n1 = jnp.where(rows == tm - 1, c0, pltpu.roll(dyc, tm - 1, 0))
        n2 = jnp.where(rows == tm - 2, c0, jnp.where(rows == tm - 1, c1, pltpu.roll(dyc, tm - 2, 0)))
        car_ref[...] = dyc[0:8, :]
        dz = cw_ref[2:3, :] * dyc + cw_ref[1:2, :] * n1 + cw_ref[0:1, :] * n2
        z = cg * u
        dg_ref[:, 512:1536] = (dz * cg).astype(BF16)
        dg_ref[:, 2560:3584] = (dz * u).astype(BF16)
        vec_ref[0:1, :] += jnp.sum(n2 * z, axis=0, keepdims=True)
        vec_ref[1:2, :] += jnp.sum(n1 * z, axis=0, keepdims=True)
        vec_ref[2:3, :] += jnp.sum(dyc * z, axis=0, keepdims=True)
        vec_ref[3:4, :] += jnp.sum(dyc, axis=0, keepdims=True)

    tok = lambda w: pl.BlockSpec((tm, w), lambda b, i: (b * spt + (spt - 1 - i), 0))
    const = lambda s: pl.BlockSpec(s, lambda b, i: (0,) * len(s))
    return pl.pallas_call(
        body, name="mix_backward", grid=(bsz, spt),
        out_shape=[SDS((t, NGATE), BF16), SDS((t, AW), F32), SDS((t, AW), F32), SDS((t, D), BF16), SDS((t, D), BF16), SDS((8, D), F32)],
        in_specs=[tok(NGATE), tok(D), tok(D), tok(D), tok(D), tok(AW), tok(AW), const((AW, D)), const((D, D)), const((D, D)), const((3, D))],
        out_specs=[tok(NGATE), tok(AW), tok(AW), tok(D), tok(D), const((8, D))],
        scratch_shapes=[pltpu.VMEM((8, D), F32)],
        compiler_params=_cp(("arbitrary", "arbitrary"), VMEM_CAP),
    )(gates, dy, a_out, s_out, y_conv, o, lj, w_ao, w_co, w_o, conv_w)


def _out_weight_grads(a_in, da_out, s_in, ds_out, merged, dy, tk=512):
    t = dy.shape[0]

    def body(ain_ref, da_ref, sin_ref, ds_ref, m_ref, dy_ref, gao_ref, gco_ref, go_ref):
        @pl.when(pl.program_id(0) == 0)
        def _():
            gao_ref[...] = jnp.zeros_like(gao_ref)
            gco_ref[...] = jnp.zeros_like(gco_ref)
            go_ref[...] = jnp.zeros_like(go_ref)

        gao_ref[...] += _dot_tn(ain_ref[...], da_ref[...])
        gco_ref[...] += _dot_tn(sin_ref[...], ds_ref[...])
        go_ref[...] += _dot_tn(m_ref[...], dy_ref[...])

    tok = lambda w: pl.BlockSpec((tk, w), lambda i: (i, 0))
    const = lambda s: pl.BlockSpec(s, lambda i: (0, 0))
    return pl.pallas_call(
        body, name="out_weight_grads", grid=(t // tk,), out_shape=[SDS((AW, D), F32), SDS((D, D), F32), SDS((D, D), F32)],
        in_specs=[tok(AW), tok(D), tok(D), tok(D), tok(D), tok(D)], out_specs=[const((AW, D)), const((D, D)), const((D, D))],
        compiler_params=_cp(("arbitrary",), VMEM_CAP),
    )(a_in, da_out, s_in, ds_out, merged, dy)


def _input_grad(dq, dk, dv, dgates, w, x2, dxr, sc1p, seq, sums, tm=1024):
    t = x2.shape[0]
    spt = seq // tm
    bsz = t // seq
    n = len(sums)

    def body(dq_ref, dk_ref, dv_ref, dg_ref, w_ref, x_ref, dxr_ref, sc_ref, *rest):
        src, (dx_ref, dsh_ref, dsc_ref), land = rest[:n], rest[n:n + 3], rest[n + 3:2 * n + 3]
        acc_ref, send_sems, recv_sems = rest[2 * n + 3:]
        i, j = pl.program_id(0), pl.program_id(1)
        px, py, pc = _place()
        chips = [(1 - px, py), (px, 1 - py), (1 - px, 1 - py)]
        copies = [pltpu.make_async_remote_copy(src_ref=src[a].at[2 * cx + cy], dst_ref=land[a].at[r], send_sem=send_sems.at[3 * a + r],
                                               recv_sem=recv_sems.at[3 * a + r], device_id=(cx, cy, pc), device_id_type=MESH)
                  for a in range(n) for r, (cx, cy) in enumerate(chips)]

        @pl.when((i == 0) & (j == 0))
        def _():
            for cp in copies:
                cp.start()

        @pl.when(j == 0)
        def _():
            acc_ref[...] = jnp.zeros_like(acc_ref)

        for k, ref in enumerate((dq_ref, dk_ref, dv_ref)):
            @pl.when((j >= k * NQT) & (j < (k + 1) * NQT))
            def _(ref=ref):
                acc_ref[...] += _dot_nt(ref[...].astype(BF16), w_ref[...])

        @pl.when(j >= 3 * NQT)
        def _():
            acc_ref[...] += _dot_nt(dg_ref[...], w_ref[...])

        @pl.when(j == NPT - 1)
        def _():
            dh = acc_ref[...]
            dx_ref[...] = dh * sc_ref[0] + dxr_ref[...]

            @pl.when(i % spt == 0)
            def _():
                dsh_ref[...] = jnp.zeros_like(dsh_ref)
                dsc_ref[...] = jnp.zeros_like(dsc_ref)

            dsh_ref[0] += jnp.sum(dh, axis=0, keepdims=True)
            dsc_ref[0] += jnp.sum(dh * x_ref[...], axis=0, keepdims=True)

        @pl.when((i == t // tm - 1) & (j == NPT - 1))
        def _():
            for cp in copies:
                cp.wait()

    def part(k):
        return pl.BlockSpec((tm, TN), lambda i, j: (i, jnp.clip(j - k * NQT, 0, NQT - 1)))

    row = pl.BlockSpec((tm, D), lambda i, j: (i, 0))
    per_seq = pl.BlockSpec((1, 1, D), lambda i, j: (i // spt, 0, 0))
    outs = pl.pallas_call(
        body, name="input_grad", grid=(t // tm, NPT),
        out_shape=[SDS((t, D), F32), SDS((bsz, 1, D), F32), SDS((bsz, 1, D), F32)] + [SDS((3,) + s.shape[1:], BF16) for s in sums],
        in_specs=[part(0), part(1), part(2), pl.BlockSpec((tm, TN), lambda i, j: (i, jnp.maximum(j - 3 * NQT, 0))),
                  pl.BlockSpec((D, TN), lambda i, j: (0, j)), row, row, per_seq] + [ANY] * n,
        out_specs=[row, per_seq, per_seq] + [ANY] * n,
        scratch_shapes=[pltpu.VMEM((tm, D), F32), pltpu.SemaphoreType.DMA((3 * NCHIP,)), pltpu.SemaphoreType.DMA((3 * NCHIP,))],
        compiler_params=_cp(("arbitrary", "arbitrary"), VMEM_CAP, side=True),
    )(dq, dk, dv, dgates, w, x2, dxr, sc1p, *sums)
    return outs[0], outs[1], outs[2], outs[3:]


def _in_weight_grad(ht, src, col0, prev, name, tk=1024):
    t = ht.shape[1]
    ncols = src.shape[1] // TN

    def body(ht_ref, s_ref, *rest):
        o_ref = rest[-1]

        @pl.when(pl.program_id(1) == 0)
        def _():
            o_ref[...] = jnp.zeros_like(o_ref)

        o_ref[...] += _dot(ht_ref[...], s_ref[...].astype(BF16))

    ins = [ht, src]
    in_specs = [pl.BlockSpec((D, tk), lambda j, i: (0, i)), pl.BlockSpec((tk, TN), lambda j, i: (i, j))]
    aliases = {}
    if prev is not None:
        ins.append(prev)
        in_specs.append(ANY)
        aliases = {2: 0}
    return pl.pallas_call(
        body, name=name, grid=(ncols, t // tk), out_shape=SDS((D, NCOL), F32), in_specs=in_specs,
        out_specs=pl.BlockSpec((D, TN), lambda j, i: (0, col0 + j)), input_output_aliases=aliases,
        compiler_params=_cp(("arbitrary", "arbitrary"), VMEM_CAP // 2),
    )(*ins)


def _sum_partials(gathered):
    def body(g_ref, o_ref):
        acc = g_ref[0]
        for k in range(1, 8):
            acc = acc + g_ref[k]
        o_ref[...] = acc

    return pl.pallas_call(body, name="sum_partials", out_shape=SDS(gathered.shape[1:], F32), in_specs=[VMEM_SPEC], out_specs=VMEM_SPEC)(gathered)


def _adamw(w, g, m, v, name, tr=256):
    r, cdim = w.shape
    tr = tr if cdim <= D else tr // 2
    tr = tr if (r % tr == 0 and r > tr) else r

    def body(w_ref, g_ref, m_ref, v_ref, d_ref, nm_ref, nv_ref):
        gv = g_ref[...]
        nm = B1 * m_ref[...] + (1.0 - B1) * gv
        nv = B2 * v_ref[...] + (1.0 - B2) * (gv * gv)
        m_hat = nm / (1.0 - B1 ** STEP)
        v_hat = nv / (1.0 - B2 ** STEP)
        d_ref[...] = -LR * (m_hat / (jnp.sqrt(v_hat) + EPS) + WD * w_ref[...])
        nm_ref[...] = nm
        nv_ref[...] = nv

    spec = pl.BlockSpec((tr, cdim), lambda i: (i, 0))
    return pl.pallas_call(
        body, name=name, grid=(r // tr,), out_shape=[SDS((r, cdim), F32)] * 3, in_specs=[spec] * 4, out_specs=[spec] * 3,
        compiler_params=_cp(("parallel",), VMEM_CAP // 2),
    )(w, g, m, v)


def _t5_bucket(dist):
    n = jnp.maximum(dist, 1).astype(F32)
    large = MAX_EXACT + (jnp.log(n / MAX_EXACT) / math.log(MAX_DISTANCE / MAX_EXACT) * (N_BUCKETS - MAX_EXACT)).astype(jnp.int32)
    large = jnp.minimum(large, N_BUCKETS - 1)
    return jnp.where(dist < MAX_EXACT, dist, large)


def _band_buckets():
    a = jnp.arange(BLK)[:, None]
    b = jnp.arange(2 * BLK)[None, :]
    steps = jnp.maximum(a + BLK - b, 0)
    return jnp.stack([_t5_bucket(steps * d) for d in DILATIONS]).astype(jnp.int32)


def _pad_rows(a, rows=8):
    return jnp.pad(a, ((0, rows - a.shape[0]), (0, 0)))


def kernel(x, c, w_ada, b_ada, w_in, conv_w, conv_b, rel_bias, w_attn_out, w_conv_out, w_o, ln_g, ln_b, loss_target, m_w_ada, m_b_ada, m_w_in, m_conv_w, m_conv_b, m_rel_bias, m_w_attn_out, m_w_conv_out, m_w_o, m_ln_g, m_ln_b, v_w_ada, v_b_ada, v_w_in, v_conv_w, v_conv_b, v_rel_bias, v_w_attn_out, v_w_conv_out, v_w_o, v_ln_g, v_ln_b):
    bsz, seq, _ = x.shape
    t = bsz * seq
    mx, my, mc = _place()
    chip = 2 * mx + my
    dev = 4 * mx + 2 * my + mc
    x2 = x.reshape(t, D)
    tgt = loss_target.reshape(t, D)

    mine = [_to_bf16_window(a, w[0], f"to_bf16_{a}") for a, w in enumerate((w_in, w_attn_out, w_conv_out, w_o))]

    n_ada = w_ada.shape[2]
    n_cw = conv_w.shape[2]
    c_and_cw = jnp.concatenate([_pad_rows(c), jnp.pad(conv_w[0], ((0, 5), (0, D - n_cw)))], axis=0)
    firsts = _all_gather8(c_and_cw, "gather_c_conv_w")
    c_all = firsts[:, 0:bsz, :].reshape(8 * bsz, D)
    conv_w_f = firsts[0::2, 8:11, 0:n_cw].transpose(1, 0, 2).reshape(3, D)
    b_cols = lax.dynamic_slice(b_ada, (0, chip * n_ada), (1, n_ada))
    mod_part = _ada_forward(c_all, w_ada[0], b_cols)
    mod_parts = _all_gather8(mod_part, "gather_mod")
    mod_all = mod_parts[0::2].transpose(1, 0, 2).reshape(8 * bsz, 3 * D)
    mod = lax.dynamic_slice(mod_all, (dev * bsz, 0), (bsz, 3 * D))
    shift = mod[:, 0:D].reshape(bsz, 1, D)
    sc1p = 1.0 + mod[:, D:2 * D].reshape(bsz, 1, D)
    gate = mod[:, 2 * D:].reshape(bsz, 1, D)

    h, ht = _modulate(x2, sc1p, shift, seq)
    tab = lax.dynamic_index_in_dim(jnp.asarray(_tile_tables()), chip, 0, keepdims=False)
    qkv, gates, (w_in_f, w_ao_f, w_co_f, w_o_f) = _project_gather(h, mine, tab)
    buckets = _band_buckets()
    bias = _bias_tables(rel_bias, buckets)
    og, lg = [], []
    for g in range(3):
        o_g, l_g = _attn_forward(g, qkv, bias[g], bsz, seq)
        og.append(o_g)
        lg.append(l_g)
    (a_in, s_in, merged, dy, a_out, s_out, y_conv, o, lj, dxr, vec_f, dgate) = _mix_forward(
        gates, og, lg, x2, tgt, gate, w_ao_f, w_co_f, w_o_f, conv_w_f, conv_b, ln_g, ln_b, bsz, seq)

    dgates, do, dl, da_out, ds_out, vec_b = _mix_backward(gates, dy, a_out, s_out, y_conv, o, lj, w_ao_f, w_co_f, w_o_f, conv_w_f, bsz, seq)
    g_ao, g_co, g_o = _out_weight_grads(a_in, da_out, s_in, ds_out, merged, dy)
    dqkv, dbs = None, []
    for g in range(3):
        dqkv, db = _attn_backward(g, qkv, do, dl, bias[g], dqkv, bsz, seq)
        dbs.append(db)
    dq, dk, dv = dqkv
    drb = _bias_grad(jnp.stack(dbs), buckets)
    drb = drb[:, :, 0:4].transpose(1, 0, 2).reshape(N_BUCKETS, 12)
    g_in = None
    for n, src in enumerate((dq, dk, dv, dgates)):
        g_in = _in_weight_grad(ht, src, n * NQT, g_in, f"in_weight_grad_{n}")

    grads = [g_in, g_ao, g_co, g_o]
    got = _swap_halves(grads)
    sums = [_chip_sum(a, grads[a], got[a], f"chip_sum_{a}") for a in range(4)]
    grad_x, dshift, dscale, landed = _input_grad(dq, dk, dv, dgates, w_in_f, x2, dxr, sc1p, seq, [s[1] for s in sums])
    halves = [_reduce_mine(a, sums[a][0], landed[a], f"reduce_mine_{a}") for a in range(4)]
    gw_in, gw_ao, gw_co, gw_o = _join_halves(halves)

    dmod = jnp.concatenate([dshift, dscale, dgate], axis=2).reshape(bsz * 3, D)
    drb_row = jnp.pad(drb.reshape(1, N_BUCKETS * 12), ((0, 0), (0, D - N_BUCKETS * 12)))
    packed = jnp.concatenate([vec_f, vec_b, _pad_rows(dmod), _pad_rows(drb_row)], axis=0)
    gathered = _all_gather8(packed, "gather_small")
    small = _sum_partials(gathered)
    g_ln_g, g_ln_b, loss_lanes = small[0:1], small[1:2], small[2:3]
    g_conv_w_full, g_conv_b = small[8:11], small[11:12]
    g_rel_bias = small[24, 0:N_BUCKETS * 12].reshape(N_BUCKETS, 12)
    loss = 0.5 / D * jnp.sum(loss_lanes)
    dmod_all = gathered[:, 16:16 + 3 * bsz, :].reshape(8 * bsz, 3 * D)
    dmod_cols = lax.dynamic_slice(dmod_all, (0, chip * n_ada), (8 * bsz, n_ada))
    gw_ada, gb_ada = _ada_backward(c_all, dmod_cols, dmod_all)
    g_conv_w = lax.dynamic_slice(g_conv_w_full, (0, chip * n_cw), (3, n_cw))

    names = ["w_ada", "b_ada", "w_in", "conv_w", "conv_b", "rel_bias", "w_attn_out", "w_conv_out", "w_o", "ln_g", "ln_b"]
    two_d = lambda a: a.reshape(a.shape[-2:]) if a.ndim == 3 else a
    weights = dict(zip(names, map(two_d, (w_ada, b_ada, w_in, conv_w, conv_b, rel_bias, w_attn_out, w_conv_out, w_o, ln_g, ln_b))))
    ms = dict(zip(names, map(two_d, (m_w_ada, m_b_ada, m_w_in, m_conv_w, m_conv_b, m_rel_bias, m_w_attn_out, m_w_conv_out, m_w_o, m_ln_g, m_ln_b))))
    vs = dict(zip(names, map(two_d, (v_w_ada, v_b_ada, v_w_in, v_conv_w, v_conv_b, v_rel_bias, v_w_attn_out, v_w_conv_out, v_w_o, v_ln_g, v_ln_b))))
    grads = dict(zip(names, (gw_ada, gb_ada, gw_in, g_conv_w, g_conv_b, g_rel_bias, gw_ao, gw_co, gw_o, g_ln_g, g_ln_b)))
    shapes = dict(zip(names, (w_ada, b_ada, w_in, conv_w, conv_b, rel_bias, w_attn_out, w_conv_out, w_o, ln_g, ln_b)))
    deltas, new_m, new_v = {}, {}, {}
    for n in names:
        deltas[n], new_m[n], new_v[n] = _adamw(weights[n], grads[n], ms[n], vs[n], f"adamw_{n}")
    shaped = lambda d: [d[n].reshape(shapes[n].shape) for n in names]
    return (loss, grad_x.reshape(bsz, seq, D), *shaped(grads), *shaped(deltas), *shaped(new_m), *shaped(new_v))
```

```python
import math

import numpy as np
import jax
import jax.numpy as jnp
from jax import lax
from jax.experimental import pallas as pl
from jax.experimental.pallas import tpu as pltpu

F32 = jnp.float32
BF16 = jnp.bfloat16
SDS = jax.ShapeDtypeStruct
MESH = pl.DeviceIdType.MESH
ANY = pl.BlockSpec(memory_space=pl.ANY)
VMEM_SPEC = pl.BlockSpec(memory_space=pltpu.VMEM)

D = 1024
HD = 128
BLK = 128
QW = 1536
AW = 512
NGATE = 6656
GATE_COLS = ((0, 512), (512, 1536), (1536, 2560), (2560, 3584), (3584, 4608), (4608, 5632), (5632, 6656))
NCOL = 3 * QW + NGATE
TN = 512
NQT = QW // TN
NPT = NCOL // TN
DILATIONS = (1, 4, 16)
N_BUCKETS, MAX_EXACT, MAX_DISTANCE = 32, 16, 2048
ALPHA = 2.0 ** 0.25
LN_EPS = 1e-5
NEG = -1e30
SCALE = HD ** -0.5
LR, B1, B2, EPS, WD, STEP = 0.001, 0.9, 0.999, 1e-08, 0.01, 10
NCHIP = 4
VMEM_CAP = 60 * 2 ** 20


def _cp(sem=None, vmem=None, side=False):
    return pltpu.CompilerParams(dimension_semantics=sem, vmem_limit_bytes=vmem, has_side_effects=side)


def _dot(a, b):
    return jnp.dot(a, b, preferred_element_type=F32)


def _dot_nt(a, b):
    return lax.dot_general(a, b, (((1,), (1,)), ((), ())), preferred_element_type=F32)


def _dot_tn(a, b):
    return lax.dot_general(a, b, (((0,), (0,)), ((), ())), preferred_element_type=F32)


def _sig(x):
    return 1.0 / (1.0 + jnp.exp(-x))


def _place():
    x, y, c = lax.axis_index("x"), lax.axis_index("y"), lax.axis_index("c")
    return x, y, c


def _all_gather8(v, name):
    r, cdim = v.shape

    def body(v_ref, out_ref, send_sems, recv_sems, local_sem):
        x, y, c = _place()
        me = 4 * x + 2 * y + c
        peers = [(x, y, 1 - c), (1 - x, y, c), (x, 1 - y, c), (1 - x, 1 - y, c),
                 (1 - x, y, 1 - c), (x, 1 - y, 1 - c), (1 - x, 1 - y, 1 - c)]
        mine = pltpu.make_async_copy(v_ref, out_ref.at[me], local_sem)
        mine.start()

        def copy(k, block, to):
            return pltpu.make_async_remote_copy(src_ref=v_ref, dst_ref=out_ref.at[block], send_sem=send_sems.at[k],
                                                recv_sem=recv_sems.at[k], device_id=to, device_id_type=MESH)

        sends = [copy(k, me, p) for k, p in enumerate(peers)]
        for cp in sends:
            cp.start()
        for k, (px, py, pc) in enumerate(peers):
            copy(k, 4 * px + 2 * py + pc, (px, py, pc)).wait_recv()
        for cp in sends:
            cp.wait_send()
        mine.wait()

    return pl.pallas_call(
        body, name=name, out_shape=SDS((8, r, cdim), v.dtype), in_specs=[VMEM_SPEC], out_specs=VMEM_SPEC,
        scratch_shapes=[pltpu.SemaphoreType.DMA((7,)), pltpu.SemaphoreType.DMA((7,)), pltpu.SemaphoreType.DMA(())],
        compiler_params=_cp(side=True),
    )(v)


W_CUTS = (("col", D, NCOL // NCHIP), ("col", AW, D // NCHIP), ("row", D // NCHIP, D), ("row", D // NCHIP, D))
W_FULL = ((D, NCOL), (AW, D), (D, D), (D, D))


def _shard_window(ref, cut, k, half):
    kind, nr, nc = cut
    hr = nr // 2
    if kind == "col":
        rows = pl.ds(0, nr) if half is None else pl.ds(pl.multiple_of(half * hr, 16), hr)
        return ref.at[rows, pl.ds(pl.multiple_of(k * nc, 128), nc)]
    if half is None:
        return ref.at[pl.ds(pl.multiple_of(k * nr, 16), nr), :]
    return ref.at[pl.ds(pl.multiple_of(k * nr + half * hr, 16), hr), :]


def _half_rows(ref, cut, half):
    hr = cut[1] // 2
    return ref.at[pl.ds(pl.multiple_of(half * hr, 16), hr), :]


def _to_bf16_window(a, w, name):
    kind, nr, nc = W_CUTS[a]
    x, y, _ = _place()
    chip = jnp.reshape(2 * x + y, (1,)).astype(jnp.int32)
    tr = min(nr, 256)

    def body(c_ref, w_ref, o_ref):
        o_ref[...] = w_ref[...].astype(BF16)

    out_map = (lambda i, cr: (i, cr[0])) if kind == "col" else (lambda i, cr: (cr[0] * (nr // tr) + i, 0))
    return pl.pallas_call(
        body, name=name, out_shape=SDS(W_FULL[a], BF16),
        grid_spec=pltpu.PrefetchScalarGridSpec(num_scalar_prefetch=1, grid=(nr // tr,),
                                               in_specs=[pl.BlockSpec((tr, nc), lambda i, cr: (i, 0))], out_specs=pl.BlockSpec((tr, nc), out_map)),
        compiler_params=_cp(("arbitrary",)),
    )(chip, w)


def _swap_halves(grads):
    n = len(grads)
    shapes = []
    for a in range(n):
        kind, nr, nc = W_CUTS[a]
        shapes.append((W_FULL[a][0] // 2, W_FULL[a][1]) if kind == "col" else (NCHIP, nr // 2, nc))

    def pieces(a, ref, land, half):
        kind, nr, nc = W_CUTS[a]
        if kind == "col":
            hr = nr // 2
            return [(ref.at[pl.ds(pl.multiple_of(half * hr, 16), hr), :], land)]
        return [(_shard_window(ref, W_CUTS[a], k, half), land.at[k]) for k in range(NCHIP)]

    def body(*refs):
        src, land = refs[:n], refs[n:2 * n]
        send_sems, recv_sems = refs[2 * n:]
        x, y, c = _place()
        sibling = (x, y, 1 - c)
        sends = []
        k = 0
        for a in range(n):
            for s, d in pieces(a, src[a], land[a], 1 - c):
                cp = pltpu.make_async_remote_copy(src_ref=s, dst_ref=d, send_sem=send_sems.at[k], recv_sem=recv_sems.at[k],
                                                  device_id=sibling, device_id_type=MESH)
                cp.start()
                sends.append(cp)
                k += 1
        for cp in sends:
            cp.wait()

    n_sems = sum(1 if W_CUTS[a][0] == "col" else NCHIP for a in range(n))
    return pl.pallas_call(
        body, name="swap_grad_halves", out_shape=[SDS(s, F32) for s in shapes], in_specs=[ANY] * n, out_specs=[ANY] * n,
        scratch_shapes=[pltpu.SemaphoreType.DMA((n_sems,)), pltpu.SemaphoreType.DMA((n_sems,))],
        compiler_params=_cp(side=True),
    )(*grads)


def _chip_sum(a, grad, got, name):
    kind, nr, nc = W_CUTS[a]
    hr = nr // 2
    c = lax.axis_index("c")
    cidx = jnp.reshape(c, (1,)).astype(jnp.int32)

    def body(c_ref, g_ref, r_ref, f_ref, b_ref):
        s = g_ref[...] + r_ref[...]
        f_ref[...] = s.reshape(f_ref.shape)
        b_ref[...] = s.astype(BF16).reshape(b_ref.shape)

    if kind == "col":
        in_specs = [pl.BlockSpec((hr, nc), lambda k, cr: (cr[0], k)), pl.BlockSpec((hr, nc), lambda k, cr: (0, k))]
    else:
        grad = grad.reshape(NCHIP, 2, hr, nc)
        in_specs = [pl.BlockSpec((1, 1, hr, nc), lambda k, cr: (k, cr[0], 0, 0)), pl.BlockSpec((1, hr, nc), lambda k, cr: (k, 0, 0))]
    out_specs = [pl.BlockSpec((1, hr, nc), lambda k, cr: (k, 0, 0))] * 2
    return pl.pallas_call(
        body, name=name, out_shape=[SDS((NCHIP, hr, nc), F32), SDS((NCHIP, hr, nc), BF16)],
        grid_spec=pltpu.PrefetchScalarGridSpec(num_scalar_prefetch=1, grid=(NCHIP,), in_specs=in_specs, out_specs=out_specs),
        compiler_params=_cp(("arbitrary",), VMEM_CAP),
    )(cidx, grad, got)


def _reduce_mine(a, mine_f32, got, name):
    kind, nr, nc = W_CUTS[a]
    hr = nr // 2
    x, y, c = _place()
    where = jnp.stack([2 * x + y, c]).astype(jnp.int32)
    tr = min(hr, 256)

    def body(w_ref, m_ref, g_ref, o_ref):
        o_ref[...] = ((m_ref[0] + g_ref[0].astype(F32)) + g_ref[1].astype(F32)) + g_ref[2].astype(F32)

    return pl.pallas_call(
        body, name=name, out_shape=SDS((nr, nc), F32),
        grid_spec=pltpu.PrefetchScalarGridSpec(
            num_scalar_prefetch=1, grid=(hr // tr,),
            in_specs=[pl.BlockSpec((1, tr, nc), lambda i, wr: (wr[0], i, 0)), pl.BlockSpec((3, tr, nc), lambda i, wr: (0, i, 0))],
            out_specs=pl.BlockSpec((tr, nc), lambda i, wr: (wr[1] * (hr // tr) + i, 0))),
        compiler_params=_cp(("arbitrary",), VMEM_CAP),
    )(where, mine_f32, got)


def _join_halves(fulls):
    n = len(fulls)

    def body(*refs):
        full = refs[n:2 * n]
        send_sems, recv_sems = refs[2 * n:]
        x, y, c = _place()
        sibling = (x, y, 1 - c)

        def swap(a, half):
            rows = _half_rows(full[a], W_CUTS[a], half)
            return pltpu.make_async_remote_copy(src_ref=rows, dst_ref=rows, send_sem=send_sems.at[a], recv_sem=recv_sems.at[a],
                                                device_id=sibling, device_id_type=MESH)

        sends = [swap(a, c) for a in range(n)]
        for cp in sends:
            cp.start()
        for a, cp in enumerate(sends):
            cp.wait_send()
            swap(a, 1 - c).wait_recv()

    return pl.pallas_call(
        body, name="join_grad_halves", out_shape=[SDS((W_CUTS[a][1], W_CUTS[a][2]), F32) for a in range(n)],
        in_specs=[ANY] * n, out_specs=[ANY] * n,
        scratch_shapes=[pltpu.SemaphoreType.DMA((n,)), pltpu.SemaphoreType.DMA((n,))],
        input_output_aliases={a: a for a in range(n)}, compiler_params=_cp(side=True),
    )(*fulls)


def _ada_forward(c_all, w_ada, b_cols):
    nb, nc = c_all.shape[0], w_ada.shape[1]

    def body(c_ref, w_ref, b_ref, o_ref):
        cv = c_ref[...]
        sc = (cv * _sig(cv)).astype(BF16)
        o_ref[...] = _dot(sc, w_ref[...].astype(BF16)) + b_ref[...]

    return pl.pallas_call(body, name="ada_forward", out_shape=SDS((nb, nc), F32), compiler_params=_cp(vmem=VMEM_CAP // 2))(c_all, w_ada, b_cols)


def _ada_backward(c_all, dmod_cols, dmod_all):
    nb, nc = dmod_cols.shape

    def body(c_ref, d_ref, a_ref, gw_ref, gb_ref):
        cv = c_ref[...]
        sc = (cv * _sig(cv)).astype(BF16)
        gw_ref[...] = _dot_tn(sc, d_ref[...].astype(BF16))
        gb_ref[...] = jnp.sum(a_ref[...], axis=0, keepdims=True)

    return pl.pallas_call(body, name="ada_backward", out_shape=[SDS((D, nc), F32), SDS((1, dmod_all.shape[1]), F32)],
                          compiler_params=_cp(vmem=VMEM_CAP // 2))(c_all, dmod_cols, dmod_all)


def _modulate(x2, sc1p, shift, seq, tm=256):
    t = x2.shape[0]
    spt = seq // tm

    def body(x_ref, sc_ref, sh_ref, h_ref, ht_ref):
        h = x_ref[...] * sc_ref[0] + sh_ref[0]
        h_ref[...] = h.astype(BF16)
        ht_ref[...] = h.T.astype(BF16)

    per_seq = pl.BlockSpec((1, 1, D), lambda i: (i // spt, 0, 0))
    return pl.pallas_call(
        body, name="modulate", out_shape=[SDS((t, D), BF16), SDS((D, t), BF16)], grid=(t // tm,),
        in_specs=[pl.BlockSpec((tm, D), lambda i: (i, 0)), per_seq, per_seq],
        out_specs=[pl.BlockSpec((tm, D), lambda i: (i, 0)), pl.BlockSpec((D, tm), lambda i: (0, i))],
        compiler_params=_cp(("parallel",)),
    )(x2, sc1p, shift)


TW = 256
TPS = NCOL // NCHIP // TW
NT = NCOL // TW
NQKV_T = 3 * QW // TW


def _tile_tables():
    tabs = np.zeros((NCHIP, 3, NT), np.int32)
    for me in range(NCHIP):
        tiles = [TPS * (me ^ (s // TPS)) + s % TPS for s in range(NT)]
        tabs[me, 0] = tiles
        for row, (lo, hi) in enumerate(((0, NQKV_T), (NQKV_T, NT))):
            mine = [w - lo if lo <= w < hi else None for w in tiles]
            held = next(m for m in mine if m is not None)
            for s, m in enumerate(mine):
                held = held if m is None else m
                tabs[me, 1 + row, s] = held
    return tabs


def _project_gather(h, fulls, tab):
    t = h.shape[0]
    n = len(fulls)

    def body(tab_ref, h_ref, *rest):
        qkv_ref, g_ref = rest[n], rest[n + 1]
        full = rest[n + 2:2 * n + 2]
        w_buf, tile_sems, send_sems, recv_sems = rest[2 * n + 2:]
        s = pl.program_id(0)
        x, y, c = _place()
        me = 2 * x + y
        peers = [(x, 1 - y), (1 - x, y), (1 - x, 1 - y)]
        sibling = (x, y, 1 - c)

        def hop(a, r, stage, chip, half, to):
            window = _shard_window(full[a], W_CUTS[a], chip, half)
            k = 6 * a + 2 * r + stage
            return pltpu.make_async_remote_copy(src_ref=window, dst_ref=window, send_sem=send_sems.at[k], recv_sem=recv_sems.at[k],
                                                device_id=to, device_id_type=MESH)

        def send(a, r):
            return hop(a, r, 0, me, c, (*peers[r], c))

        def arrive(a, r):
            px, py = peers[r]
            chip = 2 * px + py
            hop(a, r, 0, chip, c, (px, py, c)).wait_recv()
            hop(a, r, 1, chip, c, sibling).start()
            hop(a, r, 1, chip, 1 - c, sibling).wait_recv()

        def tile(step, slot):
            col = pl.multiple_of(tab_ref[0, step] * TW, TW)
            return pltpu.make_async_copy(full[0].at[:, pl.ds(col, TW)], w_buf.at[slot], tile_sems.at[slot])

        @pl.when(s == 0)
        def _():
            send(0, 0).start()
            send(0, 1).start()
            tile(0, 0).start()

        slot = s % 2
        tile(s, slot).wait()

        @pl.when((s + 1 < NT) & ((s + 1) % TPS != 0))
        def _():
            tile(s + 1, 1 - slot).start()

        is_qkv = tab_ref[0, s] < NQKV_T
        for k in range(2):
            @pl.when(slot == k)
            def _(k=k):
                acc = _dot(h_ref[...], w_buf[k])

                @pl.when(is_qkv)
                def _():
                    qkv_ref[...] = acc

                @pl.when(jnp.logical_not(is_qkv))
                def _():
                    g_ref[...] = acc.astype(BF16)

        for r in range(3):
            @pl.when(s + 1 == TPS * (r + 1))
            def _(r=r):
                arrive(0, r)
                tile(s + 1, 1 - slot).start()
                if r == 0:
                    send(0, 2).start()
                    for a in range(1, n):
                        for q in range(3):
                            send(a, q).start()

        @pl.when(s == NT - 1)
        def _():
            for a in range(1, n):
                for r in range(3):
                    arrive(a, r)
            for a in range(n):
                for r in range(3):
                    send(a, r).wait_send()
                    px, py = peers[r]
                    hop(a, r, 1, 2 * px + py, c, sibling).wait_send()

    outs = pl.pallas_call(
        body, name="project_gather", out_shape=[SDS((t, 3 * QW), F32), SDS((t, NGATE), BF16)] + [SDS(s, BF16) for s in W_FULL],
        grid_spec=pltpu.PrefetchScalarGridSpec(
            num_scalar_prefetch=1, grid=(NT,),
            in_specs=[pl.BlockSpec((t, D), lambda s, tab: (0, 0))] + [ANY] * n,
            out_specs=[pl.BlockSpec((t, TW), lambda s, tab: (0, tab[1, s])), pl.BlockSpec((t, TW), lambda s, tab: (0, tab[2, s]))] + [ANY] * n,
            scratch_shapes=[pltpu.VMEM((2, D, TW), BF16), pltpu.SemaphoreType.DMA((2,)),
                            pltpu.SemaphoreType.DMA((6 * n,)), pltpu.SemaphoreType.DMA((6 * n,))]),
        input_output_aliases={2 + a: 2 + a for a in range(n)},
        compiler_params=_cp(("arbitrary",), VMEM_CAP, side=True),
    )(tab, h, *fulls)
    return outs[0], outs[1], outs[2:]


def _bias_tables(rel_bias, buckets):
    def body(tab_ref, bk_ref, o_ref):
        a = lax.broadcasted_iota(jnp.int32, (BLK, 2 * BLK), 0)
        b = lax.broadcasted_iota(jnp.int32, (BLK, 2 * BLK), 1)
        steps = a + BLK - b
        valid = (steps >= 0) & (steps <= BLK)
        for g in range(3):
            bk = bk_ref[g]
            for j in range(4):
                def pick(kk, acc, bk=bk, col=4 * g + j):
                    return jnp.where(bk == kk, tab_ref[kk, col], acc)

                acc = lax.fori_loop(0, N_BUCKETS, pick, jnp.zeros((BLK, 2 * BLK), F32))
                o_ref[g, j] = jnp.where(valid, acc, NEG)

    return pl.pallas_call(
        body, name="bias_tables", out_shape=SDS((3, 4, BLK, 2 * BLK), F32),
        in_specs=[pl.BlockSpec(memory_space=pltpu.SMEM), VMEM_SPEC], out_specs=VMEM_SPEC,
    )(rel_bias, buckets)


def _bias_grad(ds_sum, buckets):
    def body(ds_ref, bk_ref, o_ref):
        lane = lax.broadcasted_iota(jnp.int32, (1, 128), 1)
        for g in range(3):
            def bucket(kk, carry, g=g):
                row = jnp.zeros((1, 128), F32)
                for j in range(4):
                    v = jnp.where(bk_ref[g] == kk, ds_ref[g, j], 0.0)
                    s = jnp.sum(jnp.sum(v, axis=1, keepdims=True), axis=0, keepdims=True)
                    row = jnp.where(lane == j, s, row)
                o_ref[g, pl.ds(kk, 1), :] = row
                return carry

            lax.fori_loop(0, N_BUCKETS, bucket, 0)

    return pl.pallas_call(body, name="bias_grad", out_shape=SDS((3, N_BUCKETS, 128), F32), in_specs=[VMEM_SPEC, VMEM_SPEC],
                          out_specs=VMEM_SPEC)(ds_sum, buckets)


def _sub_rows(d, r, n):
    return pl.ds(n * BLK * d + r, BLK) if d == 1 else pl.ds(n * BLK * d + r, BLK, stride=d)


def _head_spec(seq, g, part):
    return pl.BlockSpec((seq, HD), lambda b, hh: (b, part * (QW // HD) + 4 * g + hh))


def _attn_forward(g, qkv, bias, bsz, seq):
    d = DILATIONS[g]
    nblk = seq // d // BLK

    def body(q_ref, k_ref, v_ref, b_ref, o_ref, l_ref):
        hs = pl.program_id(1)
        for r in range(d):
            for n in range(nblk):
                rows = _sub_rows(d, r, n)
                qb = q_ref[rows, :].astype(BF16)
                s_c = _dot_nt(qb, k_ref[rows, :].astype(BF16)) * SCALE + b_ref[hs, :, BLK:]
                m = jnp.max(s_c, axis=1, keepdims=True)
                if n > 0:
                    prev = _sub_rows(d, r, n - 1)
                    s_p = _dot_nt(qb, k_ref[prev, :].astype(BF16)) * SCALE + b_ref[hs, :, :BLK]
                    m = jnp.maximum(m, jnp.max(s_p, axis=1, keepdims=True))
                p_c = jnp.exp(s_c - m)
                den = jnp.sum(p_c, axis=1, keepdims=True)
                acc = _dot(p_c.astype(BF16), v_ref[rows, :].astype(BF16))
                if n > 0:
                    p_p = jnp.exp(s_p - m)
                    den = den + jnp.sum(p_p, axis=1, keepdims=True)
                    acc = acc + _dot(p_p.astype(BF16), v_ref[prev, :].astype(BF16))
                o_ref[rows, :] = acc / den
                l_ref[rows, :] = jnp.broadcast_to(m + jnp.log(den), (BLK, HD))

    out_spec = pl.BlockSpec((seq, HD), lambda b, hh: (b, hh))
    return pl.pallas_call(
        body, name=f"attn_forward_{g}", out_shape=[SDS((bsz * seq, AW), F32)] * 2, grid=(bsz, 4),
        in_specs=[_head_spec(seq, g, part) for part in range(3)] + [pl.BlockSpec((4, BLK, 2 * BLK), lambda b, hh: (0, 0, 0))],
        out_specs=[out_spec, out_spec],
        compiler_params=_cp(("parallel", "parallel"), VMEM_CAP // 2),
    )(qkv, qkv, qkv, bias)


def _attn_backward(g, qkv, do, dl, bias, prev_out, bsz, seq):
    d = DILATIONS[g]
    nblk = seq // d // BLK

    def body(q_ref, k_ref, v_ref, do_ref, dl_ref, b_ref, *rest):
        dq_ref, dk_ref, dv_ref, db_ref = rest[-4:]
        hs = pl.program_id(1)

        @pl.when((pl.program_id(0) == 0) & (hs == 0))
        def _():
            db_ref[...] = jnp.zeros_like(db_ref)

        dk_ref[...] = jnp.zeros_like(dk_ref)
        dv_ref[...] = jnp.zeros_like(dv_ref)
        for r in range(d):
            for n in range(nblk):
                rows = _sub_rows(d, r, n)
                qb = q_ref[rows, :].astype(BF16)
                dob = do_ref[rows, :].astype(BF16)
                both = dl_ref[rows, :]
                lse, delta = both[:, 0:1], both[:, 64:65]
                dq = jnp.zeros((BLK, HD), F32)
                parts = [(rows, slice(BLK, 2 * BLK))]
                if n > 0:
                    parts.append((_sub_rows(d, r, n - 1), slice(0, BLK)))
                for keys, band in parts:
                    kb, vb = k_ref[keys, :].astype(BF16), v_ref[keys, :].astype(BF16)
                    p = jnp.exp(_dot_nt(qb, kb) * SCALE + b_ref[hs, :, band] - lse)
                    ds = p * (_dot_nt(dob, vb) - delta)
                    dsb = ds.astype(BF16)
                    dv_ref[keys, :] += _dot_tn(p.astype(BF16), dob)
                    dk_ref[keys, :] += _dot_tn(dsb, qb) * SCALE
                    dq = dq + _dot(dsb, kb) * SCALE
                    db_ref[hs, :, band] += ds
                dq_ref[rows, :] = dq

    qkv_spec = _head_spec(seq, g, 0)
    out_spec = pl.BlockSpec((seq, HD), lambda b, hh: (b, hh))
    band_spec = pl.BlockSpec((4, BLK, 2 * BLK), lambda b, hh: (0, 0, 0))
    ins = [qkv, qkv, qkv, do, dl, bias]
    in_specs = [_head_spec(seq, g, part) for part in range(3)] + [out_spec, out_spec, band_spec]
    aliases = {}
    if prev_out is not None:
        ins += list(prev_out)
        in_specs += [ANY] * 3
        aliases = {6: 0, 7: 1, 8: 2}
    dq, dk, dv, db = pl.pallas_call(
        body, name=f"attn_backward_{g}", out_shape=[SDS((bsz * seq, QW), F32)] * 3 + [SDS((4, BLK, 2 * BLK), F32)], grid=(bsz, 4),
        in_specs=in_specs, out_specs=[qkv_spec] * 3 + [band_spec], input_output_aliases=aliases,
        compiler_params=_cp(("arbitrary", "arbitrary"), VMEM_CAP // 2),
    )(*ins)
    return (dq, dk, dv), db


def _mix_forward(gates, og, lg, x2, tgt, gate, w_ao, w_co, w_o, conv_w, conv_b, ln_g, ln_b, bsz, seq, tm=256):
    t = x2.shape[0]
    spt = seq // tm

    def body(g_ref, o1, o2, o3, l1, l2, l3, x_ref, t_ref, gate_ref, wao_ref, wco_ref, wo_ref, cw_ref, cb_ref, lng_ref, lnb_ref,
             ain_ref, sin_ref, mrg_ref, dy_ref, aout_ref, sout_ref, yc_ref, o_ref, lj_ref, dxr_ref, vec_ref, dgate_ref, zc_ref):
        b, i = pl.program_id(0), pl.program_id(1)

        @pl.when((b == 0) & (i == 0))
        def _():
            vec_ref[...] = jnp.zeros_like(vec_ref)

        @pl.when(i == 0)
        def _():
            zc_ref[...] = jnp.zeros_like(zc_ref)
            dgate_ref[...] = jnp.zeros_like(dgate_ref)

        g_attn, u, bg, cg, g_conv, m_attn, m_conv = (g_ref[:, lo:hi].astype(F32) for lo, hi in GATE_COLS)
        la, lb, lc = l1[...], l2[...], l3[...]
        mx = jnp.maximum(la, jnp.maximum(lb, lc))
        ea, eb, ec = jnp.exp(la - mx), jnp.exp(lb - mx), jnp.exp(lc - mx)
        den = ea + eb + ec
        o = (ea * o1[...] + eb * o2[...] + ec * o3[...]) / den
        o_ref[...] = o
        lj_ref[...] = mx + jnp.log(den)
        a_in = o * (g_attn * _sig(g_attn))
        ain_ref[...] = a_in.astype(BF16)
        a_out = _dot(a_in.astype(BF16), wao_ref[...])
        aout_ref[...] = a_out.astype(BF16)
        z = cg * u
        rows = lax.broadcasted_iota(jnp.int32, (tm, D), 0)
        c6, c7 = zc_ref[6:7, :], zc_ref[7:8, :]
        z1 = jnp.where(rows == 0, c7, pltpu.roll(z, 1, 0))
        z2 = jnp.where(rows == 0, c6, jnp.where(rows == 1, c7, pltpu.roll(z, 2, 0)))
        zc_ref[...] = z[tm - 8:tm, :]
        y_conv = (cw_ref[0:1, :] * z2 + cw_ref[1:2, :] * z1 + cw_ref[2:3, :] * z) + cb_ref[...]
        yc_ref[...] = y_conv.astype(BF16)
        s_in = bg * y_conv * (g_conv * _sig(g_conv))
        sin_ref[...] = s_in.astype(BF16)
        s_out = _dot(s_in.astype(BF16), wco_ref[...])
        sout_ref[...] = s_out.astype(BF16)
        merged = _sig(m_attn) * a_out + _sig(m_conv) * s_out
        mrg_ref[...] = merged.astype(BF16)
        y = _dot(merged.astype(BF16), wo_ref[...])
        gate1 = 1.0 + gate_ref[0]
        r = ALPHA * x_ref[...] + gate1 * y
        mu = jnp.mean(r, axis=1, keepdims=True)
        rc = r - mu
        rstd = lax.rsqrt(jnp.mean(rc * rc, axis=1, keepdims=True) + LN_EPS)
        xhat = rc * rstd
        diff = (xhat * lng_ref[...] + lnb_ref[...]) - t_ref[...]
        dout = diff * (1.0 / D)
        vec_ref[0:1, :] += jnp.sum(dout * xhat, axis=0, keepdims=True)
        vec_ref[1:2, :] += jnp.sum(dout, axis=0, keepdims=True)
        vec_ref[2:3, :] += jnp.sum(diff * diff, axis=0, keepdims=True)
        dxh = dout * lng_ref[...]
        dr = rstd * (dxh - jnp.mean(dxh, axis=1, keepdims=True) - xhat * jnp.mean(dxh * xhat, axis=1, keepdims=True))
        dxr_ref[...] = ALPHA * dr
        dy_ref[...] = (dr * gate1).astype(BF16)
        dgate_ref[0] += jnp.sum(dr * y, axis=0, keepdims=True)

    tok = lambda w: pl.BlockSpec((tm, w), lambda b, i: (b * spt + i, 0))
    const = lambda s: pl.BlockSpec(s, lambda b, i: (0,) * len(s))
    per_seq = pl.BlockSpec((1, 1, D), lambda b, i: (b, 0, 0))
    outs = pl.pallas_call(
        body, name="mix_forward", grid=(bsz, spt),
        out_shape=[SDS((t, AW), BF16), SDS((t, D), BF16), SDS((t, D), BF16), SDS((t, D), BF16), SDS((t, D), BF16), SDS((t, D), BF16),
                   SDS((t, D), BF16), SDS((t, AW), F32), SDS((t, AW), F32), SDS((t, D), F32), SDS((8, D), F32), SDS((bsz, 1, D), F32)],
        in_specs=[tok(NGATE)] + [tok(AW)] * 6 + [tok(D), tok(D), per_seq, const((AW, D)), const((D, D)), const((D, D)),
                                                 const((3, D)), const((1, D)), const((1, D)), const((1, D))],
        out_specs=[tok(AW), tok(D), tok(D), tok(D), tok(D), tok(D), tok(D), tok(AW), tok(AW), tok(D), const((8, D)), per_seq],
        scratch_shapes=[pltpu.VMEM((8, D), F32)],
        compiler_params=_cp(("arbitrary", "arbitrary"), VMEM_CAP),
    )(gates, *og, *lg, x2, tgt, gate, w_ao, w_co, w_o, conv_w, conv_b, ln_g, ln_b)
    return outs


def _mix_backward(gates, dy, a_out, s_out, y_conv, o, lj, w_ao, w_co, w_o, conv_w, bsz, seq, tm=256):
    t = dy.shape[0]
    spt = seq // tm

    def body(g_ref, dy_ref, aout_ref, sout_ref, yc_ref, o_ref, lj_ref, wao_ref, wco_ref, wo_ref, cw_ref,
             dg_ref, do_ref, dl_ref, daout_ref, dsout_ref, vec_ref, car_ref):
        b, i = pl.program_id(0), pl.program_id(1)

        @pl.when((b == 0) & (i == 0))
        def _():
            vec_ref[...] = jnp.zeros_like(vec_ref)

        @pl.when(i == 0)
        def _():
            car_ref[...] = jnp.zeros_like(car_ref)

        g_attn, u, bg, cg, g_conv, m_attn, m_conv = (g_ref[:, lo:hi].astype(F32) for lo, hi in GATE_COLS)
        dmerged = _dot_nt(dy_ref[...], wo_ref[...])
        sa, sc = _sig(m_attn), _sig(m_conv)
        da_out = (dmerged * sa).astype(BF16)
        ds_out = (dmerged * sc).astype(BF16)
        daout_ref[...] = da_out
        dsout_ref[...] = ds_out
        dg_ref[:, 4608:5632] = (dmerged * aout_ref[...].astype(F32) * (sa * (1.0 - sa))).astype(BF16)
        dg_ref[:, 5632:6656] = (dmerged * sout_ref[...].astype(F32) * (sc * (1.0 - sc))).astype(BF16)
        da_in = _dot_nt(da_out, wao_ref[...])
        ds_in = _dot_nt(ds_out, wco_ref[...])
        sga = _sig(g_attn)
        o = o_ref[...]
        do = da_in * (g_attn * sga)
        do_ref[...] = do
        dg_ref[:, 0:512] = (da_in * o * (sga * (1.0 + g_attn * (1.0 - sga)))).astype(BF16)
        prod = do * o
        lane = lax.broadcasted_iota(jnp.int32, (tm, HD), 1)
        for j in range(4):
            cs = slice(j * HD, (j + 1) * HD)
            delta = jnp.sum(prod[:, cs], axis=1, keepdims=True)
            dl_ref[:, cs] = jnp.where(lane < 64, lj_ref[:, cs], delta)
        sgc = _sig(g_conv)
        silu_c = g_conv * sgc
        yc = yc_ref[...].astype(F32)
        dg_ref[:, 1536:2560] = (ds_in * yc * silu_c).astype(BF16)
        dg_ref[:, 3584:4608] = (ds_in * bg * yc * (sgc * (1.0 + g_conv * (1.0 - sgc)))).astype(BF16)
        dyc = ds_in * bg * silu_c
        rows = lax.broadcasted_iota(jnp.int32, (tm, D), 0)
        c0, c1 = car_ref[0:1, :], car_ref[1:2, :]
        n1 = jnp.where(rows == tm - 1, c0, pltpu.roll(dyc, tm - 1, 0))
        n2 = jnp.where(rows == tm - 2, c0, jnp.where(rows == tm - 1, c1, pltpu.roll(dyc, tm - 2, 0)))
        car_ref[...] = dyc[0:8, :]
        dz = cw_ref[2:3, :] * dyc + cw_ref[1:2, :] * n1 + cw_ref[0:1, :] * n2
        z = cg * u
        dg_ref[:, 512:1536] = (dz * cg).astype(BF16)
        dg_ref[:, 2560:3584] = (dz * u).astype(BF16)
        vec_ref[0:1, :] += jnp.sum(n2 * z, axis=0, keepdims=True)
        vec_ref[1:2, :] += jnp.sum(n1 * z, axis=0, keepdims=True)
        vec_ref[2:3, :] += jnp.sum(dyc * z, axis=0, keepdims=True)
        vec_ref[3:4, :] += jnp.sum(dyc, axis=0, keepdims=True)

    tok = lambda w: pl.BlockSpec((tm, w), lambda b, i: (b * spt + (spt - 1 - i), 0))
    const = lambda s: pl.BlockSpec(s, lambda b, i: (0,) * len(s))
    return pl.pallas_call(
        body, name="mix_backward", grid=(bsz, spt),
        out_shape=[SDS((t, NGATE), BF16), SDS((t, AW), F32), SDS((t, AW), F32), SDS((t, D), BF16), SDS((t, D), BF16), SDS((8, D), F32)],
        in_specs=[tok(NGATE), tok(D), tok(D), tok(D), tok(D), tok(AW), tok(AW), const((AW, D)), const((D, D)), const((D, D)), const((3, D))],
        out_specs=[tok(NGATE), tok(AW), tok(AW), tok(D), tok(D), const((8, D))],
        scratch_shapes=[pltpu.VMEM((8, D), F32)],
        compiler_params=_cp(("arbitrary", "arbitrary"), VMEM_CAP),
    )(gates, dy, a_out, s_out, y_conv, o, lj, w_ao, w_co, w_o, conv_w)


def _out_weight_grads(a_in, da_out, s_in, ds_out, merged, dy, tk=512):
    t = dy.shape[0]

    def body(ain_ref, da_ref, sin_ref, ds_ref, m_ref, dy_ref, gao_ref, gco_ref, go_ref):
        @pl.when(pl.program_id(0) == 0)
        def _():
            gao_ref[...] = jnp.zeros_like(gao_ref)
            gco_ref[...] = jnp.zeros_like(gco_ref)
            go_ref[...] = jnp.zeros_like(go_ref)

        gao_ref[...] += _dot_tn(ain_ref[...], da_ref[...])
        gco_ref[...] += _dot_tn(sin_ref[...], ds_ref[...])
        go_ref[...] += _dot_tn(m_ref[...], dy_ref[...])

    tok = lambda w: pl.BlockSpec((tk, w), lambda i: (i, 0))
    const = lambda s: pl.BlockSpec(s, lambda i: (0, 0))
    return pl.pallas_call(
        body, name="out_weight_grads", grid=(t // tk,), out_shape=[SDS((AW, D), F32), SDS((D, D), F32), SDS((D, D), F32)],
        in_specs=[tok(AW), tok(D), tok(D), tok(D), tok(D), tok(D)], out_specs=[const((AW, D)), const((D, D)), const((D, D))],
        compiler_params=_cp(("arbitrary",), VMEM_CAP),
    )(a_in, da_out, s_in, ds_out, merged, dy)


def _input_grad(dq, dk, dv, dgates, w, x2, dxr, sc1p, seq, sums, tm=1024):
    t = x2.shape[0]
    spt = seq // tm
    bsz = t // seq
    n = len(sums)

    def body(dq_ref, dk_ref, dv_ref, dg_ref, w_ref, x_ref, dxr_ref, sc_ref, *rest):
        src, (dx_ref, dsh_ref, dsc_ref), land = rest[:n], rest[n:n + 3], rest[n + 3:2 * n + 3]
        acc_ref, send_sems, recv_sems = rest[2 * n + 3:]
        i, j = pl.program_id(0), pl.program_id(1)
        px, py, pc = _place()
        chips = [(1 - px, py), (px, 1 - py), (1 - px, 1 - py)]
        copies = [pltpu.make_async_remote_copy(src_ref=src[a].at[2 * cx + cy], dst_ref=land[a].at[r], send_sem=send_sems.at[3 * a + r],
                                               recv_sem=recv_sems.at[3 * a + r], device_id=(cx, cy, pc), device_id_type=MESH)
                  for a in range(n) for r, (cx, cy) in enumerate(chips)]

        @pl.when((i == 0) & (j == 0))
        def _():
            for cp in copies:
                cp.start()

        @pl.when(j == 0)
        def _():
            acc_ref[...] = jnp.zeros_like(acc_ref)

        for k, ref in enumerate((dq_ref, dk_ref, dv_ref)):
            @pl.when((j >= k * NQT) & (j < (k + 1) * NQT))
            def _(ref=ref):
                acc_ref[...] += _dot_nt(ref[...].astype(BF16), w_ref[...])

        @pl.when(j >= 3 * NQT)
        def _():
            acc_ref[...] += _dot_nt(dg_ref[...], w_ref[...])

        @pl.when(j == NPT - 1)
        def _():
            dh = acc_ref[...]
            dx_ref[...] = dh * sc_ref[0] + dxr_ref[...]

            @pl.when(i % spt == 0)
            def _():
                dsh_ref[...] = jnp.zeros_like(dsh_ref)
                dsc_ref[...] = jnp.zeros_like(dsc_ref)

            dsh_ref[0] += jnp.sum(dh, axis=0, keepdims=True)
            dsc_ref[0] += jnp.sum(dh * x_ref[...], axis=0, keepdims=True)

        @pl.when((i == t // tm - 1) & (j == NPT - 1))
        def _():
            for cp in copies:
                cp.wait()

    def part(k):
        return pl.BlockSpec((tm, TN), lambda i, j: (i, jnp.clip(j - k * NQT, 0, NQT - 1)))

    row = pl.BlockSpec((tm, D), lambda i, j: (i, 0))
    per_seq = pl.BlockSpec((1, 1, D), lambda i, j: (i // spt, 0, 0))
    outs = pl.pallas_call(
        body, name="input_grad", grid=(t // tm, NPT),
        out_shape=[SDS((t, D), F32), SDS((bsz, 1, D), F32), SDS((bsz, 1, D), F32)] + [SDS((3,) + s.shape[1:], BF16) for s in sums],
        in_specs=[part(0), part(1), part(2), pl.BlockSpec((tm, TN), lambda i, j: (i, jnp.maximum(j - 3 * NQT, 0))),
                  pl.BlockSpec((D, TN), lambda i, j: (0, j)), row, row, per_seq] + [ANY] * n,
        out_specs=[row, per_seq, per_seq] + [ANY] * n,
        scratch_shapes=[pltpu.VMEM((tm, D), F32), pltpu.SemaphoreType.DMA((3 * NCHIP,)), pltpu.SemaphoreType.DMA((3 * NCHIP,))],
        compiler_params=_cp(("arbitrary", "arbitrary"), VMEM_CAP, side=True),
    )(dq, dk, dv, dgates, w, x2, dxr, sc1p, *sums)
    return outs[0], outs[1], outs[2], outs[3:]


def _in_weight_grad(ht, src, col0, prev, name, tk=1024):
    t = ht.shape[1]
    ncols = src.shape[1] // TN

    def body(ht_ref, s_ref, *rest):
        o_ref = rest[-1]

        @pl.when(pl.program_id(1) == 0)
        def _():
            o_ref[...] = jnp.zeros_like(o_ref)

        o_ref[...] += _dot(ht_ref[...], s_ref[...].astype(BF16))

    ins = [ht, src]
    in_specs = [pl.BlockSpec((D, tk), lambda j, i: (0, i)), pl.BlockSpec((tk, TN), lambda j, i: (i, j))]
    aliases = {}
    if prev is not None:
        ins.append(prev)
        in_specs.append(ANY)
        aliases = {2: 0}
    return pl.pallas_call(
        body, name=name, grid=(ncols, t // tk), out_shape=SDS((D, NCOL), F32), in_specs=in_specs,
        out_specs=pl.BlockSpec((D, TN), lambda j, i: (0, col0 + j)), input_output_aliases=aliases,
        compiler_params=_cp(("arbitrary", "arbitrary"), VMEM_CAP // 2),
    )(*ins)


def _sum_partials(gathered):
    def body(g_ref, o_ref):
        acc = g_ref[0]
        for k in range(1, 8):
            acc = acc + g_ref[k]
        o_ref[...] = acc

    return pl.pallas_call(body, name="sum_partials", out_shape=SDS(gathered.shape[1:], F32), in_specs=[VMEM_SPEC], out_specs=VMEM_SPEC)(gathered)


def _adamw(w, g, m, v, name, tr=256):
    r, cdim = w.shape
    tr = tr if cdim <= D else tr // 2
    tr = tr if (r % tr == 0 and r > tr) else r

    def body(w_ref, g_ref, m_ref, v_ref, d_ref, nm_ref, nv_ref):
        gv = g_ref[...]
        nm = B1 * m_ref[...] + (1.0 - B1) * gv
        nv = B2 * v_ref[...] + (1.0 - B2) * (gv * gv)
        m_hat = nm / (1.0 - B1 ** STEP)
        v_hat = nv / (1.0 - B2 ** STEP)
        d_ref[...] = -LR * (m_hat / (jnp.sqrt(v_hat) + EPS) + WD * w_ref[...])
        nm_ref[...] = nm
        nv_ref[...] = nv

    spec = pl.BlockSpec((tr, cdim), lambda i: (i, 0))
    return pl.pallas_call(
        body, name=name, grid=(r // tr,), out_shape=[SDS((r, cdim), F32)] * 3, in_specs=[spec] * 4, out_specs=[spec] * 3,
        compiler_params=_cp(("parallel",), VMEM_CAP // 2),
    )(w, g, m, v)


def _t5_bucket(dist):
    n = jnp.maximum(dist, 1).astype(F32)
    large = MAX_EXACT + (jnp.log(n / MAX_EXACT) / math.log(MAX_DISTANCE / MAX_EXACT) * (N_BUCKETS - MAX_EXACT)).astype(jnp.int32)
    large = jnp.minimum(large, N_BUCKETS - 1)
    return jnp.where(dist < MAX_EXACT, dist, large)


def _band_buckets():
    a = jnp.arange(BLK)[:, None]
    b = jnp.arange(2 * BLK)[None, :]
    steps = jnp.maximum(a + BLK - b, 0)
    return jnp.stack([_t5_bucket(steps * d) for d in DILATIONS]).astype(jnp.int32)


def _pad_rows(a, rows=8):
    return jnp.pad(a, ((0, rows - a.shape[0]), (0, 0)))


def kernel(x, c, w_ada, b_ada, w_in, conv_w, conv_b, rel_bias, w_attn_out, w_conv_out, w_o, ln_g, ln_b, loss_target, m_w_ada, m_b_ada, m_w_in, m_conv_w, m_conv_b, m_rel_bias, m_w_attn_out, m_w_conv_out, m_w_o, m_ln_g, m_ln_b, v_w_ada, v_b_ada, v_w_in, v_conv_w, v_conv_b, v_rel_bias, v_w_attn_out, v_w_conv_out, v_w_o, v_ln_g, v_ln_b):
    bsz, seq, _ = x.shape
    t = bsz * seq
    mx, my, mc = _place()
    chip = 2 * mx + my
    dev = 4 * mx + 2 * my + mc
    x2 = x.reshape(t, D)
    tgt = loss_target.reshape(t, D)

    mine = [_to_bf16_window(a, w[0], f"to_bf16_{a}") for a, w in enumerate((w_in, w_attn_out, w_conv_out, w_o))]

    n_ada = w_ada.shape[2]
    n_cw = conv_w.shape[2]
    c_and_cw = jnp.concatenate([_pad_rows(c), jnp.pad(conv_w[0], ((0, 5), (0, D - n_cw)))], axis=0)
    firsts = _all_gather8(c_and_cw, "gather_c_conv_w")
    c_all = firsts[:, 0:bsz, :].reshape(8 * bsz, D)
    conv_w_f = firsts[0::2, 8:11, 0:n_cw].transpose(1, 0, 2).reshape(3, D)
    b_cols = lax.dynamic_slice(b_ada, (0, chip * n_ada), (1, n_ada))
    mod_part = _ada_forward(c_all, w_ada[0], b_cols)
    mod_parts = _all_gather8(mod_part, "gather_mod")
    mod_all = mod_parts[0::2].transpose(1, 0, 2).reshape(8 * bsz, 3 * D)
    mod = lax.dynamic_slice(mod_all, (dev * bsz, 0), (bsz, 3 * D))
    shift = mod[:, 0:D].reshape(bsz, 1, D)
    sc1p = 1.0 + mod[:, D:2 * D].reshape(bsz, 1, D)
    gate = mod[:, 2 * D:].reshape(bsz, 1, D)

    h, ht = _modulate(x2, sc1p, shift, seq)
    tab = lax.dynamic_index_in_dim(jnp.asarray(_tile_tables()), chip, 0, keepdims=False)
    qkv, gates, (w_in_f, w_ao_f, w_co_f, w_o_f) = _project_gather(h, mine, tab)
    buckets = _band_buckets()
    bias = _bias_tables(rel_bias, buckets)
    og, lg = [], []
    for g in range(3):
        o_g, l_g = _attn_forward(g, qkv, bias[g], bsz, seq)
        og.append(o_g)
        lg.append(l_g)
    (a_in, s_in, merged, dy, a_out, s_out, y_conv, o, lj, dxr, vec_f, dgate) = _mix_forward(
        gates, og, lg, x2, tgt, gate, w_ao_f, w_co_f, w_o_f, conv_w_f, conv_b, ln_g, ln_b, bsz, seq)

    dgates, do, dl, da_out, ds_out, vec_b = _mix_backward(gates, dy, a_out, s_out, y_conv, o, lj, w_ao_f, w_co_f, w_o_f, conv_w_f, bsz, seq)
    g_ao, g_co, g_o = _out_weight_grads(a_in, da_out, s_in, ds_out, merged, dy)
    dqkv, dbs = None, []
    for g in range(3):
        dqkv, db = _attn_backward(g, qkv, do, dl, bias[g], dqkv, bsz, seq)
        dbs.append(db)
    dq, dk, dv = dqkv
    drb = _bias_grad(jnp.stack(dbs), buckets)
    drb = drb[:, :, 0:4].transpose(1, 0, 2).reshape(N_BUCKETS, 12)
    g_in = None
    for n, src in enumerate((dq, dk, dv, dgates)):
        g_in = _in_weight_grad(ht, src, n * NQT, g_in, f"in_weight_grad_{n}")

    grads = [g_in, g_ao, g_co, g_o]
    got = _swap_halves(grads)
    sums = [_chip_sum(a, grads[a], got[a], f"chip_sum_{a}") for a in range(4)]
    grad_x, dshift, dscale, landed = _input_grad(dq, dk, dv, dgates, w_in_f, x2, dxr, sc1p, seq, [s[1] for s in sums])
    halves = [_reduce_mine(a, sums[a][0], landed[a], f"reduce_mine_{a}") for a in range(4)]
    gw_in, gw_ao, gw_co, gw_o = _join_halves(halves)

    dmod = jnp.concatenate([dshift, dscale, dgate], axis=2).reshape(bsz * 3, D)
    drb_row = jnp.pad(drb.reshape(1, N_BUCKETS * 12), ((0, 0), (0, D - N_BUCKETS * 12)))
    packed = jnp.concatenate([vec_f, vec_b, _pad_rows(dmod), _pad_rows(drb_row)], axis=0)
    gathered = _all_gather8(packed, "gather_small")
    small = _sum_partials(gathered)
    g_ln_g, g_ln_b, loss_lanes = small[0:1], small[1:2], small[2:3]
    g_conv_w_full, g_conv_b = small[8:11], small[11:12]
    g_rel_bias = small[24, 0:N_BUCKETS * 12].reshape(N_BUCKETS, 12)
    loss = 0.5 / D * jnp.sum(loss_lanes)
    dmod_all = gathered[:, 16:16 + 3 * bsz, :].reshape(8 * bsz, 3 * D)
    dmod_cols = lax.dynamic_slice(dmod_all, (0, chip * n_ada), (8 * bsz, n_ada))
    gw_ada, gb_ada = _ada_backward(c_all, dmod_cols, dmod_all)
    g_conv_w = lax.dynamic_slice(g_conv_w_full, (0, chip * n_cw), (3, n_cw))

    names = ["w_ada", "b_ada", "w_in", "conv_w", "conv_b", "rel_bias", "w_attn_out", "w_conv_out", "w_o", "ln_g", "ln_b"]
    two_d = lambda a: a.reshape(a.shape[-2:]) if a.ndim == 3 else a
    weights = dict(zip(names, map(two_d, (w_ada, b_ada, w_in, conv_w, conv_b, rel_bias, w_attn_out, w_conv_out, w_o, ln_g, ln_b))))
    ms = dict(zip(names, map(two_d, (m_w_ada, m_b_ada, m_w_in, m_conv_w, m_conv_b, m_rel_bias, m_w_attn_out, m_w_conv_out, m_w_o, m_ln_g, m_ln_b))))
    vs = dict(zip(names, map(two_d, (v_w_ada, v_b_ada, v_w_in, v_conv_w, v_conv_b, v_rel_bias, v_w_attn_out, v_w_conv_out, v_w_o, v_ln_g, v_ln_b))))
    grads = dict(zip(names, (gw_ada, gb_ada, gw_in, g_conv_w, g_conv_b, g_rel_bias, gw_ao, gw_co, gw_o, g_ln_g, g_ln_b)))
    shapes = dict(zip(names, (w_ada, b_ada, w_in, conv_w, conv_b, rel_bias, w_attn_out, w_conv_out, w_o, ln_g, ln_b)))
    deltas, new_m, new_v = {}, {}, {}
    for n in names:
        deltas[n], new_m[n], new_v[n] = _adamw(weights[n], grads[n], ms[n], vs[n], f"adamw_{n}")
    shaped = lambda d: [d[n].reshape(shapes[n].shape) for n in names]
    return (loss, grad_x.reshape(bsz, seq, D), *shaped(grads), *shaped(deltas), *shaped(new_m), *shaped(new_v))
```

```python
import math

import numpy as np
import jax
import jax.numpy as jnp
from jax import lax
from jax.experimental import pallas as pl
from jax.experimental.pallas import tpu as pltpu

F32 = jnp.float32
BF16 = jnp.bfloat16
SDS = jax.ShapeDtypeStruct
MESH = pl.DeviceIdType.MESH
ANY = pl.BlockSpec(memory_space=pl.ANY)
VMEM_SPEC = pl.BlockSpec(memory_space=pltpu.VMEM)

D = 1024
HD = 128
BLK = 128
QW = 1536
AW = 512
NGATE = 6656
GATE_COLS = ((0, 512), (512, 1536), (1536, 2560), (2560, 3584), (3584, 4608), (4608, 5632), (5632, 6656))
NCOL = 3 * QW + NGATE
TN = 512
NQT = QW // TN
NPT = NCOL // TN
DILATIONS = (1, 4, 16)
N_BUCKETS, MAX_EXACT, MAX_DISTANCE = 32, 16, 2048
ALPHA = 2.0 ** 0.25
LN_EPS = 1e-5
NEG = -1e30
SCALE = HD ** -0.5
LR, B1, B2, EPS, WD, STEP = 0.001, 0.9, 0.999, 1e-08, 0.01, 10
NCHIP = 4
VMEM_CAP = 60 * 2 ** 20


def _cp(sem=None, vmem=None, side=False):
    return pltpu.CompilerParams(dimension_semantics=sem, vmem_limit_bytes=vmem, has_side_effects=side)


def _dot(a, b):
    return jnp.dot(a, b, preferred_element_type=F32)


def _dot_nt(a, b):
    return lax.dot_general(a, b, (((1,), (1,)), ((), ())), preferred_element_type=F32)


def _dot_tn(a, b):
    return lax.dot_general(a, b, (((0,), (0,)), ((), ())), preferred_element_type=F32)


def _sig(x):
    return 1.0 / (1.0 + jnp.exp(-x))


def _place():
    x, y, c = lax.axis_index("x"), lax.axis_index("y"), lax.axis_index("c")
    return x, y, c


def _all_gather8(v, name):
    r, cdim = v.shape

    def body(v_ref, out_ref, send_sems, recv_sems, local_sem):
        x, y, c = _place()
        me = 4 * x + 2 * y + c
        peers = [(x, y, 1 - c), (1 - x, y, c), (x, 1 - y, c), (1 - x, 1 - y, c),
                 (1 - x, y, 1 - c), (x, 1 - y, 1 - c), (1 - x, 1 - y, 1 - c)]
        mine = pltpu.make_async_copy(v_ref, out_ref.at[me], local_sem)
        mine.start()

        def copy(k, block, to):
            return pltpu.make_async_remote_copy(src_ref=v_ref, dst_ref=out_ref.at[block], send_sem=send_sems.at[k],
                                                recv_sem=recv_sems.at[k], device_id=to, device_id_type=MESH)

        sends = [copy(k, me, p) for k, p in enumerate(peers)]
        for cp in sends:
            cp.start()
        for k, (px, py, pc) in enumerate(peers):
            copy(k, 4 * px + 2 * py + pc, (px, py, pc)).wait_recv()
        for cp in sends:
            cp.wait_send()
        mine.wait()

    return pl.pallas_call(
        body, name=name, out_shape=SDS((8, r, cdim), v.dtype), in_specs=[VMEM_SPEC], out_specs=VMEM_SPEC,
        scratch_shapes=[pltpu.SemaphoreType.DMA((7,)), pltpu.SemaphoreType.DMA((7,)), pltpu.SemaphoreType.DMA(())],
        compiler_params=_cp(side=True),
    )(v)


W_CUTS = (("col", D, NCOL // NCHIP), ("col", AW, D // NCHIP), ("row", D // NCHIP, D), ("row", D // NCHIP, D))
W_FULL = ((D, NCOL), (AW, D), (D, D), (D, D))


def _shard_window(ref, cut, k, half):
    kind, nr, nc = cut
    hr = nr // 2
    if kind == "col":
        rows = pl.ds(0, nr) if half is None else pl.ds(pl.multiple_of(half * hr, 16), hr)
        return ref.at[rows, pl.ds(pl.multiple_of(k * nc, 128), nc)]
    if half is None:
        return ref.at[pl.ds(pl.multiple_of(k * nr, 16), nr), :]
    return ref.at[pl.ds(pl.multiple_of(k * nr + half * hr, 16), hr), :]


def _half_rows(ref, cut, half):
    hr = cut[1] // 2
    return ref.at[pl.ds(pl.multiple_of(half * hr, 16), hr), :]


def _to_bf16_window(a, w, name):
    kind, nr, nc = W_CUTS[a]
    x, y, _ = _place()
    chip = jnp.reshape(2 * x + y, (1,)).astype(jnp.int32)
    tr = min(nr, 256)

    def body(c_ref, w_ref, o_ref):
        o_ref[...] = w_ref[...].astype(BF16)

    out_map = (lambda i, cr: (i, cr[0])) if kind == "col" else (lambda i, cr: (cr[0] * (nr // tr) + i, 0))
    return pl.pallas_call(
        body, name=name, out_shape=SDS(W_FULL[a], BF16),
        grid_spec=pltpu.PrefetchScalarGridSpec(num_scalar_prefetch=1, grid=(nr // tr,),
                                               in_specs=[pl.BlockSpec((tr, nc), lambda i, cr: (i, 0))], out_specs=pl.BlockSpec((tr, nc), out_map)),
        compiler_params=_cp(("arbitrary",)),
    )(chip, w)


def _swap_halves(grads):
    n = len(grads)
    shapes = []
    for a in range(n):
        kind, nr, nc = W_CUTS[a]
        shapes.append((W_FULL[a][0] // 2, W_FULL[a][1]) if kind == "col" else (NCHIP, nr // 2, nc))

    def pieces(a, ref, land, half):
        kind, nr, nc = W_CUTS[a]
        if kind == "col":
            hr = nr // 2
            return [(ref.at[pl.ds(pl.multiple_of(half * hr, 16), hr), :], land)]
        return [(_shard_window(ref, W_CUTS[a], k, half), land.at[k]) for k in range(NCHIP)]

    def body(*refs):
        src, land = refs[:n], refs[n:2 * n]
        send_sems, recv_sems = refs[2 * n:]
        x, y, c = _place()
        sibling = (x, y, 1 - c)
        sends = []
        k = 0
        for a in range(n):
            for s, d in pieces(a, src[a], land[a], 1 - c):
                cp = pltpu.make_async_remote_copy(src_ref=s, dst_ref=d, send_sem=send_sems.at[k], recv_sem=recv_sems.at[k],
                                                  device_id=sibling, device_id_type=MESH)
                cp.start()
                sends.append(cp)
                k += 1
        for cp in sends:
            cp.wait()

    n_sems = sum(1 if W_CUTS[a][0] == "col" else NCHIP for a in range(n))
    return pl.pallas_call(
        body, name="swap_grad_halves", out_shape=[SDS(s, F32) for s in shapes], in_specs=[ANY] * n, out_specs=[ANY] * n,
        scratch_shapes=[pltpu.SemaphoreType.DMA((n_sems,)), pltpu.SemaphoreType.DMA((n_sems,))],
        compiler_params=_cp(side=True),
    )(*grads)


def _chip_sum(a, grad, got, name):
    kind, nr, nc = W_CUTS[a]
    hr = nr // 2
    c = lax.axis_index("c")
    cidx = jnp.reshape(c, (1,)).astype(jnp.int32)

    def body(c_ref, g_ref, r_ref, f_ref, b_ref):
        s = g_ref[...] + r_ref[...]
        f_ref[...] = s.reshape(f_ref.shape)
        b_ref[...] = s.astype(BF16).reshape(b_ref.shape)

    if kind == "col":
        in_specs = [pl.BlockSpec((hr, nc), lambda k, cr: (cr[0], k)), pl.BlockSpec((hr, nc), lambda k, cr: (0, k))]
    else:
        grad = grad.reshape(NCHIP, 2, hr, nc)
        in_specs = [pl.BlockSpec((1, 1, hr, nc), lambda k, cr: (k, cr[0], 0, 0)), pl.BlockSpec((1, hr, nc), lambda k, cr: (k, 0, 0))]
    out_specs = [pl.BlockSpec((1, hr, nc), lambda k, cr: (k, 0, 0))] * 2
    return pl.pallas_call(
        body, name=name, out_shape=[SDS((NCHIP, hr, nc), F32), SDS((NCHIP, hr, nc), BF16)],
        grid_spec=pltpu.PrefetchScalarGridSpec(num_scalar_prefetch=1, grid=(NCHIP,), in_specs=in_specs, out_specs=out_specs),
        compiler_params=_cp(("arbitrary",), VMEM_CAP),
    )(cidx, grad, got)


def _reduce_mine(a, mine_f32, got, name):
    kind, nr, nc = W_CUTS[a]
    hr = nr // 2
    x, y, c = _place()
    where = jnp.stack([2 * x + y, c]).astype(jnp.int32)
    tr = min(hr, 256)

    def body(w_ref, m_ref, g_ref, o_ref):
        o_ref[...] = ((m_ref[0] + g_ref[0].astype(F32)) + g_ref[1].astype(F32)) + g_ref[2].astype(F32)

    return pl.pallas_call(
        body, name=name, out_shape=SDS((nr, nc), F32),
        grid_spec=pltpu.PrefetchScalarGridSpec(
            num_scalar_prefetch=1, grid=(hr // tr,),
            in_specs=[pl.BlockSpec((1, tr, nc), lambda i, wr: (wr[0], i, 0)), pl.BlockSpec((3, tr, nc), lambda i, wr: (0, i, 0))],
            out_specs=pl.BlockSpec((tr, nc), lambda i, wr: (wr[1] * (hr // tr) + i, 0))),
        compiler_params=_cp(("arbitrary",), VMEM_CAP),
    )(where, mine_f32, got)


def _join_halves(fulls):
    n = len(fulls)

    def body(*refs):
        full = refs[n:2 * n]
        send_sems, recv_sems = refs[2 * n:]
        x, y, c = _place()
        sibling = (x, y, 1 - c)

        def swap(a, half):
            rows = _half_rows(full[a], W_CUTS[a], half)
            return pltpu.make_async_remote_copy(src_ref=rows, dst_ref=rows, send_sem=send_sems.at[a], recv_sem=recv_sems.at[a],
                                                device_id=sibling, device_id_type=MESH)

        sends = [swap(a, c) for a in range(n)]
        for cp in sends:
            cp.start()
        for a, cp in enumerate(sends):
            cp.wait_send()
            swap(a, 1 - c).wait_recv()

    return pl.pallas_call(
        body, name="join_grad_halves", out_shape=[SDS((W_CUTS[a][1], W_CUTS[a][2]), F32) for a in range(n)],
        in_specs=[ANY] * n, out_specs=[ANY] * n,
        scratch_shapes=[pltpu.SemaphoreType.DMA((n,)), pltpu.SemaphoreType.DMA((n,))],
        input_output_aliases={a: a for a in range(n)}, compiler_params=_cp(side=True),
    )(*fulls)


def _ada_forward(c_all, w_ada, b_cols):
    nb, nc = c_all.shape[0], w_ada.shape[1]

    def body(c_ref, w_ref, b_ref, o_ref):
        cv = c_ref[...]
        sc = (cv * _sig(cv)).astype(BF16)
        o_ref[...] = _dot(sc, w_ref[...].astype(BF16)) + b_ref[...]

    return pl.pallas_call(body, name="ada_forward", out_shape=SDS((nb, nc), F32), compiler_params=_cp(vmem=VMEM_CAP // 2))(c_all, w_ada, b_cols)


def _ada_backward(c_all, dmod_cols, dmod_all):
    nb, nc = dmod_cols.shape

    def body(c_ref, d_ref, a_ref, gw_ref, gb_ref):
        cv = c_ref[...]
        sc = (cv * _sig(cv)).astype(BF16)
        gw_ref[...] = _dot_tn(sc, d_ref[...].astype(BF16))
        gb_ref[...] = jnp.sum(a_ref[...], axis=0, keepdims=True)

    return pl.pallas_call(body, name="ada_backward", out_shape=[SDS((D, nc), F32), SDS((1, dmod_all.shape[1]), F32)],
                          compiler_params=_cp(vmem=VMEM_CAP // 2))(c_all, dmod_cols, dmod_all)


def _modulate(x2, sc1p, shift, seq, tm=256):
    t = x2.shape[0]
    spt = seq // tm

    def body(x_ref, sc_ref, sh_ref, h_ref, ht_ref):
        h = x_ref[...] * sc_ref[0] + sh_ref[0]
        h_ref[...] = h.astype(BF16)
        ht_ref[...] = h.T.astype(BF16)

    per_seq = pl.BlockSpec((1, 1, D), lambda i: (i // spt, 0, 0))
    return pl.pallas_call(
        body, name="modulate", out_shape=[SDS((t, D), BF16), SDS((D, t), BF16)], grid=(t // tm,),
        in_specs=[pl.BlockSpec((tm, D), lambda i: (i, 0)), per_seq, per_seq],
        out_specs=[pl.BlockSpec((tm, D), lambda i: (i, 0)), pl.BlockSpec((D, tm), lambda i: (0, i))],
        compiler_params=_cp(("parallel",)),
    )(x2, sc1p, shift)


TW = 256
TPS = NCOL // NCHIP // TW
NT = NCOL // TW
NQKV_T = 3 * QW // TW


def _tile_tables():
    tabs = np.zeros((NCHIP, 3, NT), np.int32)
    for me in range(NCHIP):
        tiles = [TPS * (me ^ (s // TPS)) + s % TPS for s in range(NT)]
        tabs[me, 0] = tiles
        for row, (lo, hi) in enumerate(((0, NQKV_T), (NQKV_T, NT))):
            mine = [w - lo if lo <= w < hi else None for w in tiles]
            held = next(m for m in mine if m is not None)
            for s, m in enumerate(mine):
                held = held if m is None else m
                tabs[me, 1 + row, s] = held
    return tabs


def _project_gather(h, fulls, tab):
    t = h.shape[0]
    n = len(fulls)

    def body(tab_ref, h_ref, *rest):
        qkv_ref, g_ref = rest[n], rest[n + 1]
        full = rest[n + 2:2 * n + 2]
        w_buf, tile_sems, send_sems, recv_sems = rest[2 * n + 2:]
        s = pl.program_id(0)
        x, y, c = _place()
        me = 2 * x + y
        peers = [(x, 1 - y), (1 - x, y), (1 - x, 1 - y)]
        sibling = (x, y, 1 - c)

        def hop(a, r, stage, chip, half, to):
            window = _shard_window(full[a], W_CUTS[a], chip, half)
            k = 6 * a + 2 * r + stage
            return pltpu.make_async_remote_copy(src_ref=window, dst_ref=window, send_sem=send_sems.at[k], recv_sem=recv_sems.at[k],
                                                device_id=to, device_id_type=MESH)

        def send(a, r):
            return hop(a, r, 0, me, c, (*peers[r], c))

        def arrive(a, r):
            px, py = peers[r]
            chip = 2 * px + py
            hop(a, r, 0, chip, c, (px, py, c)).wait_recv()
            hop(a, r, 1, chip, c, sibling).start()
            hop(a, r, 1, chip, 1 - c, sibling).wait_recv()

        def tile(step, slot):
            col = pl.multiple_of(tab_ref[0, step] * TW, TW)
            return pltpu.make_async_copy(full[0].at[:, pl.ds(col, TW)], w_buf.at[slot], tile_sems.at[slot])

        @pl.when(s == 0)
        def _():
            send(0, 0).start()
            send(0, 1).start()
            tile(0, 0).start()

        slot = s % 2
        tile(s, slot).wait()

        @pl.when((s + 1 < NT) & ((s + 1) % TPS != 0))
        def _():
            tile(s + 1, 1 - slot).start()

        is_qkv = tab_ref[0, s] < NQKV_T
        for k in range(2):
            @pl.when(slot == k)
            def _(k=k):
                acc = _dot(h_ref[...], w_buf[k])

                @pl.when(is_qkv)
                def _():
                    qkv_ref[...] = acc

                @pl.when(jnp.logical_not(is_qkv))
                def _():
                    g_ref[...] = acc.astype(BF16)

        for r in range(3):
            @pl.when(s + 1 == TPS * (r + 1))
            def _(r=r):
                arrive(0, r)
                tile(s + 1, 1 - slot).start()
                if r == 0:
                    send(0, 2).start()
                    for a in range(1, n):
                        for q in range(3):
                            send(a, q).start()

        @pl.when(s == NT - 1)
        def _():
            for a in range(1, n):
                for r in range(3):
                    arrive(a, r)
            for a in range(n):
                for r in range(3):
                    send(a, r).wait_send()
                    px, py = peers[r]
                    hop(a, r, 1, 2 * px + py, c, sibling).wait_send()

    outs = pl.pallas_call(
        body, name="project_gather", out_shape=[SDS((t, 3 * QW), F32), SDS((t, NGATE), BF16)] + [SDS(s, BF16) for s in W_FULL],
        grid_spec=pltpu.PrefetchScalarGridSpec(
            num_scalar_prefetch=1, grid=(NT,),
            in_specs=[pl.BlockSpec((t, D), lambda s, tab: (0, 0))] + [ANY] * n,
            out_specs=[pl.BlockSpec((t, TW), lambda s, tab: (0, tab[1, s])), pl.BlockSpec((t, TW), lambda s, tab: (0, tab[2, s]))] + [ANY] * n,
            scratch_shapes=[pltpu.VMEM((2, D, TW), BF16), pltpu.SemaphoreType.DMA((2,)),
                            pltpu.SemaphoreType.DMA((6 * n,)), pltpu.SemaphoreType.DMA((6 * n,))]),
        input_output_aliases={2 + a: 2 + a for a in range(n)},
        compiler_params=_cp(("arbitrary",), VMEM_CAP, side=True),
    )(tab, h, *fulls)
    return outs[0], outs[1], outs[2:]


def _bias_tables(rel_bias, buckets):
    def body(tab_ref, bk_ref, o_ref):
        a = lax.broadcasted_iota(jnp.int32, (BLK, 2 * BLK), 0)
        b = lax.broadcasted_iota(jnp.int32, (BLK, 2 * BLK), 1)
        steps = a + BLK - b
        valid = (steps >= 0) & (steps <= BLK)
        for g in range(3):
            bk = bk_ref[g]
            for j in range(4):
                def pick(kk, acc, bk=bk, col=4 * g + j):
                    return jnp.where(bk == kk, tab_ref[kk, col], acc)

                acc = lax.fori_loop(0, N_BUCKETS, pick, jnp.zeros((BLK, 2 * BLK), F32))
                o_ref[g, j] = jnp.where(valid, acc, NEG)

    return pl.pallas_call(
        body, name="bias_tables", out_shape=SDS((3, 4, BLK, 2 * BLK), F32),
        in_specs=[pl.BlockSpec(memory_space=pltpu.SMEM), VMEM_SPEC], out_specs=VMEM_SPEC,
    )(rel_bias, buckets)


def _bias_grad(ds_sum, buckets):
    def body(ds_ref, bk_ref, o_ref):
        lane = lax.broadcasted_iota(jnp.int32, (1, 128), 1)
        for g in range(3):
            def bucket(kk, carry, g=g):
                row = jnp.zeros((1, 128), F32)
                for j in range(4):
                    v = jnp.where(bk_ref[g] == kk, ds_ref[g, j], 0.0)
                    s = jnp.sum(jnp.sum(v, axis=1, keepdims=True), axis=0, keepdims=True)
                    row = jnp.where(lane == j, s, row)
                o_ref[g, pl.ds(kk, 1), :] = row
                return carry

            lax.fori_loop(0, N_BUCKETS, bucket, 0)

    return pl.pallas_call(body, name="bias_grad", out_shape=SDS((3, N_BUCKETS, 128), F32), in_specs=[VMEM_SPEC, VMEM_SPEC],
                          out_specs=VMEM_SPEC)(ds_sum, buckets)


def _sub_rows(d, r, n):
    return pl.ds(n * BLK * d + r, BLK) if d == 1 else pl.ds(n * BLK * d + r, BLK, stride=d)


def _head_spec(seq, g, part):
    return pl.BlockSpec((seq, HD), lambda b, hh: (b, part * (QW // HD) + 4 * g + hh))


def _attn_forward(g, qkv, bias, bsz, seq):
    d = DILATIONS[g]
    nblk = seq // d // BLK

    def body(q_ref, k_ref, v_ref, b_ref, o_ref, l_ref):
        hs = pl.program_id(1)
        for r in range(d):
            for n in range(nblk):
                rows = _sub_rows(d, r, n)
                qb = q_ref[rows, :].astype(BF16)
                s_c = _dot_nt(qb, k_ref[rows, :].astype(BF16)) * SCALE + b_ref[hs, :, BLK:]
                m = jnp.max(s_c, axis=1, keepdims=True)
                if n > 0:
                    prev = _sub_rows(d, r, n - 1)
                    s_p = _dot_nt(qb, k_ref[prev, :].astype(BF16)) * SCALE + b_ref[hs, :, :BLK]
                    m = jnp.maximum(m, jnp.max(s_p, axis=1, keepdims=True))
                p_c = jnp.exp(s_c - m)
                den = jnp.sum(p_c, axis=1, keepdims=True)
                acc = _dot(p_c.astype(BF16), v_ref[rows, :].astype(BF16))
                if n > 0:
                    p_p = jnp.exp(s_p - m)
                    den = den + jnp.sum(p_p, axis=1, keepdims=True)
                    acc = acc + _dot(p_p.astype(BF16), v_ref[prev, :].astype(BF16))
                o_ref[rows, :] = acc / den
                l_ref[rows, :] = jnp.broadcast_to(m + jnp.log(den), (BLK, HD))

    out_spec = pl.BlockSpec((seq, HD), lambda b, hh: (b, hh))
    return pl.pallas_call(
        body, name=f"attn_forward_{g}", out_shape=[SDS((bsz * seq, AW), F32)] * 2, grid=(bsz, 4),
        in_specs=[_head_spec(seq, g, part) for part in range(3)] + [pl.BlockSpec((4, BLK, 2 * BLK), lambda b, hh: (0, 0, 0))],
        out_specs=[out_spec, out_spec],
        compiler_params=_cp(("parallel", "parallel"), VMEM_CAP // 2),
    )(qkv, qkv, qkv, bias)


def _attn_backward(g, qkv, do, dl, bias, prev_out, bsz, seq):
    d = DILATIONS[g]
    nblk = seq // d // BLK

    def body(q_ref, k_ref, v_ref, do_ref, dl_ref, b_ref, *rest):
        dq_ref, dk_ref, dv_ref, db_ref = rest[-4:]
        hs = pl.program_id(1)

        @pl.when((pl.program_id(0) == 0) & (hs == 0))
        def _():
            db_ref[...] = jnp.zeros_like(db_ref)

        dk_ref[...] = jnp.zeros_like(dk_ref)
        dv_ref[...] = jnp.zeros_like(dv_ref)
        for r in range(d):
            for n in range(nblk):
                rows = _sub_rows(d, r, n)
                qb = q_ref[rows, :].astype(BF16)
                dob = do_ref[rows, :].astype(BF16)
                both = dl_ref[rows, :]
                lse, delta = both[:, 0:1], both[:, 64:65]
                dq = jnp.zeros((BLK, HD), F32)
                parts = [(rows, slice(BLK, 2 * BLK))]
                if n > 0:
                    parts.append((_sub_rows(d, r, n - 1), slice(0, BLK)))
                for keys, band in parts:
                    kb, vb = k_ref[keys, :].astype(BF16), v_ref[keys, :].astype(BF16)
                    p = jnp.exp(_dot_nt(qb, kb) * SCALE + b_ref[hs, :, band] - lse)
                    ds = p * (_dot_nt(dob, vb) - delta)
                    dsb = ds.astype(BF16)
                    dv_ref[keys, :] += _dot_tn(p.astype(BF16), dob)
                    dk_ref[keys, :] += _dot_tn(dsb, qb) * SCALE
                    dq = dq + _dot(dsb, kb) * SCALE
                    db_ref[hs, :, band] += ds
                dq_ref[rows, :] = dq

    qkv_spec = _head_spec(seq, g, 0)
    out_spec = pl.BlockSpec((seq, HD), lambda b, hh: (b, hh))
    band_spec = pl.BlockSpec((4, BLK, 2 * BLK), lambda b, hh: (0, 0, 0))
    ins = [qkv, qkv, qkv, do, dl, bias]
    in_specs = [_head_spec(seq, g, part) for part in range(3)] + [out_spec, out_spec, band_spec]
    aliases = {}
    if prev_out is not None:
        ins += list(prev_out)
        in_specs += [ANY] * 3
        aliases = {6: 0, 7: 1, 8: 2}
    dq, dk, dv, db = pl.pallas_call(
        body, name=f"attn_backward_{g}", out_shape=[SDS((bsz * seq, QW), F32)] * 3 + [SDS((4, BLK, 2 * BLK), F32)], grid=(bsz, 4),
        in_specs=in_specs, out_specs=[qkv_spec] * 3 + [band_spec], input_output_aliases=aliases,
        compiler_params=_cp(("arbitrary", "arbitrary"), VMEM_CAP // 2),
    )(*ins)
    return (dq, dk, dv), db


def _mix_forward(gates, og, lg, x2, tgt, gate, w_ao, w_co, w_o, conv_w, conv_b, ln_g, ln_b, bsz, seq, tm=256):
    t = x2.shape[0]
    spt = seq // tm

    def body(g_ref, o1, o2, o3, l1, l2, l3, x_ref, t_ref, gate_ref, wao_ref, wco_ref, wo_ref, cw_ref, cb_ref, lng_ref, lnb_ref,
             ain_ref, sin_ref, mrg_ref, dy_ref, aout_ref, sout_ref, yc_ref, o_ref, lj_ref, dxr_ref, vec_ref, dgate_ref, zc_ref):
        b, i = pl.program_id(0), pl.program_id(1)

        @pl.when((b == 0) & (i == 0))
        def _():
            vec_ref[...] = jnp.zeros_like(vec_ref)

        @pl.when(i == 0)
        def _():
            zc_ref[...] = jnp.zeros_like(zc_ref)
            dgate_ref[...] = jnp.zeros_like(dgate_ref)

        g_attn, u, bg, cg, g_conv, m_attn, m_conv = (g_ref[:, lo:hi].astype(F32) for lo, hi in GATE_COLS)
        la, lb, lc = l1[...], l2[...], l3[...]
        mx = jnp.maximum(la, jnp.maximum(lb, lc))
        ea, eb, ec = jnp.exp(la - mx), jnp.exp(lb - mx), jnp.exp(lc - mx)
        den = ea + eb + ec
        o = (ea * o1[...] + eb * o2[...] + ec * o3[...]) / den
        o_ref[...] = o
        lj_ref[...] = mx + jnp.log(den)
        a_in = o * (g_attn * _sig(g_attn))
        ain_ref[...] = a_in.astype(BF16)
        a_out = _dot(a_in.astype(BF16), wao_ref[...])
        aout_ref[...] = a_out.astype(BF16)
        z = cg * u
        rows = lax.broadcasted_iota(jnp.int32, (tm, D), 0)
        c6, c7 = zc_ref[6:7, :], zc_ref[7:8, :]
        z1 = jnp.where(rows == 0, c7, pltpu.roll(z, 1, 0))
        z2 = jnp.where(rows == 0, c6, jnp.where(rows == 1, c7, pltpu.roll(z, 2, 0)))
        zc_ref[...] = z[tm - 8:tm, :]
        y_conv = (cw_ref[0:1, :] * z2 + cw_ref[1:2, :] * z1 + cw_ref[2:3, :] * z) + cb_ref[...]
        yc_ref[...] = y_conv.astype(BF16)
        s_in = bg * y_conv * (g_conv * _sig(g_conv))
        sin_ref[...] = s_in.astype(BF16)
        s_out = _dot(s_in.astype(BF16), wco_ref[...])
        sout_ref[...] = s_out.astype(BF16)
        merged = _sig(m_attn) * a_out + _sig(m_conv) * s_out
        mrg_ref[...] = merged.astype(BF16)
        y = _dot(merged.astype(BF16), wo_ref[...])
        gate1 = 1.0 + gate_ref[0]
        r = ALPHA * x_ref[...] + gate1 * y
        mu = jnp.mean(r, axis=1, keepdims=True)
        rc = r - mu
        rstd = lax.rsqrt(jnp.mean(rc * rc, axis=1, keepdims=True) + LN_EPS)
        xhat = rc * rstd
        diff = (xhat * lng_ref[...] + lnb_ref[...]) - t_ref[...]
        dout = diff * (1.0 / D)
        vec_ref[0:1, :] += jnp.sum(dout * xhat, axis=0, keepdims=True)
        vec_ref[1:2, :] += jnp.sum(dout, axis=0, keepdims=True)
        vec_ref[2:3, :] += jnp.sum(diff * diff, axis=0, keepdims=True)
        dxh = dout * lng_ref[...]
        dr = rstd * (dxh - jnp.mean(dxh, axis=1, keepdims=True) - xhat * jnp.mean(dxh * xhat, axis=1, keepdims=True))
        dxr_ref[...] = ALPHA * dr
        dy_ref[...] = (dr * gate1).astype(BF16)
        dgate_ref[0] += jnp.sum(dr * y, axis=0, keepdims=True)

    tok = lambda w: pl.BlockSpec((tm, w), lambda b, i: (b * spt + i, 0))
    const = lambda s: pl.BlockSpec(s, lambda b, i: (0,) * len(s))
    per_seq = pl.BlockSpec((1, 1, D), lambda b, i: (b, 0, 0))
    outs = pl.pallas_call(
        body, name="mix_forward", grid=(bsz, spt),
        out_shape=[SDS((t, AW), BF16), SDS((t, D), BF16), SDS((t, D), BF16), SDS((t, D), BF16), SDS((t, D), BF16), SDS((t, D), BF16),
                   SDS((t, D), BF16), SDS((t, AW), F32), SDS((t, AW), F32), SDS((t, D), F32), SDS((8, D), F32), SDS((bsz, 1, D), F32)],
        in_specs=[tok(NGATE)] + [tok(AW)] * 6 + [tok(D), tok(D), per_seq, const((AW, D)), const((D, D)), const((D, D)),
                                                 const((3, D)), const((1, D)), const((1, D)), const((1, D))],
        out_specs=[tok(AW), tok(D), tok(D), tok(D), tok(D), tok(D), tok(D), tok(AW), tok(AW), tok(D), const((8, D)), per_seq],
        scratch_shapes=[pltpu.VMEM((8, D), F32)],
        compiler_params=_cp(("arbitrary", "arbitrary"), VMEM_CAP),
    )(gates, *og, *lg, x2, tgt, gate, w_ao, w_co, w_o, conv_w, conv_b, ln_g, ln_b)
    return outs


def _mix_backward(gates, dy, a_out, s_out, y_conv, o, lj, w_ao, w_co, w_o, conv_w, bsz, seq, tm=256):
    t = dy.shape[0]
    spt = seq // tm

    def body(g_ref, dy_ref, aout_ref, sout_ref, yc_ref, o_ref, lj_ref, wao_ref, wco_ref, wo_ref, cw_ref,
             dg_ref, do_ref, dl_ref, daout_ref, dsout_ref, vec_ref, car_ref):
        b, i = pl.program_id(0), pl.program_id(1)

        @pl.when((b == 0) & (i == 0))
        def _():
            vec_ref[...] = jnp.zeros_like(vec_ref)

        @pl.when(i == 0)
        def _():
            car_ref[...] = jnp.zeros_like(car_ref)

        g_attn, u, bg, cg, g_conv, m_attn, m_conv = (g_ref[:, lo:hi].astype(F32) for lo, hi in GATE_COLS)
        dmerged = _dot_nt(dy_ref[...], wo_ref[...])
        sa, sc = _sig(m_attn), _sig(m_conv)
        da_out = (dmerged * sa).astype(BF16)
        ds_out = (dmerged * sc).astype(BF16)
        daout_ref[...] = da_out
        dsout_ref[...] = ds_out
        dg_ref[:, 4608:5632] = (dmerged * aout_ref[...].astype(F32) * (sa * (1.0 - sa))).astype(BF16)
        dg_ref[:, 5632:6656] = (dmerged * sout_ref[...].astype(F32) * (sc * (1.0 - sc))).astype(BF16)
        da_in = _dot_nt(da_out, wao_ref[...])
        ds_in = _dot_nt(ds_out, wco_ref[...])
        sga = _sig(g_attn)
        o = o_ref[...]
        do = da_in * (g_attn * sga)
        do_ref[...] = do
        dg_ref[:, 0:512] = (da_in * o * (sga * (1.0 + g_attn * (1.0 - sga)))).astype(BF16)
        prod = do * o
        lane = lax.broadcasted_iota(jnp.int32, (tm, HD), 1)
        for j in range(4):
            cs = slice(j * HD, (j + 1) * HD)
            delta = jnp.sum(prod[:, cs], axis=1, keepdims=True)
            dl_ref[:, cs] = jnp.where(lane < 64, lj_ref[:, cs], delta)
        sgc = _sig(g_conv)
        silu_c = g_conv * sgc
        yc = yc_ref[...].astype(F32)
        dg_ref[:, 1536:2560] = (ds_in * yc * silu_c).astype(BF16)
        dg_ref[:, 3584:4608] = (ds_in * bg * yc * (sgc * (1.0 + g_conv * (1.0 - sgc)))).astype(BF16)
        dyc = ds_in * bg * silu_c
        rows = lax.broadcasted_iota(jnp.int32, (tm, D), 0)
        c0, c1 = car_ref[0:1, :], car_ref[1:2, :]
        n1 = jnp.where(rows == tm - 1, c0, pltpu.roll(dyc, tm - 1, 0))
        n2 = jnp.where(rows == tm - 2, c0, jnp.where(rows == tm - 1, c1, pltpu.roll(dyc, tm - 2, 0)))
        car_ref[...] = dyc[0:8, :]
        dz = cw_ref[2:3, :] * dyc + cw_ref[1:2, :] * n1 + cw_ref[0:1, :] * n2
        z = cg * u
        dg_ref[:, 512:1536] = (dz * cg).astype(BF16)
        dg_ref[:, 2560:3584] = (dz * u).astype(BF16)
        vec_ref[0:1, :] += jnp.sum(n2 * z, axis=0, keepdims=True)
        vec_ref[1:2, :] += jnp.sum(n1 * z, axis=0, keepdims=True)
        vec_ref[2:3, :] += jnp.sum(dyc * z, axis=0, keepdims=True)
        vec_ref[3:4, :] += jnp.sum(dyc, axis=0, keepdims=True)

    tok = lambda w: pl.BlockSpec((tm, w), lambda b, i: (b * spt + (spt - 1 - i), 0))
    const = lambda s: pl.BlockSpec(s, lambda b, i: (0,) * len(s))
    return pl.pallas_call(
        body, name="mix_backward", grid=(bsz, spt),
        out_shape=[SDS((t, NGATE), BF16), SDS((t, AW), F32), SDS((t, AW), F32), SDS((t, D), BF16), SDS((t, D), BF16), SDS((8, D), F32)],
        in_specs=[tok(NGATE), tok(D), tok(D), tok(D), tok(D), tok(AW), tok(AW), const((AW, D)), const((D, D)), const((D, D)), const((3, D))],
        out_specs=[tok(NGATE), tok(AW), tok(AW), tok(D), tok(D), const((8, D))],
        scratch_shapes=[pltpu.VMEM((8, D), F32)],
        compiler_params=_cp(("arbitrary", "arbitrary"), VMEM_CAP),
    )(gates, dy, a_out, s_out, y_conv, o, lj, w_ao, w_co, w_o, conv_w)


def _out_weight_grads(a_in, da_out, s_in, ds_out, merged, dy, tk=512):
    t = dy.shape[0]

    def body(ain_ref, da_ref, sin_ref, ds_ref, m_ref, dy_ref, gao_ref, gco_ref, go_ref):
        @pl.when(pl.program_id(0) == 0)
        def _():
            gao_ref[...] = jnp.zeros_like(gao_ref)
            gco_ref[...] = jnp.zeros_like(gco_ref)
            go_ref[...] = jnp.zeros_like(go_ref)

        gao_ref[...] += _dot_tn(ain_ref[...], da_ref[...])
        gco_ref[...] += _dot_tn(sin_ref[...], ds_ref[...])
        go_ref[...] += _dot_tn(m_ref[...], dy_ref[...])

    tok = lambda w: pl.BlockSpec((tk, w), lambda i: (i, 0))
    const = lambda s: pl.BlockSpec(s, lambda i: (0, 0))
    return pl.pallas_call(
        body, name="out_weight_grads", grid=(t // tk,), out_shape=[SDS((AW, D), F32), SDS((D, D), F32), SDS((D, D), F32)],
        in_specs=[tok(AW), tok(D), tok(D), tok(D), tok(D), tok(D)], out_specs=[const((AW, D)), const((D, D)), const((D, D))],
        compiler_params=_cp(("arbitrary",), VMEM_CAP),
    )(a_in, da_out, s_in, ds_out, merged, dy)


def _input_grad(dq, dk, dv, dgates, w, x2, dxr, sc1p, seq, sums, tm=512):
    t = x2.shape[0]
    spt = seq // tm
    bsz = t // seq
    n = len(sums)
    gblk = NGATE // 4
    nsteps = 3 + 4

    def body(dq_ref, dk_ref, dv_ref, dg_ref, wq_ref, wg_ref, x_ref, dxr_ref, sc_ref, *rest):
        src, (dx_ref, dsh_ref, dsc_ref), land = rest[:n], rest[n:n + 3], rest[n + 3:2 * n + 3]
        acc_ref, send_sems, recv_sems = rest[2 * n + 3:]
        i, j = pl.program_id(0), pl.program_id(1)
        px, py, pc = _place()
        chips = [(1 - px, py), (px, 1 - py), (1 - px, 1 - py)]
        copies = [pltpu.make_async_remote_copy(src_ref=src[a].at[2 * cx + cy], dst_ref=land[a].at[r], send_sem=send_sems.at[3 * a + r],
                                               recv_sem=recv_sems.at[3 * a + r], device_id=(cx, cy, pc), device_id_type=MESH)
                  for a in range(n) for r, (cx, cy) in enumerate(chips)]

        @pl.when((i == 0) & (j == 0))
        def _():
            for cp in copies:
                cp.start()

        for k, ref in enumerate((dq_ref, dk_ref, dv_ref)):
            @pl.when(j == k)
            def _(k=k, ref=ref):
                part = _dot_nt(ref[...].astype(BF16), wq_ref[...])
                if k == 0:
                    acc_ref[...] = part
                else:
                    acc_ref[...] += part

        @pl.when(j >= 3)
        def _():
            acc_ref[...] += _dot_nt(dg_ref[...], wg_ref[...])

        @pl.when(j == nsteps - 1)
        def _():
            dh = acc_ref[...]
            dx_ref[...] = dh * sc_ref[0] + dxr_ref[...]

            @pl.when(i % spt == 0)
            def _():
                dsh_ref[...] = jnp.zeros_like(dsh_ref)
                dsc_ref[...] = jnp.zeros_like(dsc_ref)

            dsh_ref[0] += jnp.sum(dh, axis=0, keepdims=True)
            dsc_ref[0] += jnp.sum(dh * x_ref[...], axis=0, keepdims=True)

        @pl.when((i == t // tm - 1) & (j == nsteps - 1))
        def _():
            for cp in copies:
                cp.wait()

    whole = pl.BlockSpec((tm, QW), lambda i, j: (i, 0))
    row = pl.BlockSpec((tm, D), lambda i, j: (i, 0))
    per_seq = pl.BlockSpec((1, 1, D), lambda i, j: (i // spt, 0, 0))
    outs = pl.pallas_call(
        body, name="input_grad", grid=(t // tm, nsteps),
        out_shape=[SDS((t, D), F32), SDS((bsz, 1, D), F32), SDS((bsz, 1, D), F32)] + [SDS((3,) + s.shape[1:], BF16) for s in sums],
        in_specs=[whole, whole, whole, pl.BlockSpec((tm, gblk), lambda i, j: (i, jnp.clip(j - 3, 0, 3))),
                  pl.BlockSpec((D, QW), lambda i, j: (0, jnp.minimum(j, 2))),
                  pl.BlockSpec((pl.Element(D), pl.Element(gblk)), lambda i, j: (0, pl.multiple_of(3 * QW + gblk * jnp.clip(j - 3, 0, 3), 128))),
                  row, row, per_seq] + [ANY] * n,
        out_specs=[row, per_seq, per_seq] + [ANY] * n,
        scratch_shapes=[pltpu.VMEM((tm, D), F32), pltpu.SemaphoreType.DMA((3 * NCHIP,)), pltpu.SemaphoreType.DMA((3 * NCHIP,))],
        compiler_params=_cp(("arbitrary", "arbitrary"), VMEM_CAP, side=True),
    )(dq, dk, dv, dgates, w, w, x2, dxr, sc1p, *sums)
    return outs[0], outs[1], outs[2], outs[3:]


def _in_weight_grad(ht, src, col0, prev, name):
    t = ht.shape[1]
    ncols = src.shape[1] // TN

    def body(ht_ref, s_ref, *rest):
        rest[-1][...] = _dot(ht_ref[...], s_ref[...].astype(BF16))

    ins = [ht, src]
    in_specs = [pl.BlockSpec((D, t), lambda j: (0, 0)), pl.BlockSpec((t, TN), lambda j: (0, j))]
    aliases = {}
    if prev is not None:
        ins.append(prev)
        in_specs.append(ANY)
        aliases = {2: 0}
    return pl.pallas_call(
        body, name=name, grid=(ncols,), out_shape=SDS((D, NCOL), F32), in_specs=in_specs,
        out_specs=pl.BlockSpec((D, TN), lambda j: (0, col0 + j)), input_output_aliases=aliases,
        compiler_params=_cp(("arbitrary",), VMEM_CAP),
    )(*ins)


def _sum_partials(gathered):
    def body(g_ref, o_ref):
        acc = g_ref[0]
        for k in range(1, 8):
            acc = acc + g_ref[k]
        o_ref[...] = acc

    return pl.pallas_call(body, name="sum_partials", out_shape=SDS(gathered.shape[1:], F32), in_specs=[VMEM_SPEC], out_specs=VMEM_SPEC)(gathered)


def _adamw(w, g, m, v, name, tr=256):
    r, cdim = w.shape
    tr = tr if cdim <= D else tr // 2
    tr = tr if (r % tr == 0 and r > tr) else r

    def body(w_ref, g_ref, m_ref, v_ref, d_ref, nm_ref, nv_ref):
        gv = g_ref[...]
        nm = B1 * m_ref[...] + (1.0 - B1) * gv
        nv = B2 * v_ref[...] + (1.0 - B2) * (gv * gv)
        m_hat = nm / (1.0 - B1 ** STEP)
        v_hat = nv / (1.0 - B2 ** STEP)
        d_ref[...] = -LR * (m_hat / (jnp.sqrt(v_hat) + EPS) + WD * w_ref[...])
        nm_ref[...] = nm
        nv_ref[...] = nv

    spec = pl.BlockSpec((tr, cdim), lambda i: (i, 0))
    return pl.pallas_call(
        body, name=name, grid=(r // tr,), out_shape=[SDS((r, cdim), F32)] * 3, in_specs=[spec] * 4, out_specs=[spec] * 3,
        compiler_params=_cp(("parallel",), VMEM_CAP // 2),
    )(w, g, m, v)


def _t5_bucket(dist):
    n = jnp.maximum(dist, 1).astype(F32)
    large = MAX_EXACT + (jnp.log(n / MAX_EXACT) / math.log(MAX_DISTANCE / MAX_EXACT) * (N_BUCKETS - MAX_EXACT)).astype(jnp.int32)
    large = jnp.minimum(large, N_BUCKETS - 1)
    return jnp.where(dist < MAX_EXACT, dist, large)


def _band_buckets():
    a = jnp.arange(BLK)[:, None]
    b = jnp.arange(2 * BLK)[None, :]
    steps = jnp.maximum(a + BLK - b, 0)
    return jnp.stack([_t5_bucket(steps * d) for d in DILATIONS]).astype(jnp.int32)


def _pad_rows(a, rows=8):
    return jnp.pad(a, ((0, rows - a.shape[0]), (0, 0)))


def kernel(x, c, w_ada, b_ada, w_in, conv_w, conv_b, rel_bias, w_attn_out, w_conv_out, w_o, ln_g, ln_b, loss_target, m_w_ada, m_b_ada, m_w_in, m_conv_w, m_conv_b, m_rel_bias, m_w_attn_out, m_w_conv_out, m_w_o, m_ln_g, m_ln_b, v_w_ada, v_b_ada, v_w_in, v_conv_w, v_conv_b, v_rel_bias, v_w_attn_out, v_w_conv_out, v_w_o, v_ln_g, v_ln_b):
    bsz, seq, _ = x.shape
    t = bsz * seq
    mx, my, mc = _place()
    chip = 2 * mx + my
    dev = 4 * mx + 2 * my + mc
    x2 = x.reshape(t, D)
    tgt = loss_target.reshape(t, D)

    mine = [_to_bf16_window(a, w[0], f"to_bf16_{a}") for a, w in enumerate((w_in, w_attn_out, w_conv_out, w_o))]

    n_ada = w_ada.shape[2]
    n_cw = conv_w.shape[2]
    c_and_cw = jnp.concatenate([_pad_rows(c), jnp.pad(conv_w[0], ((0, 5), (0, D - n_cw)))], axis=0)
    firsts = _all_gather8(c_and_cw, "gather_c_conv_w")
    c_all = firsts[:, 0:bsz, :].reshape(8 * bsz, D)
    conv_w_f = firsts[0::2, 8:11, 0:n_cw].transpose(1, 0, 2).reshape(3, D)
    b_cols = lax.dynamic_slice(b_ada, (0, chip * n_ada), (1, n_ada))
    mod_part = _ada_forward(c_all, w_ada[0], b_cols)
    mod_parts = _all_gather8(mod_part, "gather_mod")
    mod_all = mod_parts[0::2].transpose(1, 0, 2).reshape(8 * bsz, 3 * D)
    mod = lax.dynamic_slice(mod_all, (dev * bsz, 0), (bsz, 3 * D))
    shift = mod[:, 0:D].reshape(bsz, 1, D)
    sc1p = 1.0 + mod[:, D:2 * D].reshape(bsz, 1, D)
    gate = mod[:, 2 * D:].reshape(bsz, 1, D)

    h, ht = _modulate(x2, sc1p, shift, seq)
    tab = lax.dynamic_index_in_dim(jnp.asarray(_tile_tables()), chip, 0, keepdims=False)
    qkv, gates, (w_in_f, w_ao_f, w_co_f, w_o_f) = _project_gather(h, mine, tab)
    buckets = _band_buckets()
    bias = _bias_tables(rel_bias, buckets)
    og, lg = [], []
    for g in range(3):
        o_g, l_g = _attn_forward(g, qkv, bias[g], bsz, seq)
        og.append(o_g)
        lg.append(l_g)
    (a_in, s_in, merged, dy, a_out, s_out, y_conv, o, lj, dxr, vec_f, dgate) = _mix_forward(
        gates, og, lg, x2, tgt, gate, w_ao_f, w_co_f, w_o_f, conv_w_f, conv_b, ln_g, ln_b, bsz, seq)

    dgates, do, dl, da_out, ds_out, vec_b = _mix_backward(gates, dy, a_out, s_out, y_conv, o, lj, w_ao_f, w_co_f, w_o_f, conv_w_f, bsz, seq)
    g_ao, g_co, g_o = _out_weight_grads(a_in, da_out, s_in, ds_out, merged, dy)
    dqkv, dbs = None, []
    for g in range(3):
        dqkv, db = _attn_backward(g, qkv, do, dl, bias[g], dqkv, bsz, seq)
        dbs.append(db)
    dq, dk, dv = dqkv
    drb = _bias_grad(jnp.stack(dbs), buckets)
    drb = drb[:, :, 0:4].transpose(1, 0, 2).reshape(N_BUCKETS, 12)
    g_in = None
    for n, src in enumerate((dq, dk, dv, dgates)):
        g_in = _in_weight_grad(ht, src, n * NQT, g_in, f"in_weight_grad_{n}")

    grads = [g_in, g_ao, g_co, g_o]
    got = _swap_halves(grads)
    sums = [_chip_sum(a, grads[a], got[a], f"chip_sum_{a}") for a in range(4)]
    grad_x, dshift, dscale, landed = _input_grad(dq, dk, dv, dgates, w_in_f, x2, dxr, sc1p, seq, [s[1] for s in sums])
    halves = [_reduce_mine(a, sums[a][0], landed[a], f"reduce_mine_{a}") for a in range(4)]
    gw_in, gw_ao, gw_co, gw_o = _join_halves(halves)

    dmod = jnp.concatenate([dshift, dscale, dgate], axis=2).reshape(bsz * 3, D)
    drb_row = jnp.pad(drb.reshape(1, N_BUCKETS * 12), ((0, 0), (0, D - N_BUCKETS * 12)))
    packed = jnp.concatenate([vec_f, vec_b, _pad_rows(dmod), _pad_rows(drb_row)], axis=0)
    gathered = _all_gather8(packed, "gather_small")
    small = _sum_partials(gathered)
    g_ln_g, g_ln_b, loss_lanes = small[0:1], small[1:2], small[2:3]
    g_conv_w_full, g_conv_b = small[8:11], small[11:12]
    g_rel_bias = small[24, 0:N_BUCKETS * 12].reshape(N_BUCKETS, 12)
    loss = 0.5 / D * jnp.sum(loss_lanes)
    dmod_all = gathered[:, 16:16 + 3 * bsz, :].reshape(8 * bsz, 3 * D)
    dmod_cols = lax.dynamic_slice(dmod_all, (0, chip * n_ada), (8 * bsz, n_ada))
    gw_ada, gb_ada = _ada_backward(c_all, dmod_cols, dmod_all)
    g_conv_w = lax.dynamic_slice(g_conv_w_full, (0, chip * n_cw), (3, n_cw))

    names = ["w_ada", "b_ada", "w_in", "conv_w", "conv_b", "rel_bias", "w_attn_out", "w_conv_out", "w_o", "ln_g", "ln_b"]
    two_d = lambda a: a.reshape(a.shape[-2:]) if a.ndim == 3 else a
    weights = dict(zip(names, map(two_d, (w_ada, b_ada, w_in, conv_w, conv_b, rel_bias, w_attn_out, w_conv_out, w_o, ln_g, ln_b))))
    ms = dict(zip(names, map(two_d, (m_w_ada, m_b_ada, m_w_in, m_conv_w, m_conv_b, m_rel_bias, m_w_attn_out, m_w_conv_out, m_w_o, m_ln_g, m_ln_b))))
    vs = dict(zip(names, map(two_d, (v_w_ada, v_b_ada, v_w_in, v_conv_w, v_conv_b, v_rel_bias, v_w_attn_out, v_w_conv_out, v_w_o, v_ln_g, v_ln_b))))
    grads = dict(zip(names, (gw_ada, gb_ada, gw_in, g_conv_w, g_conv_b, g_rel_bias, gw_ao, gw_co, gw_o, g_ln_g, g_ln_b)))
    shapes = dict(zip(names, (w_ada, b_ada, w_in, conv_w, conv_b, rel_bias, w_attn_out, w_conv_out, w_o, ln_g, ln_b)))
    deltas, new_m, new_v = {}, {}, {}
    for n in names:
        deltas[n], new_m[n], new_v[n] = _adamw(weights[n], grads[n], ms[n], vs[n], f"adamw_{n}")
    shaped = lambda d: [d[n].reshape(shapes[n].shape) for n in names]
    return (loss, grad_x.reshape(bsz, seq, D), *shaped(grads), *shaped(deltas), *shaped(new_m), *shaped(new_v))
```

```python
import math

import numpy as np
import jax
import jax.numpy as jnp
from jax import lax
from jax.experimental import pallas as pl
from jax.experimental.pallas import tpu as pltpu

F32 = jnp.float32
BF16 = jnp.bfloat16
SDS = jax.ShapeDtypeStruct
MESH = pl.DeviceIdType.MESH
ANY = pl.BlockSpec(memory_space=pl.ANY)
VMEM_SPEC = pl.BlockSpec(memory_space=pltpu.VMEM)

D = 1024
HD = 128
BLK = 128
QW = 1536
AW = 512
NGATE = 6656
GATE_COLS = ((0, 512), (512, 1536), (1536, 2560), (2560, 3584), (3584, 4608), (4608, 5632), (5632, 6656))
NCOL = 3 * QW + NGATE
TN = 512
NQT = QW // TN
NPT = NCOL // TN
DILATIONS = (1, 4, 16)
N_BUCKETS, MAX_EXACT, MAX_DISTANCE = 32, 16, 2048
ALPHA = 2.0 ** 0.25
LN_EPS = 1e-5
NEG = -1e30
SCALE = HD ** -0.5
LR, B1, B2, EPS, WD, STEP = 0.001, 0.9, 0.999, 1e-08, 0.01, 10
NCHIP = 4
VMEM_CAP = 60 * 2 ** 20


def _cp(sem=None, vmem=None, side=False):
    return pltpu.CompilerParams(dimension_semantics=sem, vmem_limit_bytes=vmem, has_side_effects=side)


def _dot(a, b):
    return jnp.dot(a, b, preferred_element_type=F32)


def _dot_nt(a, b):
    return lax.dot_general(a, b, (((1,), (1,)), ((), ())), preferred_element_type=F32)


def _dot_tn(a, b):
    return lax.dot_general(a, b, (((0,), (0,)), ((), ())), preferred_element_type=F32)


def _sig(x):
    return 1.0 / (1.0 + jnp.exp(-x))


def _place():
    x, y, c = lax.axis_index("x"), lax.axis_index("y"), lax.axis_index("c")
    return x, y, c


def _all_gather8(v, name):
    r, cdim = v.shape

    def body(v_ref, out_ref, send_sems, recv_sems, local_sem):
        x, y, c = _place()
        me = 4 * x + 2 * y + c
        peers = [(x, y, 1 - c), (1 - x, y, c), (x, 1 - y, c), (1 - x, 1 - y, c),
                 (1 - x, y, 1 - c), (x, 1 - y, 1 - c), (1 - x, 1 - y, 1 - c)]
        mine = pltpu.make_async_copy(v_ref, out_ref.at[me], local_sem)
        mine.start()

        def copy(k, block, to):
            return pltpu.make_async_remote_copy(src_ref=v_ref, dst_ref=out_ref.at[block], send_sem=send_sems.at[k],
                                                recv_sem=recv_sems.at[k], device_id=to, device_id_type=MESH)

        sends = [copy(k, me, p) for k, p in enumerate(peers)]
        for cp in sends:
            cp.start()
        for k, (px, py, pc) in enumerate(peers):
            copy(k, 4 * px + 2 * py + pc, (px, py, pc)).wait_recv()
        for cp in sends:
            cp.wait_send()
        mine.wait()

    return pl.pallas_call(
        body, name=name, out_shape=SDS((8, r, cdim), v.dtype), in_specs=[VMEM_SPEC], out_specs=VMEM_SPEC,
        scratch_shapes=[pltpu.SemaphoreType.DMA((7,)), pltpu.SemaphoreType.DMA((7,)), pltpu.SemaphoreType.DMA(())],
        compiler_params=_cp(side=True),
    )(v)


W_CUTS = (("col", D, NCOL // NCHIP), ("col", AW, D // NCHIP), ("row", D // NCHIP, D), ("row", D // NCHIP, D))
W_FULL = ((D, NCOL), (AW, D), (D, D), (D, D))


def _shard_window(ref, cut, k, half):
    kind, nr, nc = cut
    hr = nr // 2
    if kind == "col":
        rows = pl.ds(0, nr) if half is None else pl.ds(pl.multiple_of(half * hr, 16), hr)
        return ref.at[rows, pl.ds(pl.multiple_of(k * nc, 128), nc)]
    if half is None:
        return ref.at[pl.ds(pl.multiple_of(k * nr, 16), nr), :]
    return ref.at[pl.ds(pl.multiple_of(k * nr + half * hr, 16), hr), :]


def _half_rows(ref, cut, half):
    hr = cut[1] // 2
    return ref.at[pl.ds(pl.multiple_of(half * hr, 16), hr), :]


def _to_bf16_window(a, w, name):
    kind, nr, nc = W_CUTS[a]
    x, y, _ = _place()
    chip = jnp.reshape(2 * x + y, (1,)).astype(jnp.int32)
    tr = min(nr, 256)

    def body(c_ref, w_ref, o_ref):
        o_ref[...] = w_ref[...].astype(BF16)

    out_map = (lambda i, cr: (i, cr[0])) if kind == "col" else (lambda i, cr: (cr[0] * (nr // tr) + i, 0))
    return pl.pallas_call(
        body, name=name, out_shape=SDS(W_FULL[a], BF16),
        grid_spec=pltpu.PrefetchScalarGridSpec(num_scalar_prefetch=1, grid=(nr // tr,),
                                               in_specs=[pl.BlockSpec((tr, nc), lambda i, cr: (i, 0))], out_specs=pl.BlockSpec((tr, nc), out_map)),
        compiler_params=_cp(("arbitrary",)),
    )(chip, w)


def _swap_halves(grads):
    n = len(grads)
    shapes = []
    for a in range(n):
        kind, nr, nc = W_CUTS[a]
        shapes.append((W_FULL[a][0] // 2, W_FULL[a][1]) if kind == "col" else (NCHIP, nr // 2, nc))

    def pieces(a, ref, land, half):
        kind, nr, nc = W_CUTS[a]
        if kind == "col":
            hr = nr // 2
            return [(ref.at[pl.ds(pl.multiple_of(half * hr, 16), hr), :], land)]
        return [(_shard_window(ref, W_CUTS[a], k, half), land.at[k]) for k in range(NCHIP)]

    def body(*refs):
        src, land = refs[:n], refs[n:2 * n]
        send_sems, recv_sems = refs[2 * n:]
        x, y, c = _place()
        sibling = (x, y, 1 - c)
        sends = []
        k = 0
        for a in range(n):
            for s, d in pieces(a, src[a], land[a], 1 - c):
                cp = pltpu.make_async_remote_copy(src_ref=s, dst_ref=d, send_sem=send_sems.at[k], recv_sem=recv_sems.at[k],
                                                  device_id=sibling, device_id_type=MESH)
                cp.start()
                sends.append(cp)
                k += 1
        for cp in sends:
            cp.wait()

    n_sems = sum(1 if W_CUTS[a][0] == "col" else NCHIP for a in range(n))
    return pl.pallas_call(
        body, name="swap_grad_halves", out_shape=[SDS(s, F32) for s in shapes], in_specs=[ANY] * n, out_specs=[ANY] * n,
        scratch_shapes=[pltpu.SemaphoreType.DMA((n_sems,)), pltpu.SemaphoreType.DMA((n_sems,))],
        compiler_params=_cp(side=True),
    )(*grads)


def _chip_sum(a, grad, got, name):
    kind, nr, nc = W_CUTS[a]
    hr = nr // 2
    c = lax.axis_index("c")
    cidx = jnp.reshape(c, (1,)).astype(jnp.int32)

    def body(c_ref, g_ref, r_ref, f_ref, b_ref):
        s = g_ref[...] + r_ref[...]
        f_ref[...] = s.reshape(f_ref.shape)
        b_ref[...] = s.astype(BF16).reshape(b_ref.shape)

    if kind == "col":
        in_specs = [pl.BlockSpec((hr, nc), lambda k, cr: (cr[0], k)), pl.BlockSpec((hr, nc), lambda k, cr: (0, k))]
    else:
        grad = grad.reshape(NCHIP, 2, hr, nc)
        in_specs = [pl.BlockSpec((1, 1, hr, nc), lambda k, cr: (k, cr[0], 0, 0)), pl.BlockSpec((1, hr, nc), lambda k, cr: (k, 0, 0))]
    out_specs = [pl.BlockSpec((1, hr, nc), lambda k, cr: (k, 0, 0))] * 2
    return pl.pallas_call(
        body, name=name, out_shape=[SDS((NCHIP, hr, nc), F32), SDS((NCHIP, hr, nc), BF16)],
        grid_spec=pltpu.PrefetchScalarGridSpec(num_scalar_prefetch=1, grid=(NCHIP,), in_specs=in_specs, out_specs=out_specs),
        compiler_params=_cp(("arbitrary",), VMEM_CAP),
    )(cidx, grad, got)


def _reduce_mine(a, mine_f32, got, name):
    kind, nr, nc = W_CUTS[a]
    hr = nr // 2
    x, y, c = _place()
    where = jnp.stack([2 * x + y, c]).astype(jnp.int32)
    tr = min(hr, 256)

    def body(w_ref, m_ref, g_ref, o_ref):
        o_ref[...] = ((m_ref[0] + g_ref[0].astype(F32)) + g_ref[1].astype(F32)) + g_ref[2].astype(F32)

    return pl.pallas_call(
        body, name=name, out_shape=SDS((nr, nc), F32),
        grid_spec=pltpu.PrefetchScalarGridSpec(
            num_scalar_prefetch=1, grid=(hr // tr,),
            in_specs=[pl.BlockSpec((1, tr, nc), lambda i, wr: (wr[0], i, 0)), pl.BlockSpec((3, tr, nc), lambda i, wr: (0, i, 0))],
            out_specs=pl.BlockSpec((tr, nc), lambda i, wr: (wr[1] * (hr // tr) + i, 0))),
        compiler_params=_cp(("arbitrary",), VMEM_CAP),
    )(where, mine_f32, got)


def _join_halves(fulls):
    n = len(fulls)

    def body(*refs):
        full = refs[n:2 * n]
        send_sems, recv_sems = refs[2 * n:]
        x, y, c = _place()
        sibling = (x, y, 1 - c)

        def swap(a, half):
            rows = _half_rows(full[a], W_CUTS[a], half)
            return pltpu.make_async_remote_copy(src_ref=rows, dst_ref=rows, send_sem=send_sems.at[a], recv_sem=recv_sems.at[a],
                                                device_id=sibling, device_id_type=MESH)

        sends = [swap(a, c) for a in range(n)]
        for cp in sends:
            cp.start()
        for a, cp in enumerate(sends):
            cp.wait_send()
            swap(a, 1 - c).wait_recv()

    return pl.pallas_call(
        body, name="join_grad_halves", out_shape=[SDS((W_CUTS[a][1], W_CUTS[a][2]), F32) for a in range(n)],
        in_specs=[ANY] * n, out_specs=[ANY] * n,
        scratch_shapes=[pltpu.SemaphoreType.DMA((n,)), pltpu.SemaphoreType.DMA((n,))],
        input_output_aliases={a: a for a in range(n)}, compiler_params=_cp(side=True),
    )(*fulls)


def _ada_forward(c_all, w_ada, b_cols):
    nb, nc = c_all.shape[0], w_ada.shape[1]

    def body(c_ref, w_ref, b_ref, o_ref):
        cv = c_ref[...]
        sc = (cv * _sig(cv)).astype(BF16)
        o_ref[...] = _dot(sc, w_ref[...].astype(BF16)) + b_ref[...]

    return pl.pallas_call(body, name="ada_forward", out_shape=SDS((nb, nc), F32), compiler_params=_cp(vmem=VMEM_CAP // 2))(c_all, w_ada, b_cols)


def _ada_backward(c_all, dmod_cols, dmod_all):
    nb, nc = dmod_cols.shape

    def body(c_ref, d_ref, a_ref, gw_ref, gb_ref):
        cv = c_ref[...]
        sc = (cv * _sig(cv)).astype(BF16)
        gw_ref[...] = _dot_tn(sc, d_ref[...].astype(BF16))
        gb_ref[...] = jnp.sum(a_ref[...], axis=0, keepdims=True)

    return pl.pallas_call(body, name="ada_backward", out_shape=[SDS((D, nc), F32), SDS((1, dmod_all.shape[1]), F32)],
                          compiler_params=_cp(vmem=VMEM_CAP // 2))(c_all, dmod_cols, dmod_all)


def _modulate(x2, sc1p, shift, seq, tm=256):
    t = x2.shape[0]
    spt = seq // tm

    def body(x_ref, sc_ref, sh_ref, h_ref, ht_ref):
        h = x_ref[...] * sc_ref[0] + sh_ref[0]
        h_ref[...] = h.astype(BF16)
        ht_ref[...] = h.T.astype(BF16)

    per_seq = pl.BlockSpec((1, 1, D), lambda i: (i // spt, 0, 0))
    return pl.pallas_call(
        body, name="modulate", out_shape=[SDS((t, D), BF16), SDS((D, t), BF16)], grid=(t // tm,),
        in_specs=[pl.BlockSpec((tm, D), lambda i: (i, 0)), per_seq, per_seq],
        out_specs=[pl.BlockSpec((tm, D), lambda i: (i, 0)), pl.BlockSpec((D, tm), lambda i: (0, i))],
        compiler_params=_cp(("parallel",)),
    )(x2, sc1p, shift)


TW = 256
TPS = NCOL // NCHIP // TW
NT = NCOL // TW
NQKV_T = 3 * QW // TW


def _tile_tables():
    tabs = np.zeros((NCHIP, 3, NT), np.int32)
    for me in range(NCHIP):
        tiles = [TPS * (me ^ (s // TPS)) + s % TPS for s in range(NT)]
        tabs[me, 0] = tiles
        for row, (lo, hi) in enumerate(((0, NQKV_T), (NQKV_T, NT))):
            mine = [w - lo if lo <= w < hi else None for w in tiles]
            held = next(m for m in mine if m is not None)
            for s, m in enumerate(mine):
                held = held if m is None else m
                tabs[me, 1 + row, s] = held
    return tabs


def _project_gather(h, fulls, tab):
    t = h.shape[0]
    n = len(fulls)

    def body(tab_ref, h_ref, *rest):
        qkv_ref, g_ref = rest[n], rest[n + 1]
        full = rest[n + 2:2 * n + 2]
        w_buf, tile_sems, send_sems, recv_sems = rest[2 * n + 2:]
        s = pl.program_id(0)
        x, y, c = _place()
        me = 2 * x + y
        peers = [(x, 1 - y), (1 - x, y), (1 - x, 1 - y)]
        sibling = (x, y, 1 - c)

        def hop(a, r, stage, chip, half, to):
            window = _shard_window(full[a], W_CUTS[a], chip, half)
            k = 6 * a + 2 * r + stage
            return pltpu.make_async_remote_copy(src_ref=window, dst_ref=window, send_sem=send_sems.at[k], recv_sem=recv_sems.at[k],
                                                device_id=to, device_id_type=MESH)

        def send(a, r):
            return hop(a, r, 0, me, c, (*peers[r], c))

        def arrive(a, r):
            px, py = peers[r]
            chip = 2 * px + py
            hop(a, r, 0, chip, c, (px, py, c)).wait_recv()
            hop(a, r, 1, chip, c, sibling).start()
            hop(a, r, 1, chip, 1 - c, sibling).wait_recv()

        def tile(step, slot):
            col = pl.multiple_of(tab_ref[0, step] * TW, TW)
            return pltpu.make_async_copy(full[0].at[:, pl.ds(col, TW)], w_buf.at[slot], tile_sems.at[slot])

        @pl.when(s == 0)
        def _():
            send(0, 0).start()
            send(0, 1).start()
            tile(0, 0).start()

        slot = s % 2
        tile(s, slot).wait()

        @pl.when((s + 1 < NT) & ((s + 1) % TPS != 0))
        def _():
            tile(s + 1, 1 - slot).start()

        is_qkv = tab_ref[0, s] < NQKV_T
        for k in range(2):
            @pl.when(slot == k)
            def _(k=k):
                acc = _dot(h_ref[...], w_buf[k])

                @pl.when(is_qkv)
                def _():
                    qkv_ref[...] = acc.astype(BF16)

                @pl.when(jnp.logical_not(is_qkv))
                def _():
                    g_ref[...] = acc.astype(BF16)

        for r in range(3):
            @pl.when(s + 1 == TPS * (r + 1))
            def _(r=r):
                arrive(0, r)
                tile(s + 1, 1 - slot).start()
                if r == 0:
                    send(0, 2).start()
                    for a in range(1, n):
                        for q in range(3):
                            send(a, q).start()

        @pl.when(s == NT - 1)
        def _():
            for a in range(1, n):
                for r in range(3):
                    arrive(a, r)
            for a in range(n):
                for r in range(3):
                    send(a, r).wait_send()
                    px, py = peers[r]
                    hop(a, r, 1, 2 * px + py, c, sibling).wait_send()

    outs = pl.pallas_call(
        body, name="project_gather", out_shape=[SDS((t, 3 * QW), BF16), SDS((t, NGATE), BF16)] + [SDS(s, BF16) for s in W_FULL],
        grid_spec=pltpu.PrefetchScalarGridSpec(
            num_scalar_prefetch=1, grid=(NT,),
            in_specs=[pl.BlockSpec((t, D), lambda s, tab: (0, 0))] + [ANY] * n,
            out_specs=[pl.BlockSpec((t, TW), lambda s, tab: (0, tab[1, s])), pl.BlockSpec((t, TW), lambda s, tab: (0, tab[2, s]))] + [ANY] * n,
            scratch_shapes=[pltpu.VMEM((2, D, TW), BF16), pltpu.SemaphoreType.DMA((2,)),
                            pltpu.SemaphoreType.DMA((6 * n,)), pltpu.SemaphoreType.DMA((6 * n,))]),
        input_output_aliases={2 + a: 2 + a for a in range(n)},
        compiler_params=_cp(("arbitrary",), VMEM_CAP, side=True),
    )(tab, h, *fulls)
    return outs[0], outs[1], outs[2:]


def _bias_tables(rel_bias, buckets):
    def body(tab_ref, bk_ref, o_ref):
        a = lax.broadcasted_iota(jnp.int32, (BLK, 2 * BLK), 0)
        b = lax.broadcasted_iota(jnp.int32, (BLK, 2 * BLK), 1)
        steps = a + BLK - b
        valid = (steps >= 0) & (steps <= BLK)
        for g in range(3):
            bk = bk_ref[g]
            for j in range(4):
                def pick(kk, acc, bk=bk, col=4 * g + j):
                    return jnp.where(bk == kk, tab_ref[kk, col], acc)

                acc = lax.fori_loop(0, N_BUCKETS, pick, jnp.zeros((BLK, 2 * BLK), F32))
                o_ref[g, j] = jnp.where(valid, acc, NEG)

    return pl.pallas_call(
        body, name="bias_tables", out_shape=SDS((3, 4, BLK, 2 * BLK), F32),
        in_specs=[pl.BlockSpec(memory_space=pltpu.SMEM), VMEM_SPEC], out_specs=VMEM_SPEC,
    )(rel_bias, buckets)


def _bias_grad(ds_sum, buckets):
    def body(ds_ref, bk_ref, o_ref):
        lane = lax.broadcasted_iota(jnp.int32, (1, 128), 1)
        for g in range(3):
            def bucket(kk, carry, g=g):
                row = jnp.zeros((1, 128), F32)
                for j in range(4):
                    v = jnp.where(bk_ref[g] == kk, ds_ref[g, j], 0.0)
                    s = jnp.sum(jnp.sum(v, axis=1, keepdims=True), axis=0, keepdims=True)
                    row = jnp.where(lane == j, s, row)
                o_ref[g, pl.ds(kk, 1), :] = row
                return carry

            lax.fori_loop(0, N_BUCKETS, bucket, 0)

    return pl.pallas_call(body, name="bias_grad", out_shape=SDS((3, N_BUCKETS, 128), F32), in_specs=[VMEM_SPEC, VMEM_SPEC],
                          out_specs=VMEM_SPEC)(ds_sum, buckets)


def _sub_rows(d, r, n):
    return pl.ds(n * BLK * d + r, BLK) if d == 1 else pl.ds(n * BLK * d + r, BLK, stride=d)


def _head_spec(seq, g, part):
    return pl.BlockSpec((seq, HD), lambda b, hh: (b, part * (QW // HD) + 4 * g + hh))


def _attn_forward(g, qkv, bias, bsz, seq):
    d = DILATIONS[g]
    nblk = seq // d // BLK

    def body(q_ref, k_ref, v_ref, b_ref, o_ref, l_ref, qf, kf, vf):
        hs = pl.program_id(1)
        qf[...] = q_ref[...].astype(F32)
        kf[...] = k_ref[...].astype(F32)
        vf[...] = v_ref[...].astype(F32)
        for r in range(d):
            for n in range(nblk):
                rows = _sub_rows(d, r, n)
                qb = qf[rows, :].astype(BF16)
                s_c = _dot_nt(qb, kf[rows, :].astype(BF16)) * SCALE + b_ref[hs, :, BLK:]
                m = jnp.max(s_c, axis=1, keepdims=True)
                if n > 0:
                    prev = _sub_rows(d, r, n - 1)
                    s_p = _dot_nt(qb, kf[prev, :].astype(BF16)) * SCALE + b_ref[hs, :, :BLK]
                    m = jnp.maximum(m, jnp.max(s_p, axis=1, keepdims=True))
                p_c = jnp.exp(s_c - m)
                den = jnp.sum(p_c, axis=1, keepdims=True)
                acc = _dot(p_c.astype(BF16), vf[rows, :].astype(BF16))
                if n > 0:
                    p_p = jnp.exp(s_p - m)
                    den = den + jnp.sum(p_p, axis=1, keepdims=True)
                    acc = acc + _dot(p_p.astype(BF16), vf[prev, :].astype(BF16))
                o_ref[rows, :] = acc / den
                l_ref[rows, :] = jnp.broadcast_to(m + jnp.log(den), (BLK, HD))

    out_spec = pl.BlockSpec((seq, HD), lambda b, hh: (b, hh))
    return pl.pallas_call(
        body, name=f"attn_forward_{g}", out_shape=[SDS((bsz * seq, AW), F32)] * 2, grid=(bsz, 4),
        in_specs=[_head_spec(seq, g, part) for part in range(3)] + [pl.BlockSpec((4, BLK, 2 * BLK), lambda b, hh: (0, 0, 0))],
        out_specs=[out_spec, out_spec], scratch_shapes=[pltpu.VMEM((seq, HD), F32)] * 3,
        compiler_params=_cp(("parallel", "parallel"), VMEM_CAP // 2),
    )(qkv, qkv, qkv, bias)


def _attn_backward(g, qkv, do, dl, bias, prev_out, bsz, seq):
    d = DILATIONS[g]
    nblk = seq // d // BLK

    def body(q_ref, k_ref, v_ref, do_ref, dl_ref, b_ref, *rest):
        dq_ref, dk_ref, dv_ref, db_ref, qf, kf, vf, dqf, dkf, dvf = rest[-10:]
        hs = pl.program_id(1)

        @pl.when((pl.program_id(0) == 0) & (hs == 0))
        def _():
            db_ref[...] = jnp.zeros_like(db_ref)

        qf[...] = q_ref[...].astype(F32)
        kf[...] = k_ref[...].astype(F32)
        vf[...] = v_ref[...].astype(F32)
        dkf[...] = jnp.zeros_like(dkf)
        dvf[...] = jnp.zeros_like(dvf)
        for r in range(d):
            for n in range(nblk):
                rows = _sub_rows(d, r, n)
                qb = qf[rows, :].astype(BF16)
                dob = do_ref[rows, :].astype(BF16)
                both = dl_ref[rows, :]
                lse, delta = both[:, 0:1], both[:, 64:65]
                dq = jnp.zeros((BLK, HD), F32)
                parts = [(rows, slice(BLK, 2 * BLK))]
                if n > 0:
                    parts.append((_sub_rows(d, r, n - 1), slice(0, BLK)))
                for keys, band in parts:
                    kb, vb = kf[keys, :].astype(BF16), vf[keys, :].astype(BF16)
                    p = jnp.exp(_dot_nt(qb, kb) * SCALE + b_ref[hs, :, band] - lse)
                    ds = p * (_dot_nt(dob, vb) - delta)
                    dsb = ds.astype(BF16)
                    dvf[keys, :] += _dot_tn(p.astype(BF16), dob)
                    dkf[keys, :] += _dot_tn(dsb, qb) * SCALE
                    dq = dq + _dot(dsb, kb) * SCALE
                    db_ref[hs, :, band] += ds
                dqf[rows, :] = dq
        dq_ref[...] = dqf[...].astype(BF16)
        dk_ref[...] = dkf[...].astype(BF16)
        dv_ref[...] = dvf[...].astype(BF16)

    qkv_spec = _head_spec(seq, g, 0)
    out_spec = pl.BlockSpec((seq, HD), lambda b, hh: (b, hh))
    band_spec = pl.BlockSpec((4, BLK, 2 * BLK), lambda b, hh: (0, 0, 0))
    ins = [qkv, qkv, qkv, do, dl, bias]
    in_specs = [_head_spec(seq, g, part) for part in range(3)] + [out_spec, out_spec, band_spec]
    aliases = {}
    if prev_out is not None:
        ins += list(prev_out)
        in_specs += [ANY] * 3
        aliases = {6: 0, 7: 1, 8: 2}
    dq, dk, dv, db = pl.pallas_call(
        body, name=f"attn_backward_{g}", out_shape=[SDS((bsz * seq, QW), BF16)] * 3 + [SDS((4, BLK, 2 * BLK), F32)], grid=(bsz, 4),
        in_specs=in_specs, out_specs=[qkv_spec] * 3 + [band_spec], input_output_aliases=aliases,
        scratch_shapes=[pltpu.VMEM((seq, HD), F32)] * 6,
        compiler_params=_cp(("arbitrary", "arbitrary"), VMEM_CAP // 2),
    )(*ins)
    return (dq, dk, dv), db


def _mix_forward(gates, og, lg, x2, tgt, gate, w_ao, w_co, w_o, conv_w, conv_b, ln_g, ln_b, bsz, seq, tm=256):
    t = x2.shape[0]
    spt = seq // tm

    def body(g_ref, o1, o2, o3, l1, l2, l3, x_ref, t_ref, gate_ref, wao_ref, wco_ref, wo_ref, cw_ref, cb_ref, lng_ref, lnb_ref,
             ain_ref, sin_ref, mrg_ref, dy_ref, aout_ref, sout_ref, yc_ref, o_ref, lj_ref, dxr_ref, vec_ref, dgate_ref, zc_ref):
        b, i = pl.program_id(0), pl.program_id(1)

        @pl.when((b == 0) & (i == 0))
        def _():
            vec_ref[...] = jnp.zeros_like(vec_ref)

        @pl.when(i == 0)
        def _():
            zc_ref[...] = jnp.zeros_like(zc_ref)
            dgate_ref[...] = jnp.zeros_like(dgate_ref)

        g_attn, u, bg, cg, g_conv, m_attn, m_conv = (g_ref[:, lo:hi].astype(F32) for lo, hi in GATE_COLS)
        la, lb, lc = l1[...], l2[...], l3[...]
        mx = jnp.maximum(la, jnp.maximum(lb, lc))
        ea, eb, ec = jnp.exp(la - mx), jnp.exp(lb - mx), jnp.exp(lc - mx)
        den = ea + eb + ec
        o = (ea * o1[...] + eb * o2[...] + ec * o3[...]) / den
        o_ref[...] = o
        lj_ref[...] = mx + jnp.log(den)
        a_in = o * (g_attn * _sig(g_attn))
        ain_ref[...] = a_in.astype(BF16)
        a_out = _dot(a_in.astype(BF16), wao_ref[...])
        aout_ref[...] = a_out.astype(BF16)
        z = cg * u
        rows = lax.broadcasted_iota(jnp.int32, (tm, D), 0)
        c6, c7 = zc_ref[6:7, :], zc_ref[7:8, :]
        z1 = jnp.where(rows == 0, c7, pltpu.roll(z, 1, 0))
        z2 = jnp.where(rows == 0, c6, jnp.where(rows == 1, c7, pltpu.roll(z, 2, 0)))
        zc_ref[...] = z[tm - 8:tm, :]
        y_conv = (cw_ref[0:1, :] * z2 + cw_ref[1:2, :] * z1 + cw_ref[2:3, :] * z) + cb_ref[...]
        yc_ref[...] = y_conv.astype(BF16)
        s_in = bg * y_conv * (g_conv * _sig(g_conv))
        sin_ref[...] = s_in.astype(BF16)
        s_out = _dot(s_in.astype(BF16), wco_ref[...])
        sout_ref[...] = s_out.astype(BF16)
        merged = _sig(m_attn) * a_out + _sig(m_conv) * s_out
        mrg_ref[...] = merged.astype(BF16)
        y = _dot(merged.astype(BF16), wo_ref[...])
        gate1 = 1.0 + gate_ref[0]
        r = ALPHA * x_ref[...] + gate1 * y
        mu = jnp.mean(r, axis=1, keepdims=True)
        rc = r - mu
        rstd = lax.rsqrt(jnp.mean(rc * rc, axis=1, keepdims=True) + LN_EPS)
        xhat = rc * rstd
        diff = (xhat * lng_ref[...] + lnb_ref[...]) - t_ref[...]
        dout = diff * (1.0 / D)
        vec_ref[0:1, :] += jnp.sum(dout * xhat, axis=0, keepdims=True)
        vec_ref[1:2, :] += jnp.sum(dout, axis=0, keepdims=True)
        vec_ref[2:3, :] += jnp.sum(diff * diff, axis=0, keepdims=True)
        dxh = dout * lng_ref[...]
        dr = rstd * (dxh - jnp.mean(dxh, axis=1, keepdims=True) - xhat * jnp.mean(dxh * xhat, axis=1, keepdims=True))
        dxr_ref[...] = ALPHA * dr
        dy_ref[...] = (dr * gate1).astype(BF16)
        dgate_ref[0] += jnp.sum(dr * y, axis=0, keepdims=True)

    tok = lambda w: pl.BlockSpec((tm, w), lambda b, i: (b * spt + i, 0))
    const = lambda s: pl.BlockSpec(s, lambda b, i: (0,) * len(s))
    per_seq = pl.BlockSpec((1, 1, D), lambda b, i: (b, 0, 0))
    outs = pl.pallas_call(
        body, name="mix_forward", grid=(bsz, spt),
        out_shape=[SDS((t, AW), BF16), SDS((t, D), BF16), SDS((t, D), BF16), SDS((t, D), BF16), SDS((t, D), BF16), SDS((t, D), BF16),
                   SDS((t, D), BF16), SDS((t, AW), F32), SDS((t, AW), F32), SDS((t, D), F32), SDS((8, D), F32), SDS((bsz, 1, D), F32)],
        in_specs=[tok(NGATE)] + [tok(AW)] * 6 + [tok(D), tok(D), per_seq, const((AW, D)), const((D, D)), const((D, D)),
                                                 const((3, D)), const((1, D)), const((1, D)), const((1, D))],
        out_specs=[tok(AW), tok(D), tok(D), tok(D), tok(D), tok(D), tok(D), tok(AW), tok(AW), tok(D), const((8, D)), per_seq],
        scratch_shapes=[pltpu.VMEM((8, D), F32)],
        compiler_params=_cp(("arbitrary", "arbitrary"), VMEM_CAP),
    )(gates, *og, *lg, x2, tgt, gate, w_ao, w_co, w_o, conv_w, conv_b, ln_g, ln_b)
    return outs


def _mix_backward(gates, dy, a_out, s_out, y_conv, o, lj, w_ao, w_co, w_o, conv_w, bsz, seq, tm=256):
    t = dy.shape[0]
    spt = seq // tm

    def body(g_ref, dy_ref, aout_ref, sout_ref, yc_ref, o_ref, lj_ref, wao_ref, wco_ref, wo_ref, cw_ref,
             dg_ref, do_ref, dl_ref, daout_ref, dsout_ref, vec_ref, car_ref):
        b, i = pl.program_id(0), pl.program_id(1)

        @pl.when((b == 0) & (i == 0))
        def _():
            vec_ref[...] = jnp.zeros_like(vec_ref)

        @pl.when(i == 0)
        def _():
            car_ref[...] = jnp.zeros_like(car_ref)

        g_attn, u, bg, cg, g_conv, m_attn, m_conv = (g_ref[:, lo:hi].astype(F32) for lo, hi in GATE_COLS)
        dmerged = _dot_nt(dy_ref[...], wo_ref[...])
        sa, sc = _sig(m_attn), _sig(m_conv)
        da_out = (dmerged * sa).astype(BF16)
        ds_out = (dmerged * sc).astype(BF16)
        daout_ref[...] = da_out
        dsout_ref[...] = ds_out
        dg_ref[:, 4608:5632] = (dmerged * aout_ref[...].astype(F32) * (sa * (1.0 - sa))).astype(BF16)
        dg_ref[:, 5632:6656] = (dmerged * sout_ref[...].astype(F32) * (sc * (1.0 - sc))).astype(BF16)
        da_in = _dot_nt(da_out, wao_ref[...])
        ds_in = _dot_nt(ds_out, wco_ref[...])
        sga = _sig(g_attn)
        o = o_ref[...]
        do = da_in * (g_attn * sga)
        do_ref[...] = do
        dg_ref[:, 0:512] = (da_in * o * (sga * (1.0 + g_attn * (1.0 - sga)))).astype(BF16)
        prod = do * o
        lane = lax.broadcasted_iota(jnp.int32, (tm, HD), 1)
        for j in range(4):
            cs = slice(j * HD, (j + 1) * HD)
            delta = jnp.sum(prod[:, cs], axis=1, keepdims=True)
            dl_ref[:, cs] = jnp.where(lane < 64, lj_ref[:, cs], delta)
        sgc = _sig(g_conv)
        silu_c = g_conv * sgc
        yc = yc_ref[...].astype(F32)
        dg_ref[:, 1536:2560] = (ds_in * yc * silu_c).astype(BF16)
        dg_ref[:, 3584:4608] = (ds_in * bg * yc * (sgc * (1.0 + g_conv * (1.0 - sgc)))).astype(BF16)
        dyc = ds_in * bg * silu_c
        rows = lax.broadcasted_iota(jnp.int32, (tm, D), 0)
        c0, c1 = car_ref[0:1, :], car_ref[1:2, :]
        n1 = jnp.where(rows == tm - 1, c0, pltpu.roll(dyc, tm - 1, 0))
        n2 = jnp.where(rows == tm - 2, c0, jnp.where(rows == tm - 1, c1, pltpu.roll(dyc, tm - 2, 0)))
        car_ref[...] = dyc[0:8, :]
        dz = cw_ref[2:3, :] * dyc + cw_ref[1:2, :] * n1 + cw_ref[0:1, :] * n2
        z = cg * u
        dg_ref[:, 512:1536] = (dz * cg).astype(BF16)
        dg_ref[:, 2560:3584] = (dz * u).astype(BF16)
        vec_ref[0:1, :] += jnp.sum(n2 * z, axis=0, keepdims=True)
        vec_ref[1:2, :] += jnp.sum(n1 * z, axis=0, keepdims=True)
        vec_ref[2:3, :] += jnp.sum(dyc * z, axis=0, keepdims=True)
        vec_ref[3:4, :] += jnp.sum(dyc, axis=0, keepdims=True)

    tok = lambda w: pl.BlockSpec((tm, w), lambda b, i: (b * spt + (spt - 1 - i), 0))
    const = lambda s: pl.BlockSpec(s, lambda b, i: (0,) * len(s))
    return pl.pallas_call(
        body, name="mix_backward", grid=(bsz, spt),
        out_shape=[SDS((t, NGATE), BF16), SDS((t, AW), F32), SDS((t, AW), F32), SDS((t, D), BF16), SDS((t, D), BF16), SDS((8, D), F32)],
        in_specs=[tok(NGATE), tok(D), tok(D), tok(D), tok(D), tok(AW), tok(AW), const((AW, D)), const((D, D)), const((D, D)), const((3, D))],
        out_specs=[tok(NGATE), tok(AW), tok(AW), tok(D), tok(D), const((8, D))],
        scratch_shapes=[pltpu.VMEM((8, D), F32)],
        compiler_params=_cp(("arbitrary", "arbitrary"), VMEM_CAP),
    )(gates, dy, a_out, s_out, y_conv, o, lj, w_ao, w_co, w_o, conv_w)


def _out_weight_grads(a_in, da_out, s_in, ds_out, merged, dy, tk=512):
    t = dy.shape[0]

    def body(ain_ref, da_ref, sin_ref, ds_ref, m_ref, dy_ref, gao_ref, gco_ref, go_ref):
        @pl.when(pl.program_id(0) == 0)
        def _():
            gao_ref[...] = jnp.zeros_like(gao_ref)
            gco_ref[...] = jnp.zeros_like(gco_ref)
            go_ref[...] = jnp.zeros_like(go_ref)

        gao_ref[...] += _dot_tn(ain_ref[...], da_ref[...])
        gco_ref[...] += _dot_tn(sin_ref[...], ds_ref[...])
        go_ref[...] += _dot_tn(m_ref[...], dy_ref[...])

    tok = lambda w: pl.BlockSpec((tk, w), lambda i: (i, 0))
    const = lambda s: pl.BlockSpec(s, lambda i: (0, 0))
    return pl.pallas_call(
        body, name="out_weight_grads", grid=(t // tk,), out_shape=[SDS((AW, D), F32), SDS((D, D), F32), SDS((D, D), F32)],
        in_specs=[tok(AW), tok(D), tok(D), tok(D), tok(D), tok(D)], out_specs=[const((AW, D)), const((D, D)), const((D, D))],
        compiler_params=_cp(("arbitrary",), VMEM_CAP),
    )(a_in, da_out, s_in, ds_out, merged, dy)


def _input_grad(dq, dk, dv, dgates, w, x2, dxr, sc1p, seq, sums, tm=512):
    t = x2.shape[0]
    nt = t // tm
    spt = seq // tm
    bsz = t // seq
    n = len(sums)
    gblk = NGATE // 4
    nsteps = 3 + 4

    def body(dq_ref, dk_ref, dv_ref, dg_ref, wq_ref, wg_ref, x_ref, dxr_ref, sc_ref, *rest):
        src, (dx_ref, dsh_ref, dsc_ref), land = rest[:n], rest[n:n + 3], rest[n + 3:2 * n + 3]
        acc_ref, send_sems, recv_sems = rest[2 * n + 3:]
        j, i = pl.program_id(0), pl.program_id(1)
        px, py, pc = _place()
        chips = [(1 - px, py), (px, 1 - py), (1 - px, 1 - py)]
        copies = [pltpu.make_async_remote_copy(src_ref=src[a].at[2 * cx + cy], dst_ref=land[a].at[r], send_sem=send_sems.at[3 * a + r],
                                               recv_sem=recv_sems.at[3 * a + r], device_id=(cx, cy, pc), device_id_type=MESH)
                  for a in range(n) for r, (cx, cy) in enumerate(chips)]
        rows = pl.ds(pl.multiple_of(i * tm, tm), tm)

        @pl.when((i == 0) & (j == 0))
        def _():
            for cp in copies:
                cp.start()

        for k, ref in enumerate((dq_ref, dk_ref, dv_ref)):
            @pl.when(j == k)
            def _(k=k, ref=ref):
                part = _dot_nt(ref[...], wq_ref[...])
                if k == 0:
                    acc_ref[rows, :] = part
                else:
                    acc_ref[rows, :] += part

        @pl.when((j >= 3) & (j < nsteps - 1))
        def _():
            acc_ref[rows, :] += _dot_nt(dg_ref[...], wg_ref[...])

        @pl.when(j == nsteps - 1)
        def _():
            dh = acc_ref[rows, :] + _dot_nt(dg_ref[...], wg_ref[...])
            dx_ref[...] = dh * sc_ref[0] + dxr_ref[...]

            @pl.when(i % spt == 0)
            def _():
                dsh_ref[...] = jnp.zeros_like(dsh_ref)
                dsc_ref[...] = jnp.zeros_like(dsc_ref)

            dsh_ref[0] += jnp.sum(dh, axis=0, keepdims=True)
            dsc_ref[0] += jnp.sum(dh * x_ref[...], axis=0, keepdims=True)

        @pl.when((i == nt - 1) & (j == nsteps - 1))
        def _():
            for cp in copies:
                cp.wait()

    def held(k):
        return lambda j, i: (jnp.where(j == k, i, jnp.where(j < k, 0, nt - 1)), 0)

    last = lambda j, i: (jnp.where(j == nsteps - 1, i, 0), 0)
    outs = pl.pallas_call(
        body, name="input_grad", grid=(nsteps, nt),
        out_shape=[SDS((t, D), F32), SDS((bsz, 1, D), F32), SDS((bsz, 1, D), F32)] + [SDS((3,) + s.shape[1:], BF16) for s in sums],
        in_specs=[pl.BlockSpec((tm, QW), held(0)), pl.BlockSpec((tm, QW), held(1)), pl.BlockSpec((tm, QW), held(2)),
                  pl.BlockSpec((tm, gblk), lambda j, i: (jnp.where(j >= 3, i, 0), jnp.clip(j - 3, 0, 3))),
                  pl.BlockSpec((D, QW), lambda j, i: (0, jnp.minimum(j, 2))),
                  pl.BlockSpec((pl.Element(D), pl.Element(gblk)), lambda j, i: (0, pl.multiple_of(3 * QW + gblk * jnp.clip(j - 3, 0, 3), 128))),
                  pl.BlockSpec((tm, D), last), pl.BlockSpec((tm, D), last),
                  pl.BlockSpec((1, 1, D), lambda j, i: (jnp.where(j == nsteps - 1, i // spt, 0), 0, 0))] + [ANY] * n,
        out_specs=[pl.BlockSpec((tm, D), last),
                   pl.BlockSpec((1, 1, D), lambda j, i: (jnp.where(j == nsteps - 1, i // spt, 0), 0, 0)),
                   pl.BlockSpec((1, 1, D), lambda j, i: (jnp.where(j == nsteps - 1, i // spt, 0), 0, 0))] + [ANY] * n,
        scratch_shapes=[pltpu.VMEM((t, D), F32), pltpu.SemaphoreType.DMA((3 * NCHIP,)), pltpu.SemaphoreType.DMA((3 * NCHIP,))],
        compiler_params=_cp(("arbitrary", "arbitrary"), VMEM_CAP, side=True),
    )(dq, dk, dv, dgates, w, w, x2, dxr, sc1p, *sums)
    return outs[0], outs[1], outs[2], outs[3:]


def _in_weight_grad(ht, src, col0, prev, name):
    t = ht.shape[1]
    ncols = src.shape[1] // TN

    def body(ht_ref, s_ref, *rest):
        rest[-1][...] = _dot(ht_ref[...], s_ref[...].astype(BF16))

    ins = [ht, src]
    in_specs = [pl.BlockSpec((D, t), lambda j: (0, 0)), pl.BlockSpec((t, TN), lambda j: (0, j))]
    aliases = {}
    if prev is not None:
        ins.append(prev)
        in_specs.append(ANY)
        aliases = {2: 0}
    return pl.pallas_call(
        body, name=name, grid=(ncols,), out_shape=SDS((D, NCOL), F32), in_specs=in_specs,
        out_specs=pl.BlockSpec((D, TN), lambda j: (0, col0 + j)), input_output_aliases=aliases,
        compiler_params=_cp(("arbitrary",), VMEM_CAP),
    )(*ins)


def _sum_partials(gathered):
    def body(g_ref, o_ref):
        acc = g_ref[0]
        for k in range(1, 8):
            acc = acc + g_ref[k]
        o_ref[...] = acc

    return pl.pallas_call(body, name="sum_partials", out_shape=SDS(gathered.shape[1:], F32), in_specs=[VMEM_SPEC], out_specs=VMEM_SPEC)(gathered)


def _adamw(w, g, m, v, name, tr=256):
    r, cdim = w.shape
    tr = tr if cdim <= D else tr // 2
    tr = tr if (r % tr == 0 and r > tr) else r

    def body(w_ref, g_ref, m_ref, v_ref, d_ref, nm_ref, nv_ref):
        gv = g_ref[...]
        nm = B1 * m_ref[...] + (1.0 - B1) * gv
        nv = B2 * v_ref[...] + (1.0 - B2) * (gv * gv)
        m_hat = nm / (1.0 - B1 ** STEP)
        v_hat = nv / (1.0 - B2 ** STEP)
        d_ref[...] = -LR * (m_hat / (jnp.sqrt(v_hat) + EPS) + WD * w_ref[...])
        nm_ref[...] = nm
        nv_ref[...] = nv

    spec = pl.BlockSpec((tr, cdim), lambda i: (i, 0))
    return pl.pallas_call(
        body, name=name, grid=(r // tr,), out_shape=[SDS((r, cdim), F32)] * 3, in_specs=[spec] * 4, out_specs=[spec] * 3,
        compiler_params=_cp(("parallel",), VMEM_CAP // 2),
    )(w, g, m, v)


def _t5_bucket(dist):
    n = jnp.maximum(dist, 1).astype(F32)
    large = MAX_EXACT + (jnp.log(n / MAX_EXACT) / math.log(MAX_DISTANCE / MAX_EXACT) * (N_BUCKETS - MAX_EXACT)).astype(jnp.int32)
    large = jnp.minimum(large, N_BUCKETS - 1)
    return jnp.where(dist < MAX_EXACT, dist, large)


def _band_buckets():
    a = jnp.arange(BLK)[:, None]
    b = jnp.arange(2 * BLK)[None, :]
    steps = jnp.maximum(a + BLK - b, 0)
    return jnp.stack([_t5_bucket(steps * d) for d in DILATIONS]).astype(jnp.int32)


def _pad_rows(a, rows=8):
    return jnp.pad(a, ((0, rows - a.shape[0]), (0, 0)))


def kernel(x, c, w_ada, b_ada, w_in, conv_w, conv_b, rel_bias, w_attn_out, w_conv_out, w_o, ln_g, ln_b, loss_target, m_w_ada, m_b_ada, m_w_in, m_conv_w, m_conv_b, m_rel_bias, m_w_attn_out, m_w_conv_out, m_w_o, m_ln_g, m_ln_b, v_w_ada, v_b_ada, v_w_in, v_conv_w, v_conv_b, v_rel_bias, v_w_attn_out, v_w_conv_out, v_w_o, v_ln_g, v_ln_b):
    bsz, seq, _ = x.shape
    t = bsz * seq
    mx, my, mc = _place()
    chip = 2 * mx + my
    dev = 4 * mx + 2 * my + mc
    x2 = x.reshape(t, D)
    tgt = loss_target.reshape(t, D)

    mine = [_to_bf16_window(a, w[0], f"to_bf16_{a}") for a, w in enumerate((w_in, w_attn_out, w_conv_out, w_o))]

    n_ada = w_ada.shape[2]
    n_cw = conv_w.shape[2]
    c_and_cw = jnp.concatenate([_pad_rows(c), jnp.pad(conv_w[0], ((0, 5), (0, D - n_cw)))], axis=0)
    firsts = _all_gather8(c_and_cw, "gather_c_conv_w")
    c_all = firsts[:, 0:bsz, :].reshape(8 * bsz, D)
    conv_w_f = firsts[0::2, 8:11, 0:n_cw].transpose(1, 0, 2).reshape(3, D)
    b_cols = lax.dynamic_slice(b_ada, (0, chip * n_ada), (1, n_ada))
    mod_part = _ada_forward(c_all, w_ada[0], b_cols)
    mod_parts = _all_gather8(mod_part, "gather_mod")
    mod_all = mod_parts[0::2].transpose(1, 0, 2).reshape(8 * bsz, 3 * D)
    mod = lax.dynamic_slice(mod_all, (dev * bsz, 0), (bsz, 3 * D))
    shift = mod[:, 0:D].reshape(bsz, 1, D)
    sc1p = 1.0 + mod[:, D:2 * D].reshape(bsz, 1, D)
    gate = mod[:, 2 * D:].reshape(bsz, 1, D)

    h, ht = _modulate(x2, sc1p, shift, seq)
    tab = lax.dynamic_index_in_dim(jnp.asarray(_tile_tables()), chip, 0, keepdims=False)
    qkv, gates, (w_in_f, w_ao_f, w_co_f, w_o_f) = _project_gather(h, mine, tab)
    buckets = _band_buckets()
    bias = _bias_tables(rel_bias, buckets)
    og, lg = [], []
    for g in range(3):
        o_g, l_g = _attn_forward(g, qkv, bias[g], bsz, seq)
        og.append(o_g)
        lg.append(l_g)
    (a_in, s_in, merged, dy, a_out, s_out, y_conv, o, lj, dxr, vec_f, dgate) = _mix_forward(
        gates, og, lg, x2, tgt, gate, w_ao_f, w_co_f, w_o_f, conv_w_f, conv_b, ln_g, ln_b, bsz, seq)

    dgates, do, dl, da_out, ds_out, vec_b = _mix_backward(gates, dy, a_out, s_out, y_conv, o, lj, w_ao_f, w_co_f, w_o_f, conv_w_f, bsz, seq)
    g_ao, g_co, g_o = _out_weight_grads(a_in, da_out, s_in, ds_out, merged, dy)
    dqkv, dbs = None, []
    for g in range(3):
        dqkv, db = _attn_backward(g, qkv, do, dl, bias[g], dqkv, bsz, seq)
        dbs.append(db)
    dq, dk, dv = dqkv
    drb = _bias_grad(jnp.stack(dbs), buckets)
    drb = drb[:, :, 0:4].transpose(1, 0, 2).reshape(N_BUCKETS, 12)
    g_in = None
    for n, src in enumerate((dq, dk, dv, dgates)):
        g_in = _in_weight_grad(ht, src, n * NQT, g_in, f"in_weight_grad_{n}")

    grads = [g_in, g_ao, g_co, g_o]
    got = _swap_halves(grads)
    sums = [_chip_sum(a, grads[a], got[a], f"chip_sum_{a}") for a in range(4)]
    grad_x, dshift, dscale, landed = _input_grad(dq, dk, dv, dgates, w_in_f, x2, dxr, sc1p, seq, [s[1] for s in sums])
    halves = [_reduce_mine(a, sums[a][0], landed[a], f"reduce_mine_{a}") for a in range(4)]
    gw_in, gw_ao, gw_co, gw_o = _join_halves(halves)

    dmod = jnp.concatenate([dshift, dscale, dgate], axis=2).reshape(bsz * 3, D)
    drb_row = jnp.pad(drb.reshape(1, N_BUCKETS * 12), ((0, 0), (0, D - N_BUCKETS * 12)))
    packed = jnp.concatenate([vec_f, vec_b, _pad_rows(dmod), _pad_rows(drb_row)], axis=0)
    gathered = _all_gather8(packed, "gather_small")
    small = _sum_partials(gathered)
    g_ln_g, g_ln_b, loss_lanes = small[0:1], small[1:2], small[2:3]
    g_conv_w_full, g_conv_b = small[8:11], small[11:12]
    g_rel_bias = small[24, 0:N_BUCKETS * 12].reshape(N_BUCKETS, 12)
    loss = 0.5 / D * jnp.sum(loss_lanes)
    dmod_all = gathered[:, 16:16 + 3 * bsz, :].reshape(8 * bsz, 3 * D)
    dmod_cols = lax.dynamic_slice(dmod_all, (0, chip * n_ada), (8 * bsz, n_ada))
    gw_ada, gb_ada = _ada_backward(c_all, dmod_cols, dmod_all)
    g_conv_w = lax.dynamic_slice(g_conv_w_full, (0, chip * n_cw), (3, n_cw))

    names = ["w_ada", "b_ada", "w_in", "conv_w", "conv_b", "rel_bias", "w_attn_out", "w_conv_out", "w_o", "ln_g", "ln_b"]
    two_d = lambda a: a.reshape(a.shape[-2:]) if a.ndim == 3 else a
    weights = dict(zip(names, map(two_d, (w_ada, b_ada, w_in, conv_w, conv_b, rel_bias, w_attn_out, w_conv_out, w_o, ln_g, ln_b))))
    ms = dict(zip(names, map(two_d, (m_w_ada, m_b_ada, m_w_in, m_conv_w, m_conv_b, m_rel_bias, m_w_attn_out, m_w_conv_out, m_w_o, m_ln_g, m_ln_b))))
    vs = dict(zip(names, map(two_d, (v_w_ada, v_b_ada, v_w_in, v_conv_w, v_conv_b, v_rel_bias, v_w_attn_out, v_w_conv_out, v_w_o, v_ln_g, v_ln_b))))
    grads = dict(zip(names, (gw_ada, gb_ada, gw_in, g_conv_w, g_conv_b, g_rel_bias, gw_ao, gw_co, gw_o, g_ln_g, g_ln_b)))
    shapes = dict(zip(names, (w_ada, b_ada, w_in, conv_w, conv_b, rel_bias, w_attn_out, w_conv_out, w_o, ln_g, ln_b)))
    deltas, new_m, new_v = {}, {}, {}
    for n in names:
        deltas[n], new_m[n], new_v[n] = _adamw(weights[n], grads[n], ms[n], vs[n], f"adamw_{n}")
    shaped = lambda d: [d[n].reshape(shapes[n].shape) for n in names]
    return (loss, grad_x.reshape(bsz, seq, D), *shaped(grads), *shaped(deltas), *shaped(new_m), *shaped(new_v))
```

```python
import math

import numpy as np
import jax
import jax.numpy as jnp
from jax import lax
from jax.experimental import pallas as pl
from jax.experimental.pallas import tpu as pltpu

F32 = jnp.float32
BF16 = jnp.bfloat16
SDS = jax.ShapeDtypeStruct
MESH = pl.DeviceIdType.MESH
HBM_OUT = pltpu.HBM
ANY = pl.BlockSpec(memory_space=pl.ANY)
VMEM_SPEC = pl.BlockSpec(memory_space=pltpu.VMEM)

D = 1024
HD = 128
BLK = 128
QW = 1536
AW = 512
NGATE = 6656
GATE_COLS = ((0, 512), (512, 1536), (1536, 2560), (2560, 3584), (3584, 4608), (4608, 5632), (5632, 6656))
NCOL = 3 * QW + NGATE
TN = 512
NQT = QW // TN
NPT = NCOL // TN
DILATIONS = (1, 4, 16)
N_BUCKETS, MAX_EXACT, MAX_DISTANCE = 32, 16, 2048
ALPHA = 2.0 ** 0.25
LN_EPS = 1e-5
NEG = -1e30
SCALE = HD ** -0.5
LR, B1, B2, EPS, WD, STEP = 0.001, 0.9, 0.999, 1e-08, 0.01, 10
NCHIP = 4
VMEM_CAP = 60 * 2 ** 20


def _cp(sem=None, vmem=None, side=False):
    return pltpu.CompilerParams(dimension_semantics=sem, vmem_limit_bytes=vmem, has_side_effects=side)


def _dot(a, b):
    return jnp.dot(a, b, preferred_element_type=F32)


def _dot_nt(a, b):
    return lax.dot_general(a, b, (((1,), (1,)), ((), ())), preferred_element_type=F32)


def _dot_tn(a, b):
    return lax.dot_general(a, b, (((0,), (0,)), ((), ())), preferred_element_type=F32)


def _sig(x):
    return 1.0 / (1.0 + jnp.exp(-x))


def _in_hbm(a):
    return pltpu.with_memory_space_constraint(a, pltpu.HBM)


def _place():
    x, y, c = lax.axis_index("x"), lax.axis_index("y"), lax.axis_index("c")
    return x, y, c


def _all_gather8(v, name):
    r, cdim = v.shape

    def body(v_ref, out_ref, send_sems, recv_sems, local_sem):
        x, y, c = _place()
        me = 4 * x + 2 * y + c
        peers = [(x, y, 1 - c), (1 - x, y, c), (x, 1 - y, c), (1 - x, 1 - y, c),
                 (1 - x, y, 1 - c), (x, 1 - y, 1 - c), (1 - x, 1 - y, 1 - c)]
        mine = pltpu.make_async_copy(v_ref, out_ref.at[me], local_sem)
        mine.start()

        def copy(k, block, to):
            return pltpu.make_async_remote_copy(src_ref=v_ref, dst_ref=out_ref.at[block], send_sem=send_sems.at[k],
                                                recv_sem=recv_sems.at[k], device_id=to, device_id_type=MESH)

        sends = [copy(k, me, p) for k, p in enumerate(peers)]
        for cp in sends:
            cp.start()
        for k, (px, py, pc) in enumerate(peers):
            copy(k, 4 * px + 2 * py + pc, (px, py, pc)).wait_recv()
        for cp in sends:
            cp.wait_send()
        mine.wait()

    return pl.pallas_call(
        body, name=name, out_shape=SDS((8, r, cdim), v.dtype), in_specs=[VMEM_SPEC], out_specs=VMEM_SPEC,
        scratch_shapes=[pltpu.SemaphoreType.DMA((7,)), pltpu.SemaphoreType.DMA((7,)), pltpu.SemaphoreType.DMA(())],
        compiler_params=_cp(side=True),
    )(v)


W_CUTS = (("col", D, NCOL // NCHIP), ("col", AW, D // NCHIP), ("row", D // NCHIP, D), ("row", D // NCHIP, D))
W_FULL = ((D, NCOL), (AW, D), (D, D), (D, D))


def _shard_window(ref, cut, k, half):
    kind, nr, nc = cut
    hr = nr // 2
    if kind == "col":
        rows = pl.ds(0, nr) if half is None else pl.ds(pl.multiple_of(half * hr, 16), hr)
        return ref.at[rows, pl.ds(pl.multiple_of(k * nc, 128), nc)]
    if half is None:
        return ref.at[pl.ds(pl.multiple_of(k * nr, 16), nr), :]
    return ref.at[pl.ds(pl.multiple_of(k * nr + half * hr, 16), hr), :]


def _half_rows(ref, cut, half):
    hr = cut[1] // 2
    return ref.at[pl.ds(pl.multiple_of(half * hr, 16), hr), :]


def _to_bf16_window(a, w, name):
    kind, nr, nc = W_CUTS[a]
    x, y, _ = _place()
    chip = jnp.reshape(2 * x + y, (1,)).astype(jnp.int32)
    tr = min(nr, 256)

    def body(c_ref, w_ref, o_ref):
        o_ref[...] = w_ref[...].astype(BF16)

    out_map = (lambda i, cr: (i, cr[0])) if kind == "col" else (lambda i, cr: (cr[0] * (nr // tr) + i, 0))
    return pl.pallas_call(
        body, name=name, out_shape=SDS(W_FULL[a], BF16),
        grid_spec=pltpu.PrefetchScalarGridSpec(num_scalar_prefetch=1, grid=(nr // tr,),
                                               in_specs=[pl.BlockSpec((tr, nc), lambda i, cr: (i, 0))], out_specs=pl.BlockSpec((tr, nc), out_map)),
        compiler_params=_cp(("arbitrary",)),
    )(chip, w)


def _swap_halves(grads):
    n = len(grads)
    shapes = []
    for a in range(n):
        kind, nr, nc = W_CUTS[a]
        shapes.append((W_FULL[a][0] // 2, W_FULL[a][1]) if kind == "col" else (NCHIP, nr // 2, nc))

    def pieces(a, ref, land, half):
        kind, nr, nc = W_CUTS[a]
        if kind == "col":
            hr = nr // 2
            return [(ref.at[pl.ds(pl.multiple_of(half * hr, 16), hr), :], land)]
        return [(_shard_window(ref, W_CUTS[a], k, half), land.at[k]) for k in range(NCHIP)]

    def body(*refs):
        src, land = refs[:n], refs[n:2 * n]
        send_sems, recv_sems = refs[2 * n:]
        x, y, c = _place()
        sibling = (x, y, 1 - c)
        sends = []
        k = 0
        for a in range(n):
            for s, d in pieces(a, src[a], land[a], 1 - c):
                cp = pltpu.make_async_remote_copy(src_ref=s, dst_ref=d, send_sem=send_sems.at[k], recv_sem=recv_sems.at[k],
                                                  device_id=sibling, device_id_type=MESH)
                cp.start()
                sends.append(cp)
                k += 1
        for cp in sends:
            cp.wait()

    n_sems = sum(1 if W_CUTS[a][0] == "col" else NCHIP for a in range(n))
    return pl.pallas_call(
        body, name="swap_grad_halves", out_shape=[SDS(s, F32) for s in shapes], in_specs=[ANY] * n, out_specs=[ANY] * n,
        scratch_shapes=[pltpu.SemaphoreType.DMA((n_sems,)), pltpu.SemaphoreType.DMA((n_sems,))],
        compiler_params=_cp(side=True),
    )(*grads)


def _chip_sum(a, grad, got, name):
    kind, nr, nc = W_CUTS[a]
    hr = nr // 2
    c = lax.axis_index("c")
    cidx = jnp.reshape(c, (1,)).astype(jnp.int32)

    def body(c_ref, g_ref, r_ref, f_ref, b_ref):
        s = g_ref[...] + r_ref[...]
        f_ref[...] = s.reshape(f_ref.shape)
        b_ref[...] = s.astype(BF16).reshape(b_ref.shape)

    if kind == "col":
        in_specs = [pl.BlockSpec((hr, nc), lambda k, cr: (cr[0], k)), pl.BlockSpec((hr, nc), lambda k, cr: (0, k))]
    else:
        grad = grad.reshape(NCHIP, 2, hr, nc)
        in_specs = [pl.BlockSpec((1, 1, hr, nc), lambda k, cr: (k, cr[0], 0, 0)), pl.BlockSpec((1, hr, nc), lambda k, cr: (k, 0, 0))]
    out_specs = [pl.BlockSpec((1, hr, nc), lambda k, cr: (k, 0, 0))] * 2
    return pl.pallas_call(
        body, name=name, out_shape=[SDS((NCHIP, hr, nc), F32), SDS((NCHIP, hr, nc), BF16)],
        grid_spec=pltpu.PrefetchScalarGridSpec(num_scalar_prefetch=1, grid=(NCHIP,), in_specs=in_specs, out_specs=out_specs),
        compiler_params=_cp(("arbitrary",), VMEM_CAP),
    )(cidx, grad, got)


def _reduce_mine(a, mine_f32, got, name):
    kind, nr, nc = W_CUTS[a]
    hr = nr // 2
    x, y, c = _place()
    where = jnp.stack([2 * x + y, c]).astype(jnp.int32)
    tr = min(hr, 256)

    def body(w_ref, m_ref, g_ref, o_ref):
        o_ref[...] = ((m_ref[0] + g_ref[0].astype(F32)) + g_ref[1].astype(F32)) + g_ref[2].astype(F32)

    return pl.pallas_call(
        body, name=name, out_shape=SDS((nr, nc), F32),
        grid_spec=pltpu.PrefetchScalarGridSpec(
            num_scalar_prefetch=1, grid=(hr // tr,),
            in_specs=[pl.BlockSpec((1, tr, nc), lambda i, wr: (wr[0], i, 0)), pl.BlockSpec((3, tr, nc), lambda i, wr: (0, i, 0))],
            out_specs=pl.BlockSpec((tr, nc), lambda i, wr: (wr[1] * (hr // tr) + i, 0))),
        compiler_params=_cp(("arbitrary",), VMEM_CAP),
    )(where, mine_f32, got)


def _join_halves(fulls):
    n = len(fulls)

    def body(*refs):
        full = refs[n:2 * n]
        send_sems, recv_sems = refs[2 * n:]
        x, y, c = _place()
        sibling = (x, y, 1 - c)

        def swap(a, half):
            rows = _half_rows(full[a], W_CUTS[a], half)
            return pltpu.make_async_remote_copy(src_ref=rows, dst_ref=rows, send_sem=send_sems.at[a], recv_sem=recv_sems.at[a],
                                                device_id=sibling, device_id_type=MESH)

        sends = [swap(a, c) for a in range(n)]
        for cp in sends:
            cp.start()
        for a, cp in enumerate(sends):
            cp.wait_send()
            swap(a, 1 - c).wait_recv()

    return pl.pallas_call(
        body, name="join_grad_halves", out_shape=[SDS((W_CUTS[a][1], W_CUTS[a][2]), F32) for a in range(n)],
        in_specs=[ANY] * n, out_specs=[ANY] * n,
        scratch_shapes=[pltpu.SemaphoreType.DMA((n,)), pltpu.SemaphoreType.DMA((n,))],
        input_output_aliases={a: a for a in range(n)}, compiler_params=_cp(side=True),
    )(*fulls)


def _ada_forward(c_all, w_ada, b_cols):
    nb, nc = c_all.shape[0], w_ada.shape[1]

    def body(c_ref, w_ref, b_ref, o_ref):
        cv = c_ref[...]
        sc = (cv * _sig(cv)).astype(BF16)
        o_ref[...] = _dot(sc, w_ref[...].astype(BF16)) + b_ref[...]

    return pl.pallas_call(body, name="ada_forward", out_shape=SDS((nb, nc), F32), compiler_params=_cp(vmem=VMEM_CAP // 2))(c_all, w_ada, b_cols)


def _ada_backward(c_all, dmod_cols, dmod_all):
    nb, nc = dmod_cols.shape

    def body(c_ref, d_ref, a_ref, gw_ref, gb_ref):
        cv = c_ref[...]
        sc = (cv * _sig(cv)).astype(BF16)
        gw_ref[...] = _dot_tn(sc, d_ref[...].astype(BF16))
        gb_ref[...] = jnp.sum(a_ref[...], axis=0, keepdims=True)

    return pl.pallas_call(body, name="ada_backward", out_shape=[SDS((D, nc), F32), SDS((1, dmod_all.shape[1]), F32)],
                          compiler_params=_cp(vmem=VMEM_CAP // 2))(c_all, dmod_cols, dmod_all)


def _modulate(x2, sc1p, shift, seq, tm=256):
    t = x2.shape[0]
    spt = seq // tm

    def body(x_ref, sc_ref, sh_ref, h_ref, ht_ref):
        h = x_ref[...] * sc_ref[0] + sh_ref[0]
        h_ref[...] = h.astype(BF16)
        ht_ref[...] = h.T.astype(BF16)

    per_seq = pl.BlockSpec((1, 1, D), lambda i: (i // spt, 0, 0))
    return pl.pallas_call(
        body, name="modulate", out_shape=[HBM_OUT((t, D), BF16), HBM_OUT((D, t), BF16)], grid=(t // tm,),
        in_specs=[pl.BlockSpec((tm, D), lambda i: (i, 0)), per_seq, per_seq],
        out_specs=[pl.BlockSpec((tm, D), lambda i: (i, 0)), pl.BlockSpec((D, tm), lambda i: (0, i))],
        compiler_params=_cp(("parallel",)),
    )(x2, sc1p, shift)


TW = 256
TPS = NCOL // NCHIP // TW
NT = NCOL // TW
NQKV_T = 3 * QW // TW


def _tile_tables():
    tabs = np.zeros((NCHIP, 3, NT), np.int32)
    for me in range(NCHIP):
        tiles = [TPS * (me ^ (s // TPS)) + s % TPS for s in range(NT)]
        tabs[me, 0] = tiles
        for row, (lo, hi) in enumerate(((0, NQKV_T), (NQKV_T, NT))):
            mine = [w - lo if lo <= w < hi else None for w in tiles]
            held = next(m for m in mine if m is not None)
            for s, m in enumerate(mine):
                held = held if m is None else m
                tabs[me, 1 + row, s] = held
    return tabs


def _project_gather(h, fulls, tab):
    t = h.shape[0]
    n = len(fulls)

    def body(tab_ref, h_ref, *rest):
        qkv_ref, g_ref = rest[n], rest[n + 1]
        full = rest[n + 2:2 * n + 2]
        w_buf, tile_sems, send_sems, recv_sems = rest[2 * n + 2:]
        s = pl.program_id(0)
        x, y, c = _place()
        me = 2 * x + y
        peers = [(x, 1 - y), (1 - x, y), (1 - x, 1 - y)]
        sibling = (x, y, 1 - c)

        def hop(a, r, stage, chip, half, to):
            window = _shard_window(full[a], W_CUTS[a], chip, half)
            k = 6 * a + 2 * r + stage
            return pltpu.make_async_remote_copy(src_ref=window, dst_ref=window, send_sem=send_sems.at[k], recv_sem=recv_sems.at[k],
                                                device_id=to, device_id_type=MESH)

        def send(a, r):
            return hop(a, r, 0, me, c, (*peers[r], c))

        def arrive(a, r):
            px, py = peers[r]
            chip = 2 * px + py
            hop(a, r, 0, chip, c, (px, py, c)).wait_recv()
            hop(a, r, 1, chip, c, sibling).start()
            hop(a, r, 1, chip, 1 - c, sibling).wait_recv()

        def tile(step, slot):
            col = pl.multiple_of(tab_ref[0, step] * TW, TW)
            return pltpu.make_async_copy(full[0].at[:, pl.ds(col, TW)], w_buf.at[slot], tile_sems.at[slot])

        @pl.when(s == 0)
        def _():
            send(0, 0).start()
            send(0, 1).start()
            tile(0, 0).start()

        slot = s % 2
        tile(s, slot).wait()

        @pl.when((s + 1 < NT) & ((s + 1) % TPS != 0))
        def _():
            tile(s + 1, 1 - slot).start()

        is_qkv = tab_ref[0, s] < NQKV_T
        for k in range(2):
            @pl.when(slot == k)
            def _(k=k):
                acc = _dot(h_ref[...], w_buf[k])

                @pl.when(is_qkv)
                def _():
                    qkv_ref[...] = acc.astype(BF16)

                @pl.when(jnp.logical_not(is_qkv))
                def _():
                    g_ref[...] = acc.astype(BF16)

        for r in range(3):
            @pl.when(s + 1 == TPS * (r + 1))
            def _(r=r):
                arrive(0, r)
                tile(s + 1, 1 - slot).start()
                if r == 0:
                    send(0, 2).start()
                    for a in range(1, n):
                        for q in range(3):
                            send(a, q).start()

        @pl.when(s == NT - 1)
        def _():
            for a in range(1, n):
                for r in range(3):
                    arrive(a, r)
            for a in range(n):
                for r in range(3):
                    send(a, r).wait_send()
                    px, py = peers[r]
                    hop(a, r, 1, 2 * px + py, c, sibling).wait_send()

    outs = pl.pallas_call(
        body, name="project_gather", out_shape=[HBM_OUT((t, 3 * QW), BF16), HBM_OUT((t, NGATE), BF16)] + [SDS(s, BF16) for s in W_FULL],
        grid_spec=pltpu.PrefetchScalarGridSpec(
            num_scalar_prefetch=1, grid=(NT,),
            in_specs=[pl.BlockSpec((t, D), lambda s, tab: (0, 0))] + [ANY] * n,
            out_specs=[pl.BlockSpec((t, TW), lambda s, tab: (0, tab[1, s])), pl.BlockSpec((t, TW), lambda s, tab: (0, tab[2, s]))] + [ANY] * n,
            scratch_shapes=[pltpu.VMEM((2, D, TW), BF16), pltpu.SemaphoreType.DMA((2,)),
                            pltpu.SemaphoreType.DMA((6 * n,)), pltpu.SemaphoreType.DMA((6 * n,))]),
        input_output_aliases={2 + a: 2 + a for a in range(n)},
        compiler_params=_cp(("arbitrary",), VMEM_CAP, side=True),
    )(tab, _in_hbm(h), *fulls)
    return outs[0], outs[1], outs[2:]


def _bias_tables(rel_bias, buckets):
    def body(tab_ref, bk_ref, o_ref):
        a = lax.broadcasted_iota(jnp.int32, (BLK, 2 * BLK), 0)
        b = lax.broadcasted_iota(jnp.int32, (BLK, 2 * BLK), 1)
        steps = a + BLK - b
        valid = (steps >= 0) & (steps <= BLK)
        for g in range(3):
            bk = bk_ref[g]
            for j in range(4):
                def pick(kk, acc, bk=bk, col=4 * g + j):
                    return jnp.where(bk == kk, tab_ref[kk, col], acc)

                acc = lax.fori_loop(0, N_BUCKETS, pick, jnp.zeros((BLK, 2 * BLK), F32))
                o_ref[g, j] = jnp.where(valid, acc, NEG)

    return pl.pallas_call(
        body, name="bias_tables", out_shape=SDS((3, 4, BLK, 2 * BLK), F32),
        in_specs=[pl.BlockSpec(memory_space=pltpu.SMEM), VMEM_SPEC], out_specs=VMEM_SPEC,
    )(rel_bias, buckets)


def _bias_grad(ds_sum, buckets):
    def body(ds_ref, bk_ref, o_ref):
        lane = lax.broadcasted_iota(jnp.int32, (1, 128), 1)
        for g in range(3):
            def bucket(kk, carry, g=g):
                row = jnp.zeros((1, 128), F32)
                for j in range(4):
                    v = jnp.where(bk_ref[g] == kk, ds_ref[g, j], 0.0)
                    s = jnp.sum(jnp.sum(v, axis=1, keepdims=True), axis=0, keepdims=True)
                    row = jnp.where(lane == j, s, row)
                o_ref[g, pl.ds(kk, 1), :] = row
                return carry

            lax.fori_loop(0, N_BUCKETS, bucket, 0)

    return pl.pallas_call(body, name="bias_grad", out_shape=SDS((3, N_BUCKETS, 128), F32), in_specs=[VMEM_SPEC, VMEM_SPEC],
                          out_specs=VMEM_SPEC)(ds_sum, buckets)


def _sub_rows(d, r, n):
    return pl.ds(n * BLK * d + r, BLK) if d == 1 else pl.ds(n * BLK * d + r, BLK, stride=d)


def _head_spec(seq, g, part):
    return pl.BlockSpec((seq, HD), lambda b, hh: (b, part * (QW // HD) + 4 * g + hh))


def _attn_forward(g, qkv, bias, bsz, seq):
    d = DILATIONS[g]
    nblk = seq // d // BLK

    def body(q_ref, k_ref, v_ref, b_ref, o_ref, l_ref, qf, kf, vf):
        hs = pl.program_id(1)
        qf[...] = q_ref[...].astype(F32)
        kf[...] = k_ref[...].astype(F32)
        vf[...] = v_ref[...].astype(F32)
        for r in range(d):
            for n in range(nblk):
                rows = _sub_rows(d, r, n)
                qb = qf[rows, :].astype(BF16)
                s_c = _dot_nt(qb, kf[rows, :].astype(BF16)) * SCALE + b_ref[hs, :, BLK:]
                m = jnp.max(s_c, axis=1, keepdims=True)
                if n > 0:
                    prev = _sub_rows(d, r, n - 1)
                    s_p = _dot_nt(qb, kf[prev, :].astype(BF16)) * SCALE + b_ref[hs, :, :BLK]
                    m = jnp.maximum(m, jnp.max(s_p, axis=1, keepdims=True))
                p_c = jnp.exp(s_c - m)
                den = jnp.sum(p_c, axis=1, keepdims=True)
                acc = _dot(p_c.astype(BF16), vf[rows, :].astype(BF16))
                if n > 0:
                    p_p = jnp.exp(s_p - m)
                    den = den + jnp.sum(p_p, axis=1, keepdims=True)
                    acc = acc + _dot(p_p.astype(BF16), vf[prev, :].astype(BF16))
                o_ref[rows, :] = acc / den
                l_ref[rows, :] = jnp.broadcast_to(m + jnp.log(den), (BLK, HD))

    out_spec = pl.BlockSpec((seq, HD), lambda b, hh: (b, hh))
    return pl.pallas_call(
        body, name=f"attn_forward_{g}", out_shape=[HBM_OUT((bsz * seq, AW), F32)] * 2, grid=(bsz, 4),
        in_specs=[_head_spec(seq, g, part) for part in range(3)] + [pl.BlockSpec((4, BLK, 2 * BLK), lambda b, hh: (0, 0, 0))],
        out_specs=[out_spec, out_spec], scratch_shapes=[pltpu.VMEM((seq, HD), F32)] * 3,
        compiler_params=_cp(("parallel", "parallel"), VMEM_CAP // 2),
    )(qkv, qkv, qkv, _in_hbm(bias))


def _attn_backward(g, qkv, do, dl, bias, prev_out, bsz, seq):
    d = DILATIONS[g]
    nblk = seq // d // BLK

    def body(q_ref, k_ref, v_ref, do_ref, dl_ref, b_ref, *rest):
        dq_ref, dk_ref, dv_ref, db_ref, qf, kf, vf, dqf, dkf, dvf = rest[-10:]
        hs = pl.program_id(1)

        @pl.when((pl.program_id(0) == 0) & (hs == 0))
        def _():
            db_ref[...] = jnp.zeros_like(db_ref)

        qf[...] = q_ref[...].astype(F32)
        kf[...] = k_ref[...].astype(F32)
        vf[...] = v_ref[...].astype(F32)
        dkf[...] = jnp.zeros_like(dkf)
        dvf[...] = jnp.zeros_like(dvf)
        for r in range(d):
            for n in range(nblk):
                rows = _sub_rows(d, r, n)
                qb = qf[rows, :].astype(BF16)
                dob = do_ref[rows, :].astype(BF16)
                both = dl_ref[rows, :]
                lse, delta = both[:, 0:1], both[:, 64:65]
                dq = jnp.zeros((BLK, HD), F32)
                parts = [(rows, slice(BLK, 2 * BLK))]
                if n > 0:
                    parts.append((_sub_rows(d, r, n - 1), slice(0, BLK)))
                for keys, band in parts:
                    kb, vb = kf[keys, :].astype(BF16), vf[keys, :].astype(BF16)
                    p = jnp.exp(_dot_nt(qb, kb) * SCALE + b_ref[hs, :, band] - lse)
                    ds = p * (_dot_nt(dob, vb) - delta)
                    dsb = ds.astype(BF16)
                    dvf[keys, :] += _dot_tn(p.astype(BF16), dob)
                    dkf[keys, :] += _dot_tn(dsb, qb) * SCALE
                    dq = dq + _dot(dsb, kb) * SCALE
                    db_ref[hs, :, band] += ds
                dqf[rows, :] = dq
        dq_ref[...] = dqf[...].astype(BF16)
        dk_ref[...] = dkf[...].astype(BF16)
        dv_ref[...] = dvf[...].astype(BF16)

    qkv_spec = _head_spec(seq, g, 0)
    out_spec = pl.BlockSpec((seq, HD), lambda b, hh: (b, hh))
    band_spec = pl.BlockSpec((4, BLK, 2 * BLK), lambda b, hh: (0, 0, 0))
    ins = [qkv, qkv, qkv, _in_hbm(do), _in_hbm(dl), _in_hbm(bias)]
    in_specs = [_head_spec(seq, g, part) for part in range(3)] + [out_spec, out_spec, band_spec]
    aliases = {}
    if prev_out is not None:
        ins += list(prev_out)
        in_specs += [ANY] * 3
        aliases = {6: 0, 7: 1, 8: 2}
    dq, dk, dv, db = pl.pallas_call(
        body, name=f"attn_backward_{g}", out_shape=[HBM_OUT((bsz * seq, QW), BF16)] * 3 + [SDS((4, BLK, 2 * BLK), F32)], grid=(bsz, 4),
        in_specs=in_specs, out_specs=[qkv_spec] * 3 + [band_spec], input_output_aliases=aliases,
        scratch_shapes=[pltpu.VMEM((seq, HD), F32)] * 6,
        compiler_params=_cp(("arbitrary", "arbitrary"), VMEM_CAP // 2),
    )(*ins)
    return (dq, dk, dv), db


def _mix_forward(gates, og, lg, x2, tgt, gate, w_ao, w_co, w_o, conv_w, conv_b, ln_g, ln_b, bsz, seq, tm=256):
    t = x2.shape[0]
    spt = seq // tm

    def body(g_ref, o1, o2, o3, l1, l2, l3, x_ref, t_ref, gate_ref, wao_ref, wco_ref, wo_ref, cw_ref, cb_ref, lng_ref, lnb_ref,
             ain_ref, sin_ref, mrg_ref, dy_ref, aout_ref, sout_ref, yc_ref, o_ref, lj_ref, dxr_ref, vec_ref, dgate_ref, zc_ref):
        b, i = pl.program_id(0), pl.program_id(1)

        @pl.when((b == 0) & (i == 0))
        def _():
            vec_ref[...] = jnp.zeros_like(vec_ref)

        @pl.when(i == 0)
        def _():
            zc_ref[...] = jnp.zeros_like(zc_ref)
            dgate_ref[...] = jnp.zeros_like(dgate_ref)

        g_attn, u, bg, cg, g_conv, m_attn, m_conv = (g_ref[:, lo:hi].astype(F32) for lo, hi in GATE_COLS)
        la, lb, lc = l1[...], l2[...], l3[...]
        mx = jnp.maximum(la, jnp.maximum(lb, lc))
        ea, eb, ec = jnp.exp(la - mx), jnp.exp(lb - mx), jnp.exp(lc - mx)
        den = ea + eb + ec
        o = (ea * o1[...] + eb * o2[...] + ec * o3[...]) / den
        o_ref[...] = o
        lj_ref[...] = mx + jnp.log(den)
        a_in = o * (g_attn * _sig(g_attn))
        ain_ref[...] = a_in.astype(BF16)
        a_out = _dot(a_in.astype(BF16), wao_ref[...])
        aout_ref[...] = a_out.astype(BF16)
        z = cg * u
        rows = lax.broadcasted_iota(jnp.int32, (tm, D), 0)
        c6, c7 = zc_ref[6:7, :], zc_ref[7:8, :]
        z1 = jnp.where(rows == 0, c7, pltpu.roll(z, 1, 0))
        z2 = jnp.where(rows == 0, c6, jnp.where(rows == 1, c7, pltpu.roll(z, 2, 0)))
        zc_ref[...] = z[tm - 8:tm, :]
        y_conv = (cw_ref[0:1, :] * z2 + cw_ref[1:2, :] * z1 + cw_ref[2:3, :] * z) + cb_ref[...]
        yc_ref[...] = y_conv.astype(BF16)
        s_in = bg * y_conv * (g_conv * _sig(g_conv))
        sin_ref[...] = s_in.astype(BF16)
        s_out = _dot(s_in.astype(BF16), wco_ref[...])
        sout_ref[...] = s_out.astype(BF16)
        merged = _sig(m_attn) * a_out + _sig(m_conv) * s_out
        mrg_ref[...] = merged.astype(BF16)
        y = _dot(merged.astype(BF16), wo_ref[...])
        gate1 = 1.0 + gate_ref[0]
        r = ALPHA * x_ref[...] + gate1 * y
        mu = jnp.mean(r, axis=1, keepdims=True)
        rc = r - mu
        rstd = lax.rsqrt(jnp.mean(rc * rc, axis=1, keepdims=True) + LN_EPS)
        xhat = rc * rstd
        diff = (xhat * lng_ref[...] + lnb_ref[...]) - t_ref[...]
        dout = diff * (1.0 / D)
        vec_ref[0:1, :] += jnp.sum(dout * xhat, axis=0, keepdims=True)
        vec_ref[1:2, :] += jnp.sum(dout, axis=0, keepdims=True)
        vec_ref[2:3, :] += jnp.sum(diff * diff, axis=0, keepdims=True)
        dxh = dout * lng_ref[...]
        dr = rstd * (dxh - jnp.mean(dxh, axis=1, keepdims=True) - xhat * jnp.mean(dxh * xhat, axis=1, keepdims=True))
        dxr_ref[...] = ALPHA * dr
        dy_ref[...] = (dr * gate1).astype(BF16)
        dgate_ref[0] += jnp.sum(dr * y, axis=0, keepdims=True)

    tok = lambda w: pl.BlockSpec((tm, w), lambda b, i: (b * spt + i, 0))
    const = lambda s: pl.BlockSpec(s, lambda b, i: (0,) * len(s))
    per_seq = pl.BlockSpec((1, 1, D), lambda b, i: (b, 0, 0))
    outs = pl.pallas_call(
        body, name="mix_forward", grid=(bsz, spt),
        out_shape=[HBM_OUT((t, AW), BF16), HBM_OUT((t, D), BF16), HBM_OUT((t, D), BF16), HBM_OUT((t, D), BF16), HBM_OUT((t, D), BF16),
                   HBM_OUT((t, D), BF16), HBM_OUT((t, D), BF16), HBM_OUT((t, AW), F32), HBM_OUT((t, AW), F32), HBM_OUT((t, D), F32),
                   SDS((8, D), F32), SDS((bsz, 1, D), F32)],
        in_specs=[tok(NGATE)] + [tok(AW)] * 6 + [tok(D), tok(D), per_seq, const((AW, D)), const((D, D)), const((D, D)),
                                                 const((3, D)), const((1, D)), const((1, D)), const((1, D))],
        out_specs=[tok(AW), tok(D), tok(D), tok(D), tok(D), tok(D), tok(D), tok(AW), tok(AW), tok(D), const((8, D)), per_seq],
        scratch_shapes=[pltpu.VMEM((8, D), F32)],
        compiler_params=_cp(("arbitrary", "arbitrary"), VMEM_CAP),
    )(gates, *map(_in_hbm, og), *map(_in_hbm, lg), x2, tgt, gate, w_ao, w_co, w_o, conv_w, conv_b, ln_g, ln_b)
    return outs


def _mix_backward(gates, dy, a_out, s_out, y_conv, o, lj, w_ao, w_co, w_o, conv_w, bsz, seq, tm=256):
    t = dy.shape[0]
    spt = seq // tm

    def body(g_ref, dy_ref, aout_ref, sout_ref, yc_ref, o_ref, lj_ref, wao_ref, wco_ref, wo_ref, cw_ref,
             dg_ref, do_ref, dl_ref, daout_ref, dsout_ref, vec_ref, car_ref):
        b, i = pl.program_id(0), pl.program_id(1)

        @pl.when((b == 0) & (i == 0))
        def _():
            vec_ref[...] = jnp.zeros_like(vec_ref)

        @pl.when(i == 0)
        def _():
            car_ref[...] = jnp.zeros_like(car_ref)

        g_attn, u, bg, cg, g_conv, m_attn, m_conv = (g_ref[:, lo:hi].astype(F32) for lo, hi in GATE_COLS)
        dmerged = _dot_nt(dy_ref[...], wo_ref[...])
        sa, sc = _sig(m_attn), _sig(m_conv)
        da_out = (dmerged * sa).astype(BF16)
        ds_out = (dmerged * sc).astype(BF16)
        daout_ref[...] = da_out
        dsout_ref[...] = ds_out
        dg_ref[:, 4608:5632] = (dmerged * aout_ref[...].astype(F32) * (sa * (1.0 - sa))).astype(BF16)
        dg_ref[:, 5632:6656] = (dmerged * sout_ref[...].astype(F32) * (sc * (1.0 - sc))).astype(BF16)
        da_in = _dot_nt(da_out, wao_ref[...])
        ds_in = _dot_nt(ds_out, wco_ref[...])
        sga = _sig(g_attn)
        o = o_ref[...]
        do = da_in * (g_attn * sga)
        do_ref[...] = do
        dg_ref[:, 0:512] = (da_in * o * (sga * (1.0 + g_attn * (1.0 - sga)))).astype(BF16)
        prod = do * o
        lane = lax.broadcasted_iota(jnp.int32, (tm, HD), 1)
        for j in range(4):
            cs = slice(j * HD, (j + 1) * HD)
            delta = jnp.sum(prod[:, cs], axis=1, keepdims=True)
            dl_ref[:, cs] = jnp.where(lane < 64, lj_ref[:, cs], delta)
        sgc = _sig(g_conv)
        silu_c = g_conv * sgc
        yc = yc_ref[...].astype(F32)
        dg_ref[:, 1536:2560] = (ds_in * yc * silu_c).astype(BF16)
        dg_ref[:, 3584:4608] = (ds_in * bg * yc * (sgc * (1.0 + g_conv * (1.0 - sgc)))).astype(BF16)
        dyc = ds_in * bg * silu_c
        rows = lax.broadcasted_iota(jnp.int32, (tm, D), 0)
        c0, c1 = car_ref[0:1, :], car_ref[1:2, :]
        n1 = jnp.where(rows == tm - 1, c0, pltpu.roll(dyc, tm - 1, 0))
        n2 = jnp.where(rows == tm - 2, c0, jnp.where(rows == tm - 1, c1, pltpu.roll(dyc, tm - 2, 0)))
        car_ref[...] = dyc[0:8, :]
        dz = cw_ref[2:3, :] * dyc + cw_ref[1:2, :] * n1 + cw_ref[0:1, :] * n2
        z = cg * u
        dg_ref[:, 512:1536] = (dz * cg).astype(BF16)
        dg_ref[:, 2560:3584] = (dz * u).astype(BF16)
        vec_ref[0:1, :] += jnp.sum(n2 * z, axis=0, keepdims=True)
        vec_ref[1:2, :] += jnp.sum(n1 * z, axis=0, keepdims=True)
        vec_ref[2:3, :] += jnp.sum(dyc * z, axis=0, keepdims=True)
        vec_ref[3:4, :] += jnp.sum(dyc, axis=0, keepdims=True)

    tok = lambda w: pl.BlockSpec((tm, w), lambda b, i: (b * spt + (spt - 1 - i), 0))
    const = lambda s: pl.BlockSpec(s, lambda b, i: (0,) * len(s))
    return pl.pallas_call(
        body, name="mix_backward", grid=(bsz, spt),
        out_shape=[HBM_OUT((t, NGATE), BF16), HBM_OUT((t, AW), F32), HBM_OUT((t, AW), F32), HBM_OUT((t, D), BF16), HBM_OUT((t, D), BF16),
                   SDS((8, D), F32)],
        in_specs=[tok(NGATE), tok(D), tok(D), tok(D), tok(D), tok(AW), tok(AW), const((AW, D)), const((D, D)), const((D, D)), const((3, D))],
        out_specs=[tok(NGATE), tok(AW), tok(AW), tok(D), tok(D), const((8, D))],
        scratch_shapes=[pltpu.VMEM((8, D), F32)],
        compiler_params=_cp(("arbitrary", "arbitrary"), VMEM_CAP),
    )(gates, dy, a_out, s_out, y_conv, o, lj, w_ao, w_co, w_o, conv_w)


def _out_weight_grads(a_in, da_out, s_in, ds_out, merged, dy, tk=512):
    t = dy.shape[0]

    def body(ain_ref, da_ref, sin_ref, ds_ref, m_ref, dy_ref, gao_ref, gco_ref, go_ref):
        @pl.when(pl.program_id(0) == 0)
        def _():
            gao_ref[...] = jnp.zeros_like(gao_ref)
            gco_ref[...] = jnp.zeros_like(gco_ref)
            go_ref[...] = jnp.zeros_like(go_ref)

        gao_ref[...] += _dot_tn(ain_ref[...], da_ref[...])
        gco_ref[...] += _dot_tn(sin_ref[...], ds_ref[...])
        go_ref[...] += _dot_tn(m_ref[...], dy_ref[...])

    tok = lambda w: pl.BlockSpec((tk, w), lambda i: (i, 0))
    const = lambda s: pl.BlockSpec(s, lambda i: (0, 0))
    return pl.pallas_call(
        body, name="out_weight_grads", grid=(t // tk,), out_shape=[SDS((AW, D), F32), SDS((D, D), F32), SDS((D, D), F32)],
        in_specs=[tok(AW), tok(D), tok(D), tok(D), tok(D), tok(D)], out_specs=[const((AW, D)), const((D, D)), const((D, D))],
        compiler_params=_cp(("arbitrary",), VMEM_CAP),
    )(a_in, da_out, s_in, ds_out, merged, dy)


def _input_grad(dq, dk, dv, dgates, w, x2, dxr, sc1p, seq, sums, tm=512):
    t = x2.shape[0]
    nt = t // tm
    spt = seq // tm
    bsz = t // seq
    n = len(sums)
    gblk = NGATE // 4
    nsteps = 3 + 4

    def body(dq_ref, dk_ref, dv_ref, dg_ref, wq_ref, wg_ref, x_ref, dxr_ref, sc_ref, *rest):
        src, (dx_ref, dsh_ref, dsc_ref), land = rest[:n], rest[n:n + 3], rest[n + 3:2 * n + 3]
        acc_ref, send_sems, recv_sems = rest[2 * n + 3:]
        j, i = pl.program_id(0), pl.program_id(1)
        px, py, pc = _place()
        chips = [(1 - px, py), (px, 1 - py), (1 - px, 1 - py)]
        copies = [pltpu.make_async_remote_copy(src_ref=src[a].at[2 * cx + cy], dst_ref=land[a].at[r], send_sem=send_sems.at[3 * a + r],
                                               recv_sem=recv_sems.at[3 * a + r], device_id=(cx, cy, pc), device_id_type=MESH)
                  for a in range(n) for r, (cx, cy) in enumerate(chips)]
        rows = pl.ds(pl.multiple_of(i * tm, tm), tm)

        @pl.when((i == 0) & (j == 0))
        def _():
            for cp in copies:
                cp.start()

        for k, ref in enumerate((dq_ref, dk_ref, dv_ref)):
            @pl.when(j == k)
            def _(k=k, ref=ref):
                part = _dot_nt(ref[...], wq_ref[...])
                if k == 0:
                    acc_ref[rows, :] = part
                else:
                    acc_ref[rows, :] += part

        @pl.when((j >= 3) & (j < nsteps - 1))
        def _():
            acc_ref[rows, :] += _dot_nt(dg_ref[...], wg_ref[...])

        @pl.when(j == nsteps - 1)
        def _():
            dh = acc_ref[rows, :] + _dot_nt(dg_ref[...], wg_ref[...])
            dx_ref[...] = dh * sc_ref[0] + dxr_ref[...]

            @pl.when(i % spt == 0)
            def _():
                dsh_ref[...] = jnp.zeros_like(dsh_ref)
                dsc_ref[...] = jnp.zeros_like(dsc_ref)

            dsh_ref[0] += jnp.sum(dh, axis=0, keepdims=True)
            dsc_ref[0] += jnp.sum(dh * x_ref[...], axis=0, keepdims=True)

        @pl.when((i == nt - 1) & (j == nsteps - 1))
        def _():
            for cp in copies:
                cp.wait()

    def held(k):
        return lambda j, i: (jnp.where(j == k, i, jnp.where(j < k, 0, nt - 1)), 0)

    last = lambda j, i: (jnp.where(j == nsteps - 1, i, 0), 0)
    outs = pl.pallas_call(
        body, name="input_grad", grid=(nsteps, nt),
        out_shape=[SDS((t, D), F32), SDS((bsz, 1, D), F32), SDS((bsz, 1, D), F32)] + [SDS((3,) + s.shape[1:], BF16) for s in sums],
        in_specs=[pl.BlockSpec((tm, QW), held(0)), pl.BlockSpec((tm, QW), held(1)), pl.BlockSpec((tm, QW), held(2)),
                  pl.BlockSpec((tm, gblk), lambda j, i: (jnp.where(j >= 3, i, 0), jnp.clip(j - 3, 0, 3))),
                  pl.BlockSpec((D, QW), lambda j, i: (0, jnp.minimum(j, 2))),
                  pl.BlockSpec((pl.Element(D), pl.Element(gblk)), lambda j, i: (0, pl.multiple_of(3 * QW + gblk * jnp.clip(j - 3, 0, 3), 128))),
                  pl.BlockSpec((tm, D), last), pl.BlockSpec((tm, D), last),
                  pl.BlockSpec((1, 1, D), lambda j, i: (jnp.where(j == nsteps - 1, i // spt, 0), 0, 0))] + [ANY] * n,
        out_specs=[pl.BlockSpec((tm, D), last),
                   pl.BlockSpec((1, 1, D), lambda j, i: (jnp.where(j == nsteps - 1, i // spt, 0), 0, 0)),
                   pl.BlockSpec((1, 1, D), lambda j, i: (jnp.where(j == nsteps - 1, i // spt, 0), 0, 0))] + [ANY] * n,
        scratch_shapes=[pltpu.VMEM((t, D), F32), pltpu.SemaphoreType.DMA((3 * NCHIP,)), pltpu.SemaphoreType.DMA((3 * NCHIP,))],
        compiler_params=_cp(("arbitrary", "arbitrary"), VMEM_CAP, side=True),
    )(dq, dk, dv, dgates, w, w, x2, dxr, sc1p, *sums)
    return outs[0], outs[1], outs[2], outs[3:]


def _in_weight_grad(ht, src, col0, prev, name):
    t = ht.shape[1]
    ncols = src.shape[1] // TN

    def body(ht_ref, s_ref, *rest):
        rest[-1][...] = _dot(ht_ref[...], s_ref[...].astype(BF16))

    ins = [ht, src]
    in_specs = [pl.BlockSpec((D, t), lambda j: (0, 0)), pl.BlockSpec((t, TN), lambda j: (0, j))]
    aliases = {}
    if prev is not None:
        ins.append(prev)
        in_specs.append(ANY)
        aliases = {2: 0}
    return pl.pallas_call(
        body, name=name, grid=(ncols,), out_shape=SDS((D, NCOL), F32), in_specs=in_specs,
        out_specs=pl.BlockSpec((D, TN), lambda j: (0, col0 + j)), input_output_aliases=aliases,
        compiler_params=_cp(("arbitrary",), VMEM_CAP),
    )(*ins)


def _sum_partials(gathered):
    def body(g_ref, o_ref):
        acc = g_ref[0]
        for k in range(1, 8):
            acc = acc + g_ref[k]
        o_ref[...] = acc

    return pl.pallas_call(body, name="sum_partials", out_shape=SDS(gathered.shape[1:], F32), in_specs=[VMEM_SPEC], out_specs=VMEM_SPEC)(gathered)


def _adamw(w, g, m, v, name, tr=256):
    r, cdim = w.shape
    tr = tr if cdim <= D else tr // 2
    tr = tr if (r % tr == 0 and r > tr) else r

    def body(w_ref, g_ref, m_ref, v_ref, d_ref, nm_ref, nv_ref):
        gv = g_ref[...]
        nm = B1 * m_ref[...] + (1.0 - B1) * gv
        nv = B2 * v_ref[...] + (1.0 - B2) * (gv * gv)
        m_hat = nm / (1.0 - B1 ** STEP)
        v_hat = nv / (1.0 - B2 ** STEP)
        d_ref[...] = -LR * (m_hat / (jnp.sqrt(v_hat) + EPS) + WD * w_ref[...])
        nm_ref[...] = nm
        nv_ref[...] = nv

    spec = pl.BlockSpec((tr, cdim), lambda i: (i, 0))
    return pl.pallas_call(
        body, name=name, grid=(r // tr,), out_shape=[SDS((r, cdim), F32)] * 3, in_specs=[spec] * 4, out_specs=[spec] * 3,
        compiler_params=_cp(("parallel",), VMEM_CAP // 2),
    )(w, g, m, v)


def _t5_bucket(dist):
    n = jnp.maximum(dist, 1).astype(F32)
    large = MAX_EXACT + (jnp.log(n / MAX_EXACT) / math.log(MAX_DISTANCE / MAX_EXACT) * (N_BUCKETS - MAX_EXACT)).astype(jnp.int32)
    large = jnp.minimum(large, N_BUCKETS - 1)
    return jnp.where(dist < MAX_EXACT, dist, large)


def _band_buckets():
    a = jnp.arange(BLK)[:, None]
    b = jnp.arange(2 * BLK)[None, :]
    steps = jnp.maximum(a + BLK - b, 0)
    return jnp.stack([_t5_bucket(steps * d) for d in DILATIONS]).astype(jnp.int32)


def _pad_rows(a, rows=8):
    return jnp.pad(a, ((0, rows - a.shape[0]), (0, 0)))


def kernel(x, c, w_ada, b_ada, w_in, conv_w, conv_b, rel_bias, w_attn_out, w_conv_out, w_o, ln_g, ln_b, loss_target, m_w_ada, m_b_ada, m_w_in, m_conv_w, m_conv_b, m_rel_bias, m_w_attn_out, m_w_conv_out, m_w_o, m_ln_g, m_ln_b, v_w_ada, v_b_ada, v_w_in, v_conv_w, v_conv_b, v_rel_bias, v_w_attn_out, v_w_conv_out, v_w_o, v_ln_g, v_ln_b):
    bsz, seq, _ = x.shape
    t = bsz * seq
    mx, my, mc = _place()
    chip = 2 * mx + my
    dev = 4 * mx + 2 * my + mc
    x2 = x.reshape(t, D)
    tgt = loss_target.reshape(t, D)

    mine = [_to_bf16_window(a, w[0], f"to_bf16_{a}") for a, w in enumerate((w_in, w_attn_out, w_conv_out, w_o))]

    n_ada = w_ada.shape[2]
    n_cw = conv_w.shape[2]
    c_and_cw = jnp.concatenate([_pad_rows(c), jnp.pad(conv_w[0], ((0, 5), (0, D - n_cw)))], axis=0)
    firsts = _all_gather8(c_and_cw, "gather_c_conv_w")
    c_all = firsts[:, 0:bsz, :].reshape(8 * bsz, D)
    conv_w_f = firsts[0::2, 8:11, 0:n_cw].transpose(1, 0, 2).reshape(3, D)
    b_cols = lax.dynamic_slice(b_ada, (0, chip * n_ada), (1, n_ada))
    mod_part = _ada_forward(c_all, w_ada[0], b_cols)
    mod_parts = _all_gather8(mod_part, "gather_mod")
    mod_all = mod_parts[0::2].transpose(1, 0, 2).reshape(8 * bsz, 3 * D)
    mod = lax.dynamic_slice(mod_all, (dev * bsz, 0), (bsz, 3 * D))
    shift = mod[:, 0:D].reshape(bsz, 1, D)
    sc1p = 1.0 + mod[:, D:2 * D].reshape(bsz, 1, D)
    gate = mod[:, 2 * D:].reshape(bsz, 1, D)

    h, ht = _modulate(x2, sc1p, shift, seq)
    tab = lax.dynamic_index_in_dim(jnp.asarray(_tile_tables()), chip, 0, keepdims=False)
    qkv, gates, (w_in_f, w_ao_f, w_co_f, w_o_f) = _project_gather(h, mine, tab)
    buckets = _band_buckets()
    bias = _bias_tables(rel_bias, buckets)
    og, lg = [], []
    for g in range(3):
        o_g, l_g = _attn_forward(g, qkv, bias[g], bsz, seq)
        og.append(o_g)
        lg.append(l_g)
    (a_in, s_in, merged, dy, a_out, s_out, y_conv, o, lj, dxr, vec_f, dgate) = _mix_forward(
        gates, og, lg, x2, tgt, gate, w_ao_f, w_co_f, w_o_f, conv_w_f, conv_b, ln_g, ln_b, bsz, seq)

    dgates, do, dl, da_out, ds_out, vec_b = _mix_backward(gates, dy, a_out, s_out, y_conv, o, lj, w_ao_f, w_co_f, w_o_f, conv_w_f, bsz, seq)
    g_ao, g_co, g_o = _out_weight_grads(a_in, da_out, s_in, ds_out, merged, dy)
    dqkv, dbs = None, []
    for g in range(3):
        dqkv, db = _attn_backward(g, qkv, do, dl, bias[g], dqkv, bsz, seq)
        dbs.append(db)
    dq, dk, dv = dqkv
    drb = _bias_grad(jnp.stack(dbs), buckets)
    drb = drb[:, :, 0:4].transpose(1, 0, 2).reshape(N_BUCKETS, 12)
    g_in = None
    for n, src in enumerate((dq, dk, dv, dgates)):
        g_in = _in_weight_grad(ht, src, n * NQT, g_in, f"in_weight_grad_{n}")

    grads = [g_in, g_ao, g_co, g_o]
    got = _swap_halves(grads)
    sums = [_chip_sum(a, grads[a], got[a], f"chip_sum_{a}") for a in range(4)]
    grad_x, dshift, dscale, landed = _input_grad(dq, dk, dv, dgates, w_in_f, x2, dxr, sc1p, seq, [s[1] for s in sums])
    halves = [_reduce_mine(a, sums[a][0], landed[a], f"reduce_mine_{a}") for a in range(4)]
    gw_in, gw_ao, gw_co, gw_o = _join_halves(halves)

    dmod = jnp.concatenate([dshift, dscale, dgate], axis=2).reshape(bsz * 3, D)
    drb_row = jnp.pad(drb.reshape(1, N_BUCKETS * 12), ((0, 0), (0, D - N_BUCKETS * 12)))
    packed = jnp.concatenate([vec_f, vec_b, _pad_rows(dmod), _pad_rows(drb_row)], axis=0)
    gathered = _all_gather8(packed, "gather_small")
    small = _sum_partials(gathered)
    g_ln_g, g_ln_b, loss_lanes = small[0:1], small[1:2], small[2:3]
    g_conv_w_full, g_conv_b = small[8:11], small[11:12]
    g_rel_bias = small[24, 0:N_BUCKETS * 12].reshape(N_BUCKETS, 12)
    loss = 0.5 / D * jnp.sum(loss_lanes)
    dmod_all = gathered[:, 16:16 + 3 * bsz, :].reshape(8 * bsz, 3 * D)
    dmod_cols = lax.dynamic_slice(dmod_all, (0, chip * n_ada), (8 * bsz, n_ada))
    gw_ada, gb_ada = _ada_backward(c_all, dmod_cols, dmod_all)
    g_conv_w = lax.dynamic_slice(g_conv_w_full, (0, chip * n_cw), (3, n_cw))

    names = ["w_ada", "b_ada", "w_in", "conv_w", "conv_b", "rel_bias", "w_attn_out", "w_conv_out", "w_o", "ln_g", "ln_b"]
    two_d = lambda a: a.reshape(a.shape[-2:]) if a.ndim == 3 else a
    weights = dict(zip(names, map(two_d, (w_ada, b_ada, w_in, conv_w, conv_b, rel_bias, w_attn_out, w_conv_out, w_o, ln_g, ln_b))))
    ms = dict(zip(names, map(two_d, (m_w_ada, m_b_ada, m_w_in, m_conv_w, m_conv_b, m_rel_bias, m_w_attn_out, m_w_conv_out, m_w_o, m_ln_g, m_ln_b))))
    vs = dict(zip(names, map(two_d, (v_w_ada, v_b_ada, v_w_in, v_conv_w, v_conv_b, v_rel_bias, v_w_attn_out, v_w_conv_out, v_w_o, v_ln_g, v_ln_b))))
    grads = dict(zip(names, (gw_ada, gb_ada, gw_in, g_conv_w, g_conv_b, g_rel_bias, gw_ao, gw_co, gw_o, g_ln_g, g_ln_b)))
    shapes = dict(zip(names, (w_ada, b_ada, w_in, conv_w, conv_b, rel_bias, w_attn_out, w_conv_out, w_o, ln_g, ln_b)))
    deltas, new_m, new_v = {}, {}, {}
    for n in names:
        deltas[n], new_m[n], new_v[n] = _adamw(weights[n], grads[n], ms[n], vs[n], f"adamw_{n}")
    shaped = lambda d: [d[n].reshape(shapes[n].shape) for n in names]
    return (loss, grad_x.reshape(bsz, seq, D), *shaped(grads), *shaped(deltas), *shaped(new_m), *shaped(new_v))
```

```python
import math

import numpy as np
import jax
import jax.numpy as jnp
from jax import lax
from jax.experimental import pallas as pl
from jax.experimental.pallas import tpu as pltpu

F32 = jnp.float32
BF16 = jnp.bfloat16
SDS = jax.ShapeDtypeStruct
MESH = pl.DeviceIdType.MESH
HBM_OUT = pltpu.HBM
ANY = pl.BlockSpec(memory_space=pl.ANY)
VMEM_SPEC = pl.BlockSpec(memory_space=pltpu.VMEM)

D = 1024
HD = 128
BLK = 128
QW = 1536
AW = 512
NGATE = 6656
GATE_COLS = ((0, 512), (512, 1536), (1536, 2560), (2560, 3584), (3584, 4608), (4608, 5632), (5632, 6656))
NCOL = 3 * QW + NGATE
TN = 512
NQT = QW // TN
NPT = NCOL // TN
DILATIONS = (1, 4, 16)
N_BUCKETS, MAX_EXACT, MAX_DISTANCE = 32, 16, 2048
ALPHA = 2.0 ** 0.25
LN_EPS = 1e-5
NEG = -1e30
SCALE = HD ** -0.5
LR, B1, B2, EPS, WD, STEP = 0.001, 0.9, 0.999, 1e-08, 0.01, 10
NCHIP = 4
VMEM_CAP = 60 * 2 ** 20


def _cp(sem=None, vmem=None, side=False):
    return pltpu.CompilerParams(dimension_semantics=sem, vmem_limit_bytes=vmem, has_side_effects=side)


def _dot(a, b):
    return jnp.dot(a, b, preferred_element_type=F32)


def _dot_nt(a, b):
    return lax.dot_general(a, b, (((1,), (1,)), ((), ())), preferred_element_type=F32)


def _dot_tn(a, b):
    return lax.dot_general(a, b, (((0,), (0,)), ((), ())), preferred_element_type=F32)


def _sig(x):
    return 1.0 / (1.0 + jnp.exp(-x))


def _in_hbm(a):
    return pltpu.with_memory_space_constraint(a, pltpu.HBM)


def _place():
    x, y, c = lax.axis_index("x"), lax.axis_index("y"), lax.axis_index("c")
    return x, y, c


def _all_gather8(v, name):
    r, cdim = v.shape

    def body(v_ref, out_ref, send_sems, recv_sems, local_sem):
        x, y, c = _place()
        me = 4 * x + 2 * y + c
        peers = [(x, y, 1 - c), (1 - x, y, c), (x, 1 - y, c), (1 - x, 1 - y, c),
                 (1 - x, y, 1 - c), (x, 1 - y, 1 - c), (1 - x, 1 - y, 1 - c)]
        mine = pltpu.make_async_copy(v_ref, out_ref.at[me], local_sem)
        mine.start()

        def copy(k, block, to):
            return pltpu.make_async_remote_copy(src_ref=v_ref, dst_ref=out_ref.at[block], send_sem=send_sems.at[k],
                                                recv_sem=recv_sems.at[k], device_id=to, device_id_type=MESH)

        sends = [copy(k, me, p) for k, p in enumerate(peers)]
        for cp in sends:
            cp.start()
        for k, (px, py, pc) in enumerate(peers):
            copy(k, 4 * px + 2 * py + pc, (px, py, pc)).wait_recv()
        for cp in sends:
            cp.wait_send()
        mine.wait()

    return pl.pallas_call(
        body, name=name, out_shape=SDS((8, r, cdim), v.dtype), in_specs=[VMEM_SPEC], out_specs=VMEM_SPEC,
        scratch_shapes=[pltpu.SemaphoreType.DMA((7,)), pltpu.SemaphoreType.DMA((7,)), pltpu.SemaphoreType.DMA(())],
        compiler_params=_cp(side=True),
    )(v)


W_CUTS = (("col", D, NCOL // NCHIP), ("col", AW, D // NCHIP), ("row", D // NCHIP, D), ("row", D // NCHIP, D))
W_FULL = ((D, NCOL), (AW, D), (D, D), (D, D))


def _shard_window(ref, cut, k, half):
    kind, nr, nc = cut
    hr = nr // 2
    if kind == "col":
        rows = pl.ds(0, nr) if half is None else pl.ds(pl.multiple_of(half * hr, 16), hr)
        return ref.at[rows, pl.ds(pl.multiple_of(k * nc, 128), nc)]
    if half is None:
        return ref.at[pl.ds(pl.multiple_of(k * nr, 16), nr), :]
    return ref.at[pl.ds(pl.multiple_of(k * nr + half * hr, 16), hr), :]


def _half_rows(ref, cut, half):
    hr = cut[1] // 2
    return ref.at[pl.ds(pl.multiple_of(half * hr, 16), hr), :]


def _to_bf16_window(a, w, name):
    kind, nr, nc = W_CUTS[a]
    x, y, _ = _place()
    chip = jnp.reshape(2 * x + y, (1,)).astype(jnp.int32)
    tr = min(nr, 256)

    def body(c_ref, w_ref, o_ref):
        o_ref[...] = w_ref[...].astype(BF16)

    out_map = (lambda i, cr: (i, cr[0])) if kind == "col" else (lambda i, cr: (cr[0] * (nr // tr) + i, 0))
    return pl.pallas_call(
        body, name=name, out_shape=SDS(W_FULL[a], BF16),
        grid_spec=pltpu.PrefetchScalarGridSpec(num_scalar_prefetch=1, grid=(nr // tr,),
                                               in_specs=[pl.BlockSpec((tr, nc), lambda i, cr: (i, 0))], out_specs=pl.BlockSpec((tr, nc), out_map)),
        compiler_params=_cp(("arbitrary",)),
    )(chip, w)


def _swap_halves(grads):
    n = len(grads)
    shapes = []
    for a in range(n):
        kind, nr, nc = W_CUTS[a]
        shapes.append((W_FULL[a][0] // 2, W_FULL[a][1]) if kind == "col" else (NCHIP, nr // 2, nc))

    def pieces(a, ref, land, half):
        kind, nr, nc = W_CUTS[a]
        if kind == "col":
            hr = nr // 2
            return [(ref.at[pl.ds(pl.multiple_of(half * hr, 16), hr), :], land)]
        return [(_shard_window(ref, W_CUTS[a], k, half), land.at[k]) for k in range(NCHIP)]

    def body(*refs):
        src, land = refs[:n], refs[n:2 * n]
        send_sems, recv_sems = refs[2 * n:]
        x, y, c = _place()
        sibling = (x, y, 1 - c)
        sends = []
        k = 0
        for a in range(n):
            for s, d in pieces(a, src[a], land[a], 1 - c):
                cp = pltpu.make_async_remote_copy(src_ref=s, dst_ref=d, send_sem=send_sems.at[k], recv_sem=recv_sems.at[k],
                                                  device_id=sibling, device_id_type=MESH)
                cp.start()
                sends.append(cp)
                k += 1
        for cp in sends:
            cp.wait()

    n_sems = sum(1 if W_CUTS[a][0] == "col" else NCHIP for a in range(n))
    return pl.pallas_call(
        body, name="swap_grad_halves", out_shape=[SDS(s, F32) for s in shapes], in_specs=[ANY] * n, out_specs=[ANY] * n,
        scratch_shapes=[pltpu.SemaphoreType.DMA((n_sems,)), pltpu.SemaphoreType.DMA((n_sems,))],
        compiler_params=_cp(side=True),
    )(*grads)


def _chip_sum(a, grad, got, name):
    kind, nr, nc = W_CUTS[a]
    hr = nr // 2
    c = lax.axis_index("c")
    cidx = jnp.reshape(c, (1,)).astype(jnp.int32)

    def body(c_ref, g_ref, r_ref, f_ref, b_ref):
        s = g_ref[...] + r_ref[...]
        f_ref[...] = s.reshape(f_ref.shape)
        b_ref[...] = s.astype(BF16).reshape(b_ref.shape)

    if kind == "col":
        in_specs = [pl.BlockSpec((hr, nc), lambda k, cr: (cr[0], k)), pl.BlockSpec((hr, nc), lambda k, cr: (0, k))]
    else:
        grad = grad.reshape(NCHIP, 2, hr, nc)
        in_specs = [pl.BlockSpec((1, 1, hr, nc), lambda k, cr: (k, cr[0], 0, 0)), pl.BlockSpec((1, hr, nc), lambda k, cr: (k, 0, 0))]
    out_specs = [pl.BlockSpec((1, hr, nc), lambda k, cr: (k, 0, 0))] * 2
    return pl.pallas_call(
        body, name=name, out_shape=[SDS((NCHIP, hr, nc), F32), SDS((NCHIP, hr, nc), BF16)],
        grid_spec=pltpu.PrefetchScalarGridSpec(num_scalar_prefetch=1, grid=(NCHIP,), in_specs=in_specs, out_specs=out_specs),
        compiler_params=_cp(("arbitrary",), VMEM_CAP),
    )(cidx, grad, got)


def _reduce_mine(a, mine_f32, got, name):
    kind, nr, nc = W_CUTS[a]
    hr = nr // 2
    x, y, c = _place()
    where = jnp.stack([2 * x + y, c]).astype(jnp.int32)
    tr = min(hr, 256)

    def body(w_ref, m_ref, g_ref, o_ref):
        o_ref[...] = ((m_ref[0] + g_ref[0].astype(F32)) + g_ref[1].astype(F32)) + g_ref[2].astype(F32)

    return pl.pallas_call(
        body, name=name, out_shape=SDS((nr, nc), F32),
        grid_spec=pltpu.PrefetchScalarGridSpec(
            num_scalar_prefetch=1, grid=(hr // tr,),
            in_specs=[pl.BlockSpec((1, tr, nc), lambda i, wr: (wr[0], i, 0)), pl.BlockSpec((3, tr, nc), lambda i, wr: (0, i, 0))],
            out_specs=pl.BlockSpec((tr, nc), lambda i, wr: (wr[1] * (hr // tr) + i, 0))),
        compiler_params=_cp(("arbitrary",), VMEM_CAP),
    )(where, mine_f32, got)


def _join_halves(fulls):
    n = len(fulls)

    def body(*refs):
        full = refs[n:2 * n]
        send_sems, recv_sems = refs[2 * n:]
        x, y, c = _place()
        sibling = (x, y, 1 - c)

        def swap(a, half):
            rows = _half_rows(full[a], W_CUTS[a], half)
            return pltpu.make_async_remote_copy(src_ref=rows, dst_ref=rows, send_sem=send_sems.at[a], recv_sem=recv_sems.at[a],
                                                device_id=sibling, device_id_type=MESH)

        sends = [swap(a, c) for a in range(n)]
        for cp in sends:
            cp.start()
        for a, cp in enumerate(sends):
            cp.wait_send()
            swap(a, 1 - c).wait_recv()

    return pl.pallas_call(
        body, name="join_grad_halves", out_shape=[SDS((W_CUTS[a][1], W_CUTS[a][2]), F32) for a in range(n)],
        in_specs=[ANY] * n, out_specs=[ANY] * n,
        scratch_shapes=[pltpu.SemaphoreType.DMA((n,)), pltpu.SemaphoreType.DMA((n,))],
        input_output_aliases={a: a for a in range(n)}, compiler_params=_cp(side=True),
    )(*fulls)


def _ada_forward(c_all, w_ada, b_cols):
    nb, nc = c_all.shape[0], w_ada.shape[1]

    def body(c_ref, w_ref, b_ref, o_ref):
        cv = c_ref[...]
        sc = (cv * _sig(cv)).astype(BF16)
        o_ref[...] = _dot(sc, w_ref[...].astype(BF16)) + b_ref[...]

    return pl.pallas_call(body, name="ada_forward", out_shape=SDS((nb, nc), F32), compiler_params=_cp(vmem=VMEM_CAP // 2))(c_all, w_ada, b_cols)


def _ada_backward(c_all, dmod_cols, dmod_all):
    nb, nc = dmod_cols.shape

    def body(c_ref, d_ref, a_ref, gw_ref, gb_ref):
        cv = c_ref[...]
        sc = (cv * _sig(cv)).astype(BF16)
        gw_ref[...] = _dot_tn(sc, d_ref[...].astype(BF16))
        gb_ref[...] = jnp.sum(a_ref[...], axis=0, keepdims=True)

    return pl.pallas_call(body, name="ada_backward", out_shape=[SDS((D, nc), F32), SDS((1, dmod_all.shape[1]), F32)],
                          compiler_params=_cp(vmem=VMEM_CAP // 2))(c_all, dmod_cols, dmod_all)


def _modulate(x2, sc1p, shift, seq, tm=256):
    t = x2.shape[0]
    spt = seq // tm

    def body(x_ref, sc_ref, sh_ref, h_ref, ht_ref):
        h = x_ref[...] * sc_ref[0] + sh_ref[0]
        h_ref[...] = h.astype(BF16)
        ht_ref[...] = h.T.astype(BF16)

    per_seq = pl.BlockSpec((1, 1, D), lambda i: (i // spt, 0, 0))
    return pl.pallas_call(
        body, name="modulate", out_shape=[HBM_OUT((t, D), BF16), HBM_OUT((D, t), BF16)], grid=(t // tm,),
        in_specs=[pl.BlockSpec((tm, D), lambda i: (i, 0)), per_seq, per_seq],
        out_specs=[pl.BlockSpec((tm, D), lambda i: (i, 0)), pl.BlockSpec((D, tm), lambda i: (0, i))],
        compiler_params=_cp(("parallel",)),
    )(x2, sc1p, shift)


TW = 256
TPS = NCOL // NCHIP // TW
NT = NCOL // TW
NQKV_T = 3 * QW // TW


def _tile_tables():
    tabs = np.zeros((NCHIP, 3, NT), np.int32)
    for me in range(NCHIP):
        tiles = [TPS * (me ^ (s // TPS)) + s % TPS for s in range(NT)]
        tabs[me, 0] = tiles
        for row, (lo, hi) in enumerate(((0, NQKV_T), (NQKV_T, NT))):
            mine = [w - lo if lo <= w < hi else None for w in tiles]
            held = next(m for m in mine if m is not None)
            for s, m in enumerate(mine):
                held = held if m is None else m
                tabs[me, 1 + row, s] = held
    return tabs


def _project_gather(h, fulls, tab):
    t = h.shape[0]
    n = len(fulls)

    def body(tab_ref, h_ref, *rest):
        qkv_ref, g_ref = rest[n], rest[n + 1]
        full = rest[n + 2:2 * n + 2]
        w_buf, tile_sems, send_sems, recv_sems = rest[2 * n + 2:]
        s = pl.program_id(0)
        x, y, c = _place()
        me = 2 * x + y
        peers = [(x, 1 - y), (1 - x, y), (1 - x, 1 - y)]
        sibling = (x, y, 1 - c)

        def hop(a, r, stage, chip, half, to):
            window = _shard_window(full[a], W_CUTS[a], chip, half)
            k = 6 * a + 2 * r + stage
            return pltpu.make_async_remote_copy(src_ref=window, dst_ref=window, send_sem=send_sems.at[k], recv_sem=recv_sems.at[k],
                                                device_id=to, device_id_type=MESH)

        def send(a, r):
            return hop(a, r, 0, me, c, (*peers[r], c))

        def arrive(a, r):
            px, py = peers[r]
            chip = 2 * px + py
            hop(a, r, 0, chip, c, (px, py, c)).wait_recv()
            hop(a, r, 1, chip, c, sibling).start()
            hop(a, r, 1, chip, 1 - c, sibling).wait_recv()

        def tile(step, slot):
            col = pl.multiple_of(tab_ref[0, step] * TW, TW)
            return pltpu.make_async_copy(full[0].at[:, pl.ds(col, TW)], w_buf.at[slot], tile_sems.at[slot])

        @pl.when(s == 0)
        def _():
            send(0, 0).start()
            send(0, 1).start()
            tile(0, 0).start()

        slot = s % 2
        tile(s, slot).wait()

        @pl.when((s + 1 < NT) & ((s + 1) % TPS != 0))
        def _():
            tile(s + 1, 1 - slot).start()

        is_qkv = tab_ref[0, s] < NQKV_T
        for k in range(2):
            @pl.when(slot == k)
            def _(k=k):
                acc = _dot(h_ref[...], w_buf[k])

                @pl.when(is_qkv)
                def _():
                    qkv_ref[...] = acc.astype(BF16)

                @pl.when(jnp.logical_not(is_qkv))
                def _():
                    g_ref[...] = acc.astype(BF16)

        for r in range(3):
            @pl.when(s + 1 == TPS * (r + 1))
            def _(r=r):
                arrive(0, r)
                tile(s + 1, 1 - slot).start()
                if r == 0:
                    send(0, 2).start()
                    for a in range(1, n):
                        for q in range(3):
                            send(a, q).start()

        @pl.when(s == NT - 1)
        def _():
            for a in range(1, n):
                for r in range(3):
                    arrive(a, r)
            for a in range(n):
                for r in range(3):
                    send(a, r).wait_send()
                    px, py = peers[r]
                    hop(a, r, 1, 2 * px + py, c, sibling).wait_send()

    outs = pl.pallas_call(
        body, name="project_gather", out_shape=[HBM_OUT((t, 3 * QW), BF16), HBM_OUT((t, NGATE), BF16)] + [SDS(s, BF16) for s in W_FULL],
        grid_spec=pltpu.PrefetchScalarGridSpec(
            num_scalar_prefetch=1, grid=(NT,),
            in_specs=[pl.BlockSpec((t, D), lambda s, tab: (0, 0))] + [ANY] * n,
            out_specs=[pl.BlockSpec((t, TW), lambda s, tab: (0, tab[1, s])), pl.BlockSpec((t, TW), lambda s, tab: (0, tab[2, s]))] + [ANY] * n,
            scratch_shapes=[pltpu.VMEM((2, D, TW), BF16), pltpu.SemaphoreType.DMA((2,)),
                            pltpu.SemaphoreType.DMA((6 * n,)), pltpu.SemaphoreType.DMA((6 * n,))]),
        input_output_aliases={2 + a: 2 + a for a in range(n)},
        compiler_params=_cp(("arbitrary",), VMEM_CAP, side=True),
    )(tab, _in_hbm(h), *fulls)
    return outs[0], outs[1], outs[2:]


def _bias_tables(rel_bias, buckets):
    def body(tab_ref, bk_ref, o_ref):
        a = lax.broadcasted_iota(jnp.int32, (BLK, 2 * BLK), 0)
        b = lax.broadcasted_iota(jnp.int32, (BLK, 2 * BLK), 1)
        steps = a + BLK - b
        valid = (steps >= 0) & (steps <= BLK)
        for g in range(3):
            bk = bk_ref[g]
            for j in range(4):
                def pick(kk, acc, bk=bk, col=4 * g + j):
                    return jnp.where(bk == kk, tab_ref[kk, col], acc)

                acc = lax.fori_loop(0, N_BUCKETS, pick, jnp.zeros((BLK, 2 * BLK), F32))
                o_ref[g, j] = jnp.where(valid, acc, NEG)

    return pl.pallas_call(
        body, name="bias_tables", out_shape=SDS((3, 4, BLK, 2 * BLK), F32),
        in_specs=[pl.BlockSpec(memory_space=pltpu.SMEM), VMEM_SPEC], out_specs=VMEM_SPEC,
    )(rel_bias, buckets)


def _bias_grad(ds_sum, buckets):
    def body(ds_ref, bk_ref, o_ref):
        lane = lax.broadcasted_iota(jnp.int32, (1, 128), 1)
        for g in range(3):
            def bucket(kk, carry, g=g):
                row = jnp.zeros((1, 128), F32)
                for j in range(4):
                    v = jnp.where(bk_ref[g] == kk, ds_ref[g, j], 0.0)
                    s = jnp.sum(jnp.sum(v, axis=1, keepdims=True), axis=0, keepdims=True)
                    row = jnp.where(lane == j, s, row)
                o_ref[g, pl.ds(kk, 1), :] = row
                return carry

            lax.fori_loop(0, N_BUCKETS, bucket, 0)

    return pl.pallas_call(body, name="bias_grad", out_shape=SDS((3, N_BUCKETS, 128), F32), in_specs=[VMEM_SPEC, VMEM_SPEC],
                          out_specs=VMEM_SPEC)(ds_sum, buckets)


def _sub_rows(d, r, first, size):
    return pl.ds(first * d + r, size) if d == 1 else pl.ds(first * d + r, size, stride=d)


def _head_spec(seq, g, part):
    return pl.BlockSpec((seq, HD), lambda b, hh: (b, part * (QW // HD) + 4 * g + hh))


def _by_subsequence(dst, src, d, wide=None):
    seq = src.shape[0]
    if wide is not None:
        wide[...] = src[...].astype(F32)
        src = wide
    for r in range(d):
        for first in range(0, seq // d, BLK):
            dst[pl.ds(r * (seq // d) + first, BLK), :] = src[_sub_rows(d, r, first, BLK), :].astype(dst.dtype)


def _attn_forward(g, qkv, bias, bsz, seq):
    d = DILATIONS[g]
    ln = seq // d
    units = [(r, n) for r in range(d) for n in range(ln // BLK)]

    def band(n):
        return slice(BLK, 2 * BLK) if n == 0 else slice(0, 2 * BLK)

    def body(q_ref, k_ref, v_ref, b_ref, o_ref, l_ref, *scratch):
        hs = pl.program_id(1)
        if d == 1:
            (s_scr, p_scr), (qd, kd, vd) = scratch, (q_ref, k_ref, v_ref)
        else:
            wide, qd, kd, vd, s_scr, p_scr = scratch
            for dst, src in ((qd, q_ref), (kd, k_ref), (vd, v_ref)):
                _by_subsequence(dst, src, d, wide)
        blk = lambda r, n: pl.ds(r * ln + n * BLK, BLK)
        for u, (r, n) in enumerate(units):
            s_scr[u, :, BLK:] = _dot_nt(qd[blk(r, n), :], kd[blk(r, n), :])
            if n > 0:
                s_scr[u, :, :BLK] = _dot_nt(qd[blk(r, n), :], kd[blk(r, n - 1), :])
        for u, (r, n) in enumerate(units):
            s = s_scr[u, :, band(n)] * SCALE + b_ref[hs, :, band(n)]
            m = jnp.max(s, axis=1, keepdims=True)
            e = jnp.exp(s - m)
            den = jnp.sum(e, axis=1, keepdims=True)
            p_scr[u, :, band(n)] = (e * (1.0 / den)).astype(BF16)
            l_ref[_sub_rows(d, r, n * BLK, BLK), :] = jnp.broadcast_to(m + jnp.log(den), (BLK, HD))
        for u, (r, n) in enumerate(units):
            acc = _dot(p_scr[u, :, BLK:], vd[blk(r, n), :])
            if n > 0:
                acc = acc + _dot(p_scr[u, :, :BLK], vd[blk(r, n - 1), :])
            o_ref[_sub_rows(d, r, n * BLK, BLK), :] = acc

    regrouped = [] if d == 1 else [pltpu.VMEM((seq, HD), F32)] + [pltpu.VMEM((seq, HD), BF16)] * 3
    out_spec = pl.BlockSpec((seq, HD), lambda b, hh: (b, hh))
    return pl.pallas_call(
        body, name=f"attn_forward_{g}", out_shape=[HBM_OUT((bsz * seq, AW), F32)] * 2, grid=(bsz, 4),
        in_specs=[_head_spec(seq, g, part) for part in range(3)] + [pl.BlockSpec((4, BLK, 2 * BLK), lambda b, hh: (0, 0, 0))],
        out_specs=[out_spec, out_spec],
        scratch_shapes=regrouped + [pltpu.VMEM((len(units), BLK, 2 * BLK), F32), pltpu.VMEM((len(units), BLK, 2 * BLK), BF16)],
        compiler_params=_cp(("parallel", "parallel"), VMEM_CAP // 2),
    )(qkv, qkv, qkv, _in_hbm(bias))


def _attn_backward(g, qkv, do, dl, bias, prev_out, bsz, seq):
    d = DILATIONS[g]
    ln = seq // d
    units = [(r, n) for r in range(d) for n in range(ln // BLK)]

    def body(q_ref, k_ref, v_ref, do_ref, dl_ref, b_ref, *rest):
        dq_ref, dk_ref, dv_ref, db_ref = rest[-17:-13]
        wide, qd, kd, vd, dod, dld, dqd, dkd, dvd, s_scr, dp_scr, p_scr, ds_scr = rest[-13:]
        hs = pl.program_id(1)

        @pl.when((pl.program_id(0) == 0) & (hs == 0))
        def _():
            db_ref[...] = jnp.zeros_like(db_ref)

        for dst, src in ((qd, q_ref), (kd, k_ref), (vd, v_ref)):
            _by_subsequence(dst, src, d, wide)
        _by_subsequence(dod, do_ref, d)
        _by_subsequence(dld, dl_ref, d)
        dkd[...] = jnp.zeros_like(dkd)
        dvd[...] = jnp.zeros_like(dvd)
        blk = lambda r, n: pl.ds(r * ln + n * BLK, BLK)
        keys = lambda r, n: [(blk(r, n), slice(BLK, 2 * BLK))] + ([(blk(r, n - 1), slice(0, BLK))] if n > 0 else [])
        for u, (r, n) in enumerate(units):
            for rows, band in keys(r, n):
                s_scr[u, :, band] = _dot_nt(qd[blk(r, n), :], kd[rows, :])
                dp_scr[u, :, band] = _dot_nt(dod[blk(r, n), :], vd[rows, :])
        for u, (r, n) in enumerate(units):
            both = dld[blk(r, n), :]
            lse, delta = both[:, 0:1], both[:, 64:65]
            band = slice(BLK, 2 * BLK) if n == 0 else slice(0, 2 * BLK)
            p = jnp.exp(s_scr[u, :, band] * SCALE + b_ref[hs, :, band] - lse)
            ds = p * (dp_scr[u, :, band] - delta)
            p_scr[u, :, band] = p.astype(BF16)
            ds_scr[u, :, band] = ds.astype(BF16)
            db_ref[hs, :, band] += ds
        for u, (r, n) in enumerate(units):
            dq = jnp.zeros((BLK, HD), F32)
            for rows, band in keys(r, n):
                dvd[rows, :] += _dot_tn(p_scr[u, :, band], dod[blk(r, n), :])
                dkd[rows, :] += _dot_tn(ds_scr[u, :, band], qd[blk(r, n), :]) * SCALE
                dq = dq + _dot(ds_scr[u, :, band], kd[rows, :])
            dqd[blk(r, n), :] = dq * SCALE
        for out, acc in ((dq_ref, dqd), (dk_ref, dkd), (dv_ref, dvd)):
            if d == 1:
                out[...] = acc[...].astype(BF16)
            else:
                for r in range(d):
                    for first in range(0, ln, BLK):
                        wide[_sub_rows(d, r, first, BLK), :] = acc[pl.ds(r * ln + first, BLK), :]
                out[...] = wide[...].astype(BF16)

    qkv_spec = _head_spec(seq, g, 0)
    out_spec = pl.BlockSpec((seq, HD), lambda b, hh: (b, hh))
    band_spec = pl.BlockSpec((4, BLK, 2 * BLK), lambda b, hh: (0, 0, 0))
    ins = [qkv, qkv, qkv, _in_hbm(do), _in_hbm(dl), _in_hbm(bias)]
    in_specs = [_head_spec(seq, g, part) for part in range(3)] + [out_spec, out_spec, band_spec]
    aliases = {}
    if prev_out is not None:
        ins += list(prev_out)
        in_specs += [ANY] * 3
        aliases = {6: 0, 7: 1, 8: 2}
    rows_bf16, rows_f32 = pltpu.VMEM((seq, HD), BF16), pltpu.VMEM((seq, HD), F32)
    staged = [pltpu.VMEM((len(units), BLK, 2 * BLK), F32)] * 2 + [pltpu.VMEM((len(units), BLK, 2 * BLK), BF16)] * 2
    dq, dk, dv, db = pl.pallas_call(
        body, name=f"attn_backward_{g}", out_shape=[HBM_OUT((bsz * seq, QW), BF16)] * 3 + [SDS((4, BLK, 2 * BLK), F32)], grid=(bsz, 4),
        in_specs=in_specs, out_specs=[qkv_spec] * 3 + [band_spec], input_output_aliases=aliases,
        scratch_shapes=[rows_f32] + [rows_bf16] * 4 + [rows_f32] * 4 + staged,
        compiler_params=_cp(("arbitrary", "arbitrary"), VMEM_CAP // 2),
    )(*ins)
    return (dq, dk, dv), db


def _mix_forward(gates, og, lg, x2, tgt, gate, w_ao, w_co, w_o, conv_w, conv_b, ln_g, ln_b, bsz, seq, tm=256):
    t = x2.shape[0]
    spt = seq // tm

    def body(g_ref, o1, o2, o3, l1, l2, l3, x_ref, t_ref, gate_ref, wao_ref, wco_ref, wo_ref, cw_ref, cb_ref, lng_ref, lnb_ref,
             ain_ref, sin_ref, mrg_ref, dy_ref, aout_ref, sout_ref, yc_ref, o_ref, lj_ref, dxr_ref, vec_ref, dgate_ref, zc_ref):
        b, i = pl.program_id(0), pl.program_id(1)

        @pl.when((b == 0) & (i == 0))
        def _():
            vec_ref[...] = jnp.zeros_like(vec_ref)

        @pl.when(i == 0)
        def _():
            zc_ref[...] = jnp.zeros_like(zc_ref)
            dgate_ref[...] = jnp.zeros_like(dgate_ref)

        g_attn, u, bg, cg, g_conv, m_attn, m_conv = (g_ref[:, lo:hi].astype(F32) for lo, hi in GATE_COLS)
        la, lb, lc = l1[...], l2[...], l3[...]
        mx = jnp.maximum(la, jnp.maximum(lb, lc))
        ea, eb, ec = jnp.exp(la - mx), jnp.exp(lb - mx), jnp.exp(lc - mx)
        den = ea + eb + ec
        o = (ea * o1[...] + eb * o2[...] + ec * o3[...]) / den
        o_ref[...] = o
        lj_ref[...] = mx + jnp.log(den)
        a_in = o * (g_attn * _sig(g_attn))
        ain_ref[...] = a_in.astype(BF16)
        a_out = _dot(a_in.astype(BF16), wao_ref[...])
        aout_ref[...] = a_out.astype(BF16)
        z = cg * u
        rows = lax.broadcasted_iota(jnp.int32, (tm, D), 0)
        c6, c7 = zc_ref[6:7, :], zc_ref[7:8, :]
        z1 = jnp.where(rows == 0, c7, pltpu.roll(z, 1, 0))
        z2 = jnp.where(rows == 0, c6, jnp.where(rows == 1, c7, pltpu.roll(z, 2, 0)))
        zc_ref[...] = z[tm - 8:tm, :]
        y_conv = (cw_ref[0:1, :] * z2 + cw_ref[1:2, :] * z1 + cw_ref[2:3, :] * z) + cb_ref[...]
        yc_ref[...] = y_conv.astype(BF16)
        s_in = bg * y_conv * (g_conv * _sig(g_conv))
        sin_ref[...] = s_in.astype(BF16)
        s_out = _dot(s_in.astype(BF16), wco_ref[...])
        sout_ref[...] = s_out.astype(BF16)
        merged = _sig(m_attn) * a_out + _sig(m_conv) * s_out
        mrg_ref[...] = merged.astype(BF16)
        y = _dot(merged.astype(BF16), wo_ref[...])
        gate1 = 1.0 + gate_ref[0]
        r = ALPHA * x_ref[...] + gate1 * y
        mu = jnp.mean(r, axis=1, keepdims=True)
        rc = r - mu
        rstd = lax.rsqrt(jnp.mean(rc * rc, axis=1, keepdims=True) + LN_EPS)
        xhat = rc * rstd
        diff = (xhat * lng_ref[...] + lnb_ref[...]) - t_ref[...]
        dout = diff * (1.0 / D)
        vec_ref[0:1, :] += jnp.sum(dout * xhat, axis=0, keepdims=True)
        vec_ref[1:2, :] += jnp.sum(dout, axis=0, keepdims=True)
        vec_ref[2:3, :] += jnp.sum(diff * diff, axis=0, keepdims=True)
        dxh = dout * lng_ref[...]
        dr = rstd * (dxh - jnp.mean(dxh, axis=1, keepdims=True) - xhat * jnp.mean(dxh * xhat, axis=1, keepdims=True))
        dxr_ref[...] = ALPHA * dr
        dy_ref[...] = (dr * gate1).astype(BF16)
        dgate_ref[0] += jnp.sum(dr * y, axis=0, keepdims=True)

    tok = lambda w: pl.BlockSpec((tm, w), lambda b, i: (b * spt + i, 0))
    const = lambda s: pl.BlockSpec(s, lambda b, i: (0,) * len(s))
    per_seq = pl.BlockSpec((1, 1, D), lambda b, i: (b, 0, 0))
    outs = pl.pallas_call(
        body, name="mix_forward", grid=(bsz, spt),
        out_shape=[HBM_OUT((t, AW), BF16), HBM_OUT((t, D), BF16), HBM_OUT((t, D), BF16), HBM_OUT((t, D), BF16), HBM_OUT((t, D), BF16),
                   HBM_OUT((t, D), BF16), HBM_OUT((t, D), BF16), HBM_OUT((t, AW), F32), HBM_OUT((t, AW), F32), HBM_OUT((t, D), F32),
                   SDS((8, D), F32), SDS((bsz, 1, D), F32)],
        in_specs=[tok(NGATE)] + [tok(AW)] * 6 + [tok(D), tok(D), per_seq, const((AW, D)), const((D, D)), const((D, D)),
                                                 const((3, D)), const((1, D)), const((1, D)), const((1, D))],
        out_specs=[tok(AW), tok(D), tok(D), tok(D), tok(D), tok(D), tok(D), tok(AW), tok(AW), tok(D), const((8, D)), per_seq],
        scratch_shapes=[pltpu.VMEM((8, D), F32)],
        compiler_params=_cp(("arbitrary", "arbitrary"), VMEM_CAP),
    )(gates, *map(_in_hbm, og), *map(_in_hbm, lg), x2, tgt, gate, w_ao, w_co, w_o, conv_w, conv_b, ln_g, ln_b)
    return outs


def _mix_backward(gates, dy, a_out, s_out, y_conv, o, lj, w_ao, w_co, w_o, conv_w, bsz, seq, tm=256):
    t = dy.shape[0]
    spt = seq // tm

    def body(g_ref, dy_ref, aout_ref, sout_ref, yc_ref, o_ref, lj_ref, wao_ref, wco_ref, wo_ref, cw_ref,
             dg_ref, do_ref, dl_ref, daout_ref, dsout_ref, vec_ref, car_ref):
        b, i = pl.program_id(0), pl.program_id(1)

        @pl.when((b == 0) & (i == 0))
        def _():
            vec_ref[...] = jnp.zeros_like(vec_ref)

        @pl.when(i == 0)
        def _():
            car_ref[...] = jnp.zeros_like(car_ref)

        g_attn, u, bg, cg, g_conv, m_attn, m_conv = (g_ref[:, lo:hi].astype(F32) for lo, hi in GATE_COLS)
        dmerged = _dot_nt(dy_ref[...], wo_ref[...])
        sa, sc = _sig(m_attn), _sig(m_conv)
        da_out = (dmerged * sa).astype(BF16)
        ds_out = (dmerged * sc).astype(BF16)
        daout_ref[...] = da_out
        dsout_ref[...] = ds_out
        dg_ref[:, 4608:5632] = (dmerged * aout_ref[...].astype(F32) * (sa * (1.0 - sa))).astype(BF16)
        dg_ref[:, 5632:6656] = (dmerged * sout_ref[...].astype(F32) * (sc * (1.0 - sc))).astype(BF16)
        da_in = _dot_nt(da_out, wao_ref[...])
        ds_in = _dot_nt(ds_out, wco_ref[...])
        sga = _sig(g_attn)
        o = o_ref[...]
        do = da_in * (g_attn * sga)
        do_ref[...] = do
        dg_ref[:, 0:512] = (da_in * o * (sga * (1.0 + g_attn * (1.0 - sga)))).astype(BF16)
        prod = do * o
        lane = lax.broadcasted_iota(jnp.int32, (tm, HD), 1)
        for j in range(4):
            cs = slice(j * HD, (j + 1) * HD)
            delta = jnp.sum(prod[:, cs], axis=1, keepdims=True)
            dl_ref[:, cs] = jnp.where(lane < 64, lj_ref[:, cs], delta)
        sgc = _sig(g_conv)
        silu_c = g_conv * sgc
        yc = yc_ref[...].astype(F32)
        dg_ref[:, 1536:2560] = (ds_in * yc * silu_c).astype(BF16)
        dg_ref[:, 3584:4608] = (ds_in * bg * yc * (sgc * (1.0 + g_conv * (1.0 - sgc)))).astype(BF16)
        dyc = ds_in * bg * silu_c
        rows = lax.broadcasted_iota(jnp.int32, (tm, D), 0)
        c0, c1 = car_ref[0:1, :], car_ref[1:2, :]
        n1 = jnp.where(rows == tm - 1, c0, pltpu.roll(dyc, tm - 1, 0))
        n2 = jnp.where(rows == tm - 2, c0, jnp.where(rows == tm - 1, c1, pltpu.roll(dyc, tm - 2, 0)))
        car_ref[...] = dyc[0:8, :]
        dz = cw_ref[2:3, :] * dyc + cw_ref[1:2, :] * n1 + cw_ref[0:1, :] * n2
        z = cg * u
        dg_ref[:, 512:1536] = (dz * cg).astype(BF16)
        dg_ref[:, 2560:3584] = (dz * u).astype(BF16)
        vec_ref[0:1, :] += jnp.sum(n2 * z, axis=0, keepdims=True)
        vec_ref[1:2, :] += jnp.sum(n1 * z, axis=0, keepdims=True)
        vec_ref[2:3, :] += jnp.sum(dyc * z, axis=0, keepdims=True)
        vec_ref[3:4, :] += jnp.sum(dyc, axis=0, keepdims=True)

    tok = lambda w: pl.BlockSpec((tm, w), lambda b, i: (b * spt + (spt - 1 - i), 0))
    const = lambda s: pl.BlockSpec(s, lambda b, i: (0,) * len(s))
    return pl.pallas_call(
        body, name="mix_backward", grid=(bsz, spt),
        out_shape=[HBM_OUT((t, NGATE), BF16), HBM_OUT((t, AW), F32), HBM_OUT((t, AW), F32), HBM_OUT((t, D), BF16), HBM_OUT((t, D), BF16),
                   SDS((8, D), F32)],
        in_specs=[tok(NGATE), tok(D), tok(D), tok(D), tok(D), tok(AW), tok(AW), const((AW, D)), const((D, D)), const((D, D)), const((3, D))],
        out_specs=[tok(NGATE), tok(AW), tok(AW), tok(D), tok(D), const((8, D))],
        scratch_shapes=[pltpu.VMEM((8, D), F32)],
        compiler_params=_cp(("arbitrary", "arbitrary"), VMEM_CAP),
    )(gates, dy, a_out, s_out, y_conv, o, lj, w_ao, w_co, w_o, conv_w)


def _out_weight_grads(a_in, da_out, s_in, ds_out, merged, dy, tk=512):
    t = dy.shape[0]

    def body(ain_ref, da_ref, sin_ref, ds_ref, m_ref, dy_ref, gao_ref, gco_ref, go_ref):
        @pl.when(pl.program_id(0) == 0)
        def _():
            gao_ref[...] = jnp.zeros_like(gao_ref)
            gco_ref[...] = jnp.zeros_like(gco_ref)
            go_ref[...] = jnp.zeros_like(go_ref)

        gao_ref[...] += _dot_tn(ain_ref[...], da_ref[...])
        gco_ref[...] += _dot_tn(sin_ref[...], ds_ref[...])
        go_ref[...] += _dot_tn(m_ref[...], dy_ref[...])

    tok = lambda w: pl.BlockSpec((tk, w), lambda i: (i, 0))
    const = lambda s: pl.BlockSpec(s, lambda i: (0, 0))
    return pl.pallas_call(
        body, name="out_weight_grads", grid=(t // tk,), out_shape=[SDS((AW, D), F32), SDS((D, D), F32), SDS((D, D), F32)],
        in_specs=[tok(AW), tok(D), tok(D), tok(D), tok(D), tok(D)], out_specs=[const((AW, D)), const((D, D)), const((D, D))],
        compiler_params=_cp(("arbitrary",), VMEM_CAP),
    )(a_in, da_out, s_in, ds_out, merged, dy)


def _input_grad(dq, dk, dv, dgates, w, x2, dxr, sc1p, seq, sums, tm=512):
    t = x2.shape[0]
    nt = t // tm
    spt = seq // tm
    bsz = t // seq
    n = len(sums)
    gblk = NGATE // 4
    nsteps = 3 + 4

    def body(dq_ref, dk_ref, dv_ref, dg_ref, wq_ref, wg_ref, x_ref, dxr_ref, sc_ref, *rest):
        src, (dx_ref, dsh_ref, dsc_ref), land = rest[:n], rest[n:n + 3], rest[n + 3:2 * n + 3]
        acc_ref, send_sems, recv_sems = rest[2 * n + 3:]
        j, i = pl.program_id(0), pl.program_id(1)
        px, py, pc = _place()
        chips = [(1 - px, py), (px, 1 - py), (1 - px, 1 - py)]
        copies = [pltpu.make_async_remote_copy(src_ref=src[a].at[2 * cx + cy], dst_ref=land[a].at[r], send_sem=send_sems.at[3 * a + r],
                                               recv_sem=recv_sems.at[3 * a + r], device_id=(cx, cy, pc), device_id_type=MESH)
                  for a in range(n) for r, (cx, cy) in enumerate(chips)]
        rows = pl.ds(pl.multiple_of(i * tm, tm), tm)

        @pl.when((i == 0) & (j == 0))
        def _():
            for cp in copies:
                cp.start()

        for k, ref in enumerate((dq_ref, dk_ref, dv_ref)):
            @pl.when(j == k)
            def _(k=k, ref=ref):
                part = _dot_nt(ref[...], wq_ref[...])
                if k == 0:
                    acc_ref[rows, :] = part
                else:
                    acc_ref[rows, :] += part

        @pl.when((j >= 3) & (j < nsteps - 1))
        def _():
            acc_ref[rows, :] += _dot_nt(dg_ref[...], wg_ref[...])

        @pl.when(j == nsteps - 1)
        def _():
            dh = acc_ref[rows, :] + _dot_nt(dg_ref[...], wg_ref[...])
            dx_ref[...] = dh * sc_ref[0] + dxr_ref[...]

            @pl.when(i % spt == 0)
            def _():
                dsh_ref[...] = jnp.zeros_like(dsh_ref)
                dsc_ref[...] = jnp.zeros_like(dsc_ref)

            dsh_ref[0] += jnp.sum(dh, axis=0, keepdims=True)
            dsc_ref[0] += jnp.sum(dh * x_ref[...], axis=0, keepdims=True)

        @pl.when((i == nt - 1) & (j == nsteps - 1))
        def _():
            for cp in copies:
                cp.wait()

    def held(k):
        return lambda j, i: (jnp.where(j == k, i, jnp.where(j < k, 0, nt - 1)), 0)

    last = lambda j, i: (jnp.where(j == nsteps - 1, i, 0), 0)
    outs = pl.pallas_call(
        body, name="input_grad", grid=(nsteps, nt),
        out_shape=[SDS((t, D), F32), SDS((bsz, 1, D), F32), SDS((bsz, 1, D), F32)] + [SDS((3,) + s.shape[1:], BF16) for s in sums],
        in_specs=[pl.BlockSpec((tm, QW), held(0)), pl.BlockSpec((tm, QW), held(1)), pl.BlockSpec((tm, QW), held(2)),
                  pl.BlockSpec((tm, gblk), lambda j, i: (jnp.where(j >= 3, i, 0), jnp.clip(j - 3, 0, 3))),
                  pl.BlockSpec((D, QW), lambda j, i: (0, jnp.minimum(j, 2))),
                  pl.BlockSpec((pl.Element(D), pl.Element(gblk)), lambda j, i: (0, pl.multiple_of(3 * QW + gblk * jnp.clip(j - 3, 0, 3), 128))),
                  pl.BlockSpec((tm, D), last), pl.BlockSpec((tm, D), last),
                  pl.BlockSpec((1, 1, D), lambda j, i: (jnp.where(j == nsteps - 1, i // spt, 0), 0, 0))] + [ANY] * n,
        out_specs=[pl.BlockSpec((tm, D), last),
                   pl.BlockSpec((1, 1, D), lambda j, i: (jnp.where(j == nsteps - 1, i // spt, 0), 0, 0)),
                   pl.BlockSpec((1, 1, D), lambda j, i: (jnp.where(j == nsteps - 1, i // spt, 0), 0, 0))] + [ANY] * n,
        scratch_shapes=[pltpu.VMEM((t, D), F32), pltpu.SemaphoreType.DMA((3 * NCHIP,)), pltpu.SemaphoreType.DMA((3 * NCHIP,))],
        compiler_params=_cp(("arbitrary", "arbitrary"), VMEM_CAP, side=True),
    )(dq, dk, dv, dgates, w, w, x2, dxr, sc1p, *sums)
    return outs[0], outs[1], outs[2], outs[3:]


def _in_weight_grad(ht, src, col0, prev, name):
    t = ht.shape[1]
    ncols = src.shape[1] // TN

    def body(ht_ref, s_ref, *rest):
        rest[-1][...] = _dot(ht_ref[...], s_ref[...].astype(BF16))

    ins = [ht, src]
    in_specs = [pl.BlockSpec((D, t), lambda j: (0, 0)), pl.BlockSpec((t, TN), lambda j: (0, j))]
    aliases = {}
    if prev is not None:
        ins.append(prev)
        in_specs.append(ANY)
        aliases = {2: 0}
    return pl.pallas_call(
        body, name=name, grid=(ncols,), out_shape=SDS((D, NCOL), F32), in_specs=in_specs,
        out_specs=pl.BlockSpec((D, TN), lambda j: (0, col0 + j)), input_output_aliases=aliases,
        compiler_params=_cp(("arbitrary",), VMEM_CAP),
    )(*ins)


def _sum_partials(gathered):
    def body(g_ref, o_ref):
        acc = g_ref[0]
        for k in range(1, 8):
            acc = acc + g_ref[k]
        o_ref[...] = acc

    return pl.pallas_call(body, name="sum_partials", out_shape=SDS(gathered.shape[1:], F32), in_specs=[VMEM_SPEC], out_specs=VMEM_SPEC)(gathered)


def _adamw(w, g, m, v, name, tr=256):
    r, cdim = w.shape
    tr = tr if cdim <= D else tr // 2
    tr = tr if (r % tr == 0 and r > tr) else r

    def body(w_ref, g_ref, m_ref, v_ref, d_ref, nm_ref, nv_ref):
        gv = g_ref[...]
        nm = B1 * m_ref[...] + (1.0 - B1) * gv
        nv = B2 * v_ref[...] + (1.0 - B2) * (gv * gv)
        m_hat = nm / (1.0 - B1 ** STEP)
        v_hat = nv / (1.0 - B2 ** STEP)
        d_ref[...] = -LR * (m_hat / (jnp.sqrt(v_hat) + EPS) + WD * w_ref[...])
        nm_ref[...] = nm
        nv_ref[...] = nv

    spec = pl.BlockSpec((tr, cdim), lambda i: (i, 0))
    return pl.pallas_call(
        body, name=name, grid=(r // tr,), out_shape=[SDS((r, cdim), F32)] * 3, in_specs=[spec] * 4, out_specs=[spec] * 3,
        compiler_params=_cp(("parallel",), VMEM_CAP // 2),
    )(w, g, m, v)


def _t5_bucket(dist):
    n = jnp.maximum(dist, 1).astype(F32)
    large = MAX_EXACT + (jnp.log(n / MAX_EXACT) / math.log(MAX_DISTANCE / MAX_EXACT) * (N_BUCKETS - MAX_EXACT)).astype(jnp.int32)
    large = jnp.minimum(large, N_BUCKETS - 1)
    return jnp.where(dist < MAX_EXACT, dist, large)


def _band_buckets():
    a = jnp.arange(BLK)[:, None]
    b = jnp.arange(2 * BLK)[None, :]
    steps = jnp.maximum(a + BLK - b, 0)
    return jnp.stack([_t5_bucket(steps * d) for d in DILATIONS]).astype(jnp.int32)


def _pad_rows(a, rows=8):
    return jnp.pad(a, ((0, rows - a.shape[0]), (0, 0)))


def kernel(x, c, w_ada, b_ada, w_in, conv_w, conv_b, rel_bias, w_attn_out, w_conv_out, w_o, ln_g, ln_b, loss_target, m_w_ada, m_b_ada, m_w_in, m_conv_w, m_conv_b, m_rel_bias, m_w_attn_out, m_w_conv_out, m_w_o, m_ln_g, m_ln_b, v_w_ada, v_b_ada, v_w_in, v_conv_w, v_conv_b, v_rel_bias, v_w_attn_out, v_w_conv_out, v_w_o, v_ln_g, v_ln_b):
    bsz, seq, _ = x.shape
    t = bsz * seq
    mx, my, mc = _place()
    chip = 2 * mx + my
    dev = 4 * mx + 2 * my + mc
    x2 = x.reshape(t, D)
    tgt = loss_target.reshape(t, D)

    mine = [_to_bf16_window(a, w[0], f"to_bf16_{a}") for a, w in enumerate((w_in, w_attn_out, w_conv_out, w_o))]

    n_ada = w_ada.shape[2]
    n_cw = conv_w.shape[2]
    c_and_cw = jnp.concatenate([_pad_rows(c), jnp.pad(conv_w[0], ((0, 5), (0, D - n_cw)))], axis=0)
    firsts = _all_gather8(c_and_cw, "gather_c_conv_w")
    c_all = firsts[:, 0:bsz, :].reshape(8 * bsz, D)
    conv_w_f = firsts[0::2, 8:11, 0:n_cw].transpose(1, 0, 2).reshape(3, D)
    b_cols = lax.dynamic_slice(b_ada, (0, chip * n_ada), (1, n_ada))
    mod_part = _ada_forward(c_all, w_ada[0], b_cols)
    mod_parts = _all_gather8(mod_part, "gather_mod")
    mod_all = mod_parts[0::2].transpose(1, 0, 2).reshape(8 * bsz, 3 * D)
    mod = lax.dynamic_slice(mod_all, (dev * bsz, 0), (bsz, 3 * D))
    shift = mod[:, 0:D].reshape(bsz, 1, D)
    sc1p = 1.0 + mod[:, D:2 * D].reshape(bsz, 1, D)
    gate = mod[:, 2 * D:].reshape(bsz, 1, D)

    h, ht = _modulate(x2, sc1p, shift, seq)
    tab = lax.dynamic_index_in_dim(jnp.asarray(_tile_tables()), chip, 0, keepdims=False)
    qkv, gates, (w_in_f, w_ao_f, w_co_f, w_o_f) = _project_gather(h, mine, tab)
    buckets = _band_buckets()
    bias = _bias_tables(rel_bias, buckets)
    og, lg = [], []
    for g in range(3):
        o_g, l_g = _attn_forward(g, qkv, bias[g], bsz, seq)
        og.append(o_g)
        lg.append(l_g)
    (a_in, s_in, merged, dy, a_out, s_out, y_conv, o, lj, dxr, vec_f, dgate) = _mix_forward(
        gates, og, lg, x2, tgt, gate, w_ao_f, w_co_f, w_o_f, conv_w_f, conv_b, ln_g, ln_b, bsz, seq)

    dgates, do, dl, da_out, ds_out, vec_b = _mix_backward(gates, dy, a_out, s_out, y_conv, o, lj, w_ao_f, w_co_f, w_o_f, conv_w_f, bsz, seq)
    g_ao, g_co, g_o = _out_weight_grads(a_in, da_out, s_in, ds_out, merged, dy)
    dqkv, dbs = None, []
    for g in range(3):
        dqkv, db = _attn_backward(g, qkv, do, dl, bias[g], dqkv, bsz, seq)
        dbs.append(db)
    dq, dk, dv = dqkv
    drb = _bias_grad(jnp.stack(dbs), buckets)
    drb = drb[:, :, 0:4].transpose(1, 0, 2).reshape(N_BUCKETS, 12)
    g_in = None
    for n, src in enumerate((dq, dk, dv, dgates)):
        g_in = _in_weight_grad(ht, src, n * NQT, g_in, f"in_weight_grad_{n}")

    grads = [g_in, g_ao, g_co, g_o]
    got = _swap_halves(grads)
    sums = [_chip_sum(a, grads[a], got[a], f"chip_sum_{a}") for a in range(4)]
    grad_x, dshift, dscale, landed = _input_grad(dq, dk, dv, dgates, w_in_f, x2, dxr, sc1p, seq, [s[1] for s in sums])
    halves = [_reduce_mine(a, sums[a][0], landed[a], f"reduce_mine_{a}") for a in range(4)]
    gw_in, gw_ao, gw_co, gw_o = _join_halves(halves)

    dmod = jnp.concatenate([dshift, dscale, dgate], axis=2).reshape(bsz * 3, D)
    drb_row = jnp.pad(drb.reshape(1, N_BUCKETS * 12), ((0, 0), (0, D - N_BUCKETS * 12)))
    packed = jnp.concatenate([vec_f, vec_b, _pad_rows(dmod), _pad_rows(drb_row)], axis=0)
    gathered = _all_gather8(packed, "gather_small")
    small = _sum_partials(gathered)
    g_ln_g, g_ln_b, loss_lanes = small[0:1], small[1:2], small[2:3]
    g_conv_w_full, g_conv_b = small[8:11], small[11:12]
    g_rel_bias = small[24, 0:N_BUCKETS * 12].reshape(N_BUCKETS, 12)
    loss = 0.5 / D * jnp.sum(loss_lanes)
    dmod_all = gathered[:, 16:16 + 3 * bsz, :].reshape(8 * bsz, 3 * D)
    dmod_cols = lax.dynamic_slice(dmod_all, (0, chip * n_ada), (8 * bsz, n_ada))
    gw_ada, gb_ada = _ada_backward(c_all, dmod_cols, dmod_all)
    g_conv_w = lax.dynamic_slice(g_conv_w_full, (0, chip * n_cw), (3, n_cw))

    names = ["w_ada", "b_ada", "w_in", "conv_w", "conv_b", "rel_bias", "w_attn_out", "w_conv_out", "w_o", "ln_g", "ln_b"]
    two_d = lambda a: a.reshape(a.shape[-2:]) if a.ndim == 3 else a
    weights = dict(zip(names, map(two_d, (w_ada, b_ada, w_in, conv_w, conv_b, rel_bias, w_attn_out, w_conv_out, w_o, ln_g, ln_b))))
    ms = dict(zip(names, map(two_d, (m_w_ada, m_b_ada, m_w_in, m_conv_w, m_conv_b, m_rel_bias, m_w_attn_out, m_w_conv_out, m_w_o, m_ln_g, m_ln_b))))
    vs = dict(zip(names, map(two_d, (v_w_ada, v_b_ada, v_w_in, v_conv_w, v_conv_b, v_rel_bias, v_w_attn_out, v_w_conv_out, v_w_o, v_ln_g, v_ln_b))))
    grads = dict(zip(names, (gw_ada, gb_ada, gw_in, g_conv_w, g_conv_b, g_rel_bias, gw_ao, gw_co, gw_o, g_ln_g, g_ln_b)))
    shapes = dict(zip(names, (w_ada, b_ada, w_in, conv_w, conv_b, rel_bias, w_attn_out, w_conv_out, w_o, ln_g, ln_b)))
    deltas, new_m, new_v = {}, {}, {}
    for n in names:
        deltas[n], new_m[n], new_v[n] = _adamw(weights[n], grads[n], ms[n], vs[n], f"adamw_{n}")
    shaped = lambda d: [d[n].reshape(shapes[n].shape) for n in names]
    return (loss, grad_x.reshape(bsz, seq, D), *shaped(grads), *shaped(deltas), *shaped(new_m), *shaped(new_v))
```

```python
import math

import numpy as np
import jax
import jax.numpy as jnp
from jax import lax
from jax.experimental import pallas as pl
from jax.experimental.pallas import tpu as pltpu

F32 = jnp.float32
BF16 = jnp.bfloat16
SDS = jax.ShapeDtypeStruct
MESH = pl.DeviceIdType.MESH
HBM_OUT = pltpu.HBM
ANY = pl.BlockSpec(memory_space=pl.ANY)
VMEM_SPEC = pl.BlockSpec(memory_space=pltpu.VMEM)

D = 1024
HD = 128
BLK = 128
QW = 1536
AW = 512
NGATE = 6656
GATE_COLS = ((0, 512), (512, 1536), (1536, 2560), (2560, 3584), (3584, 4608), (4608, 5632), (5632, 6656))
NCOL = 3 * QW + NGATE
TN = 512
NQT = QW // TN
NPT = NCOL // TN
DILATIONS = (1, 4, 16)
N_BUCKETS, MAX_EXACT, MAX_DISTANCE = 32, 16, 2048
ALPHA = 2.0 ** 0.25
LN_EPS = 1e-5
NEG = -1e30
SCALE = HD ** -0.5
LR, B1, B2, EPS, WD, STEP = 0.001, 0.9, 0.999, 1e-08, 0.01, 10
NCHIP = 4
VMEM_CAP = 60 * 2 ** 20


def _cp(sem=None, vmem=None, side=False):
    return pltpu.CompilerParams(dimension_semantics=sem, vmem_limit_bytes=vmem, has_side_effects=side)


def _dot(a, b):
    return jnp.dot(a, b, preferred_element_type=F32)


def _dot_nt(a, b):
    return lax.dot_general(a, b, (((1,), (1,)), ((), ())), preferred_element_type=F32)


def _dot_tn(a, b):
    return lax.dot_general(a, b, (((0,), (0,)), ((), ())), preferred_element_type=F32)


def _sig(x):
    return 1.0 / (1.0 + jnp.exp(-x))


def _in_hbm(a):
    return pltpu.with_memory_space_constraint(a, pltpu.HBM)


def _place():
    x, y, c = lax.axis_index("x"), lax.axis_index("y"), lax.axis_index("c")
    return x, y, c


def _all_gather8(v, name):
    r, cdim = v.shape

    def body(v_ref, out_ref, send_sems, recv_sems, local_sem):
        x, y, c = _place()
        me = 4 * x + 2 * y + c
        peers = [(x, y, 1 - c), (1 - x, y, c), (x, 1 - y, c), (1 - x, 1 - y, c),
                 (1 - x, y, 1 - c), (x, 1 - y, 1 - c), (1 - x, 1 - y, 1 - c)]
        mine = pltpu.make_async_copy(v_ref, out_ref.at[me], local_sem)
        mine.start()

        def copy(k, block, to):
            return pltpu.make_async_remote_copy(src_ref=v_ref, dst_ref=out_ref.at[block], send_sem=send_sems.at[k],
                                                recv_sem=recv_sems.at[k], device_id=to, device_id_type=MESH)

        sends = [copy(k, me, p) for k, p in enumerate(peers)]
        for cp in sends:
            cp.start()
        for k, (px, py, pc) in enumerate(peers):
            copy(k, 4 * px + 2 * py + pc, (px, py, pc)).wait_recv()
        for cp in sends:
            cp.wait_send()
        mine.wait()

    return pl.pallas_call(
        body, name=name, out_shape=SDS((8, r, cdim), v.dtype), in_specs=[VMEM_SPEC], out_specs=VMEM_SPEC,
        scratch_shapes=[pltpu.SemaphoreType.DMA((7,)), pltpu.SemaphoreType.DMA((7,)), pltpu.SemaphoreType.DMA(())],
        compiler_params=_cp(side=True),
    )(v)


W_CUTS = (("col", D, NCOL // NCHIP), ("col", AW, D // NCHIP), ("row", D // NCHIP, D), ("row", D // NCHIP, D))
W_FULL = ((D, NCOL), (AW, D), (D, D), (D, D))


def _shard_window(ref, cut, k, half):
    kind, nr, nc = cut
    hr = nr // 2
    if kind == "col":
        rows = pl.ds(0, nr) if half is None else pl.ds(pl.multiple_of(half * hr, 16), hr)
        return ref.at[rows, pl.ds(pl.multiple_of(k * nc, 128), nc)]
    if half is None:
        return ref.at[pl.ds(pl.multiple_of(k * nr, 16), nr), :]
    return ref.at[pl.ds(pl.multiple_of(k * nr + half * hr, 16), hr), :]


def _half_rows(ref, cut, half):
    hr = cut[1] // 2
    return ref.at[pl.ds(pl.multiple_of(half * hr, 16), hr), :]


def _to_bf16_window(a, w, name):
    kind, nr, nc = W_CUTS[a]
    x, y, _ = _place()
    chip = jnp.reshape(2 * x + y, (1,)).astype(jnp.int32)
    tr = min(nr, 256)

    def body(c_ref, w_ref, o_ref):
        o_ref[...] = w_ref[...].astype(BF16)

    out_map = (lambda i, cr: (i, cr[0])) if kind == "col" else (lambda i, cr: (cr[0] * (nr // tr) + i, 0))
    return pl.pallas_call(
        body, name=name, out_shape=SDS(W_FULL[a], BF16),
        grid_spec=pltpu.PrefetchScalarGridSpec(num_scalar_prefetch=1, grid=(nr // tr,),
                                               in_specs=[pl.BlockSpec((tr, nc), lambda i, cr: (i, 0))], out_specs=pl.BlockSpec((tr, nc), out_map)),
        compiler_params=_cp(("arbitrary",)),
    )(chip, w)


def _swap_halves(grads):
    n = len(grads)
    shapes = []
    for a in range(n):
        kind, nr, nc = W_CUTS[a]
        shapes.append((W_FULL[a][0] // 2, W_FULL[a][1]) if kind == "col" else (NCHIP, nr // 2, nc))

    def pieces(a, ref, land, half):
        kind, nr, nc = W_CUTS[a]
        if kind == "col":
            hr = nr // 2
            return [(ref.at[pl.ds(pl.multiple_of(half * hr, 16), hr), :], land)]
        return [(_shard_window(ref, W_CUTS[a], k, half), land.at[k]) for k in range(NCHIP)]

    def body(*refs):
        src, land = refs[:n], refs[n:2 * n]
        send_sems, recv_sems = refs[2 * n:]
        x, y, c = _place()
        sibling = (x, y, 1 - c)
        sends = []
        k = 0
        for a in range(n):
            for s, d in pieces(a, src[a], land[a], 1 - c):
                cp = pltpu.make_async_remote_copy(src_ref=s, dst_ref=d, send_sem=send_sems.at[k], recv_sem=recv_sems.at[k],
                                                  device_id=sibling, device_id_type=MESH)
                cp.start()
                sends.append(cp)
                k += 1
        for cp in sends:
            cp.wait()

    n_sems = sum(1 if W_CUTS[a][0] == "col" else NCHIP for a in range(n))
    return pl.pallas_call(
        body, name="swap_grad_halves", out_shape=[SDS(s, F32) for s in shapes], in_specs=[ANY] * n, out_specs=[ANY] * n,
        scratch_shapes=[pltpu.SemaphoreType.DMA((n_sems,)), pltpu.SemaphoreType.DMA((n_sems,))],
        compiler_params=_cp(side=True),
    )(*grads)


def _chip_sum(a, grad, got, name):
    kind, nr, nc = W_CUTS[a]
    hr = nr // 2
    c = lax.axis_index("c")
    cidx = jnp.reshape(c, (1,)).astype(jnp.int32)

    def body(c_ref, g_ref, r_ref, f_ref, b_ref):
        s = g_ref[...] + r_ref[...]
        f_ref[...] = s.reshape(f_ref.shape)
        b_ref[...] = s.astype(BF16).reshape(b_ref.shape)

    if kind == "col":
        in_specs = [pl.BlockSpec((hr, nc), lambda k, cr: (cr[0], k)), pl.BlockSpec((hr, nc), lambda k, cr: (0, k))]
    else:
        grad = grad.reshape(NCHIP, 2, hr, nc)
        in_specs = [pl.BlockSpec((1, 1, hr, nc), lambda k, cr: (k, cr[0], 0, 0)), pl.BlockSpec((1, hr, nc), lambda k, cr: (k, 0, 0))]
    out_specs = [pl.BlockSpec((1, hr, nc), lambda k, cr: (k, 0, 0))] * 2
    return pl.pallas_call(
        body, name=name, out_shape=[SDS((NCHIP, hr, nc), F32), SDS((NCHIP, hr, nc), BF16)],
        grid_spec=pltpu.PrefetchScalarGridSpec(num_scalar_prefetch=1, grid=(NCHIP,), in_specs=in_specs, out_specs=out_specs),
        compiler_params=_cp(("arbitrary",), VMEM_CAP),
    )(cidx, grad, got)


def _reduce_mine(a, mine_f32, got, name):
    kind, nr, nc = W_CUTS[a]
    hr = nr // 2
    x, y, c = _place()
    where = jnp.stack([2 * x + y, c]).astype(jnp.int32)
    tr = min(hr, 256)

    def body(w_ref, m_ref, g_ref, o_ref):
        o_ref[...] = ((m_ref[0] + g_ref[0].astype(F32)) + g_ref[1].astype(F32)) + g_ref[2].astype(F32)

    return pl.pallas_call(
        body, name=name, out_shape=SDS((nr, nc), F32),
        grid_spec=pltpu.PrefetchScalarGridSpec(
            num_scalar_prefetch=1, grid=(hr // tr,),
            in_specs=[pl.BlockSpec((1, tr, nc), lambda i, wr: (wr[0], i, 0)), pl.BlockSpec((3, tr, nc), lambda i, wr: (0, i, 0))],
            out_specs=pl.BlockSpec((tr, nc), lambda i, wr: (wr[1] * (hr // tr) + i, 0))),
        compiler_params=_cp(("arbitrary",), VMEM_CAP),
    )(where, mine_f32, got)


def _join_halves(fulls):
    n = len(fulls)

    def body(*refs):
        full = refs[n:2 * n]
        send_sems, recv_sems = refs[2 * n:]
        x, y, c = _place()
        sibling = (x, y, 1 - c)

        def swap(a, half):
            rows = _half_rows(full[a], W_CUTS[a], half)
            return pltpu.make_async_remote_copy(src_ref=rows, dst_ref=rows, send_sem=send_sems.at[a], recv_sem=recv_sems.at[a],
                                                device_id=sibling, device_id_type=MESH)

        sends = [swap(a, c) for a in range(n)]
        for cp in sends:
            cp.start()
        for a, cp in enumerate(sends):
            cp.wait_send()
            swap(a, 1 - c).wait_recv()

    return pl.pallas_call(
        body, name="join_grad_halves", out_shape=[SDS((W_CUTS[a][1], W_CUTS[a][2]), F32) for a in range(n)],
        in_specs=[ANY] * n, out_specs=[ANY] * n,
        scratch_shapes=[pltpu.SemaphoreType.DMA((n,)), pltpu.SemaphoreType.DMA((n,))],
        input_output_aliases={a: a for a in range(n)}, compiler_params=_cp(side=True),
    )(*fulls)


def _ada_forward(c_all, w_ada, b_cols):
    nb, nc = c_all.shape[0], w_ada.shape[1]

    def body(c_ref, w_ref, b_ref, o_ref):
        cv = c_ref[...]
        sc = (cv * _sig(cv)).astype(BF16)
        o_ref[...] = _dot(sc, w_ref[...].astype(BF16)) + b_ref[...]

    return pl.pallas_call(body, name="ada_forward", out_shape=SDS((nb, nc), F32), compiler_params=_cp(vmem=VMEM_CAP // 2))(c_all, w_ada, b_cols)


def _ada_backward(c_all, dmod_cols, dmod_all):
    nb, nc = dmod_cols.shape

    def body(c_ref, d_ref, a_ref, gw_ref, gb_ref):
        cv = c_ref[...]
        sc = (cv * _sig(cv)).astype(BF16)
        gw_ref[...] = _dot_tn(sc, d_ref[...].astype(BF16))
        gb_ref[...] = jnp.sum(a_ref[...], axis=0, keepdims=True)

    return pl.pallas_call(body, name="ada_backward", out_shape=[SDS((D, nc), F32), SDS((1, dmod_all.shape[1]), F32)],
                          compiler_params=_cp(vmem=VMEM_CAP // 2))(c_all, dmod_cols, dmod_all)


def _modulate(x2, sc1p, shift, seq, tm=256):
    t = x2.shape[0]
    spt = seq // tm

    def body(x_ref, sc_ref, sh_ref, h_ref, ht_ref):
        h = x_ref[...] * sc_ref[0] + sh_ref[0]
        h_ref[...] = h.astype(BF16)
        ht_ref[...] = h.T.astype(BF16)

    per_seq = pl.BlockSpec((1, 1, D), lambda i: (i // spt, 0, 0))
    return pl.pallas_call(
        body, name="modulate", out_shape=[HBM_OUT((t, D), BF16), HBM_OUT((D, t), BF16)], grid=(t // tm,),
        in_specs=[pl.BlockSpec((tm, D), lambda i: (i, 0)), per_seq, per_seq],
        out_specs=[pl.BlockSpec((tm, D), lambda i: (i, 0)), pl.BlockSpec((D, tm), lambda i: (0, i))],
        compiler_params=_cp(("parallel",)),
    )(x2, sc1p, shift)


TW = 256
TPS = NCOL // NCHIP // TW
NT = NCOL // TW
NQKV_T = 3 * QW // TW


def _tile_tables():
    tabs = np.zeros((NCHIP, 3, NT), np.int32)
    for me in range(NCHIP):
        tiles = [TPS * (me ^ (s // TPS)) + s % TPS for s in range(NT)]
        tabs[me, 0] = tiles
        for row, (lo, hi) in enumerate(((0, NQKV_T), (NQKV_T, NT))):
            mine = [w - lo if lo <= w < hi else None for w in tiles]
            held = next(m for m in mine if m is not None)
            for s, m in enumerate(mine):
                held = held if m is None else m
                tabs[me, 1 + row, s] = held
    return tabs


def _project_gather(h, fulls, tab):
    t = h.shape[0]
    n = len(fulls)

    def body(tab_ref, h_ref, *rest):
        qkv_ref, g_ref = rest[n], rest[n + 1]
        full = rest[n + 2:2 * n + 2]
        w_buf, tile_sems, send_sems, recv_sems = rest[2 * n + 2:]
        s = pl.program_id(0)
        x, y, c = _place()
        me = 2 * x + y
        peers = [(x, 1 - y), (1 - x, y), (1 - x, 1 - y)]
        sibling = (x, y, 1 - c)

        def hop(a, r, stage, chip, half, to):
            window = _shard_window(full[a], W_CUTS[a], chip, half)
            k = 6 * a + 2 * r + stage
            return pltpu.make_async_remote_copy(src_ref=window, dst_ref=window, send_sem=send_sems.at[k], recv_sem=recv_sems.at[k],
                                                device_id=to, device_id_type=MESH)

        def send(a, r):
            return hop(a, r, 0, me, c, (*peers[r], c))

        def arrive(a, r):
            px, py = peers[r]
            chip = 2 * px + py
            hop(a, r, 0, chip, c, (px, py, c)).wait_recv()
            hop(a, r, 1, chip, c, sibling).start()
            hop(a, r, 1, chip, 1 - c, sibling).wait_recv()

        def tile(step, slot):
            col = pl.multiple_of(tab_ref[0, step] * TW, TW)
            return pltpu.make_async_copy(full[0].at[:, pl.ds(col, TW)], w_buf.at[slot], tile_sems.at[slot])

        @pl.when(s == 0)
        def _():
            send(0, 0).start()
            send(0, 1).start()
            tile(0, 0).start()

        slot = s % 2
        tile(s, slot).wait()

        @pl.when((s + 1 < NT) & ((s + 1) % TPS != 0))
        def _():
            tile(s + 1, 1 - slot).start()

        is_qkv = tab_ref[0, s] < NQKV_T
        for k in range(2):
            @pl.when(slot == k)
            def _(k=k):
                acc = _dot(h_ref[...], w_buf[k])

                @pl.when(is_qkv)
                def _():
                    qkv_ref[...] = acc.astype(BF16)

                @pl.when(jnp.logical_not(is_qkv))
                def _():
                    g_ref[...] = acc.astype(BF16)

        for r in range(3):
            @pl.when(s + 1 == TPS * (r + 1))
            def _(r=r):
                arrive(0, r)
                tile(s + 1, 1 - slot).start()
                if r == 0:
                    send(0, 2).start()
                    for a in range(1, n):
                        for q in range(3):
                            send(a, q).start()

        @pl.when(s == NT - 1)
        def _():
            for a in range(1, n):
                for r in range(3):
                    arrive(a, r)
            for a in range(n):
                for r in range(3):
                    send(a, r).wait_send()
                    px, py = peers[r]
                    hop(a, r, 1, 2 * px + py, c, sibling).wait_send()

    outs = pl.pallas_call(
        body, name="project_gather", out_shape=[HBM_OUT((t, 3 * QW), BF16), HBM_OUT((t, NGATE), BF16)] + [SDS(s, BF16) for s in W_FULL],
        grid_spec=pltpu.PrefetchScalarGridSpec(
            num_scalar_prefetch=1, grid=(NT,),
            in_specs=[pl.BlockSpec((t, D), lambda s, tab: (0, 0))] + [ANY] * n,
            out_specs=[pl.BlockSpec((t, TW), lambda s, tab: (0, tab[1, s])), pl.BlockSpec((t, TW), lambda s, tab: (0, tab[2, s]))] + [ANY] * n,
            scratch_shapes=[pltpu.VMEM((2, D, TW), BF16), pltpu.SemaphoreType.DMA((2,)),
                            pltpu.SemaphoreType.DMA((6 * n,)), pltpu.SemaphoreType.DMA((6 * n,))]),
        input_output_aliases={2 + a: 2 + a for a in range(n)},
        compiler_params=_cp(("arbitrary",), VMEM_CAP, side=True),
    )(tab, _in_hbm(h), *fulls)
    return outs[0], outs[1], outs[2:]


def _bias_tables(rel_bias, buckets):
    def body(tab_ref, bk_ref, o_ref):
        a = lax.broadcasted_iota(jnp.int32, (BLK, 2 * BLK), 0)
        b = lax.broadcasted_iota(jnp.int32, (BLK, 2 * BLK), 1)
        steps = a + BLK - b
        valid = (steps >= 0) & (steps <= BLK)
        for g in range(3):
            bk = bk_ref[g]
            for j in range(4):
                def pick(kk, acc, bk=bk, col=4 * g + j):
                    return jnp.where(bk == kk, tab_ref[kk, col], acc)

                acc = lax.fori_loop(0, N_BUCKETS, pick, jnp.zeros((BLK, 2 * BLK), F32))
                o_ref[g, j] = jnp.where(valid, acc, NEG)

    return pl.pallas_call(
        body, name="bias_tables", out_shape=SDS((3, 4, BLK, 2 * BLK), F32),
        in_specs=[pl.BlockSpec(memory_space=pltpu.SMEM), VMEM_SPEC], out_specs=VMEM_SPEC,
    )(rel_bias, buckets)


def _bias_grad(ds_sum, buckets):
    def body(ds_ref, bk_ref, o_ref):
        lane = lax.broadcasted_iota(jnp.int32, (1, 128), 1)
        for g in range(3):
            def bucket(kk, carry, g=g):
                row = jnp.zeros((1, 128), F32)
                for j in range(4):
                    v = jnp.where(bk_ref[g] == kk, ds_ref[g, j], 0.0)
                    s = jnp.sum(jnp.sum(v, axis=1, keepdims=True), axis=0, keepdims=True)
                    row = jnp.where(lane == j, s, row)
                o_ref[g, pl.ds(kk, 1), :] = row
                return carry

            lax.fori_loop(0, N_BUCKETS, bucket, 0)

    return pl.pallas_call(body, name="bias_grad", out_shape=SDS((3, N_BUCKETS, 128), F32), in_specs=[VMEM_SPEC, VMEM_SPEC],
                          out_specs=VMEM_SPEC)(ds_sum, buckets)


def _sub_rows(d, r, first, size):
    return pl.ds(first * d + r, size) if d == 1 else pl.ds(first * d + r, size, stride=d)


def _head_spec(seq, g, part):
    return pl.BlockSpec((seq, HD), lambda b, hh: (b, part * (QW // HD) + 4 * g + hh))


def _rows(start, count, stride):
    return pl.ds(start, count) if stride == 1 else pl.ds(start, count, stride=stride)


def _gather_rows(dst, dst0, src, src0, stride, count):
    for first in range(0, count, BLK):
        dst[pl.ds(dst0 + first, BLK), :] = src[_rows(src0 + first * stride, BLK, stride), :].astype(dst.dtype)


def _scatter_rows(dst, dst0, stride, src, src0, count):
    for first in range(0, count, BLK):
        dst[_rows(dst0 + first * stride, BLK, stride), :] = src[pl.ds(src0 + first, BLK), :].astype(dst.dtype)


def _by_subsequence(dst, src, d, wide=None, tmp=None):
    seq = src.shape[0]
    ln = seq // d
    if wide is not None:
        wide[...] = src[...].astype(F32)
        src = wide
    if d <= 4:
        for r in range(d):
            _gather_rows(dst, r * ln, src, r, d, ln)
    else:
        quarter = seq // 4
        for r4 in range(4):
            _gather_rows(tmp, r4 * quarter, src, r4, 4, quarter)
        for r4 in range(4):
            for a in range(d // 4):
                _gather_rows(dst, (4 * a + r4) * ln, tmp, r4 * quarter + a, d // 4, ln)


def _to_sequence(dst, src, d, tmp=None):
    seq = dst.shape[0]
    ln = seq // d
    if d <= 4:
        for r in range(d):
            _scatter_rows(dst, r, d, src, r * ln, ln)
    else:
        quarter = seq // 4
        for r4 in range(4):
            for a in range(d // 4):
                _scatter_rows(tmp, r4 * quarter + a, d // 4, src, (4 * a + r4) * ln, ln)
        for r4 in range(4):
            _scatter_rows(dst, r4, 4, tmp, r4 * quarter, quarter)


def _attn_forward(g, qkv, bias, bsz, seq):
    d = DILATIONS[g]
    ln = seq // d
    units = [(r, n) for r in range(d) for n in range(ln // BLK)]

    def band(n):
        return slice(BLK, 2 * BLK) if n == 0 else slice(0, 2 * BLK)

    def body(q_ref, k_ref, v_ref, b_ref, o_ref, l_ref, *scratch):
        hs = pl.program_id(1)
        s_scr, p_scr = scratch[:2]
        if d == 1:
            qd, kd, vd = q_ref, k_ref, v_ref
        else:
            wide, tmp, qd, kd, vd = scratch[2:7]
            for dst, src in ((qd, q_ref), (kd, k_ref), (vd, v_ref)):
                _by_subsequence(dst, src, d, wide, tmp)
        blk = lambda r, n: pl.ds(r * ln + n * BLK, BLK)
        direct = d <= 4
        out_rows = (lambda r, n: _sub_rows(d, r, n * BLK, BLK)) if direct else blk
        o_dst, l_dst = (o_ref, l_ref) if direct else scratch[7:9]
        for u, (r, n) in enumerate(units):
            s_scr[u, :, BLK:] = _dot_nt(qd[blk(r, n), :], kd[blk(r, n), :])
            if n > 0:
                s_scr[u, :, :BLK] = _dot_nt(qd[blk(r, n), :], kd[blk(r, n - 1), :])
        for u, (r, n) in enumerate(units):
            s = s_scr[u, :, band(n)] * SCALE + b_ref[hs, :, band(n)]
            m = jnp.max(s, axis=1, keepdims=True)
            e = jnp.exp(s - m)
            den = jnp.sum(e, axis=1, keepdims=True)
            p_scr[u, :, band(n)] = (e * (1.0 / den)).astype(BF16)
            l_dst[out_rows(r, n), :] = jnp.broadcast_to(m + jnp.log(den), (BLK, HD))
        for u, (r, n) in enumerate(units):
            acc = _dot(p_scr[u, :, BLK:], vd[blk(r, n), :])
            if n > 0:
                acc = acc + _dot(p_scr[u, :, :BLK], vd[blk(r, n - 1), :])
            o_dst[out_rows(r, n), :] = acc
        if not direct:
            _to_sequence(o_ref, o_dst, d, tmp)
            _to_sequence(l_ref, l_dst, d, tmp)

    rows_f32, rows_bf16 = pltpu.VMEM((seq, HD), F32), pltpu.VMEM((seq, HD), BF16)
    regrouped = [] if d == 1 else [rows_f32] * 2 + [rows_bf16] * 3 + ([] if d <= 4 else [rows_f32] * 2)
    out_spec = pl.BlockSpec((seq, HD), lambda b, hh: (b, hh))
    return pl.pallas_call(
        body, name=f"attn_forward_{g}", out_shape=[HBM_OUT((bsz * seq, AW), F32)] * 2, grid=(bsz, 4),
        in_specs=[_head_spec(seq, g, part) for part in range(3)] + [pl.BlockSpec((4, BLK, 2 * BLK), lambda b, hh: (0, 0, 0))],
        out_specs=[out_spec, out_spec],
        scratch_shapes=[pltpu.VMEM((len(units), BLK, 2 * BLK), F32), pltpu.VMEM((len(units), BLK, 2 * BLK), BF16)] + regrouped,
        compiler_params=_cp(("parallel", "parallel"), VMEM_CAP // 2),
    )(qkv, qkv, qkv, _in_hbm(bias))


def _attn_backward(g, qkv, do, dl, bias, prev_out, bsz, seq):
    d = DILATIONS[g]
    ln = seq // d
    units = [(r, n) for r in range(d) for n in range(ln // BLK)]

    def body(q_ref, k_ref, v_ref, do_ref, dl_ref, b_ref, *rest):
        dq_ref, dk_ref, dv_ref, db_ref = rest[-18:-14]
        wide, tmp, qd, kd, vd, dod, dld, dqd, dkd, dvd, s_scr, dp_scr, p_scr, ds_scr = rest[-14:]
        hs = pl.program_id(1)

        @pl.when((pl.program_id(0) == 0) & (hs == 0))
        def _():
            db_ref[...] = jnp.zeros_like(db_ref)

        for dst, src in ((qd, q_ref), (kd, k_ref), (vd, v_ref)):
            _by_subsequence(dst, src, d, wide, tmp)
        _by_subsequence(dod, do_ref, d, None, tmp)
        _by_subsequence(dld, dl_ref, d, None, tmp)
        dkd[...] = jnp.zeros_like(dkd)
        dvd[...] = jnp.zeros_like(dvd)
        blk = lambda r, n: pl.ds(r * ln + n * BLK, BLK)
        keys = lambda r, n: [(blk(r, n), slice(BLK, 2 * BLK))] + ([(blk(r, n - 1), slice(0, BLK))] if n > 0 else [])
        for u, (r, n) in enumerate(units):
            for rows, band in keys(r, n):
                s_scr[u, :, band] = _dot_nt(qd[blk(r, n), :], kd[rows, :])
                dp_scr[u, :, band] = _dot_nt(dod[blk(r, n), :], vd[rows, :])
        for u, (r, n) in enumerate(units):
            both = dld[blk(r, n), :]
            lse, delta = both[:, 0:1], both[:, 64:65]
            band = slice(BLK, 2 * BLK) if n == 0 else slice(0, 2 * BLK)
            p = jnp.exp(s_scr[u, :, band] * SCALE + b_ref[hs, :, band] - lse)
            ds = p * (dp_scr[u, :, band] - delta)
            p_scr[u, :, band] = p.astype(BF16)
            ds_scr[u, :, band] = ds.astype(BF16)
            db_ref[hs, :, band] += ds
        for u, (r, n) in enumerate(units):
            dq = jnp.zeros((BLK, HD), F32)
            for rows, band in keys(r, n):
                dvd[rows, :] += _dot_tn(p_scr[u, :, band], dod[blk(r, n), :])
                dkd[rows, :] += _dot_tn(ds_scr[u, :, band], qd[blk(r, n), :]) * SCALE
                dq = dq + _dot(ds_scr[u, :, band], kd[rows, :])
            dqd[blk(r, n), :] = dq * SCALE
        for out, acc in ((dq_ref, dqd), (dk_ref, dkd), (dv_ref, dvd)):
            if d == 1:
                out[...] = acc[...].astype(BF16)
            else:
                _to_sequence(wide, acc, d, tmp)
                out[...] = wide[...].astype(BF16)

    qkv_spec = _head_spec(seq, g, 0)
    out_spec = pl.BlockSpec((seq, HD), lambda b, hh: (b, hh))
    band_spec = pl.BlockSpec((4, BLK, 2 * BLK), lambda b, hh: (0, 0, 0))
    ins = [qkv, qkv, qkv, _in_hbm(do), _in_hbm(dl), _in_hbm(bias)]
    in_specs = [_head_spec(seq, g, part) for part in range(3)] + [out_spec, out_spec, band_spec]
    aliases = {}
    if prev_out is not None:
        ins += list(prev_out)
        in_specs += [ANY] * 3
        aliases = {6: 0, 7: 1, 8: 2}
    rows_bf16, rows_f32 = pltpu.VMEM((seq, HD), BF16), pltpu.VMEM((seq, HD), F32)
    staged = [pltpu.VMEM((len(units), BLK, 2 * BLK), F32)] * 2 + [pltpu.VMEM((len(units), BLK, 2 * BLK), BF16)] * 2
    dq, dk, dv, db = pl.pallas_call(
        body, name=f"attn_backward_{g}", out_shape=[HBM_OUT((bsz * seq, QW), BF16)] * 3 + [SDS((4, BLK, 2 * BLK), F32)], grid=(bsz, 4),
        in_specs=in_specs, out_specs=[qkv_spec] * 3 + [band_spec], input_output_aliases=aliases,
        scratch_shapes=[rows_f32] * 2 + [rows_bf16] * 4 + [rows_f32] * 4 + staged,
        compiler_params=_cp(("arbitrary", "arbitrary"), VMEM_CAP // 2),
    )(*ins)
    return (dq, dk, dv), db


def _mix_forward(gates, og, lg, x2, tgt, gate, w_ao, w_co, w_o, conv_w, conv_b, ln_g, ln_b, bsz, seq, tm=256):
    t = x2.shape[0]
    spt = seq // tm

    def body(g_ref, o1, o2, o3, l1, l2, l3, x_ref, t_ref, gate_ref, wao_ref, wco_ref, wo_ref, cw_ref, cb_ref, lng_ref, lnb_ref,
             ain_ref, sin_ref, mrg_ref, dy_ref, aout_ref, sout_ref, yc_ref, o_ref, lj_ref, dxr_ref, vec_ref, dgate_ref, zc_ref):
        b, i = pl.program_id(0), pl.program_id(1)

        @pl.when((b == 0) & (i == 0))
        def _():
            vec_ref[...] = jnp.zeros_like(vec_ref)

        @pl.when(i == 0)
        def _():
            zc_ref[...] = jnp.zeros_like(zc_ref)
            dgate_ref[...] = jnp.zeros_like(dgate_ref)

        g_attn, u, bg, cg, g_conv, m_attn, m_conv = (g_ref[:, lo:hi].astype(F32) for lo, hi in GATE_COLS)
        la, lb, lc = l1[...], l2[...], l3[...]
        mx = jnp.maximum(la, jnp.maximum(lb, lc))
        ea, eb, ec = jnp.exp(la - mx), jnp.exp(lb - mx), jnp.exp(lc - mx)
        den = ea + eb + ec
        o = (ea * o1[...] + eb * o2[...] + ec * o3[...]) / den
        o_ref[...] = o
        lj_ref[...] = mx + jnp.log(den)
        a_in = o * (g_attn * _sig(g_attn))
        ain_ref[...] = a_in.astype(BF16)
        a_out = _dot(a_in.astype(BF16), wao_ref[...])
        aout_ref[...] = a_out.astype(BF16)
        z = cg * u
        rows = lax.broadcasted_iota(jnp.int32, (tm, D), 0)
        c6, c7 = zc_ref[6:7, :], zc_ref[7:8, :]
        z1 = jnp.where(rows == 0, c7, pltpu.roll(z, 1, 0))
        z2 = jnp.where(rows == 0, c6, jnp.where(rows == 1, c7, pltpu.roll(z, 2, 0)))
        zc_ref[...] = z[tm - 8:tm, :]
        y_conv = (cw_ref[0:1, :] * z2 + cw_ref[1:2, :] * z1 + cw_ref[2:3, :] * z) + cb_ref[...]
        yc_ref[...] = y_conv.astype(BF16)
        s_in = bg * y_conv * (g_conv * _sig(g_conv))
        sin_ref[...] = s_in.astype(BF16)
        s_out = _dot(s_in.astype(BF16), wco_ref[...])
        sout_ref[...] = s_out.astype(BF16)
        merged = _sig(m_attn) * a_out + _sig(m_conv) * s_out
        mrg_ref[...] = merged.astype(BF16)
        y = _dot(merged.astype(BF16), wo_ref[...])
        gate1 = 1.0 + gate_ref[0]
        r = ALPHA * x_ref[...] + gate1 * y
        mu = jnp.mean(r, axis=1, keepdims=True)
        rc = r - mu
        rstd = lax.rsqrt(jnp.mean(rc * rc, axis=1, keepdims=True) + LN_EPS)
        xhat = rc * rstd
        diff = (xhat * lng_ref[...] + lnb_ref[...]) - t_ref[...]
        dout = diff * (1.0 / D)
        vec_ref[0:1, :] += jnp.sum(dout * xhat, axis=0, keepdims=True)
        vec_ref[1:2, :] += jnp.sum(dout, axis=0, keepdims=True)
        vec_ref[2:3, :] += jnp.sum(diff * diff, axis=0, keepdims=True)
        dxh = dout * lng_ref[...]
        dr = rstd * (dxh - jnp.mean(dxh, axis=1, keepdims=True) - xhat * jnp.mean(dxh * xhat, axis=1, keepdims=True))
        dxr_ref[...] = ALPHA * dr
        dy_ref[...] = (dr * gate1).astype(BF16)
        dgate_ref[0] += jnp.sum(dr * y, axis=0, keepdims=True)

    tok = lambda w: pl.BlockSpec((tm, w), lambda b, i: (b * spt + i, 0))
    const = lambda s: pl.BlockSpec(s, lambda b, i: (0,) * len(s))
    per_seq = pl.BlockSpec((1, 1, D), lambda b, i: (b, 0, 0))
    outs = pl.pallas_call(
        body, name="mix_forward", grid=(bsz, spt),
        out_shape=[HBM_OUT((t, AW), BF16), HBM_OUT((t, D), BF16), HBM_OUT((t, D), BF16), HBM_OUT((t, D), BF16), HBM_OUT((t, D), BF16),
                   HBM_OUT((t, D), BF16), HBM_OUT((t, D), BF16), HBM_OUT((t, AW), F32), HBM_OUT((t, AW), F32), HBM_OUT((t, D), F32),
                   SDS((8, D), F32), SDS((bsz, 1, D), F32)],
        in_specs=[tok(NGATE)] + [tok(AW)] * 6 + [tok(D), tok(D), per_seq, const((AW, D)), const((D, D)), const((D, D)),
                                                 const((3, D)), const((1, D)), const((1, D)), const((1, D))],
        out_specs=[tok(AW), tok(D), tok(D), tok(D), tok(D), tok(D), tok(D), tok(AW), tok(AW), tok(D), const((8, D)), per_seq],
        scratch_shapes=[pltpu.VMEM((8, D), F32)],
        compiler_params=_cp(("arbitrary", "arbitrary"), VMEM_CAP),
    )(gates, *map(_in_hbm, og), *map(_in_hbm, lg), x2, tgt, gate, w_ao, w_co, w_o, conv_w, conv_b, ln_g, ln_b)
    return outs


def _mix_backward(gates, dy, a_out, s_out, y_conv, o, lj, w_ao, w_co, w_o, conv_w, bsz, seq, tm=256):
    t = dy.shape[0]
    spt = seq // tm

    def body(g_ref, dy_ref, aout_ref, sout_ref, yc_ref, o_ref, lj_ref, wao_ref, wco_ref, wo_ref, cw_ref,
             dg_ref, do_ref, dl_ref, daout_ref, dsout_ref, vec_ref, car_ref):
        b, i = pl.program_id(0), pl.program_id(1)

        @pl.when((b == 0) & (i == 0))
        def _():
            vec_ref[...] = jnp.zeros_like(vec_ref)

        @pl.when(i == 0)
        def _():
            car_ref[...] = jnp.zeros_like(car_ref)

        g_attn, u, bg, cg, g_conv, m_attn, m_conv = (g_ref[:, lo:hi].astype(F32) for lo, hi in GATE_COLS)
        dmerged = _dot_nt(dy_ref[...], wo_ref[...])
        sa, sc = _sig(m_attn), _sig(m_conv)
        da_out = (dmerged * sa).astype(BF16)
        ds_out = (dmerged * sc).astype(BF16)
        daout_ref[...] = da_out
        dsout_ref[...] = ds_out
        dg_ref[:, 4608:5632] = (dmerged * aout_ref[...].astype(F32) * (sa * (1.0 - sa))).astype(BF16)
        dg_ref[:, 5632:6656] = (dmerged * sout_ref[...].astype(F32) * (sc * (1.0 - sc))).astype(BF16)
        da_in = _dot_nt(da_out, wao_ref[...])
        ds_in = _dot_nt(ds_out, wco_ref[...])
        sga = _sig(g_attn)
        o = o_ref[...]
        do = da_in * (g_attn * sga)
        do_ref[...] = do
        dg_ref[:, 0:512] = (da_in * o * (sga * (1.0 + g_attn * (1.0 - sga)))).astype(BF16)
        prod = do * o
        lane = lax.broadcasted_iota(jnp.int32, (tm, HD), 1)
        for j in range(4):
            cs = slice(j * HD, (j + 1) * HD)
            delta = jnp.sum(prod[:, cs], axis=1, keepdims=True)
            dl_ref[:, cs] = jnp.where(lane < 64, lj_ref[:, cs], delta)
        sgc = _sig(g_conv)
        silu_c = g_conv * sgc
        yc = yc_ref[...].astype(F32)
        dg_ref[:, 1536:2560] = (ds_in * yc * silu_c).astype(BF16)
        dg_ref[:, 3584:4608] = (ds_in * bg * yc * (sgc * (1.0 + g_conv * (1.0 - sgc)))).astype(BF16)
        dyc = ds_in * bg * silu_c
        rows = lax.broadcasted_iota(jnp.int32, (tm, D), 0)
        c0, c1 = car_ref[0:1, :], car_ref[1:2, :]
        n1 = jnp.where(rows == tm - 1, c0, pltpu.roll(dyc, tm - 1, 0))
        n2 = jnp.where(rows == tm - 2, c0, jnp.where(rows == tm - 1, c1, pltpu.roll(dyc, tm - 2, 0)))
        car_ref[...] = dyc[0:8, :]
        dz = cw_ref[2:3, :] * dyc + cw_ref[1:2, :] * n1 + cw_ref[0:1, :] * n2
        z = cg * u
        dg_ref[:, 512:1536] = (dz * cg).astype(BF16)
        dg_ref[:, 2560:3584] = (dz * u).astype(BF16)
        vec_ref[0:1, :] += jnp.sum(n2 * z, axis=0, keepdims=True)
        vec_ref[1:2, :] += jnp.sum(n1 * z, axis=0, keepdims=True)
        vec_ref[2:3, :] += jnp.sum(dyc * z, axis=0, keepdims=True)
        vec_ref[3:4, :] += jnp.sum(dyc, axis=0, keepdims=True)

    tok = lambda w: pl.BlockSpec((tm, w), lambda b, i: (b * spt + (spt - 1 - i), 0))
    const = lambda s: pl.BlockSpec(s, lambda b, i: (0,) * len(s))
    return pl.pallas_call(
        body, name="mix_backward", grid=(bsz, spt),
        out_shape=[HBM_OUT((t, NGATE), BF16), HBM_OUT((t, AW), F32), HBM_OUT((t, AW), F32), HBM_OUT((t, D), BF16), HBM_OUT((t, D), BF16),
                   SDS((8, D), F32)],
        in_specs=[tok(NGATE), tok(D), tok(D), tok(D), tok(D), tok(AW), tok(AW), const((AW, D)), const((D, D)), const((D, D)), const((3, D))],
        out_specs=[tok(NGATE), tok(AW), tok(AW), tok(D), tok(D), const((8, D))],
        scratch_shapes=[pltpu.VMEM((8, D), F32)],
        compiler_params=_cp(("arbitrary", "arbitrary"), VMEM_CAP),
    )(gates, dy, a_out, s_out, y_conv, o, lj, w_ao, w_co, w_o, conv_w)


def _out_weight_grads(a_in, da_out, s_in, ds_out, merged, dy, tk=512):
    t = dy.shape[0]

    def body(ain_ref, da_ref, sin_ref, ds_ref, m_ref, dy_ref, gao_ref, gco_ref, go_ref):
        @pl.when(pl.program_id(0) == 0)
        def _():
            gao_ref[...] = jnp.zeros_like(gao_ref)
            gco_ref[...] = jnp.zeros_like(gco_ref)
            go_ref[...] = jnp.zeros_like(go_ref)

        gao_ref[...] += _dot_tn(ain_ref[...], da_ref[...])
        gco_ref[...] += _dot_tn(sin_ref[...], ds_ref[...])
        go_ref[...] += _dot_tn(m_ref[...], dy_ref[...])

    tok = lambda w: pl.BlockSpec((tk, w), lambda i: (i, 0))
    const = lambda s: pl.BlockSpec(s, lambda i: (0, 0))
    return pl.pallas_call(
        body, name="out_weight_grads", grid=(t // tk,), out_shape=[SDS((AW, D), F32), SDS((D, D), F32), SDS((D, D), F32)],
        in_specs=[tok(AW), tok(D), tok(D), tok(D), tok(D), tok(D)], out_specs=[const((AW, D)), const((D, D)), const((D, D))],
        compiler_params=_cp(("arbitrary",), VMEM_CAP),
    )(a_in, da_out, s_in, ds_out, merged, dy)


def _input_grad(dq, dk, dv, dgates, w, x2, dxr, sc1p, seq, sums, tm=512):
    t = x2.shape[0]
    nt = t // tm
    spt = seq // tm
    bsz = t // seq
    n = len(sums)
    gblk = NGATE // 4
    nsteps = 3 + 4

    def body(dq_ref, dk_ref, dv_ref, dg_ref, wq_ref, wg_ref, x_ref, dxr_ref, sc_ref, *rest):
        src, (dx_ref, dsh_ref, dsc_ref), land = rest[:n], rest[n:n + 3], rest[n + 3:2 * n + 3]
        acc_ref, send_sems, recv_sems = rest[2 * n + 3:]
        j, i = pl.program_id(0), pl.program_id(1)
        px, py, pc = _place()
        chips = [(1 - px, py), (px, 1 - py), (1 - px, 1 - py)]
        copies = [pltpu.make_async_remote_copy(src_ref=src[a].at[2 * cx + cy], dst_ref=land[a].at[r], send_sem=send_sems.at[3 * a + r],
                                               recv_sem=recv_sems.at[3 * a + r], device_id=(cx, cy, pc), device_id_type=MESH)
                  for a in range(n) for r, (cx, cy) in enumerate(chips)]
        rows = pl.ds(pl.multiple_of(i * tm, tm), tm)

        @pl.when((i == 0) & (j == 0))
        def _():
            for cp in copies:
                cp.start()

        for k, ref in enumerate((dq_ref, dk_ref, dv_ref)):
            @pl.when(j == k)
            def _(k=k, ref=ref):
                part = _dot_nt(ref[...], wq_ref[...])
                if k == 0:
                    acc_ref[rows, :] = part
                else:
                    acc_ref[rows, :] += part

        @pl.when((j >= 3) & (j < nsteps - 1))
        def _():
            acc_ref[rows, :] += _dot_nt(dg_ref[...], wg_ref[...])

        @pl.when(j == nsteps - 1)
        def _():
            dh = acc_ref[rows, :] + _dot_nt(dg_ref[...], wg_ref[...])
            dx_ref[...] = dh * sc_ref[0] + dxr_ref[...]

            @pl.when(i % spt == 0)
            def _():
                dsh_ref[...] = jnp.zeros_like(dsh_ref)
                dsc_ref[...] = jnp.zeros_like(dsc_ref)

            dsh_ref[0] += jnp.sum(dh, axis=0, keepdims=True)
            dsc_ref[0] += jnp.sum(dh * x_ref[...], axis=0, keepdims=True)

        @pl.when((i == nt - 1) & (j == nsteps - 1))
        def _():
            for cp in copies:
                cp.wait()

    def held(k):
        return lambda j, i: (jnp.where(j == k, i, jnp.where(j < k, 0, nt - 1)), 0)

    last = lambda j, i: (jnp.where(j == nsteps - 1, i, 0), 0)
    outs = pl.pallas_call(
        body, name="input_grad", grid=(nsteps, nt),
        out_shape=[SDS((t, D), F32), SDS((bsz, 1, D), F32), SDS((bsz, 1, D), F32)] + [SDS((3,) + s.shape[1:], BF16) for s in sums],
        in_specs=[pl.BlockSpec((tm, QW), held(0)), pl.BlockSpec((tm, QW), held(1)), pl.BlockSpec((tm, QW), held(2)),
                  pl.BlockSpec((tm, gblk), lambda j, i: (jnp.where(j >= 3, i, 0), jnp.clip(j - 3, 0, 3))),
                  pl.BlockSpec((D, QW), lambda j, i: (0, jnp.minimum(j, 2))),
                  pl.BlockSpec((pl.Element(D), pl.Element(gblk)), lambda j, i: (0, pl.multiple_of(3 * QW + gblk * jnp.clip(j - 3, 0, 3), 128))),
                  pl.BlockSpec((tm, D), last), pl.BlockSpec((tm, D), last),
                  pl.BlockSpec((1, 1, D), lambda j, i: (jnp.where(j == nsteps - 1, i // spt, 0), 0, 0))] + [ANY] * n,
        out_specs=[pl.BlockSpec((tm, D), last),
                   pl.BlockSpec((1, 1, D), lambda j, i: (jnp.where(j == nsteps - 1, i // spt, 0), 0, 0)),
                   pl.BlockSpec((1, 1, D), lambda j, i: (jnp.where(j == nsteps - 1, i // spt, 0), 0, 0))] + [ANY] * n,
        scratch_shapes=[pltpu.VMEM((t, D), F32), pltpu.SemaphoreType.DMA((3 * NCHIP,)), pltpu.SemaphoreType.DMA((3 * NCHIP,))],
        compiler_params=_cp(("arbitrary", "arbitrary"), VMEM_CAP, side=True),
    )(dq, dk, dv, dgates, w, w, x2, dxr, sc1p, *sums)
    return outs[0], outs[1], outs[2], outs[3:]


def _in_weight_grad(ht, src, col0, prev, name):
    t = ht.shape[1]
    ncols = src.shape[1] // TN

    def body(ht_ref, s_ref, *rest):
        rest[-1][...] = _dot(ht_ref[...], s_ref[...].astype(BF16))

    ins = [ht, src]
    in_specs = [pl.BlockSpec((D, t), lambda j: (0, 0)), pl.BlockSpec((t, TN), lambda j: (0, j))]
    aliases = {}
    if prev is not None:
        ins.append(prev)
        in_specs.append(ANY)
        aliases = {2: 0}
    return pl.pallas_call(
        body, name=name, grid=(ncols,), out_shape=SDS((D, NCOL), F32), in_specs=in_specs,
        out_specs=pl.BlockSpec((D, TN), lambda j: (0, col0 + j)), input_output_aliases=aliases,
        compiler_params=_cp(("arbitrary",), VMEM_CAP),
    )(*ins)


def _sum_partials(gathered):
    def body(g_ref, o_ref):
        acc = g_ref[0]
        for k in range(1, 8):
            acc = acc + g_ref[k]
        o_ref[...] = acc

    return pl.pallas_call(body, name="sum_partials", out_shape=SDS(gathered.shape[1:], F32), in_specs=[VMEM_SPEC], out_specs=VMEM_SPEC)(gathered)


def _adamw(w, g, m, v, name, tr=256):
    r, cdim = w.shape
    tr = tr if cdim <= D else tr // 2
    tr = tr if (r % tr == 0 and r > tr) else r

    def body(w_ref, g_ref, m_ref, v_ref, d_ref, nm_ref, nv_ref):
        gv = g_ref[...]
        nm = B1 * m_ref[...] + (1.0 - B1) * gv
        nv = B2 * v_ref[...] + (1.0 - B2) * (gv * gv)
        m_hat = nm / (1.0 - B1 ** STEP)
        v_hat = nv / (1.0 - B2 ** STEP)
        d_ref[...] = -LR * (m_hat / (jnp.sqrt(v_hat) + EPS) + WD * w_ref[...])
        nm_ref[...] = nm
        nv_ref[...] = nv

    spec = pl.BlockSpec((tr, cdim), lambda i: (i, 0))
    return pl.pallas_call(
        body, name=name, grid=(r // tr,), out_shape=[SDS((r, cdim), F32)] * 3, in_specs=[spec] * 4, out_specs=[spec] * 3,
        compiler_params=_cp(("parallel",), VMEM_CAP // 2),
    )(w, g, m, v)


def _t5_bucket(dist):
    n = jnp.maximum(dist, 1).astype(F32)
    large = MAX_EXACT + (jnp.log(n / MAX_EXACT) / math.log(MAX_DISTANCE / MAX_EXACT) * (N_BUCKETS - MAX_EXACT)).astype(jnp.int32)
    large = jnp.minimum(large, N_BUCKETS - 1)
    return jnp.where(dist < MAX_EXACT, dist, large)


def _band_buckets():
    a = jnp.arange(BLK)[:, None]
    b = jnp.arange(2 * BLK)[None, :]
    steps = jnp.maximum(a + BLK - b, 0)
    return jnp.stack([_t5_bucket(steps * d) for d in DILATIONS]).astype(jnp.int32)


def _pad_rows(a, rows=8):
    return jnp.pad(a, ((0, rows - a.shape[0]), (0, 0)))


def kernel(x, c, w_ada, b_ada, w_in, conv_w, conv_b, rel_bias, w_attn_out, w_conv_out, w_o, ln_g, ln_b, loss_target, m_w_ada, m_b_ada, m_w_in, m_conv_w, m_conv_b, m_rel_bias, m_w_attn_out, m_w_conv_out, m_w_o, m_ln_g, m_ln_b, v_w_ada, v_b_ada, v_w_in, v_conv_w, v_conv_b, v_rel_bias, v_w_attn_out, v_w_conv_out, v_w_o, v_ln_g, v_ln_b):
    bsz, seq, _ = x.shape
    t = bsz * seq
    mx, my, mc = _place()
    chip = 2 * mx + my
    dev = 4 * mx + 2 * my + mc
    x2 = x.reshape(t, D)
    tgt = loss_target.reshape(t, D)

    mine = [_to_bf16_window(a, w[0], f"to_bf16_{a}") for a, w in enumerate((w_in, w_attn_out, w_conv_out, w_o))]

    n_ada = w_ada.shape[2]
    n_cw = conv_w.shape[2]
    c_and_cw = jnp.concatenate([_pad_rows(c), jnp.pad(conv_w[0], ((0, 5), (0, D - n_cw)))], axis=0)
    firsts = _all_gather8(c_and_cw, "gather_c_conv_w")
    c_all = firsts[:, 0:bsz, :].reshape(8 * bsz, D)
    conv_w_f = firsts[0::2, 8:11, 0:n_cw].transpose(1, 0, 2).reshape(3, D)
    b_cols = lax.dynamic_slice(b_ada, (0, chip * n_ada), (1, n_ada))
    mod_part = _ada_forward(c_all, w_ada[0], b_cols)
    mod_parts = _all_gather8(mod_part, "gather_mod")
    mod_all = mod_parts[0::2].transpose(1, 0, 2).reshape(8 * bsz, 3 * D)
    mod = lax.dynamic_slice(mod_all, (dev * bsz, 0), (bsz, 3 * D))
    shift = mod[:, 0:D].reshape(bsz, 1, D)
    sc1p = 1.0 + mod[:, D:2 * D].reshape(bsz, 1, D)
    gate = mod[:, 2 * D:].reshape(bsz, 1, D)

    h, ht = _modulate(x2, sc1p, shift, seq)
    tab = lax.dynamic_index_in_dim(jnp.asarray(_tile_tables()), chip, 0, keepdims=False)
    qkv, gates, (w_in_f, w_ao_f, w_co_f, w_o_f) = _project_gather(h, mine, tab)
    buckets = _band_buckets()
    bias = _bias_tables(rel_bias, buckets)
    og, lg = [], []
    for g in range(3):
        o_g, l_g = _attn_forward(g, qkv, bias[g], bsz, seq)
        og.append(o_g)
        lg.append(l_g)
    (a_in, s_in, merged, dy, a_out, s_out, y_conv, o, lj, dxr, vec_f, dgate) = _mix_forward(
        gates, og, lg, x2, tgt, gate, w_ao_f, w_co_f, w_o_f, conv_w_f, conv_b, ln_g, ln_b, bsz, seq)

    dgates, do, dl, da_out, ds_out, vec_b = _mix_backward(gates, dy, a_out, s_out, y_conv, o, lj, w_ao_f, w_co_f, w_o_f, conv_w_f, bsz, seq)
    g_ao, g_co, g_o = _out_weight_grads(a_in, da_out, s_in, ds_out, merged, dy)
    dqkv, dbs = None, []
    for g in range(3):
        dqkv, db = _attn_backward(g, qkv, do, dl, bias[g], dqkv, bsz, seq)
        dbs.append(db)
    dq, dk, dv = dqkv
    drb = _bias_grad(jnp.stack(dbs), buckets)
    drb = drb[:, :, 0:4].transpose(1, 0, 2).reshape(N_BUCKETS, 12)
    g_in = None
    for n, src in enumerate((dq, dk, dv, dgates)):
        g_in = _in_weight_grad(ht, src, n * NQT, g_in, f"in_weight_grad_{n}")

    grads = [g_in, g_ao, g_co, g_o]
    got = _swap_halves(grads)
    sums = [_chip_sum(a, grads[a], got[a], f"chip_sum_{a}") for a in range(4)]
    grad_x, dshift, dscale, landed = _input_grad(dq, dk, dv, dgates, w_in_f, x2, dxr, sc1p, seq, [s[1] for s in sums])
    halves = [_reduce_mine(a, sums[a][0], landed[a], f"reduce_mine_{a}") for a in range(4)]
    gw_in, gw_ao, gw_co, gw_o = _join_halves(halves)

    dmod = jnp.concatenate([dshift, dscale, dgate], axis=2).reshape(bsz * 3, D)
    drb_row = jnp.pad(drb.reshape(1, N_BUCKETS * 12), ((0, 0), (0, D - N_BUCKETS * 12)))
    packed = jnp.concatenate([vec_f, vec_b, _pad_rows(dmod), _pad_rows(drb_row)], axis=0)
    gathered = _all_gather8(packed, "gather_small")
    small = _sum_partials(gathered)
    g_ln_g, g_ln_b, loss_lanes = small[0:1], small[1:2], small[2:3]
    g_conv_w_full, g_conv_b = small[8:11], small[11:12]
    g_rel_bias = small[24, 0:N_BUCKETS * 12].reshape(N_BUCKETS, 12)
    loss = 0.5 / D * jnp.sum(loss_lanes)
    dmod_all = gathered[:, 16:16 + 3 * bsz, :].reshape(8 * bsz, 3 * D)
    dmod_cols = lax.dynamic_slice(dmod_all, (0, chip * n_ada), (8 * bsz, n_ada))
    gw_ada, gb_ada = _ada_backward(c_all, dmod_cols, dmod_all)
    g_conv_w = lax.dynamic_slice(g_conv_w_full, (0, chip * n_cw), (3, n_cw))

    names = ["w_ada", "b_ada", "w_in", "conv_w", "conv_b", "rel_bias", "w_attn_out", "w_conv_out", "w_o", "ln_g", "ln_b"]
    two_d = lambda a: a.reshape(a.shape[-2:]) if a.ndim == 3 else a
    weights = dict(zip(names, map(two_d, (w_ada, b_ada, w_in, conv_w, conv_b, rel_bias, w_attn_out, w_conv_out, w_o, ln_g, ln_b))))
    ms = dict(zip(names, map(two_d, (m_w_ada, m_b_ada, m_w_in, m_conv_w, m_conv_b, m_rel_bias, m_w_attn_out, m_w_conv_out, m_w_o, m_ln_g, m_ln_b))))
    vs = dict(zip(names, map(two_d, (v_w_ada, v_b_ada, v_w_in, v_conv_w, v_conv_b, v_rel_bias, v_w_attn_out, v_w_conv_out, v_w_o, v_ln_g, v_ln_b))))
    grads = dict(zip(names, (gw_ada, gb_ada, gw_in, g_conv_w, g_conv_b, g_rel_bias, gw_ao, gw_co, gw_o, g_ln_g, g_ln_b)))
    shapes = dict(zip(names, (w_ada, b_ada, w_in, conv_w, conv_b, rel_bias, w_attn_out, w_conv_out, w_o, ln_g, ln_b)))
    deltas, new_m, new_v = {}, {}, {}
    for n in names:
        deltas[n], new_m[n], new_v[n] = _adamw(weights[n], grads[n], ms[n], vs[n], f"adamw_{n}")
    shaped = lambda d: [d[n].reshape(shapes[n].shape) for n in names]
    return (loss, grad_x.reshape(bsz, seq, D), *shaped(grads), *shaped(deltas), *shaped(new_m), *shaped(new_v))
```

```python
import math

import numpy as np
import jax
import jax.numpy as jnp
from jax import lax
from jax.experimental import pallas as pl
from jax.experimental.pallas import tpu as pltpu

F32 = jnp.float32
BF16 = jnp.bfloat16
SDS = jax.ShapeDtypeStruct
MESH = pl.DeviceIdType.MESH
HBM_OUT = pltpu.HBM
ANY = pl.BlockSpec(memory_space=pl.ANY)
VMEM_SPEC = pl.BlockSpec(memory_space=pltpu.VMEM)

D = 1024
HD = 128
BLK = 128
QW = 1536
AW = 512
NGATE = 6656
GATE_COLS = ((0, 512), (512, 1536), (1536, 2560), (2560, 3584), (3584, 4608), (4608, 5632), (5632, 6656))
NCOL = 3 * QW + NGATE
TN = 512
NQT = QW // TN
NPT = NCOL // TN
DILATIONS = (1, 4, 16)
N_BUCKETS, MAX_EXACT, MAX_DISTANCE = 32, 16, 2048
ALPHA = 2.0 ** 0.25
LN_EPS = 1e-5
NEG = -1e30
SCALE = HD ** -0.5
LR, B1, B2, EPS, WD, STEP = 0.001, 0.9, 0.999, 1e-08, 0.01, 10
NCHIP = 4
VMEM_CAP = 60 * 2 ** 20


def _cp(sem=None, vmem=None, side=False):
    return pltpu.CompilerParams(dimension_semantics=sem, vmem_limit_bytes=vmem, has_side_effects=side)


def _dot(a, b):
    return jnp.dot(a, b, preferred_element_type=F32)


def _dot_nt(a, b):
    return lax.dot_general(a, b, (((1,), (1,)), ((), ())), preferred_element_type=F32)


def _dot_tn(a, b):
    return lax.dot_general(a, b, (((0,), (0,)), ((), ())), preferred_element_type=F32)


def _sig(x):
    return 1.0 / (1.0 + jnp.exp(-x))


def _in_hbm(a):
    return pltpu.with_memory_space_constraint(a, pltpu.HBM)


def _place():
    x, y, c = lax.axis_index("x"), lax.axis_index("y"), lax.axis_index("c")
    return x, y, c


def _all_gather8(v, name):
    r, cdim = v.shape

    def body(v_ref, out_ref, send_sems, recv_sems, local_sem):
        x, y, c = _place()
        me = 4 * x + 2 * y + c
        peers = [(x, y, 1 - c), (1 - x, y, c), (x, 1 - y, c), (1 - x, 1 - y, c),
                 (1 - x, y, 1 - c), (x, 1 - y, 1 - c), (1 - x, 1 - y, 1 - c)]
        mine = pltpu.make_async_copy(v_ref, out_ref.at[me], local_sem)
        mine.start()

        def copy(k, block, to):
            return pltpu.make_async_remote_copy(src_ref=v_ref, dst_ref=out_ref.at[block], send_sem=send_sems.at[k],
                                                recv_sem=recv_sems.at[k], device_id=to, device_id_type=MESH)

        sends = [copy(k, me, p) for k, p in enumerate(peers)]
        for cp in sends:
            cp.start()
        for k, (px, py, pc) in enumerate(peers):
            copy(k, 4 * px + 2 * py + pc, (px, py, pc)).wait_recv()
        for cp in sends:
            cp.wait_send()
        mine.wait()

    return pl.pallas_call(
        body, name=name, out_shape=SDS((8, r, cdim), v.dtype), in_specs=[VMEM_SPEC], out_specs=VMEM_SPEC,
        scratch_shapes=[pltpu.SemaphoreType.DMA((7,)), pltpu.SemaphoreType.DMA((7,)), pltpu.SemaphoreType.DMA(())],
        compiler_params=_cp(side=True),
    )(v)


W_CUTS = (("col", D, NCOL // NCHIP), ("col", AW, D // NCHIP), ("row", D // NCHIP, D), ("row", D // NCHIP, D))
W_FULL = ((D, NCOL), (AW, D), (D, D), (D, D))


def _shard_window(ref, cut, k, half):
    kind, nr, nc = cut
    hr = nr // 2
    if kind == "col":
        rows = pl.ds(0, nr) if half is None else pl.ds(pl.multiple_of(half * hr, 16), hr)
        return ref.at[rows, pl.ds(pl.multiple_of(k * nc, 128), nc)]
    if half is None:
        return ref.at[pl.ds(pl.multiple_of(k * nr, 16), nr), :]
    return ref.at[pl.ds(pl.multiple_of(k * nr + half * hr, 16), hr), :]


def _half_rows(ref, cut, half):
    hr = cut[1] // 2
    return ref.at[pl.ds(pl.multiple_of(half * hr, 16), hr), :]


def _to_bf16_window(a, w, name):
    kind, nr, nc = W_CUTS[a]
    x, y, _ = _place()
    chip = jnp.reshape(2 * x + y, (1,)).astype(jnp.int32)
    tr = min(nr, 256)

    def body(c_ref, w_ref, o_ref):
        o_ref[...] = w_ref[...].astype(BF16)

    out_map = (lambda i, cr: (i, cr[0])) if kind == "col" else (lambda i, cr: (cr[0] * (nr // tr) + i, 0))
    return pl.pallas_call(
        body, name=name, out_shape=SDS(W_FULL[a], BF16),
        grid_spec=pltpu.PrefetchScalarGridSpec(num_scalar_prefetch=1, grid=(nr // tr,),
                                               in_specs=[pl.BlockSpec((tr, nc), lambda i, cr: (i, 0))], out_specs=pl.BlockSpec((tr, nc), out_map)),
        compiler_params=_cp(("arbitrary",)),
    )(chip, w)


def _swap_halves(theirs):
    n = len(theirs)

    def body(*refs):
        src, land = refs[:n], refs[n:2 * n]
        send_sems, recv_sems = refs[2 * n:]
        x, y, c = _place()
        copies = [pltpu.make_async_remote_copy(src_ref=src[a], dst_ref=land[a], send_sem=send_sems.at[a], recv_sem=recv_sems.at[a],
                                               device_id=(x, y, 1 - c), device_id_type=MESH) for a in range(n)]
        for cp in copies:
            cp.start()
        for cp in copies:
            cp.wait()

    return pl.pallas_call(
        body, name="swap_grad_halves", out_shape=[SDS(v.shape, v.dtype) for v in theirs], in_specs=[ANY] * n, out_specs=[ANY] * n,
        scratch_shapes=[pltpu.SemaphoreType.DMA((n,)), pltpu.SemaphoreType.DMA((n,))],
        compiler_params=_cp(side=True),
    )(*theirs)


def _chip_sum(a, mine, got, name):
    kind, nr, nc = W_CUTS[a]
    hr = nr // 2
    x, y, _ = _place()
    me = jnp.reshape(2 * x + y, (1,)).astype(jnp.int32)

    def body(me_ref, g_ref, r_ref, f_ref, b_ref):
        s = (g_ref[...] + r_ref[...].astype(F32)).reshape(hr, nc)
        b_ref[0] = s.astype(BF16)

        @pl.when(pl.program_id(0) == me_ref[0])
        def _():
            f_ref[...] = s

    if kind == "col":
        in_spec = pl.BlockSpec((hr, nc), lambda k, mr: (0, k))
    else:
        in_spec = pl.BlockSpec((1, hr, nc), lambda k, mr: (k, 0, 0))
    return pl.pallas_call(
        body, name=name, out_shape=[SDS((hr, nc), F32), SDS((NCHIP, hr, nc), BF16)],
        grid_spec=pltpu.PrefetchScalarGridSpec(
            num_scalar_prefetch=1, grid=(NCHIP,), in_specs=[in_spec, in_spec],
            out_specs=[pl.BlockSpec((hr, nc), lambda k, mr: (0, 0)), pl.BlockSpec((1, hr, nc), lambda k, mr: (k, 0, 0))]),
        compiler_params=_cp(("arbitrary",), VMEM_CAP),
    )(me, mine, got)


def _reduce_mine(a, mine_f32, got, name):
    kind, nr, nc = W_CUTS[a]
    hr = nr // 2
    x, y, c = _place()
    where = jnp.stack([2 * x + y, c]).astype(jnp.int32)
    tr = min(hr, 256)

    def body(w_ref, m_ref, g_ref, o_ref):
        o_ref[...] = ((m_ref[...] + g_ref[0].astype(F32)) + g_ref[1].astype(F32)) + g_ref[2].astype(F32)

    return pl.pallas_call(
        body, name=name, out_shape=SDS((nr, nc), F32),
        grid_spec=pltpu.PrefetchScalarGridSpec(
            num_scalar_prefetch=1, grid=(hr // tr,),
            in_specs=[pl.BlockSpec((tr, nc), lambda i, wr: (i, 0)), pl.BlockSpec((3, tr, nc), lambda i, wr: (0, i, 0))],
            out_specs=pl.BlockSpec((tr, nc), lambda i, wr: (wr[1] * (hr // tr) + i, 0))),
        compiler_params=_cp(("arbitrary",), VMEM_CAP),
    )(where, mine_f32, got)


def _join_halves(fulls):
    n = len(fulls)

    def body(*refs):
        full = refs[n:2 * n]
        send_sems, recv_sems = refs[2 * n:]
        x, y, c = _place()
        sibling = (x, y, 1 - c)

        def swap(a, half):
            rows = _half_rows(full[a], W_CUTS[a], half)
            return pltpu.make_async_remote_copy(src_ref=rows, dst_ref=rows, send_sem=send_sems.at[a], recv_sem=recv_sems.at[a],
                                                device_id=sibling, device_id_type=MESH)

        sends = [swap(a, c) for a in range(n)]
        for cp in sends:
            cp.start()
        for a, cp in enumerate(sends):
            cp.wait_send()
            swap(a, 1 - c).wait_recv()

    return pl.pallas_call(
        body, name="join_grad_halves", out_shape=[SDS((W_CUTS[a][1], W_CUTS[a][2]), F32) for a in range(n)],
        in_specs=[ANY] * n, out_specs=[ANY] * n,
        scratch_shapes=[pltpu.SemaphoreType.DMA((n,)), pltpu.SemaphoreType.DMA((n,))],
        input_output_aliases={a: a for a in range(n)}, compiler_params=_cp(side=True),
    )(*fulls)


def _ada_forward(c_all, w_ada, b_cols):
    nb, nc = c_all.shape[0], w_ada.shape[1]

    def body(c_ref, w_ref, b_ref, o_ref):
        cv = c_ref[...]
        sc = (cv * _sig(cv)).astype(BF16)
        o_ref[...] = _dot(sc, w_ref[...].astype(BF16)) + b_ref[...]

    return pl.pallas_call(body, name="ada_forward", out_shape=SDS((nb, nc), F32), compiler_params=_cp(vmem=VMEM_CAP // 2))(c_all, w_ada, b_cols)


def _ada_backward(c_all, dmod_cols, dmod_all):
    nb, nc = dmod_cols.shape

    def body(c_ref, d_ref, a_ref, gw_ref, gb_ref):
        cv = c_ref[...]
        sc = (cv * _sig(cv)).astype(BF16)
        gw_ref[...] = _dot_tn(sc, d_ref[...].astype(BF16))
        gb_ref[...] = jnp.sum(a_ref[...], axis=0, keepdims=True)

    return pl.pallas_call(body, name="ada_backward", out_shape=[SDS((D, nc), F32), SDS((1, dmod_all.shape[1]), F32)],
                          compiler_params=_cp(vmem=VMEM_CAP // 2))(c_all, dmod_cols, dmod_all)


def _modulate(x2, sc1p, shift, seq, tm=256):
    t = x2.shape[0]
    spt = seq // tm

    def body(x_ref, sc_ref, sh_ref, h_ref, ht_ref):
        h = x_ref[...] * sc_ref[0] + sh_ref[0]
        h_ref[...] = h.astype(BF16)
        ht_ref[...] = h.T.astype(BF16)

    per_seq = pl.BlockSpec((1, 1, D), lambda i: (i // spt, 0, 0))
    return pl.pallas_call(
        body, name="modulate", out_shape=[HBM_OUT((t, D), BF16), HBM_OUT((D, t), BF16)], grid=(t // tm,),
        in_specs=[pl.BlockSpec((tm, D), lambda i: (i, 0)), per_seq, per_seq],
        out_specs=[pl.BlockSpec((tm, D), lambda i: (i, 0)), pl.BlockSpec((D, tm), lambda i: (0, i))],
        compiler_params=_cp(("parallel",)),
    )(x2, sc1p, shift)


TW = 256
TPS = NCOL // NCHIP // TW
NT = NCOL // TW
NQKV_T = 3 * QW // TW


def _tile_tables():
    tabs = np.zeros((NCHIP, 3, NT), np.int32)
    for me in range(NCHIP):
        tiles = [TPS * (me ^ (s // TPS)) + s % TPS for s in range(NT)]
        tabs[me, 0] = tiles
        for row, (lo, hi) in enumerate(((0, NQKV_T), (NQKV_T, NT))):
            mine = [w - lo if lo <= w < hi else None for w in tiles]
            held = next(m for m in mine if m is not None)
            for s, m in enumerate(mine):
                held = held if m is None else m
                tabs[me, 1 + row, s] = held
    return tabs


def _project_gather(h, fulls, tab):
    t = h.shape[0]
    n = len(fulls)

    def body(tab_ref, h_ref, *rest):
        qkv_ref, g_ref = rest[n], rest[n + 1]
        full = rest[n + 2:2 * n + 2]
        w_buf, tile_sems, send_sems, recv_sems = rest[2 * n + 2:]
        s = pl.program_id(0)
        x, y, c = _place()
        me = 2 * x + y
        peers = [(x, 1 - y), (1 - x, y), (1 - x, 1 - y)]
        sibling = (x, y, 1 - c)

        def hop(a, r, stage, chip, half, to):
            window = _shard_window(full[a], W_CUTS[a], chip, half)
            k = 6 * a + 2 * r + stage
            return pltpu.make_async_remote_copy(src_ref=window, dst_ref=window, send_sem=send_sems.at[k], recv_sem=recv_sems.at[k],
                                                device_id=to, device_id_type=MESH)

        def send(a, r):
            return hop(a, r, 0, me, c, (*peers[r], c))

        def arrive(a, r):
            px, py = peers[r]
            chip = 2 * px + py
            hop(a, r, 0, chip, c, (px, py, c)).wait_recv()
            hop(a, r, 1, chip, c, sibling).start()
            hop(a, r, 1, chip, 1 - c, sibling).wait_recv()

        def tile(step, slot):
            col = pl.multiple_of(tab_ref[0, step] * TW, TW)
            return pltpu.make_async_copy(full[0].at[:, pl.ds(col, TW)], w_buf.at[slot], tile_sems.at[slot])

        @pl.when(s == 0)
        def _():
            send(0, 0).start()
            send(0, 1).start()
            tile(0, 0).start()

        slot = s % 2
        tile(s, slot).wait()

        @pl.when((s + 1 < NT) & ((s + 1) % TPS != 0))
        def _():
            tile(s + 1, 1 - slot).start()

        is_qkv = tab_ref[0, s] < NQKV_T
        for k in range(2):
            @pl.when(slot == k)
            def _(k=k):
                acc = _dot(h_ref[...], w_buf[k])

                @pl.when(is_qkv)
                def _():
                    qkv_ref[...] = acc.astype(BF16)

                @pl.when(jnp.logical_not(is_qkv))
                def _():
                    g_ref[...] = acc.astype(BF16)

        for r in range(3):
            @pl.when(s + 1 == TPS * (r + 1))
            def _(r=r):
                arrive(0, r)
                tile(s + 1, 1 - slot).start()
                if r == 0:
                    send(0, 2).start()
                    for a in range(1, n):
                        for q in range(3):
                            send(a, q).start()

        @pl.when(s == NT - 1)
        def _():
            for a in range(1, n):
                for r in range(3):
                    arrive(a, r)
            for a in range(n):
                for r in range(3):
                    send(a, r).wait_send()
                    px, py = peers[r]
                    hop(a, r, 1, 2 * px + py, c, sibling).wait_send()

    outs = pl.pallas_call(
        body, name="project_gather", out_shape=[HBM_OUT((t, 3 * QW), BF16), HBM_OUT((t, NGATE), BF16)] + [SDS(s, BF16) for s in W_FULL],
        grid_spec=pltpu.PrefetchScalarGridSpec(
            num_scalar_prefetch=1, grid=(NT,),
            in_specs=[pl.BlockSpec((t, D), lambda s, tab: (0, 0))] + [ANY] * n,
            out_specs=[pl.BlockSpec((t, TW), lambda s, tab: (0, tab[1, s])), pl.BlockSpec((t, TW), lambda s, tab: (0, tab[2, s]))] + [ANY] * n,
            scratch_shapes=[pltpu.VMEM((2, D, TW), BF16), pltpu.SemaphoreType.DMA((2,)),
                            pltpu.SemaphoreType.DMA((6 * n,)), pltpu.SemaphoreType.DMA((6 * n,))]),
        input_output_aliases={2 + a: 2 + a for a in range(n)},
        compiler_params=_cp(("arbitrary",), VMEM_CAP, side=True),
    )(tab, _in_hbm(h), *fulls)
    return outs[0], outs[1], outs[2:]


def _bias_tables(rel_bias, buckets):
    def body(tab_ref, bk_ref, o_ref):
        a = lax.broadcasted_iota(jnp.int32, (BLK, 2 * BLK), 0)
        b = lax.broadcasted_iota(jnp.int32, (BLK, 2 * BLK), 1)
        steps = a + BLK - b
        valid = (steps >= 0) & (steps <= BLK)
        for g in range(3):
            bk = bk_ref[g]
            for j in range(4):
                def pick(kk, acc, bk=bk, col=4 * g + j):
                    return jnp.where(bk == kk, tab_ref[kk, col], acc)

                acc = lax.fori_loop(0, N_BUCKETS, pick, jnp.zeros((BLK, 2 * BLK), F32))
                o_ref[g, j] = jnp.where(valid, acc, NEG)

    return pl.pallas_call(
        body, name="bias_tables", out_shape=SDS((3, 4, BLK, 2 * BLK), F32),
        in_specs=[pl.BlockSpec(memory_space=pltpu.SMEM), VMEM_SPEC], out_specs=VMEM_SPEC,
    )(rel_bias, buckets)


def _bias_grad(ds_sum, buckets):
    def body(ds_ref, bk_ref, o_ref):
        lane = lax.broadcasted_iota(jnp.int32, (1, 128), 1)
        for g in range(3):
            def bucket(kk, carry, g=g):
                row = jnp.zeros((1, 128), F32)
                for j in range(4):
                    v = jnp.where(bk_ref[g] == kk, ds_ref[g, j], 0.0)
                    s = jnp.sum(jnp.sum(v, axis=1, keepdims=True), axis=0, keepdims=True)
                    row = jnp.where(lane == j, s, row)
                o_ref[g, pl.ds(kk, 1), :] = row
                return carry

            lax.fori_loop(0, N_BUCKETS, bucket, 0)

    return pl.pallas_call(body, name="bias_grad", out_shape=SDS((3, N_BUCKETS, 128), F32), in_specs=[VMEM_SPEC, VMEM_SPEC],
                          out_specs=VMEM_SPEC)(ds_sum, buckets)


def _sub_rows(d, r, first, size):
    return pl.ds(first * d + r, size) if d == 1 else pl.ds(first * d + r, size, stride=d)


def _head_spec(seq, g, part):
    return pl.BlockSpec((seq, HD), lambda b, hh: (b, part * (QW // HD) + 4 * g + hh))


def _rows(start, count, stride):
    return pl.ds(start, count) if stride == 1 else pl.ds(start, count, stride=stride)


def _gather_rows(dst, dst0, src, src0, stride, count):
    for first in range(0, count, BLK):
        dst[pl.ds(dst0 + first, BLK), :] = src[_rows(src0 + first * stride, BLK, stride), :].astype(dst.dtype)


def _scatter_rows(dst, dst0, stride, src, src0, count):
    for first in range(0, count, BLK):
        dst[_rows(dst0 + first * stride, BLK, stride), :] = src[pl.ds(src0 + first, BLK), :].astype(dst.dtype)


def _by_subsequence(dst, src, d, wide=None, tmp=None):
    seq = src.shape[0]
    ln = seq // d
    if wide is not None:
        wide[...] = src[...].astype(F32)
        src = wide
    if d <= 4:
        for r in range(d):
            _gather_rows(dst, r * ln, src, r, d, ln)
    else:
        quarter = seq // 4
        for r4 in range(4):
            _gather_rows(tmp, r4 * quarter, src, r4, 4, quarter)
        for r4 in range(4):
            for a in range(d // 4):
                _gather_rows(dst, (4 * a + r4) * ln, tmp, r4 * quarter + a, d // 4, ln)


def _to_sequence(dst, src, d, tmp=None):
    seq = dst.shape[0]
    ln = seq // d
    if d <= 4:
        for r in range(d):
            _scatter_rows(dst, r, d, src, r * ln, ln)
    else:
        quarter = seq // 4
        for r4 in range(4):
            for a in range(d // 4):
                _scatter_rows(tmp, r4 * quarter + a, d // 4, src, (4 * a + r4) * ln, ln)
        for r4 in range(4):
            _scatter_rows(dst, r4, 4, tmp, r4 * quarter, quarter)


def _attn_forward(g, qkv, bias, bsz, seq):
    d = DILATIONS[g]
    ln = seq // d
    units = [(r, n) for r in range(d) for n in range(ln // BLK)]

    def band(n):
        return slice(BLK, 2 * BLK) if n == 0 else slice(0, 2 * BLK)

    def body(q_ref, k_ref, v_ref, b_ref, o_ref, l_ref, *scratch):
        hs = pl.program_id(1)
        s_scr, p_scr = scratch[:2]
        if d == 1:
            qd, kd, vd = q_ref, k_ref, v_ref
        else:
            wide, tmp, qd, kd, vd = scratch[2:7]
            for dst, src in ((qd, q_ref), (kd, k_ref), (vd, v_ref)):
                _by_subsequence(dst, src, d, wide, tmp)
        blk = lambda r, n: pl.ds(r * ln + n * BLK, BLK)
        direct = d <= 4
        out_rows = (lambda r, n: _sub_rows(d, r, n * BLK, BLK)) if direct else blk
        o_dst, l_dst = (o_ref, l_ref) if direct else scratch[7:9]
        for u, (r, n) in enumerate(units):
            s_scr[u, :, BLK:] = _dot_nt(qd[blk(r, n), :], kd[blk(r, n), :])
            if n > 0:
                s_scr[u, :, :BLK] = _dot_nt(qd[blk(r, n), :], kd[blk(r, n - 1), :])
        for u, (r, n) in enumerate(units):
            s = s_scr[u, :, band(n)] * SCALE + b_ref[hs, :, band(n)]
            m = jnp.max(s, axis=1, keepdims=True)
            e = jnp.exp(s - m)
            den = jnp.sum(e, axis=1, keepdims=True)
            p_scr[u, :, band(n)] = (e * (1.0 / den)).astype(BF16)
            l_dst[out_rows(r, n), :] = jnp.broadcast_to(m + jnp.log(den), (BLK, HD))
        for u, (r, n) in enumerate(units):
            acc = _dot(p_scr[u, :, BLK:], vd[blk(r, n), :])
            if n > 0:
                acc = acc + _dot(p_scr[u, :, :BLK], vd[blk(r, n - 1), :])
            o_dst[out_rows(r, n), :] = acc
        if not direct:
            _to_sequence(o_ref, o_dst, d, tmp)
            _to_sequence(l_ref, l_dst, d, tmp)

    rows_f32, rows_bf16 = pltpu.VMEM((seq, HD), F32), pltpu.VMEM((seq, HD), BF16)
    regrouped = [] if d == 1 else [rows_f32] * 2 + [rows_bf16] * 3 + ([] if d <= 4 else [rows_f32] * 2)
    out_spec = pl.BlockSpec((seq, HD), lambda b, hh: (b, hh))
    return pl.pallas_call(
        body, name=f"attn_forward_{g}", out_shape=[HBM_OUT((bsz * seq, AW), F32)] * 2, grid=(bsz, 4),
        in_specs=[_head_spec(seq, g, part) for part in range(3)] + [pl.BlockSpec((4, BLK, 2 * BLK), lambda b, hh: (0, 0, 0))],
        out_specs=[out_spec, out_spec],
        scratch_shapes=[pltpu.VMEM((len(units), BLK, 2 * BLK), F32), pltpu.VMEM((len(units), BLK, 2 * BLK), BF16)] + regrouped,
        compiler_params=_cp(("parallel", "parallel"), VMEM_CAP // 2),
    )(qkv, qkv, qkv, _in_hbm(bias))


def _attn_backward(g, qkv, do, dl, bias, prev_out, bsz, seq):
    d = DILATIONS[g]
    ln = seq // d
    units = [(r, n) for r in range(d) for n in range(ln // BLK)]

    def body(q_ref, k_ref, v_ref, do_ref, dl_ref, b_ref, *rest):
        dq_ref, dk_ref, dv_ref, db_ref = rest[-18:-14]
        wide, tmp, qd, kd, vd, dod, dld, dqd, dkd, dvd, s_scr, dp_scr, p_scr, ds_scr = rest[-14:]
        hs = pl.program_id(1)

        @pl.when((pl.program_id(0) == 0) & (hs == 0))
        def _():
            db_ref[...] = jnp.zeros_like(db_ref)

        for dst, src in ((qd, q_ref), (kd, k_ref), (vd, v_ref)):
            _by_subsequence(dst, src, d, wide, tmp)
        _by_subsequence(dod, do_ref, d, None, tmp)
        _by_subsequence(dld, dl_ref, d, None, tmp)
        dkd[...] = jnp.zeros_like(dkd)
        dvd[...] = jnp.zeros_like(dvd)
        blk = lambda r, n: pl.ds(r * ln + n * BLK, BLK)
        keys = lambda r, n: [(blk(r, n), slice(BLK, 2 * BLK))] + ([(blk(r, n - 1), slice(0, BLK))] if n > 0 else [])
        for u, (r, n) in enumerate(units):
            for rows, band in keys(r, n):
                s_scr[u, :, band] = _dot_nt(qd[blk(r, n), :], kd[rows, :])
                dp_scr[u, :, band] = _dot_nt(dod[blk(r, n), :], vd[rows, :])
        for u, (r, n) in enumerate(units):
            both = dld[blk(r, n), :]
            lse, delta = both[:, 0:1], both[:, 64:65]
            band = slice(BLK, 2 * BLK) if n == 0 else slice(0, 2 * BLK)
            p = jnp.exp(s_scr[u, :, band] * SCALE + b_ref[hs, :, band] - lse)
            ds = p * (dp_scr[u, :, band] - delta)
            p_scr[u, :, band] = p.astype(BF16)
            ds_scr[u, :, band] = ds.astype(BF16)
            db_ref[hs, :, band] += ds
        for u, (r, n) in enumerate(units):
            dq = jnp.zeros((BLK, HD), F32)
            for rows, band in keys(r, n):
                dvd[rows, :] += _dot_tn(p_scr[u, :, band], dod[blk(r, n), :])
                dkd[rows, :] += _dot_tn(ds_scr[u, :, band], qd[blk(r, n), :]) * SCALE
                dq = dq + _dot(ds_scr[u, :, band], kd[rows, :])
            dqd[blk(r, n), :] = dq * SCALE
        for out, acc in ((dq_ref, dqd), (dk_ref, dkd), (dv_ref, dvd)):
            if d == 1:
                out[...] = acc[...].astype(BF16)
            else:
                _to_sequence(wide, acc, d, tmp)
                out[...] = wide[...].astype(BF16)

    qkv_spec = _head_spec(seq, g, 0)
    out_spec = pl.BlockSpec((seq, HD), lambda b, hh: (b, hh))
    band_spec = pl.BlockSpec((4, BLK, 2 * BLK), lambda b, hh: (0, 0, 0))
    ins = [qkv, qkv, qkv, _in_hbm(do), _in_hbm(dl), _in_hbm(bias)]
    in_specs = [_head_spec(seq, g, part) for part in range(3)] + [out_spec, out_spec, band_spec]
    aliases = {}
    if prev_out is not None:
        ins += list(prev_out)
        in_specs += [ANY] * 3
        aliases = {6: 0, 7: 1, 8: 2}
    rows_bf16, rows_f32 = pltpu.VMEM((seq, HD), BF16), pltpu.VMEM((seq, HD), F32)
    staged = [pltpu.VMEM((len(units), BLK, 2 * BLK), F32)] * 2 + [pltpu.VMEM((len(units), BLK, 2 * BLK), BF16)] * 2
    dq, dk, dv, db = pl.pallas_call(
        body, name=f"attn_backward_{g}", out_shape=[HBM_OUT((bsz * seq, QW), BF16)] * 3 + [SDS((4, BLK, 2 * BLK), F32)], grid=(bsz, 4),
        in_specs=in_specs, out_specs=[qkv_spec] * 3 + [band_spec], input_output_aliases=aliases,
        scratch_shapes=[rows_f32] * 2 + [rows_bf16] * 4 + [rows_f32] * 4 + staged,
        compiler_params=_cp(("arbitrary", "arbitrary"), VMEM_CAP // 2),
    )(*ins)
    return (dq, dk, dv), db


def _mix_forward(gates, og, lg, x2, tgt, gate, w_ao, w_co, w_o, conv_w, conv_b, ln_g, ln_b, bsz, seq, tm=256):
    t = x2.shape[0]
    spt = seq // tm

    def body(g_ref, o1, o2, o3, l1, l2, l3, x_ref, t_ref, gate_ref, wao_ref, wco_ref, wo_ref, cw_ref, cb_ref, lng_ref, lnb_ref,
             ain_ref, sin_ref, mrg_ref, dy_ref, aout_ref, sout_ref, yc_ref, o_ref, lj_ref, dxr_ref, vec_ref, dgate_ref, zc_ref):
        b, i = pl.program_id(0), pl.program_id(1)

        @pl.when((b == 0) & (i == 0))
        def _():
            vec_ref[...] = jnp.zeros_like(vec_ref)

        @pl.when(i == 0)
        def _():
            zc_ref[...] = jnp.zeros_like(zc_ref)
            dgate_ref[...] = jnp.zeros_like(dgate_ref)

        g_attn, u, bg, cg, g_conv, m_attn, m_conv = (g_ref[:, lo:hi].astype(F32) for lo, hi in GATE_COLS)
        la, lb, lc = l1[...], l2[...], l3[...]
        mx = jnp.maximum(la, jnp.maximum(lb, lc))
        ea, eb, ec = jnp.exp(la - mx), jnp.exp(lb - mx), jnp.exp(lc - mx)
        den = ea + eb + ec
        o = (ea * o1[...] + eb * o2[...] + ec * o3[...]) / den
        o_ref[...] = o
        lj_ref[...] = mx + jnp.log(den)
        a_in = o * (g_attn * _sig(g_attn))
        ain_ref[...] = a_in.astype(BF16)
        a_out = _dot(a_in.astype(BF16), wao_ref[...])
        aout_ref[...] = a_out.astype(BF16)
        z = cg * u
        rows = lax.broadcasted_iota(jnp.int32, (tm, D), 0)
        c6, c7 = zc_ref[6:7, :], zc_ref[7:8, :]
        z1 = jnp.where(rows == 0, c7, pltpu.roll(z, 1, 0))
        z2 = jnp.where(rows == 0, c6, jnp.where(rows == 1, c7, pltpu.roll(z, 2, 0)))
        zc_ref[...] = z[tm - 8:tm, :]
        y_conv = (cw_ref[0:1, :] * z2 + cw_ref[1:2, :] * z1 + cw_ref[2:3, :] * z) + cb_ref[...]
        yc_ref[...] = y_conv.astype(BF16)
        s_in = bg * y_conv * (g_conv * _sig(g_conv))
        sin_ref[...] = s_in.astype(BF16)
        s_out = _dot(s_in.astype(BF16), wco_ref[...])
        sout_ref[...] = s_out.astype(BF16)
        merged = _sig(m_attn) * a_out + _sig(m_conv) * s_out
        mrg_ref[...] = merged.astype(BF16)
        y = _dot(merged.astype(BF16), wo_ref[...])
        gate1 = 1.0 + gate_ref[0]
        r = ALPHA * x_ref[...] + gate1 * y
        mu = jnp.mean(r, axis=1, keepdims=True)
        rc = r - mu
        rstd = lax.rsqrt(jnp.mean(rc * rc, axis=1, keepdims=True) + LN_EPS)
        xhat = rc * rstd
        diff = (xhat * lng_ref[...] + lnb_ref[...]) - t_ref[...]
        dout = diff * (1.0 / D)
        vec_ref[0:1, :] += jnp.sum(dout * xhat, axis=0, keepdims=True)
        vec_ref[1:2, :] += jnp.sum(dout, axis=0, keepdims=True)
        vec_ref[2:3, :] += jnp.sum(diff * diff, axis=0, keepdims=True)
        dxh = dout * lng_ref[...]
        dr = rstd * (dxh - jnp.mean(dxh, axis=1, keepdims=True) - xhat * jnp.mean(dxh * xhat, axis=1, keepdims=True))
        dxr_ref[...] = ALPHA * dr
        dy_ref[...] = (dr * gate1).astype(BF16)
        dgate_ref[0] += jnp.sum(dr * y, axis=0, keepdims=True)

    tok = lambda w: pl.BlockSpec((tm, w), lambda b, i: (b * spt + i, 0))
    const = lambda s: pl.BlockSpec(s, lambda b, i: (0,) * len(s))
    per_seq = pl.BlockSpec((1, 1, D), lambda b, i: (b, 0, 0))
    outs = pl.pallas_call(
        body, name="mix_forward", grid=(bsz, spt),
        out_shape=[HBM_OUT((t, AW), BF16), HBM_OUT((t, D), BF16), HBM_OUT((t, D), BF16), HBM_OUT((t, D), BF16), HBM_OUT((t, D), BF16),
                   HBM_OUT((t, D), BF16), HBM_OUT((t, D), BF16), HBM_OUT((t, AW), F32), HBM_OUT((t, AW), F32), HBM_OUT((t, D), F32),
                   SDS((8, D), F32), SDS((bsz, 1, D), F32)],
        in_specs=[tok(NGATE)] + [tok(AW)] * 6 + [tok(D), tok(D), per_seq, const((AW, D)), const((D, D)), const((D, D)),
                                                 const((3, D)), const((1, D)), const((1, D)), const((1, D))],
        out_specs=[tok(AW), tok(D), tok(D), tok(D), tok(D), tok(D), tok(D), tok(AW), tok(AW), tok(D), const((8, D)), per_seq],
        scratch_shapes=[pltpu.VMEM((8, D), F32)],
        compiler_params=_cp(("arbitrary", "arbitrary"), VMEM_CAP),
    )(gates, *map(_in_hbm, og), *map(_in_hbm, lg), x2, tgt, gate, w_ao, w_co, w_o, conv_w, conv_b, ln_g, ln_b)
    return outs


def _mix_backward(gates, dy, a_out, s_out, y_conv, o, lj, w_ao, w_co, w_o, conv_w, bsz, seq, tm=256):
    t = dy.shape[0]
    spt = seq // tm

    def body(g_ref, dy_ref, aout_ref, sout_ref, yc_ref, o_ref, lj_ref, wao_ref, wco_ref, wo_ref, cw_ref,
             dg_ref, do_ref, dl_ref, daout_ref, dsout_ref, vec_ref, car_ref):
        b, i = pl.program_id(0), pl.program_id(1)

        @pl.when((b == 0) & (i == 0))
        def _():
            vec_ref[...] = jnp.zeros_like(vec_ref)

        @pl.when(i == 0)
        def _():
            car_ref[...] = jnp.zeros_like(car_ref)

        g_attn, u, bg, cg, g_conv, m_attn, m_conv = (g_ref[:, lo:hi].astype(F32) for lo, hi in GATE_COLS)
        dmerged = _dot_nt(dy_ref[...], wo_ref[...])
        sa, sc = _sig(m_attn), _sig(m_conv)
        da_out = (dmerged * sa).astype(BF16)
        ds_out = (dmerged * sc).astype(BF16)
        daout_ref[...] = da_out
        dsout_ref[...] = ds_out
        dg_ref[:, 4608:5632] = (dmerged * aout_ref[...].astype(F32) * (sa * (1.0 - sa))).astype(BF16)
        dg_ref[:, 5632:6656] = (dmerged * sout_ref[...].astype(F32) * (sc * (1.0 - sc))).astype(BF16)
        da_in = _dot_nt(da_out, wao_ref[...])
        ds_in = _dot_nt(ds_out, wco_ref[...])
        sga = _sig(g_attn)
        o = o_ref[...]
        do = da_in * (g_attn * sga)
        do_ref[...] = do
        dg_ref[:, 0:512] = (da_in * o * (sga * (1.0 + g_attn * (1.0 - sga)))).astype(BF16)
        prod = do * o
        lane = lax.broadcasted_iota(jnp.int32, (tm, HD), 1)
        for j in range(4):
            cs = slice(j * HD, (j + 1) * HD)
            delta = jnp.sum(prod[:, cs], axis=1, keepdims=True)
            dl_ref[:, cs] = jnp.where(lane < 64, lj_ref[:, cs], delta)
        sgc = _sig(g_conv)
        silu_c = g_conv * sgc
        yc = yc_ref[...].astype(F32)
        dg_ref[:, 1536:2560] = (ds_in * yc * silu_c).astype(BF16)
        dg_ref[:, 3584:4608] = (ds_in * bg * yc * (sgc * (1.0 + g_conv * (1.0 - sgc)))).astype(BF16)
        dyc = ds_in * bg * silu_c
        rows = lax.broadcasted_iota(jnp.int32, (tm, D), 0)
        c0, c1 = car_ref[0:1, :], car_ref[1:2, :]
        n1 = jnp.where(rows == tm - 1, c0, pltpu.roll(dyc, tm - 1, 0))
        n2 = jnp.where(rows == tm - 2, c0, jnp.where(rows == tm - 1, c1, pltpu.roll(dyc, tm - 2, 0)))
        car_ref[...] = dyc[0:8, :]
        dz = cw_ref[2:3, :] * dyc + cw_ref[1:2, :] * n1 + cw_ref[0:1, :] * n2
        z = cg * u
        dg_ref[:, 512:1536] = (dz * cg).astype(BF16)
        dg_ref[:, 2560:3584] = (dz * u).astype(BF16)
        vec_ref[0:1, :] += jnp.sum(n2 * z, axis=0, keepdims=True)
        vec_ref[1:2, :] += jnp.sum(n1 * z, axis=0, keepdims=True)
        vec_ref[2:3, :] += jnp.sum(dyc * z, axis=0, keepdims=True)
        vec_ref[3:4, :] += jnp.sum(dyc, axis=0, keepdims=True)

    tok = lambda w: pl.BlockSpec((tm, w), lambda b, i: (b * spt + (spt - 1 - i), 0))
    const = lambda s: pl.BlockSpec(s, lambda b, i: (0,) * len(s))
    return pl.pallas_call(
        body, name="mix_backward", grid=(bsz, spt),
        out_shape=[HBM_OUT((t, NGATE), BF16), HBM_OUT((t, AW), F32), HBM_OUT((t, AW), F32), HBM_OUT((t, D), BF16), HBM_OUT((t, D), BF16),
                   SDS((8, D), F32)],
        in_specs=[tok(NGATE), tok(D), tok(D), tok(D), tok(D), tok(AW), tok(AW), const((AW, D)), const((D, D)), const((D, D)), const((3, D))],
        out_specs=[tok(NGATE), tok(AW), tok(AW), tok(D), tok(D), const((8, D))],
        scratch_shapes=[pltpu.VMEM((8, D), F32)],
        compiler_params=_cp(("arbitrary", "arbitrary"), VMEM_CAP),
    )(gates, dy, a_out, s_out, y_conv, o, lj, w_ao, w_co, w_o, conv_w)


def _halves_out(a):
    kind, nr, nc = W_CUTS[a]
    shape = (nr // 2, W_FULL[a][1]) if kind == "col" else (NCHIP, nr // 2, nc)
    return [SDS(shape, F32), SDS(shape, BF16)]


def _write_halves(a, acc_ref, c, mine_ref, theirs_ref):
    kind, nr, nc = W_CUTS[a]
    hr = nr // 2
    if kind == "col":
        mine_ref[...] = acc_ref[pl.ds(pl.multiple_of(c * hr, hr), hr), :]
        theirs_ref[...] = acc_ref[pl.ds(pl.multiple_of((1 - c) * hr, hr), hr), :].astype(BF16)
    else:
        for k in range(NCHIP):
            mine_ref[k] = acc_ref[pl.ds(pl.multiple_of(k * nr + c * hr, hr), hr), :]
            theirs_ref[k] = acc_ref[pl.ds(pl.multiple_of(k * nr + (1 - c) * hr, hr), hr), :].astype(BF16)


def _out_weight_grads(a_in, da_out, s_in, ds_out, merged, dy, core, tk=512):
    t = dy.shape[0]
    nt = t // tk

    def body(c_ref, ain_ref, da_ref, sin_ref, ds_ref, m_ref, dy_ref, *rest):
        outs, (gao, gco, go) = rest[:6], rest[6:]

        @pl.when(pl.program_id(0) == 0)
        def _():
            gao[...] = jnp.zeros_like(gao)
            gco[...] = jnp.zeros_like(gco)
            go[...] = jnp.zeros_like(go)

        gao[...] += _dot_tn(ain_ref[...], da_ref[...])
        gco[...] += _dot_tn(sin_ref[...], ds_ref[...])
        go[...] += _dot_tn(m_ref[...], dy_ref[...])

        @pl.when(pl.program_id(0) == nt - 1)
        def _():
            for a, acc in ((1, gao), (2, gco), (3, go)):
                _write_halves(a, acc, c_ref[0], outs[2 * a - 2], outs[2 * a - 1])

    tok = lambda w: pl.BlockSpec((tk, w), lambda i, cr: (i, 0))
    out_shape = _halves_out(1) + _halves_out(2) + _halves_out(3)
    outs = pl.pallas_call(
        body, name="out_weight_grads", out_shape=out_shape,
        grid_spec=pltpu.PrefetchScalarGridSpec(
            num_scalar_prefetch=1, grid=(nt,), in_specs=[tok(AW), tok(D), tok(D), tok(D), tok(D), tok(D)],
            out_specs=[pl.BlockSpec(o.shape, lambda i, cr, nd=len(o.shape): (0,) * nd) for o in out_shape],
            scratch_shapes=[pltpu.VMEM((AW, D), F32), pltpu.VMEM((D, D), F32), pltpu.VMEM((D, D), F32)]),
        compiler_params=_cp(("arbitrary",), VMEM_CAP),
    )(core, a_in, da_out, s_in, ds_out, merged, dy)
    return [(outs[0], outs[1]), (outs[2], outs[3]), (outs[4], outs[5])]


def _input_grad(dq, dk, dv, dgates, w, x2, dxr, sc1p, seq, sums, tm=512):
    t = x2.shape[0]
    nt = t // tm
    spt = seq // tm
    bsz = t // seq
    n = len(sums)
    gblk = NGATE // 4
    nsteps = 3 + 4

    def body(dq_ref, dk_ref, dv_ref, dg_ref, wq_ref, wg_ref, x_ref, dxr_ref, sc_ref, *rest):
        src, (dx_ref, dsh_ref, dsc_ref), land = rest[:n], rest[n:n + 3], rest[n + 3:2 * n + 3]
        acc_ref, send_sems, recv_sems = rest[2 * n + 3:]
        j, i = pl.program_id(0), pl.program_id(1)
        px, py, pc = _place()
        chips = [(1 - px, py), (px, 1 - py), (1 - px, 1 - py)]
        copies = [pltpu.make_async_remote_copy(src_ref=src[a].at[2 * cx + cy], dst_ref=land[a].at[r], send_sem=send_sems.at[3 * a + r],
                                               recv_sem=recv_sems.at[3 * a + r], device_id=(cx, cy, pc), device_id_type=MESH)
                  for a in range(n) for r, (cx, cy) in enumerate(chips)]
        rows = pl.ds(pl.multiple_of(i * tm, tm), tm)

        @pl.when((i == 0) & (j == 0))
        def _():
            for cp in copies:
                cp.start()

        for k, ref in enumerate((dq_ref, dk_ref, dv_ref)):
            @pl.when(j == k)
            def _(k=k, ref=ref):
                part = _dot_nt(ref[...], wq_ref[...])
                if k == 0:
                    acc_ref[rows, :] = part
                else:
                    acc_ref[rows, :] += part

        @pl.when((j >= 3) & (j < nsteps - 1))
        def _():
            acc_ref[rows, :] += _dot_nt(dg_ref[...], wg_ref[...])

        @pl.when(j == nsteps - 1)
        def _():
            dh = acc_ref[rows, :] + _dot_nt(dg_ref[...], wg_ref[...])
            dx_ref[...] = dh * sc_ref[0] + dxr_ref[...]

            @pl.when(i % spt == 0)
            def _():
                dsh_ref[...] = jnp.zeros_like(dsh_ref)
                dsc_ref[...] = jnp.zeros_like(dsc_ref)

            dsh_ref[0] += jnp.sum(dh, axis=0, keepdims=True)
            dsc_ref[0] += jnp.sum(dh * x_ref[...], axis=0, keepdims=True)

        @pl.when((i == nt - 1) & (j == nsteps - 1))
        def _():
            for cp in copies:
                cp.wait()

    def held(k):
        return lambda j, i: (jnp.where(j == k, i, jnp.where(j < k, 0, nt - 1)), 0)

    last = lambda j, i: (jnp.where(j == nsteps - 1, i, 0), 0)
    outs = pl.pallas_call(
        body, name="input_grad", grid=(nsteps, nt),
        out_shape=[SDS((t, D), F32), SDS((bsz, 1, D), F32), SDS((bsz, 1, D), F32)] + [SDS((3,) + s.shape[1:], BF16) for s in sums],
        in_specs=[pl.BlockSpec((tm, QW), held(0)), pl.BlockSpec((tm, QW), held(1)), pl.BlockSpec((tm, QW), held(2)),
                  pl.BlockSpec((tm, gblk), lambda j, i: (jnp.where(j >= 3, i, 0), jnp.clip(j - 3, 0, 3))),
                  pl.BlockSpec((D, QW), lambda j, i: (0, jnp.minimum(j, 2))),
                  pl.BlockSpec((pl.Element(D), pl.Element(gblk)), lambda j, i: (0, pl.multiple_of(3 * QW + gblk * jnp.clip(j - 3, 0, 3), 128))),
                  pl.BlockSpec((tm, D), last), pl.BlockSpec((tm, D), last),
                  pl.BlockSpec((1, 1, D), lambda j, i: (jnp.where(j == nsteps - 1, i // spt, 0), 0, 0))] + [ANY] * n,
        out_specs=[pl.BlockSpec((tm, D), last),
                   pl.BlockSpec((1, 1, D), lambda j, i: (jnp.where(j == nsteps - 1, i // spt, 0), 0, 0)),
                   pl.BlockSpec((1, 1, D), lambda j, i: (jnp.where(j == nsteps - 1, i // spt, 0), 0, 0))] + [ANY] * n,
        scratch_shapes=[pltpu.VMEM((t, D), F32), pltpu.SemaphoreType.DMA((3 * NCHIP,)), pltpu.SemaphoreType.DMA((3 * NCHIP,))],
        compiler_params=_cp(("arbitrary", "arbitrary"), VMEM_CAP, side=True),
    )(dq, dk, dv, dgates, w, w, x2, dxr, sc1p, *sums)
    return outs[0], outs[1], outs[2], outs[3:]


def _in_weight_grad(ht, src, col0, prev, core, name):
    t = ht.shape[1]
    ncols = src.shape[1] // TN
    hr = D // 2

    def body(c_ref, ht_ref, s_ref, *rest):
        mine_ref, theirs_ref, acc_ref = rest[-3:]
        acc_ref[...] = _dot(ht_ref[...], s_ref[...])
        _write_halves(0, acc_ref, c_ref[0], mine_ref, theirs_ref)

    ins = [core, ht, src]
    in_specs = [pl.BlockSpec((D, t), lambda j, cr: (0, 0)), pl.BlockSpec((t, TN), lambda j, cr: (0, j))]
    aliases = {}
    if prev is not None:
        ins += list(prev)
        in_specs += [ANY] * 2
        aliases = {3: 0, 4: 1}
    out_spec = pl.BlockSpec((hr, TN), lambda j, cr: (0, col0 + j))
    return pl.pallas_call(
        body, name=name, out_shape=[SDS((hr, NCOL), F32), SDS((hr, NCOL), BF16)],
        grid_spec=pltpu.PrefetchScalarGridSpec(num_scalar_prefetch=1, grid=(ncols,), in_specs=in_specs, out_specs=[out_spec, out_spec],
                                               scratch_shapes=[pltpu.VMEM((D, TN), F32)]),
        input_output_aliases=aliases, compiler_params=_cp(("arbitrary",), VMEM_CAP),
    )(*ins)


def _sum_partials(gathered):
    def body(g_ref, o_ref):
        acc = g_ref[0]
        for k in range(1, 8):
            acc = acc + g_ref[k]
        o_ref[...] = acc

    return pl.pallas_call(body, name="sum_partials", out_shape=SDS(gathered.shape[1:], F32), in_specs=[VMEM_SPEC], out_specs=VMEM_SPEC)(gathered)


def _adamw(w, g, m, v, name, tr=256):
    r, cdim = w.shape
    tr = tr if cdim <= D else tr // 2
    tr = tr if (r % tr == 0 and r > tr) else r

    def body(w_ref, g_ref, m_ref, v_ref, d_ref, nm_ref, nv_ref):
        gv = g_ref[...]
        nm = B1 * m_ref[...] + (1.0 - B1) * gv
        nv = B2 * v_ref[...] + (1.0 - B2) * (gv * gv)
        m_hat = nm / (1.0 - B1 ** STEP)
        v_hat = nv / (1.0 - B2 ** STEP)
        d_ref[...] = -LR * (m_hat / (jnp.sqrt(v_hat) + EPS) + WD * w_ref[...])
        nm_ref[...] = nm
        nv_ref[...] = nv

    spec = pl.BlockSpec((tr, cdim), lambda i: (i, 0))
    return pl.pallas_call(
        body, name=name, grid=(r // tr,), out_shape=[SDS((r, cdim), F32)] * 3, in_specs=[spec] * 4, out_specs=[spec] * 3,
        compiler_params=_cp(("parallel",), VMEM_CAP // 2),
    )(w, g, m, v)


def _t5_bucket(dist):
    n = jnp.maximum(dist, 1).astype(F32)
    large = MAX_EXACT + (jnp.log(n / MAX_EXACT) / math.log(MAX_DISTANCE / MAX_EXACT) * (N_BUCKETS - MAX_EXACT)).astype(jnp.int32)
    large = jnp.minimum(large, N_BUCKETS - 1)
    return jnp.where(dist < MAX_EXACT, dist, large)


def _band_buckets():
    a = jnp.arange(BLK)[:, None]
    b = jnp.arange(2 * BLK)[None, :]
    steps = jnp.maximum(a + BLK - b, 0)
    return jnp.stack([_t5_bucket(steps * d) for d in DILATIONS]).astype(jnp.int32)


def _pad_rows(a, rows=8):
    return jnp.pad(a, ((0, rows - a.shape[0]), (0, 0)))


def kernel(x, c, w_ada, b_ada, w_in, conv_w, conv_b, rel_bias, w_attn_out, w_conv_out, w_o, ln_g, ln_b, loss_target, m_w_ada, m_b_ada, m_w_in, m_conv_w, m_conv_b, m_rel_bias, m_w_attn_out, m_w_conv_out, m_w_o, m_ln_g, m_ln_b, v_w_ada, v_b_ada, v_w_in, v_conv_w, v_conv_b, v_rel_bias, v_w_attn_out, v_w_conv_out, v_w_o, v_ln_g, v_ln_b):
    bsz, seq, _ = x.shape
    t = bsz * seq
    mx, my, mc = _place()
    chip = 2 * mx + my
    dev = 4 * mx + 2 * my + mc
    x2 = x.reshape(t, D)
    tgt = loss_target.reshape(t, D)

    mine = [_to_bf16_window(a, w[0], f"to_bf16_{a}") for a, w in enumerate((w_in, w_attn_out, w_conv_out, w_o))]

    n_ada = w_ada.shape[2]
    n_cw = conv_w.shape[2]
    c_and_cw = jnp.concatenate([_pad_rows(c), jnp.pad(conv_w[0], ((0, 5), (0, D - n_cw)))], axis=0)
    firsts = _all_gather8(c_and_cw, "gather_c_conv_w")
    c_all = firsts[:, 0:bsz, :].reshape(8 * bsz, D)
    conv_w_f = firsts[0::2, 8:11, 0:n_cw].transpose(1, 0, 2).reshape(3, D)
    b_cols = lax.dynamic_slice(b_ada, (0, chip * n_ada), (1, n_ada))
    mod_part = _ada_forward(c_all, w_ada[0], b_cols)
    mod_parts = _all_gather8(mod_part, "gather_mod")
    mod_all = mod_parts[0::2].transpose(1, 0, 2).reshape(8 * bsz, 3 * D)
    mod = lax.dynamic_slice(mod_all, (dev * bsz, 0), (bsz, 3 * D))
    shift = mod[:, 0:D].reshape(bsz, 1, D)
    sc1p = 1.0 + mod[:, D:2 * D].reshape(bsz, 1, D)
    gate = mod[:, 2 * D:].reshape(bsz, 1, D)

    h, ht = _modulate(x2, sc1p, shift, seq)
    tab = lax.dynamic_index_in_dim(jnp.asarray(_tile_tables()), chip, 0, keepdims=False)
    qkv, gates, (w_in_f, w_ao_f, w_co_f, w_o_f) = _project_gather(h, mine, tab)
    buckets = _band_buckets()
    bias = _bias_tables(rel_bias, buckets)
    og, lg = [], []
    for g in range(3):
        o_g, l_g = _attn_forward(g, qkv, bias[g], bsz, seq)
        og.append(o_g)
        lg.append(l_g)
    (a_in, s_in, merged, dy, a_out, s_out, y_conv, o, lj, dxr, vec_f, dgate) = _mix_forward(
        gates, og, lg, x2, tgt, gate, w_ao_f, w_co_f, w_o_f, conv_w_f, conv_b, ln_g, ln_b, bsz, seq)

    dgates, do, dl, da_out, ds_out, vec_b = _mix_backward(gates, dy, a_out, s_out, y_conv, o, lj, w_ao_f, w_co_f, w_o_f, conv_w_f, bsz, seq)
    core = jnp.reshape(mc, (1,)).astype(jnp.int32)
    small_grads = _out_weight_grads(a_in, da_out, s_in, ds_out, merged, dy, core)
    dqkv, dbs = None, []
    for g in range(3):
        dqkv, db = _attn_backward(g, qkv, do, dl, bias[g], dqkv, bsz, seq)
        dbs.append(db)
    dq, dk, dv = dqkv
    drb = _bias_grad(jnp.stack(dbs), buckets)
    drb = drb[:, :, 0:4].transpose(1, 0, 2).reshape(N_BUCKETS, 12)
    g_in = None
    for n, src in enumerate((dq, dk, dv, dgates)):
        g_in = _in_weight_grad(ht, src, n * NQT, g_in, core, f"in_weight_grad_{n}")

    halves = [tuple(g_in)] + small_grads
    got = _swap_halves([theirs for _, theirs in halves])
    sums = [_chip_sum(a, halves[a][0], got[a], f"chip_sum_{a}") for a in range(4)]
    grad_x, dshift, dscale, landed = _input_grad(dq, dk, dv, dgates, w_in_f, x2, dxr, sc1p, seq, [s[1] for s in sums])
    halves = [_reduce_mine(a, sums[a][0], landed[a], f"reduce_mine_{a}") for a in range(4)]
    gw_in, gw_ao, gw_co, gw_o = _join_halves(halves)

    dmod = jnp.concatenate([dshift, dscale, dgate], axis=2).reshape(bsz * 3, D)
    drb_row = jnp.pad(drb.reshape(1, N_BUCKETS * 12), ((0, 0), (0, D - N_BUCKETS * 12)))
    packed = jnp.concatenate([vec_f, vec_b, _pad_rows(dmod), _pad_rows(drb_row)], axis=0)
    gathered = _all_gather8(packed, "gather_small")
    small = _sum_partials(gathered)
    g_ln_g, g_ln_b, loss_lanes = small[0:1], small[1:2], small[2:3]
    g_conv_w_full, g_conv_b = small[8:11], small[11:12]
    g_rel_bias = small[24, 0:N_BUCKETS * 12].reshape(N_BUCKETS, 12)
    loss = 0.5 / D * jnp.sum(loss_lanes)
    dmod_all = gathered[:, 16:16 + 3 * bsz, :].reshape(8 * bsz, 3 * D)
    dmod_cols = lax.dynamic_slice(dmod_all, (0, chip * n_ada), (8 * bsz, n_ada))
    gw_ada, gb_ada = _ada_backward(c_all, dmod_cols, dmod_all)
    g_conv_w = lax.dynamic_slice(g_conv_w_full, (0, chip * n_cw), (3, n_cw))

    names = ["w_ada", "b_ada", "w_in", "conv_w", "conv_b", "rel_bias", "w_attn_out", "w_conv_out", "w_o", "ln_g", "ln_b"]
    two_d = lambda a: a.reshape(a.shape[-2:]) if a.ndim == 3 else a
    weights = dict(zip(names, map(two_d, (w_ada, b_ada, w_in, conv_w, conv_b, rel_bias, w_attn_out, w_conv_out, w_o, ln_g, ln_b))))
    ms = dict(zip(names, map(two_d, (m_w_ada, m_b_ada, m_w_in, m_conv_w, m_conv_b, m_rel_bias, m_w_attn_out, m_w_conv_out, m_w_o, m_ln_g, m_ln_b))))
    vs = dict(zip(names, map(two_d, (v_w_ada, v_b_ada, v_w_in, v_conv_w, v_conv_b, v_rel_bias, v_w_attn_out, v_w_conv_out, v_w_o, v_ln_g, v_ln_b))))
    grads = dict(zip(names, (gw_ada, gb_ada, gw_in, g_conv_w, g_conv_b, g_rel_bias, gw_ao, gw_co, gw_o, g_ln_g, g_ln_b)))
    shapes = dict(zip(names, (w_ada, b_ada, w_in, conv_w, conv_b, rel_bias, w_attn_out, w_conv_out, w_o, ln_g, ln_b)))
    deltas, new_m, new_v = {}, {}, {}
    for n in names:
        deltas[n], new_m[n], new_v[n] = _adamw(weights[n], grads[n], ms[n], vs[n], f"adamw_{n}")
    shaped = lambda d: [d[n].reshape(shapes[n].shape) for n in names]
    return (loss, grad_x.reshape(bsz, seq, D), *shaped(grads), *shaped(deltas), *shaped(new_m), *shaped(new_v))
```

```python
import math

import numpy as np
import jax
import jax.numpy as jnp
from jax import lax
from jax.experimental import pallas as pl
from jax.experimental.pallas import tpu as pltpu

F32 = jnp.float32
BF16 = jnp.bfloat16
SDS = jax.ShapeDtypeStruct
MESH = pl.DeviceIdType.MESH
HBM_OUT = pltpu.HBM
ANY = pl.BlockSpec(memory_space=pl.ANY)
VMEM_SPEC = pl.BlockSpec(memory_space=pltpu.VMEM)

D = 1024
HD = 128
BLK = 128
QW = 1536
AW = 512
NGATE = 6656
GATE_COLS = ((0, 512), (512, 1536), (1536, 2560), (2560, 3584), (3584, 4608), (4608, 5632), (5632, 6656))
NCOL = 3 * QW + NGATE
TN = 512
NQT = QW // TN
NPT = NCOL // TN
DILATIONS = (1, 4, 16)
N_BUCKETS, MAX_EXACT, MAX_DISTANCE = 32, 16, 2048
ALPHA = 2.0 ** 0.25
LN_EPS = 1e-5
NEG = -1e30
SCALE = HD ** -0.5
LR, B1, B2, EPS, WD, STEP = 0.001, 0.9, 0.999, 1e-08, 0.01, 10
NCHIP = 4
VMEM_CAP = 60 * 2 ** 20


def _cp(sem=None, vmem=None, side=False):
    return pltpu.CompilerParams(dimension_semantics=sem, vmem_limit_bytes=vmem, has_side_effects=side)


def _dot(a, b):
    return jnp.dot(a, b, preferred_element_type=F32)


def _dot_nt(a, b):
    return lax.dot_general(a, b, (((1,), (1,)), ((), ())), preferred_element_type=F32)


def _dot_tn(a, b):
    return lax.dot_general(a, b, (((0,), (0,)), ((), ())), preferred_element_type=F32)


def _sig(x):
    return 1.0 / (1.0 + jnp.exp(-x))


def _in_hbm(a):
    return pltpu.with_memory_space_constraint(a, pltpu.HBM)


def _place():
    x, y, c = lax.axis_index("x"), lax.axis_index("y"), lax.axis_index("c")
    return x, y, c


def _all_gather8(v, name):
    r, cdim = v.shape

    def body(v_ref, out_ref, send_sems, recv_sems, local_sem):
        x, y, c = _place()
        me = 4 * x + 2 * y + c
        peers = [(x, y, 1 - c), (1 - x, y, c), (x, 1 - y, c), (1 - x, 1 - y, c),
                 (1 - x, y, 1 - c), (x, 1 - y, 1 - c), (1 - x, 1 - y, 1 - c)]
        mine = pltpu.make_async_copy(v_ref, out_ref.at[me], local_sem)
        mine.start()

        def copy(k, block, to):
            return pltpu.make_async_remote_copy(src_ref=v_ref, dst_ref=out_ref.at[block], send_sem=send_sems.at[k],
                                                recv_sem=recv_sems.at[k], device_id=to, device_id_type=MESH)

        sends = [copy(k, me, p) for k, p in enumerate(peers)]
        for cp in sends:
            cp.start()
        for k, (px, py, pc) in enumerate(peers):
            copy(k, 4 * px + 2 * py + pc, (px, py, pc)).wait_recv()
        for cp in sends:
            cp.wait_send()
        mine.wait()

    return pl.pallas_call(
        body, name=name, out_shape=SDS((8, r, cdim), v.dtype), in_specs=[VMEM_SPEC], out_specs=VMEM_SPEC,
        scratch_shapes=[pltpu.SemaphoreType.DMA((7,)), pltpu.SemaphoreType.DMA((7,)), pltpu.SemaphoreType.DMA(())],
        compiler_params=_cp(side=True),
    )(v)


W_CUTS = (("col", D, NCOL // NCHIP), ("col", AW, D // NCHIP), ("row", D // NCHIP, D), ("row", D // NCHIP, D))
W_FULL = ((D, NCOL), (AW, D), (D, D), (D, D))


def _shard_window(ref, cut, k, half):
    kind, nr, nc = cut
    hr = nr // 2
    if kind == "col":
        rows = pl.ds(0, nr) if half is None else pl.ds(pl.multiple_of(half * hr, 16), hr)
        return ref.at[rows, pl.ds(pl.multiple_of(k * nc, 128), nc)]
    if half is None:
        return ref.at[pl.ds(pl.multiple_of(k * nr, 16), nr), :]
    return ref.at[pl.ds(pl.multiple_of(k * nr + half * hr, 16), hr), :]


def _half_rows(ref, cut, half):
    hr = cut[1] // 2
    return ref.at[pl.ds(pl.multiple_of(half * hr, 16), hr), :]


def _to_bf16_window(a, w, name):
    kind, nr, nc = W_CUTS[a]
    x, y, _ = _place()
    chip = jnp.reshape(2 * x + y, (1,)).astype(jnp.int32)
    tr = min(nr, 256)

    def body(c_ref, w_ref, o_ref):
        o_ref[...] = w_ref[...].astype(BF16)

    out_map = (lambda i, cr: (i, cr[0])) if kind == "col" else (lambda i, cr: (cr[0] * (nr // tr) + i, 0))
    return pl.pallas_call(
        body, name=name, out_shape=SDS(W_FULL[a], BF16),
        grid_spec=pltpu.PrefetchScalarGridSpec(num_scalar_prefetch=1, grid=(nr // tr,),
                                               in_specs=[pl.BlockSpec((tr, nc), lambda i, cr: (i, 0))], out_specs=pl.BlockSpec((tr, nc), out_map)),
        compiler_params=_cp(("arbitrary",)),
    )(chip, w)


def _swap_halves(theirs):
    n = len(theirs)

    def body(*refs):
        src, land = refs[:n], refs[n:2 * n]
        send_sems, recv_sems = refs[2 * n:]
        x, y, c = _place()
        copies = [pltpu.make_async_remote_copy(src_ref=src[a], dst_ref=land[a], send_sem=send_sems.at[a], recv_sem=recv_sems.at[a],
                                               device_id=(x, y, 1 - c), device_id_type=MESH) for a in range(n)]
        for cp in copies:
            cp.start()
        for cp in copies:
            cp.wait()

    return pl.pallas_call(
        body, name="swap_grad_halves", out_shape=[SDS(v.shape, v.dtype) for v in theirs], in_specs=[ANY] * n, out_specs=[ANY] * n,
        scratch_shapes=[pltpu.SemaphoreType.DMA((n,)), pltpu.SemaphoreType.DMA((n,))],
        compiler_params=_cp(side=True),
    )(*theirs)


def _chip_sum(a, mine, got, name):
    kind, nr, nc = W_CUTS[a]
    hr = nr // 2
    x, y, _ = _place()
    me = jnp.reshape(2 * x + y, (1,)).astype(jnp.int32)

    def body(me_ref, g_ref, r_ref, f_ref, b_ref):
        s = (g_ref[...] + r_ref[...].astype(F32)).reshape(hr, nc)
        b_ref[0] = s.astype(BF16)

        @pl.when(pl.program_id(0) == me_ref[0])
        def _():
            f_ref[...] = s

    if kind == "col":
        in_spec = pl.BlockSpec((hr, nc), lambda k, mr: (0, k))
    else:
        in_spec = pl.BlockSpec((1, hr, nc), lambda k, mr: (k, 0, 0))
    return pl.pallas_call(
        body, name=name, out_shape=[SDS((hr, nc), F32), SDS((NCHIP, hr, nc), BF16)],
        grid_spec=pltpu.PrefetchScalarGridSpec(
            num_scalar_prefetch=1, grid=(NCHIP,), in_specs=[in_spec, in_spec],
            out_specs=[pl.BlockSpec((hr, nc), lambda k, mr: (0, 0)), pl.BlockSpec((1, hr, nc), lambda k, mr: (k, 0, 0))]),
        compiler_params=_cp(("arbitrary",), VMEM_CAP),
    )(me, mine, got)


def _reduce_mine(a, mine_f32, got, name):
    kind, nr, nc = W_CUTS[a]
    hr = nr // 2
    x, y, c = _place()
    where = jnp.stack([2 * x + y, c]).astype(jnp.int32)
    tr = min(hr, 256)

    def body(w_ref, m_ref, g_ref, o_ref):
        o_ref[...] = ((m_ref[...] + g_ref[0].astype(F32)) + g_ref[1].astype(F32)) + g_ref[2].astype(F32)

    return pl.pallas_call(
        body, name=name, out_shape=SDS((nr, nc), F32),
        grid_spec=pltpu.PrefetchScalarGridSpec(
            num_scalar_prefetch=1, grid=(hr // tr,),
            in_specs=[pl.BlockSpec((tr, nc), lambda i, wr: (i, 0)), pl.BlockSpec((3, tr, nc), lambda i, wr: (0, i, 0))],
            out_specs=pl.BlockSpec((tr, nc), lambda i, wr: (wr[1] * (hr // tr) + i, 0))),
        compiler_params=_cp(("arbitrary",), VMEM_CAP),
    )(where, mine_f32, got)


def _join_halves(fulls):
    n = len(fulls)

    def body(*refs):
        full = refs[n:2 * n]
        send_sems, recv_sems = refs[2 * n:]
        x, y, c = _place()
        sibling = (x, y, 1 - c)

        def swap(a, half):
            rows = _half_rows(full[a], W_CUTS[a], half)
            return pltpu.make_async_remote_copy(src_ref=rows, dst_ref=rows, send_sem=send_sems.at[a], recv_sem=recv_sems.at[a],
                                                device_id=sibling, device_id_type=MESH)

        sends = [swap(a, c) for a in range(n)]
        for cp in sends:
            cp.start()
        for a, cp in enumerate(sends):
            cp.wait_send()
            swap(a, 1 - c).wait_recv()

    return pl.pallas_call(
        body, name="join_grad_halves", out_shape=[SDS((W_CUTS[a][1], W_CUTS[a][2]), F32) for a in range(n)],
        in_specs=[ANY] * n, out_specs=[ANY] * n,
        scratch_shapes=[pltpu.SemaphoreType.DMA((n,)), pltpu.SemaphoreType.DMA((n,))],
        input_output_aliases={a: a for a in range(n)}, compiler_params=_cp(side=True),
    )(*fulls)


def _ada_forward(c_all, w_ada, b_cols):
    nb, nc = c_all.shape[0], w_ada.shape[1]

    def body(c_ref, w_ref, b_ref, o_ref):
        cv = c_ref[...]
        sc = (cv * _sig(cv)).astype(BF16)
        o_ref[...] = _dot(sc, w_ref[...].astype(BF16)) + b_ref[...]

    return pl.pallas_call(body, name="ada_forward", out_shape=SDS((nb, nc), F32), compiler_params=_cp(vmem=VMEM_CAP // 2))(c_all, w_ada, b_cols)


def _ada_backward(c_all, dmod_cols, dmod_all):
    nb, nc = dmod_cols.shape

    def body(c_ref, d_ref, a_ref, gw_ref, gb_ref):
        cv = c_ref[...]
        sc = (cv * _sig(cv)).astype(BF16)
        gw_ref[...] = _dot_tn(sc, d_ref[...].astype(BF16))
        gb_ref[...] = jnp.sum(a_ref[...], axis=0, keepdims=True)

    return pl.pallas_call(body, name="ada_backward", out_shape=[SDS((D, nc), F32), SDS((1, dmod_all.shape[1]), F32)],
                          compiler_params=_cp(vmem=VMEM_CAP // 2))(c_all, dmod_cols, dmod_all)


def _modulate(x2, sc1p, shift, seq, tm=256):
    t = x2.shape[0]
    spt = seq // tm

    def body(x_ref, sc_ref, sh_ref, h_ref, ht_ref):
        h = x_ref[...] * sc_ref[0] + sh_ref[0]
        h_ref[...] = h.astype(BF16)
        ht_ref[...] = h.T.astype(BF16)

    per_seq = pl.BlockSpec((1, 1, D), lambda i: (i // spt, 0, 0))
    return pl.pallas_call(
        body, name="modulate", out_shape=[HBM_OUT((t, D), BF16), HBM_OUT((D, t), BF16)], grid=(t // tm,),
        in_specs=[pl.BlockSpec((tm, D), lambda i: (i, 0)), per_seq, per_seq],
        out_specs=[pl.BlockSpec((tm, D), lambda i: (i, 0)), pl.BlockSpec((D, tm), lambda i: (0, i))],
        compiler_params=_cp(("parallel",)),
    )(x2, sc1p, shift)


TW = 256
TPS = NCOL // NCHIP // TW
NT = NCOL // TW
NQKV_T = 3 * QW // TW
N_TILE_SEMS = 2 * 3 * TPS


def _tile_tables():
    tabs = np.zeros((NCHIP, 3, NT), np.int32)
    for me in range(NCHIP):
        tiles = [TPS * (me ^ (s // TPS)) + s % TPS for s in range(NT)]
        tabs[me, 0] = tiles
        for row, (lo, hi) in enumerate(((0, NQKV_T), (NQKV_T, NT))):
            mine = [w - lo if lo <= w < hi else None for w in tiles]
            held = next(m for m in mine if m is not None)
            for s, m in enumerate(mine):
                held = held if m is None else m
                tabs[me, 1 + row, s] = held
    return tabs


def _project_gather(h, fulls, tab):
    t = h.shape[0]
    n = len(fulls)

    def body(tab_ref, h_ref, *rest):
        qkv_ref, g_ref = rest[n], rest[n + 1]
        full = rest[n + 2:2 * n + 2]
        w_buf, tile_sems, send_sems, recv_sems = rest[2 * n + 2:]
        s = pl.program_id(0)
        x, y, c = _place()
        me = 2 * x + y
        peers = [(x, 1 - y), (1 - x, y), (1 - x, 1 - y)]
        sibling = (x, y, 1 - c)

        def hop(a, r, stage, chip, half, to):
            window = _shard_window(full[a], W_CUTS[a], chip, half)
            k = N_TILE_SEMS + 6 * (a - 1) + 2 * r + stage
            return pltpu.make_async_remote_copy(src_ref=window, dst_ref=window, send_sem=send_sems.at[k], recv_sem=recv_sems.at[k],
                                                device_id=to, device_id_type=MESH)

        def tile_hop(q, stage, col_step, half, to):
            col = pl.multiple_of(tab_ref[0, col_step] * TW, TW)
            window = full[0].at[pl.ds(pl.multiple_of(half * (D // 2), 16), D // 2), pl.ds(col, TW)]
            k = 2 * (q - TPS) + stage
            return pltpu.make_async_remote_copy(src_ref=window, dst_ref=window, send_sem=send_sems.at[k], recv_sem=recv_sems.at[k],
                                                device_id=to, device_id_type=MESH)

        def send_tile(r, j):
            return tile_hop(TPS * (r + 1) + j, 0, j, c, (*peers[r], c))

        def arrive(a, r):
            px, py = peers[r]
            chip = 2 * px + py
            hop(a, r, 0, chip, c, (px, py, c)).wait_recv()
            hop(a, r, 1, chip, c, sibling).start()
            hop(a, r, 1, chip, 1 - c, sibling).wait_recv()

        def tile(step, slot):
            col = pl.multiple_of(tab_ref[0, step] * TW, TW)
            return pltpu.make_async_copy(full[0].at[:, pl.ds(col, TW)], w_buf.at[slot], tile_sems.at[slot])

        @pl.when(s == 0)
        def _():
            for r in range(2):
                for j in range(TPS):
                    send_tile(r, j).start()
            tile(0, 0).start()

        @pl.when((s + 1 >= TPS) & (s + 1 < NT))
        def _():
            tile_hop(s + 1, 1, s + 1, 1 - c, sibling).wait_recv()

        @pl.when(s + 1 < NT)
        def _():
            tile(s + 1, 1 - (s % 2)).start()

        @pl.when((s + 2 >= TPS) & (s + 2 < NT))
        def _():
            tile_hop(s + 2, 0, s + 2, c, sibling).wait_recv()
            tile_hop(s + 2, 1, s + 2, c, sibling).start()

        @pl.when(s + 2 == 2 * TPS - 1)
        def _():
            for j in range(TPS):
                send_tile(2, j).start()
            for a in range(1, n):
                for r in range(3):
                    hop(a, r, 0, me, c, (*peers[r], c)).start()

        slot = s % 2
        tile(s, slot).wait()
        is_qkv = tab_ref[0, s] < NQKV_T
        for k in range(2):
            @pl.when(slot == k)
            def _(k=k):
                acc = _dot(h_ref[...], w_buf[k])

                @pl.when(is_qkv)
                def _():
                    qkv_ref[...] = acc.astype(BF16)

                @pl.when(jnp.logical_not(is_qkv))
                def _():
                    g_ref[...] = acc.astype(BF16)

        @pl.when(s == NT - 1)
        def _():
            for a in range(1, n):
                for r in range(3):
                    arrive(a, r)
            for r in range(3):
                for j in range(TPS):
                    send_tile(r, j).wait_send()
                    tile_hop(TPS * (r + 1) + j, 1, TPS * (r + 1) + j, c, sibling).wait_send()
                for a in range(1, n):
                    hop(a, r, 0, me, c, (*peers[r], c)).wait_send()
                    px, py = peers[r]
                    hop(a, r, 1, 2 * px + py, c, sibling).wait_send()

    n_sems = N_TILE_SEMS + 6 * (n - 1)
    outs = pl.pallas_call(
        body, name="project_gather", out_shape=[HBM_OUT((t, 3 * QW), BF16), HBM_OUT((t, NGATE), BF16)] + [SDS(s, BF16) for s in W_FULL],
        grid_spec=pltpu.PrefetchScalarGridSpec(
            num_scalar_prefetch=1, grid=(NT,),
            in_specs=[pl.BlockSpec((t, D), lambda s, tab: (0, 0))] + [ANY] * n,
            out_specs=[pl.BlockSpec((t, TW), lambda s, tab: (0, tab[1, s])), pl.BlockSpec((t, TW), lambda s, tab: (0, tab[2, s]))] + [ANY] * n,
            scratch_shapes=[pltpu.VMEM((2, D, TW), BF16), pltpu.SemaphoreType.DMA((2,)),
                            pltpu.SemaphoreType.DMA((n_sems,)), pltpu.SemaphoreType.DMA((n_sems,))]),
        input_output_aliases={2 + a: 2 + a for a in range(n)},
        compiler_params=_cp(("arbitrary",), VMEM_CAP, side=True),
    )(tab, _in_hbm(h), *fulls)
    return outs[0], outs[1], outs[2:]


def _bias_tables(rel_bias, buckets):
    def body(tab_ref, bk_ref, o_ref):
        a = lax.broadcasted_iota(jnp.int32, (BLK, 2 * BLK), 0)
        b = lax.broadcasted_iota(jnp.int32, (BLK, 2 * BLK), 1)
        steps = a + BLK - b
        valid = (steps >= 0) & (steps <= BLK)
        for g in range(3):
            bk = bk_ref[g]
            for j in range(4):
                def pick(kk, acc, bk=bk, col=4 * g + j):
                    return jnp.where(bk == kk, tab_ref[kk, col], acc)

                acc = lax.fori_loop(0, N_BUCKETS, pick, jnp.zeros((BLK, 2 * BLK), F32))
                o_ref[g, j] = jnp.where(valid, acc, NEG)

    return pl.pallas_call(
        body, name="bias_tables", out_shape=SDS((3, 4, BLK, 2 * BLK), F32),
        in_specs=[pl.BlockSpec(memory_space=pltpu.SMEM), VMEM_SPEC], out_specs=VMEM_SPEC,
    )(rel_bias, buckets)


def _bias_grad(ds_sum, buckets):
    def body(ds_ref, bk_ref, o_ref):
        lane = lax.broadcasted_iota(jnp.int32, (1, 128), 1)
        for g in range(3):
            def bucket(kk, carry, g=g):
                row = jnp.zeros((1, 128), F32)
                for j in range(4):
                    v = jnp.where(bk_ref[g] == kk, ds_ref[g, j], 0.0)
                    s = jnp.sum(jnp.sum(v, axis=1, keepdims=True), axis=0, keepdims=True)
                    row = jnp.where(lane == j, s, row)
                o_ref[g, pl.ds(kk, 1), :] = row
                return carry

            lax.fori_loop(0, N_BUCKETS, bucket, 0)

    return pl.pallas_call(body, name="bias_grad", out_shape=SDS((3, N_BUCKETS, 128), F32), in_specs=[VMEM_SPEC, VMEM_SPEC],
                          out_specs=VMEM_SPEC)(ds_sum, buckets)


def _sub_rows(d, r, first, size):
    return pl.ds(first * d + r, size) if d == 1 else pl.ds(first * d + r, size, stride=d)


def _head_spec(seq, g, part):
    return pl.BlockSpec((seq, HD), lambda b, hh: (b, part * (QW // HD) + 4 * g + hh))


def _rows(start, count, stride):
    return pl.ds(start, count) if stride == 1 else pl.ds(start, count, stride=stride)


def _gather_rows(dst, dst0, src, src0, stride, count):
    for first in range(0, count, BLK):
        dst[pl.ds(dst0 + first, BLK), :] = src[_rows(src0 + first * stride, BLK, stride), :].astype(dst.dtype)


def _scatter_rows(dst, dst0, stride, src, src0, count):
    for first in range(0, count, BLK):
        dst[_rows(dst0 + first * stride, BLK, stride), :] = src[pl.ds(src0 + first, BLK), :].astype(dst.dtype)


def _by_subsequence(dst, src, d, wide=None, tmp=None):
    seq = src.shape[0]
    ln = seq // d
    if wide is not None:
        wide[...] = src[...].astype(F32)
        src = wide
    if d <= 4:
        for r in range(d):
            _gather_rows(dst, r * ln, src, r, d, ln)
    else:
        quarter = seq // 4
        for r4 in range(4):
            _gather_rows(tmp, r4 * quarter, src, r4, 4, quarter)
        for r4 in range(4):
            for a in range(d // 4):
                _gather_rows(dst, (4 * a + r4) * ln, tmp, r4 * quarter + a, d // 4, ln)


def _to_sequence(dst, src, d, tmp=None):
    seq = dst.shape[0]
    ln = seq // d
    if d <= 4:
        for r in range(d):
            _scatter_rows(dst, r, d, src, r * ln, ln)
    else:
        quarter = seq // 4
        for r4 in range(4):
            for a in range(d // 4):
                _scatter_rows(tmp, r4 * quarter + a, d // 4, src, (4 * a + r4) * ln, ln)
        for r4 in range(4):
            _scatter_rows(dst, r4, 4, tmp, r4 * quarter, quarter)


def _attn_forward(g, qkv, bias, bsz, seq):
    d = DILATIONS[g]
    ln = seq // d
    units = [(r, n) for r in range(d) for n in range(ln // BLK)]

    def band(n):
        return slice(BLK, 2 * BLK) if n == 0 else slice(0, 2 * BLK)

    def body(q_ref, k_ref, v_ref, b_ref, o_ref, l_ref, *scratch):
        hs = pl.program_id(1)
        s_scr, p_scr = scratch[:2]
        if d == 1:
            qd, kd, vd = q_ref, k_ref, v_ref
        else:
            wide, tmp, qd, kd, vd = scratch[2:7]
            for dst, src in ((qd, q_ref), (kd, k_ref), (vd, v_ref)):
                _by_subsequence(dst, src, d, wide, tmp)
        blk = lambda r, n: pl.ds(r * ln + n * BLK, BLK)
        direct = d <= 4
        out_rows = (lambda r, n: _sub_rows(d, r, n * BLK, BLK)) if direct else blk
        o_dst, l_dst = (o_ref, l_ref) if direct else scratch[7:9]
        for u, (r, n) in enumerate(units):
            s_scr[u, :, BLK:] = _dot_nt(qd[blk(r, n), :], kd[blk(r, n), :])
            if n > 0:
                s_scr[u, :, :BLK] = _dot_nt(qd[blk(r, n), :], kd[blk(r, n - 1), :])
        for u, (r, n) in enumerate(units):
            s = s_scr[u, :, band(n)] * SCALE + b_ref[hs, :, band(n)]
            m = jnp.max(s, axis=1, keepdims=True)
            e = jnp.exp(s - m)
            den = jnp.sum(e, axis=1, keepdims=True)
            p_scr[u, :, band(n)] = (e * (1.0 / den)).astype(BF16)
            l_dst[out_rows(r, n), :] = jnp.broadcast_to(m + jnp.log(den), (BLK, HD))
        for u, (r, n) in enumerate(units):
            acc = _dot(p_scr[u, :, BLK:], vd[blk(r, n), :])
            if n > 0:
                acc = acc + _dot(p_scr[u, :, :BLK], vd[blk(r, n - 1), :])
            o_dst[out_rows(r, n), :] = acc
        if not direct:
            _to_sequence(o_ref, o_dst, d, tmp)
            _to_sequence(l_ref, l_dst, d, tmp)

    rows_f32, rows_bf16 = pltpu.VMEM((seq, HD), F32), pltpu.VMEM((seq, HD), BF16)
    regrouped = [] if d == 1 else [rows_f32] * 2 + [rows_bf16] * 3 + ([] if d <= 4 else [rows_f32] * 2)
    out_spec = pl.BlockSpec((seq, HD), lambda b, hh: (b, hh))
    return pl.pallas_call(
        body, name=f"attn_forward_{g}", out_shape=[HBM_OUT((bsz * seq, AW), F32)] * 2, grid=(bsz, 4),
        in_specs=[_head_spec(seq, g, part) for part in range(3)] + [pl.BlockSpec((4, BLK, 2 * BLK), lambda b, hh: (0, 0, 0))],
        out_specs=[out_spec, out_spec],
        scratch_shapes=[pltpu.VMEM((len(units), BLK, 2 * BLK), F32), pltpu.VMEM((len(units), BLK, 2 * BLK), BF16)] + regrouped,
        compiler_params=_cp(("parallel", "parallel"), VMEM_CAP // 2),
    )(qkv, qkv, qkv, _in_hbm(bias))


def _attn_backward(g, qkv, do, dl, bias, prev_out, bsz, seq):
    d = DILATIONS[g]
    ln = seq // d
    units = [(r, n) for r in range(d) for n in range(ln // BLK)]

    def body(q_ref, k_ref, v_ref, do_ref, dl_ref, b_ref, *rest):
        dq_ref, dk_ref, dv_ref, db_ref = rest[-18:-14]
        wide, tmp, qd, kd, vd, dod, dld, dqd, dkd, dvd, s_scr, dp_scr, p_scr, ds_scr = rest[-14:]
        hs = pl.program_id(1)

        @pl.when((pl.program_id(0) == 0) & (hs == 0))
        def _():
            db_ref[...] = jnp.zeros_like(db_ref)

        for dst, src in ((qd, q_ref), (kd, k_ref), (vd, v_ref)):
            _by_subsequence(dst, src, d, wide, tmp)
        _by_subsequence(dod, do_ref, d, None, tmp)
        _by_subsequence(dld, dl_ref, d, None, tmp)
        dkd[...] = jnp.zeros_like(dkd)
        dvd[...] = jnp.zeros_like(dvd)
        blk = lambda r, n: pl.ds(r * ln + n * BLK, BLK)
        keys = lambda r, n: [(blk(r, n), slice(BLK, 2 * BLK))] + ([(blk(r, n - 1), slice(0, BLK))] if n > 0 else [])
        for u, (r, n) in enumerate(units):
            for rows, band in keys(r, n):
                s_scr[u, :, band] = _dot_nt(qd[blk(r, n), :], kd[rows, :])
                dp_scr[u, :, band] = _dot_nt(dod[blk(r, n), :], vd[rows, :])
        for u, (r, n) in enumerate(units):
            both = dld[blk(r, n), :]
            lse, delta = both[:, 0:1], both[:, 64:65]
            band = slice(BLK, 2 * BLK) if n == 0 else slice(0, 2 * BLK)
            p = jnp.exp(s_scr[u, :, band] * SCALE + b_ref[hs, :, band] - lse)
            ds = p * (dp_scr[u, :, band] - delta)
            p_scr[u, :, band] = p.astype(BF16)
            ds_scr[u, :, band] = ds.astype(BF16)
            db_ref[hs, :, band] += ds
        for u, (r, n) in enumerate(units):
            dq = jnp.zeros((BLK, HD), F32)
            for rows, band in keys(r, n):
                dvd[rows, :] += _dot_tn(p_scr[u, :, band], dod[blk(r, n), :])
                dkd[rows, :] += _dot_tn(ds_scr[u, :, band], qd[blk(r, n), :]) * SCALE
                dq = dq + _dot(ds_scr[u, :, band], kd[rows, :])
            dqd[blk(r, n), :] = dq * SCALE
        for out, acc in ((dq_ref, dqd), (dk_ref, dkd), (dv_ref, dvd)):
            if d == 1:
                out[...] = acc[...].astype(BF16)
            else:
                _to_sequence(wide, acc, d, tmp)
                out[...] = wide[...].astype(BF16)

    qkv_spec = _head_spec(seq, g, 0)
    out_spec = pl.BlockSpec((seq, HD), lambda b, hh: (b, hh))
    band_spec = pl.BlockSpec((4, BLK, 2 * BLK), lambda b, hh: (0, 0, 0))
    ins = [qkv, qkv, qkv, _in_hbm(do), _in_hbm(dl), _in_hbm(bias)]
    in_specs = [_head_spec(seq, g, part) for part in range(3)] + [out_spec, out_spec, band_spec]
    aliases = {}
    if prev_out is not None:
        ins += list(prev_out)
        in_specs += [ANY] * 3
        aliases = {6: 0, 7: 1, 8: 2}
    rows_bf16, rows_f32 = pltpu.VMEM((seq, HD), BF16), pltpu.VMEM((seq, HD), F32)
    staged = [pltpu.VMEM((len(units), BLK, 2 * BLK), F32)] * 2 + [pltpu.VMEM((len(units), BLK, 2 * BLK), BF16)] * 2
    dq, dk, dv, db = pl.pallas_call(
        body, name=f"attn_backward_{g}", out_shape=[HBM_OUT((bsz * seq, QW), BF16)] * 3 + [SDS((4, BLK, 2 * BLK), F32)], grid=(bsz, 4),
        in_specs=in_specs, out_specs=[qkv_spec] * 3 + [band_spec], input_output_aliases=aliases,
        scratch_shapes=[rows_f32] * 2 + [rows_bf16] * 4 + [rows_f32] * 4 + staged,
        compiler_params=_cp(("arbitrary", "arbitrary"), VMEM_CAP // 2),
    )(*ins)
    return (dq, dk, dv), db


def _mix_forward(gates, og, lg, x2, tgt, gate, w_ao, w_co, w_o, conv_w, conv_b, ln_g, ln_b, bsz, seq, tm=256):
    t = x2.shape[0]
    spt = seq // tm

    def body(g_ref, o1, o2, o3, l1, l2, l3, x_ref, t_ref, gate_ref, wao_ref, wco_ref, wo_ref, cw_ref, cb_ref, lng_ref, lnb_ref,
             ain_ref, sin_ref, mrg_ref, dy_ref, aout_ref, sout_ref, yc_ref, o_ref, lj_ref, dxr_ref, vec_ref, dgate_ref, zc_ref):
        b, i = pl.program_id(0), pl.program_id(1)

        @pl.when((b == 0) & (i == 0))
        def _():
            vec_ref[...] = jnp.zeros_like(vec_ref)

        @pl.when(i == 0)
        def _():
            zc_ref[...] = jnp.zeros_like(zc_ref)
            dgate_ref[...] = jnp.zeros_like(dgate_ref)

        g_attn, u, bg, cg, g_conv, m_attn, m_conv = (g_ref[:, lo:hi].astype(F32) for lo, hi in GATE_COLS)
        la, lb, lc = l1[...], l2[...], l3[...]
        mx = jnp.maximum(la, jnp.maximum(lb, lc))
        ea, eb, ec = jnp.exp(la - mx), jnp.exp(lb - mx), jnp.exp(lc - mx)
        den = ea + eb + ec
        o = (ea * o1[...] + eb * o2[...] + ec * o3[...]) / den
        o_ref[...] = o
        lj_ref[...] = mx + jnp.log(den)
        a_in = o * (g_attn * _sig(g_attn))
        ain_ref[...] = a_in.astype(BF16)
        a_out = _dot(a_in.astype(BF16), wao_ref[...])
        aout_ref[...] = a_out.astype(BF16)
        z = cg * u
        rows = lax.broadcasted_iota(jnp.int32, (tm, D), 0)
        c6, c7 = zc_ref[6:7, :], zc_ref[7:8, :]
        z1 = jnp.where(rows == 0, c7, pltpu.roll(z, 1, 0))
        z2 = jnp.where(rows == 0, c6, jnp.where(rows == 1, c7, pltpu.roll(z, 2, 0)))
        zc_ref[...] = z[tm - 8:tm, :]
        y_conv = (cw_ref[0:1, :] * z2 + cw_ref[1:2, :] * z1 + cw_ref[2:3, :] * z) + cb_ref[...]
        yc_ref[...] = y_conv.astype(BF16)
        s_in = bg * y_conv * (g_conv * _sig(g_conv))
        sin_ref[...] = s_in.astype(BF16)
        s_out = _dot(s_in.astype(BF16), wco_ref[...])
        sout_ref[...] = s_out.astype(BF16)
        merged = _sig(m_attn) * a_out + _sig(m_conv) * s_out
        mrg_ref[...] = merged.astype(BF16)
        y = _dot(merged.astype(BF16), wo_ref[...])
        gate1 = 1.0 + gate_ref[0]
        r = ALPHA * x_ref[...] + gate1 * y
        mu = jnp.mean(r, axis=1, keepdims=True)
        rc = r - mu
        rstd = lax.rsqrt(jnp.mean(rc * rc, axis=1, keepdims=True) + LN_EPS)
        xhat = rc * rstd
        diff = (xhat * lng_ref[...] + lnb_ref[...]) - t_ref[...]
        dout = diff * (1.0 / D)
        vec_ref[0:1, :] += jnp.sum(dout * xhat, axis=0, keepdims=True)
        vec_ref[1:2, :] += jnp.sum(dout, axis=0, keepdims=True)
        vec_ref[2:3, :] += jnp.sum(diff * diff, axis=0, keepdims=True)
        dxh = dout * lng_ref[...]
        dr = rstd * (dxh - jnp.mean(dxh, axis=1, keepdims=True) - xhat * jnp.mean(dxh * xhat, axis=1, keepdims=True))
        dxr_ref[...] = ALPHA * dr
        dy_ref[...] = (dr * gate1).astype(BF16)
        dgate_ref[0] += jnp.sum(dr * y, axis=0, keepdims=True)

    tok = lambda w: pl.BlockSpec((tm, w), lambda b, i: (b * spt + i, 0))
    const = lambda s: pl.BlockSpec(s, lambda b, i: (0,) * len(s))
    per_seq = pl.BlockSpec((1, 1, D), lambda b, i: (b, 0, 0))
    outs = pl.pallas_call(
        body, name="mix_forward", grid=(bsz, spt),
        out_shape=[HBM_OUT((t, AW), BF16), HBM_OUT((t, D), BF16), HBM_OUT((t, D), BF16), HBM_OUT((t, D), BF16), HBM_OUT((t, D), BF16),
                   HBM_OUT((t, D), BF16), HBM_OUT((t, D), BF16), HBM_OUT((t, AW), F32), HBM_OUT((t, AW), F32), HBM_OUT((t, D), F32),
                   SDS((8, D), F32), SDS((bsz, 1, D), F32)],
        in_specs=[tok(NGATE)] + [tok(AW)] * 6 + [tok(D), tok(D), per_seq, const((AW, D)), const((D, D)), const((D, D)),
                                                 const((3, D)), const((1, D)), const((1, D)), const((1, D))],
        out_specs=[tok(AW), tok(D), tok(D), tok(D), tok(D), tok(D), tok(D), tok(AW), tok(AW), tok(D), const((8, D)), per_seq],
        scratch_shapes=[pltpu.VMEM((8, D), F32)],
        compiler_params=_cp(("arbitrary", "arbitrary"), VMEM_CAP),
    )(gates, *map(_in_hbm, og), *map(_in_hbm, lg), x2, tgt, gate, w_ao, w_co, w_o, conv_w, conv_b, ln_g, ln_b)
    return outs


def _mix_backward(gates, dy, a_out, s_out, y_conv, o, lj, w_ao, w_co, w_o, conv_w, vec_f, bsz, seq, tm=256):
    t = dy.shape[0]
    spt = seq // tm

    def body(g_ref, dy_ref, aout_ref, sout_ref, yc_ref, o_ref, lj_ref, wao_ref, wco_ref, wo_ref, cw_ref, vecf_ref,
             dg_ref, do_ref, dl_ref, daout_ref, dsout_ref, vec_ref, car_ref):
        b, i = pl.program_id(0), pl.program_id(1)

        @pl.when((b == 0) & (i == 0))
        def _():
            vec_ref[...] = vecf_ref[...]

        @pl.when(i == 0)
        def _():
            car_ref[...] = jnp.zeros_like(car_ref)

        g_attn, u, bg, cg, g_conv, m_attn, m_conv = (g_ref[:, lo:hi].astype(F32) for lo, hi in GATE_COLS)
        dmerged = _dot_nt(dy_ref[...], wo_ref[...])
        sa, sc = _sig(m_attn), _sig(m_conv)
        da_out = (dmerged * sa).astype(BF16)
        ds_out = (dmerged * sc).astype(BF16)
        daout_ref[...] = da_out
        dsout_ref[...] = ds_out
        dg_ref[:, 4608:5632] = (dmerged * aout_ref[...].astype(F32) * (sa * (1.0 - sa))).astype(BF16)
        dg_ref[:, 5632:6656] = (dmerged * sout_ref[...].astype(F32) * (sc * (1.0 - sc))).astype(BF16)
        da_in = _dot_nt(da_out, wao_ref[...])
        ds_in = _dot_nt(ds_out, wco_ref[...])
        sga = _sig(g_attn)
        o = o_ref[...]
        do = da_in * (g_attn * sga)
        do_ref[...] = do
        dg_ref[:, 0:512] = (da_in * o * (sga * (1.0 + g_attn * (1.0 - sga)))).astype(BF16)
        prod = do * o
        lane = lax.broadcasted_iota(jnp.int32, (tm, HD), 1)
        for j in range(4):
            cs = slice(j * HD, (j + 1) * HD)
            delta = jnp.sum(prod[:, cs], axis=1, keepdims=True)
            dl_ref[:, cs] = jnp.where(lane < 64, lj_ref[:, cs], delta)
        sgc = _sig(g_conv)
        silu_c = g_conv * sgc
        yc = yc_ref[...].astype(F32)
        dg_ref[:, 1536:2560] = (ds_in * yc * silu_c).astype(BF16)
        dg_ref[:, 3584:4608] = (ds_in * bg * yc * (sgc * (1.0 + g_conv * (1.0 - sgc)))).astype(BF16)
        dyc = ds_in * bg * silu_c
        rows = lax.broadcasted_iota(jnp.int32, (tm, D), 0)
        c0, c1 = car_ref[0:1, :], car_ref[1:2, :]
        n1 = jnp.where(rows == tm - 1, c0, pltpu.roll(dyc, tm - 1, 0))
        n2 = jnp.where(rows == tm - 2, c0, jnp.where(rows == tm - 1, c1, pltpu.roll(dyc, tm - 2, 0)))
        car_ref[...] = dyc[0:8, :]
        dz = cw_ref[2:3, :] * dyc + cw_ref[1:2, :] * n1 + cw_ref[0:1, :] * n2
        z = cg * u
        dg_ref[:, 512:1536] = (dz * cg).astype(BF16)
        dg_ref[:, 2560:3584] = (dz * u).astype(BF16)
        vec_ref[3:4, :] += jnp.sum(n2 * z, axis=0, keepdims=True)
        vec_ref[4:5, :] += jnp.sum(n1 * z, axis=0, keepdims=True)
        vec_ref[5:6, :] += jnp.sum(dyc * z, axis=0, keepdims=True)
        vec_ref[6:7, :] += jnp.sum(dyc, axis=0, keepdims=True)

    tok = lambda w: pl.BlockSpec((tm, w), lambda b, i: (b * spt + (spt - 1 - i), 0))
    const = lambda s: pl.BlockSpec(s, lambda b, i: (0,) * len(s))
    return pl.pallas_call(
        body, name="mix_backward", grid=(bsz, spt),
        out_shape=[HBM_OUT((t, NGATE), BF16), HBM_OUT((t, AW), F32), HBM_OUT((t, AW), F32), HBM_OUT((t, D), BF16), HBM_OUT((t, D), BF16),
                   SDS((8, D), F32)],
        in_specs=[tok(NGATE), tok(D), tok(D), tok(D), tok(D), tok(AW), tok(AW), const((AW, D)), const((D, D)), const((D, D)), const((3, D)),
                  const((8, D))],
        out_specs=[tok(NGATE), tok(AW), tok(AW), tok(D), tok(D), const((8, D))],
        scratch_shapes=[pltpu.VMEM((8, D), F32)],
        compiler_params=_cp(("arbitrary", "arbitrary"), VMEM_CAP),
    )(gates, dy, a_out, s_out, y_conv, o, lj, w_ao, w_co, w_o, conv_w, vec_f)


def _halves_out(a):
    kind, nr, nc = W_CUTS[a]
    shape = (nr // 2, W_FULL[a][1]) if kind == "col" else (NCHIP, nr // 2, nc)
    return [SDS(shape, F32), SDS(shape, BF16)]


def _write_halves(a, acc_ref, c, mine_ref, theirs_ref):
    kind, nr, nc = W_CUTS[a]
    hr = nr // 2
    if kind == "col":
        mine_ref[...] = acc_ref[pl.ds(pl.multiple_of(c * hr, hr), hr), :]
        theirs_ref[...] = acc_ref[pl.ds(pl.multiple_of((1 - c) * hr, hr), hr), :].astype(BF16)
    else:
        for k in range(NCHIP):
            mine_ref[k] = acc_ref[pl.ds(pl.multiple_of(k * nr + c * hr, hr), hr), :]
            theirs_ref[k] = acc_ref[pl.ds(pl.multiple_of(k * nr + (1 - c) * hr, hr), hr), :].astype(BF16)


def _out_weight_grads(a_in, da_out, s_in, ds_out, merged, dy, core, tk=512):
    t = dy.shape[0]
    nt = t // tk

    def body(c_ref, ain_ref, da_ref, sin_ref, ds_ref, m_ref, dy_ref, *rest):
        outs, (gao, gco, go) = rest[:6], rest[6:]

        @pl.when(pl.program_id(0) == 0)
        def _():
            gao[...] = jnp.zeros_like(gao)
            gco[...] = jnp.zeros_like(gco)
            go[...] = jnp.zeros_like(go)

        gao[...] += _dot_tn(ain_ref[...], da_ref[...])
        gco[...] += _dot_tn(sin_ref[...], ds_ref[...])
        go[...] += _dot_tn(m_ref[...], dy_ref[...])

        @pl.when(pl.program_id(0) == nt - 1)
        def _():
            for a, acc in ((1, gao), (2, gco), (3, go)):
                _write_halves(a, acc, c_ref[0], outs[2 * a - 2], outs[2 * a - 1])

    tok = lambda w: pl.BlockSpec((tk, w), lambda i, cr: (i, 0))
    out_shape = _halves_out(1) + _halves_out(2) + _halves_out(3)
    outs = pl.pallas_call(
        body, name="out_weight_grads", out_shape=out_shape,
        grid_spec=pltpu.PrefetchScalarGridSpec(
            num_scalar_prefetch=1, grid=(nt,), in_specs=[tok(AW), tok(D), tok(D), tok(D), tok(D), tok(D)],
            out_specs=[pl.BlockSpec(o.shape, lambda i, cr, nd=len(o.shape): (0,) * nd) for o in out_shape],
            scratch_shapes=[pltpu.VMEM((AW, D), F32), pltpu.VMEM((D, D), F32), pltpu.VMEM((D, D), F32)]),
        compiler_params=_cp(("arbitrary",), VMEM_CAP),
    )(core, a_in, da_out, s_in, ds_out, merged, dy)
    return [(outs[0], outs[1]), (outs[2], outs[3]), (outs[4], outs[5])]


def _input_grad(dq, dk, dv, dgates, w, x2, dxr, sc1p, seq, sums, tm=512):
    t = x2.shape[0]
    nt = t // tm
    spt = seq // tm
    bsz = t // seq
    n = len(sums)
    gblk = NGATE // 4
    nsteps = 3 + 4

    def body(dq_ref, dk_ref, dv_ref, dg_ref, wq_ref, wg_ref, x_ref, dxr_ref, sc_ref, *rest):
        src, (dx_ref, dsh_ref, dsc_ref), land = rest[:n], rest[n:n + 3], rest[n + 3:2 * n + 3]
        acc_ref, send_sems, recv_sems = rest[2 * n + 3:]
        j, i = pl.program_id(0), pl.program_id(1)
        px, py, pc = _place()
        chips = [(1 - px, py), (px, 1 - py), (1 - px, 1 - py)]
        copies = [pltpu.make_async_remote_copy(src_ref=src[a].at[2 * cx + cy], dst_ref=land[a].at[r], send_sem=send_sems.at[3 * a + r],
                                               recv_sem=recv_sems.at[3 * a + r], device_id=(cx, cy, pc), device_id_type=MESH)
                  for a in range(n) for r, (cx, cy) in enumerate(chips)]
        rows = pl.ds(pl.multiple_of(i * tm, tm), tm)

        @pl.when((i == 0) & (j == 0))
        def _():
            for cp in copies:
                cp.start()

        for k, ref in enumerate((dq_ref, dk_ref, dv_ref)):
            @pl.when(j == k)
            def _(k=k, ref=ref):
                part = _dot_nt(ref[...], wq_ref[...])
                if k == 0:
                    acc_ref[rows, :] = part
                else:
                    acc_ref[rows, :] += part

        @pl.when((j >= 3) & (j < nsteps - 1))
        def _():
            acc_ref[rows, :] += _dot_nt(dg_ref[...], wg_ref[...])

        @pl.when(j == nsteps - 1)
        def _():
            dh = acc_ref[rows, :] + _dot_nt(dg_ref[...], wg_ref[...])
            dx_ref[...] = dh * sc_ref[0] + dxr_ref[...]

            @pl.when(i % spt == 0)
            def _():
                dsh_ref[...] = jnp.zeros_like(dsh_ref)
                dsc_ref[...] = jnp.zeros_like(dsc_ref)

            dsh_ref[0] += jnp.sum(dh, axis=0, keepdims=True)
            dsc_ref[0] += jnp.sum(dh * x_ref[...], axis=0, keepdims=True)

        @pl.when((i == nt - 1) & (j == nsteps - 1))
        def _():
            for cp in copies:
                cp.wait()

    def held(k):
        return lambda j, i: (jnp.where(j == k, i, jnp.where(j < k, 0, nt - 1)), 0)

    last = lambda j, i: (jnp.where(j == nsteps - 1, i, 0), 0)
    outs = pl.pallas_call(
        body, name="input_grad", grid=(nsteps, nt),
        out_shape=[SDS((t, D), F32), SDS((bsz, 1, D), F32), SDS((bsz, 1, D), F32)] + [SDS((3,) + s.shape[1:], BF16) for s in sums],
        in_specs=[pl.BlockSpec((tm, QW), held(0)), pl.BlockSpec((tm, QW), held(1)), pl.BlockSpec((tm, QW), held(2)),
                  pl.BlockSpec((tm, gblk), lambda j, i: (jnp.where(j >= 3, i, 0), jnp.clip(j - 3, 0, 3))),
                  pl.BlockSpec((D, QW), lambda j, i: (0, jnp.minimum(j, 2))),
                  pl.BlockSpec((pl.Element(D), pl.Element(gblk)), lambda j, i: (0, pl.multiple_of(3 * QW + gblk * jnp.clip(j - 3, 0, 3), 128))),
                  pl.BlockSpec((tm, D), last), pl.BlockSpec((tm, D), last),
                  pl.BlockSpec((1, 1, D), lambda j, i: (jnp.where(j == nsteps - 1, i // spt, 0), 0, 0))] + [ANY] * n,
        out_specs=[pl.BlockSpec((tm, D), last),
                   pl.BlockSpec((1, 1, D), lambda j, i: (jnp.where(j == nsteps - 1, i // spt, 0), 0, 0)),
                   pl.BlockSpec((1, 1, D), lambda j, i: (jnp.where(j == nsteps - 1, i // spt, 0), 0, 0))] + [ANY] * n,
        scratch_shapes=[pltpu.VMEM((t, D), F32), pltpu.SemaphoreType.DMA((3 * NCHIP,)), pltpu.SemaphoreType.DMA((3 * NCHIP,))],
        compiler_params=_cp(("arbitrary", "arbitrary"), VMEM_CAP, side=True),
    )(dq, dk, dv, dgates, w, w, x2, dxr, sc1p, *sums)
    return outs[0], outs[1], outs[2], outs[3:]


def _in_weight_grad(ht, src, col0, prev, core, name):
    t = ht.shape[1]
    ncols = src.shape[1] // TN
    hr = D // 2

    def body(c_ref, ht_ref, s_ref, *rest):
        mine_ref, theirs_ref, acc_ref = rest[-3:]
        acc_ref[...] = _dot(ht_ref[...], s_ref[...])
        _write_halves(0, acc_ref, c_ref[0], mine_ref, theirs_ref)

    ins = [core, ht, src]
    in_specs = [pl.BlockSpec((D, t), lambda j, cr: (0, 0)), pl.BlockSpec((t, TN), lambda j, cr: (0, j))]
    aliases = {}
    if prev is not None:
        ins += list(prev)
        in_specs += [ANY] * 2
        aliases = {3: 0, 4: 1}
    out_spec = pl.BlockSpec((hr, TN), lambda j, cr: (0, col0 + j))
    return pl.pallas_call(
        body, name=name, out_shape=[SDS((hr, NCOL), F32), SDS((hr, NCOL), BF16)],
        grid_spec=pltpu.PrefetchScalarGridSpec(num_scalar_prefetch=1, grid=(ncols,), in_specs=in_specs, out_specs=[out_spec, out_spec],
                                               scratch_shapes=[pltpu.VMEM((D, TN), F32)]),
        input_output_aliases=aliases, compiler_params=_cp(("arbitrary",), VMEM_CAP),
    )(*ins)


def _sum_partials(gathered):
    def body(g_ref, o_ref):
        acc = g_ref[0]
        for k in range(1, 8):
            acc = acc + g_ref[k]
        o_ref[...] = acc

    return pl.pallas_call(body, name="sum_partials", out_shape=SDS(gathered.shape[1:], F32), in_specs=[VMEM_SPEC], out_specs=VMEM_SPEC)(gathered)


def _adamw(w, g, m, v, name, tr=256):
    r, cdim = w.shape
    tr = tr if cdim <= D else tr // 2
    tr = tr if (r % tr == 0 and r > tr) else r

    def body(w_ref, g_ref, m_ref, v_ref, go_ref, d_ref, nm_ref, nv_ref):
        gv = g_ref[...]
        go_ref[...] = gv
        nm = B1 * m_ref[...] + (1.0 - B1) * gv
        nv = B2 * v_ref[...] + (1.0 - B2) * (gv * gv)
        m_hat = nm / (1.0 - B1 ** STEP)
        v_hat = nv / (1.0 - B2 ** STEP)
        d_ref[...] = -LR * (m_hat / (jnp.sqrt(v_hat) + EPS) + WD * w_ref[...])
        nm_ref[...] = nm
        nv_ref[...] = nv

    spec = pl.BlockSpec((tr, cdim), lambda i: (i, 0))
    return pl.pallas_call(
        body, name=name, grid=(r // tr,), out_shape=[SDS((r, cdim), F32)] * 4, in_specs=[spec] * 4, out_specs=[spec] * 4,
        compiler_params=_cp(("parallel",), VMEM_CAP // 2),
    )(w, g, m, v)


def _t5_bucket(dist):
    n = jnp.maximum(dist, 1).astype(F32)
    large = MAX_EXACT + (jnp.log(n / MAX_EXACT) / math.log(MAX_DISTANCE / MAX_EXACT) * (N_BUCKETS - MAX_EXACT)).astype(jnp.int32)
    large = jnp.minimum(large, N_BUCKETS - 1)
    return jnp.where(dist < MAX_EXACT, dist, large)


def _band_buckets():
    a = jnp.arange(BLK)[:, None]
    b = jnp.arange(2 * BLK)[None, :]
    steps = jnp.maximum(a + BLK - b, 0)
    return jnp.stack([_t5_bucket(steps * d) for d in DILATIONS]).astype(jnp.int32)


def _pad_rows(a, rows=8):
    return jnp.pad(a, ((0, rows - a.shape[0]), (0, 0)))


def kernel(x, c, w_ada, b_ada, w_in, conv_w, conv_b, rel_bias, w_attn_out, w_conv_out, w_o, ln_g, ln_b, loss_target, m_w_ada, m_b_ada, m_w_in, m_conv_w, m_conv_b, m_rel_bias, m_w_attn_out, m_w_conv_out, m_w_o, m_ln_g, m_ln_b, v_w_ada, v_b_ada, v_w_in, v_conv_w, v_conv_b, v_rel_bias, v_w_attn_out, v_w_conv_out, v_w_o, v_ln_g, v_ln_b):
    bsz, seq, _ = x.shape
    t = bsz * seq
    mx, my, mc = _place()
    chip = 2 * mx + my
    dev = 4 * mx + 2 * my + mc
    x2 = x.reshape(t, D)
    tgt = loss_target.reshape(t, D)

    mine = [_to_bf16_window(a, w[0], f"to_bf16_{a}") for a, w in enumerate((w_in, w_attn_out, w_conv_out, w_o))]

    n_ada = w_ada.shape[2]
    n_cw = conv_w.shape[2]
    c_and_cw = jnp.concatenate([_pad_rows(c), jnp.pad(conv_w[0], ((0, 5), (0, D - n_cw)))], axis=0)
    firsts = _all_gather8(c_and_cw, "gather_c_conv_w")
    c_all = firsts[:, 0:bsz, :].reshape(8 * bsz, D)
    conv_w_f = firsts[0::2, 8:11, 0:n_cw].transpose(1, 0, 2).reshape(3, D)
    b_cols = lax.dynamic_slice(b_ada, (0, chip * n_ada), (1, n_ada))
    mod_part = _ada_forward(c_all, w_ada[0], b_cols)
    mod_parts = _all_gather8(mod_part, "gather_mod")
    mod_all = mod_parts[0::2].transpose(1, 0, 2).reshape(8 * bsz, 3 * D)
    mod = lax.dynamic_slice(mod_all, (dev * bsz, 0), (bsz, 3 * D))
    shift = mod[:, 0:D].reshape(bsz, 1, D)
    sc1p = 1.0 + mod[:, D:2 * D].reshape(bsz, 1, D)
    gate = mod[:, 2 * D:].reshape(bsz, 1, D)

    h, ht = _modulate(x2, sc1p, shift, seq)
    tab = lax.dynamic_index_in_dim(jnp.asarray(_tile_tables()), chip, 0, keepdims=False)
    qkv, gates, (w_in_f, w_ao_f, w_co_f, w_o_f) = _project_gather(h, mine, tab)
    buckets = _band_buckets()
    bias = _bias_tables(rel_bias, buckets)
    og, lg = [], []
    for g in range(3):
        o_g, l_g = _attn_forward(g, qkv, bias[g], bsz, seq)
        og.append(o_g)
        lg.append(l_g)
    (a_in, s_in, merged, dy, a_out, s_out, y_conv, o, lj, dxr, vec_f, dgate) = _mix_forward(
        gates, og, lg, x2, tgt, gate, w_ao_f, w_co_f, w_o_f, conv_w_f, conv_b, ln_g, ln_b, bsz, seq)

    dgates, do, dl, da_out, ds_out, vec = _mix_backward(gates, dy, a_out, s_out, y_conv, o, lj, w_ao_f, w_co_f, w_o_f, conv_w_f, vec_f, bsz, seq)
    core = jnp.reshape(mc, (1,)).astype(jnp.int32)
    small_grads = _out_weight_grads(a_in, da_out, s_in, ds_out, merged, dy, core)
    dqkv, dbs = None, []
    for g in range(3):
        dqkv, db = _attn_backward(g, qkv, do, dl, bias[g], dqkv, bsz, seq)
        dbs.append(db)
    dq, dk, dv = dqkv
    drb = _bias_grad(jnp.stack(dbs), buckets)
    drb = drb[:, :, 0:4].transpose(1, 0, 2).reshape(N_BUCKETS, 12)
    g_in = None
    for n, src in enumerate((dq, dk, dv, dgates)):
        g_in = _in_weight_grad(ht, src, n * NQT, g_in, core, f"in_weight_grad_{n}")

    halves = [tuple(g_in)] + small_grads
    got = _swap_halves([theirs for _, theirs in halves])
    sums = [_chip_sum(a, halves[a][0], got[a], f"chip_sum_{a}") for a in range(4)]
    grad_x, dshift, dscale, landed = _input_grad(dq, dk, dv, dgates, w_in_f, x2, dxr, sc1p, seq, [s[1] for s in sums])
    halves = [_reduce_mine(a, sums[a][0], landed[a], f"reduce_mine_{a}") for a in range(4)]
    gw_in, gw_ao, gw_co, gw_o = _join_halves(halves)

    dmod = jnp.concatenate([dshift, dscale, dgate], axis=2).reshape(bsz * 3, D)
    drb_row = jnp.pad(drb.reshape(1, N_BUCKETS * 12), ((0, 0), (0, D - N_BUCKETS * 12)))
    vec = lax.dynamic_update_slice(vec, drb_row, (7, 0))
    packed = jnp.concatenate([vec, _pad_rows(dmod)], axis=0)
    gathered = _all_gather8(packed, "gather_small")
    small = _sum_partials(gathered)
    g_ln_g, g_ln_b, loss_lanes = small[0:1], small[1:2], small[2:3]
    g_conv_w_full, g_conv_b = small[3:6], small[6:7]
    g_rel_bias = small[7, 0:N_BUCKETS * 12].reshape(N_BUCKETS, 12)
    loss = 0.5 / D * jnp.sum(loss_lanes)
    dmod_all = gathered[:, 8:8 + 3 * bsz, :].reshape(8 * bsz, 3 * D)
    dmod_cols = lax.dynamic_slice(dmod_all, (0, chip * n_ada), (8 * bsz, n_ada))
    gw_ada, gb_ada = _ada_backward(c_all, dmod_cols, dmod_all)
    g_conv_w = lax.dynamic_slice(g_conv_w_full, (0, chip * n_cw), (3, n_cw))

    names = ["w_ada", "b_ada", "w_in", "conv_w", "conv_b", "rel_bias", "w_attn_out", "w_conv_out", "w_o", "ln_g", "ln_b"]
    two_d = lambda a: a.reshape(a.shape[-2:]) if a.ndim == 3 else a
    weights = dict(zip(names, map(two_d, (w_ada, b_ada, w_in, conv_w, conv_b, rel_bias, w_attn_out, w_conv_out, w_o, ln_g, ln_b))))
    ms = dict(zip(names, map(two_d, (m_w_ada, m_b_ada, m_w_in, m_conv_w, m_conv_b, m_rel_bias, m_w_attn_out, m_w_conv_out, m_w_o, m_ln_g, m_ln_b))))
    vs = dict(zip(names, map(two_d, (v_w_ada, v_b_ada, v_w_in, v_conv_w, v_conv_b, v_rel_bias, v_w_attn_out, v_w_conv_out, v_w_o, v_ln_g, v_ln_b))))
    grads = dict(zip(names, (gw_ada, gb_ada, gw_in, g_conv_w, g_conv_b, g_rel_bias, gw_ao, gw_co, gw_o, g_ln_g, g_ln_b)))
    shapes = dict(zip(names, (w_ada, b_ada, w_in, conv_w, conv_b, rel_bias, w_attn_out, w_conv_out, w_o, ln_g, ln_b)))
    grad_out, deltas, new_m, new_v = {}, {}, {}, {}
    for n in names:
        grad_out[n], deltas[n], new_m[n], new_v[n] = _adamw(weights[n], grads[n], ms[n], vs[n], f"adamw_{n}")
    shaped = lambda d: [d[n].reshape(shapes[n].shape) for n in names]
    return (loss, grad_x.reshape(bsz, seq, D), *shaped(grad_out), *shaped(deltas), *shaped(new_m), *shaped(new_v))
```

```python
import math

import numpy as np
import jax
import jax.numpy as jnp
from jax import lax
from jax.experimental import pallas as pl
from jax.experimental.pallas import tpu as pltpu

F32 = jnp.float32
BF16 = jnp.bfloat16
SDS = jax.ShapeDtypeStruct
MESH = pl.DeviceIdType.MESH
HBM_OUT = pltpu.HBM
ANY = pl.BlockSpec(memory_space=pl.ANY)
VMEM_SPEC = pl.BlockSpec(memory_space=pltpu.VMEM)

D = 1024
HD = 128
BLK = 128
QW = 1536
AW = 512
NGATE = 6656
GATE_COLS = ((0, 512), (512, 1536), (1536, 2560), (2560, 3584), (3584, 4608), (4608, 5632), (5632, 6656))
NCOL = 3 * QW + NGATE
TN = 512
NQT = QW // TN
NPT = NCOL // TN
DILATIONS = (1, 4, 16)
N_BUCKETS, MAX_EXACT, MAX_DISTANCE = 32, 16, 2048
ALPHA = 2.0 ** 0.25
LN_EPS = 1e-5
NEG = -1e30
SCALE = HD ** -0.5
LR, B1, B2, EPS, WD, STEP = 0.001, 0.9, 0.999, 1e-08, 0.01, 10
NCHIP = 4
VMEM_CAP = 60 * 2 ** 20


def _cp(sem=None, vmem=None, side=False):
    return pltpu.CompilerParams(dimension_semantics=sem, vmem_limit_bytes=vmem, has_side_effects=side)


def _dot(a, b):
    return jnp.dot(a, b, preferred_element_type=F32)


def _dot_nt(a, b):
    return lax.dot_general(a, b, (((1,), (1,)), ((), ())), preferred_element_type=F32)


def _dot_tn(a, b):
    return lax.dot_general(a, b, (((0,), (0,)), ((), ())), preferred_element_type=F32)


def _sig(x):
    return 1.0 / (1.0 + jnp.exp(-x))


def _in_hbm(a):
    return pltpu.with_memory_space_constraint(a, pltpu.HBM)


def _place():
    x, y, c = lax.axis_index("x"), lax.axis_index("y"), lax.axis_index("c")
    return x, y, c


def _all_gather8(v, name):
    r, cdim = v.shape

    def body(v_ref, out_ref, send_sems, recv_sems, local_sem):
        x, y, c = _place()
        me = 4 * x + 2 * y + c
        peers = [(x, y, 1 - c), (1 - x, y, c), (x, 1 - y, c), (1 - x, 1 - y, c),
                 (1 - x, y, 1 - c), (x, 1 - y, 1 - c), (1 - x, 1 - y, 1 - c)]
        mine = pltpu.make_async_copy(v_ref, out_ref.at[me], local_sem)
        mine.start()

        def copy(k, block, to):
            return pltpu.make_async_remote_copy(src_ref=v_ref, dst_ref=out_ref.at[block], send_sem=send_sems.at[k],
                                                recv_sem=recv_sems.at[k], device_id=to, device_id_type=MESH)

        sends = [copy(k, me, p) for k, p in enumerate(peers)]
        for cp in sends:
            cp.start()
        for k, (px, py, pc) in enumerate(peers):
            copy(k, 4 * px + 2 * py + pc, (px, py, pc)).wait_recv()
        for cp in sends:
            cp.wait_send()
        mine.wait()

    return pl.pallas_call(
        body, name=name, out_shape=SDS((8, r, cdim), v.dtype), in_specs=[VMEM_SPEC], out_specs=VMEM_SPEC,
        scratch_shapes=[pltpu.SemaphoreType.DMA((7,)), pltpu.SemaphoreType.DMA((7,)), pltpu.SemaphoreType.DMA(())],
        compiler_params=_cp(side=True),
    )(v)


W_CUTS = (("col", D, NCOL // NCHIP), ("col", AW, D // NCHIP), ("row", D // NCHIP, D), ("row", D // NCHIP, D))
W_FULL = ((D, NCOL), (AW, D), (D, D), (D, D))


def _shard_window(ref, cut, k, half):
    kind, nr, nc = cut
    hr = nr // 2
    if kind == "col":
        rows = pl.ds(0, nr) if half is None else pl.ds(pl.multiple_of(half * hr, 16), hr)
        return ref.at[rows, pl.ds(pl.multiple_of(k * nc, 128), nc)]
    if half is None:
        return ref.at[pl.ds(pl.multiple_of(k * nr, 16), nr), :]
    return ref.at[pl.ds(pl.multiple_of(k * nr + half * hr, 16), hr), :]


def _half_rows(ref, cut, half):
    hr = cut[1] // 2
    return ref.at[pl.ds(pl.multiple_of(half * hr, 16), hr), :]


def _to_bf16_windows(ws):
    x, y, _ = _place()
    chip = jnp.reshape(2 * x + y, (1,)).astype(jnp.int32)
    tr = 256
    n = len(ws)

    def body(c_ref, *refs):
        src, dst = refs[:n], refs[n:]
        dst[0][...] = src[0][...].astype(BF16)

        @pl.when(pl.program_id(0) == 0)
        def _():
            for a in range(1, n):
                dst[a][...] = src[a][...].astype(BF16)

    in_specs = [pl.BlockSpec((tr, W_CUTS[0][2]), lambda i, cr: (i, 0))]
    out_specs = [pl.BlockSpec((tr, W_CUTS[0][2]), lambda i, cr: (i, cr[0]))]
    for a in range(1, n):
        kind, nr, nc = W_CUTS[a]
        in_specs.append(pl.BlockSpec((nr, nc), lambda i, cr: (0, 0)))
        out_specs.append(pl.BlockSpec((nr, nc), (lambda i, cr: (0, cr[0])) if kind == "col" else (lambda i, cr: (cr[0], 0))))
    return pl.pallas_call(
        body, name="to_bf16", out_shape=[SDS(W_FULL[a], BF16) for a in range(n)],
        grid_spec=pltpu.PrefetchScalarGridSpec(num_scalar_prefetch=1, grid=(D // tr,), in_specs=in_specs, out_specs=out_specs),
        compiler_params=_cp(("arbitrary",)),
    )(chip, *ws)


def _swap_halves(theirs):
    n = len(theirs)

    def body(*refs):
        src, land = refs[:n], refs[n:2 * n]
        send_sems, recv_sems = refs[2 * n:]
        x, y, c = _place()
        copies = [pltpu.make_async_remote_copy(src_ref=src[a], dst_ref=land[a], send_sem=send_sems.at[a], recv_sem=recv_sems.at[a],
                                               device_id=(x, y, 1 - c), device_id_type=MESH) for a in range(n)]
        for cp in copies:
            cp.start()
        for cp in copies:
            cp.wait()

    return pl.pallas_call(
        body, name="swap_grad_halves", out_shape=[SDS(v.shape, v.dtype) for v in theirs], in_specs=[ANY] * n, out_specs=[ANY] * n,
        scratch_shapes=[pltpu.SemaphoreType.DMA((n,)), pltpu.SemaphoreType.DMA((n,))],
        compiler_params=_cp(side=True),
    )(*theirs)


def _chip_sums(mines, gots):
    n = len(mines)
    x, y, _ = _place()
    me = jnp.reshape(2 * x + y, (1,)).astype(jnp.int32)

    def body(me_ref, *refs):
        ins, outs = refs[:2 * n], refs[2 * n:]
        for a in range(n):
            hr, nc = W_CUTS[a][1] // 2, W_CUTS[a][2]
            s = (ins[2 * a][...] + ins[2 * a + 1][...].astype(F32)).reshape(hr, nc)
            outs[2 * a + 1][0] = s.astype(BF16)

            @pl.when(pl.program_id(0) == me_ref[0])
            def _(a=a, s=s):
                outs[2 * a][...] = s

    in_specs, out_specs, out_shape = [], [], []
    for a in range(n):
        kind, nr, nc = W_CUTS[a]
        hr = nr // 2
        spec = pl.BlockSpec((hr, nc), lambda k, mr: (0, k)) if kind == "col" else pl.BlockSpec((1, hr, nc), lambda k, mr: (k, 0, 0))
        in_specs += [spec, spec]
        out_specs += [pl.BlockSpec((hr, nc), lambda k, mr: (0, 0)), pl.BlockSpec((1, hr, nc), lambda k, mr: (k, 0, 0))]
        out_shape += [SDS((hr, nc), F32), SDS((NCHIP, hr, nc), BF16)]
    outs = pl.pallas_call(
        body, name="chip_sums", out_shape=out_shape,
        grid_spec=pltpu.PrefetchScalarGridSpec(num_scalar_prefetch=1, grid=(NCHIP,), in_specs=in_specs, out_specs=out_specs),
        compiler_params=_cp(("arbitrary",), VMEM_CAP),
    )(me, *[v for pair in zip(mines, gots) for v in pair])
    return [(outs[2 * a], outs[2 * a + 1]) for a in range(n)]


def _reduce_mine(mines, gots):
    n = len(mines)
    _, _, c = _place()
    core = jnp.reshape(c, (1,)).astype(jnp.int32)
    tr = 256
    nsteps = W_CUTS[0][1] // 2 // tr

    def body(c_ref, *refs):
        ins, outs = refs[:2 * n], refs[2 * n:]

        def add(a):
            m_ref, g_ref = ins[2 * a], ins[2 * a + 1]
            outs[a][...] = ((m_ref[...] + g_ref[0].astype(F32)) + g_ref[1].astype(F32)) + g_ref[2].astype(F32)

        add(0)

        @pl.when(pl.program_id(0) == 0)
        def _():
            for a in range(1, n):
                add(a)

    nc0 = W_CUTS[0][2]
    in_specs = [pl.BlockSpec((tr, nc0), lambda i, cr: (i, 0)), pl.BlockSpec((3, tr, nc0), lambda i, cr: (0, i, 0))]
    out_specs = [pl.BlockSpec((tr, nc0), lambda i, cr: (cr[0] * nsteps + i, 0))]
    for a in range(1, n):
        hr, nc = W_CUTS[a][1] // 2, W_CUTS[a][2]
        in_specs += [pl.BlockSpec((hr, nc), lambda i, cr: (0, 0)), pl.BlockSpec((3, hr, nc), lambda i, cr: (0, 0, 0))]
        out_specs.append(pl.BlockSpec((hr, nc), lambda i, cr: (cr[0], 0)))
    return pl.pallas_call(
        body, name="reduce_mine", out_shape=[SDS((W_CUTS[a][1], W_CUTS[a][2]), F32) for a in range(n)],
        grid_spec=pltpu.PrefetchScalarGridSpec(num_scalar_prefetch=1, grid=(nsteps,), in_specs=in_specs, out_specs=out_specs),
        compiler_params=_cp(("arbitrary",), VMEM_CAP),
    )(core, *[v for pair in zip(mines, gots) for v in pair])


def _join_halves(fulls):
    n = len(fulls)

    def body(*refs):
        full = refs[n:2 * n]
        send_sems, recv_sems = refs[2 * n:]
        x, y, c = _place()
        sibling = (x, y, 1 - c)

        def swap(a, half):
            rows = _half_rows(full[a], W_CUTS[a], half)
            return pltpu.make_async_remote_copy(src_ref=rows, dst_ref=rows, send_sem=send_sems.at[a], recv_sem=recv_sems.at[a],
                                                device_id=sibling, device_id_type=MESH)

        sends = [swap(a, c) for a in range(n)]
        for cp in sends:
            cp.start()
        for a, cp in enumerate(sends):
            cp.wait_send()
            swap(a, 1 - c).wait_recv()

    return pl.pallas_call(
        body, name="join_grad_halves", out_shape=[SDS((W_CUTS[a][1], W_CUTS[a][2]), F32) for a in range(n)],
        in_specs=[ANY] * n, out_specs=[ANY] * n,
        scratch_shapes=[pltpu.SemaphoreType.DMA((n,)), pltpu.SemaphoreType.DMA((n,))],
        input_output_aliases={a: a for a in range(n)}, compiler_params=_cp(side=True),
    )(*fulls)


def _ada_forward(c_all, w_ada, b_cols):
    nb, nc = c_all.shape[0], w_ada.shape[1]

    def body(c_ref, w_ref, b_ref, o_ref):
        cv = c_ref[...]
        sc = (cv * _sig(cv)).astype(BF16)
        o_ref[...] = _dot(sc, w_ref[...].astype(BF16)) + b_ref[...]

    return pl.pallas_call(body, name="ada_forward", out_shape=SDS((nb, nc), F32), compiler_params=_cp(vmem=VMEM_CAP // 2))(c_all, w_ada, b_cols)


def _ada_backward(c_all, dmod_cols, dmod_all):
    nb, nc = dmod_cols.shape

    def body(c_ref, d_ref, a_ref, gw_ref, gb_ref):
        cv = c_ref[...]
        sc = (cv * _sig(cv)).astype(BF16)
        gw_ref[...] = _dot_tn(sc, d_ref[...].astype(BF16))
        gb_ref[...] = jnp.sum(a_ref[...], axis=0, keepdims=True)

    return pl.pallas_call(body, name="ada_backward", out_shape=[SDS((D, nc), F32), SDS((1, dmod_all.shape[1]), F32)],
                          compiler_params=_cp(vmem=VMEM_CAP // 2))(c_all, dmod_cols, dmod_all)


def _modulate(x2, sc1p, shift, seq, tm=256):
    t = x2.shape[0]
    spt = seq // tm

    def body(x_ref, sc_ref, sh_ref, h_ref, ht_ref):
        h = x_ref[...] * sc_ref[0] + sh_ref[0]
        h_ref[...] = h.astype(BF16)
        ht_ref[...] = h.T.astype(BF16)

    per_seq = pl.BlockSpec((1, 1, D), lambda i: (i // spt, 0, 0))
    return pl.pallas_call(
        body, name="modulate", out_shape=[HBM_OUT((t, D), BF16), HBM_OUT((D, t), BF16)], grid=(t // tm,),
        in_specs=[pl.BlockSpec((tm, D), lambda i: (i, 0)), per_seq, per_seq],
        out_specs=[pl.BlockSpec((tm, D), lambda i: (i, 0)), pl.BlockSpec((D, tm), lambda i: (0, i))],
        compiler_params=_cp(("parallel",)),
    )(x2, sc1p, shift)


TW = 256
TPS = NCOL // NCHIP // TW
NT = NCOL // TW
NQKV_T = 3 * QW // TW
N_TILE_SEMS = 2 * 3 * TPS


def _tile_tables():
    tabs = np.zeros((NCHIP, 3, NT), np.int32)
    for me in range(NCHIP):
        tiles = [TPS * (me ^ (s // TPS)) + s % TPS for s in range(NT)]
        tabs[me, 0] = tiles
        for row, (lo, hi) in enumerate(((0, NQKV_T), (NQKV_T, NT))):
            mine = [w - lo if lo <= w < hi else None for w in tiles]
            held = next(m for m in mine if m is not None)
            for s, m in enumerate(mine):
                held = held if m is None else m
                tabs[me, 1 + row, s] = held
    return tabs


def _project_gather(h, fulls, tab):
    t = h.shape[0]
    n = len(fulls)

    def body(tab_ref, h_ref, *rest):
        qkv_ref, g_ref = rest[n], rest[n + 1]
        full = rest[n + 2:2 * n + 2]
        w_buf, tile_sems, send_sems, recv_sems = rest[2 * n + 2:]
        s = pl.program_id(0)
        x, y, c = _place()
        me = 2 * x + y
        peers = [(x, 1 - y), (1 - x, y), (1 - x, 1 - y)]
        sibling = (x, y, 1 - c)

        def hop(a, r, stage, chip, half, to):
            window = _shard_window(full[a], W_CUTS[a], chip, half)
            k = N_TILE_SEMS + 6 * (a - 1) + 2 * r + stage
            return pltpu.make_async_remote_copy(src_ref=window, dst_ref=window, send_sem=send_sems.at[k], recv_sem=recv_sems.at[k],
                                                device_id=to, device_id_type=MESH)

        def tile_hop(q, stage, col_step, half, to):
            col = pl.multiple_of(tab_ref[0, col_step] * TW, TW)
            window = full[0].at[pl.ds(pl.multiple_of(half * (D // 2), 16), D // 2), pl.ds(col, TW)]
            k = 2 * (q - TPS) + stage
            return pltpu.make_async_remote_copy(src_ref=window, dst_ref=window, send_sem=send_sems.at[k], recv_sem=recv_sems.at[k],
                                                device_id=to, device_id_type=MESH)

        def send_tile(r, j):
            return tile_hop(TPS * (r + 1) + j, 0, j, c, (*peers[r], c))

        def arrive(a, r):
            px, py = peers[r]
            chip = 2 * px + py
            hop(a, r, 0, chip, c, (px, py, c)).wait_recv()
            hop(a, r, 1, chip, c, sibling).start()
            hop(a, r, 1, chip, 1 - c, sibling).wait_recv()

        def tile(step, slot):
            col = pl.multiple_of(tab_ref[0, step] * TW, TW)
            return pltpu.make_async_copy(full[0].at[:, pl.ds(col, TW)], w_buf.at[slot], tile_sems.at[slot])

        @pl.when(s == 0)
        def _():
            for r in range(2):
                for j in range(TPS):
                    send_tile(r, j).start()
            tile(0, 0).start()

        @pl.when((s + 1 >= TPS) & (s + 1 < NT))
        def _():
            tile_hop(s + 1, 1, s + 1, 1 - c, sibling).wait_recv()

        @pl.when(s + 1 < NT)
        def _():
            tile(s + 1, 1 - (s % 2)).start()

        @pl.when((s + 2 >= TPS) & (s + 2 < NT))
        def _():
            tile_hop(s + 2, 0, s + 2, c, sibling).wait_recv()
            tile_hop(s + 2, 1, s + 2, c, sibling).start()

        @pl.when(s + 2 == 2 * TPS - 1)
        def _():
            for j in range(TPS):
                send_tile(2, j).start()
            for a in range(1, n):
                for r in range(3):
                    hop(a, r, 0, me, c, (*peers[r], c)).start()

        slot = s % 2
        tile(s, slot).wait()
        is_qkv = tab_ref[0, s] < NQKV_T
        for k in range(2):
            @pl.when(slot == k)
            def _(k=k):
                acc = _dot(h_ref[...], w_buf[k])

                @pl.when(is_qkv)
                def _():
                    qkv_ref[...] = acc.astype(BF16)

                @pl.when(jnp.logical_not(is_qkv))
                def _():
                    g_ref[...] = acc.astype(BF16)

        @pl.when(s == NT - 1)
        def _():
            for a in range(1, n):
                for r in range(3):
                    arrive(a, r)
            for r in range(3):
                for j in range(TPS):
                    send_tile(r, j).wait_send()
                    tile_hop(TPS * (r + 1) + j, 1, TPS * (r + 1) + j, c, sibling).wait_send()
                for a in range(1, n):
                    hop(a, r, 0, me, c, (*peers[r], c)).wait_send()
                    px, py = peers[r]
                    hop(a, r, 1, 2 * px + py, c, sibling).wait_send()

    n_sems = N_TILE_SEMS + 6 * (n - 1)
    outs = pl.pallas_call(
        body, name="project_gather", out_shape=[HBM_OUT((t, 3 * QW), BF16), HBM_OUT((t, NGATE), BF16)] + [SDS(s, BF16) for s in W_FULL],
        grid_spec=pltpu.PrefetchScalarGridSpec(
            num_scalar_prefetch=1, grid=(NT,),
            in_specs=[pl.BlockSpec((t, D), lambda s, tab: (0, 0))] + [ANY] * n,
            out_specs=[pl.BlockSpec((t, TW), lambda s, tab: (0, tab[1, s])), pl.BlockSpec((t, TW), lambda s, tab: (0, tab[2, s]))] + [ANY] * n,
            scratch_shapes=[pltpu.VMEM((2, D, TW), BF16), pltpu.SemaphoreType.DMA((2,)),
                            pltpu.SemaphoreType.DMA((n_sems,)), pltpu.SemaphoreType.DMA((n_sems,))]),
        input_output_aliases={2 + a: 2 + a for a in range(n)},
        compiler_params=_cp(("arbitrary",), VMEM_CAP, side=True),
    )(tab, _in_hbm(h), *fulls)
    return outs[0], outs[1], outs[2:]


def _bias_tables(rel_bias, buckets):
    def body(tab_ref, bk_ref, o_ref):
        a = lax.broadcasted_iota(jnp.int32, (BLK, 2 * BLK), 0)
        b = lax.broadcasted_iota(jnp.int32, (BLK, 2 * BLK), 1)
        steps = a + BLK - b
        valid = (steps >= 0) & (steps <= BLK)
        for g in range(3):
            bk = bk_ref[g]
            for j in range(4):
                def pick(kk, acc, bk=bk, col=4 * g + j):
                    return jnp.where(bk == kk, tab_ref[kk, col], acc)

                acc = lax.fori_loop(0, N_BUCKETS, pick, jnp.zeros((BLK, 2 * BLK), F32))
                o_ref[g, j] = jnp.where(valid, acc, NEG)

    return pl.pallas_call(
        body, name="bias_tables", out_shape=SDS((3, 4, BLK, 2 * BLK), F32),
        in_specs=[pl.BlockSpec(memory_space=pltpu.SMEM), VMEM_SPEC], out_specs=VMEM_SPEC,
    )(rel_bias, buckets)


def _bias_grad(ds_sum, buckets):
    def body(ds_ref, bk_ref, o_ref):
        lane = lax.broadcasted_iota(jnp.int32, (1, 128), 1)
        for g in range(3):
            def bucket(kk, carry, g=g):
                row = jnp.zeros((1, 128), F32)
                for j in range(4):
                    v = jnp.where(bk_ref[g] == kk, ds_ref[g, j], 0.0)
                    v = jnp.sum(v.reshape(BLK // 8, 8, 2 * BLK), axis=0)
                    s = jnp.sum(jnp.sum(v, axis=1, keepdims=True), axis=0, keepdims=True)
                    row = jnp.where(lane == j, s, row)
                o_ref[g, pl.ds(kk, 1), :] = row
                return carry

            lax.fori_loop(0, N_BUCKETS, bucket, 0)

    return pl.pallas_call(body, name="bias_grad", out_shape=SDS((3, N_BUCKETS, 128), F32), in_specs=[VMEM_SPEC, VMEM_SPEC],
                          out_specs=VMEM_SPEC)(ds_sum, buckets)


def _sub_rows(d, r, first, size):
    return pl.ds(first * d + r, size) if d == 1 else pl.ds(first * d + r, size, stride=d)


def _head_spec(seq, g, part):
    return pl.BlockSpec((seq, HD), lambda b, hh: (b, part * (QW // HD) + 4 * g + hh))


def _rows(start, count, stride):
    return pl.ds(start, count) if stride == 1 else pl.ds(start, count, stride=stride)


def _gather_rows(dst, dst0, src, src0, stride, count):
    for first in range(0, count, BLK):
        dst[pl.ds(dst0 + first, BLK), :] = src[_rows(src0 + first * stride, BLK, stride), :].astype(dst.dtype)


def _scatter_rows(dst, dst0, stride, src, src0, count):
    for first in range(0, count, BLK):
        dst[_rows(dst0 + first * stride, BLK, stride), :] = src[pl.ds(src0 + first, BLK), :].astype(dst.dtype)


def _by_subsequence(dst, src, d, wide=None, tmp=None):
    seq = src.shape[0]
    ln = seq // d
    if wide is not None:
        wide[...] = src[...].astype(F32)
        src = wide
    if d <= 4:
        for r in range(d):
            _gather_rows(dst, r * ln, src, r, d, ln)
    else:
        quarter = seq // 4
        for r4 in range(4):
            _gather_rows(tmp, r4 * quarter, src, r4, 4, quarter)
        for r4 in range(4):
            for a in range(d // 4):
                _gather_rows(dst, (4 * a + r4) * ln, tmp, r4 * quarter + a, d // 4, ln)


def _to_sequence(dst, src, d, tmp=None):
    seq = dst.shape[0]
    ln = seq // d
    if d <= 4:
        for r in range(d):
            _scatter_rows(dst, r, d, src, r * ln, ln)
    else:
        quarter = seq // 4
        for r4 in range(4):
            for a in range(d // 4):
                _scatter_rows(tmp, r4 * quarter + a, d // 4, src, (4 * a + r4) * ln, ln)
        for r4 in range(4):
            _scatter_rows(dst, r4, 4, tmp, r4 * quarter, quarter)


def _attn_forward(g, qkv, bias, bsz, seq):
    d = DILATIONS[g]
    ln = seq // d
    units = [(r, n) for r in range(d) for n in range(ln // BLK)]

    def band(n):
        return slice(BLK, 2 * BLK) if n == 0 else slice(0, 2 * BLK)

    def body(q_ref, k_ref, v_ref, b_ref, o_ref, l_ref, *scratch):
        hs = pl.program_id(1)
        s_scr, p_scr = scratch[:2]
        if d == 1:
            qd, kd, vd = q_ref, k_ref, v_ref
        else:
            wide, tmp, qd, kd, vd = scratch[2:7]
            for dst, src in ((qd, q_ref), (kd, k_ref), (vd, v_ref)):
                _by_subsequence(dst, src, d, wide, tmp)
        blk = lambda r, n: pl.ds(r * ln + n * BLK, BLK)
        direct = d <= 4
        out_rows = (lambda r, n: _sub_rows(d, r, n * BLK, BLK)) if direct else blk
        o_dst, l_dst = (o_ref, l_ref) if direct else scratch[7:9]
        for u, (r, n) in enumerate(units):
            s_scr[u, :, BLK:] = _dot_nt(qd[blk(r, n), :], kd[blk(r, n), :])
            if n > 0:
                s_scr[u, :, :BLK] = _dot_nt(qd[blk(r, n), :], kd[blk(r, n - 1), :])
        for u, (r, n) in enumerate(units):
            s = s_scr[u, :, band(n)] * SCALE + b_ref[hs, :, band(n)]
            m = jnp.max(s, axis=1, keepdims=True)
            e = jnp.exp(s - m)
            den = jnp.sum(e, axis=1, keepdims=True)
            p_scr[u, :, band(n)] = (e * (1.0 / den)).astype(BF16)
            l_dst[out_rows(r, n), :] = jnp.broadcast_to(m + jnp.log(den), (BLK, HD))
        for u, (r, n) in enumerate(units):
            acc = _dot(p_scr[u, :, BLK:], vd[blk(r, n), :])
            if n > 0:
                acc = acc + _dot(p_scr[u, :, :BLK], vd[blk(r, n - 1), :])
            o_dst[out_rows(r, n), :] = acc
        if not direct:
            _to_sequence(o_ref, o_dst, d, tmp)
            _to_sequence(l_ref, l_dst, d, tmp)

    rows_f32, rows_bf16 = pltpu.VMEM((seq, HD), F32), pltpu.VMEM((seq, HD), BF16)
    regrouped = [] if d == 1 else [rows_f32] * 2 + [rows_bf16] * 3 + ([] if d <= 4 else [rows_f32] * 2)
    out_spec = pl.BlockSpec((seq, HD), lambda b, hh: (b, hh))
    return pl.pallas_call(
        body, name=f"attn_forward_{g}", out_shape=[HBM_OUT((bsz * seq, AW), F32)] * 2, grid=(bsz, 4),
        in_specs=[_head_spec(seq, g, part) for part in range(3)] + [pl.BlockSpec((4, BLK, 2 * BLK), lambda b, hh: (0, 0, 0))],
        out_specs=[out_spec, out_spec],
        scratch_shapes=[pltpu.VMEM((len(units), BLK, 2 * BLK), F32), pltpu.VMEM((len(units), BLK, 2 * BLK), BF16)] + regrouped,
        compiler_params=_cp(("parallel", "parallel"), VMEM_CAP // 2),
    )(qkv, qkv, qkv, _in_hbm(bias))


def _attn_backward(g, qkv, do, dl, bias, prev_out, bsz, seq):
    d = DILATIONS[g]
    ln = seq // d
    units = [(r, n) for r in range(d) for n in range(ln // BLK)]

    def body(q_ref, k_ref, v_ref, do_ref, dl_ref, b_ref, *rest):
        dq_ref, dk_ref, dv_ref, db_ref = rest[-18:-14]
        wide, tmp, qd, kd, vd, dod, dld, dqd, dkd, dvd, s_scr, dp_scr, p_scr, ds_scr = rest[-14:]
        hs = pl.program_id(1)

        @pl.when((pl.program_id(0) == 0) & (hs == 0))
        def _():
            db_ref[...] = jnp.zeros_like(db_ref)

        for dst, src in ((qd, q_ref), (kd, k_ref), (vd, v_ref)):
            _by_subsequence(dst, src, d, wide, tmp)
        _by_subsequence(dod, do_ref, d, None, tmp)
        _by_subsequence(dld, dl_ref, d, None, tmp)
        dkd[...] = jnp.zeros_like(dkd)
        dvd[...] = jnp.zeros_like(dvd)
        blk = lambda r, n: pl.ds(r * ln + n * BLK, BLK)
        keys = lambda r, n: [(blk(r, n), slice(BLK, 2 * BLK))] + ([(blk(r, n - 1), slice(0, BLK))] if n > 0 else [])
        for u, (r, n) in enumerate(units):
            for rows, band in keys(r, n):
                s_scr[u, :, band] = _dot_nt(qd[blk(r, n), :], kd[rows, :])
                dp_scr[u, :, band] = _dot_nt(dod[blk(r, n), :], vd[rows, :])
        for u, (r, n) in enumerate(units):
            both = dld[blk(r, n), :]
            lse, delta = both[:, 0:1], both[:, 64:65]
            band = slice(BLK, 2 * BLK) if n == 0 else slice(0, 2 * BLK)
            p = jnp.exp(s_scr[u, :, band] * SCALE + b_ref[hs, :, band] - lse)
            ds = p * (dp_scr[u, :, band] - delta)
            p_scr[u, :, band] = p.astype(BF16)
            ds_scr[u, :, band] = ds.astype(BF16)
            db_ref[hs, :, band] += ds
        for u, (r, n) in enumerate(units):
            dq = jnp.zeros((BLK, HD), F32)
            for rows, band in keys(r, n):
                dvd[rows, :] += _dot_tn(p_scr[u, :, band], dod[blk(r, n), :])
                dkd[rows, :] += _dot_tn(ds_scr[u, :, band], qd[blk(r, n), :]) * SCALE
                dq = dq + _dot(ds_scr[u, :, band], kd[rows, :])
            dqd[blk(r, n), :] = dq * SCALE
        for out, acc in ((dq_ref, dqd), (dk_ref, dkd), (dv_ref, dvd)):
            if d == 1:
                out[...] = acc[...].astype(BF16)
            else:
                _to_sequence(wide, acc, d, tmp)
                out[...] = wide[...].astype(BF16)

    qkv_spec = _head_spec(seq, g, 0)
    out_spec = pl.BlockSpec((seq, HD), lambda b, hh: (b, hh))
    band_spec = pl.BlockSpec((4, BLK, 2 * BLK), lambda b, hh: (0, 0, 0))
    ins = [qkv, qkv, qkv, _in_hbm(do), _in_hbm(dl), _in_hbm(bias)]
    in_specs = [_head_spec(seq, g, part) for part in range(3)] + [out_spec, out_spec, band_spec]
    aliases = {}
    if prev_out is not None:
        ins += list(prev_out)
        in_specs += [ANY] * 3
        aliases = {6: 0, 7: 1, 8: 2}
    rows_bf16, rows_f32 = pltpu.VMEM((seq, HD), BF16), pltpu.VMEM((seq, HD), F32)
    staged = [pltpu.VMEM((len(units), BLK, 2 * BLK), F32)] * 2 + [pltpu.VMEM((len(units), BLK, 2 * BLK), BF16)] * 2
    dq, dk, dv, db = pl.pallas_call(
        body, name=f"attn_backward_{g}", out_shape=[HBM_OUT((bsz * seq, QW), BF16)] * 3 + [SDS((4, BLK, 2 * BLK), F32)], grid=(bsz, 4),
        in_specs=in_specs, out_specs=[qkv_spec] * 3 + [band_spec], input_output_aliases=aliases,
        scratch_shapes=[rows_f32] * 2 + [rows_bf16] * 4 + [rows_f32] * 4 + staged,
        compiler_params=_cp(("arbitrary", "arbitrary"), VMEM_CAP // 2),
    )(*ins)
    return (dq, dk, dv), db


def _mix_forward(gates, og, lg, x2, tgt, gate, w_ao, w_co, w_o, conv_w, conv_b, ln_g, ln_b, bsz, seq, tm=256):
    t = x2.shape[0]
    spt = seq // tm

    def body(g_ref, o1, o2, o3, l1, l2, l3, x_ref, t_ref, gate_ref, wao_ref, wco_ref, wo_ref, cw_ref, cb_ref, lng_ref, lnb_ref,
             ain_ref, sin_ref, mrg_ref, dy_ref, aout_ref, sout_ref, yc_ref, o_ref, lj_ref, dxr_ref, vec_ref, dgate_ref, zc_ref):
        b, i = pl.program_id(0), pl.program_id(1)

        @pl.when((b == 0) & (i == 0))
        def _():
            vec_ref[...] = jnp.zeros_like(vec_ref)

        @pl.when(i == 0)
        def _():
            zc_ref[...] = jnp.zeros_like(zc_ref)
            dgate_ref[...] = jnp.zeros_like(dgate_ref)

        g_attn, u, bg, cg, g_conv, m_attn, m_conv = (g_ref[:, lo:hi].astype(F32) for lo, hi in GATE_COLS)
        la, lb, lc = l1[...], l2[...], l3[...]
        mx = jnp.maximum(la, jnp.maximum(lb, lc))
        ea, eb, ec = jnp.exp(la - mx), jnp.exp(lb - mx), jnp.exp(lc - mx)
        den = ea + eb + ec
        o = (ea * o1[...] + eb * o2[...] + ec * o3[...]) / den
        o_ref[...] = o
        lj_ref[...] = mx + jnp.log(den)
        a_in = o * (g_attn * _sig(g_attn))
        ain_ref[...] = a_in.astype(BF16)
        a_out = _dot(a_in.astype(BF16), wao_ref[...])
        aout_ref[...] = a_out.astype(BF16)
        z = cg * u
        rows = lax.broadcasted_iota(jnp.int32, (tm, D), 0)
        c6, c7 = zc_ref[6:7, :], zc_ref[7:8, :]
        z1 = jnp.where(rows == 0, c7, pltpu.roll(z, 1, 0))
        z2 = jnp.where(rows == 0, c6, jnp.where(rows == 1, c7, pltpu.roll(z, 2, 0)))
        zc_ref[...] = z[tm - 8:tm, :]
        y_conv = (cw_ref[0:1, :] * z2 + cw_ref[1:2, :] * z1 + cw_ref[2:3, :] * z) + cb_ref[...]
        yc_ref[...] = y_conv.astype(BF16)
        s_in = bg * y_conv * (g_conv * _sig(g_conv))
        sin_ref[...] = s_in.astype(BF16)
        s_out = _dot(s_in.astype(BF16), wco_ref[...])
        sout_ref[...] = s_out.astype(BF16)
        merged = _sig(m_attn) * a_out + _sig(m_conv) * s_out
        mrg_ref[...] = merged.astype(BF16)
        y = _dot(merged.astype(BF16), wo_ref[...])
        gate1 = 1.0 + gate_ref[0]
        r = ALPHA * x_ref[...] + gate1 * y
        mu = jnp.mean(r, axis=1, keepdims=True)
        rc = r - mu
        rstd = lax.rsqrt(jnp.mean(rc * rc, axis=1, keepdims=True) + LN_EPS)
        xhat = rc * rstd
        diff = (xhat * lng_ref[...] + lnb_ref[...]) - t_ref[...]
        dout = diff * (1.0 / D)
        vec_ref[0:1, :] += jnp.sum(dout * xhat, axis=0, keepdims=True)
        vec_ref[1:2, :] += jnp.sum(dout, axis=0, keepdims=True)
        vec_ref[2:3, :] += jnp.sum(diff * diff, axis=0, keepdims=True)
        dxh = dout * lng_ref[...]
        dr = rstd * (dxh - jnp.mean(dxh, axis=1, keepdims=True) - xhat * jnp.mean(dxh * xhat, axis=1, keepdims=True))
        dxr_ref[...] = ALPHA * dr
        dy_ref[...] = (dr * gate1).astype(BF16)
        dgate_ref[0] += jnp.sum(dr * y, axis=0, keepdims=True)

    tok = lambda w: pl.BlockSpec((tm, w), lambda b, i: (b * spt + i, 0))
    const = lambda s: pl.BlockSpec(s, lambda b, i: (0,) * len(s))
    per_seq = pl.BlockSpec((1, 1, D), lambda b, i: (b, 0, 0))
    outs = pl.pallas_call(
        body, name="mix_forward", grid=(bsz, spt),
        out_shape=[HBM_OUT((t, AW), BF16), HBM_OUT((t, D), BF16), HBM_OUT((t, D), BF16), HBM_OUT((t, D), BF16), HBM_OUT((t, D), BF16),
                   HBM_OUT((t, D), BF16), HBM_OUT((t, D), BF16), HBM_OUT((t, AW), F32), HBM_OUT((t, AW), F32), HBM_OUT((t, D), F32),
                   SDS((8, D), F32), SDS((bsz, 1, D), F32)],
        in_specs=[tok(NGATE)] + [tok(AW)] * 6 + [tok(D), tok(D), per_seq, const((AW, D)), const((D, D)), const((D, D)),
                                                 const((3, D)), const((1, D)), const((1, D)), const((1, D))],
        out_specs=[tok(AW), tok(D), tok(D), tok(D), tok(D), tok(D), tok(D), tok(AW), tok(AW), tok(D), const((8, D)), per_seq],
        scratch_shapes=[pltpu.VMEM((8, D), F32)],
        compiler_params=_cp(("arbitrary", "arbitrary"), VMEM_CAP),
    )(gates, *map(_in_hbm, og), *map(_in_hbm, lg), x2, tgt, gate, w_ao, w_co, w_o, conv_w, conv_b, ln_g, ln_b)
    return outs


def _mix_backward(gates, dy, a_out, s_out, y_conv, o, lj, w_ao, w_co, w_o, conv_w, vec_f, bsz, seq, tm=256):
    t = dy.shape[0]
    spt = seq // tm

    def body(g_ref, dy_ref, aout_ref, sout_ref, yc_ref, o_ref, lj_ref, wao_ref, wco_ref, wo_ref, cw_ref, vecf_ref,
             dg_ref, do_ref, dl_ref, daout_ref, dsout_ref, vec_ref, car_ref):
        b, i = pl.program_id(0), pl.program_id(1)

        @pl.when((b == 0) & (i == 0))
        def _():
            vec_ref[...] = vecf_ref[...]

        @pl.when(i == 0)
        def _():
            car_ref[...] = jnp.zeros_like(car_ref)

        g_attn, u, bg, cg, g_conv, m_attn, m_conv = (g_ref[:, lo:hi].astype(F32) for lo, hi in GATE_COLS)
        dmerged = _dot_nt(dy_ref[...], wo_ref[...])
        sa, sc = _sig(m_attn), _sig(m_conv)
        da_out = (dmerged * sa).astype(BF16)
        ds_out = (dmerged * sc).astype(BF16)
        daout_ref[...] = da_out
        dsout_ref[...] = ds_out
        dg_ref[:, 4608:5632] = (dmerged * aout_ref[...].astype(F32) * (sa * (1.0 - sa))).astype(BF16)
        dg_ref[:, 5632:6656] = (dmerged * sout_ref[...].astype(F32) * (sc * (1.0 - sc))).astype(BF16)
        da_in = _dot_nt(da_out, wao_ref[...])
        ds_in = _dot_nt(ds_out, wco_ref[...])
        sga = _sig(g_attn)
        o = o_ref[...]
        do = da_in * (g_attn * sga)
        do_ref[...] = do
        dg_ref[:, 0:512] = (da_in * o * (sga * (1.0 + g_attn * (1.0 - sga)))).astype(BF16)
        prod = do * o
        lane = lax.broadcasted_iota(jnp.int32, (tm, HD), 1)
        for j in range(4):
            cs = slice(j * HD, (j + 1) * HD)
            delta = jnp.sum(prod[:, cs], axis=1, keepdims=True)
            dl_ref[:, cs] = jnp.where(lane < 64, lj_ref[:, cs], delta)
        sgc = _sig(g_conv)
        silu_c = g_conv * sgc
        yc = yc_ref[...].astype(F32)
        dg_ref[:, 1536:2560] = (ds_in * yc * silu_c).astype(BF16)
        dg_ref[:, 3584:4608] = (ds_in * bg * yc * (sgc * (1.0 + g_conv * (1.0 - sgc)))).astype(BF16)
        dyc = ds_in * bg * silu_c
        rows = lax.broadcasted_iota(jnp.int32, (tm, D), 0)
        c0, c1 = car_ref[0:1, :], car_ref[1:2, :]
        n1 = jnp.where(rows == tm - 1, c0, pltpu.roll(dyc, tm - 1, 0))
        n2 = jnp.where(rows == tm - 2, c0, jnp.where(rows == tm - 1, c1, pltpu.roll(dyc, tm - 2, 0)))
        car_ref[...] = dyc[0:8, :]
        dz = cw_ref[2:3, :] * dyc + cw_ref[1:2, :] * n1 + cw_ref[0:1, :] * n2
        z = cg * u
        dg_ref[:, 512:1536] = (dz * cg).astype(BF16)
        dg_ref[:, 2560:3584] = (dz * u).astype(BF16)
        vec_ref[3:4, :] += jnp.sum(n2 * z, axis=0, keepdims=True)
        vec_ref[4:5, :] += jnp.sum(n1 * z, axis=0, keepdims=True)
        vec_ref[5:6, :] += jnp.sum(dyc * z, axis=0, keepdims=True)
        vec_ref[6:7, :] += jnp.sum(dyc, axis=0, keepdims=True)

    tok = lambda w: pl.BlockSpec((tm, w), lambda b, i: (b * spt + (spt - 1 - i), 0))
    const = lambda s: pl.BlockSpec(s, lambda b, i: (0,) * len(s))
    return pl.pallas_call(
        body, name="mix_backward", grid=(bsz, spt),
        out_shape=[HBM_OUT((t, NGATE), BF16), HBM_OUT((t, AW), F32), HBM_OUT((t, AW), F32), HBM_OUT((t, D), BF16), HBM_OUT((t, D), BF16),
                   SDS((8, D), F32)],
        in_specs=[tok(NGATE), tok(D), tok(D), tok(D), tok(D), tok(AW), tok(AW), const((AW, D)), const((D, D)), const((D, D)), const((3, D)),
                  const((8, D))],
        out_specs=[tok(NGATE), tok(AW), tok(AW), tok(D), tok(D), const((8, D))],
        scratch_shapes=[pltpu.VMEM((8, D), F32)],
        compiler_params=_cp(("arbitrary", "arbitrary"), VMEM_CAP),
    )(gates, dy, a_out, s_out, y_conv, o, lj, w_ao, w_co, w_o, conv_w, vec_f)


def _halves_out(a):
    kind, nr, nc = W_CUTS[a]
    shape = (nr // 2, W_FULL[a][1]) if kind == "col" else (NCHIP, nr // 2, nc)
    return [SDS(shape, F32), SDS(shape, BF16)]


def _write_halves(a, acc_ref, c, mine_ref, theirs_ref):
    kind, nr, nc = W_CUTS[a]
    hr = nr // 2
    if kind == "col":
        mine_ref[...] = acc_ref[pl.ds(pl.multiple_of(c * hr, hr), hr), :]
        theirs_ref[...] = acc_ref[pl.ds(pl.multiple_of((1 - c) * hr, hr), hr), :].astype(BF16)
    else:
        for k in range(NCHIP):
            mine_ref[k] = acc_ref[pl.ds(pl.multiple_of(k * nr + c * hr, hr), hr), :]
            theirs_ref[k] = acc_ref[pl.ds(pl.multiple_of(k * nr + (1 - c) * hr, hr), hr), :].astype(BF16)


def _out_weight_grads(a_in, da_out, s_in, ds_out, merged, dy, core, tk=512):
    t = dy.shape[0]
    nt = t // tk

    def body(c_ref, ain_ref, da_ref, sin_ref, ds_ref, m_ref, dy_ref, *rest):
        outs, (gao, gco, go) = rest[:6], rest[6:]

        @pl.when(pl.program_id(0) == 0)
        def _():
            gao[...] = jnp.zeros_like(gao)
            gco[...] = jnp.zeros_like(gco)
            go[...] = jnp.zeros_like(go)

        gao[...] += _dot_tn(ain_ref[...], da_ref[...])
        gco[...] += _dot_tn(sin_ref[...], ds_ref[...])
        go[...] += _dot_tn(m_ref[...], dy_ref[...])

        @pl.when(pl.program_id(0) == nt - 1)
        def _():
            for a, acc in ((1, gao), (2, gco), (3, go)):
                _write_halves(a, acc, c_ref[0], outs[2 * a - 2], outs[2 * a - 1])

    tok = lambda w: pl.BlockSpec((tk, w), lambda i, cr: (i, 0))
    out_shape = _halves_out(1) + _halves_out(2) + _halves_out(3)
    outs = pl.pallas_call(
        body, name="out_weight_grads", out_shape=out_shape,
        grid_spec=pltpu.PrefetchScalarGridSpec(
            num_scalar_prefetch=1, grid=(nt,), in_specs=[tok(AW), tok(D), tok(D), tok(D), tok(D), tok(D)],
            out_specs=[pl.BlockSpec(o.shape, lambda i, cr, nd=len(o.shape): (0,) * nd) for o in out_shape],
            scratch_shapes=[pltpu.VMEM((AW, D), F32), pltpu.VMEM((D, D), F32), pltpu.VMEM((D, D), F32)]),
        compiler_params=_cp(("arbitrary",), VMEM_CAP),
    )(core, a_in, da_out, s_in, ds_out, merged, dy)
    return [(outs[0], outs[1]), (outs[2], outs[3]), (outs[4], outs[5])]


def _input_grad(dq, dk, dv, dgates, w, x2, dxr, sc1p, seq, sums, tm=512):
    t = x2.shape[0]
    nt = t // tm
    spt = seq // tm
    bsz = t // seq
    n = len(sums)
    gblk = NGATE // 4
    nsteps = 3 + 4

    def body(dq_ref, dk_ref, dv_ref, dg_ref, wq_ref, wg_ref, x_ref, dxr_ref, sc_ref, *rest):
        src, (dx_ref, dsh_ref, dsc_ref), land = rest[:n], rest[n:n + 3], rest[n + 3:2 * n + 3]
        acc_ref, send_sems, recv_sems = rest[2 * n + 3:]
        j, i = pl.program_id(0), pl.program_id(1)
        px, py, pc = _place()
        chips = [(1 - px, py), (px, 1 - py), (1 - px, 1 - py)]
        copies = [pltpu.make_async_remote_copy(src_ref=src[a].at[2 * cx + cy], dst_ref=land[a].at[r], send_sem=send_sems.at[3 * a + r],
                                               recv_sem=recv_sems.at[3 * a + r], device_id=(cx, cy, pc), device_id_type=MESH)
                  for a in range(n) for r, (cx, cy) in enumerate(chips)]
        rows = pl.ds(pl.multiple_of(i * tm, tm), tm)

        @pl.when((i == 0) & (j == 0))
        def _():
            for cp in copies:
                cp.start()

        for k, ref in enumerate((dq_ref, dk_ref, dv_ref)):
            @pl.when(j == k)
            def _(k=k, ref=ref):
                part = _dot_nt(ref[...], wq_ref[...])
                if k == 0:
                    acc_ref[rows, :] = part
                else:
                    acc_ref[rows, :] += part

        @pl.when((j >= 3) & (j < nsteps - 1))
        def _():
            acc_ref[rows, :] += _dot_nt(dg_ref[...], wg_ref[...])

        @pl.when(j == nsteps - 1)
        def _():
            dh = acc_ref[rows, :] + _dot_nt(dg_ref[...], wg_ref[...])
            dx_ref[...] = dh * sc_ref[0] + dxr_ref[...]

            @pl.when(i % spt == 0)
            def _():
                dsh_ref[...] = jnp.zeros_like(dsh_ref)
                dsc_ref[...] = jnp.zeros_like(dsc_ref)

            dsh_ref[0] += jnp.sum(dh, axis=0, keepdims=True)
            dsc_ref[0] += jnp.sum(dh * x_ref[...], axis=0, keepdims=True)

        @pl.when((i == nt - 1) & (j == nsteps - 1))
        def _():
            for cp in copies:
                cp.wait()

    def held(k):
        return lambda j, i: (jnp.where(j == k, i, jnp.where(j < k, 0, nt - 1)), 0)

    last = lambda j, i: (jnp.where(j == nsteps - 1, i, 0), 0)
    outs = pl.pallas_call(
        body, name="input_grad", grid=(nsteps, nt),
        out_shape=[SDS((t, D), F32), SDS((bsz, 1, D), F32), SDS((bsz, 1, D), F32)] + [SDS((3,) + s.shape[1:], BF16) for s in sums],
        in_specs=[pl.BlockSpec((tm, QW), held(0)), pl.BlockSpec((tm, QW), held(1)), pl.BlockSpec((tm, QW), held(2)),
                  pl.BlockSpec((tm, gblk), lambda j, i: (jnp.where(j >= 3, i, 0), jnp.clip(j - 3, 0, 3))),
                  pl.BlockSpec((D, QW), lambda j, i: (0, jnp.minimum(j, 2))),
                  pl.BlockSpec((pl.Element(D), pl.Element(gblk)), lambda j, i: (0, pl.multiple_of(3 * QW + gblk * jnp.clip(j - 3, 0, 3), 128))),
                  pl.BlockSpec((tm, D), last), pl.BlockSpec((tm, D), last),
                  pl.BlockSpec((1, 1, D), lambda j, i: (jnp.where(j == nsteps - 1, i // spt, 0), 0, 0))] + [ANY] * n,
        out_specs=[pl.BlockSpec((tm, D), last),
                   pl.BlockSpec((1, 1, D), lambda j, i: (jnp.where(j == nsteps - 1, i // spt, 0), 0, 0)),
                   pl.BlockSpec((1, 1, D), lambda j, i: (jnp.where(j == nsteps - 1, i // spt, 0), 0, 0))] + [ANY] * n,
        scratch_shapes=[pltpu.VMEM((t, D), F32), pltpu.SemaphoreType.DMA((3 * NCHIP,)), pltpu.SemaphoreType.DMA((3 * NCHIP,))],
        compiler_params=_cp(("arbitrary", "arbitrary"), VMEM_CAP, side=True),
    )(dq, dk, dv, dgates, w, w, x2, dxr, sc1p, *sums)
    return outs[0], outs[1], outs[2], outs[3:]


def _in_weight_grad(ht, src, col0, prev, core, name):
    t = ht.shape[1]
    ncols = src.shape[1] // TN
    hr = D // 2

    def body(c_ref, ht_ref, s_ref, *rest):
        mine_ref, theirs_ref, acc_ref = rest[-3:]
        acc_ref[...] = _dot(ht_ref[...], s_ref[...])
        _write_halves(0, acc_ref, c_ref[0], mine_ref, theirs_ref)

    ins = [core, ht, src]
    in_specs = [pl.BlockSpec((D, t), lambda j, cr: (0, 0)), pl.BlockSpec((t, TN), lambda j, cr: (0, j))]
    aliases = {}
    if prev is not None:
        ins += list(prev)
        in_specs += [ANY] * 2
        aliases = {3: 0, 4: 1}
    out_spec = pl.BlockSpec((hr, TN), lambda j, cr: (0, col0 + j))
    return pl.pallas_call(
        body, name=name, out_shape=[SDS((hr, NCOL), F32), SDS((hr, NCOL), BF16)],
        grid_spec=pltpu.PrefetchScalarGridSpec(num_scalar_prefetch=1, grid=(ncols,), in_specs=in_specs, out_specs=[out_spec, out_spec],
                                               scratch_shapes=[pltpu.VMEM((D, TN), F32)]),
        input_output_aliases=aliases, compiler_params=_cp(("arbitrary",), VMEM_CAP),
    )(*ins)


def _sum_partials(gathered):
    def body(g_ref, o_ref):
        acc = g_ref[0]
        for k in range(1, 8):
            acc = acc + g_ref[k]
        o_ref[...] = acc

    return pl.pallas_call(body, name="sum_partials", out_shape=SDS(gathered.shape[1:], F32), in_specs=[VMEM_SPEC], out_specs=VMEM_SPEC)(gathered)


def _adamw(w, g, m, v, name, tr=256):
    r, cdim = w.shape
    tr = tr if cdim <= D else tr // 2
    tr = tr if (r % tr == 0 and r > tr) else r

    def body(w_ref, g_ref, m_ref, v_ref, go_ref, d_ref, nm_ref, nv_ref):
        gv = g_ref[...]
        go_ref[...] = gv
        nm = B1 * m_ref[...] + (1.0 - B1) * gv
        nv = B2 * v_ref[...] + (1.0 - B2) * (gv * gv)
        m_hat = nm / (1.0 - B1 ** STEP)
        v_hat = nv / (1.0 - B2 ** STEP)
        d_ref[...] = -LR * (m_hat / (jnp.sqrt(v_hat) + EPS) + WD * w_ref[...])
        nm_ref[...] = nm
        nv_ref[...] = nv

    spec = pl.BlockSpec((tr, cdim), lambda i: (i, 0))
    return pl.pallas_call(
        body, name=name, grid=(r // tr,), out_shape=[SDS((r, cdim), F32)] * 4, in_specs=[spec] * 4, out_specs=[spec] * 4,
        compiler_params=_cp(("parallel",), VMEM_CAP // 2),
    )(w, g, m, v)


def _t5_bucket(dist):
    n = jnp.maximum(dist, 1).astype(F32)
    large = MAX_EXACT + (jnp.log(n / MAX_EXACT) / math.log(MAX_DISTANCE / MAX_EXACT) * (N_BUCKETS - MAX_EXACT)).astype(jnp.int32)
    large = jnp.minimum(large, N_BUCKETS - 1)
    return jnp.where(dist < MAX_EXACT, dist, large)


def _band_buckets():
    a = jnp.arange(BLK)[:, None]
    b = jnp.arange(2 * BLK)[None, :]
    steps = jnp.maximum(a + BLK - b, 0)
    return jnp.stack([_t5_bucket(steps * d) for d in DILATIONS]).astype(jnp.int32)


def _pad_rows(a, rows=8):
    return jnp.pad(a, ((0, rows - a.shape[0]), (0, 0)))


def kernel(x, c, w_ada, b_ada, w_in, conv_w, conv_b, rel_bias, w_attn_out, w_conv_out, w_o, ln_g, ln_b, loss_target, m_w_ada, m_b_ada, m_w_in, m_conv_w, m_conv_b, m_rel_bias, m_w_attn_out, m_w_conv_out, m_w_o, m_ln_g, m_ln_b, v_w_ada, v_b_ada, v_w_in, v_conv_w, v_conv_b, v_rel_bias, v_w_attn_out, v_w_conv_out, v_w_o, v_ln_g, v_ln_b):
    bsz, seq, _ = x.shape
    t = bsz * seq
    mx, my, mc = _place()
    chip = 2 * mx + my
    dev = 4 * mx + 2 * my + mc
    x2 = x.reshape(t, D)
    tgt = loss_target.reshape(t, D)

    mine = _to_bf16_windows([w[0] for w in (w_in, w_attn_out, w_conv_out, w_o)])

    n_ada = w_ada.shape[2]
    n_cw = conv_w.shape[2]
    c_and_cw = jnp.concatenate([_pad_rows(c), jnp.pad(conv_w[0], ((0, 5), (0, D - n_cw)))], axis=0)
    firsts = _all_gather8(c_and_cw, "gather_c_conv_w")
    c_all = firsts[:, 0:bsz, :].reshape(8 * bsz, D)
    conv_w_f = firsts[0::2, 8:11, 0:n_cw].transpose(1, 0, 2).reshape(3, D)
    b_cols = lax.dynamic_slice(b_ada, (0, chip * n_ada), (1, n_ada))
    mod_part = _ada_forward(c_all, w_ada[0], b_cols)
    mod_parts = _all_gather8(mod_part, "gather_mod")
    mod_all = mod_parts[0::2].transpose(1, 0, 2).reshape(8 * bsz, 3 * D)
    mod = lax.dynamic_slice(mod_all, (dev * bsz, 0), (bsz, 3 * D))
    shift = mod[:, 0:D].reshape(bsz, 1, D)
    sc1p = 1.0 + mod[:, D:2 * D].reshape(bsz, 1, D)
    gate = mod[:, 2 * D:].reshape(bsz, 1, D)

    h, ht = _modulate(x2, sc1p, shift, seq)
    tab = lax.dynamic_index_in_dim(jnp.asarray(_tile_tables()), chip, 0, keepdims=False)
    qkv, gates, (w_in_f, w_ao_f, w_co_f, w_o_f) = _project_gather(h, mine, tab)
    buckets = _band_buckets()
    bias = _bias_tables(rel_bias, buckets)
    og, lg = [], []
    for g in range(3):
        o_g, l_g = _attn_forward(g, qkv, bias[g], bsz, seq)
        og.append(o_g)
        lg.append(l_g)
    (a_in, s_in, merged, dy, a_out, s_out, y_conv, o, lj, dxr, vec_f, dgate) = _mix_forward(
        gates, og, lg, x2, tgt, gate, w_ao_f, w_co_f, w_o_f, conv_w_f, conv_b, ln_g, ln_b, bsz, seq)

    dgates, do, dl, da_out, ds_out, vec = _mix_backward(gates, dy, a_out, s_out, y_conv, o, lj, w_ao_f, w_co_f, w_o_f, conv_w_f, vec_f, bsz, seq)
    core = jnp.reshape(mc, (1,)).astype(jnp.int32)
    small_grads = _out_weight_grads(a_in, da_out, s_in, ds_out, merged, dy, core)
    dqkv, dbs = None, []
    for g in range(3):
        dqkv, db = _attn_backward(g, qkv, do, dl, bias[g], dqkv, bsz, seq)
        dbs.append(db)
    dq, dk, dv = dqkv
    drb = _bias_grad(jnp.stack(dbs), buckets)
    drb = drb[:, :, 0:4].transpose(1, 0, 2).reshape(N_BUCKETS, 12)
    g_in = None
    for n, src in enumerate((dq, dk, dv, dgates)):
        g_in = _in_weight_grad(ht, src, n * NQT, g_in, core, f"in_weight_grad_{n}")

    halves = [tuple(g_in)] + small_grads
    got = _swap_halves([theirs for _, theirs in halves])
    sums = _chip_sums([own for own, _ in halves], got)
    grad_x, dshift, dscale, landed = _input_grad(dq, dk, dv, dgates, w_in_f, x2, dxr, sc1p, seq, [s[1] for s in sums])
    halves = _reduce_mine([own for own, _ in sums], landed)
    gw_in, gw_ao, gw_co, gw_o = _join_halves(halves)

    dmod = jnp.concatenate([dshift, dscale, dgate], axis=2).reshape(bsz * 3, D)
    drb_row = jnp.pad(drb.reshape(1, N_BUCKETS * 12), ((0, 0), (0, D - N_BUCKETS * 12)))
    vec = lax.dynamic_update_slice(vec, drb_row, (7, 0))
    packed = jnp.concatenate([vec, _pad_rows(dmod)], axis=0)
    gathered = _all_gather8(packed, "gather_small")
    small = _sum_partials(gathered)
    g_ln_g, g_ln_b, loss_lanes = small[0:1], small[1:2], small[2:3]
    g_conv_w_full, g_conv_b = small[3:6], small[6:7]
    g_rel_bias = small[7, 0:N_BUCKETS * 12].reshape(N_BUCKETS, 12)
    loss = 0.5 / D * jnp.sum(loss_lanes)
    dmod_all = gathered[:, 8:8 + 3 * bsz, :].reshape(8 * bsz, 3 * D)
    dmod_cols = lax.dynamic_slice(dmod_all, (0, chip * n_ada), (8 * bsz, n_ada))
    gw_ada, gb_ada = _ada_backward(c_all, dmod_cols, dmod_all)
    g_conv_w = lax.dynamic_slice(g_conv_w_full, (0, chip * n_cw), (3, n_cw))

    names = ["w_ada", "b_ada", "w_in", "conv_w", "conv_b", "rel_bias", "w_attn_out", "w_conv_out", "w_o", "ln_g", "ln_b"]
    two_d = lambda a: a.reshape(a.shape[-2:]) if a.ndim == 3 else a
    weights = dict(zip(names, map(two_d, (w_ada, b_ada, w_in, conv_w, conv_b, rel_bias, w_attn_out, w_conv_out, w_o, ln_g, ln_b))))
    ms = dict(zip(names, map(two_d, (m_w_ada, m_b_ada, m_w_in, m_conv_w, m_conv_b, m_rel_bias, m_w_attn_out, m_w_conv_out, m_w_o, m_ln_g, m_ln_b))))
    vs = dict(zip(names, map(two_d, (v_w_ada, v_b_ada, v_w_in, v_conv_w, v_conv_b, v_rel_bias, v_w_attn_out, v_w_conv_out, v_w_o, v_ln_g, v_ln_b))))
    grads = dict(zip(names, (gw_ada, gb_ada, gw_in, g_conv_w, g_conv_b, g_rel_bias, gw_ao, gw_co, gw_o, g_ln_g, g_ln_b)))
    shapes = dict(zip(names, (w_ada, b_ada, w_in, conv_w, conv_b, rel_bias, w_attn_out, w_conv_out, w_o, ln_g, ln_b)))
    grad_out, deltas, new_m, new_v = {}, {}, {}, {}
    for n in names:
        grad_out[n], deltas[n], new_m[n], new_v[n] = _adamw(weights[n], grads[n], ms[n], vs[n], f"adamw_{n}")
    shaped = lambda d: [d[n].reshape(shapes[n].shape) for n in names]
    return (loss, grad_x.reshape(bsz, seq, D), *shaped(grad_out), *shaped(deltas), *shaped(new_m), *shaped(new_v))
```

```python
import math

import numpy as np
import jax
import jax.numpy as jnp
from jax import lax
from jax.experimental import pallas as pl
from jax.experimental.pallas import tpu as pltpu

F32 = jnp.float32
BF16 = jnp.bfloat16
SDS = jax.ShapeDtypeStruct
MESH = pl.DeviceIdType.MESH
HBM_OUT = pltpu.HBM
ANY = pl.BlockSpec(memory_space=pl.ANY)
VMEM_SPEC = pl.BlockSpec(memory_space=pltpu.VMEM)

D = 1024
HD = 128
BLK = 128
QW = 1536
AW = 512
NGATE = 6656
GATE_COLS = ((0, 512), (512, 1536), (1536, 2560), (2560, 3584), (3584, 4608), (4608, 5632), (5632, 6656))
NCOL = 3 * QW + NGATE
TN = 512
NQT = QW // TN
NPT = NCOL // TN
DILATIONS = (1, 4, 16)
N_BUCKETS, MAX_EXACT, MAX_DISTANCE = 32, 16, 2048
ALPHA = 2.0 ** 0.25
LN_EPS = 1e-5
NEG = -1e30
SCALE = HD ** -0.5
LR, B1, B2, EPS, WD, STEP = 0.001, 0.9, 0.999, 1e-08, 0.01, 10
NCHIP = 4
VMEM_CAP = 60 * 2 ** 20


def _cp(sem=None, vmem=None, side=False):
    return pltpu.CompilerParams(dimension_semantics=sem, vmem_limit_bytes=vmem, has_side_effects=side)


def _dot(a, b):
    return jnp.dot(a, b, preferred_element_type=F32)


def _dot_nt(a, b):
    return lax.dot_general(a, b, (((1,), (1,)), ((), ())), preferred_element_type=F32)


def _dot_tn(a, b):
    return lax.dot_general(a, b, (((0,), (0,)), ((), ())), preferred_element_type=F32)


def _sig(x):
    return 1.0 / (1.0 + jnp.exp(-x))


def _in_hbm(a):
    return pltpu.with_memory_space_constraint(a, pltpu.HBM)


def _place():
    x, y, c = lax.axis_index("x"), lax.axis_index("y"), lax.axis_index("c")
    return x, y, c


def _all_gather8(v, name):
    r, cdim = v.shape

    def body(v_ref, out_ref, send_sems, recv_sems, local_sem):
        x, y, c = _place()
        me = 4 * x + 2 * y + c
        peers = [(x, y, 1 - c), (1 - x, y, c), (x, 1 - y, c), (1 - x, 1 - y, c),
                 (1 - x, y, 1 - c), (x, 1 - y, 1 - c), (1 - x, 1 - y, 1 - c)]
        mine = pltpu.make_async_copy(v_ref, out_ref.at[me], local_sem)
        mine.start()

        def copy(k, block, to):
            return pltpu.make_async_remote_copy(src_ref=v_ref, dst_ref=out_ref.at[block], send_sem=send_sems.at[k],
                                                recv_sem=recv_sems.at[k], device_id=to, device_id_type=MESH)

        sends = [copy(k, me, p) for k, p in enumerate(peers)]
        for cp in sends:
            cp.start()
        for k, (px, py, pc) in enumerate(peers):
            copy(k, 4 * px + 2 * py + pc, (px, py, pc)).wait_recv()
        for cp in sends:
            cp.wait_send()
        mine.wait()

    return pl.pallas_call(
        body, name=name, out_shape=SDS((8, r, cdim), v.dtype), in_specs=[VMEM_SPEC], out_specs=VMEM_SPEC,
        scratch_shapes=[pltpu.SemaphoreType.DMA((7,)), pltpu.SemaphoreType.DMA((7,)), pltpu.SemaphoreType.DMA(())],
        compiler_params=_cp(side=True),
    )(v)


W_CUTS = (("col", D, NCOL // NCHIP), ("col", AW, D // NCHIP), ("row", D // NCHIP, D), ("row", D // NCHIP, D))
W_FULL = ((D, NCOL), (AW, D), (D, D), (D, D))


def _shard_window(ref, cut, k, half):
    kind, nr, nc = cut
    hr = nr // 2
    if kind == "col":
        rows = pl.ds(0, nr) if half is None else pl.ds(pl.multiple_of(half * hr, 16), hr)
        return ref.at[rows, pl.ds(pl.multiple_of(k * nc, 128), nc)]
    if half is None:
        return ref.at[pl.ds(pl.multiple_of(k * nr, 16), nr), :]
    return ref.at[pl.ds(pl.multiple_of(k * nr + half * hr, 16), hr), :]


def _half_rows(ref, cut, half):
    hr = cut[1] // 2
    return ref.at[pl.ds(pl.multiple_of(half * hr, 16), hr), :]


def _to_bf16_windows(ws):
    x, y, _ = _place()
    chip = jnp.reshape(2 * x + y, (1,)).astype(jnp.int32)
    tr = 256
    n = len(ws)

    def body(c_ref, *refs):
        src, dst = refs[:n], refs[n:]
        dst[0][...] = src[0][...].astype(BF16)

        @pl.when(pl.program_id(0) == 0)
        def _():
            for a in range(1, n):
                dst[a][...] = src[a][...].astype(BF16)

    in_specs = [pl.BlockSpec((tr, W_CUTS[0][2]), lambda i, cr: (i, 0))]
    out_specs = [pl.BlockSpec((tr, W_CUTS[0][2]), lambda i, cr: (i, cr[0]))]
    for a in range(1, n):
        kind, nr, nc = W_CUTS[a]
        in_specs.append(pl.BlockSpec((nr, nc), lambda i, cr: (0, 0)))
        out_specs.append(pl.BlockSpec((nr, nc), (lambda i, cr: (0, cr[0])) if kind == "col" else (lambda i, cr: (cr[0], 0))))
    return pl.pallas_call(
        body, name="to_bf16", out_shape=[SDS(W_FULL[a], BF16) for a in range(n)],
        grid_spec=pltpu.PrefetchScalarGridSpec(num_scalar_prefetch=1, grid=(D // tr,), in_specs=in_specs, out_specs=out_specs),
        compiler_params=_cp(("arbitrary",)),
    )(chip, *ws)


def _swap_halves(theirs):
    n = len(theirs)

    def body(*refs):
        src, land = refs[:n], refs[n:2 * n]
        send_sems, recv_sems = refs[2 * n:]
        x, y, c = _place()
        copies = [pltpu.make_async_remote_copy(src_ref=src[a], dst_ref=land[a], send_sem=send_sems.at[a], recv_sem=recv_sems.at[a],
                                               device_id=(x, y, 1 - c), device_id_type=MESH) for a in range(n)]
        for cp in copies:
            cp.start()
        for cp in copies:
            cp.wait()

    return pl.pallas_call(
        body, name="swap_grad_halves", out_shape=[SDS(v.shape, v.dtype) for v in theirs], in_specs=[ANY] * n, out_specs=[ANY] * n,
        scratch_shapes=[pltpu.SemaphoreType.DMA((n,)), pltpu.SemaphoreType.DMA((n,))],
        compiler_params=_cp(side=True),
    )(*theirs)


def _chip_sums(mines, gots):
    n = len(mines)
    x, y, _ = _place()
    me = jnp.reshape(2 * x + y, (1,)).astype(jnp.int32)

    def body(me_ref, *refs):
        ins, outs = refs[:2 * n], refs[2 * n:]
        for a in range(n):
            hr, nc = W_CUTS[a][1] // 2, W_CUTS[a][2]
            s = (ins[2 * a][...] + ins[2 * a + 1][...].astype(F32)).reshape(hr, nc)
            outs[2 * a + 1][0] = s.astype(BF16)

            @pl.when(pl.program_id(0) == me_ref[0])
            def _(a=a, s=s):
                outs[2 * a][...] = s

    in_specs, out_specs, out_shape = [], [], []
    for a in range(n):
        kind, nr, nc = W_CUTS[a]
        hr = nr // 2
        spec = pl.BlockSpec((hr, nc), lambda k, mr: (0, k)) if kind == "col" else pl.BlockSpec((1, hr, nc), lambda k, mr: (k, 0, 0))
        in_specs += [spec, spec]
        out_specs += [pl.BlockSpec((hr, nc), lambda k, mr: (0, 0)), pl.BlockSpec((1, hr, nc), lambda k, mr: (k, 0, 0))]
        out_shape += [SDS((hr, nc), F32), SDS((NCHIP, hr, nc), BF16)]
    outs = pl.pallas_call(
        body, name="chip_sums", out_shape=out_shape,
        grid_spec=pltpu.PrefetchScalarGridSpec(num_scalar_prefetch=1, grid=(NCHIP,), in_specs=in_specs, out_specs=out_specs),
        compiler_params=_cp(("arbitrary",), VMEM_CAP),
    )(me, *[v for pair in zip(mines, gots) for v in pair])
    return [(outs[2 * a], outs[2 * a + 1]) for a in range(n)]


def _reduce_mine(mines, gots):
    n = len(mines)
    _, _, c = _place()
    core = jnp.reshape(c, (1,)).astype(jnp.int32)
    tr = 256
    nsteps = W_CUTS[0][1] // 2 // tr

    def body(c_ref, *refs):
        ins, outs = refs[:2 * n], refs[2 * n:]

        def add(a):
            m_ref, g_ref = ins[2 * a], ins[2 * a + 1]
            outs[a][...] = ((m_ref[...] + g_ref[0].astype(F32)) + g_ref[1].astype(F32)) + g_ref[2].astype(F32)

        add(0)

        @pl.when(pl.program_id(0) == 0)
        def _():
            for a in range(1, n):
                add(a)

    nc0 = W_CUTS[0][2]
    in_specs = [pl.BlockSpec((tr, nc0), lambda i, cr: (i, 0)), pl.BlockSpec((3, tr, nc0), lambda i, cr: (0, i, 0))]
    out_specs = [pl.BlockSpec((tr, nc0), lambda i, cr: (cr[0] * nsteps + i, 0))]
    for a in range(1, n):
        hr, nc = W_CUTS[a][1] // 2, W_CUTS[a][2]
        in_specs += [pl.BlockSpec((hr, nc), lambda i, cr: (0, 0)), pl.BlockSpec((3, hr, nc), lambda i, cr: (0, 0, 0))]
        out_specs.append(pl.BlockSpec((hr, nc), lambda i, cr: (cr[0], 0)))
    return pl.pallas_call(
        body, name="reduce_mine", out_shape=[SDS((W_CUTS[a][1], W_CUTS[a][2]), F32) for a in range(n)],
        grid_spec=pltpu.PrefetchScalarGridSpec(num_scalar_prefetch=1, grid=(nsteps,), in_specs=in_specs, out_specs=out_specs),
        compiler_params=_cp(("arbitrary",), VMEM_CAP),
    )(core, *[v for pair in zip(mines, gots) for v in pair])


def _join_halves(fulls):
    n = len(fulls)

    def body(*refs):
        full = refs[n:2 * n]
        send_sems, recv_sems = refs[2 * n:]
        x, y, c = _place()
        sibling = (x, y, 1 - c)

        def swap(a, half):
            rows = _half_rows(full[a], W_CUTS[a], half)
            return pltpu.make_async_remote_copy(src_ref=rows, dst_ref=rows, send_sem=send_sems.at[a], recv_sem=recv_sems.at[a],
                                                device_id=sibling, device_id_type=MESH)

        sends = [swap(a, c) for a in range(n)]
        for cp in sends:
            cp.start()
        for a, cp in enumerate(sends):
            cp.wait_send()
            swap(a, 1 - c).wait_recv()

    return pl.pallas_call(
        body, name="join_grad_halves", out_shape=[SDS((W_CUTS[a][1], W_CUTS[a][2]), F32) for a in range(n)],
        in_specs=[ANY] * n, out_specs=[ANY] * n,
        scratch_shapes=[pltpu.SemaphoreType.DMA((n,)), pltpu.SemaphoreType.DMA((n,))],
        input_output_aliases={a: a for a in range(n)}, compiler_params=_cp(side=True),
    )(*fulls)


def _ada_forward(c_all, w_ada, b_cols):
    nb, nc = c_all.shape[0], w_ada.shape[1]

    def body(c_ref, w_ref, b_ref, o_ref):
        cv = c_ref[...]
        sc = (cv * _sig(cv)).astype(BF16)
        o_ref[...] = _dot(sc, w_ref[...].astype(BF16)) + b_ref[...]

    return pl.pallas_call(body, name="ada_forward", out_shape=SDS((nb, nc), F32), compiler_params=_cp(vmem=VMEM_CAP // 2))(c_all, w_ada, b_cols)


def _ada_backward(c_all, dmod_cols, dmod_all):
    nb, nc = dmod_cols.shape

    def body(c_ref, d_ref, a_ref, gw_ref, gb_ref):
        cv = c_ref[...]
        sc = (cv * _sig(cv)).astype(BF16)
        gw_ref[...] = _dot_tn(sc, d_ref[...].astype(BF16))
        gb_ref[...] = jnp.sum(a_ref[...], axis=0, keepdims=True)

    return pl.pallas_call(body, name="ada_backward", out_shape=[SDS((D, nc), F32), SDS((1, dmod_all.shape[1]), F32)],
                          compiler_params=_cp(vmem=VMEM_CAP // 2))(c_all, dmod_cols, dmod_all)


def _modulate(x2, sc1p, shift, seq, tm=256):
    t = x2.shape[0]
    spt = seq // tm

    def body(x_ref, sc_ref, sh_ref, h_ref, ht_ref):
        h = x_ref[...] * sc_ref[0] + sh_ref[0]
        h_ref[...] = h.astype(BF16)
        ht_ref[...] = h.T.astype(BF16)

    per_seq = pl.BlockSpec((1, 1, D), lambda i: (i // spt, 0, 0))
    return pl.pallas_call(
        body, name="modulate", out_shape=[HBM_OUT((t, D), BF16), HBM_OUT((D, t), BF16)], grid=(t // tm,),
        in_specs=[pl.BlockSpec((tm, D), lambda i: (i, 0)), per_seq, per_seq],
        out_specs=[pl.BlockSpec((tm, D), lambda i: (i, 0)), pl.BlockSpec((D, tm), lambda i: (0, i))],
        compiler_params=_cp(("parallel",)),
    )(x2, sc1p, shift)


TW = 256
TPS = NCOL // NCHIP // TW
NT = NCOL // TW
NQKV_T = 3 * QW // TW
N_TILE_SEMS = 2 * 3 * TPS


def _tile_tables():
    tabs = np.zeros((NCHIP, 3, NT), np.int32)
    for me in range(NCHIP):
        tiles = [TPS * (me ^ (s // TPS)) + s % TPS for s in range(NT)]
        tabs[me, 0] = tiles
        for row, (lo, hi) in enumerate(((0, NQKV_T), (NQKV_T, NT))):
            mine = [w - lo if lo <= w < hi else None for w in tiles]
            held = next(m for m in mine if m is not None)
            for s, m in enumerate(mine):
                held = held if m is None else m
                tabs[me, 1 + row, s] = held
    return tabs


def _project_gather(h, fulls, tab):
    t = h.shape[0]
    n = len(fulls)

    def body(tab_ref, h_ref, *rest):
        qkv_ref, g_ref = rest[n], rest[n + 1]
        full = rest[n + 2:2 * n + 2]
        w_buf, tile_sems, send_sems, recv_sems = rest[2 * n + 2:]
        s = pl.program_id(0)
        x, y, c = _place()
        me = 2 * x + y
        peers = [(x, 1 - y), (1 - x, y), (1 - x, 1 - y)]
        sibling = (x, y, 1 - c)

        def hop(a, r, stage, chip, half, to):
            window = _shard_window(full[a], W_CUTS[a], chip, half)
            k = N_TILE_SEMS + 6 * (a - 1) + 2 * r + stage
            return pltpu.make_async_remote_copy(src_ref=window, dst_ref=window, send_sem=send_sems.at[k], recv_sem=recv_sems.at[k],
                                                device_id=to, device_id_type=MESH)

        def tile_hop(q, stage, col_step, half, to):
            col = pl.multiple_of(tab_ref[0, col_step] * TW, TW)
            window = full[0].at[pl.ds(pl.multiple_of(half * (D // 2), 16), D // 2), pl.ds(col, TW)]
            k = 2 * (q - TPS) + stage
            return pltpu.make_async_remote_copy(src_ref=window, dst_ref=window, send_sem=send_sems.at[k], recv_sem=recv_sems.at[k],
                                                device_id=to, device_id_type=MESH)

        def send_tile(r, j):
            return tile_hop(TPS * (r + 1) + j, 0, j, c, (*peers[r], c))

        def arrive(a, r):
            px, py = peers[r]
            chip = 2 * px + py
            hop(a, r, 0, chip, c, (px, py, c)).wait_recv()
            hop(a, r, 1, chip, c, sibling).start()
            hop(a, r, 1, chip, 1 - c, sibling).wait_recv()

        def tile(step, slot):
            col = pl.multiple_of(tab_ref[0, step] * TW, TW)
            return pltpu.make_async_copy(full[0].at[:, pl.ds(col, TW)], w_buf.at[slot], tile_sems.at[slot])

        @pl.when(s == 0)
        def _():
            for r in range(2):
                for j in range(TPS):
                    send_tile(r, j).start()
            tile(0, 0).start()

        @pl.when((s + 1 >= TPS) & (s + 1 < NT))
        def _():
            tile_hop(s + 1, 1, s + 1, 1 - c, sibling).wait_recv()

        @pl.when(s + 1 < NT)
        def _():
            tile(s + 1, 1 - (s % 2)).start()

        @pl.when((s + 2 >= TPS) & (s + 2 < NT))
        def _():
            tile_hop(s + 2, 0, s + 2, c, sibling).wait_recv()
            tile_hop(s + 2, 1, s + 2, c, sibling).start()

        @pl.when(s + 2 == 2 * TPS - 1)
        def _():
            for j in range(TPS):
                send_tile(2, j).start()
            for a in range(1, n):
                for r in range(3):
                    hop(a, r, 0, me, c, (*peers[r], c)).start()

        slot = s % 2
        tile(s, slot).wait()
        is_qkv = tab_ref[0, s] < NQKV_T
        for k in range(2):
            @pl.when(slot == k)
            def _(k=k):
                acc = _dot(h_ref[...], w_buf[k])

                @pl.when(is_qkv)
                def _():
                    qkv_ref[...] = acc.astype(BF16)

                @pl.when(jnp.logical_not(is_qkv))
                def _():
                    g_ref[...] = acc.astype(BF16)

        @pl.when(s == NT - 1)
        def _():
            for a in range(1, n):
                for r in range(3):
                    arrive(a, r)
            for r in range(3):
                for j in range(TPS):
                    send_tile(r, j).wait_send()
                    tile_hop(TPS * (r + 1) + j, 1, TPS * (r + 1) + j, c, sibling).wait_send()
                for a in range(1, n):
                    hop(a, r, 0, me, c, (*peers[r], c)).wait_send()
                    px, py = peers[r]
                    hop(a, r, 1, 2 * px + py, c, sibling).wait_send()

    n_sems = N_TILE_SEMS + 6 * (n - 1)
    outs = pl.pallas_call(
        body, name="project_gather", out_shape=[HBM_OUT((t, 3 * QW), BF16), HBM_OUT((t, NGATE), BF16)] + [SDS(s, BF16) for s in W_FULL],
        grid_spec=pltpu.PrefetchScalarGridSpec(
            num_scalar_prefetch=1, grid=(NT,),
            in_specs=[pl.BlockSpec((t, D), lambda s, tab: (0, 0))] + [ANY] * n,
            out_specs=[pl.BlockSpec((t, TW), lambda s, tab: (0, tab[1, s])), pl.BlockSpec((t, TW), lambda s, tab: (0, tab[2, s]))] + [ANY] * n,
            scratch_shapes=[pltpu.VMEM((2, D, TW), BF16), pltpu.SemaphoreType.DMA((2,)),
                            pltpu.SemaphoreType.DMA((n_sems,)), pltpu.SemaphoreType.DMA((n_sems,))]),
        input_output_aliases={2 + a: 2 + a for a in range(n)},
        compiler_params=_cp(("arbitrary",), VMEM_CAP, side=True),
    )(tab, _in_hbm(h), *fulls)
    return outs[0], outs[1], outs[2:]


def _bias_tables(rel_bias, buckets):
    def body(tab_ref, bk_ref, o_ref):
        a = lax.broadcasted_iota(jnp.int32, (BLK, 2 * BLK), 0)
        b = lax.broadcasted_iota(jnp.int32, (BLK, 2 * BLK), 1)
        steps = a + BLK - b
        valid = (steps >= 0) & (steps <= BLK)
        for g in range(3):
            bk = bk_ref[g]
            for j in range(4):
                def pick(kk, acc, bk=bk, col=4 * g + j):
                    return jnp.where(bk == kk, tab_ref[kk, col], acc)

                acc = lax.fori_loop(0, N_BUCKETS, pick, jnp.zeros((BLK, 2 * BLK), F32))
                o_ref[g, j] = jnp.where(valid, acc, NEG)

    return pl.pallas_call(
        body, name="bias_tables", out_shape=SDS((3, 4, BLK, 2 * BLK), F32),
        in_specs=[pl.BlockSpec(memory_space=pltpu.SMEM), VMEM_SPEC], out_specs=VMEM_SPEC,
    )(rel_bias, buckets)


def _bias_grad(ds_sum, buckets):
    def body(ds_ref, bk_ref, o_ref):
        lane = lax.broadcasted_iota(jnp.int32, (1, 128), 1)
        for g in range(3):
            def bucket(kk, carry, g=g):
                row = jnp.zeros((1, 128), F32)
                for j in range(4):
                    v = jnp.where(bk_ref[g] == kk, ds_ref[g, j], 0.0)
                    v = jnp.sum(v.reshape(BLK // 8, 8, 2 * BLK), axis=0)
                    s = jnp.sum(jnp.sum(v, axis=1, keepdims=True), axis=0, keepdims=True)
                    row = jnp.where(lane == j, s, row)
                o_ref[g, pl.ds(kk, 1), :] = row
                return carry

            lax.fori_loop(0, N_BUCKETS, bucket, 0)

    return pl.pallas_call(body, name="bias_grad", out_shape=SDS((3, N_BUCKETS, 128), F32), in_specs=[VMEM_SPEC, VMEM_SPEC],
                          out_specs=VMEM_SPEC)(ds_sum, buckets)


def _sub_rows(d, r, first, size):
    return pl.ds(first * d + r, size) if d == 1 else pl.ds(first * d + r, size, stride=d)


def _head_spec(seq, g, part):
    return pl.BlockSpec((seq, HD), lambda b, hh: (b, part * (QW // HD) + 4 * g + hh))


def _rows(start, count, stride):
    return pl.ds(start, count) if stride == 1 else pl.ds(start, count, stride=stride)


def _gather_rows(dst, dst0, src, src0, stride, count):
    for first in range(0, count, BLK):
        dst[pl.ds(dst0 + first, BLK), :] = src[_rows(src0 + first * stride, BLK, stride), :].astype(dst.dtype)


def _scatter_rows(dst, dst0, stride, src, src0, count):
    for first in range(0, count, BLK):
        dst[_rows(dst0 + first * stride, BLK, stride), :] = src[pl.ds(src0 + first, BLK), :].astype(dst.dtype)


def _by_subsequence(dst, src, d, wide=None, tmp=None):
    seq = src.shape[0]
    ln = seq // d
    if wide is not None:
        wide[...] = src[...].astype(F32)
        src = wide
    if d <= 4:
        for r in range(d):
            _gather_rows(dst, r * ln, src, r, d, ln)
    else:
        quarter = seq // 4
        for r4 in range(4):
            _gather_rows(tmp, r4 * quarter, src, r4, 4, quarter)
        for r4 in range(4):
            for a in range(d // 4):
                _gather_rows(dst, (4 * a + r4) * ln, tmp, r4 * quarter + a, d // 4, ln)


def _to_sequence(dst, src, d, tmp=None):
    seq = dst.shape[0]
    ln = seq // d
    if d <= 4:
        for r in range(d):
            _scatter_rows(dst, r, d, src, r * ln, ln)
    else:
        quarter = seq // 4
        for r4 in range(4):
            for a in range(d // 4):
                _scatter_rows(tmp, r4 * quarter + a, d // 4, src, (4 * a + r4) * ln, ln)
        for r4 in range(4):
            _scatter_rows(dst, r4, 4, tmp, r4 * quarter, quarter)


def _attn_forward(g, qkv, bias, bsz, seq):
    d = DILATIONS[g]
    ln = seq // d
    units = [(r, n) for r in range(d) for n in range(ln // BLK)]

    def band(n):
        return slice(BLK, 2 * BLK) if n == 0 else slice(0, 2 * BLK)

    def body(q_ref, k_ref, v_ref, b_ref, o_ref, l_ref, *scratch):
        hs = pl.program_id(1)
        s_scr, p_scr = scratch[:2]
        if d == 1:
            qd, kd, vd = q_ref, k_ref, v_ref
        else:
            wide, tmp, qd, kd, vd = scratch[2:7]
            for dst, src in ((qd, q_ref), (kd, k_ref), (vd, v_ref)):
                _by_subsequence(dst, src, d, wide, tmp)
        blk = lambda r, n: pl.ds(r * ln + n * BLK, BLK)
        direct = d <= 4
        out_rows = (lambda r, n: _sub_rows(d, r, n * BLK, BLK)) if direct else blk
        o_dst, l_dst = (o_ref, l_ref) if direct else scratch[7:9]
        for u, (r, n) in enumerate(units):
            s_scr[u, :, BLK:] = _dot_nt(qd[blk(r, n), :], kd[blk(r, n), :])
            if n > 0:
                s_scr[u, :, :BLK] = _dot_nt(qd[blk(r, n), :], kd[blk(r, n - 1), :])
        for u, (r, n) in enumerate(units):
            s = s_scr[u, :, band(n)] * SCALE + b_ref[hs, :, band(n)]
            m = jnp.max(s, axis=1, keepdims=True)
            e = jnp.exp(s - m)
            den = jnp.sum(e, axis=1, keepdims=True)
            p_scr[u, :, band(n)] = (e * (1.0 / den)).astype(BF16)
            l_dst[out_rows(r, n), :] = jnp.broadcast_to(m + jnp.log(den), (BLK, HD))
        for u, (r, n) in enumerate(units):
            acc = _dot(p_scr[u, :, BLK:], vd[blk(r, n), :])
            if n > 0:
                acc = acc + _dot(p_scr[u, :, :BLK], vd[blk(r, n - 1), :])
            o_dst[out_rows(r, n), :] = acc
        if not direct:
            _to_sequence(o_ref, o_dst, d, tmp)
            _to_sequence(l_ref, l_dst, d, tmp)

    rows_f32, rows_bf16 = pltpu.VMEM((seq, HD), F32), pltpu.VMEM((seq, HD), BF16)
    regrouped = [] if d == 1 else [rows_f32] * 2 + [rows_bf16] * 3 + ([] if d <= 4 else [rows_f32] * 2)
    out_spec = pl.BlockSpec((seq, HD), lambda b, hh: (b, hh))
    return pl.pallas_call(
        body, name=f"attn_forward_{g}", out_shape=[HBM_OUT((bsz * seq, AW), F32)] * 2, grid=(bsz, 4),
        in_specs=[_head_spec(seq, g, part) for part in range(3)] + [pl.BlockSpec((4, BLK, 2 * BLK), lambda b, hh: (0, 0, 0))],
        out_specs=[out_spec, out_spec],
        scratch_shapes=[pltpu.VMEM((len(units), BLK, 2 * BLK), F32), pltpu.VMEM((len(units), BLK, 2 * BLK), BF16)] + regrouped,
        compiler_params=_cp(("parallel", "parallel"), VMEM_CAP // 2),
    )(qkv, qkv, qkv, _in_hbm(bias))


def _attn_backward(g, qkv, do, dl, bias, prev_out, bsz, seq):
    d = DILATIONS[g]
    ln = seq // d
    units = [(r, n) for r in range(d) for n in range(ln // BLK)]

    def body(q_ref, k_ref, v_ref, do_ref, dl_ref, b_ref, *rest):
        dq_ref, dk_ref, dv_ref, db_ref = rest[-18:-14]
        wide, tmp, qd, kd, vd, dod, dld, dqd, dkd, dvd, s_scr, dp_scr, p_scr, ds_scr = rest[-14:]
        hs = pl.program_id(1)

        @pl.when((pl.program_id(0) == 0) & (hs == 0))
        def _():
            db_ref[...] = jnp.zeros_like(db_ref)

        for dst, src in ((qd, q_ref), (kd, k_ref), (vd, v_ref)):
            _by_subsequence(dst, src, d, wide, tmp)
        _by_subsequence(dod, do_ref, d, None, tmp)
        _by_subsequence(dld, dl_ref, d, None, tmp)
        dkd[...] = jnp.zeros_like(dkd)
        dvd[...] = jnp.zeros_like(dvd)
        blk = lambda r, n: pl.ds(r * ln + n * BLK, BLK)
        keys = lambda r, n: [(blk(r, n), slice(BLK, 2 * BLK))] + ([(blk(r, n - 1), slice(0, BLK))] if n > 0 else [])
        for u, (r, n) in enumerate(units):
            for rows, band in keys(r, n):
                s_scr[u, :, band] = _dot_nt(qd[blk(r, n), :], kd[rows, :])
                dp_scr[u, :, band] = _dot_nt(dod[blk(r, n), :], vd[rows, :])
        for u, (r, n) in enumerate(units):
            both = dld[blk(r, n), :]
            lse, delta = both[:, 0:1], both[:, 64:65]
            band = slice(BLK, 2 * BLK) if n == 0 else slice(0, 2 * BLK)
            p = jnp.exp(s_scr[u, :, band] * SCALE + b_ref[hs, :, band] - lse)
            ds = p * (dp_scr[u, :, band] - delta)
            p_scr[u, :, band] = p.astype(BF16)
            ds_scr[u, :, band] = ds.astype(BF16)
            db_ref[hs, :, band] += ds
        for u, (r, n) in enumerate(units):
            dq = jnp.zeros((BLK, HD), F32)
            for rows, band in keys(r, n):
                dvd[rows, :] += _dot_tn(p_scr[u, :, band], dod[blk(r, n), :])
                dkd[rows, :] += _dot_tn(ds_scr[u, :, band], qd[blk(r, n), :]) * SCALE
                dq = dq + _dot(ds_scr[u, :, band], kd[rows, :])
            dqd[blk(r, n), :] = dq * SCALE
        for out, acc in ((dq_ref, dqd), (dk_ref, dkd), (dv_ref, dvd)):
            if d == 1:
                out[...] = acc[...].astype(BF16)
            else:
                _to_sequence(wide, acc, d, tmp)
                out[...] = wide[...].astype(BF16)

    qkv_spec = _head_spec(seq, g, 0)
    out_spec = pl.BlockSpec((seq, HD), lambda b, hh: (b, hh))
    band_spec = pl.BlockSpec((4, BLK, 2 * BLK), lambda b, hh: (0, 0, 0))
    ins = [qkv, qkv, qkv, _in_hbm(do), _in_hbm(dl), _in_hbm(bias)]
    in_specs = [_head_spec(seq, g, part) for part in range(3)] + [out_spec, out_spec, band_spec]
    aliases = {}
    if prev_out is not None:
        ins += list(prev_out)
        in_specs += [ANY] * 3
        aliases = {6: 0, 7: 1, 8: 2}
    rows_bf16, rows_f32 = pltpu.VMEM((seq, HD), BF16), pltpu.VMEM((seq, HD), F32)
    staged = [pltpu.VMEM((len(units), BLK, 2 * BLK), F32)] * 2 + [pltpu.VMEM((len(units), BLK, 2 * BLK), BF16)] * 2
    dq, dk, dv, db = pl.pallas_call(
        body, name=f"attn_backward_{g}", out_shape=[HBM_OUT((bsz * seq, QW), BF16)] * 3 + [SDS((4, BLK, 2 * BLK), F32)], grid=(bsz, 4),
        in_specs=in_specs, out_specs=[qkv_spec] * 3 + [band_spec], input_output_aliases=aliases,
        scratch_shapes=[rows_f32] * 2 + [rows_bf16] * 4 + [rows_f32] * 4 + staged,
        compiler_params=_cp(("arbitrary", "arbitrary"), VMEM_CAP // 2),
    )(*ins)
    return (dq, dk, dv), db


def _mix_forward(gates, og, lg, x2, tgt, gate, w_ao, w_co, w_o, conv_w, conv_b, ln_g, ln_b, bsz, seq, tm=256):
    t = x2.shape[0]
    spt = seq // tm

    def body(g_ref, o1, o2, o3, l1, l2, l3, x_ref, t_ref, gate_ref, wao_ref, wco_ref, wo_ref, cw_ref, cb_ref, lng_ref, lnb_ref,
             ain_ref, sin_ref, mrg_ref, dy_ref, aout_ref, sout_ref, yc_ref, o_ref, lj_ref, dxr_ref, vec_ref, dgate_ref, zc_ref):
        b, i = pl.program_id(0), pl.program_id(1)

        @pl.when((b == 0) & (i == 0))
        def _():
            vec_ref[...] = jnp.zeros_like(vec_ref)

        @pl.when(i == 0)
        def _():
            zc_ref[...] = jnp.zeros_like(zc_ref)
            dgate_ref[...] = jnp.zeros_like(dgate_ref)

        g_attn, u, bg, cg, g_conv, m_attn, m_conv = (g_ref[:, lo:hi].astype(F32) for lo, hi in GATE_COLS)
        la, lb, lc = l1[...], l2[...], l3[...]
        mx = jnp.maximum(la, jnp.maximum(lb, lc))
        ea, eb, ec = jnp.exp(la - mx), jnp.exp(lb - mx), jnp.exp(lc - mx)
        den = ea + eb + ec
        o = (ea * o1[...] + eb * o2[...] + ec * o3[...]) / den
        o_ref[...] = o
        lj_ref[...] = mx + jnp.log(den)
        a_in = o * (g_attn * _sig(g_attn))
        ain_ref[...] = a_in.astype(BF16)
        a_out = _dot(a_in.astype(BF16), wao_ref[...])
        aout_ref[...] = a_out.astype(BF16)
        z = cg * u
        rows = lax.broadcasted_iota(jnp.int32, (tm, D), 0)
        c6, c7 = zc_ref[6:7, :], zc_ref[7:8, :]
        z1 = jnp.where(rows == 0, c7, pltpu.roll(z, 1, 0))
        z2 = jnp.where(rows == 0, c6, jnp.where(rows == 1, c7, pltpu.roll(z, 2, 0)))
        zc_ref[...] = z[tm - 8:tm, :]
        y_conv = (cw_ref[0:1, :] * z2 + cw_ref[1:2, :] * z1 + cw_ref[2:3, :] * z) + cb_ref[...]
        yc_ref[...] = y_conv.astype(BF16)
        s_in = bg * y_conv * (g_conv * _sig(g_conv))
        sin_ref[...] = s_in.astype(BF16)
        s_out = _dot(s_in.astype(BF16), wco_ref[...])
        sout_ref[...] = s_out.astype(BF16)
        merged = _sig(m_attn) * a_out + _sig(m_conv) * s_out
        mrg_ref[...] = merged.astype(BF16)
        y = _dot(merged.astype(BF16), wo_ref[...])
        gate1 = 1.0 + gate_ref[0]
        r = ALPHA * x_ref[...] + gate1 * y
        mu = jnp.mean(r, axis=1, keepdims=True)
        rc = r - mu
        rstd = lax.rsqrt(jnp.mean(rc * rc, axis=1, keepdims=True) + LN_EPS)
        xhat = rc * rstd
        diff = (xhat * lng_ref[...] + lnb_ref[...]) - t_ref[...]
        dout = diff * (1.0 / D)
        vec_ref[0:1, :] += jnp.sum(dout * xhat, axis=0, keepdims=True)
        vec_ref[1:2, :] += jnp.sum(dout, axis=0, keepdims=True)
        vec_ref[2:3, :] += jnp.sum(diff * diff, axis=0, keepdims=True)
        dxh = dout * lng_ref[...]
        dr = rstd * (dxh - jnp.mean(dxh, axis=1, keepdims=True) - xhat * jnp.mean(dxh * xhat, axis=1, keepdims=True))
        dxr_ref[...] = ALPHA * dr
        dy_ref[...] = (dr * gate1).astype(BF16)
        dgate_ref[0] += jnp.sum(dr * y, axis=0, keepdims=True)

    tok = lambda w: pl.BlockSpec((tm, w), lambda b, i: (b * spt + i, 0))
    const = lambda s: pl.BlockSpec(s, lambda b, i: (0,) * len(s))
    per_seq = pl.BlockSpec((1, 1, D), lambda b, i: (b, 0, 0))
    outs = pl.pallas_call(
        body, name="mix_forward", grid=(bsz, spt),
        out_shape=[HBM_OUT((t, AW), BF16), HBM_OUT((t, D), BF16), HBM_OUT((t, D), BF16), HBM_OUT((t, D), BF16), HBM_OUT((t, D), BF16),
                   HBM_OUT((t, D), BF16), HBM_OUT((t, D), BF16), HBM_OUT((t, AW), F32), HBM_OUT((t, AW), F32), HBM_OUT((t, D), F32),
                   SDS((8, D), F32), SDS((bsz, 1, D), F32)],
        in_specs=[tok(NGATE)] + [tok(AW)] * 6 + [tok(D), tok(D), per_seq, const((AW, D)), const((D, D)), const((D, D)),
                                                 const((3, D)), const((1, D)), const((1, D)), const((1, D))],
        out_specs=[tok(AW), tok(D), tok(D), tok(D), tok(D), tok(D), tok(D), tok(AW), tok(AW), tok(D), const((8, D)), per_seq],
        scratch_shapes=[pltpu.VMEM((8, D), F32)],
        compiler_params=_cp(("arbitrary", "arbitrary"), VMEM_CAP),
    )(gates, *map(_in_hbm, og), *map(_in_hbm, lg), x2, tgt, gate, w_ao, w_co, w_o, conv_w, conv_b, ln_g, ln_b)
    return outs


def _mix_backward(gates, dy, a_out, s_out, y_conv, o, lj, w_ao, w_co, w_o, conv_w, vec_f, bsz, seq, tm=256):
    t = dy.shape[0]
    spt = seq // tm

    def body(g_ref, dy_ref, aout_ref, sout_ref, yc_ref, o_ref, lj_ref, wao_ref, wco_ref, wo_ref, cw_ref, vecf_ref,
             dg_ref, do_ref, dl_ref, daout_ref, dsout_ref, vec_ref, car_ref):
        b, i = pl.program_id(0), pl.program_id(1)

        @pl.when((b == 0) & (i == 0))
        def _():
            vec_ref[...] = vecf_ref[...]

        @pl.when(i == 0)
        def _():
            car_ref[...] = jnp.zeros_like(car_ref)

        g_attn, u, bg, cg, g_conv, m_attn, m_conv = (g_ref[:, lo:hi].astype(F32) for lo, hi in GATE_COLS)
        dmerged = _dot_nt(dy_ref[...], wo_ref[...])
        sa, sc = _sig(m_attn), _sig(m_conv)
        da_out = (dmerged * sa).astype(BF16)
        ds_out = (dmerged * sc).astype(BF16)
        daout_ref[...] = da_out
        dsout_ref[...] = ds_out
        dg_ref[:, 4608:5632] = (dmerged * aout_ref[...].astype(F32) * (sa * (1.0 - sa))).astype(BF16)
        dg_ref[:, 5632:6656] = (dmerged * sout_ref[...].astype(F32) * (sc * (1.0 - sc))).astype(BF16)
        da_in = _dot_nt(da_out, wao_ref[...])
        ds_in = _dot_nt(ds_out, wco_ref[...])
        sga = _sig(g_attn)
        o = o_ref[...]
        do = da_in * (g_attn * sga)
        do_ref[...] = do
        dg_ref[:, 0:512] = (da_in * o * (sga * (1.0 + g_attn * (1.0 - sga)))).astype(BF16)
        prod = do * o
        lane = lax.broadcasted_iota(jnp.int32, (tm, HD), 1)
        for j in range(4):
            cs = slice(j * HD, (j + 1) * HD)
            delta = jnp.sum(prod[:, cs], axis=1, keepdims=True)
            dl_ref[:, cs] = jnp.where(lane < 64, lj_ref[:, cs], delta)
        sgc = _sig(g_conv)
        silu_c = g_conv * sgc
        yc = yc_ref[...].astype(F32)
        dg_ref[:, 1536:2560] = (ds_in * yc * silu_c).astype(BF16)
        dg_ref[:, 3584:4608] = (ds_in * bg * yc * (sgc * (1.0 + g_conv * (1.0 - sgc)))).astype(BF16)
        dyc = ds_in * bg * silu_c
        rows = lax.broadcasted_iota(jnp.int32, (tm, D), 0)
        c0, c1 = car_ref[0:1, :], car_ref[1:2, :]
        n1 = jnp.where(rows == tm - 1, c0, pltpu.roll(dyc, tm - 1, 0))
        n2 = jnp.where(rows == tm - 2, c0, jnp.where(rows == tm - 1, c1, pltpu.roll(dyc, tm - 2, 0)))
        car_ref[...] = dyc[0:8, :]
        dz = cw_ref[2:3, :] * dyc + cw_ref[1:2, :] * n1 + cw_ref[0:1, :] * n2
        z = cg * u
        dg_ref[:, 512:1536] = (dz * cg).astype(BF16)
        dg_ref[:, 2560:3584] = (dz * u).astype(BF16)
        vec_ref[3:4, :] += jnp.sum(n2 * z, axis=0, keepdims=True)
        vec_ref[4:5, :] += jnp.sum(n1 * z, axis=0, keepdims=True)
        vec_ref[5:6, :] += jnp.sum(dyc * z, axis=0, keepdims=True)
        vec_ref[6:7, :] += jnp.sum(dyc, axis=0, keepdims=True)

    tok = lambda w: pl.BlockSpec((tm, w), lambda b, i: (b * spt + (spt - 1 - i), 0))
    const = lambda s: pl.BlockSpec(s, lambda b, i: (0,) * len(s))
    return pl.pallas_call(
        body, name="mix_backward", grid=(bsz, spt),
        out_shape=[HBM_OUT((t, NGATE), BF16), HBM_OUT((t, AW), F32), HBM_OUT((t, AW), F32), HBM_OUT((t, D), BF16), HBM_OUT((t, D), BF16),
                   SDS((8, D), F32)],
        in_specs=[tok(NGATE), tok(D), tok(D), tok(D), tok(D), tok(AW), tok(AW), const((AW, D)), const((D, D)), const((D, D)), const((3, D)),
                  const((8, D))],
        out_specs=[tok(NGATE), tok(AW), tok(AW), tok(D), tok(D), const((8, D))],
        scratch_shapes=[pltpu.VMEM((8, D), F32)],
        compiler_params=_cp(("arbitrary", "arbitrary"), VMEM_CAP),
    )(gates, dy, a_out, s_out, y_conv, o, lj, w_ao, w_co, w_o, conv_w, vec_f)


def _halves_out(a):
    kind, nr, nc = W_CUTS[a]
    shape = (nr // 2, W_FULL[a][1]) if kind == "col" else (NCHIP, nr // 2, nc)
    return [SDS(shape, F32), SDS(shape, BF16)]


def _write_halves(a, acc_ref, c, mine_ref, theirs_ref):
    kind, nr, nc = W_CUTS[a]
    hr = nr // 2
    if kind == "col":
        mine_ref[...] = acc_ref[pl.ds(pl.multiple_of(c * hr, hr), hr), :]
        theirs_ref[...] = acc_ref[pl.ds(pl.multiple_of((1 - c) * hr, hr), hr), :].astype(BF16)
    else:
        for k in range(NCHIP):
            mine_ref[k] = acc_ref[pl.ds(pl.multiple_of(k * nr + c * hr, hr), hr), :]
            theirs_ref[k] = acc_ref[pl.ds(pl.multiple_of(k * nr + (1 - c) * hr, hr), hr), :].astype(BF16)


def _out_weight_grads(a_in, da_out, s_in, ds_out, merged, dy, core, tk=512):
    t = dy.shape[0]
    nt = t // tk

    def body(c_ref, ain_ref, da_ref, sin_ref, ds_ref, m_ref, dy_ref, *rest):
        outs, (gao, gco, go) = rest[:6], rest[6:]

        @pl.when(pl.program_id(0) == 0)
        def _():
            gao[...] = jnp.zeros_like(gao)
            gco[...] = jnp.zeros_like(gco)
            go[...] = jnp.zeros_like(go)

        gao[...] += _dot_tn(ain_ref[...], da_ref[...])
        gco[...] += _dot_tn(sin_ref[...], ds_ref[...])
        go[...] += _dot_tn(m_ref[...], dy_ref[...])

        @pl.when(pl.program_id(0) == nt - 1)
        def _():
            for a, acc in ((1, gao), (2, gco), (3, go)):
                _write_halves(a, acc, c_ref[0], outs[2 * a - 2], outs[2 * a - 1])

    tok = lambda w: pl.BlockSpec((tk, w), lambda i, cr: (i, 0))
    out_shape = _halves_out(1) + _halves_out(2) + _halves_out(3)
    outs = pl.pallas_call(
        body, name="out_weight_grads", out_shape=out_shape,
        grid_spec=pltpu.PrefetchScalarGridSpec(
            num_scalar_prefetch=1, grid=(nt,), in_specs=[tok(AW), tok(D), tok(D), tok(D), tok(D), tok(D)],
            out_specs=[pl.BlockSpec(o.shape, lambda i, cr, nd=len(o.shape): (0,) * nd) for o in out_shape],
            scratch_shapes=[pltpu.VMEM((AW, D), F32), pltpu.VMEM((D, D), F32), pltpu.VMEM((D, D), F32)]),
        compiler_params=_cp(("arbitrary",), VMEM_CAP),
    )(core, a_in, da_out, s_in, ds_out, merged, dy)
    return [(outs[0], outs[1]), (outs[2], outs[3]), (outs[4], outs[5])]


def _input_grad(dq, dk, dv, dgates, w, x2, dxr, sc1p, seq, sums, tm=512):
    t = x2.shape[0]
    nt = t // tm
    spt = seq // tm
    bsz = t // seq
    n = len(sums)
    gblk = NGATE // 4
    nsteps = 3 + 4

    def body(dq_ref, dk_ref, dv_ref, dg_ref, wq_ref, wg_ref, x_ref, dxr_ref, sc_ref, *rest):
        src, (dx_ref, dsh_ref, dsc_ref), land = rest[:n], rest[n:n + 3], rest[n + 3:2 * n + 3]
        acc_ref, send_sems, recv_sems = rest[2 * n + 3:]
        j, i = pl.program_id(0), pl.program_id(1)
        px, py, pc = _place()
        chips = [(1 - px, py), (px, 1 - py), (1 - px, 1 - py)]
        copies = [pltpu.make_async_remote_copy(src_ref=src[a].at[2 * cx + cy], dst_ref=land[a].at[r], send_sem=send_sems.at[3 * a + r],
                                               recv_sem=recv_sems.at[3 * a + r], device_id=(cx, cy, pc), device_id_type=MESH)
                  for a in range(n) for r, (cx, cy) in enumerate(chips)]
        rows = pl.ds(pl.multiple_of(i * tm, tm), tm)

        @pl.when((i == 0) & (j == 0))
        def _():
            for cp in copies:
                cp.start()

        for k, ref in enumerate((dq_ref, dk_ref, dv_ref)):
            @pl.when(j == k)
            def _(k=k, ref=ref):
                part = _dot_nt(ref[...], wq_ref[...])
                if k == 0:
                    acc_ref[rows, :] = part
                else:
                    acc_ref[rows, :] += part

        @pl.when((j >= 3) & (j < nsteps - 1))
        def _():
            acc_ref[rows, :] += _dot_nt(dg_ref[...], wg_ref[...])

        @pl.when(j == nsteps - 1)
        def _():
            dh = acc_ref[rows, :] + _dot_nt(dg_ref[...], wg_ref[...])
            dx_ref[...] = dh * sc_ref[0] + dxr_ref[...]

            @pl.when(i % spt == 0)
            def _():
                dsh_ref[...] = jnp.zeros_like(dsh_ref)
                dsc_ref[...] = jnp.zeros_like(dsc_ref)

            dsh_ref[0] += jnp.sum(dh, axis=0, keepdims=True)
            dsc_ref[0] += jnp.sum(dh * x_ref[...], axis=0, keepdims=True)

        @pl.when((i == nt - 1) & (j == nsteps - 1))
        def _():
            for cp in copies:
                cp.wait()

    def held(k):
        return lambda j, i: (jnp.where(j == k, i, jnp.where(j < k, 0, nt - 1)), 0)

    last = lambda j, i: (jnp.where(j == nsteps - 1, i, 0), 0)
    outs = pl.pallas_call(
        body, name="input_grad", grid=(nsteps, nt),
        out_shape=[SDS((t, D), F32), SDS((bsz, 1, D), F32), SDS((bsz, 1, D), F32)] + [SDS((3,) + s.shape[1:], BF16) for s in sums],
        in_specs=[pl.BlockSpec((tm, QW), held(0)), pl.BlockSpec((tm, QW), held(1)), pl.BlockSpec((tm, QW), held(2)),
                  pl.BlockSpec((tm, gblk), lambda j, i: (jnp.where(j >= 3, i, 0), jnp.clip(j - 3, 0, 3))),
                  pl.BlockSpec((D, QW), lambda j, i: (0, jnp.minimum(j, 2))),
                  pl.BlockSpec((pl.Element(D), pl.Element(gblk)), lambda j, i: (0, pl.multiple_of(3 * QW + gblk * jnp.clip(j - 3, 0, 3), 128))),
                  pl.BlockSpec((tm, D), last), pl.BlockSpec((tm, D), last),
                  pl.BlockSpec((1, 1, D), lambda j, i: (jnp.where(j == nsteps - 1, i // spt, 0), 0, 0))] + [ANY] * n,
        out_specs=[pl.BlockSpec((tm, D), last),
                   pl.BlockSpec((1, 1, D), lambda j, i: (jnp.where(j == nsteps - 1, i // spt, 0), 0, 0)),
                   pl.BlockSpec((1, 1, D), lambda j, i: (jnp.where(j == nsteps - 1, i // spt, 0), 0, 0))] + [ANY] * n,
        scratch_shapes=[pltpu.VMEM((t, D), F32), pltpu.SemaphoreType.DMA((3 * NCHIP,)), pltpu.SemaphoreType.DMA((3 * NCHIP,))],
        compiler_params=_cp(("arbitrary", "arbitrary"), VMEM_CAP, side=True),
    )(dq, dk, dv, dgates, w, w, x2, dxr, sc1p, *sums)
    return outs[0], outs[1], outs[2], outs[3:]


def _in_weight_grad(ht, dq, dk, dv, dgates, core):
    t = ht.shape[1]
    hr = D // 2

    def body(c_ref, ht_ref, dq_ref, dk_ref, dv_ref, dg_ref, mine_ref, theirs_ref, acc_ref):
        j = pl.program_id(0)
        for k, ref in enumerate((dq_ref, dk_ref, dv_ref)):
            @pl.when((j >= k * NQT) & (j < (k + 1) * NQT))
            def _(ref=ref):
                acc_ref[...] = _dot(ht_ref[...], ref[...])

        @pl.when(j >= 3 * NQT)
        def _():
            acc_ref[...] = _dot(ht_ref[...], dg_ref[...])

        _write_halves(0, acc_ref, c_ref[0], mine_ref, theirs_ref)

    def part(k):
        return pl.BlockSpec((t, TN), lambda j, cr: (0, jnp.clip(j - k * NQT, 0, NQT - 1)))

    out_spec = pl.BlockSpec((hr, TN), lambda j, cr: (0, j))
    return pl.pallas_call(
        body, name="in_weight_grad", out_shape=[SDS((hr, NCOL), F32), SDS((hr, NCOL), BF16)],
        grid_spec=pltpu.PrefetchScalarGridSpec(
            num_scalar_prefetch=1, grid=(NPT,),
            in_specs=[pl.BlockSpec((D, t), lambda j, cr: (0, 0)), part(0), part(1), part(2),
                      pl.BlockSpec((t, TN), lambda j, cr: (0, jnp.maximum(j - 3 * NQT, 0)))],
            out_specs=[out_spec, out_spec], scratch_shapes=[pltpu.VMEM((D, TN), F32)]),
        compiler_params=_cp(("arbitrary",), VMEM_CAP),
    )(core, ht, dq, dk, dv, dgates)


def _sum_partials(gathered):
    def body(g_ref, o_ref):
        acc = g_ref[0]
        for k in range(1, 8):
            acc = acc + g_ref[k]
        o_ref[...] = acc

    return pl.pallas_call(body, name="sum_partials", out_shape=SDS(gathered.shape[1:], F32), in_specs=[VMEM_SPEC], out_specs=VMEM_SPEC)(gathered)


def _adamw(w, g, m, v, name, tr=256):
    r, cdim = w.shape
    tr = tr if cdim <= D else tr // 2
    tr = tr if (r % tr == 0 and r > tr) else r

    def body(w_ref, g_ref, m_ref, v_ref, go_ref, d_ref, nm_ref, nv_ref):
        gv = g_ref[...]
        go_ref[...] = gv
        nm = B1 * m_ref[...] + (1.0 - B1) * gv
        nv = B2 * v_ref[...] + (1.0 - B2) * (gv * gv)
        m_hat = nm / (1.0 - B1 ** STEP)
        v_hat = nv / (1.0 - B2 ** STEP)
        d_ref[...] = -LR * (m_hat / (jnp.sqrt(v_hat) + EPS) + WD * w_ref[...])
        nm_ref[...] = nm
        nv_ref[...] = nv

    spec = pl.BlockSpec((tr, cdim), lambda i: (i, 0))
    return pl.pallas_call(
        body, name=name, grid=(r // tr,), out_shape=[SDS((r, cdim), F32)] * 4, in_specs=[spec] * 4, out_specs=[spec] * 4,
        compiler_params=_cp(("parallel",), VMEM_CAP // 2),
    )(w, g, m, v)


def _t5_bucket(dist):
    n = jnp.maximum(dist, 1).astype(F32)
    large = MAX_EXACT + (jnp.log(n / MAX_EXACT) / math.log(MAX_DISTANCE / MAX_EXACT) * (N_BUCKETS - MAX_EXACT)).astype(jnp.int32)
    large = jnp.minimum(large, N_BUCKETS - 1)
    return jnp.where(dist < MAX_EXACT, dist, large)


def _band_buckets():
    a = jnp.arange(BLK)[:, None]
    b = jnp.arange(2 * BLK)[None, :]
    steps = jnp.maximum(a + BLK - b, 0)
    return jnp.stack([_t5_bucket(steps * d) for d in DILATIONS]).astype(jnp.int32)


def _pad_rows(a, rows=8):
    return jnp.pad(a, ((0, rows - a.shape[0]), (0, 0)))


def kernel(x, c, w_ada, b_ada, w_in, conv_w, conv_b, rel_bias, w_attn_out, w_conv_out, w_o, ln_g, ln_b, loss_target, m_w_ada, m_b_ada, m_w_in, m_conv_w, m_conv_b, m_rel_bias, m_w_attn_out, m_w_conv_out, m_w_o, m_ln_g, m_ln_b, v_w_ada, v_b_ada, v_w_in, v_conv_w, v_conv_b, v_rel_bias, v_w_attn_out, v_w_conv_out, v_w_o, v_ln_g, v_ln_b):
    bsz, seq, _ = x.shape
    t = bsz * seq
    mx, my, mc = _place()
    chip = 2 * mx + my
    dev = 4 * mx + 2 * my + mc
    x2 = x.reshape(t, D)
    tgt = loss_target.reshape(t, D)

    mine = _to_bf16_windows([w[0] for w in (w_in, w_attn_out, w_conv_out, w_o)])

    n_ada = w_ada.shape[2]
    n_cw = conv_w.shape[2]
    c_and_cw = jnp.concatenate([_pad_rows(c), jnp.pad(conv_w[0], ((0, 5), (0, D - n_cw)))], axis=0)
    firsts = _all_gather8(c_and_cw, "gather_c_conv_w")
    c_all = firsts[:, 0:bsz, :].reshape(8 * bsz, D)
    conv_w_f = firsts[0::2, 8:11, 0:n_cw].transpose(1, 0, 2).reshape(3, D)
    b_cols = lax.dynamic_slice(b_ada, (0, chip * n_ada), (1, n_ada))
    mod_part = _ada_forward(c_all, w_ada[0], b_cols)
    mod_parts = _all_gather8(mod_part, "gather_mod")
    mod_all = mod_parts[0::2].transpose(1, 0, 2).reshape(8 * bsz, 3 * D)
    mod = lax.dynamic_slice(mod_all, (dev * bsz, 0), (bsz, 3 * D))
    shift = mod[:, 0:D].reshape(bsz, 1, D)
    sc1p = 1.0 + mod[:, D:2 * D].reshape(bsz, 1, D)
    gate = mod[:, 2 * D:].reshape(bsz, 1, D)

    h, ht = _modulate(x2, sc1p, shift, seq)
    tab = lax.dynamic_index_in_dim(jnp.asarray(_tile_tables()), chip, 0, keepdims=False)
    qkv, gates, (w_in_f, w_ao_f, w_co_f, w_o_f) = _project_gather(h, mine, tab)
    buckets = _band_buckets()
    bias = _bias_tables(rel_bias, buckets)
    og, lg = [], []
    for g in range(3):
        o_g, l_g = _attn_forward(g, qkv, bias[g], bsz, seq)
        og.append(o_g)
        lg.append(l_g)
    (a_in, s_in, merged, dy, a_out, s_out, y_conv, o, lj, dxr, vec_f, dgate) = _mix_forward(
        gates, og, lg, x2, tgt, gate, w_ao_f, w_co_f, w_o_f, conv_w_f, conv_b, ln_g, ln_b, bsz, seq)

    dgates, do, dl, da_out, ds_out, vec = _mix_backward(gates, dy, a_out, s_out, y_conv, o, lj, w_ao_f, w_co_f, w_o_f, conv_w_f, vec_f, bsz, seq)
    core = jnp.reshape(mc, (1,)).astype(jnp.int32)
    small_grads = _out_weight_grads(a_in, da_out, s_in, ds_out, merged, dy, core)
    dqkv, dbs = None, []
    for g in range(3):
        dqkv, db = _attn_backward(g, qkv, do, dl, bias[g], dqkv, bsz, seq)
        dbs.append(db)
    dq, dk, dv = dqkv
    drb = _bias_grad(jnp.stack(dbs), buckets)
    drb = drb[:, :, 0:4].transpose(1, 0, 2).reshape(N_BUCKETS, 12)
    g_in = _in_weight_grad(ht, dq, dk, dv, dgates, core)

    halves = [tuple(g_in)] + small_grads
    got = _swap_halves([theirs for _, theirs in halves])
    sums = _chip_sums([own for own, _ in halves], got)
    grad_x, dshift, dscale, landed = _input_grad(dq, dk, dv, dgates, w_in_f, x2, dxr, sc1p, seq, [s[1] for s in sums])
    halves = _reduce_mine([own for own, _ in sums], landed)
    gw_in, gw_ao, gw_co, gw_o = _join_halves(halves)

    dmod = jnp.concatenate([dshift, dscale, dgate], axis=2).reshape(bsz * 3, D)
    drb_row = jnp.pad(drb.reshape(1, N_BUCKETS * 12), ((0, 0), (0, D - N_BUCKETS * 12)))
    vec = lax.dynamic_update_slice(vec, drb_row, (7, 0))
    packed = jnp.concatenate([vec, _pad_rows(dmod)], axis=0)
    gathered = _all_gather8(packed, "gather_small")
    small = _sum_partials(gathered)
    g_ln_g, g_ln_b, loss_lanes = small[0:1], small[1:2], small[2:3]
    g_conv_w_full, g_conv_b = small[3:6], small[6:7]
    g_rel_bias = small[7, 0:N_BUCKETS * 12].reshape(N_BUCKETS, 12)
    loss = 0.5 / D * jnp.sum(loss_lanes)
    dmod_all = gathered[:, 8:8 + 3 * bsz, :].reshape(8 * bsz, 3 * D)
    dmod_cols = lax.dynamic_slice(dmod_all, (0, chip * n_ada), (8 * bsz, n_ada))
    gw_ada, gb_ada = _ada_backward(c_all, dmod_cols, dmod_all)
    g_conv_w = lax.dynamic_slice(g_conv_w_full, (0, chip * n_cw), (3, n_cw))

    names = ["w_ada", "b_ada", "w_in", "conv_w", "conv_b", "rel_bias", "w_attn_out", "w_conv_out", "w_o", "ln_g", "ln_b"]
    two_d = lambda a: a.reshape(a.shape[-2:]) if a.ndim == 3 else a
    weights = dict(zip(names, map(two_d, (w_ada, b_ada, w_in, conv_w, conv_b, rel_bias, w_attn_out, w_conv_out, w_o, ln_g, ln_b))))
    ms = dict(zip(names, map(two_d, (m_w_ada, m_b_ada, m_w_in, m_conv_w, m_conv_b, m_rel_bias, m_w_attn_out, m_w_conv_out, m_w_o, m_ln_g, m_ln_b))))
    vs = dict(zip(names, map(two_d, (v_w_ada, v_b_ada, v_w_in, v_conv_w, v_conv_b, v_rel_bias, v_w_attn_out, v_w_conv_out, v_w_o, v_ln_g, v_ln_b))))
    grads = dict(zip(names, (gw_ada, gb_ada, gw_in, g_conv_w, g_conv_b, g_rel_bias, gw_ao, gw_co, gw_o, g_ln_g, g_ln_b)))
    shapes = dict(zip(names, (w_ada, b_ada, w_in, conv_w, conv_b, rel_bias, w_attn_out, w_conv_out, w_o, ln_g, ln_b)))
    grad_out, deltas, new_m, new_v = {}, {}, {}, {}
    for n in names:
        grad_out[n], deltas[n], new_m[n], new_v[n] = _adamw(weights[n], grads[n], ms[n], vs[n], f"adamw_{n}")
    shaped = lambda d: [d[n].reshape(shapes[n].shape) for n in names]
    return (loss, grad_x.reshape(bsz, seq, D), *shaped(grad_out), *shaped(deltas), *shaped(new_m), *shaped(new_v))
```

```python
import math

import numpy as np
import jax
import jax.numpy as jnp
from jax import lax
from jax.experimental import pallas as pl
from jax.experimental.pallas import tpu as pltpu

F32 = jnp.float32
BF16 = jnp.bfloat16
SDS = jax.ShapeDtypeStruct
MESH = pl.DeviceIdType.MESH
HBM_OUT = pltpu.HBM
ANY = pl.BlockSpec(memory_space=pl.ANY)
VMEM_SPEC = pl.BlockSpec(memory_space=pltpu.VMEM)

D = 1024
HD = 128
BLK = 128
QW = 1536
AW = 512
NGATE = 6656
GATE_COLS = ((0, 512), (512, 1536), (1536, 2560), (2560, 3584), (3584, 4608), (4608, 5632), (5632, 6656))
NCOL = 3 * QW + NGATE
TN = 512
NQT = QW // TN
NPT = NCOL // TN
DILATIONS = (1, 4, 16)
N_BUCKETS, MAX_EXACT, MAX_DISTANCE = 32, 16, 2048
ALPHA = 2.0 ** 0.25
LN_EPS = 1e-5
NEG = -1e30
SCALE = HD ** -0.5
LR, B1, B2, EPS, WD, STEP = 0.001, 0.9, 0.999, 1e-08, 0.01, 10
NCHIP = 4
VMEM_CAP = 60 * 2 ** 20


def _cp(sem=None, vmem=None, side=False):
    return pltpu.CompilerParams(dimension_semantics=sem, vmem_limit_bytes=vmem, has_side_effects=side)


def _dot(a, b):
    return jnp.dot(a, b, preferred_element_type=F32)


def _dot_nt(a, b):
    return lax.dot_general(a, b, (((1,), (1,)), ((), ())), preferred_element_type=F32)


def _dot_tn(a, b):
    return lax.dot_general(a, b, (((0,), (0,)), ((), ())), preferred_element_type=F32)


def _sig(x):
    return 1.0 / (1.0 + jnp.exp(-x))


def _in_hbm(a):
    return pltpu.with_memory_space_constraint(a, pltpu.HBM)


def _place():
    x, y, c = lax.axis_index("x"), lax.axis_index("y"), lax.axis_index("c")
    return x, y, c


def _all_gather8(v, name):
    r, cdim = v.shape

    def body(v_ref, out_ref, send_sems, recv_sems, local_sem):
        x, y, c = _place()
        me = 4 * x + 2 * y + c
        peers = [(x, y, 1 - c), (1 - x, y, c), (x, 1 - y, c), (1 - x, 1 - y, c),
                 (1 - x, y, 1 - c), (x, 1 - y, 1 - c), (1 - x, 1 - y, 1 - c)]
        mine = pltpu.make_async_copy(v_ref, out_ref.at[me], local_sem)
        mine.start()

        def copy(k, block, to):
            return pltpu.make_async_remote_copy(src_ref=v_ref, dst_ref=out_ref.at[block], send_sem=send_sems.at[k],
                                                recv_sem=recv_sems.at[k], device_id=to, device_id_type=MESH)

        sends = [copy(k, me, p) for k, p in enumerate(peers)]
        for cp in sends:
            cp.start()
        for k, (px, py, pc) in enumerate(peers):
            copy(k, 4 * px + 2 * py + pc, (px, py, pc)).wait_recv()
        for cp in sends:
            cp.wait_send()
        mine.wait()

    return pl.pallas_call(
        body, name=name, out_shape=SDS((8, r, cdim), v.dtype), in_specs=[VMEM_SPEC], out_specs=VMEM_SPEC,
        scratch_shapes=[pltpu.SemaphoreType.DMA((7,)), pltpu.SemaphoreType.DMA((7,)), pltpu.SemaphoreType.DMA(())],
        compiler_params=_cp(side=True),
    )(v)


W_CUTS = (("col", D, NCOL // NCHIP), ("col", AW, D // NCHIP), ("row", D // NCHIP, D), ("row", D // NCHIP, D))
W_FULL = ((D, NCOL), (AW, D), (D, D), (D, D))


def _shard_window(ref, cut, k, half):
    kind, nr, nc = cut
    hr = nr // 2
    if kind == "col":
        rows = pl.ds(0, nr) if half is None else pl.ds(pl.multiple_of(half * hr, 16), hr)
        return ref.at[rows, pl.ds(pl.multiple_of(k * nc, 128), nc)]
    if half is None:
        return ref.at[pl.ds(pl.multiple_of(k * nr, 16), nr), :]
    return ref.at[pl.ds(pl.multiple_of(k * nr + half * hr, 16), hr), :]


def _half_rows(ref, cut, half):
    hr = cut[1] // 2
    return ref.at[pl.ds(pl.multiple_of(half * hr, 16), hr), :]


def _to_bf16_windows(ws):
    x, y, _ = _place()
    chip = jnp.reshape(2 * x + y, (1,)).astype(jnp.int32)
    tr = 256
    n = len(ws)

    def body(c_ref, *refs):
        src, dst = refs[:n], refs[n:]
        dst[0][...] = src[0][...].astype(BF16)

        @pl.when(pl.program_id(0) == 0)
        def _():
            for a in range(1, n):
                dst[a][...] = src[a][...].astype(BF16)

    in_specs = [pl.BlockSpec((tr, W_CUTS[0][2]), lambda i, cr: (i, 0))]
    out_specs = [pl.BlockSpec((tr, W_CUTS[0][2]), lambda i, cr: (i, cr[0]))]
    for a in range(1, n):
        kind, nr, nc = W_CUTS[a]
        in_specs.append(pl.BlockSpec((nr, nc), lambda i, cr: (0, 0)))
        out_specs.append(pl.BlockSpec((nr, nc), (lambda i, cr: (0, cr[0])) if kind == "col" else (lambda i, cr: (cr[0], 0))))
    return pl.pallas_call(
        body, name="to_bf16", out_shape=[SDS(W_FULL[a], BF16) for a in range(n)],
        grid_spec=pltpu.PrefetchScalarGridSpec(num_scalar_prefetch=1, grid=(D // tr,), in_specs=in_specs, out_specs=out_specs),
        compiler_params=_cp(("arbitrary",)),
    )(chip, *ws)


def _swap_halves(theirs, name):
    n = len(theirs)

    def body(*refs):
        src, land = refs[:n], refs[n:2 * n]
        send_sems, recv_sems = refs[2 * n:]
        x, y, c = _place()
        copies = [pltpu.make_async_remote_copy(src_ref=src[a], dst_ref=land[a], send_sem=send_sems.at[a], recv_sem=recv_sems.at[a],
                                               device_id=(x, y, 1 - c), device_id_type=MESH) for a in range(n)]
        for cp in copies:
            cp.start()
        for cp in copies:
            cp.wait()

    return pl.pallas_call(
        body, name=name, out_shape=[SDS(v.shape, v.dtype) for v in theirs], in_specs=[ANY] * n, out_specs=[ANY] * n,
        scratch_shapes=[pltpu.SemaphoreType.DMA((n,)), pltpu.SemaphoreType.DMA((n,))],
        compiler_params=_cp(side=True),
    )(*theirs)


def _chip_sums(mines, gots, first, name):
    n = len(mines)
    x, y, _ = _place()
    me = jnp.reshape(2 * x + y, (1,)).astype(jnp.int32)

    def body(me_ref, *refs):
        ins, outs = refs[:2 * n], refs[2 * n:]
        for a in range(n):
            hr, nc = W_CUTS[first + a][1] // 2, W_CUTS[first + a][2]
            s = (ins[2 * a][...] + ins[2 * a + 1][...].astype(F32)).reshape(hr, nc)
            outs[2 * a + 1][0] = s.astype(BF16)

            @pl.when(pl.program_id(0) == me_ref[0])
            def _(a=a, s=s):
                outs[2 * a][...] = s

    in_specs, out_specs, out_shape = [], [], []
    for a in range(n):
        kind, nr, nc = W_CUTS[first + a]
        hr = nr // 2
        spec = pl.BlockSpec((hr, nc), lambda k, mr: (0, k)) if kind == "col" else pl.BlockSpec((1, hr, nc), lambda k, mr: (k, 0, 0))
        in_specs += [spec, spec]
        out_specs += [pl.BlockSpec((hr, nc), lambda k, mr: (0, 0)), pl.BlockSpec((1, hr, nc), lambda k, mr: (k, 0, 0))]
        out_shape += [SDS((hr, nc), F32), SDS((NCHIP, hr, nc), BF16)]
    outs = pl.pallas_call(
        body, name=name, out_shape=out_shape,
        grid_spec=pltpu.PrefetchScalarGridSpec(num_scalar_prefetch=1, grid=(NCHIP,), in_specs=in_specs, out_specs=out_specs),
        compiler_params=_cp(("arbitrary",), VMEM_CAP),
    )(me, *[v for pair in zip(mines, gots) for v in pair])
    return [(outs[2 * a], outs[2 * a + 1]) for a in range(n)]


def _reduce_mine(mines, gots):
    n = len(mines)
    _, _, c = _place()
    core = jnp.reshape(c, (1,)).astype(jnp.int32)
    tr = 256
    nsteps = W_CUTS[0][1] // 2 // tr

    def body(c_ref, *refs):
        ins, outs = refs[:2 * n], refs[2 * n:]

        def add(a):
            m_ref, g_ref = ins[2 * a], ins[2 * a + 1]
            outs[a][...] = ((m_ref[...] + g_ref[0].astype(F32)) + g_ref[1].astype(F32)) + g_ref[2].astype(F32)

        add(0)

        @pl.when(pl.program_id(0) == 0)
        def _():
            for a in range(1, n):
                add(a)

    nc0 = W_CUTS[0][2]
    in_specs = [pl.BlockSpec((tr, nc0), lambda i, cr: (i, 0)), pl.BlockSpec((3, tr, nc0), lambda i, cr: (0, i, 0))]
    out_specs = [pl.BlockSpec((tr, nc0), lambda i, cr: (cr[0] * nsteps + i, 0))]
    for a in range(1, n):
        hr, nc = W_CUTS[a][1] // 2, W_CUTS[a][2]
        in_specs += [pl.BlockSpec((hr, nc), lambda i, cr: (0, 0)), pl.BlockSpec((3, hr, nc), lambda i, cr: (0, 0, 0))]
        out_specs.append(pl.BlockSpec((hr, nc), lambda i, cr: (cr[0], 0)))
    return pl.pallas_call(
        body, name="reduce_mine", out_shape=[SDS((W_CUTS[a][1], W_CUTS[a][2]), F32) for a in range(n)],
        grid_spec=pltpu.PrefetchScalarGridSpec(num_scalar_prefetch=1, grid=(nsteps,), in_specs=in_specs, out_specs=out_specs),
        compiler_params=_cp(("arbitrary",), VMEM_CAP),
    )(core, *[v for pair in zip(mines, gots) for v in pair])


def _join_halves(fulls):
    n = len(fulls)

    def body(*refs):
        full = refs[n:2 * n]
        send_sems, recv_sems = refs[2 * n:]
        x, y, c = _place()
        sibling = (x, y, 1 - c)

        def swap(a, half):
            rows = _half_rows(full[a], W_CUTS[a], half)
            return pltpu.make_async_remote_copy(src_ref=rows, dst_ref=rows, send_sem=send_sems.at[a], recv_sem=recv_sems.at[a],
                                                device_id=sibling, device_id_type=MESH)

        sends = [swap(a, c) for a in range(n)]
        for cp in sends:
            cp.start()
        for a, cp in enumerate(sends):
            cp.wait_send()
            swap(a, 1 - c).wait_recv()

    return pl.pallas_call(
        body, name="join_grad_halves", out_shape=[SDS((W_CUTS[a][1], W_CUTS[a][2]), F32) for a in range(n)],
        in_specs=[ANY] * n, out_specs=[ANY] * n,
        scratch_shapes=[pltpu.SemaphoreType.DMA((n,)), pltpu.SemaphoreType.DMA((n,))],
        input_output_aliases={a: a for a in range(n)}, compiler_params=_cp(side=True),
    )(*fulls)


def _ada_forward(c_all, w_ada, b_cols):
    nb, nc = c_all.shape[0], w_ada.shape[1]

    def body(c_ref, w_ref, b_ref, o_ref):
        cv = c_ref[...]
        sc = (cv * _sig(cv)).astype(BF16)
        o_ref[...] = _dot(sc, w_ref[...].astype(BF16)) + b_ref[...]

    return pl.pallas_call(body, name="ada_forward", out_shape=SDS((nb, nc), F32), compiler_params=_cp(vmem=VMEM_CAP // 2))(c_all, w_ada, b_cols)


def _ada_backward(c_all, dmod_cols, dmod_all):
    nb, nc = dmod_cols.shape

    def body(c_ref, d_ref, a_ref, gw_ref, gb_ref):
        cv = c_ref[...]
        sc = (cv * _sig(cv)).astype(BF16)
        gw_ref[...] = _dot_tn(sc, d_ref[...].astype(BF16))
        gb_ref[...] = jnp.sum(a_ref[...], axis=0, keepdims=True)

    return pl.pallas_call(body, name="ada_backward", out_shape=[SDS((D, nc), F32), SDS((1, dmod_all.shape[1]), F32)],
                          compiler_params=_cp(vmem=VMEM_CAP // 2))(c_all, dmod_cols, dmod_all)


def _modulate(x2, sc1p, shift, seq, tm=256):
    t = x2.shape[0]
    spt = seq // tm

    def body(x_ref, sc_ref, sh_ref, h_ref, ht_ref):
        h = x_ref[...] * sc_ref[0] + sh_ref[0]
        h_ref[...] = h.astype(BF16)
        ht_ref[...] = h.T.astype(BF16)

    per_seq = pl.BlockSpec((1, 1, D), lambda i: (i // spt, 0, 0))
    return pl.pallas_call(
        body, name="modulate", out_shape=[HBM_OUT((t, D), BF16), HBM_OUT((D, t), BF16)], grid=(t // tm,),
        in_specs=[pl.BlockSpec((tm, D), lambda i: (i, 0)), per_seq, per_seq],
        out_specs=[pl.BlockSpec((tm, D), lambda i: (i, 0)), pl.BlockSpec((D, tm), lambda i: (0, i))],
        compiler_params=_cp(("parallel",)),
    )(x2, sc1p, shift)


TW = 256
TPS = NCOL // NCHIP // TW
NT = NCOL // TW
NQKV_T = 3 * QW // TW
N_TILE_SEMS = 2 * 3 * TPS


def _tile_tables():
    tabs = np.zeros((NCHIP, 3, NT), np.int32)
    for me in range(NCHIP):
        tiles = [TPS * (me ^ (s // TPS)) + s % TPS for s in range(NT)]
        tabs[me, 0] = tiles
        for row, (lo, hi) in enumerate(((0, NQKV_T), (NQKV_T, NT))):
            mine = [w - lo if lo <= w < hi else None for w in tiles]
            held = next(m for m in mine if m is not None)
            for s, m in enumerate(mine):
                held = held if m is None else m
                tabs[me, 1 + row, s] = held
    return tabs


def _project_gather(h, fulls, tab):
    t = h.shape[0]
    n = len(fulls)

    def body(tab_ref, h_ref, *rest):
        qkv_ref, g_ref = rest[n], rest[n + 1]
        full = rest[n + 2:2 * n + 2]
        w_buf, tile_sems, send_sems, recv_sems = rest[2 * n + 2:]
        s = pl.program_id(0)
        x, y, c = _place()
        me = 2 * x + y
        peers = [(x, 1 - y), (1 - x, y), (1 - x, 1 - y)]
        sibling = (x, y, 1 - c)

        def hop(a, r, stage, chip, half, to):
            window = _shard_window(full[a], W_CUTS[a], chip, half)
            k = N_TILE_SEMS + 6 * (a - 1) + 2 * r + stage
            return pltpu.make_async_remote_copy(src_ref=window, dst_ref=window, send_sem=send_sems.at[k], recv_sem=recv_sems.at[k],
                                                device_id=to, device_id_type=MESH)

        def tile_hop(q, stage, col_step, half, to):
            col = pl.multiple_of(tab_ref[0, col_step] * TW, TW)
            window = full[0].at[pl.ds(pl.multiple_of(half * (D // 2), 16), D // 2), pl.ds(col, TW)]
            k = 2 * (q - TPS) + stage
            return pltpu.make_async_remote_copy(src_ref=window, dst_ref=window, send_sem=send_sems.at[k], recv_sem=recv_sems.at[k],
                                                device_id=to, device_id_type=MESH)

        def send_tile(r, j):
            return tile_hop(TPS * (r + 1) + j, 0, j, c, (*peers[r], c))

        def arrive(a, r):
            px, py = peers[r]
            chip = 2 * px + py
            hop(a, r, 0, chip, c, (px, py, c)).wait_recv()
            hop(a, r, 1, chip, c, sibling).start()
            hop(a, r, 1, chip, 1 - c, sibling).wait_recv()

        def tile(step, slot):
            col = pl.multiple_of(tab_ref[0, step] * TW, TW)
            return pltpu.make_async_copy(full[0].at[:, pl.ds(col, TW)], w_buf.at[slot], tile_sems.at[slot])

        @pl.when(s == 0)
        def _():
            for r in range(2):
                for j in range(TPS):
                    send_tile(r, j).start()
            tile(0, 0).start()

        @pl.when((s + 1 >= TPS) & (s + 1 < NT))
        def _():
            tile_hop(s + 1, 1, s + 1, 1 - c, sibling).wait_recv()

        @pl.when(s + 1 < NT)
        def _():
            tile(s + 1, 1 - (s % 2)).start()

        @pl.when((s + 2 >= TPS) & (s + 2 < NT))
        def _():
            tile_hop(s + 2, 0, s + 2, c, sibling).wait_recv()
            tile_hop(s + 2, 1, s + 2, c, sibling).start()

        @pl.when(s + 2 == 2 * TPS - 1)
        def _():
            for j in range(TPS):
                send_tile(2, j).start()
            for a in range(1, n):
                for r in range(3):
                    hop(a, r, 0, me, c, (*peers[r], c)).start()

        slot = s % 2
        tile(s, slot).wait()
        is_qkv = tab_ref[0, s] < NQKV_T
        for k in range(2):
            @pl.when(slot == k)
            def _(k=k):
                acc = _dot(h_ref[...], w_buf[k])

                @pl.when(is_qkv)
                def _():
                    qkv_ref[...] = acc.astype(BF16)

                @pl.when(jnp.logical_not(is_qkv))
                def _():
                    g_ref[...] = acc.astype(BF16)

        @pl.when(s == NT - 1)
        def _():
            for a in range(1, n):
                for r in range(3):
                    arrive(a, r)
            for r in range(3):
                for j in range(TPS):
                    send_tile(r, j).wait_send()
                    tile_hop(TPS * (r + 1) + j, 1, TPS * (r + 1) + j, c, sibling).wait_send()
                for a in range(1, n):
                    hop(a, r, 0, me, c, (*peers[r], c)).wait_send()
                    px, py = peers[r]
                    hop(a, r, 1, 2 * px + py, c, sibling).wait_send()

    n_sems = N_TILE_SEMS + 6 * (n - 1)
    outs = pl.pallas_call(
        body, name="project_gather", out_shape=[HBM_OUT((t, 3 * QW), BF16), HBM_OUT((t, NGATE), BF16)] + [SDS(s, BF16) for s in W_FULL],
        grid_spec=pltpu.PrefetchScalarGridSpec(
            num_scalar_prefetch=1, grid=(NT,),
            in_specs=[pl.BlockSpec((t, D), lambda s, tab: (0, 0))] + [ANY] * n,
            out_specs=[pl.BlockSpec((t, TW), lambda s, tab: (0, tab[1, s])), pl.BlockSpec((t, TW), lambda s, tab: (0, tab[2, s]))] + [ANY] * n,
            scratch_shapes=[pltpu.VMEM((2, D, TW), BF16), pltpu.SemaphoreType.DMA((2,)),
                            pltpu.SemaphoreType.DMA((n_sems,)), pltpu.SemaphoreType.DMA((n_sems,))]),
        input_output_aliases={2 + a: 2 + a for a in range(n)},
        compiler_params=_cp(("arbitrary",), VMEM_CAP, side=True),
    )(tab, _in_hbm(h), *fulls)
    return outs[0], outs[1], outs[2:]


def _bias_tables(rel_bias, buckets):
    def body(tab_ref, bk_ref, o_ref):
        a = lax.broadcasted_iota(jnp.int32, (BLK, 2 * BLK), 0)
        b = lax.broadcasted_iota(jnp.int32, (BLK, 2 * BLK), 1)
        steps = a + BLK - b
        valid = (steps >= 0) & (steps <= BLK)
        for g in range(3):
            bk = bk_ref[g]
            for j in range(4):
                def pick(kk, acc, bk=bk, col=4 * g + j):
                    return jnp.where(bk == kk, tab_ref[kk, col], acc)

                acc = lax.fori_loop(0, N_BUCKETS, pick, jnp.zeros((BLK, 2 * BLK), F32))
                o_ref[g, j] = jnp.where(valid, acc, NEG)

    return pl.pallas_call(
        body, name="bias_tables", out_shape=SDS((3, 4, BLK, 2 * BLK), F32),
        in_specs=[pl.BlockSpec(memory_space=pltpu.SMEM), VMEM_SPEC], out_specs=VMEM_SPEC,
    )(rel_bias, buckets)


def _bias_grad(ds_sum, buckets):
    def body(ds_ref, bk_ref, o_ref):
        lane = lax.broadcasted_iota(jnp.int32, (1, 128), 1)
        for g in range(3):
            def bucket(kk, carry, g=g):
                row = jnp.zeros((1, 128), F32)
                for j in range(4):
                    v = jnp.where(bk_ref[g] == kk, ds_ref[g, j], 0.0)
                    v = jnp.sum(v.reshape(BLK // 8, 8, 2 * BLK), axis=0)
                    s = jnp.sum(jnp.sum(v, axis=1, keepdims=True), axis=0, keepdims=True)
                    row = jnp.where(lane == j, s, row)
                o_ref[g, pl.ds(kk, 1), :] = row
                return carry

            lax.fori_loop(0, N_BUCKETS, bucket, 0)

    return pl.pallas_call(body, name="bias_grad", out_shape=SDS((3, N_BUCKETS, 128), F32), in_specs=[VMEM_SPEC, VMEM_SPEC],
                          out_specs=VMEM_SPEC)(ds_sum, buckets)


def _sub_rows(d, r, first, size):
    return pl.ds(first * d + r, size) if d == 1 else pl.ds(first * d + r, size, stride=d)


def _head_spec(seq, g, part):
    return pl.BlockSpec((seq, HD), lambda b, hh: (b, part * (QW // HD) + 4 * g + hh))


def _rows(start, count, stride):
    return pl.ds(start, count) if stride == 1 else pl.ds(start, count, stride=stride)


def _gather_rows(dst, dst0, src, src0, stride, count):
    for first in range(0, count, BLK):
        dst[pl.ds(dst0 + first, BLK), :] = src[_rows(src0 + first * stride, BLK, stride), :].astype(dst.dtype)


def _scatter_rows(dst, dst0, stride, src, src0, count):
    for first in range(0, count, BLK):
        dst[_rows(dst0 + first * stride, BLK, stride), :] = src[pl.ds(src0 + first, BLK), :].astype(dst.dtype)


def _by_subsequence(dst, src, d, wide=None, tmp=None):
    seq = src.shape[0]
    ln = seq // d
    if wide is not None:
        wide[...] = src[...].astype(F32)
        src = wide
    if d <= 4:
        for r in range(d):
            _gather_rows(dst, r * ln, src, r, d, ln)
    else:
        quarter = seq // 4
        for r4 in range(4):
            _gather_rows(tmp, r4 * quarter, src, r4, 4, quarter)
        for r4 in range(4):
            for a in range(d // 4):
                _gather_rows(dst, (4 * a + r4) * ln, tmp, r4 * quarter + a, d // 4, ln)


def _to_sequence(dst, src, d, tmp=None):
    seq = dst.shape[0]
    ln = seq // d
    if d <= 4:
        for r in range(d):
            _scatter_rows(dst, r, d, src, r * ln, ln)
    else:
        quarter = seq // 4
        for r4 in range(4):
            for a in range(d // 4):
                _scatter_rows(tmp, r4 * quarter + a, d // 4, src, (4 * a + r4) * ln, ln)
        for r4 in range(4):
            _scatter_rows(dst, r4, 4, tmp, r4 * quarter, quarter)


def _attn_forward(g, qkv, bias, bsz, seq):
    d = DILATIONS[g]
    ln = seq // d
    units = [(r, n) for r in range(d) for n in range(ln // BLK)]

    def band(n):
        return slice(BLK, 2 * BLK) if n == 0 else slice(0, 2 * BLK)

    def body(q_ref, k_ref, v_ref, b_ref, o_ref, l_ref, *scratch):
        hs = pl.program_id(1)
        s_scr, p_scr = scratch[:2]
        if d == 1:
            qd, kd, vd = q_ref, k_ref, v_ref
        else:
            wide, tmp, qd, kd, vd = scratch[2:7]
            for dst, src in ((qd, q_ref), (kd, k_ref), (vd, v_ref)):
                _by_subsequence(dst, src, d, wide, tmp)
        blk = lambda r, n: pl.ds(r * ln + n * BLK, BLK)
        direct = d <= 4
        out_rows = (lambda r, n: _sub_rows(d, r, n * BLK, BLK)) if direct else blk
        o_dst, l_dst = (o_ref, l_ref) if direct else scratch[7:9]
        for u, (r, n) in enumerate(units):
            s_scr[u, :, BLK:] = _dot_nt(qd[blk(r, n), :], kd[blk(r, n), :])
            if n > 0:
                s_scr[u, :, :BLK] = _dot_nt(qd[blk(r, n), :], kd[blk(r, n - 1), :])
        for u, (r, n) in enumerate(units):
            s = s_scr[u, :, band(n)] * SCALE + b_ref[hs, :, band(n)]
            m = jnp.max(s, axis=1, keepdims=True)
            e = jnp.exp(s - m)
            den = jnp.sum(e, axis=1, keepdims=True)
            p_scr[u, :, band(n)] = (e * (1.0 / den)).astype(BF16)
            l_dst[out_rows(r, n), :] = jnp.broadcast_to(m + jnp.log(den), (BLK, HD))
        for u, (r, n) in enumerate(units):
            acc = _dot(p_scr[u, :, BLK:], vd[blk(r, n), :])
            if n > 0:
                acc = acc + _dot(p_scr[u, :, :BLK], vd[blk(r, n - 1), :])
            o_dst[out_rows(r, n), :] = acc
        if not direct:
            _to_sequence(o_ref, o_dst, d, tmp)
            _to_sequence(l_ref, l_dst, d, tmp)

    rows_f32, rows_bf16 = pltpu.VMEM((seq, HD), F32), pltpu.VMEM((seq, HD), BF16)
    regrouped = [] if d == 1 else [rows_f32] * 2 + [rows_bf16] * 3 + ([] if d <= 4 else [rows_f32] * 2)
    out_spec = pl.BlockSpec((seq, HD), lambda b, hh: (b, hh))
    return pl.pallas_call(
        body, name=f"attn_forward_{g}", out_shape=[HBM_OUT((bsz * seq, AW), F32)] * 2, grid=(bsz, 4),
        in_specs=[_head_spec(seq, g, part) for part in range(3)] + [pl.BlockSpec((4, BLK, 2 * BLK), lambda b, hh: (0, 0, 0))],
        out_specs=[out_spec, out_spec],
        scratch_shapes=[pltpu.VMEM((len(units), BLK, 2 * BLK), F32), pltpu.VMEM((len(units), BLK, 2 * BLK), BF16)] + regrouped,
        compiler_params=_cp(("parallel", "parallel"), VMEM_CAP // 2),
    )(qkv, qkv, qkv, _in_hbm(bias))


def _attn_backward(g, qkv, do, dl, bias, prev_out, bsz, seq):
    d = DILATIONS[g]
    ln = seq // d
    units = [(r, n) for r in range(d) for n in range(ln // BLK)]

    def body(q_ref, k_ref, v_ref, do_ref, dl_ref, b_ref, *rest):
        dq_ref, dk_ref, dv_ref, db_ref = rest[-18:-14]
        wide, tmp, qd, kd, vd, dod, dld, dqd, dkd, dvd, s_scr, dp_scr, p_scr, ds_scr = rest[-14:]
        hs = pl.program_id(1)

        @pl.when((pl.program_id(0) == 0) & (hs == 0))
        def _():
            db_ref[...] = jnp.zeros_like(db_ref)

        for dst, src in ((qd, q_ref), (kd, k_ref), (vd, v_ref)):
            _by_subsequence(dst, src, d, wide, tmp)
        _by_subsequence(dod, do_ref, d, None, tmp)
        _by_subsequence(dld, dl_ref, d, None, tmp)
        dkd[...] = jnp.zeros_like(dkd)
        dvd[...] = jnp.zeros_like(dvd)
        blk = lambda r, n: pl.ds(r * ln + n * BLK, BLK)
        keys = lambda r, n: [(blk(r, n), slice(BLK, 2 * BLK))] + ([(blk(r, n - 1), slice(0, BLK))] if n > 0 else [])
        for u, (r, n) in enumerate(units):
            for rows, band in keys(r, n):
                s_scr[u, :, band] = _dot_nt(qd[blk(r, n), :], kd[rows, :])
                dp_scr[u, :, band] = _dot_nt(dod[blk(r, n), :], vd[rows, :])
        for u, (r, n) in enumerate(units):
            both = dld[blk(r, n), :]
            lse, delta = both[:, 0:1], both[:, 64:65]
            band = slice(BLK, 2 * BLK) if n == 0 else slice(0, 2 * BLK)
            p = jnp.exp(s_scr[u, :, band] * SCALE + b_ref[hs, :, band] - lse)
            ds = p * (dp_scr[u, :, band] - delta)
            p_scr[u, :, band] = p.astype(BF16)
            ds_scr[u, :, band] = ds.astype(BF16)
            db_ref[hs, :, band] += ds
        for u, (r, n) in enumerate(units):
            dq = jnp.zeros((BLK, HD), F32)
            for rows, band in keys(r, n):
                dvd[rows, :] += _dot_tn(p_scr[u, :, band], dod[blk(r, n), :])
                dkd[rows, :] += _dot_tn(ds_scr[u, :, band], qd[blk(r, n), :]) * SCALE
                dq = dq + _dot(ds_scr[u, :, band], kd[rows, :])
            dqd[blk(r, n), :] = dq * SCALE
        for out, acc in ((dq_ref, dqd), (dk_ref, dkd), (dv_ref, dvd)):
            if d == 1:
                out[...] = acc[...].astype(BF16)
            else:
                _to_sequence(wide, acc, d, tmp)
                out[...] = wide[...].astype(BF16)

    qkv_spec = _head_spec(seq, g, 0)
    out_spec = pl.BlockSpec((seq, HD), lambda b, hh: (b, hh))
    band_spec = pl.BlockSpec((4, BLK, 2 * BLK), lambda b, hh: (0, 0, 0))
    ins = [qkv, qkv, qkv, _in_hbm(do), _in_hbm(dl), _in_hbm(bias)]
    in_specs = [_head_spec(seq, g, part) for part in range(3)] + [out_spec, out_spec, band_spec]
    aliases = {}
    if prev_out is not None:
        ins += list(prev_out)
        in_specs += [ANY] * 3
        aliases = {6: 0, 7: 1, 8: 2}
    rows_bf16, rows_f32 = pltpu.VMEM((seq, HD), BF16), pltpu.VMEM((seq, HD), F32)
    staged = [pltpu.VMEM((len(units), BLK, 2 * BLK), F32)] * 2 + [pltpu.VMEM((len(units), BLK, 2 * BLK), BF16)] * 2
    dq, dk, dv, db = pl.pallas_call(
        body, name=f"attn_backward_{g}", out_shape=[HBM_OUT((bsz * seq, QW), BF16)] * 3 + [SDS((4, BLK, 2 * BLK), F32)], grid=(bsz, 4),
        in_specs=in_specs, out_specs=[qkv_spec] * 3 + [band_spec], input_output_aliases=aliases,
        scratch_shapes=[rows_f32] * 2 + [rows_bf16] * 4 + [rows_f32] * 4 + staged,
        compiler_params=_cp(("arbitrary", "arbitrary"), VMEM_CAP // 2),
    )(*ins)
    return (dq, dk, dv), db


def _mix_forward(gates, og, lg, x2, tgt, gate, w_ao, w_co, w_o, conv_w, conv_b, ln_g, ln_b, bsz, seq, tm=256):
    t = x2.shape[0]
    spt = seq // tm

    def body(g_ref, o1, o2, o3, l1, l2, l3, x_ref, t_ref, gate_ref, wao_ref, wco_ref, wo_ref, cw_ref, cb_ref, lng_ref, lnb_ref,
             ain_ref, sin_ref, mrg_ref, dy_ref, aout_ref, sout_ref, yc_ref, o_ref, lj_ref, dxr_ref, vec_ref, dgate_ref, zc_ref):
        b, i = pl.program_id(0), pl.program_id(1)

        @pl.when((b == 0) & (i == 0))
        def _():
            vec_ref[...] = jnp.zeros_like(vec_ref)

        @pl.when(i == 0)
        def _():
            zc_ref[...] = jnp.zeros_like(zc_ref)
            dgate_ref[...] = jnp.zeros_like(dgate_ref)

        g_attn, u, bg, cg, g_conv, m_attn, m_conv = (g_ref[:, lo:hi].astype(F32) for lo, hi in GATE_COLS)
        la, lb, lc = l1[...], l2[...], l3[...]
        mx = jnp.maximum(la, jnp.maximum(lb, lc))
        ea, eb, ec = jnp.exp(la - mx), jnp.exp(lb - mx), jnp.exp(lc - mx)
        den = ea + eb + ec
        o = (ea * o1[...] + eb * o2[...] + ec * o3[...]) / den
        o_ref[...] = o
        lj_ref[...] = mx + jnp.log(den)
        a_in = o * (g_attn * _sig(g_attn))
        ain_ref[...] = a_in.astype(BF16)
        a_out = _dot(a_in.astype(BF16), wao_ref[...])
        aout_ref[...] = a_out.astype(BF16)
        z = cg * u
        rows = lax.broadcasted_iota(jnp.int32, (tm, D), 0)
        c6, c7 = zc_ref[6:7, :], zc_ref[7:8, :]
        z1 = jnp.where(rows == 0, c7, pltpu.roll(z, 1, 0))
        z2 = jnp.where(rows == 0, c6, jnp.where(rows == 1, c7, pltpu.roll(z, 2, 0)))
        zc_ref[...] = z[tm - 8:tm, :]
        y_conv = (cw_ref[0:1, :] * z2 + cw_ref[1:2, :] * z1 + cw_ref[2:3, :] * z) + cb_ref[...]
        yc_ref[...] = y_conv.astype(BF16)
        s_in = bg * y_conv * (g_conv * _sig(g_conv))
        sin_ref[...] = s_in.astype(BF16)
        s_out = _dot(s_in.astype(BF16), wco_ref[...])
        sout_ref[...] = s_out.astype(BF16)
        merged = _sig(m_attn) * a_out + _sig(m_conv) * s_out
        mrg_ref[...] = merged.astype(BF16)
        y = _dot(merged.astype(BF16), wo_ref[...])
        gate1 = 1.0 + gate_ref[0]
        r = ALPHA * x_ref[...] + gate1 * y
        mu = jnp.mean(r, axis=1, keepdims=True)
        rc = r - mu
        rstd = lax.rsqrt(jnp.mean(rc * rc, axis=1, keepdims=True) + LN_EPS)
        xhat = rc * rstd
        diff = (xhat * lng_ref[...] + lnb_ref[...]) - t_ref[...]
        dout = diff * (1.0 / D)
        vec_ref[0:1, :] += jnp.sum(dout * xhat, axis=0, keepdims=True)
        vec_ref[1:2, :] += jnp.sum(dout, axis=0, keepdims=True)
        vec_ref[2:3, :] += jnp.sum(diff * diff, axis=0, keepdims=True)
        dxh = dout * lng_ref[...]
        dr = rstd * (dxh - jnp.mean(dxh, axis=1, keepdims=True) - xhat * jnp.mean(dxh * xhat, axis=1, keepdims=True))
        dxr_ref[...] = ALPHA * dr
        dy_ref[...] = (dr * gate1).astype(BF16)
        dgate_ref[0] += jnp.sum(dr * y, axis=0, keepdims=True)

    tok = lambda w: pl.BlockSpec((tm, w), lambda b, i: (b * spt + i, 0))
    const = lambda s: pl.BlockSpec(s, lambda b, i: (0,) * len(s))
    per_seq = pl.BlockSpec((1, 1, D), lambda b, i: (b, 0, 0))
    outs = pl.pallas_call(
        body, name="mix_forward", grid=(bsz, spt),
        out_shape=[HBM_OUT((t, AW), BF16), HBM_OUT((t, D), BF16), HBM_OUT((t, D), BF16), HBM_OUT((t, D), BF16), HBM_OUT((t, D), BF16),
                   HBM_OUT((t, D), BF16), HBM_OUT((t, D), BF16), HBM_OUT((t, AW), F32), HBM_OUT((t, AW), F32), HBM_OUT((t, D), F32),
                   SDS((8, D), F32), SDS((bsz, 1, D), F32)],
        in_specs=[tok(NGATE)] + [tok(AW)] * 6 + [tok(D), tok(D), per_seq, const((AW, D)), const((D, D)), const((D, D)),
                                                 const((3, D)), const((1, D)), const((1, D)), const((1, D))],
        out_specs=[tok(AW), tok(D), tok(D), tok(D), tok(D), tok(D), tok(D), tok(AW), tok(AW), tok(D), const((8, D)), per_seq],
        scratch_shapes=[pltpu.VMEM((8, D), F32)],
        compiler_params=_cp(("arbitrary", "arbitrary"), VMEM_CAP),
    )(gates, *map(_in_hbm, og), *map(_in_hbm, lg), x2, tgt, gate, w_ao, w_co, w_o, conv_w, conv_b, ln_g, ln_b)
    return outs


def _mix_backward(gates, dy, a_out, s_out, y_conv, o, lj, w_ao, w_co, w_o, conv_w, vec_f, bsz, seq, tm=256):
    t = dy.shape[0]
    spt = seq // tm

    def body(g_ref, dy_ref, aout_ref, sout_ref, yc_ref, o_ref, lj_ref, wao_ref, wco_ref, wo_ref, cw_ref, vecf_ref,
             dg_ref, do_ref, dl_ref, daout_ref, dsout_ref, vec_ref, car_ref):
        b, i = pl.program_id(0), pl.program_id(1)

        @pl.when((b == 0) & (i == 0))
        def _():
            vec_ref[...] = vecf_ref[...]

        @pl.when(i == 0)
        def _():
            car_ref[...] = jnp.zeros_like(car_ref)

        g_attn, u, bg, cg, g_conv, m_attn, m_conv = (g_ref[:, lo:hi].astype(F32) for lo, hi in GATE_COLS)
        dmerged = _dot_nt(dy_ref[...], wo_ref[...])
        sa, sc = _sig(m_attn), _sig(m_conv)
        da_out = (dmerged * sa).astype(BF16)
        ds_out = (dmerged * sc).astype(BF16)
        daout_ref[...] = da_out
        dsout_ref[...] = ds_out
        dg_ref[:, 4608:5632] = (dmerged * aout_ref[...].astype(F32) * (sa * (1.0 - sa))).astype(BF16)
        dg_ref[:, 5632:6656] = (dmerged * sout_ref[...].astype(F32) * (sc * (1.0 - sc))).astype(BF16)
        da_in = _dot_nt(da_out, wao_ref[...])
        ds_in = _dot_nt(ds_out, wco_ref[...])
        sga = _sig(g_attn)
        o = o_ref[...]
        do = da_in * (g_attn * sga)
        do_ref[...] = do
        dg_ref[:, 0:512] = (da_in * o * (sga * (1.0 + g_attn * (1.0 - sga)))).astype(BF16)
        prod = do * o
        lane = lax.broadcasted_iota(jnp.int32, (tm, HD), 1)
        for j in range(4):
            cs = slice(j * HD, (j + 1) * HD)
            delta = jnp.sum(prod[:, cs], axis=1, keepdims=True)
            dl_ref[:, cs] = jnp.where(lane < 64, lj_ref[:, cs], delta)
        sgc = _sig(g_conv)
        silu_c = g_conv * sgc
        yc = yc_ref[...].astype(F32)
        dg_ref[:, 1536:2560] = (ds_in * yc * silu_c).astype(BF16)
        dg_ref[:, 3584:4608] = (ds_in * bg * yc * (sgc * (1.0 + g_conv * (1.0 - sgc)))).astype(BF16)
        dyc = ds_in * bg * silu_c
        rows = lax.broadcasted_iota(jnp.int32, (tm, D), 0)
        c0, c1 = car_ref[0:1, :], car_ref[1:2, :]
        n1 = jnp.where(rows == tm - 1, c0, pltpu.roll(dyc, tm - 1, 0))
        n2 = jnp.where(rows == tm - 2, c0, jnp.where(rows == tm - 1, c1, pltpu.roll(dyc, tm - 2, 0)))
        car_ref[...] = dyc[0:8, :]
        dz = cw_ref[2:3, :] * dyc + cw_ref[1:2, :] * n1 + cw_ref[0:1, :] * n2
        z = cg * u
        dg_ref[:, 512:1536] = (dz * cg).astype(BF16)
        dg_ref[:, 2560:3584] = (dz * u).astype(BF16)
        vec_ref[3:4, :] += jnp.sum(n2 * z, axis=0, keepdims=True)
        vec_ref[4:5, :] += jnp.sum(n1 * z, axis=0, keepdims=True)
        vec_ref[5:6, :] += jnp.sum(dyc * z, axis=0, keepdims=True)
        vec_ref[6:7, :] += jnp.sum(dyc, axis=0, keepdims=True)

    tok = lambda w: pl.BlockSpec((tm, w), lambda b, i: (b * spt + (spt - 1 - i), 0))
    const = lambda s: pl.BlockSpec(s, lambda b, i: (0,) * len(s))
    return pl.pallas_call(
        body, name="mix_backward", grid=(bsz, spt),
        out_shape=[HBM_OUT((t, NGATE), BF16), HBM_OUT((t, AW), F32), HBM_OUT((t, AW), F32), HBM_OUT((t, D), BF16), HBM_OUT((t, D), BF16),
                   SDS((8, D), F32)],
        in_specs=[tok(NGATE), tok(D), tok(D), tok(D), tok(D), tok(AW), tok(AW), const((AW, D)), const((D, D)), const((D, D)), const((3, D)),
                  const((8, D))],
        out_specs=[tok(NGATE), tok(AW), tok(AW), tok(D), tok(D), const((8, D))],
        scratch_shapes=[pltpu.VMEM((8, D), F32)],
        compiler_params=_cp(("arbitrary", "arbitrary"), VMEM_CAP),
    )(gates, dy, a_out, s_out, y_conv, o, lj, w_ao, w_co, w_o, conv_w, vec_f)


def _scatter_copies(src, land, send_sems, recv_sems):
    x, y, c = _place()
    chips = [(1 - x, y), (x, 1 - y), (1 - x, 1 - y)]
    return [pltpu.make_async_remote_copy(src_ref=src[a].at[2 * cx + cy], dst_ref=land[a].at[r], send_sem=send_sems.at[3 * a + r],
                                         recv_sem=recv_sems.at[3 * a + r], device_id=(cx, cy, c), device_id_type=MESH)
            for a in range(len(src)) for r, (cx, cy) in enumerate(chips)]


def _halves_out(a):
    kind, nr, nc = W_CUTS[a]
    shape = (nr // 2, W_FULL[a][1]) if kind == "col" else (NCHIP, nr // 2, nc)
    return [SDS(shape, F32), SDS(shape, BF16)]


def _write_halves(a, acc_ref, c, mine_ref, theirs_ref):
    kind, nr, nc = W_CUTS[a]
    hr = nr // 2
    if kind == "col":
        mine_ref[...] = acc_ref[pl.ds(pl.multiple_of(c * hr, hr), hr), :]
        theirs_ref[...] = acc_ref[pl.ds(pl.multiple_of((1 - c) * hr, hr), hr), :].astype(BF16)
    else:
        for k in range(NCHIP):
            mine_ref[k] = acc_ref[pl.ds(pl.multiple_of(k * nr + c * hr, hr), hr), :]
            theirs_ref[k] = acc_ref[pl.ds(pl.multiple_of(k * nr + (1 - c) * hr, hr), hr), :].astype(BF16)


def _out_weight_grads(a_in, da_out, s_in, ds_out, merged, dy, core, tk=512):
    t = dy.shape[0]
    nt = t // tk

    def body(c_ref, ain_ref, da_ref, sin_ref, ds_ref, m_ref, dy_ref, *rest):
        outs, (gao, gco, go) = rest[:6], rest[6:]

        @pl.when(pl.program_id(0) == 0)
        def _():
            gao[...] = jnp.zeros_like(gao)
            gco[...] = jnp.zeros_like(gco)
            go[...] = jnp.zeros_like(go)

        gao[...] += _dot_tn(ain_ref[...], da_ref[...])
        gco[...] += _dot_tn(sin_ref[...], ds_ref[...])
        go[...] += _dot_tn(m_ref[...], dy_ref[...])

        @pl.when(pl.program_id(0) == nt - 1)
        def _():
            for a, acc in ((1, gao), (2, gco), (3, go)):
                _write_halves(a, acc, c_ref[0], outs[2 * a - 2], outs[2 * a - 1])

    tok = lambda w: pl.BlockSpec((tk, w), lambda i, cr: (i, 0))
    out_shape = _halves_out(1) + _halves_out(2) + _halves_out(3)
    outs = pl.pallas_call(
        body, name="out_weight_grads", out_shape=out_shape,
        grid_spec=pltpu.PrefetchScalarGridSpec(
            num_scalar_prefetch=1, grid=(nt,), in_specs=[tok(AW), tok(D), tok(D), tok(D), tok(D), tok(D)],
            out_specs=[pl.BlockSpec(o.shape, lambda i, cr, nd=len(o.shape): (0,) * nd) for o in out_shape],
            scratch_shapes=[pltpu.VMEM((AW, D), F32), pltpu.VMEM((D, D), F32), pltpu.VMEM((D, D), F32)]),
        compiler_params=_cp(("arbitrary",), VMEM_CAP),
    )(core, a_in, da_out, s_in, ds_out, merged, dy)
    return [(outs[0], outs[1]), (outs[2], outs[3]), (outs[4], outs[5])]


def _input_grad(dq, dk, dv, dgates, w, x2, dxr, sc1p, seq, sums, tm=512):
    t = x2.shape[0]
    nt = t // tm
    spt = seq // tm
    bsz = t // seq
    n = len(sums)
    gblk = NGATE // 4
    nsteps = 3 + 4

    def body(dq_ref, dk_ref, dv_ref, dg_ref, wq_ref, wg_ref, x_ref, dxr_ref, sc_ref, *rest):
        src, (dx_ref, dsh_ref, dsc_ref), land = rest[:n], rest[n:n + 3], rest[n + 3:2 * n + 3]
        acc_ref, send_sems, recv_sems = rest[2 * n + 3:]
        j, i = pl.program_id(0), pl.program_id(1)
        copies = _scatter_copies(src, land, send_sems, recv_sems)
        rows = pl.ds(pl.multiple_of(i * tm, tm), tm)

        @pl.when((i == 0) & (j == 0))
        def _():
            for cp in copies:
                cp.start()

        for k, ref in enumerate((dq_ref, dk_ref, dv_ref)):
            @pl.when(j == k)
            def _(k=k, ref=ref):
                part = _dot_nt(ref[...], wq_ref[...])
                if k == 0:
                    acc_ref[rows, :] = part
                else:
                    acc_ref[rows, :] += part

        @pl.when((j >= 3) & (j < nsteps - 1))
        def _():
            acc_ref[rows, :] += _dot_nt(dg_ref[...], wg_ref[...])

        @pl.when(j == nsteps - 1)
        def _():
            dh = acc_ref[rows, :] + _dot_nt(dg_ref[...], wg_ref[...])
            dx_ref[...] = dh * sc_ref[0] + dxr_ref[...]

            @pl.when(i % spt == 0)
            def _():
                dsh_ref[...] = jnp.zeros_like(dsh_ref)
                dsc_ref[...] = jnp.zeros_like(dsc_ref)

            dsh_ref[0] += jnp.sum(dh, axis=0, keepdims=True)
            dsc_ref[0] += jnp.sum(dh * x_ref[...], axis=0, keepdims=True)

        @pl.when((i == nt - 1) & (j == nsteps - 1))
        def _():
            for cp in copies:
                cp.wait()

    def held(k):
        return lambda j, i: (jnp.where(j == k, i, jnp.where(j < k, 0, nt - 1)), 0)

    last = lambda j, i: (jnp.where(j == nsteps - 1, i, 0), 0)
    outs = pl.pallas_call(
        body, name="input_grad", grid=(nsteps, nt),
        out_shape=[SDS((t, D), F32), SDS((bsz, 1, D), F32), SDS((bsz, 1, D), F32)] + [SDS((3,) + s.shape[1:], BF16) for s in sums],
        in_specs=[pl.BlockSpec((tm, QW), held(0)), pl.BlockSpec((tm, QW), held(1)), pl.BlockSpec((tm, QW), held(2)),
                  pl.BlockSpec((tm, gblk), lambda j, i: (jnp.where(j >= 3, i, 0), jnp.clip(j - 3, 0, 3))),
                  pl.BlockSpec((D, QW), lambda j, i: (0, jnp.minimum(j, 2))),
                  pl.BlockSpec((pl.Element(D), pl.Element(gblk)), lambda j, i: (0, pl.multiple_of(3 * QW + gblk * jnp.clip(j - 3, 0, 3), 128))),
                  pl.BlockSpec((tm, D), last), pl.BlockSpec((tm, D), last),
                  pl.BlockSpec((1, 1, D), lambda j, i: (jnp.where(j == nsteps - 1, i // spt, 0), 0, 0))] + [ANY] * n,
        out_specs=[pl.BlockSpec((tm, D), last),
                   pl.BlockSpec((1, 1, D), lambda j, i: (jnp.where(j == nsteps - 1, i // spt, 0), 0, 0)),
                   pl.BlockSpec((1, 1, D), lambda j, i: (jnp.where(j == nsteps - 1, i // spt, 0), 0, 0))] + [ANY] * n,
        scratch_shapes=[pltpu.VMEM((t, D), F32), pltpu.SemaphoreType.DMA((3 * NCHIP,)), pltpu.SemaphoreType.DMA((3 * NCHIP,))],
        compiler_params=_cp(("arbitrary", "arbitrary"), VMEM_CAP, side=True),
    )(dq, dk, dv, dgates, w, w, x2, dxr, sc1p, *sums)
    return outs[0], outs[1], outs[2], outs[3:]


def _in_weight_grad(ht, dq, dk, dv, dgates, core, sums):
    t = ht.shape[1]
    hr = D // 2
    n = len(sums)

    def body(c_ref, ht_ref, dq_ref, dk_ref, dv_ref, dg_ref, *rest):
        src, (mine_ref, theirs_ref), land = rest[:n], rest[n:n + 2], rest[n + 2:2 * n + 2]
        acc_ref, send_sems, recv_sems = rest[2 * n + 2:]
        j = pl.program_id(0)
        copies = _scatter_copies(src, land, send_sems, recv_sems)

        @pl.when(j == 0)
        def _():
            for cp in copies:
                cp.start()

        for k, ref in enumerate((dq_ref, dk_ref, dv_ref)):
            @pl.when((j >= k * NQT) & (j < (k + 1) * NQT))
            def _(ref=ref):
                acc_ref[...] = _dot(ht_ref[...], ref[...])

        @pl.when(j >= 3 * NQT)
        def _():
            acc_ref[...] = _dot(ht_ref[...], dg_ref[...])

        _write_halves(0, acc_ref, c_ref[0], mine_ref, theirs_ref)

        @pl.when(j == NPT - 1)
        def _():
            for cp in copies:
                cp.wait()

    def part(k):
        return pl.BlockSpec((t, TN), lambda j, cr: (0, jnp.clip(j - k * NQT, 0, NQT - 1)))

    out_spec = pl.BlockSpec((hr, TN), lambda j, cr: (0, j))
    outs = pl.pallas_call(
        body, name="in_weight_grad", out_shape=[SDS((hr, NCOL), F32), SDS((hr, NCOL), BF16)] + [SDS((3,) + v.shape[1:], BF16) for v in sums],
        grid_spec=pltpu.PrefetchScalarGridSpec(
            num_scalar_prefetch=1, grid=(NPT,),
            in_specs=[pl.BlockSpec((D, t), lambda j, cr: (0, 0)), part(0), part(1), part(2),
                      pl.BlockSpec((t, TN), lambda j, cr: (0, jnp.maximum(j - 3 * NQT, 0)))] + [ANY] * n,
            out_specs=[out_spec, out_spec] + [ANY] * n,
            scratch_shapes=[pltpu.VMEM((D, TN), F32), pltpu.SemaphoreType.DMA((3 * NCHIP,)), pltpu.SemaphoreType.DMA((3 * NCHIP,))]),
        compiler_params=_cp(("arbitrary",), VMEM_CAP, side=True),
    )(core, ht, dq, dk, dv, dgates, *sums)
    return outs[0], outs[1], outs[2:]


def _sum_partials(gathered):
    def body(g_ref, o_ref):
        acc = g_ref[0]
        for k in range(1, 8):
            acc = acc + g_ref[k]
        o_ref[...] = acc

    return pl.pallas_call(body, name="sum_partials", out_shape=SDS(gathered.shape[1:], F32), in_specs=[VMEM_SPEC], out_specs=VMEM_SPEC)(gathered)


def _adamw(w, g, m, v, name, tr=256):
    r, cdim = w.shape
    tr = tr if cdim <= D else tr // 2
    tr = tr if (r % tr == 0 and r > tr) else r

    def body(w_ref, g_ref, m_ref, v_ref, go_ref, d_ref, nm_ref, nv_ref):
        gv = g_ref[...]
        go_ref[...] = gv
        nm = B1 * m_ref[...] + (1.0 - B1) * gv
        nv = B2 * v_ref[...] + (1.0 - B2) * (gv * gv)
        m_hat = nm / (1.0 - B1 ** STEP)
        v_hat = nv / (1.0 - B2 ** STEP)
        d_ref[...] = -LR * (m_hat / (jnp.sqrt(v_hat) + EPS) + WD * w_ref[...])
        nm_ref[...] = nm
        nv_ref[...] = nv

    spec = pl.BlockSpec((tr, cdim), lambda i: (i, 0))
    return pl.pallas_call(
        body, name=name, grid=(r // tr,), out_shape=[SDS((r, cdim), F32)] * 4, in_specs=[spec] * 4, out_specs=[spec] * 4,
        compiler_params=_cp(("parallel",), VMEM_CAP // 2),
    )(w, g, m, v)


def _t5_bucket(dist):
    n = jnp.maximum(dist, 1).astype(F32)
    large = MAX_EXACT + (jnp.log(n / MAX_EXACT) / math.log(MAX_DISTANCE / MAX_EXACT) * (N_BUCKETS - MAX_EXACT)).astype(jnp.int32)
    large = jnp.minimum(large, N_BUCKETS - 1)
    return jnp.where(dist < MAX_EXACT, dist, large)


def _band_buckets():
    a = jnp.arange(BLK)[:, None]
    b = jnp.arange(2 * BLK)[None, :]
    steps = jnp.maximum(a + BLK - b, 0)
    return jnp.stack([_t5_bucket(steps * d) for d in DILATIONS]).astype(jnp.int32)


def _pad_rows(a, rows=8):
    return jnp.pad(a, ((0, rows - a.shape[0]), (0, 0)))


def kernel(x, c, w_ada, b_ada, w_in, conv_w, conv_b, rel_bias, w_attn_out, w_conv_out, w_o, ln_g, ln_b, loss_target, m_w_ada, m_b_ada, m_w_in, m_conv_w, m_conv_b, m_rel_bias, m_w_attn_out, m_w_conv_out, m_w_o, m_ln_g, m_ln_b, v_w_ada, v_b_ada, v_w_in, v_conv_w, v_conv_b, v_rel_bias, v_w_attn_out, v_w_conv_out, v_w_o, v_ln_g, v_ln_b):
    bsz, seq, _ = x.shape
    t = bsz * seq
    mx, my, mc = _place()
    chip = 2 * mx + my
    dev = 4 * mx + 2 * my + mc
    x2 = x.reshape(t, D)
    tgt = loss_target.reshape(t, D)

    mine = _to_bf16_windows([w[0] for w in (w_in, w_attn_out, w_conv_out, w_o)])

    n_ada = w_ada.shape[2]
    n_cw = conv_w.shape[2]
    c_and_cw = jnp.concatenate([_pad_rows(c), jnp.pad(conv_w[0], ((0, 5), (0, D - n_cw)))], axis=0)
    firsts = _all_gather8(c_and_cw, "gather_c_conv_w")
    c_all = firsts[:, 0:bsz, :].reshape(8 * bsz, D)
    conv_w_f = firsts[0::2, 8:11, 0:n_cw].transpose(1, 0, 2).reshape(3, D)
    b_cols = lax.dynamic_slice(b_ada, (0, chip * n_ada), (1, n_ada))
    mod_part = _ada_forward(c_all, w_ada[0], b_cols)
    mod_parts = _all_gather8(mod_part, "gather_mod")
    mod_all = mod_parts[0::2].transpose(1, 0, 2).reshape(8 * bsz, 3 * D)
    mod = lax.dynamic_slice(mod_all, (dev * bsz, 0), (bsz, 3 * D))
    shift = mod[:, 0:D].reshape(bsz, 1, D)
    sc1p = 1.0 + mod[:, D:2 * D].reshape(bsz, 1, D)
    gate = mod[:, 2 * D:].reshape(bsz, 1, D)

    h, ht = _modulate(x2, sc1p, shift, seq)
    tab = lax.dynamic_index_in_dim(jnp.asarray(_tile_tables()), chip, 0, keepdims=False)
    qkv, gates, (w_in_f, w_ao_f, w_co_f, w_o_f) = _project_gather(h, mine, tab)
    buckets = _band_buckets()
    bias = _bias_tables(rel_bias, buckets)
    og, lg = [], []
    for g in range(3):
        o_g, l_g = _attn_forward(g, qkv, bias[g], bsz, seq)
        og.append(o_g)
        lg.append(l_g)
    (a_in, s_in, merged, dy, a_out, s_out, y_conv, o, lj, dxr, vec_f, dgate) = _mix_forward(
        gates, og, lg, x2, tgt, gate, w_ao_f, w_co_f, w_o_f, conv_w_f, conv_b, ln_g, ln_b, bsz, seq)

    dgates, do, dl, da_out, ds_out, vec = _mix_backward(gates, dy, a_out, s_out, y_conv, o, lj, w_ao_f, w_co_f, w_o_f, conv_w_f, vec_f, bsz, seq)
    core = jnp.reshape(mc, (1,)).astype(jnp.int32)
    small_grads = _out_weight_grads(a_in, da_out, s_in, ds_out, merged, dy, core)
    got_small = _swap_halves([theirs for _, theirs in small_grads], "swap_small_grad_halves")
    sums_small = _chip_sums([own for own, _ in small_grads], got_small, 1, "chip_sums_small")
    dqkv, dbs = None, []
    for g in range(3):
        dqkv, db = _attn_backward(g, qkv, do, dl, bias[g], dqkv, bsz, seq)
        dbs.append(db)
    dq, dk, dv = dqkv
    drb = _bias_grad(jnp.stack(dbs), buckets)
    drb = drb[:, :, 0:4].transpose(1, 0, 2).reshape(N_BUCKETS, 12)
    g_in_mine, g_in_theirs, landed_small = _in_weight_grad(ht, dq, dk, dv, dgates, core, [bf for _, bf in sums_small])
    got_in = _swap_halves([g_in_theirs], "swap_in_grad_halves")
    sums_in = _chip_sums([g_in_mine], got_in, 0, "chip_sums_in")
    grad_x, dshift, dscale, landed_in = _input_grad(dq, dk, dv, dgates, w_in_f, x2, dxr, sc1p, seq, [bf for _, bf in sums_in])
    halves = _reduce_mine([own for own, _ in sums_in + sums_small], list(landed_in) + list(landed_small))
    gw_in, gw_ao, gw_co, gw_o = _join_halves(halves)

    dmod = jnp.concatenate([dshift, dscale, dgate], axis=2).reshape(bsz * 3, D)
    drb_row = jnp.pad(drb.reshape(1, N_BUCKETS * 12), ((0, 0), (0, D - N_BUCKETS * 12)))
    vec = lax.dynamic_update_slice(vec, drb_row, (7, 0))
    packed = jnp.concatenate([vec, _pad_rows(dmod)], axis=0)
    gathered = _all_gather8(packed, "gather_small")
    small = _sum_partials(gathered)
    g_ln_g, g_ln_b, loss_lanes = small[0:1], small[1:2], small[2:3]
    g_conv_w_full, g_conv_b = small[3:6], small[6:7]
    g_rel_bias = small[7, 0:N_BUCKETS * 12].reshape(N_BUCKETS, 12)
    loss = 0.5 / D * jnp.sum(loss_lanes)
    dmod_all = gathered[:, 8:8 + 3 * bsz, :].reshape(8 * bsz, 3 * D)
    dmod_cols = lax.dynamic_slice(dmod_all, (0, chip * n_ada), (8 * bsz, n_ada))
    gw_ada, gb_ada = _ada_backward(c_all, dmod_cols, dmod_all)
    g_conv_w = lax.dynamic_slice(g_conv_w_full, (0, chip * n_cw), (3, n_cw))

    names = ["w_ada", "b_ada", "w_in", "conv_w", "conv_b", "rel_bias", "w_attn_out", "w_conv_out", "w_o", "ln_g", "ln_b"]
    two_d = lambda a: a.reshape(a.shape[-2:]) if a.ndim == 3 else a
    weights = dict(zip(names, map(two_d, (w_ada, b_ada, w_in, conv_w, conv_b, rel_bias, w_attn_out, w_conv_out, w_o, ln_g, ln_b))))
    ms = dict(zip(names, map(two_d, (m_w_ada, m_b_ada, m_w_in, m_conv_w, m_conv_b, m_rel_bias, m_w_attn_out, m_w_conv_out, m_w_o, m_ln_g, m_ln_b))))
    vs = dict(zip(names, map(two_d, (v_w_ada, v_b_ada, v_w_in, v_conv_w, v_conv_b, v_rel_bias, v_w_attn_out, v_w_conv_out, v_w_o, v_ln_g, v_ln_b))))
    grads = dict(zip(names, (gw_ada, gb_ada, gw_in, g_conv_w, g_conv_b, g_rel_bias, gw_ao, gw_co, gw_o, g_ln_g, g_ln_b)))
    shapes = dict(zip(names, (w_ada, b_ada, w_in, conv_w, conv_b, rel_bias, w_attn_out, w_conv_out, w_o, ln_g, ln_b)))
    grad_out, deltas, new_m, new_v = {}, {}, {}, {}
    for n in names:
        grad_out[n], deltas[n], new_m[n], new_v[n] = _adamw(weights[n], grads[n], ms[n], vs[n], f"adamw_{n}")
    shaped = lambda d: [d[n].reshape(shapes[n].shape) for n in names]
    return (loss, grad_x.reshape(bsz, seq, D), *shaped(grad_out), *shaped(deltas), *shaped(new_m), *shaped(new_v))
```

```python
import math

import numpy as np
import jax
import jax.numpy as jnp
from jax import lax
from jax.experimental import pallas as pl
from jax.experimental.pallas import tpu as pltpu

F32 = jnp.float32
BF16 = jnp.bfloat16
SDS = jax.ShapeDtypeStruct
MESH = pl.DeviceIdType.MESH
HBM_OUT = pltpu.HBM
ANY = pl.BlockSpec(memory_space=pl.ANY)
VMEM_SPEC = pl.BlockSpec(memory_space=pltpu.VMEM)

D = 1024
HD = 128
BLK = 128
QW = 1536
AW = 512
NGATE = 6656
GATE_COLS = ((0, 512), (512, 1536), (1536, 2560), (2560, 3584), (3584, 4608), (4608, 5632), (5632, 6656))
NCOL = 3 * QW + NGATE
TN = 512
NQT = QW // TN
NPT = NCOL // TN
DILATIONS = (1, 4, 16)
N_BUCKETS, MAX_EXACT, MAX_DISTANCE = 32, 16, 2048
ALPHA = 2.0 ** 0.25
LN_EPS = 1e-5
NEG = -1e30
SCALE = HD ** -0.5
LR, B1, B2, EPS, WD, STEP = 0.001, 0.9, 0.999, 1e-08, 0.01, 10
NCHIP = 4
VMEM_CAP = 60 * 2 ** 20


def _cp(sem=None, vmem=None, side=False):
    return pltpu.CompilerParams(dimension_semantics=sem, vmem_limit_bytes=vmem, has_side_effects=side)


def _dot(a, b):
    return jnp.dot(a, b, preferred_element_type=F32)


def _dot_nt(a, b):
    return lax.dot_general(a, b, (((1,), (1,)), ((), ())), preferred_element_type=F32)


def _dot_tn(a, b):
    return lax.dot_general(a, b, (((0,), (0,)), ((), ())), preferred_element_type=F32)


def _sig(x):
    return 1.0 / (1.0 + jnp.exp(-x))


def _in_hbm(a):
    return pltpu.with_memory_space_constraint(a, pltpu.HBM)


def _place():
    x, y, c = lax.axis_index("x"), lax.axis_index("y"), lax.axis_index("c")
    return x, y, c


def _all_gather8(v, name):
    r, cdim = v.shape

    def body(v_ref, out_ref, send_sems, recv_sems, local_sem):
        x, y, c = _place()
        me = 4 * x + 2 * y + c
        peers = [(x, y, 1 - c), (1 - x, y, c), (x, 1 - y, c), (1 - x, 1 - y, c),
                 (1 - x, y, 1 - c), (x, 1 - y, 1 - c), (1 - x, 1 - y, 1 - c)]
        mine = pltpu.make_async_copy(v_ref, out_ref.at[me], local_sem)
        mine.start()

        def copy(k, block, to):
            return pltpu.make_async_remote_copy(src_ref=v_ref, dst_ref=out_ref.at[block], send_sem=send_sems.at[k],
                                                recv_sem=recv_sems.at[k], device_id=to, device_id_type=MESH)

        sends = [copy(k, me, p) for k, p in enumerate(peers)]
        for cp in sends:
            cp.start()
        for k, (px, py, pc) in enumerate(peers):
            copy(k, 4 * px + 2 * py + pc, (px, py, pc)).wait_recv()
        for cp in sends:
            cp.wait_send()
        mine.wait()

    return pl.pallas_call(
        body, name=name, out_shape=SDS((8, r, cdim), v.dtype), in_specs=[VMEM_SPEC], out_specs=VMEM_SPEC,
        scratch_shapes=[pltpu.SemaphoreType.DMA((7,)), pltpu.SemaphoreType.DMA((7,)), pltpu.SemaphoreType.DMA(())],
        compiler_params=_cp(side=True),
    )(v)


W_CUTS = (("col", D, NCOL // NCHIP), ("col", AW, D // NCHIP), ("row", D // NCHIP, D), ("row", D // NCHIP, D))
W_FULL = ((D, NCOL), (AW, D), (D, D), (D, D))


def _shard_window(ref, cut, k, half):
    kind, nr, nc = cut
    hr = nr // 2
    if kind == "col":
        rows = pl.ds(0, nr) if half is None else pl.ds(pl.multiple_of(half * hr, 16), hr)
        return ref.at[rows, pl.ds(pl.multiple_of(k * nc, 128), nc)]
    if half is None:
        return ref.at[pl.ds(pl.multiple_of(k * nr, 16), nr), :]
    return ref.at[pl.ds(pl.multiple_of(k * nr + half * hr, 16), hr), :]


def _half_rows(ref, cut, half):
    hr = cut[1] // 2
    return ref.at[pl.ds(pl.multiple_of(half * hr, 16), hr), :]


def _to_bf16_windows(ws):
    x, y, _ = _place()
    chip = jnp.reshape(2 * x + y, (1,)).astype(jnp.int32)
    tr = 256
    n = len(ws)

    def body(c_ref, *refs):
        src, dst = refs[:n], refs[n:]
        dst[0][...] = src[0][...].astype(BF16)

        @pl.when(pl.program_id(0) == 0)
        def _():
            for a in range(1, n):
                dst[a][...] = src[a][...].astype(BF16)

    in_specs = [pl.BlockSpec((tr, W_CUTS[0][2]), lambda i, cr: (i, 0))]
    out_specs = [pl.BlockSpec((tr, W_CUTS[0][2]), lambda i, cr: (i, cr[0]))]
    for a in range(1, n):
        kind, nr, nc = W_CUTS[a]
        in_specs.append(pl.BlockSpec((nr, nc), lambda i, cr: (0, 0)))
        out_specs.append(pl.BlockSpec((nr, nc), (lambda i, cr: (0, cr[0])) if kind == "col" else (lambda i, cr: (cr[0], 0))))
    return pl.pallas_call(
        body, name="to_bf16", out_shape=[SDS(W_FULL[a], BF16) for a in range(n)],
        grid_spec=pltpu.PrefetchScalarGridSpec(num_scalar_prefetch=1, grid=(D // tr,), in_specs=in_specs, out_specs=out_specs),
        compiler_params=_cp(("arbitrary",)),
    )(chip, *ws)


def _swap_halves(theirs, name):
    n = len(theirs)

    def body(*refs):
        src, land = refs[:n], refs[n:2 * n]
        send_sems, recv_sems = refs[2 * n:]
        x, y, c = _place()
        copies = [pltpu.make_async_remote_copy(src_ref=src[a], dst_ref=land[a], send_sem=send_sems.at[a], recv_sem=recv_sems.at[a],
                                               device_id=(x, y, 1 - c), device_id_type=MESH) for a in range(n)]
        for cp in copies:
            cp.start()
        for cp in copies:
            cp.wait()

    return pl.pallas_call(
        body, name=name, out_shape=[SDS(v.shape, v.dtype) for v in theirs], in_specs=[ANY] * n, out_specs=[ANY] * n,
        scratch_shapes=[pltpu.SemaphoreType.DMA((n,)), pltpu.SemaphoreType.DMA((n,))],
        compiler_params=_cp(side=True),
    )(*theirs)


def _chip_sums(mines, gots, first, name):
    n = len(mines)
    x, y, _ = _place()
    me = jnp.reshape(2 * x + y, (1,)).astype(jnp.int32)

    def body(me_ref, *refs):
        ins, outs = refs[:2 * n], refs[2 * n:]
        for a in range(n):
            hr, nc = W_CUTS[first + a][1] // 2, W_CUTS[first + a][2]
            s = (ins[2 * a][...] + ins[2 * a + 1][...].astype(F32)).reshape(hr, nc)
            outs[2 * a + 1][0] = s.astype(BF16)

            @pl.when(pl.program_id(0) == me_ref[0])
            def _(a=a, s=s):
                outs[2 * a][...] = s

    in_specs, out_specs, out_shape = [], [], []
    for a in range(n):
        kind, nr, nc = W_CUTS[first + a]
        hr = nr // 2
        spec = pl.BlockSpec((hr, nc), lambda k, mr: (0, k)) if kind == "col" else pl.BlockSpec((1, hr, nc), lambda k, mr: (k, 0, 0))
        in_specs += [spec, spec]
        out_specs += [pl.BlockSpec((hr, nc), lambda k, mr: (0, 0)), pl.BlockSpec((1, hr, nc), lambda k, mr: (k, 0, 0))]
        out_shape += [SDS((hr, nc), F32), SDS((NCHIP, hr, nc), BF16)]
    outs = pl.pallas_call(
        body, name=name, out_shape=out_shape,
        grid_spec=pltpu.PrefetchScalarGridSpec(num_scalar_prefetch=1, grid=(NCHIP,), in_specs=in_specs, out_specs=out_specs),
        compiler_params=_cp(("arbitrary",), VMEM_CAP),
    )(me, *[v for pair in zip(mines, gots) for v in pair])
    return [(outs[2 * a], outs[2 * a + 1]) for a in range(n)]


def _reduce_mine(mines, gots):
    n = len(mines)
    _, _, c = _place()
    core = jnp.reshape(c, (1,)).astype(jnp.int32)
    tr = 256
    nsteps = W_CUTS[0][1] // 2 // tr

    def body(c_ref, *refs):
        ins, outs = refs[:2 * n], refs[2 * n:]

        def add(a):
            m_ref, g_ref = ins[2 * a], ins[2 * a + 1]
            outs[a][...] = ((m_ref[...] + g_ref[0].astype(F32)) + g_ref[1].astype(F32)) + g_ref[2].astype(F32)

        add(0)

        @pl.when(pl.program_id(0) == 0)
        def _():
            for a in range(1, n):
                add(a)

    nc0 = W_CUTS[0][2]
    in_specs = [pl.BlockSpec((tr, nc0), lambda i, cr: (i, 0)), pl.BlockSpec((3, tr, nc0), lambda i, cr: (0, i, 0))]
    out_specs = [pl.BlockSpec((tr, nc0), lambda i, cr: (cr[0] * nsteps + i, 0))]
    for a in range(1, n):
        hr, nc = W_CUTS[a][1] // 2, W_CUTS[a][2]
        in_specs += [pl.BlockSpec((hr, nc), lambda i, cr: (0, 0)), pl.BlockSpec((3, hr, nc), lambda i, cr: (0, 0, 0))]
        out_specs.append(pl.BlockSpec((hr, nc), lambda i, cr: (cr[0], 0)))
    return pl.pallas_call(
        body, name="reduce_mine", out_shape=[SDS((W_CUTS[a][1], W_CUTS[a][2]), F32) for a in range(n)],
        grid_spec=pltpu.PrefetchScalarGridSpec(num_scalar_prefetch=1, grid=(nsteps,), in_specs=in_specs, out_specs=out_specs),
        compiler_params=_cp(("arbitrary",), VMEM_CAP),
    )(core, *[v for pair in zip(mines, gots) for v in pair])


def _join_halves(fulls):
    n = len(fulls)

    def body(*refs):
        full = refs[n:2 * n]
        send_sems, recv_sems = refs[2 * n:]
        x, y, c = _place()
        sibling = (x, y, 1 - c)

        def swap(a, half):
            rows = _half_rows(full[a], W_CUTS[a], half)
            return pltpu.make_async_remote_copy(src_ref=rows, dst_ref=rows, send_sem=send_sems.at[a], recv_sem=recv_sems.at[a],
                                                device_id=sibling, device_id_type=MESH)

        sends = [swap(a, c) for a in range(n)]
        for cp in sends:
            cp.start()
        for a, cp in enumerate(sends):
            cp.wait_send()
            swap(a, 1 - c).wait_recv()

    return pl.pallas_call(
        body, name="join_grad_halves", out_shape=[SDS((W_CUTS[a][1], W_CUTS[a][2]), F32) for a in range(n)],
        in_specs=[ANY] * n, out_specs=[ANY] * n,
        scratch_shapes=[pltpu.SemaphoreType.DMA((n,)), pltpu.SemaphoreType.DMA((n,))],
        input_output_aliases={a: a for a in range(n)}, compiler_params=_cp(side=True),
    )(*fulls)


def _ada_forward(c_all, w_ada, b_cols):
    nb, nc = c_all.shape[0], w_ada.shape[1]

    def body(c_ref, w_ref, b_ref, o_ref):
        cv = c_ref[...]
        sc = (cv * _sig(cv)).astype(BF16)
        o_ref[...] = _dot(sc, w_ref[...].astype(BF16)) + b_ref[...]

    return pl.pallas_call(body, name="ada_forward", out_shape=SDS((nb, nc), F32), compiler_params=_cp(vmem=VMEM_CAP // 2))(c_all, w_ada, b_cols)


def _ada_backward(c_all, dmod_cols, dmod_all):
    nb, nc = dmod_cols.shape

    def body(c_ref, d_ref, a_ref, gw_ref, gb_ref):
        cv = c_ref[...]
        sc = (cv * _sig(cv)).astype(BF16)
        gw_ref[...] = _dot_tn(sc, d_ref[...].astype(BF16))
        gb_ref[...] = jnp.sum(a_ref[...], axis=0, keepdims=True)

    return pl.pallas_call(body, name="ada_backward", out_shape=[SDS((D, nc), F32), SDS((1, dmod_all.shape[1]), F32)],
                          compiler_params=_cp(vmem=VMEM_CAP // 2))(c_all, dmod_cols, dmod_all)


def _modulate(x2, sc1p, shift, seq, tm=256):
    t = x2.shape[0]
    spt = seq // tm

    def body(x_ref, sc_ref, sh_ref, h_ref, ht_ref):
        h = x_ref[...] * sc_ref[0] + sh_ref[0]
        h_ref[...] = h.astype(BF16)
        ht_ref[...] = h.T.astype(BF16)

    per_seq = pl.BlockSpec((1, 1, D), lambda i: (i // spt, 0, 0))
    return pl.pallas_call(
        body, name="modulate", out_shape=[HBM_OUT((t, D), BF16), HBM_OUT((D, t), BF16)], grid=(t // tm,),
        in_specs=[pl.BlockSpec((tm, D), lambda i: (i, 0)), per_seq, per_seq],
        out_specs=[pl.BlockSpec((tm, D), lambda i: (i, 0)), pl.BlockSpec((D, tm), lambda i: (0, i))],
        compiler_params=_cp(("parallel",)),
    )(x2, sc1p, shift)


TW = 256
TPS = NCOL // NCHIP // TW
NT = NCOL // TW
NQKV_T = 3 * QW // TW
N_TILE_SEMS = 2 * 3 * TPS


def _tile_tables():
    tabs = np.zeros((NCHIP, 3, NT), np.int32)
    for me in range(NCHIP):
        tiles = [TPS * (me ^ (s // TPS)) + s % TPS for s in range(NT)]
        tabs[me, 0] = tiles
        for row, (lo, hi) in enumerate(((0, NQKV_T), (NQKV_T, NT))):
            mine = [w - lo if lo <= w < hi else None for w in tiles]
            held = next(m for m in mine if m is not None)
            for s, m in enumerate(mine):
                held = held if m is None else m
                tabs[me, 1 + row, s] = held
    return tabs


def _project_gather(h, fulls, tab):
    t = h.shape[0]
    n = len(fulls)

    def body(tab_ref, h_ref, *rest):
        qkv_ref, g_ref = rest[n], rest[n + 1]
        full = rest[n + 2:2 * n + 2]
        w_buf, tile_sems, send_sems, recv_sems = rest[2 * n + 2:]
        s = pl.program_id(0)
        x, y, c = _place()
        me = 2 * x + y
        peers = [(x, 1 - y), (1 - x, y), (1 - x, 1 - y)]
        sibling = (x, y, 1 - c)

        def hop(a, r, stage, chip, half, to):
            window = _shard_window(full[a], W_CUTS[a], chip, half)
            k = N_TILE_SEMS + 6 * (a - 1) + 2 * r + stage
            return pltpu.make_async_remote_copy(src_ref=window, dst_ref=window, send_sem=send_sems.at[k], recv_sem=recv_sems.at[k],
                                                device_id=to, device_id_type=MESH)

        def tile_hop(q, stage, col_step, half, to):
            col = pl.multiple_of(tab_ref[0, col_step] * TW, TW)
            window = full[0].at[pl.ds(pl.multiple_of(half * (D // 2), 16), D // 2), pl.ds(col, TW)]
            k = 2 * (q - TPS) + stage
            return pltpu.make_async_remote_copy(src_ref=window, dst_ref=window, send_sem=send_sems.at[k], recv_sem=recv_sems.at[k],
                                                device_id=to, device_id_type=MESH)

        def send_tile(r, j):
            return tile_hop(TPS * (r + 1) + j, 0, j, c, (*peers[r], c))

        def pass_on(q, to):
            return tile_hop(3 * TPS + q % TPS, 0, q, c, to)

        def arrive(a, r):
            px, py = peers[r]
            chip = 2 * px + py
            hop(a, r, 0, chip, c, (px, py, c)).wait_recv()
            hop(a, r, 1, chip, c, sibling).start()
            hop(a, r, 1, chip, 1 - c, sibling).wait_recv()

        def tile(step, slot):
            col = pl.multiple_of(tab_ref[0, step] * TW, TW)
            return pltpu.make_async_copy(full[0].at[:, pl.ds(col, TW)], w_buf.at[slot], tile_sems.at[slot])

        @pl.when(s == 0)
        def _():
            for r in range(2):
                for j in range(TPS):
                    send_tile(r, j).start()
            tile(0, 0).start()

        @pl.when((s + 1 >= TPS) & (s + 1 < NT))
        def _():
            tile_hop(s + 1, 1, s + 1, 1 - c, sibling).wait_recv()

        @pl.when(s + 1 < NT)
        def _():
            tile(s + 1, 1 - (s % 2)).start()

        @pl.when((s + 2 >= TPS) & (s + 2 < NT))
        def _():
            tile_hop(s + 2, 0, s + 2, c, sibling).wait_recv()
            tile_hop(s + 2, 1, s + 2, c, sibling).start()

        for r in range(2):
            @pl.when(((s + 2) // TPS == r + 1) & ((s + 2) % 2 == (r + 1 + TPS * (r + 1)) % 2))
            def _(r=r):
                pass_on(s + 2, (*peers[1 - r], c)).start()

        @pl.when(s + 2 == 2 * TPS - 1)
        def _():
            for a in range(1, n):
                for r in range(3):
                    hop(a, r, 0, me, c, (*peers[r], c)).start()

        slot = s % 2
        tile(s, slot).wait()
        is_qkv = tab_ref[0, s] < NQKV_T
        for k in range(2):
            @pl.when(slot == k)
            def _(k=k):
                acc = _dot(h_ref[...], w_buf[k])

                @pl.when(is_qkv)
                def _():
                    qkv_ref[...] = acc.astype(BF16)

                @pl.when(jnp.logical_not(is_qkv))
                def _():
                    g_ref[...] = acc.astype(BF16)

        @pl.when(s == NT - 1)
        def _():
            for a in range(1, n):
                for r in range(3):
                    arrive(a, r)
            for r in range(3):
                for j in range(TPS):
                    send_tile(r, j).wait_send()
                    tile_hop(TPS * (r + 1) + j, 1, TPS * (r + 1) + j, c, sibling).wait_send()
                for a in range(1, n):
                    hop(a, r, 0, me, c, (*peers[r], c)).wait_send()
                    px, py = peers[r]
                    hop(a, r, 1, 2 * px + py, c, sibling).wait_send()

    n_sems = N_TILE_SEMS + 6 * (n - 1)
    outs = pl.pallas_call(
        body, name="project_gather", out_shape=[HBM_OUT((t, 3 * QW), BF16), HBM_OUT((t, NGATE), BF16)] + [SDS(s, BF16) for s in W_FULL],
        grid_spec=pltpu.PrefetchScalarGridSpec(
            num_scalar_prefetch=1, grid=(NT,),
            in_specs=[pl.BlockSpec((t, D), lambda s, tab: (0, 0))] + [ANY] * n,
            out_specs=[pl.BlockSpec((t, TW), lambda s, tab: (0, tab[1, s])), pl.BlockSpec((t, TW), lambda s, tab: (0, tab[2, s]))] + [ANY] * n,
            scratch_shapes=[pltpu.VMEM((2, D, TW), BF16), pltpu.SemaphoreType.DMA((2,)),
                            pltpu.SemaphoreType.DMA((n_sems,)), pltpu.SemaphoreType.DMA((n_sems,))]),
        input_output_aliases={2 + a: 2 + a for a in range(n)},
        compiler_params=_cp(("arbitrary",), VMEM_CAP, side=True),
    )(tab, _in_hbm(h), *fulls)
    return outs[0], outs[1], outs[2:]


def _bias_tables(rel_bias, buckets):
    def body(tab_ref, bk_ref, o_ref):
        a = lax.broadcasted_iota(jnp.int32, (BLK, 2 * BLK), 0)
        b = lax.broadcasted_iota(jnp.int32, (BLK, 2 * BLK), 1)
        steps = a + BLK - b
        valid = (steps >= 0) & (steps <= BLK)
        for g in range(3):
            bk = bk_ref[g]
            for j in range(4):
                def pick(kk, acc, bk=bk, col=4 * g + j):
                    return jnp.where(bk == kk, tab_ref[kk, col], acc)

                acc = lax.fori_loop(0, N_BUCKETS, pick, jnp.zeros((BLK, 2 * BLK), F32))
                o_ref[g, j] = jnp.where(valid, acc, NEG)

    return pl.pallas_call(
        body, name="bias_tables", out_shape=SDS((3, 4, BLK, 2 * BLK), F32),
        in_specs=[pl.BlockSpec(memory_space=pltpu.SMEM), VMEM_SPEC], out_specs=VMEM_SPEC,
    )(rel_bias, buckets)


def _bias_grad(ds_sum, buckets):
    def body(ds_ref, bk_ref, o_ref):
        lane = lax.broadcasted_iota(jnp.int32, (1, 128), 1)
        for g in range(3):
            def bucket(kk, carry, g=g):
                row = jnp.zeros((1, 128), F32)
                for j in range(4):
                    v = jnp.where(bk_ref[g] == kk, ds_ref[g, j], 0.0)
                    v = jnp.sum(v.reshape(BLK // 8, 8, 2 * BLK), axis=0)
                    s = jnp.sum(jnp.sum(v, axis=1, keepdims=True), axis=0, keepdims=True)
                    row = jnp.where(lane == j, s, row)
                o_ref[g, pl.ds(kk, 1), :] = row
                return carry

            lax.fori_loop(0, N_BUCKETS, bucket, 0)

    return pl.pallas_call(body, name="bias_grad", out_shape=SDS((3, N_BUCKETS, 128), F32), in_specs=[VMEM_SPEC, VMEM_SPEC],
                          out_specs=VMEM_SPEC)(ds_sum, buckets)


def _sub_rows(d, r, first, size):
    return pl.ds(first * d + r, size) if d == 1 else pl.ds(first * d + r, size, stride=d)


def _head_spec(seq, g, part):
    return pl.BlockSpec((seq, HD), lambda b, hh: (b, part * (QW // HD) + 4 * g + hh))


def _rows(start, count, stride):
    return pl.ds(start, count) if stride == 1 else pl.ds(start, count, stride=stride)


def _gather_rows(dst, dst0, src, src0, stride, count):
    for first in range(0, count, BLK):
        dst[pl.ds(dst0 + first, BLK), :] = src[_rows(src0 + first * stride, BLK, stride), :].astype(dst.dtype)


def _scatter_rows(dst, dst0, stride, src, src0, count):
    for first in range(0, count, BLK):
        dst[_rows(dst0 + first * stride, BLK, stride), :] = src[pl.ds(src0 + first, BLK), :].astype(dst.dtype)


def _by_subsequence(dst, src, d, wide=None, tmp=None):
    seq = src.shape[0]
    ln = seq // d
    if wide is not None:
        wide[...] = src[...].astype(F32)
        src = wide
    if d <= 4:
        for r in range(d):
            _gather_rows(dst, r * ln, src, r, d, ln)
    else:
        quarter = seq // 4
        for r4 in range(4):
            _gather_rows(tmp, r4 * quarter, src, r4, 4, quarter)
        for r4 in range(4):
            for a in range(d // 4):
                _gather_rows(dst, (4 * a + r4) * ln, tmp, r4 * quarter + a, d // 4, ln)


def _to_sequence(dst, src, d, tmp=None):
    seq = dst.shape[0]
    ln = seq // d
    if d <= 4:
        for r in range(d):
            _scatter_rows(dst, r, d, src, r * ln, ln)
    else:
        quarter = seq // 4
        for r4 in range(4):
            for a in range(d // 4):
                _scatter_rows(tmp, r4 * quarter + a, d // 4, src, (4 * a + r4) * ln, ln)
        for r4 in range(4):
            _scatter_rows(dst, r4, 4, tmp, r4 * quarter, quarter)


def _attn_forward(g, qkv, bias, bsz, seq):
    d = DILATIONS[g]
    ln = seq // d
    units = [(r, n) for r in range(d) for n in range(ln // BLK)]

    def band(n):
        return slice(BLK, 2 * BLK) if n == 0 else slice(0, 2 * BLK)

    def body(q_ref, k_ref, v_ref, b_ref, o_ref, l_ref, *scratch):
        hs = pl.program_id(1)
        s_scr, p_scr = scratch[:2]
        if d == 1:
            qd, kd, vd = q_ref, k_ref, v_ref
        else:
            wide, tmp, qd, kd, vd = scratch[2:7]
            for dst, src in ((qd, q_ref), (kd, k_ref), (vd, v_ref)):
                _by_subsequence(dst, src, d, wide, tmp)
        blk = lambda r, n: pl.ds(r * ln + n * BLK, BLK)
        direct = d <= 4
        out_rows = (lambda r, n: _sub_rows(d, r, n * BLK, BLK)) if direct else blk
        o_dst, l_dst = (o_ref, l_ref) if direct else scratch[7:9]
        for u, (r, n) in enumerate(units):
            s_scr[u, :, BLK:] = _dot_nt(qd[blk(r, n), :], kd[blk(r, n), :])
            if n > 0:
                s_scr[u, :, :BLK] = _dot_nt(qd[blk(r, n), :], kd[blk(r, n - 1), :])
        for u, (r, n) in enumerate(units):
            s = s_scr[u, :, band(n)] * SCALE + b_ref[hs, :, band(n)]
            m = jnp.max(s, axis=1, keepdims=True)
            e = jnp.exp(s - m)
            den = jnp.sum(e, axis=1, keepdims=True)
            p_scr[u, :, band(n)] = (e * (1.0 / den)).astype(BF16)
            l_dst[out_rows(r, n), :] = jnp.broadcast_to(m + jnp.log(den), (BLK, HD))
        for u, (r, n) in enumerate(units):
            acc = _dot(p_scr[u, :, BLK:], vd[blk(r, n), :])
            if n > 0:
                acc = acc + _dot(p_scr[u, :, :BLK], vd[blk(r, n - 1), :])
            o_dst[out_rows(r, n), :] = acc
        if not direct:
            _to_sequence(o_ref, o_dst, d, tmp)
            _to_sequence(l_ref, l_dst, d, tmp)

    rows_f32, rows_bf16 = pltpu.VMEM((seq, HD), F32), pltpu.VMEM((seq, HD), BF16)
    regrouped = [] if d == 1 else [rows_f32] * 2 + [rows_bf16] * 3 + ([] if d <= 4 else [rows_f32] * 2)
    out_spec = pl.BlockSpec((seq, HD), lambda b, hh: (b, hh))
    return pl.pallas_call(
        body, name=f"attn_forward_{g}", out_shape=[HBM_OUT((bsz * seq, AW), F32)] * 2, grid=(bsz, 4),
        in_specs=[_head_spec(seq, g, part) for part in range(3)] + [pl.BlockSpec((4, BLK, 2 * BLK), lambda b, hh: (0, 0, 0))],
        out_specs=[out_spec, out_spec],
        scratch_shapes=[pltpu.VMEM((len(units), BLK, 2 * BLK), F32), pltpu.VMEM((len(units), BLK, 2 * BLK), BF16)] + regrouped,
        compiler_params=_cp(("parallel", "parallel"), VMEM_CAP // 2),
    )(qkv, qkv, qkv, _in_hbm(bias))


def _attn_backward(g, qkv, do, dl, bias, prev_out, bsz, seq):
    d = DILATIONS[g]
    ln = seq // d
    units = [(r, n) for r in range(d) for n in range(ln // BLK)]

    def body(q_ref, k_ref, v_ref, do_ref, dl_ref, b_ref, *rest):
        dq_ref, dk_ref, dv_ref, db_ref = rest[-18:-14]
        wide, tmp, qd, kd, vd, dod, dld, dqd, dkd, dvd, s_scr, dp_scr, p_scr, ds_scr = rest[-14:]
        hs = pl.program_id(1)

        @pl.when((pl.program_id(0) == 0) & (hs == 0))
        def _():
            db_ref[...] = jnp.zeros_like(db_ref)

        for dst, src in ((qd, q_ref), (kd, k_ref), (vd, v_ref)):
            _by_subsequence(dst, src, d, wide, tmp)
        _by_subsequence(dod, do_ref, d, None, tmp)
        _by_subsequence(dld, dl_ref, d, None, tmp)
        dkd[...] = jnp.zeros_like(dkd)
        dvd[...] = jnp.zeros_like(dvd)
        blk = lambda r, n: pl.ds(r * ln + n * BLK, BLK)
        keys = lambda r, n: [(blk(r, n), slice(BLK, 2 * BLK))] + ([(blk(r, n - 1), slice(0, BLK))] if n > 0 else [])
        for u, (r, n) in enumerate(units):
            for rows, band in keys(r, n):
                s_scr[u, :, band] = _dot_nt(qd[blk(r, n), :], kd[rows, :])
                dp_scr[u, :, band] = _dot_nt(dod[blk(r, n), :], vd[rows, :])
        for u, (r, n) in enumerate(units):
            both = dld[blk(r, n), :]
            lse, delta = both[:, 0:1], both[:, 64:65]
            band = slice(BLK, 2 * BLK) if n == 0 else slice(0, 2 * BLK)
            p = jnp.exp(s_scr[u, :, band] * SCALE + b_ref[hs, :, band] - lse)
            ds = p * (dp_scr[u, :, band] - delta)
            p_scr[u, :, band] = p.astype(BF16)
            ds_scr[u, :, band] = ds.astype(BF16)
            db_ref[hs, :, band] += ds
        for u, (r, n) in enumerate(units):
            dq = jnp.zeros((BLK, HD), F32)
            for rows, band in keys(r, n):
                dvd[rows, :] += _dot_tn(p_scr[u, :, band], dod[blk(r, n), :])
                dkd[rows, :] += _dot_tn(ds_scr[u, :, band], qd[blk(r, n), :]) * SCALE
                dq = dq + _dot(ds_scr[u, :, band], kd[rows, :])
            dqd[blk(r, n), :] = dq * SCALE
        for out, acc in ((dq_ref, dqd), (dk_ref, dkd), (dv_ref, dvd)):
            if d == 1:
                out[...] = acc[...].astype(BF16)
            else:
                _to_sequence(wide, acc, d, tmp)
                out[...] = wide[...].astype(BF16)

    qkv_spec = _head_spec(seq, g, 0)
    out_spec = pl.BlockSpec((seq, HD), lambda b, hh: (b, hh))
    band_spec = pl.BlockSpec((4, BLK, 2 * BLK), lambda b, hh: (0, 0, 0))
    ins = [qkv, qkv, qkv, _in_hbm(do), _in_hbm(dl), _in_hbm(bias)]
    in_specs = [_head_spec(seq, g, part) for part in range(3)] + [out_spec, out_spec, band_spec]
    aliases = {}
    if prev_out is not None:
        ins += list(prev_out)
        in_specs += [ANY] * 3
        aliases = {6: 0, 7: 1, 8: 2}
    rows_bf16, rows_f32 = pltpu.VMEM((seq, HD), BF16), pltpu.VMEM((seq, HD), F32)
    staged = [pltpu.VMEM((len(units), BLK, 2 * BLK), F32)] * 2 + [pltpu.VMEM((len(units), BLK, 2 * BLK), BF16)] * 2
    dq, dk, dv, db = pl.pallas_call(
        body, name=f"attn_backward_{g}", out_shape=[HBM_OUT((bsz * seq, QW), BF16)] * 3 + [SDS((4, BLK, 2 * BLK), F32)], grid=(bsz, 4),
        in_specs=in_specs, out_specs=[qkv_spec] * 3 + [band_spec], input_output_aliases=aliases,
        scratch_shapes=[rows_f32] * 2 + [rows_bf16] * 4 + [rows_f32] * 4 + staged,
        compiler_params=_cp(("arbitrary", "arbitrary"), VMEM_CAP // 2),
    )(*ins)
    return (dq, dk, dv), db


def _mix_forward(gates, og, lg, x2, tgt, gate, w_ao, w_co, w_o, conv_w, conv_b, ln_g, ln_b, bsz, seq, tm=256):
    t = x2.shape[0]
    spt = seq // tm

    def body(g_ref, o1, o2, o3, l1, l2, l3, x_ref, t_ref, gate_ref, wao_ref, wco_ref, wo_ref, cw_ref, cb_ref, lng_ref, lnb_ref,
             ain_ref, sin_ref, mrg_ref, dy_ref, aout_ref, sout_ref, yc_ref, o_ref, lj_ref, dxr_ref, vec_ref, dgate_ref, zc_ref):
        b, i = pl.program_id(0), pl.program_id(1)

        @pl.when((b == 0) & (i == 0))
        def _():
            vec_ref[...] = jnp.zeros_like(vec_ref)

        @pl.when(i == 0)
        def _():
            zc_ref[...] = jnp.zeros_like(zc_ref)
            dgate_ref[...] = jnp.zeros_like(dgate_ref)

        g_attn, u, bg, cg, g_conv, m_attn, m_conv = (g_ref[:, lo:hi].astype(F32) for lo, hi in GATE_COLS)
        la, lb, lc = l1[...], l2[...], l3[...]
        mx = jnp.maximum(la, jnp.maximum(lb, lc))
        ea, eb, ec = jnp.exp(la - mx), jnp.exp(lb - mx), jnp.exp(lc - mx)
        den = ea + eb + ec
        o = (ea * o1[...] + eb * o2[...] + ec * o3[...]) / den
        o_ref[...] = o
        lj_ref[...] = mx + jnp.log(den)
        a_in = o * (g_attn * _sig(g_attn))
        ain_ref[...] = a_in.astype(BF16)
        a_out = _dot(a_in.astype(BF16), wao_ref[...])
        aout_ref[...] = a_out.astype(BF16)
        z = cg * u
        rows = lax.broadcasted_iota(jnp.int32, (tm, D), 0)
        c6, c7 = zc_ref[6:7, :], zc_ref[7:8, :]
        z1 = jnp.where(rows == 0, c7, pltpu.roll(z, 1, 0))
        z2 = jnp.where(rows == 0, c6, jnp.where(rows == 1, c7, pltpu.roll(z, 2, 0)))
        zc_ref[...] = z[tm - 8:tm, :]
        y_conv = (cw_ref[0:1, :] * z2 + cw_ref[1:2, :] * z1 + cw_ref[2:3, :] * z) + cb_ref[...]
        yc_ref[...] = y_conv.astype(BF16)
        s_in = bg * y_conv * (g_conv * _sig(g_conv))
        sin_ref[...] = s_in.astype(BF16)
        s_out = _dot(s_in.astype(BF16), wco_ref[...])
        sout_ref[...] = s_out.astype(BF16)
        merged = _sig(m_attn) * a_out + _sig(m_conv) * s_out
        mrg_ref[...] = merged.astype(BF16)
        y = _dot(merged.astype(BF16), wo_ref[...])
        gate1 = 1.0 + gate_ref[0]
        r = ALPHA * x_ref[...] + gate1 * y
        mu = jnp.mean(r, axis=1, keepdims=True)
        rc = r - mu
        rstd = lax.rsqrt(jnp.mean(rc * rc, axis=1, keepdims=True) + LN_EPS)
        xhat = rc * rstd
        diff = (xhat * lng_ref[...] + lnb_ref[...]) - t_ref[...]
        dout = diff * (1.0 / D)
        vec_ref[0:1, :] += jnp.sum(dout * xhat, axis=0, keepdims=True)
        vec_ref[1:2, :] += jnp.sum(dout, axis=0, keepdims=True)
        vec_ref[2:3, :] += jnp.sum(diff * diff, axis=0, keepdims=True)
        dxh = dout * lng_ref[...]
        dr = rstd * (dxh - jnp.mean(dxh, axis=1, keepdims=True) - xhat * jnp.mean(dxh * xhat, axis=1, keepdims=True))
        dxr_ref[...] = ALPHA * dr
        dy_ref[...] = (dr * gate1).astype(BF16)
        dgate_ref[0] += jnp.sum(dr * y, axis=0, keepdims=True)

    tok = lambda w: pl.BlockSpec((tm, w), lambda b, i: (b * spt + i, 0))
    const = lambda s: pl.BlockSpec(s, lambda b, i: (0,) * len(s))
    per_seq = pl.BlockSpec((1, 1, D), lambda b, i: (b, 0, 0))
    outs = pl.pallas_call(
        body, name="mix_forward", grid=(bsz, spt),
        out_shape=[HBM_OUT((t, AW), BF16), HBM_OUT((t, D), BF16), HBM_OUT((t, D), BF16), HBM_OUT((t, D), BF16), HBM_OUT((t, D), BF16),
                   HBM_OUT((t, D), BF16), HBM_OUT((t, D), BF16), HBM_OUT((t, AW), F32), HBM_OUT((t, AW), F32), HBM_OUT((t, D), F32),
                   SDS((8, D), F32), SDS((bsz, 1, D), F32)],
        in_specs=[tok(NGATE)] + [tok(AW)] * 6 + [tok(D), tok(D), per_seq, const((AW, D)), const((D, D)), const((D, D)),
                                                 const((3, D)), const((1, D)), const((1, D)), const((1, D))],
        out_specs=[tok(AW), tok(D), tok(D), tok(D), tok(D), tok(D), tok(D), tok(AW), tok(AW), tok(D), const((8, D)), per_seq],
        scratch_shapes=[pltpu.VMEM((8, D), F32)],
        compiler_params=_cp(("arbitrary", "arbitrary"), VMEM_CAP),
    )(gates, *map(_in_hbm, og), *map(_in_hbm, lg), x2, tgt, gate, w_ao, w_co, w_o, conv_w, conv_b, ln_g, ln_b)
    return outs


def _mix_backward(gates, dy, a_out, s_out, y_conv, o, lj, w_ao, w_co, w_o, conv_w, vec_f, bsz, seq, tm=256):
    t = dy.shape[0]
    spt = seq // tm

    def body(g_ref, dy_ref, aout_ref, sout_ref, yc_ref, o_ref, lj_ref, wao_ref, wco_ref, wo_ref, cw_ref, vecf_ref,
             dg_ref, do_ref, dl_ref, daout_ref, dsout_ref, vec_ref, car_ref):
        b, i = pl.program_id(0), pl.program_id(1)

        @pl.when((b == 0) & (i == 0))
        def _():
            vec_ref[...] = vecf_ref[...]

        @pl.when(i == 0)
        def _():
            car_ref[...] = jnp.zeros_like(car_ref)

        g_attn, u, bg, cg, g_conv, m_attn, m_conv = (g_ref[:, lo:hi].astype(F32) for lo, hi in GATE_COLS)
        dmerged = _dot_nt(dy_ref[...], wo_ref[...])
        sa, sc = _sig(m_attn), _sig(m_conv)
        da_out = (dmerged * sa).astype(BF16)
        ds_out = (dmerged * sc).astype(BF16)
        daout_ref[...] = da_out
        dsout_ref[...] = ds_out
        dg_ref[:, 4608:5632] = (dmerged * aout_ref[...].astype(F32) * (sa * (1.0 - sa))).astype(BF16)
        dg_ref[:, 5632:6656] = (dmerged * sout_ref[...].astype(F32) * (sc * (1.0 - sc))).astype(BF16)
        da_in = _dot_nt(da_out, wao_ref[...])
        ds_in = _dot_nt(ds_out, wco_ref[...])
        sga = _sig(g_attn)
        o = o_ref[...]
        do = da_in * (g_attn * sga)
        do_ref[...] = do
        dg_ref[:, 0:512] = (da_in * o * (sga * (1.0 + g_attn * (1.0 - sga)))).astype(BF16)
        prod = do * o
        lane = lax.broadcasted_iota(jnp.int32, (tm, HD), 1)
        for j in range(4):
            cs = slice(j * HD, (j + 1) * HD)
            delta = jnp.sum(prod[:, cs], axis=1, keepdims=True)
            dl_ref[:, cs] = jnp.where(lane < 64, lj_ref[:, cs], delta)
        sgc = _sig(g_conv)
        silu_c = g_conv * sgc
        yc = yc_ref[...].astype(F32)
        dg_ref[:, 1536:2560] = (ds_in * yc * silu_c).astype(BF16)
        dg_ref[:, 3584:4608] = (ds_in * bg * yc * (sgc * (1.0 + g_conv * (1.0 - sgc)))).astype(BF16)
        dyc = ds_in * bg * silu_c
        rows = lax.broadcasted_iota(jnp.int32, (tm, D), 0)
        c0, c1 = car_ref[0:1, :], car_ref[1:2, :]
        n1 = jnp.where(rows == tm - 1, c0, pltpu.roll(dyc, tm - 1, 0))
        n2 = jnp.where(rows == tm - 2, c0, jnp.where(rows == tm - 1, c1, pltpu.roll(dyc, tm - 2, 0)))
        car_ref[...] = dyc[0:8, :]
        dz = cw_ref[2:3, :] * dyc + cw_ref[1:2, :] * n1 + cw_ref[0:1, :] * n2
        z = cg * u
        dg_ref[:, 512:1536] = (dz * cg).astype(BF16)
        dg_ref[:, 2560:3584] = (dz * u).astype(BF16)
        vec_ref[3:4, :] += jnp.sum(n2 * z, axis=0, keepdims=True)
        vec_ref[4:5, :] += jnp.sum(n1 * z, axis=0, keepdims=True)
        vec_ref[5:6, :] += jnp.sum(dyc * z, axis=0, keepdims=True)
        vec_ref[6:7, :] += jnp.sum(dyc, axis=0, keepdims=True)

    tok = lambda w: pl.BlockSpec((tm, w), lambda b, i: (b * spt + (spt - 1 - i), 0))
    const = lambda s: pl.BlockSpec(s, lambda b, i: (0,) * len(s))
    return pl.pallas_call(
        body, name="mix_backward", grid=(bsz, spt),
        out_shape=[HBM_OUT((t, NGATE), BF16), HBM_OUT((t, AW), F32), HBM_OUT((t, AW), F32), HBM_OUT((t, D), BF16), HBM_OUT((t, D), BF16),
                   SDS((8, D), F32)],
        in_specs=[tok(NGATE), tok(D), tok(D), tok(D), tok(D), tok(AW), tok(AW), const((AW, D)), const((D, D)), const((D, D)), const((3, D)),
                  const((8, D))],
        out_specs=[tok(NGATE), tok(AW), tok(AW), tok(D), tok(D), const((8, D))],
        scratch_shapes=[pltpu.VMEM((8, D), F32)],
        compiler_params=_cp(("arbitrary", "arbitrary"), VMEM_CAP),
    )(gates, dy, a_out, s_out, y_conv, o, lj, w_ao, w_co, w_o, conv_w, vec_f)


def _scatter_copies(src, land, send_sems, recv_sems):
    x, y, c = _place()
    chips = [(1 - x, y), (x, 1 - y), (1 - x, 1 - y)]
    return [pltpu.make_async_remote_copy(src_ref=src[a].at[2 * cx + cy], dst_ref=land[a].at[r], send_sem=send_sems.at[3 * a + r],
                                         recv_sem=recv_sems.at[3 * a + r], device_id=(cx, cy, c), device_id_type=MESH)
            for a in range(len(src)) for r, (cx, cy) in enumerate(chips)]


def _halves_out(a):
    kind, nr, nc = W_CUTS[a]
    shape = (nr // 2, W_FULL[a][1]) if kind == "col" else (NCHIP, nr // 2, nc)
    return [SDS(shape, F32), SDS(shape, BF16)]


def _write_halves(a, acc_ref, c, mine_ref, theirs_ref):
    kind, nr, nc = W_CUTS[a]
    hr = nr // 2
    if kind == "col":
        mine_ref[...] = acc_ref[pl.ds(pl.multiple_of(c * hr, hr), hr), :]
        theirs_ref[...] = acc_ref[pl.ds(pl.multiple_of((1 - c) * hr, hr), hr), :].astype(BF16)
    else:
        for k in range(NCHIP):
            mine_ref[k] = acc_ref[pl.ds(pl.multiple_of(k * nr + c * hr, hr), hr), :]
            theirs_ref[k] = acc_ref[pl.ds(pl.multiple_of(k * nr + (1 - c) * hr, hr), hr), :].astype(BF16)


def _out_weight_grads(a_in, da_out, s_in, ds_out, merged, dy, core, tk=512):
    t = dy.shape[0]
    nt = t // tk

    def body(c_ref, ain_ref, da_ref, sin_ref, ds_ref, m_ref, dy_ref, *rest):
        outs, (gao, gco, go) = rest[:6], rest[6:]

        @pl.when(pl.program_id(0) == 0)
        def _():
            gao[...] = jnp.zeros_like(gao)
            gco[...] = jnp.zeros_like(gco)
            go[...] = jnp.zeros_like(go)

        gao[...] += _dot_tn(ain_ref[...], da_ref[...])
        gco[...] += _dot_tn(sin_ref[...], ds_ref[...])
        go[...] += _dot_tn(m_ref[...], dy_ref[...])

        @pl.when(pl.program_id(0) == nt - 1)
        def _():
            for a, acc in ((1, gao), (2, gco), (3, go)):
                _write_halves(a, acc, c_ref[0], outs[2 * a - 2], outs[2 * a - 1])

    tok = lambda w: pl.BlockSpec((tk, w), lambda i, cr: (i, 0))
    out_shape = _halves_out(1) + _halves_out(2) + _halves_out(3)
    outs = pl.pallas_call(
        body, name="out_weight_grads", out_shape=out_shape,
        grid_spec=pltpu.PrefetchScalarGridSpec(
            num_scalar_prefetch=1, grid=(nt,), in_specs=[tok(AW), tok(D), tok(D), tok(D), tok(D), tok(D)],
            out_specs=[pl.BlockSpec(o.shape, lambda i, cr, nd=len(o.shape): (0,) * nd) for o in out_shape],
            scratch_shapes=[pltpu.VMEM((AW, D), F32), pltpu.VMEM((D, D), F32), pltpu.VMEM((D, D), F32)]),
        compiler_params=_cp(("arbitrary",), VMEM_CAP),
    )(core, a_in, da_out, s_in, ds_out, merged, dy)
    return [(outs[0], outs[1]), (outs[2], outs[3]), (outs[4], outs[5])]


def _input_grad(dq, dk, dv, dgates, w, x2, dxr, sc1p, seq, sums, tm=512):
    t = x2.shape[0]
    nt = t // tm
    spt = seq // tm
    bsz = t // seq
    n = len(sums)
    gblk = NGATE // 4
    nsteps = 3 + 4

    def body(dq_ref, dk_ref, dv_ref, dg_ref, wq_ref, wg_ref, x_ref, dxr_ref, sc_ref, *rest):
        src, (dx_ref, dsh_ref, dsc_ref), land = rest[:n], rest[n:n + 3], rest[n + 3:2 * n + 3]
        acc_ref, send_sems, recv_sems = rest[2 * n + 3:]
        j, i = pl.program_id(0), pl.program_id(1)
        copies = _scatter_copies(src, land, send_sems, recv_sems)
        rows = pl.ds(pl.multiple_of(i * tm, tm), tm)

        @pl.when((i == 0) & (j == 0))
        def _():
            for cp in copies:
                cp.start()

        for k, ref in enumerate((dq_ref, dk_ref, dv_ref)):
            @pl.when(j == k)
            def _(k=k, ref=ref):
                part = _dot_nt(ref[...], wq_ref[...])
                if k == 0:
                    acc_ref[rows, :] = part
                else:
                    acc_ref[rows, :] += part

        @pl.when((j >= 3) & (j < nsteps - 1))
        def _():
            acc_ref[rows, :] += _dot_nt(dg_ref[...], wg_ref[...])

        @pl.when(j == nsteps - 1)
        def _():
            dh = acc_ref[rows, :] + _dot_nt(dg_ref[...], wg_ref[...])
            dx_ref[...] = dh * sc_ref[0] + dxr_ref[...]

            @pl.when(i % spt == 0)
            def _():
                dsh_ref[...] = jnp.zeros_like(dsh_ref)
                dsc_ref[...] = jnp.zeros_like(dsc_ref)

            dsh_ref[0] += jnp.sum(dh, axis=0, keepdims=True)
            dsc_ref[0] += jnp.sum(dh * x_ref[...], axis=0, keepdims=True)

        @pl.when((i == nt - 1) & (j == nsteps - 1))
        def _():
            for cp in copies:
                cp.wait()

    def held(k):
        return lambda j, i: (jnp.where(j == k, i, jnp.where(j < k, 0, nt - 1)), 0)

    last = lambda j, i: (jnp.where(j == nsteps - 1, i, 0), 0)
    outs = pl.pallas_call(
        body, name="input_grad", grid=(nsteps, nt),
        out_shape=[SDS((t, D), F32), SDS((bsz, 1, D), F32), SDS((bsz, 1, D), F32)] + [SDS((3,) + s.shape[1:], BF16) for s in sums],
        in_specs=[pl.BlockSpec((tm, QW), held(0)), pl.BlockSpec((tm, QW), held(1)), pl.BlockSpec((tm, QW), held(2)),
                  pl.BlockSpec((tm, gblk), lambda j, i: (jnp.where(j >= 3, i, 0), jnp.clip(j - 3, 0, 3))),
                  pl.BlockSpec((D, QW), lambda j, i: (0, jnp.minimum(j, 2))),
                  pl.BlockSpec((pl.Element(D), pl.Element(gblk)), lambda j, i: (0, pl.multiple_of(3 * QW + gblk * jnp.clip(j - 3, 0, 3), 128))),
                  pl.BlockSpec((tm, D), last), pl.BlockSpec((tm, D), last),
                  pl.BlockSpec((1, 1, D), lambda j, i: (jnp.where(j == nsteps - 1, i // spt, 0), 0, 0))] + [ANY] * n,
        out_specs=[pl.BlockSpec((tm, D), last),
                   pl.BlockSpec((1, 1, D), lambda j, i: (jnp.where(j == nsteps - 1, i // spt, 0), 0, 0)),
                   pl.BlockSpec((1, 1, D), lambda j, i: (jnp.where(j == nsteps - 1, i // spt, 0), 0, 0))] + [ANY] * n,
        scratch_shapes=[pltpu.VMEM((t, D), F32), pltpu.SemaphoreType.DMA((3 * NCHIP,)), pltpu.SemaphoreType.DMA((3 * NCHIP,))],
        compiler_params=_cp(("arbitrary", "arbitrary"), VMEM_CAP, side=True),
    )(dq, dk, dv, dgates, w, w, x2, dxr, sc1p, *sums)
    return outs[0], outs[1], outs[2], outs[3:]


def _in_weight_grad(ht, dq, dk, dv, dgates, core, sums):
    t = ht.shape[1]
    hr = D // 2
    n = len(sums)

    def body(c_ref, ht_ref, dq_ref, dk_ref, dv_ref, dg_ref, *rest):
        src, (mine_ref, theirs_ref), land = rest[:n], rest[n:n + 2], rest[n + 2:2 * n + 2]
        acc_ref, send_sems, recv_sems = rest[2 * n + 2:]
        j = pl.program_id(0)
        copies = _scatter_copies(src, land, send_sems, recv_sems)

        @pl.when(j == 0)
        def _():
            for cp in copies:
                cp.start()

        for k, ref in enumerate((dq_ref, dk_ref, dv_ref)):
            @pl.when((j >= k * NQT) & (j < (k + 1) * NQT))
            def _(ref=ref):
                acc_ref[...] = _dot(ht_ref[...], ref[...])

        @pl.when(j >= 3 * NQT)
        def _():
            acc_ref[...] = _dot(ht_ref[...], dg_ref[...])

        _write_halves(0, acc_ref, c_ref[0], mine_ref, theirs_ref)

        @pl.when(j == NPT - 1)
        def _():
            for cp in copies:
                cp.wait()

    def part(k):
        return pl.BlockSpec((t, TN), lambda j, cr: (0, jnp.clip(j - k * NQT, 0, NQT - 1)))

    out_spec = pl.BlockSpec((hr, TN), lambda j, cr: (0, j))
    outs = pl.pallas_call(
        body, name="in_weight_grad", out_shape=[SDS((hr, NCOL), F32), SDS((hr, NCOL), BF16)] + [SDS((3,) + v.shape[1:], BF16) for v in sums],
        grid_spec=pltpu.PrefetchScalarGridSpec(
            num_scalar_prefetch=1, grid=(NPT,),
            in_specs=[pl.BlockSpec((D, t), lambda j, cr: (0, 0)), part(0), part(1), part(2),
                      pl.BlockSpec((t, TN), lambda j, cr: (0, jnp.maximum(j - 3 * NQT, 0)))] + [ANY] * n,
            out_specs=[out_spec, out_spec] + [ANY] * n,
            scratch_shapes=[pltpu.VMEM((D, TN), F32), pltpu.SemaphoreType.DMA((3 * NCHIP,)), pltpu.SemaphoreType.DMA((3 * NCHIP,))]),
        compiler_params=_cp(("arbitrary",), VMEM_CAP, side=True),
    )(core, ht, dq, dk, dv, dgates, *sums)
    return outs[0], outs[1], outs[2:]


def _sum_partials(gathered):
    def body(g_ref, o_ref):
        acc = g_ref[0]
        for k in range(1, 8):
            acc = acc + g_ref[k]
        o_ref[...] = acc

    return pl.pallas_call(body, name="sum_partials", out_shape=SDS(gathered.shape[1:], F32), in_specs=[VMEM_SPEC], out_specs=VMEM_SPEC)(gathered)


def _adamw(w, g, m, v, name, tr=256):
    r, cdim = w.shape
    tr = tr if cdim <= D else tr // 2
    tr = tr if (r % tr == 0 and r > tr) else r

    def body(w_ref, g_ref, m_ref, v_ref, go_ref, d_ref, nm_ref, nv_ref):
        gv = g_ref[...]
        go_ref[...] = gv
        nm = B1 * m_ref[...] + (1.0 - B1) * gv
        nv = B2 * v_ref[...] + (1.0 - B2) * (gv * gv)
        m_hat = nm / (1.0 - B1 ** STEP)
        v_hat = nv / (1.0 - B2 ** STEP)
        d_ref[...] = -LR * (m_hat / (jnp.sqrt(v_hat) + EPS) + WD * w_ref[...])
        nm_ref[...] = nm
        nv_ref[...] = nv

    spec = pl.BlockSpec((tr, cdim), lambda i: (i, 0))
    return pl.pallas_call(
        body, name=name, grid=(r // tr,), out_shape=[SDS((r, cdim), F32)] * 4, in_specs=[spec] * 4, out_specs=[spec] * 4,
        compiler_params=_cp(("parallel",), VMEM_CAP // 2),
    )(w, g, m, v)


def _t5_bucket(dist):
    n = jnp.maximum(dist, 1).astype(F32)
    large = MAX_EXACT + (jnp.log(n / MAX_EXACT) / math.log(MAX_DISTANCE / MAX_EXACT) * (N_BUCKETS - MAX_EXACT)).astype(jnp.int32)
    large = jnp.minimum(large, N_BUCKETS - 1)
    return jnp.where(dist < MAX_EXACT, dist, large)


def _band_buckets():
    a = jnp.arange(BLK)[:, None]
    b = jnp.arange(2 * BLK)[None, :]
    steps = jnp.maximum(a + BLK - b, 0)
    return jnp.stack([_t5_bucket(steps * d) for d in DILATIONS]).astype(jnp.int32)


def _pad_rows(a, rows=8):
    return jnp.pad(a, ((0, rows - a.shape[0]), (0, 0)))


def kernel(x, c, w_ada, b_ada, w_in, conv_w, conv_b, rel_bias, w_attn_out, w_conv_out, w_o, ln_g, ln_b, loss_target, m_w_ada, m_b_ada, m_w_in, m_conv_w, m_conv_b, m_rel_bias, m_w_attn_out, m_w_conv_out, m_w_o, m_ln_g, m_ln_b, v_w_ada, v_b_ada, v_w_in, v_conv_w, v_conv_b, v_rel_bias, v_w_attn_out, v_w_conv_out, v_w_o, v_ln_g, v_ln_b):
    bsz, seq, _ = x.shape
    t = bsz * seq
    mx, my, mc = _place()
    chip = 2 * mx + my
    dev = 4 * mx + 2 * my + mc
    x2 = x.reshape(t, D)
    tgt = loss_target.reshape(t, D)

    mine = _to_bf16_windows([w[0] for w in (w_in, w_attn_out, w_conv_out, w_o)])

    n_ada = w_ada.shape[2]
    n_cw = conv_w.shape[2]
    c_and_cw = jnp.concatenate([_pad_rows(c), jnp.pad(conv_w[0], ((0, 5), (0, D - n_cw)))], axis=0)
    firsts = _all_gather8(c_and_cw, "gather_c_conv_w")
    c_all = firsts[:, 0:bsz, :].reshape(8 * bsz, D)
    conv_w_f = firsts[0::2, 8:11, 0:n_cw].transpose(1, 0, 2).reshape(3, D)
    b_cols = lax.dynamic_slice(b_ada, (0, chip * n_ada), (1, n_ada))
    mod_part = _ada_forward(c_all, w_ada[0], b_cols)
    mod_parts = _all_gather8(mod_part, "gather_mod")
    mod_all = mod_parts[0::2].transpose(1, 0, 2).reshape(8 * bsz, 3 * D)
    mod = lax.dynamic_slice(mod_all, (dev * bsz, 0), (bsz, 3 * D))
    shift = mod[:, 0:D].reshape(bsz, 1, D)
    sc1p = 1.0 + mod[:, D:2 * D].reshape(bsz, 1, D)
    gate = mod[:, 2 * D:].reshape(bsz, 1, D)

    h, ht = _modulate(x2, sc1p, shift, seq)
    tab = lax.dynamic_index_in_dim(jnp.asarray(_tile_tables()), chip, 0, keepdims=False)
    qkv, gates, (w_in_f, w_ao_f, w_co_f, w_o_f) = _project_gather(h, mine, tab)
    buckets = _band_buckets()
    bias = _bias_tables(rel_bias, buckets)
    og, lg = [], []
    for g in range(3):
        o_g, l_g = _attn_forward(g, qkv, bias[g], bsz, seq)
        og.append(o_g)
        lg.append(l_g)
    (a_in, s_in, merged, dy, a_out, s_out, y_conv, o, lj, dxr, vec_f, dgate) = _mix_forward(
        gates, og, lg, x2, tgt, gate, w_ao_f, w_co_f, w_o_f, conv_w_f, conv_b, ln_g, ln_b, bsz, seq)

    dgates, do, dl, da_out, ds_out, vec = _mix_backward(gates, dy, a_out, s_out, y_conv, o, lj, w_ao_f, w_co_f, w_o_f, conv_w_f, vec_f, bsz, seq)
    core = jnp.reshape(mc, (1,)).astype(jnp.int32)
    small_grads = _out_weight_grads(a_in, da_out, s_in, ds_out, merged, dy, core)
    got_small = _swap_halves([theirs for _, theirs in small_grads], "swap_small_grad_halves")
    sums_small = _chip_sums([own for own, _ in small_grads], got_small, 1, "chip_sums_small")
    dqkv, dbs = None, []
    for g in range(3):
        dqkv, db = _attn_backward(g, qkv, do, dl, bias[g], dqkv, bsz, seq)
        dbs.append(db)
    dq, dk, dv = dqkv
    drb = _bias_grad(jnp.stack(dbs), buckets)
    drb = drb[:, :, 0:4].transpose(1, 0, 2).reshape(N_BUCKETS, 12)
    g_in_mine, g_in_theirs, landed_small = _in_weight_grad(ht, dq, dk, dv, dgates, core, [bf for _, bf in sums_small])
    got_in = _swap_halves([g_in_theirs], "swap_in_grad_halves")
    sums_in = _chip_sums([g_in_mine], got_in, 0, "chip_sums_in")
    grad_x, dshift, dscale, landed_in = _input_grad(dq, dk, dv, dgates, w_in_f, x2, dxr, sc1p, seq, [bf for _, bf in sums_in])
    halves = _reduce_mine([own for own, _ in sums_in + sums_small], list(landed_in) + list(landed_small))
    gw_in, gw_ao, gw_co, gw_o = _join_halves(halves)

    dmod = jnp.concatenate([dshift, dscale, dgate], axis=2).reshape(bsz * 3, D)
    drb_row = jnp.pad(drb.reshape(1, N_BUCKETS * 12), ((0, 0), (0, D - N_BUCKETS * 12)))
    vec = lax.dynamic_update_slice(vec, drb_row, (7, 0))
    packed = jnp.concatenate([vec, _pad_rows(dmod)], axis=0)
    gathered = _all_gather8(packed, "gather_small")
    small = _sum_partials(gathered)
    g_ln_g, g_ln_b, loss_lanes = small[0:1], small[1:2], small[2:3]
    g_conv_w_full, g_conv_b = small[3:6], small[6:7]
    g_rel_bias = small[7, 0:N_BUCKETS * 12].reshape(N_BUCKETS, 12)
    loss = 0.5 / D * jnp.sum(loss_lanes)
    dmod_all = gathered[:, 8:8 + 3 * bsz, :].reshape(8 * bsz, 3 * D)
    dmod_cols = lax.dynamic_slice(dmod_all, (0, chip * n_ada), (8 * bsz, n_ada))
    gw_ada, gb_ada = _ada_backward(c_all, dmod_cols, dmod_all)
    g_conv_w = lax.dynamic_slice(g_conv_w_full, (0, chip * n_cw), (3, n_cw))

    names = ["w_ada", "b_ada", "w_in", "conv_w", "conv_b", "rel_bias", "w_attn_out", "w_conv_out", "w_o", "ln_g", "ln_b"]
    two_d = lambda a: a.reshape(a.shape[-2:]) if a.ndim == 3 else a
    weights = dict(zip(names, map(two_d, (w_ada, b_ada, w_in, conv_w, conv_b, rel_bias, w_attn_out, w_conv_out, w_o, ln_g, ln_b))))
    ms = dict(zip(names, map(two_d, (m_w_ada, m_b_ada, m_w_in, m_conv_w, m_conv_b, m_rel_bias, m_w_attn_out, m_w_conv_out, m_w_o, m_ln_g, m_ln_b))))
    vs = dict(zip(names, map(two_d, (v_w_ada, v_b_ada, v_w_in, v_conv_w, v_conv_b, v_rel_bias, v_w_attn_out, v_w_conv_out, v_w_o, v_ln_g, v_ln_b))))
    grads = dict(zip(names, (gw_ada, gb_ada, gw_in, g_conv_w, g_conv_b, g_rel_bias, gw_ao, gw_co, gw_o, g_ln_g, g_ln_b)))
    shapes = dict(zip(names, (w_ada, b_ada, w_in, conv_w, conv_b, rel_bias, w_attn_out, w_conv_out, w_o, ln_g, ln_b)))
    grad_out, deltas, new_m, new_v = {}, {}, {}, {}
    for n in names:
        grad_out[n], deltas[n], new_m[n], new_v[n] = _adamw(weights[n], grads[n], ms[n], vs[n], f"adamw_{n}")
    shaped = lambda d: [d[n].reshape(shapes[n].shape) for n in names]
    return (loss, grad_x.reshape(bsz, seq, D), *shaped(grad_out), *shaped(deltas), *shaped(new_m), *shaped(new_v))
```

```python
import math

import numpy as np
import jax
import jax.numpy as jnp
from jax import lax
from jax.experimental import pallas as pl
from jax.experimental.pallas import tpu as pltpu

F32 = jnp.float32
BF16 = jnp.bfloat16
SDS = jax.ShapeDtypeStruct
MESH = pl.DeviceIdType.MESH
HBM_OUT = pltpu.HBM
ANY = pl.BlockSpec(memory_space=pl.ANY)
VMEM_SPEC = pl.BlockSpec(memory_space=pltpu.VMEM)

D = 1024
HD = 128
BLK = 128
QW = 1536
AW = 512
NGATE = 6656
GATE_COLS = ((0, 512), (512, 1536), (1536, 2560), (2560, 3584), (3584, 4608), (4608, 5632), (5632, 6656))
NCOL = 3 * QW + NGATE
TN = 512
NQT = QW // TN
NPT = NCOL // TN
DILATIONS = (1, 4, 16)
N_BUCKETS, MAX_EXACT, MAX_DISTANCE = 32, 16, 2048
ALPHA = 2.0 ** 0.25
LN_EPS = 1e-5
NEG = -1e30
SCALE = HD ** -0.5
LR, B1, B2, EPS, WD, STEP = 0.001, 0.9, 0.999, 1e-08, 0.01, 10
NCHIP = 4
VMEM_CAP = 60 * 2 ** 20


def _cp(sem=None, vmem=None, side=False):
    return pltpu.CompilerParams(dimension_semantics=sem, vmem_limit_bytes=vmem, has_side_effects=side)


def _dot(a, b):
    return jnp.dot(a, b, preferred_element_type=F32)


def _dot_nt(a, b):
    return lax.dot_general(a, b, (((1,), (1,)), ((), ())), preferred_element_type=F32)


def _dot_tn(a, b):
    return lax.dot_general(a, b, (((0,), (0,)), ((), ())), preferred_element_type=F32)


def _sig(x):
    return 1.0 / (1.0 + jnp.exp(-x))


def _in_hbm(a):
    return pltpu.with_memory_space_constraint(a, pltpu.HBM)


def _place():
    x, y, c = lax.axis_index("x"), lax.axis_index("y"), lax.axis_index("c")
    return x, y, c


def _all_gather8(v, name):
    r, cdim = v.shape

    def body(v_ref, out_ref, send_sems, recv_sems, local_sem):
        x, y, c = _place()
        me = 4 * x + 2 * y + c
        peers = [(x, y, 1 - c), (1 - x, y, c), (x, 1 - y, c), (1 - x, 1 - y, c),
                 (1 - x, y, 1 - c), (x, 1 - y, 1 - c), (1 - x, 1 - y, 1 - c)]
        mine = pltpu.make_async_copy(v_ref, out_ref.at[me], local_sem)
        mine.start()

        def copy(k, block, to):
            return pltpu.make_async_remote_copy(src_ref=v_ref, dst_ref=out_ref.at[block], send_sem=send_sems.at[k],
                                                recv_sem=recv_sems.at[k], device_id=to, device_id_type=MESH)

        sends = [copy(k, me, p) for k, p in enumerate(peers)]
        for cp in sends:
            cp.start()
        for k, (px, py, pc) in enumerate(peers):
            copy(k, 4 * px + 2 * py + pc, (px, py, pc)).wait_recv()
        for cp in sends:
            cp.wait_send()
        mine.wait()

    return pl.pallas_call(
        body, name=name, out_shape=SDS((8, r, cdim), v.dtype), in_specs=[VMEM_SPEC], out_specs=VMEM_SPEC,
        scratch_shapes=[pltpu.SemaphoreType.DMA((7,)), pltpu.SemaphoreType.DMA((7,)), pltpu.SemaphoreType.DMA(())],
        compiler_params=_cp(side=True),
    )(v)


W_CUTS = (("col", D, NCOL // NCHIP), ("col", AW, D // NCHIP), ("row", D // NCHIP, D), ("row", D // NCHIP, D))
W_FULL = ((D, NCOL), (AW, D), (D, D), (D, D))


def _shard_window(ref, cut, k, half):
    kind, nr, nc = cut
    hr = nr // 2
    if kind == "col":
        rows = pl.ds(0, nr) if half is None else pl.ds(pl.multiple_of(half * hr, 16), hr)
        return ref.at[rows, pl.ds(pl.multiple_of(k * nc, 128), nc)]
    if half is None:
        return ref.at[pl.ds(pl.multiple_of(k * nr, 16), nr), :]
    return ref.at[pl.ds(pl.multiple_of(k * nr + half * hr, 16), hr), :]


def _half_rows(ref, cut, half):
    hr = cut[1] // 2
    return ref.at[pl.ds(pl.multiple_of(half * hr, 16), hr), :]


def _to_bf16_windows(ws):
    x, y, _ = _place()
    chip = jnp.reshape(2 * x + y, (1,)).astype(jnp.int32)
    tr = 256
    n = len(ws)

    def body(c_ref, *refs):
        src, dst = refs[:n], refs[n:]
        dst[0][...] = src[0][...].astype(BF16)

        @pl.when(pl.program_id(0) == 0)
        def _():
            for a in range(1, n):
                dst[a][...] = src[a][...].astype(BF16)

    in_specs = [pl.BlockSpec((tr, W_CUTS[0][2]), lambda i, cr: (i, 0))]
    out_specs = [pl.BlockSpec((tr, W_CUTS[0][2]), lambda i, cr: (i, cr[0]))]
    for a in range(1, n):
        kind, nr, nc = W_CUTS[a]
        in_specs.append(pl.BlockSpec((nr, nc), lambda i, cr: (0, 0)))
        out_specs.append(pl.BlockSpec((nr, nc), (lambda i, cr: (0, cr[0])) if kind == "col" else (lambda i, cr: (cr[0], 0))))
    return pl.pallas_call(
        body, name="to_bf16", out_shape=[SDS(W_FULL[a], BF16) for a in range(n)],
        grid_spec=pltpu.PrefetchScalarGridSpec(num_scalar_prefetch=1, grid=(D // tr,), in_specs=in_specs, out_specs=out_specs),
        compiler_params=_cp(("arbitrary",)),
    )(chip, *map(_in_hbm, ws))


def _swap_halves(theirs, name):
    n = len(theirs)

    def body(*refs):
        src, land = refs[:n], refs[n:2 * n]
        send_sems, recv_sems = refs[2 * n:]
        x, y, c = _place()
        copies = [pltpu.make_async_remote_copy(src_ref=src[a], dst_ref=land[a], send_sem=send_sems.at[a], recv_sem=recv_sems.at[a],
                                               device_id=(x, y, 1 - c), device_id_type=MESH) for a in range(n)]
        for cp in copies:
            cp.start()
        for cp in copies:
            cp.wait()

    return pl.pallas_call(
        body, name=name, out_shape=[SDS(v.shape, v.dtype) for v in theirs], in_specs=[ANY] * n, out_specs=[ANY] * n,
        scratch_shapes=[pltpu.SemaphoreType.DMA((n,)), pltpu.SemaphoreType.DMA((n,))],
        compiler_params=_cp(side=True),
    )(*theirs)


def _chip_sums(mines, gots, first, name):
    n = len(mines)
    x, y, _ = _place()
    me = jnp.reshape(2 * x + y, (1,)).astype(jnp.int32)

    def body(me_ref, *refs):
        ins, outs = refs[:2 * n], refs[2 * n:]
        for a in range(n):
            hr, nc = W_CUTS[first + a][1] // 2, W_CUTS[first + a][2]
            s = (ins[2 * a][...] + ins[2 * a + 1][...].astype(F32)).reshape(hr, nc)
            outs[2 * a + 1][0] = s.astype(BF16)

            @pl.when(pl.program_id(0) == me_ref[0])
            def _(a=a, s=s):
                outs[2 * a][...] = s

    in_specs, out_specs, out_shape = [], [], []
    for a in range(n):
        kind, nr, nc = W_CUTS[first + a]
        hr = nr // 2
        spec = pl.BlockSpec((hr, nc), lambda k, mr: (0, k)) if kind == "col" else pl.BlockSpec((1, hr, nc), lambda k, mr: (k, 0, 0))
        in_specs += [spec, spec]
        out_specs += [pl.BlockSpec((hr, nc), lambda k, mr: (0, 0)), pl.BlockSpec((1, hr, nc), lambda k, mr: (k, 0, 0))]
        out_shape += [SDS((hr, nc), F32), SDS((NCHIP, hr, nc), BF16)]
    outs = pl.pallas_call(
        body, name=name, out_shape=out_shape,
        grid_spec=pltpu.PrefetchScalarGridSpec(num_scalar_prefetch=1, grid=(NCHIP,), in_specs=in_specs, out_specs=out_specs),
        compiler_params=_cp(("arbitrary",), VMEM_CAP),
    )(me, *[v for pair in zip(mines, gots) for v in pair])
    return [(outs[2 * a], outs[2 * a + 1]) for a in range(n)]


def _reduce_mine(mines, gots):
    n = len(mines)
    _, _, c = _place()
    core = jnp.reshape(c, (1,)).astype(jnp.int32)
    tr = 256
    nsteps = W_CUTS[0][1] // 2 // tr

    def body(c_ref, *refs):
        ins, outs = refs[:2 * n], refs[2 * n:]

        def add(a):
            m_ref, g_ref = ins[2 * a], ins[2 * a + 1]
            outs[a][...] = ((m_ref[...] + g_ref[0].astype(F32)) + g_ref[1].astype(F32)) + g_ref[2].astype(F32)

        add(0)

        @pl.when(pl.program_id(0) == 0)
        def _():
            for a in range(1, n):
                add(a)

    nc0 = W_CUTS[0][2]
    in_specs = [pl.BlockSpec((tr, nc0), lambda i, cr: (i, 0)), pl.BlockSpec((3, tr, nc0), lambda i, cr: (0, i, 0))]
    out_specs = [pl.BlockSpec((tr, nc0), lambda i, cr: (cr[0] * nsteps + i, 0))]
    for a in range(1, n):
        hr, nc = W_CUTS[a][1] // 2, W_CUTS[a][2]
        in_specs += [pl.BlockSpec((hr, nc), lambda i, cr: (0, 0)), pl.BlockSpec((3, hr, nc), lambda i, cr: (0, 0, 0))]
        out_specs.append(pl.BlockSpec((hr, nc), lambda i, cr: (cr[0], 0)))
    return pl.pallas_call(
        body, name="reduce_mine", out_shape=[SDS((W_CUTS[a][1], W_CUTS[a][2]), F32) for a in range(n)],
        grid_spec=pltpu.PrefetchScalarGridSpec(num_scalar_prefetch=1, grid=(nsteps,), in_specs=in_specs, out_specs=out_specs),
        compiler_params=_cp(("arbitrary",), VMEM_CAP),
    )(core, *[v for pair in zip(mines, gots) for v in pair])


def _join_halves(fulls):
    n = len(fulls)

    def body(*refs):
        full = refs[n:2 * n]
        send_sems, recv_sems = refs[2 * n:]
        x, y, c = _place()
        sibling = (x, y, 1 - c)

        def swap(a, half):
            rows = _half_rows(full[a], W_CUTS[a], half)
            return pltpu.make_async_remote_copy(src_ref=rows, dst_ref=rows, send_sem=send_sems.at[a], recv_sem=recv_sems.at[a],
                                                device_id=sibling, device_id_type=MESH)

        sends = [swap(a, c) for a in range(n)]
        for cp in sends:
            cp.start()
        for a, cp in enumerate(sends):
            cp.wait_send()
            swap(a, 1 - c).wait_recv()

    return pl.pallas_call(
        body, name="join_grad_halves", out_shape=[SDS((W_CUTS[a][1], W_CUTS[a][2]), F32) for a in range(n)],
        in_specs=[ANY] * n, out_specs=[ANY] * n,
        scratch_shapes=[pltpu.SemaphoreType.DMA((n,)), pltpu.SemaphoreType.DMA((n,))],
        input_output_aliases={a: a for a in range(n)}, compiler_params=_cp(side=True),
    )(*fulls)


def _ada_forward(c_all, w_ada, b_cols):
    nb, nc = c_all.shape[0], w_ada.shape[1]

    def body(c_ref, w_ref, b_ref, o_ref):
        cv = c_ref[...]
        sc = (cv * _sig(cv)).astype(BF16)
        o_ref[...] = _dot(sc, w_ref[...].astype(BF16)) + b_ref[...]

    return pl.pallas_call(body, name="ada_forward", out_shape=SDS((nb, nc), F32), compiler_params=_cp(vmem=VMEM_CAP // 2))(c_all, w_ada, b_cols)


def _ada_backward(c_all, dmod_cols, dmod_all):
    nb, nc = dmod_cols.shape

    def body(c_ref, d_ref, a_ref, gw_ref, gb_ref):
        cv = c_ref[...]
        sc = (cv * _sig(cv)).astype(BF16)
        gw_ref[...] = _dot_tn(sc, d_ref[...].astype(BF16))
        gb_ref[...] = jnp.sum(a_ref[...], axis=0, keepdims=True)

    return pl.pallas_call(body, name="ada_backward", out_shape=[SDS((D, nc), F32), SDS((1, dmod_all.shape[1]), F32)],
                          compiler_params=_cp(vmem=VMEM_CAP // 2))(c_all, dmod_cols, dmod_all)


def _modulate(x2, sc1p, shift, seq, tm=256):
    t = x2.shape[0]
    spt = seq // tm

    def body(x_ref, sc_ref, sh_ref, h_ref, ht_ref):
        h = x_ref[...] * sc_ref[0] + sh_ref[0]
        h_ref[...] = h.astype(BF16)
        ht_ref[...] = h.T.astype(BF16)

    per_seq = pl.BlockSpec((1, 1, D), lambda i: (i // spt, 0, 0))
    return pl.pallas_call(
        body, name="modulate", out_shape=[HBM_OUT((t, D), BF16), HBM_OUT((D, t), BF16)], grid=(t // tm,),
        in_specs=[pl.BlockSpec((tm, D), lambda i: (i, 0)), per_seq, per_seq],
        out_specs=[pl.BlockSpec((tm, D), lambda i: (i, 0)), pl.BlockSpec((D, tm), lambda i: (0, i))],
        compiler_params=_cp(("parallel",)),
    )(_in_hbm(x2), sc1p, shift)


TW = 256
TPS = NCOL // NCHIP // TW
NT = NCOL // TW
NQKV_T = 3 * QW // TW
N_TILE_SEMS = 2 * 3 * TPS


def _tile_tables():
    tabs = np.zeros((NCHIP, 3, NT), np.int32)
    for me in range(NCHIP):
        tiles = [TPS * (me ^ (s // TPS)) + s % TPS for s in range(NT)]
        tabs[me, 0] = tiles
        for row, (lo, hi) in enumerate(((0, NQKV_T), (NQKV_T, NT))):
            mine = [w - lo if lo <= w < hi else None for w in tiles]
            held = next(m for m in mine if m is not None)
            for s, m in enumerate(mine):
                held = held if m is None else m
                tabs[me, 1 + row, s] = held
    return tabs


def _project_gather(h, fulls, tab):
    t = h.shape[0]
    n = len(fulls)

    def body(tab_ref, h_ref, *rest):
        qkv_ref, g_ref = rest[n], rest[n + 1]
        full = rest[n + 2:2 * n + 2]
        w_buf, tile_sems, send_sems, recv_sems = rest[2 * n + 2:]
        s = pl.program_id(0)
        x, y, c = _place()
        me = 2 * x + y
        peers = [(x, 1 - y), (1 - x, y), (1 - x, 1 - y)]
        sibling = (x, y, 1 - c)

        def hop(a, r, stage, chip, half, to):
            window = _shard_window(full[a], W_CUTS[a], chip, half)
            k = N_TILE_SEMS + 6 * (a - 1) + 2 * r + stage
            return pltpu.make_async_remote_copy(src_ref=window, dst_ref=window, send_sem=send_sems.at[k], recv_sem=recv_sems.at[k],
                                                device_id=to, device_id_type=MESH)

        def tile_hop(q, stage, col_step, half, to):
            col = pl.multiple_of(tab_ref[0, col_step] * TW, TW)
            window = full[0].at[pl.ds(pl.multiple_of(half * (D // 2), 16), D // 2), pl.ds(col, TW)]
            k = 2 * (q - TPS) + stage
            return pltpu.make_async_remote_copy(src_ref=window, dst_ref=window, send_sem=send_sems.at[k], recv_sem=recv_sems.at[k],
                                                device_id=to, device_id_type=MESH)

        def send_tile(r, j):
            return tile_hop(TPS * (r + 1) + j, 0, j, c, (*peers[r], c))

        def pass_on(q, to):
            return tile_hop(3 * TPS + q % TPS, 0, q, c, to)

        def arrive(a, r):
            px, py = peers[r]
            chip = 2 * px + py
            hop(a, r, 0, chip, c, (px, py, c)).wait_recv()
            hop(a, r, 1, chip, c, sibling).start()
            hop(a, r, 1, chip, 1 - c, sibling).wait_recv()

        def tile(step, slot):
            col = pl.multiple_of(tab_ref[0, step] * TW, TW)
            return pltpu.make_async_copy(full[0].at[:, pl.ds(col, TW)], w_buf.at[slot], tile_sems.at[slot])

        @pl.when(s == 0)
        def _():
            for r in range(2):
                for j in range(TPS):
                    send_tile(r, j).start()
            tile(0, 0).start()

        @pl.when((s + 1 >= TPS) & (s + 1 < NT))
        def _():
            tile_hop(s + 1, 1, s + 1, 1 - c, sibling).wait_recv()

        @pl.when(s + 1 < NT)
        def _():
            tile(s + 1, 1 - (s % 2)).start()

        @pl.when((s + 2 >= TPS) & (s + 2 < NT))
        def _():
            tile_hop(s + 2, 0, s + 2, c, sibling).wait_recv()
            tile_hop(s + 2, 1, s + 2, c, sibling).start()

        for r in range(2):
            @pl.when(((s + 2) // TPS == r + 1) & ((s + 2) % 2 == (r + 1 + TPS * (r + 1)) % 2))
            def _(r=r):
                pass_on(s + 2, (*peers[1 - r], c)).start()

        @pl.when(s + 2 == 2 * TPS - 1)
        def _():
            for a in range(1, n):
                for r in range(3):
                    hop(a, r, 0, me, c, (*peers[r], c)).start()

        slot = s % 2
        tile(s, slot).wait()
        is_qkv = tab_ref[0, s] < NQKV_T
        for k in range(2):
            @pl.when(slot == k)
            def _(k=k):
                acc = _dot(h_ref[...], w_buf[k])

                @pl.when(is_qkv)
                def _():
                    qkv_ref[...] = acc.astype(BF16)

                @pl.when(jnp.logical_not(is_qkv))
                def _():
                    g_ref[...] = acc.astype(BF16)

        @pl.when(s == NT - 1)
        def _():
            for a in range(1, n):
                for r in range(3):
                    arrive(a, r)
            for r in range(3):
                for j in range(TPS):
                    send_tile(r, j).wait_send()
                    tile_hop(TPS * (r + 1) + j, 1, TPS * (r + 1) + j, c, sibling).wait_send()
                for a in range(1, n):
                    hop(a, r, 0, me, c, (*peers[r], c)).wait_send()
                    px, py = peers[r]
                    hop(a, r, 1, 2 * px + py, c, sibling).wait_send()

    n_sems = N_TILE_SEMS + 6 * (n - 1)
    outs = pl.pallas_call(
        body, name="project_gather", out_shape=[HBM_OUT((t, 3 * QW), BF16), HBM_OUT((t, NGATE), BF16)] + [SDS(s, BF16) for s in W_FULL],
        grid_spec=pltpu.PrefetchScalarGridSpec(
            num_scalar_prefetch=1, grid=(NT,),
            in_specs=[pl.BlockSpec((t, D), lambda s, tab: (0, 0))] + [ANY] * n,
            out_specs=[pl.BlockSpec((t, TW), lambda s, tab: (0, tab[1, s])), pl.BlockSpec((t, TW), lambda s, tab: (0, tab[2, s]))] + [ANY] * n,
            scratch_shapes=[pltpu.VMEM((2, D, TW), BF16), pltpu.SemaphoreType.DMA((2,)),
                            pltpu.SemaphoreType.DMA((n_sems,)), pltpu.SemaphoreType.DMA((n_sems,))]),
        input_output_aliases={2 + a: 2 + a for a in range(n)},
        compiler_params=_cp(("arbitrary",), VMEM_CAP, side=True),
    )(tab, _in_hbm(h), *fulls)
    return outs[0], outs[1], outs[2:]


def _bias_tables(rel_bias, buckets):
    def body(tab_ref, bk_ref, o_ref):
        a = lax.broadcasted_iota(jnp.int32, (BLK, 2 * BLK), 0)
        b = lax.broadcasted_iota(jnp.int32, (BLK, 2 * BLK), 1)
        steps = a + BLK - b
        valid = (steps >= 0) & (steps <= BLK)
        for g in range(3):
            bk = bk_ref[g]
            for j in range(4):
                def pick(kk, acc, bk=bk, col=4 * g + j):
                    return jnp.where(bk == kk, tab_ref[kk, col], acc)

                acc = lax.fori_loop(0, N_BUCKETS, pick, jnp.zeros((BLK, 2 * BLK), F32))
                o_ref[g, j] = jnp.where(valid, acc, NEG)

    return pl.pallas_call(
        body, name="bias_tables", out_shape=SDS((3, 4, BLK, 2 * BLK), F32),
        in_specs=[pl.BlockSpec(memory_space=pltpu.SMEM), VMEM_SPEC], out_specs=VMEM_SPEC,
    )(rel_bias, buckets)


def _bias_grad(ds_sum, buckets):
    def body(ds_ref, bk_ref, o_ref):
        lane = lax.broadcasted_iota(jnp.int32, (1, 128), 1)
        for g in range(3):
            def bucket(kk, carry, g=g):
                row = jnp.zeros((1, 128), F32)
                for j in range(4):
                    v = jnp.where(bk_ref[g] == kk, ds_ref[g, j], 0.0)
                    v = jnp.sum(v.reshape(BLK // 8, 8, 2 * BLK), axis=0)
                    s = jnp.sum(jnp.sum(v, axis=1, keepdims=True), axis=0, keepdims=True)
                    row = jnp.where(lane == j, s, row)
                o_ref[g, pl.ds(kk, 1), :] = row
                return carry

            lax.fori_loop(0, N_BUCKETS, bucket, 0)

    return pl.pallas_call(body, name="bias_grad", out_shape=SDS((3, N_BUCKETS, 128), F32), in_specs=[VMEM_SPEC, VMEM_SPEC],
                          out_specs=VMEM_SPEC)(ds_sum, buckets)


def _sub_rows(d, r, first, size):
    return pl.ds(first * d + r, size) if d == 1 else pl.ds(first * d + r, size, stride=d)


def _head_spec(seq, g, part):
    return pl.BlockSpec((seq, HD), lambda b, hh: (b, part * (QW // HD) + 4 * g + hh))


def _rows(start, count, stride):
    return pl.ds(start, count) if stride == 1 else pl.ds(start, count, stride=stride)


def _gather_rows(dst, dst0, src, src0, stride, count):
    for first in range(0, count, BLK):
        dst[pl.ds(dst0 + first, BLK), :] = src[_rows(src0 + first * stride, BLK, stride), :].astype(dst.dtype)


def _scatter_rows(dst, dst0, stride, src, src0, count):
    for first in range(0, count, BLK):
        dst[_rows(dst0 + first * stride, BLK, stride), :] = src[pl.ds(src0 + first, BLK), :].astype(dst.dtype)


def _by_subsequence(dst, src, d, wide=None, tmp=None):
    seq = src.shape[0]
    ln = seq // d
    if wide is not None:
        wide[...] = src[...].astype(F32)
        src = wide
    if d <= 4:
        for r in range(d):
            _gather_rows(dst, r * ln, src, r, d, ln)
    else:
        quarter = seq // 4
        for r4 in range(4):
            _gather_rows(tmp, r4 * quarter, src, r4, 4, quarter)
        for r4 in range(4):
            for a in range(d // 4):
                _gather_rows(dst, (4 * a + r4) * ln, tmp, r4 * quarter + a, d // 4, ln)


def _to_sequence(dst, src, d, tmp=None):
    seq = dst.shape[0]
    ln = seq // d
    if d <= 4:
        for r in range(d):
            _scatter_rows(dst, r, d, src, r * ln, ln)
    else:
        quarter = seq // 4
        for r4 in range(4):
            for a in range(d // 4):
                _scatter_rows(tmp, r4 * quarter + a, d // 4, src, (4 * a + r4) * ln, ln)
        for r4 in range(4):
            _scatter_rows(dst, r4, 4, tmp, r4 * quarter, quarter)


def _attn_forward(g, qkv, bias, bsz, seq):
    d = DILATIONS[g]
    ln = seq // d
    units = [(r, n) for r in range(d) for n in range(ln // BLK)]

    def band(n):
        return slice(BLK, 2 * BLK) if n == 0 else slice(0, 2 * BLK)

    def body(q_ref, k_ref, v_ref, b_ref, o_ref, l_ref, *scratch):
        hs = pl.program_id(1)
        s_scr, p_scr = scratch[:2]
        if d == 1:
            qd, kd, vd = q_ref, k_ref, v_ref
        else:
            wide, tmp, qd, kd, vd = scratch[2:7]
            for dst, src in ((qd, q_ref), (kd, k_ref), (vd, v_ref)):
                _by_subsequence(dst, src, d, wide, tmp)
        blk = lambda r, n: pl.ds(r * ln + n * BLK, BLK)
        direct = d <= 4
        out_rows = (lambda r, n: _sub_rows(d, r, n * BLK, BLK)) if direct else blk
        o_dst, l_dst = (o_ref, l_ref) if direct else scratch[7:9]
        for u, (r, n) in enumerate(units):
            s_scr[u, :, BLK:] = _dot_nt(qd[blk(r, n), :], kd[blk(r, n), :])
            if n > 0:
                s_scr[u, :, :BLK] = _dot_nt(qd[blk(r, n), :], kd[blk(r, n - 1), :])
        for u, (r, n) in enumerate(units):
            s = s_scr[u, :, band(n)] * SCALE + b_ref[hs, :, band(n)]
            m = jnp.max(s, axis=1, keepdims=True)
            e = jnp.exp(s - m)
            den = jnp.sum(e, axis=1, keepdims=True)
            p_scr[u, :, band(n)] = (e * (1.0 / den)).astype(BF16)
            l_dst[out_rows(r, n), :] = jnp.broadcast_to(m + jnp.log(den), (BLK, HD))
        for u, (r, n) in enumerate(units):
            acc = _dot(p_scr[u, :, BLK:], vd[blk(r, n), :])
            if n > 0:
                acc = acc + _dot(p_scr[u, :, :BLK], vd[blk(r, n - 1), :])
            o_dst[out_rows(r, n), :] = acc
        if not direct:
            _to_sequence(o_ref, o_dst, d, tmp)
            _to_sequence(l_ref, l_dst, d, tmp)

    rows_f32, rows_bf16 = pltpu.VMEM((seq, HD), F32), pltpu.VMEM((seq, HD), BF16)
    regrouped = [] if d == 1 else [rows_f32] * 2 + [rows_bf16] * 3 + ([] if d <= 4 else [rows_f32] * 2)
    out_spec = pl.BlockSpec((seq, HD), lambda b, hh: (b, hh))
    return pl.pallas_call(
        body, name=f"attn_forward_{g}", out_shape=[HBM_OUT((bsz * seq, AW), F32)] * 2, grid=(bsz, 4),
        in_specs=[_head_spec(seq, g, part) for part in range(3)] + [pl.BlockSpec((4, BLK, 2 * BLK), lambda b, hh: (0, 0, 0))],
        out_specs=[out_spec, out_spec],
        scratch_shapes=[pltpu.VMEM((len(units), BLK, 2 * BLK), F32), pltpu.VMEM((len(units), BLK, 2 * BLK), BF16)] + regrouped,
        compiler_params=_cp(("parallel", "parallel"), VMEM_CAP // 2),
    )(qkv, qkv, qkv, _in_hbm(bias))


def _attn_backward(g, qkv, do, dl, bias, prev_out, bsz, seq):
    d = DILATIONS[g]
    ln = seq // d
    units = [(r, n) for r in range(d) for n in range(ln // BLK)]

    def body(q_ref, k_ref, v_ref, do_ref, dl_ref, b_ref, *rest):
        dq_ref, dk_ref, dv_ref, db_ref = rest[-18:-14]
        wide, tmp, qd, kd, vd, dod, dld, dqd, dkd, dvd, s_scr, dp_scr, p_scr, ds_scr = rest[-14:]
        hs = pl.program_id(1)

        @pl.when((pl.program_id(0) == 0) & (hs == 0))
        def _():
            db_ref[...] = jnp.zeros_like(db_ref)

        for dst, src in ((qd, q_ref), (kd, k_ref), (vd, v_ref)):
            _by_subsequence(dst, src, d, wide, tmp)
        _by_subsequence(dod, do_ref, d, None, tmp)
        _by_subsequence(dld, dl_ref, d, None, tmp)
        dkd[...] = jnp.zeros_like(dkd)
        dvd[...] = jnp.zeros_like(dvd)
        blk = lambda r, n: pl.ds(r * ln + n * BLK, BLK)
        keys = lambda r, n: [(blk(r, n), slice(BLK, 2 * BLK))] + ([(blk(r, n - 1), slice(0, BLK))] if n > 0 else [])
        for u, (r, n) in enumerate(units):
            for rows, band in keys(r, n):
                s_scr[u, :, band] = _dot_nt(qd[blk(r, n), :], kd[rows, :])
                dp_scr[u, :, band] = _dot_nt(dod[blk(r, n), :], vd[rows, :])
        for u, (r, n) in enumerate(units):
            both = dld[blk(r, n), :]
            lse, delta = both[:, 0:1], both[:, 64:65]
            band = slice(BLK, 2 * BLK) if n == 0 else slice(0, 2 * BLK)
            p = jnp.exp(s_scr[u, :, band] * SCALE + b_ref[hs, :, band] - lse)
            ds = p * (dp_scr[u, :, band] - delta)
            p_scr[u, :, band] = p.astype(BF16)
            ds_scr[u, :, band] = ds.astype(BF16)
            db_ref[hs, :, band] += ds
        for u, (r, n) in enumerate(units):
            dq = jnp.zeros((BLK, HD), F32)
            for rows, band in keys(r, n):
                dvd[rows, :] += _dot_tn(p_scr[u, :, band], dod[blk(r, n), :])
                dkd[rows, :] += _dot_tn(ds_scr[u, :, band], qd[blk(r, n), :]) * SCALE
                dq = dq + _dot(ds_scr[u, :, band], kd[rows, :])
            dqd[blk(r, n), :] = dq * SCALE
        for out, acc in ((dq_ref, dqd), (dk_ref, dkd), (dv_ref, dvd)):
            if d == 1:
                out[...] = acc[...].astype(BF16)
            else:
                _to_sequence(wide, acc, d, tmp)
                out[...] = wide[...].astype(BF16)

    qkv_spec = _head_spec(seq, g, 0)
    out_spec = pl.BlockSpec((seq, HD), lambda b, hh: (b, hh))
    band_spec = pl.BlockSpec((4, BLK, 2 * BLK), lambda b, hh: (0, 0, 0))
    ins = [qkv, qkv, qkv, _in_hbm(do), _in_hbm(dl), _in_hbm(bias)]
    in_specs = [_head_spec(seq, g, part) for part in range(3)] + [out_spec, out_spec, band_spec]
    aliases = {}
    if prev_out is not None:
        ins += list(prev_out)
        in_specs += [ANY] * 3
        aliases = {6: 0, 7: 1, 8: 2}
    rows_bf16, rows_f32 = pltpu.VMEM((seq, HD), BF16), pltpu.VMEM((seq, HD), F32)
    staged = [pltpu.VMEM((len(units), BLK, 2 * BLK), F32)] * 2 + [pltpu.VMEM((len(units), BLK, 2 * BLK), BF16)] * 2
    dq, dk, dv, db = pl.pallas_call(
        body, name=f"attn_backward_{g}", out_shape=[HBM_OUT((bsz * seq, QW), BF16)] * 3 + [SDS((4, BLK, 2 * BLK), F32)], grid=(bsz, 4),
        in_specs=in_specs, out_specs=[qkv_spec] * 3 + [band_spec], input_output_aliases=aliases,
        scratch_shapes=[rows_f32] * 2 + [rows_bf16] * 4 + [rows_f32] * 4 + staged,
        compiler_params=_cp(("arbitrary", "arbitrary"), VMEM_CAP // 2),
    )(*ins)
    return (dq, dk, dv), db


def _mix_forward(gates, og, lg, x2, tgt, gate, w_ao, w_co, w_o, conv_w, conv_b, ln_g, ln_b, bsz, seq, tm=256):
    t = x2.shape[0]
    spt = seq // tm

    def body(g_ref, o1, o2, o3, l1, l2, l3, x_ref, t_ref, gate_ref, wao_ref, wco_ref, wo_ref, cw_ref, cb_ref, lng_ref, lnb_ref,
             ain_ref, sin_ref, mrg_ref, dy_ref, aout_ref, sout_ref, yc_ref, o_ref, lj_ref, dxr_ref, vec_ref, dgate_ref, zc_ref):
        b, i = pl.program_id(0), pl.program_id(1)

        @pl.when((b == 0) & (i == 0))
        def _():
            vec_ref[...] = jnp.zeros_like(vec_ref)

        @pl.when(i == 0)
        def _():
            zc_ref[...] = jnp.zeros_like(zc_ref)
            dgate_ref[...] = jnp.zeros_like(dgate_ref)

        g_attn, u, bg, cg, g_conv, m_attn, m_conv = (g_ref[:, lo:hi].astype(F32) for lo, hi in GATE_COLS)
        la, lb, lc = l1[...], l2[...], l3[...]
        mx = jnp.maximum(la, jnp.maximum(lb, lc))
        ea, eb, ec = jnp.exp(la - mx), jnp.exp(lb - mx), jnp.exp(lc - mx)
        den = ea + eb + ec
        o = (ea * o1[...] + eb * o2[...] + ec * o3[...]) / den
        o_ref[...] = o
        lj_ref[...] = mx + jnp.log(den)
        a_in = o * (g_attn * _sig(g_attn))
        ain_ref[...] = a_in.astype(BF16)
        a_out = _dot(a_in.astype(BF16), wao_ref[...])
        aout_ref[...] = a_out.astype(BF16)
        z = cg * u
        rows = lax.broadcasted_iota(jnp.int32, (tm, D), 0)
        c6, c7 = zc_ref[6:7, :], zc_ref[7:8, :]
        z1 = jnp.where(rows == 0, c7, pltpu.roll(z, 1, 0))
        z2 = jnp.where(rows == 0, c6, jnp.where(rows == 1, c7, pltpu.roll(z, 2, 0)))
        zc_ref[...] = z[tm - 8:tm, :]
        y_conv = (cw_ref[0:1, :] * z2 + cw_ref[1:2, :] * z1 + cw_ref[2:3, :] * z) + cb_ref[...]
        yc_ref[...] = y_conv.astype(BF16)
        s_in = bg * y_conv * (g_conv * _sig(g_conv))
        sin_ref[...] = s_in.astype(BF16)
        s_out = _dot(s_in.astype(BF16), wco_ref[...])
        sout_ref[...] = s_out.astype(BF16)
        merged = _sig(m_attn) * a_out + _sig(m_conv) * s_out
        mrg_ref[...] = merged.astype(BF16)
        y = _dot(merged.astype(BF16), wo_ref[...])
        gate1 = 1.0 + gate_ref[0]
        r = ALPHA * x_ref[...] + gate1 * y
        mu = jnp.mean(r, axis=1, keepdims=True)
        rc = r - mu
        rstd = lax.rsqrt(jnp.mean(rc * rc, axis=1, keepdims=True) + LN_EPS)
        xhat = rc * rstd
        diff = (xhat * lng_ref[...] + lnb_ref[...]) - t_ref[...]
        dout = diff * (1.0 / D)
        vec_ref[0:1, :] += jnp.sum(dout * xhat, axis=0, keepdims=True)
        vec_ref[1:2, :] += jnp.sum(dout, axis=0, keepdims=True)
        vec_ref[2:3, :] += jnp.sum(diff * diff, axis=0, keepdims=True)
        dxh = dout * lng_ref[...]
        dr = rstd * (dxh - jnp.mean(dxh, axis=1, keepdims=True) - xhat * jnp.mean(dxh * xhat, axis=1, keepdims=True))
        dxr_ref[...] = ALPHA * dr
        dy_ref[...] = (dr * gate1).astype(BF16)
        dgate_ref[0] += jnp.sum(dr * y, axis=0, keepdims=True)

    tok = lambda w: pl.BlockSpec((tm, w), lambda b, i: (b * spt + i, 0))
    const = lambda s: pl.BlockSpec(s, lambda b, i: (0,) * len(s))
    per_seq = pl.BlockSpec((1, 1, D), lambda b, i: (b, 0, 0))
    outs = pl.pallas_call(
        body, name="mix_forward", grid=(bsz, spt),
        out_shape=[HBM_OUT((t, AW), BF16), HBM_OUT((t, D), BF16), HBM_OUT((t, D), BF16), HBM_OUT((t, D), BF16), HBM_OUT((t, D), BF16),
                   HBM_OUT((t, D), BF16), HBM_OUT((t, D), BF16), HBM_OUT((t, AW), F32), HBM_OUT((t, AW), F32), HBM_OUT((t, D), F32),
                   SDS((8, D), F32), SDS((bsz, 1, D), F32)],
        in_specs=[tok(NGATE)] + [tok(AW)] * 6 + [tok(D), tok(D), per_seq, const((AW, D)), const((D, D)), const((D, D)),
                                                 const((3, D)), const((1, D)), const((1, D)), const((1, D))],
        out_specs=[tok(AW), tok(D), tok(D), tok(D), tok(D), tok(D), tok(D), tok(AW), tok(AW), tok(D), const((8, D)), per_seq],
        scratch_shapes=[pltpu.VMEM((8, D), F32)],
        compiler_params=_cp(("arbitrary", "arbitrary"), VMEM_CAP),
    )(_in_hbm(gates), *map(_in_hbm, og), *map(_in_hbm, lg), _in_hbm(x2), _in_hbm(tgt), gate, w_ao, w_co, w_o, conv_w, conv_b, ln_g, ln_b)
    return outs


def _mix_backward(gates, dy, a_out, s_out, y_conv, o, lj, w_ao, w_co, w_o, conv_w, vec_f, bsz, seq, tm=256):
    t = dy.shape[0]
    spt = seq // tm

    def body(g_ref, dy_ref, aout_ref, sout_ref, yc_ref, o_ref, lj_ref, wao_ref, wco_ref, wo_ref, cw_ref, vecf_ref,
             dg_ref, do_ref, dl_ref, daout_ref, dsout_ref, vec_ref, car_ref):
        b, i = pl.program_id(0), pl.program_id(1)

        @pl.when((b == 0) & (i == 0))
        def _():
            vec_ref[...] = vecf_ref[...]

        @pl.when(i == 0)
        def _():
            car_ref[...] = jnp.zeros_like(car_ref)

        g_attn, u, bg, cg, g_conv, m_attn, m_conv = (g_ref[:, lo:hi].astype(F32) for lo, hi in GATE_COLS)
        dmerged = _dot_nt(dy_ref[...], wo_ref[...])
        sa, sc = _sig(m_attn), _sig(m_conv)
        da_out = (dmerged * sa).astype(BF16)
        ds_out = (dmerged * sc).astype(BF16)
        daout_ref[...] = da_out
        dsout_ref[...] = ds_out
        dg_ref[:, 4608:5632] = (dmerged * aout_ref[...].astype(F32) * (sa * (1.0 - sa))).astype(BF16)
        dg_ref[:, 5632:6656] = (dmerged * sout_ref[...].astype(F32) * (sc * (1.0 - sc))).astype(BF16)
        da_in = _dot_nt(da_out, wao_ref[...])
        ds_in = _dot_nt(ds_out, wco_ref[...])
        sga = _sig(g_attn)
        o = o_ref[...]
        do = da_in * (g_attn * sga)
        do_ref[...] = do
        dg_ref[:, 0:512] = (da_in * o * (sga * (1.0 + g_attn * (1.0 - sga)))).astype(BF16)
        prod = do * o
        lane = lax.broadcasted_iota(jnp.int32, (tm, HD), 1)
        for j in range(4):
            cs = slice(j * HD, (j + 1) * HD)
            delta = jnp.sum(prod[:, cs], axis=1, keepdims=True)
            dl_ref[:, cs] = jnp.where(lane < 64, lj_ref[:, cs], delta)
        sgc = _sig(g_conv)
        silu_c = g_conv * sgc
        yc = yc_ref[...].astype(F32)
        dg_ref[:, 1536:2560] = (ds_in * yc * silu_c).astype(BF16)
        dg_ref[:, 3584:4608] = (ds_in * bg * yc * (sgc * (1.0 + g_conv * (1.0 - sgc)))).astype(BF16)
        dyc = ds_in * bg * silu_c
        rows = lax.broadcasted_iota(jnp.int32, (tm, D), 0)
        c0, c1 = car_ref[0:1, :], car_ref[1:2, :]
        n1 = jnp.where(rows == tm - 1, c0, pltpu.roll(dyc, tm - 1, 0))
        n2 = jnp.where(rows == tm - 2, c0, jnp.where(rows == tm - 1, c1, pltpu.roll(dyc, tm - 2, 0)))
        car_ref[...] = dyc[0:8, :]
        dz = cw_ref[2:3, :] * dyc + cw_ref[1:2, :] * n1 + cw_ref[0:1, :] * n2
        z = cg * u
        dg_ref[:, 512:1536] = (dz * cg).astype(BF16)
        dg_ref[:, 2560:3584] = (dz * u).astype(BF16)
        vec_ref[3:4, :] += jnp.sum(n2 * z, axis=0, keepdims=True)
        vec_ref[4:5, :] += jnp.sum(n1 * z, axis=0, keepdims=True)
        vec_ref[5:6, :] += jnp.sum(dyc * z, axis=0, keepdims=True)
        vec_ref[6:7, :] += jnp.sum(dyc, axis=0, keepdims=True)

    tok = lambda w: pl.BlockSpec((tm, w), lambda b, i: (b * spt + (spt - 1 - i), 0))
    const = lambda s: pl.BlockSpec(s, lambda b, i: (0,) * len(s))
    return pl.pallas_call(
        body, name="mix_backward", grid=(bsz, spt),
        out_shape=[HBM_OUT((t, NGATE), BF16), HBM_OUT((t, AW), F32), HBM_OUT((t, AW), F32), HBM_OUT((t, D), BF16), HBM_OUT((t, D), BF16),
                   SDS((8, D), F32)],
        in_specs=[tok(NGATE), tok(D), tok(D), tok(D), tok(D), tok(AW), tok(AW), const((AW, D)), const((D, D)), const((D, D)), const((3, D)),
                  const((8, D))],
        out_specs=[tok(NGATE), tok(AW), tok(AW), tok(D), tok(D), const((8, D))],
        scratch_shapes=[pltpu.VMEM((8, D), F32)],
        compiler_params=_cp(("arbitrary", "arbitrary"), VMEM_CAP),
    )(*map(_in_hbm, (gates, dy, a_out, s_out, y_conv, o, lj)), w_ao, w_co, w_o, conv_w, vec_f)


def _scatter_copies(src, land, send_sems, recv_sems):
    x, y, c = _place()
    chips = [(1 - x, y), (x, 1 - y), (1 - x, 1 - y)]
    return [pltpu.make_async_remote_copy(src_ref=src[a].at[2 * cx + cy], dst_ref=land[a].at[r], send_sem=send_sems.at[3 * a + r],
                                         recv_sem=recv_sems.at[3 * a + r], device_id=(cx, cy, c), device_id_type=MESH)
            for a in range(len(src)) for r, (cx, cy) in enumerate(chips)]


def _halves_out(a):
    kind, nr, nc = W_CUTS[a]
    shape = (nr // 2, W_FULL[a][1]) if kind == "col" else (NCHIP, nr // 2, nc)
    return [SDS(shape, F32), SDS(shape, BF16)]


def _write_halves(a, acc_ref, c, mine_ref, theirs_ref):
    kind, nr, nc = W_CUTS[a]
    hr = nr // 2
    if kind == "col":
        mine_ref[...] = acc_ref[pl.ds(pl.multiple_of(c * hr, hr), hr), :]
        theirs_ref[...] = acc_ref[pl.ds(pl.multiple_of((1 - c) * hr, hr), hr), :].astype(BF16)
    else:
        for k in range(NCHIP):
            mine_ref[k] = acc_ref[pl.ds(pl.multiple_of(k * nr + c * hr, hr), hr), :]
            theirs_ref[k] = acc_ref[pl.ds(pl.multiple_of(k * nr + (1 - c) * hr, hr), hr), :].astype(BF16)


def _out_weight_grads(a_in, da_out, s_in, ds_out, merged, dy, core, tk=512):
    t = dy.shape[0]
    nt = t // tk

    def body(c_ref, ain_ref, da_ref, sin_ref, ds_ref, m_ref, dy_ref, *rest):
        outs, (gao, gco, go) = rest[:6], rest[6:]

        @pl.when(pl.program_id(0) == 0)
        def _():
            gao[...] = jnp.zeros_like(gao)
            gco[...] = jnp.zeros_like(gco)
            go[...] = jnp.zeros_like(go)

        gao[...] += _dot_tn(ain_ref[...], da_ref[...])
        gco[...] += _dot_tn(sin_ref[...], ds_ref[...])
        go[...] += _dot_tn(m_ref[...], dy_ref[...])

        @pl.when(pl.program_id(0) == nt - 1)
        def _():
            for a, acc in ((1, gao), (2, gco), (3, go)):
                _write_halves(a, acc, c_ref[0], outs[2 * a - 2], outs[2 * a - 1])

    tok = lambda w: pl.BlockSpec((tk, w), lambda i, cr: (i, 0))
    out_shape = _halves_out(1) + _halves_out(2) + _halves_out(3)
    outs = pl.pallas_call(
        body, name="out_weight_grads", out_shape=out_shape,
        grid_spec=pltpu.PrefetchScalarGridSpec(
            num_scalar_prefetch=1, grid=(nt,), in_specs=[tok(AW), tok(D), tok(D), tok(D), tok(D), tok(D)],
            out_specs=[pl.BlockSpec(o.shape, lambda i, cr, nd=len(o.shape): (0,) * nd) for o in out_shape],
            scratch_shapes=[pltpu.VMEM((AW, D), F32), pltpu.VMEM((D, D), F32), pltpu.VMEM((D, D), F32)]),
        compiler_params=_cp(("arbitrary",), VMEM_CAP),
    )(core, a_in, da_out, s_in, ds_out, merged, dy)
    return [(outs[0], outs[1]), (outs[2], outs[3]), (outs[4], outs[5])]


def _input_grad(dq, dk, dv, dgates, w, x2, dxr, sc1p, seq, sums, tm=512):
    t = x2.shape[0]
    nt = t // tm
    spt = seq // tm
    bsz = t // seq
    n = len(sums)
    gblk = NGATE // 4
    nsteps = 3 + 4

    def body(dq_ref, dk_ref, dv_ref, dg_ref, wq_ref, wg_ref, x_ref, dxr_ref, sc_ref, *rest):
        src, (dx_ref, dsh_ref, dsc_ref), land = rest[:n], rest[n:n + 3], rest[n + 3:2 * n + 3]
        acc_ref, send_sems, recv_sems = rest[2 * n + 3:]
        j, i = pl.program_id(0), pl.program_id(1)
        copies = _scatter_copies(src, land, send_sems, recv_sems)
        rows = pl.ds(pl.multiple_of(i * tm, tm), tm)

        @pl.when((i == 0) & (j == 0))
        def _():
            for cp in copies:
                cp.start()

        for k, ref in enumerate((dq_ref, dk_ref, dv_ref)):
            @pl.when(j == k)
            def _(k=k, ref=ref):
                part = _dot_nt(ref[...], wq_ref[...])
                if k == 0:
                    acc_ref[rows, :] = part
                else:
                    acc_ref[rows, :] += part

        @pl.when((j >= 3) & (j < nsteps - 1))
        def _():
            acc_ref[rows, :] += _dot_nt(dg_ref[...], wg_ref[...])

        @pl.when(j == nsteps - 1)
        def _():
            dh = acc_ref[rows, :] + _dot_nt(dg_ref[...], wg_ref[...])
            dx_ref[...] = dh * sc_ref[0] + dxr_ref[...]

            @pl.when(i % spt == 0)
            def _():
                dsh_ref[...] = jnp.zeros_like(dsh_ref)
                dsc_ref[...] = jnp.zeros_like(dsc_ref)

            dsh_ref[0] += jnp.sum(dh, axis=0, keepdims=True)
            dsc_ref[0] += jnp.sum(dh * x_ref[...], axis=0, keepdims=True)

        @pl.when((i == nt - 1) & (j == nsteps - 1))
        def _():
            for cp in copies:
                cp.wait()

    def held(k):
        return lambda j, i: (jnp.where(j == k, i, jnp.where(j < k, 0, nt - 1)), 0)

    last = lambda j, i: (jnp.where(j == nsteps - 1, i, 0), 0)
    outs = pl.pallas_call(
        body, name="input_grad", grid=(nsteps, nt),
        out_shape=[SDS((t, D), F32), SDS((bsz, 1, D), F32), SDS((bsz, 1, D), F32)] + [SDS((3,) + s.shape[1:], BF16) for s in sums],
        in_specs=[pl.BlockSpec((tm, QW), held(0)), pl.BlockSpec((tm, QW), held(1)), pl.BlockSpec((tm, QW), held(2)),
                  pl.BlockSpec((tm, gblk), lambda j, i: (jnp.where(j >= 3, i, 0), jnp.clip(j - 3, 0, 3))),
                  pl.BlockSpec((D, QW), lambda j, i: (0, jnp.minimum(j, 2))),
                  pl.BlockSpec((pl.Element(D), pl.Element(gblk)), lambda j, i: (0, pl.multiple_of(3 * QW + gblk * jnp.clip(j - 3, 0, 3), 128))),
                  pl.BlockSpec((tm, D), last), pl.BlockSpec((tm, D), last),
                  pl.BlockSpec((1, 1, D), lambda j, i: (jnp.where(j == nsteps - 1, i // spt, 0), 0, 0))] + [ANY] * n,
        out_specs=[pl.BlockSpec((tm, D), last),
                   pl.BlockSpec((1, 1, D), lambda j, i: (jnp.where(j == nsteps - 1, i // spt, 0), 0, 0)),
                   pl.BlockSpec((1, 1, D), lambda j, i: (jnp.where(j == nsteps - 1, i // spt, 0), 0, 0))] + [ANY] * n,
        scratch_shapes=[pltpu.VMEM((t, D), F32), pltpu.SemaphoreType.DMA((3 * NCHIP,)), pltpu.SemaphoreType.DMA((3 * NCHIP,))],
        compiler_params=_cp(("arbitrary", "arbitrary"), VMEM_CAP, side=True),
    )(*map(_in_hbm, (dq, dk, dv, dgates, w, w, x2, dxr)), sc1p, *sums)
    return outs[0], outs[1], outs[2], outs[3:]


def _in_weight_grad(ht, dq, dk, dv, dgates, core, sums):
    t = ht.shape[1]
    hr = D // 2
    n = len(sums)

    def body(c_ref, ht_ref, dq_ref, dk_ref, dv_ref, dg_ref, *rest):
        src, (mine_ref, theirs_ref), land = rest[:n], rest[n:n + 2], rest[n + 2:2 * n + 2]
        acc_ref, send_sems, recv_sems = rest[2 * n + 2:]
        j = pl.program_id(0)
        copies = _scatter_copies(src, land, send_sems, recv_sems)

        @pl.when(j == 0)
        def _():
            for cp in copies:
                cp.start()

        for k, ref in enumerate((dq_ref, dk_ref, dv_ref)):
            @pl.when((j >= k * NQT) & (j < (k + 1) * NQT))
            def _(ref=ref):
                acc_ref[...] = _dot(ht_ref[...], ref[...])

        @pl.when(j >= 3 * NQT)
        def _():
            acc_ref[...] = _dot(ht_ref[...], dg_ref[...])

        _write_halves(0, acc_ref, c_ref[0], mine_ref, theirs_ref)

        @pl.when(j == NPT - 1)
        def _():
            for cp in copies:
                cp.wait()

    def part(k):
        return pl.BlockSpec((t, TN), lambda j, cr: (0, jnp.clip(j - k * NQT, 0, NQT - 1)))

    out_spec = pl.BlockSpec((hr, TN), lambda j, cr: (0, j))
    outs = pl.pallas_call(
        body, name="in_weight_grad", out_shape=[SDS((hr, NCOL), F32), SDS((hr, NCOL), BF16)] + [SDS((3,) + v.shape[1:], BF16) for v in sums],
        grid_spec=pltpu.PrefetchScalarGridSpec(
            num_scalar_prefetch=1, grid=(NPT,),
            in_specs=[pl.BlockSpec((D, t), lambda j, cr: (0, 0)), part(0), part(1), part(2),
                      pl.BlockSpec((t, TN), lambda j, cr: (0, jnp.maximum(j - 3 * NQT, 0)))] + [ANY] * n,
            out_specs=[out_spec, out_spec] + [ANY] * n,
            scratch_shapes=[pltpu.VMEM((D, TN), F32), pltpu.SemaphoreType.DMA((3 * NCHIP,)), pltpu.SemaphoreType.DMA((3 * NCHIP,))]),
        compiler_params=_cp(("arbitrary",), VMEM_CAP, side=True),
    )(core, *map(_in_hbm, (ht, dq, dk, dv, dgates)), *sums)
    return outs[0], outs[1], outs[2:]


def _sum_partials(gathered):
    def body(g_ref, o_ref):
        acc = g_ref[0]
        for k in range(1, 8):
            acc = acc + g_ref[k]
        o_ref[...] = acc

    return pl.pallas_call(body, name="sum_partials", out_shape=SDS(gathered.shape[1:], F32), in_specs=[VMEM_SPEC], out_specs=VMEM_SPEC)(gathered)


def _adamw(w, g, m, v, name, tr=256):
    r, cdim = w.shape
    tr = tr if cdim <= D else tr // 2
    tr = tr if (r % tr == 0 and r > tr) else r

    def body(w_ref, g_ref, m_ref, v_ref, go_ref, d_ref, nm_ref, nv_ref):
        gv = g_ref[...]
        go_ref[...] = gv
        nm = B1 * m_ref[...] + (1.0 - B1) * gv
        nv = B2 * v_ref[...] + (1.0 - B2) * (gv * gv)
        m_hat = nm / (1.0 - B1 ** STEP)
        v_hat = nv / (1.0 - B2 ** STEP)
        d_ref[...] = -LR * (m_hat / (jnp.sqrt(v_hat) + EPS) + WD * w_ref[...])
        nm_ref[...] = nm
        nv_ref[...] = nv

    spec = pl.BlockSpec((tr, cdim), lambda i: (i, 0))
    return pl.pallas_call(
        body, name=name, grid=(r // tr,), out_shape=[SDS((r, cdim), F32)] * 4, in_specs=[spec] * 4, out_specs=[spec] * 4,
        compiler_params=_cp(("parallel",), VMEM_CAP // 2),
    )(*map(_in_hbm, (w, g, m, v)))


def _t5_bucket(dist):
    n = jnp.maximum(dist, 1).astype(F32)
    large = MAX_EXACT + (jnp.log(n / MAX_EXACT) / math.log(MAX_DISTANCE / MAX_EXACT) * (N_BUCKETS - MAX_EXACT)).astype(jnp.int32)
    large = jnp.minimum(large, N_BUCKETS - 1)
    return jnp.where(dist < MAX_EXACT, dist, large)


def _band_buckets():
    a = jnp.arange(BLK)[:, None]
    b = jnp.arange(2 * BLK)[None, :]
    steps = jnp.maximum(a + BLK - b, 0)
    return jnp.stack([_t5_bucket(steps * d) for d in DILATIONS]).astype(jnp.int32)


def _pad_rows(a, rows=8):
    return jnp.pad(a, ((0, rows - a.shape[0]), (0, 0)))


def kernel(x, c, w_ada, b_ada, w_in, conv_w, conv_b, rel_bias, w_attn_out, w_conv_out, w_o, ln_g, ln_b, loss_target, m_w_ada, m_b_ada, m_w_in, m_conv_w, m_conv_b, m_rel_bias, m_w_attn_out, m_w_conv_out, m_w_o, m_ln_g, m_ln_b, v_w_ada, v_b_ada, v_w_in, v_conv_w, v_conv_b, v_rel_bias, v_w_attn_out, v_w_conv_out, v_w_o, v_ln_g, v_ln_b):
    bsz, seq, _ = x.shape
    t = bsz * seq
    mx, my, mc = _place()
    chip = 2 * mx + my
    dev = 4 * mx + 2 * my + mc
    x2 = x.reshape(t, D)
    tgt = loss_target.reshape(t, D)

    mine = _to_bf16_windows([w[0] for w in (w_in, w_attn_out, w_conv_out, w_o)])

    n_ada = w_ada.shape[2]
    n_cw = conv_w.shape[2]
    c_and_cw = jnp.concatenate([_pad_rows(c), jnp.pad(conv_w[0], ((0, 5), (0, D - n_cw)))], axis=0)
    firsts = _all_gather8(c_and_cw, "gather_c_conv_w")
    c_all = firsts[:, 0:bsz, :].reshape(8 * bsz, D)
    conv_w_f = firsts[0::2, 8:11, 0:n_cw].transpose(1, 0, 2).reshape(3, D)
    b_cols = lax.dynamic_slice(b_ada, (0, chip * n_ada), (1, n_ada))
    mod_part = _ada_forward(c_all, w_ada[0], b_cols)
    mod_parts = _all_gather8(mod_part, "gather_mod")
    mod_all = mod_parts[0::2].transpose(1, 0, 2).reshape(8 * bsz, 3 * D)
    mod = lax.dynamic_slice(mod_all, (dev * bsz, 0), (bsz, 3 * D))
    shift = mod[:, 0:D].reshape(bsz, 1, D)
    sc1p = 1.0 + mod[:, D:2 * D].reshape(bsz, 1, D)
    gate = mod[:, 2 * D:].reshape(bsz, 1, D)

    h, ht = _modulate(x2, sc1p, shift, seq)
    tab = lax.dynamic_index_in_dim(jnp.asarray(_tile_tables()), chip, 0, keepdims=False)
    qkv, gates, (w_in_f, w_ao_f, w_co_f, w_o_f) = _project_gather(h, mine, tab)
    buckets = _band_buckets()
    bias = _bias_tables(rel_bias, buckets)
    og, lg = [], []
    for g in range(3):
        o_g, l_g = _attn_forward(g, qkv, bias[g], bsz, seq)
        og.append(o_g)
        lg.append(l_g)
    (a_in, s_in, merged, dy, a_out, s_out, y_conv, o, lj, dxr, vec_f, dgate) = _mix_forward(
        gates, og, lg, x2, tgt, gate, w_ao_f, w_co_f, w_o_f, conv_w_f, conv_b, ln_g, ln_b, bsz, seq)

    dgates, do, dl, da_out, ds_out, vec = _mix_backward(gates, dy, a_out, s_out, y_conv, o, lj, w_ao_f, w_co_f, w_o_f, conv_w_f, vec_f, bsz, seq)
    core = jnp.reshape(mc, (1,)).astype(jnp.int32)
    small_grads = _out_weight_grads(a_in, da_out, s_in, ds_out, merged, dy, core)
    got_small = _swap_halves([theirs for _, theirs in small_grads], "swap_small_grad_halves")
    sums_small = _chip_sums([own for own, _ in small_grads], got_small, 1, "chip_sums_small")
    dqkv, dbs = None, []
    for g in range(3):
        dqkv, db = _attn_backward(g, qkv, do, dl, bias[g], dqkv, bsz, seq)
        dbs.append(db)
    dq, dk, dv = dqkv
    drb = _bias_grad(jnp.stack(dbs), buckets)
    drb = drb[:, :, 0:4].transpose(1, 0, 2).reshape(N_BUCKETS, 12)
    g_in_mine, g_in_theirs, landed_small = _in_weight_grad(ht, dq, dk, dv, dgates, core, [bf for _, bf in sums_small])
    got_in = _swap_halves([g_in_theirs], "swap_in_grad_halves")
    sums_in = _chip_sums([g_in_mine], got_in, 0, "chip_sums_in")
    grad_x, dshift, dscale, landed_in = _input_grad(dq, dk, dv, dgates, w_in_f, x2, dxr, sc1p, seq, [bf for _, bf in sums_in])
    halves = _reduce_mine([own for own, _ in sums_in + sums_small], list(landed_in) + list(landed_small))
    gw_in, gw_ao, gw_co, gw_o = _join_halves(halves)

    dmod = jnp.concatenate([dshift, dscale, dgate], axis=2).reshape(bsz * 3, D)
    drb_row = jnp.pad(drb.reshape(1, N_BUCKETS * 12), ((0, 0), (0, D - N_BUCKETS * 12)))
    vec = lax.dynamic_update_slice(vec, drb_row, (7, 0))
    packed = jnp.concatenate([vec, _pad_rows(dmod)], axis=0)
    gathered = _all_gather8(packed, "gather_small")
    small = _sum_partials(gathered)
    g_ln_g, g_ln_b, loss_lanes = small[0:1], small[1:2], small[2:3]
    g_conv_w_full, g_conv_b = small[3:6], small[6:7]
    g_rel_bias = small[7, 0:N_BUCKETS * 12].reshape(N_BUCKETS, 12)
    loss = 0.5 / D * jnp.sum(loss_lanes)
    dmod_all = gathered[:, 8:8 + 3 * bsz, :].reshape(8 * bsz, 3 * D)
    dmod_cols = lax.dynamic_slice(dmod_all, (0, chip * n_ada), (8 * bsz, n_ada))
    gw_ada, gb_ada = _ada_backward(c_all, dmod_cols, dmod_all)
    g_conv_w = lax.dynamic_slice(g_conv_w_full, (0, chip * n_cw), (3, n_cw))

    names = ["w_ada", "b_ada", "w_in", "conv_w", "conv_b", "rel_bias", "w_attn_out", "w_conv_out", "w_o", "ln_g", "ln_b"]
    two_d = lambda a: a.reshape(a.shape[-2:]) if a.ndim == 3 else a
    weights = dict(zip(names, map(two_d, (w_ada, b_ada, w_in, conv_w, conv_b, rel_bias, w_attn_out, w_conv_out, w_o, ln_g, ln_b))))
    ms = dict(zip(names, map(two_d, (m_w_ada, m_b_ada, m_w_in, m_conv_w, m_conv_b, m_rel_bias, m_w_attn_out, m_w_conv_out, m_w_o, m_ln_g, m_ln_b))))
    vs = dict(zip(names, map(two_d, (v_w_ada, v_b_ada, v_w_in, v_conv_w, v_conv_b, v_rel_bias, v_w_attn_out, v_w_conv_out, v_w_o, v_ln_g, v_ln_b))))
    grads = dict(zip(names, (gw_ada, gb_ada, gw_in, g_conv_w, g_conv_b, g_rel_bias, gw_ao, gw_co, gw_o, g_ln_g, g_ln_b)))
    shapes = dict(zip(names, (w_ada, b_ada, w_in, conv_w, conv_b, rel_bias, w_attn_out, w_conv_out, w_o, ln_g, ln_b)))
    grad_out, deltas, new_m, new_v = {}, {}, {}, {}
    for n in names:
        grad_out[n], deltas[n], new_m[n], new_v[n] = _adamw(weights[n], grads[n], ms[n], vs[n], f"adamw_{n}")
    shaped = lambda d: [d[n].reshape(shapes[n].shape) for n in names]
    return (loss, grad_x.reshape(bsz, seq, D), *shaped(grad_out), *shaped(deltas), *shaped(new_m), *shaped(new_v))
```

```python
import math

import numpy as np
import jax
import jax.numpy as jnp
from jax import lax
from jax.experimental import pallas as pl
from jax.experimental.pallas import tpu as pltpu

F32 = jnp.float32
BF16 = jnp.bfloat16
SDS = jax.ShapeDtypeStruct
MESH = pl.DeviceIdType.MESH
HBM_OUT = pltpu.HBM
ANY = pl.BlockSpec(memory_space=pl.ANY)
VMEM_SPEC = pl.BlockSpec(memory_space=pltpu.VMEM)

D = 1024
HD = 128
BLK = 128
QW = 1536
AW = 512
NGATE = 6656
GATE_COLS = ((0, 512), (512, 1536), (1536, 2560), (2560, 3584), (3584, 4608), (4608, 5632), (5632, 6656))
NCOL = 3 * QW + NGATE
TN = 512
NQT = QW // TN
NPT = NCOL // TN
DILATIONS = (1, 4, 16)
N_BUCKETS, MAX_EXACT, MAX_DISTANCE = 32, 16, 2048
ALPHA = 2.0 ** 0.25
LN_EPS = 1e-5
NEG = -1e30
SCALE = HD ** -0.5
LR, B1, B2, EPS, WD, STEP = 0.001, 0.9, 0.999, 1e-08, 0.01, 10
NCHIP = 4
VMEM_CAP = 60 * 2 ** 20


def _cp(sem=None, vmem=None, side=False):
    return pltpu.CompilerParams(dimension_semantics=sem, vmem_limit_bytes=vmem, has_side_effects=side)


def _dot(a, b):
    return jnp.dot(a, b, preferred_element_type=F32)


def _dot_nt(a, b):
    return lax.dot_general(a, b, (((1,), (1,)), ((), ())), preferred_element_type=F32)


def _dot_tn(a, b):
    return lax.dot_general(a, b, (((0,), (0,)), ((), ())), preferred_element_type=F32)


def _sig(x):
    return 1.0 / (1.0 + jnp.exp(-x))


def _in_hbm(a):
    return pltpu.with_memory_space_constraint(a, pltpu.HBM)


def _place():
    x, y, c = lax.axis_index("x"), lax.axis_index("y"), lax.axis_index("c")
    return x, y, c


def _all_gather8(v, name):
    r, cdim = v.shape

    def body(v_ref, out_ref, send_sems, recv_sems, local_sem):
        x, y, c = _place()
        me = 4 * x + 2 * y + c
        peers = [(x, y, 1 - c), (1 - x, y, c), (x, 1 - y, c), (1 - x, 1 - y, c),
                 (1 - x, y, 1 - c), (x, 1 - y, 1 - c), (1 - x, 1 - y, 1 - c)]
        mine = pltpu.make_async_copy(v_ref, out_ref.at[me], local_sem)
        mine.start()

        def copy(k, block, to):
            return pltpu.make_async_remote_copy(src_ref=v_ref, dst_ref=out_ref.at[block], send_sem=send_sems.at[k],
                                                recv_sem=recv_sems.at[k], device_id=to, device_id_type=MESH)

        sends = [copy(k, me, p) for k, p in enumerate(peers)]
        for cp in sends:
            cp.start()
        for k, (px, py, pc) in enumerate(peers):
            copy(k, 4 * px + 2 * py + pc, (px, py, pc)).wait_recv()
        for cp in sends:
            cp.wait_send()
        mine.wait()

    return pl.pallas_call(
        body, name=name, out_shape=SDS((8, r, cdim), v.dtype), in_specs=[VMEM_SPEC], out_specs=VMEM_SPEC,
        scratch_shapes=[pltpu.SemaphoreType.DMA((7,)), pltpu.SemaphoreType.DMA((7,)), pltpu.SemaphoreType.DMA(())],
        compiler_params=_cp(side=True),
    )(v)


W_CUTS = (("col", D, NCOL // NCHIP), ("col", AW, D // NCHIP), ("row", D // NCHIP, D), ("row", D // NCHIP, D))
W_FULL = ((D, NCOL), (AW, D), (D, D), (D, D))


def _shard_window(ref, cut, k, half):
    kind, nr, nc = cut
    hr = nr // 2
    if kind == "col":
        rows = pl.ds(0, nr) if half is None else pl.ds(pl.multiple_of(half * hr, 16), hr)
        return ref.at[rows, pl.ds(pl.multiple_of(k * nc, 128), nc)]
    if half is None:
        return ref.at[pl.ds(pl.multiple_of(k * nr, 16), nr), :]
    return ref.at[pl.ds(pl.multiple_of(k * nr + half * hr, 16), hr), :]


def _half_rows(ref, cut, half):
    hr = cut[1] // 2
    return ref.at[pl.ds(pl.multiple_of(half * hr, 16), hr), :]


def _to_bf16_windows(ws):
    x, y, _ = _place()
    chip = jnp.reshape(2 * x + y, (1,)).astype(jnp.int32)
    tr = 256
    n = len(ws)

    def body(c_ref, *refs):
        src, dst = refs[:n], refs[n:]
        dst[0][...] = src[0][...].astype(BF16)

        @pl.when(pl.program_id(0) == 0)
        def _():
            for a in range(1, n):
                dst[a][...] = src[a][...].astype(BF16)

    in_specs = [pl.BlockSpec((tr, W_CUTS[0][2]), lambda i, cr: (i, 0))]
    out_specs = [pl.BlockSpec((tr, W_CUTS[0][2]), lambda i, cr: (i, cr[0]))]
    for a in range(1, n):
        kind, nr, nc = W_CUTS[a]
        in_specs.append(pl.BlockSpec((nr, nc), lambda i, cr: (0, 0)))
        out_specs.append(pl.BlockSpec((nr, nc), (lambda i, cr: (0, cr[0])) if kind == "col" else (lambda i, cr: (cr[0], 0))))
    return pl.pallas_call(
        body, name="to_bf16", out_shape=[SDS(W_FULL[a], BF16) for a in range(n)],
        grid_spec=pltpu.PrefetchScalarGridSpec(num_scalar_prefetch=1, grid=(D // tr,), in_specs=in_specs, out_specs=out_specs),
        compiler_params=_cp(("arbitrary",)),
    )(chip, *ws)


def _swap_halves(theirs, name):
    n = len(theirs)

    def body(*refs):
        src, land = refs[:n], refs[n:2 * n]
        send_sems, recv_sems = refs[2 * n:]
        x, y, c = _place()
        copies = [pltpu.make_async_remote_copy(src_ref=src[a], dst_ref=land[a], send_sem=send_sems.at[a], recv_sem=recv_sems.at[a],
                                               device_id=(x, y, 1 - c), device_id_type=MESH) for a in range(n)]
        for cp in copies:
            cp.start()
        for cp in copies:
            cp.wait()

    return pl.pallas_call(
        body, name=name, out_shape=[SDS(v.shape, v.dtype) for v in theirs], in_specs=[ANY] * n, out_specs=[ANY] * n,
        scratch_shapes=[pltpu.SemaphoreType.DMA((n,)), pltpu.SemaphoreType.DMA((n,))],
        compiler_params=_cp(side=True),
    )(*theirs)


def _chip_sums(mines, gots, first, name):
    n = len(mines)
    x, y, _ = _place()
    me = jnp.reshape(2 * x + y, (1,)).astype(jnp.int32)

    def body(me_ref, *refs):
        ins, outs = refs[:2 * n], refs[2 * n:]
        for a in range(n):
            hr, nc = W_CUTS[first + a][1] // 2, W_CUTS[first + a][2]
            s = (ins[2 * a][...] + ins[2 * a + 1][...].astype(F32)).reshape(hr, nc)
            outs[2 * a + 1][0] = s.astype(BF16)

            @pl.when(pl.program_id(0) == me_ref[0])
            def _(a=a, s=s):
                outs[2 * a][...] = s

    in_specs, out_specs, out_shape = [], [], []
    for a in range(n):
        kind, nr, nc = W_CUTS[first + a]
        hr = nr // 2
        spec = pl.BlockSpec((hr, nc), lambda k, mr: (0, k)) if kind == "col" else pl.BlockSpec((1, hr, nc), lambda k, mr: (k, 0, 0))
        in_specs += [spec, spec]
        out_specs += [pl.BlockSpec((hr, nc), lambda k, mr: (0, 0)), pl.BlockSpec((1, hr, nc), lambda k, mr: (k, 0, 0))]
        out_shape += [SDS((hr, nc), F32), SDS((NCHIP, hr, nc), BF16)]
    outs = pl.pallas_call(
        body, name=name, out_shape=out_shape,
        grid_spec=pltpu.PrefetchScalarGridSpec(num_scalar_prefetch=1, grid=(NCHIP,), in_specs=in_specs, out_specs=out_specs),
        compiler_params=_cp(("arbitrary",), VMEM_CAP),
    )(me, *[v for pair in zip(mines, gots) for v in pair])
    return [(outs[2 * a], outs[2 * a + 1]) for a in range(n)]


def _reduce_mine(mines, gots):
    n = len(mines)
    _, _, c = _place()
    core = jnp.reshape(c, (1,)).astype(jnp.int32)
    tr = 256
    nsteps = W_CUTS[0][1] // 2 // tr

    def body(c_ref, *refs):
        ins, outs = refs[:2 * n], refs[2 * n:]

        def add(a):
            m_ref, g_ref = ins[2 * a], ins[2 * a + 1]
            outs[a][...] = ((m_ref[...] + g_ref[0].astype(F32)) + g_ref[1].astype(F32)) + g_ref[2].astype(F32)

        add(0)

        @pl.when(pl.program_id(0) == 0)
        def _():
            for a in range(1, n):
                add(a)

    nc0 = W_CUTS[0][2]
    in_specs = [pl.BlockSpec((tr, nc0), lambda i, cr: (i, 0)), pl.BlockSpec((3, tr, nc0), lambda i, cr: (0, i, 0))]
    out_specs = [pl.BlockSpec((tr, nc0), lambda i, cr: (cr[0] * nsteps + i, 0))]
    for a in range(1, n):
        hr, nc = W_CUTS[a][1] // 2, W_CUTS[a][2]
        in_specs += [pl.BlockSpec((hr, nc), lambda i, cr: (0, 0)), pl.BlockSpec((3, hr, nc), lambda i, cr: (0, 0, 0))]
        out_specs.append(pl.BlockSpec((hr, nc), lambda i, cr: (cr[0], 0)))
    return pl.pallas_call(
        body, name="reduce_mine", out_shape=[SDS((W_CUTS[a][1], W_CUTS[a][2]), F32) for a in range(n)],
        grid_spec=pltpu.PrefetchScalarGridSpec(num_scalar_prefetch=1, grid=(nsteps,), in_specs=in_specs, out_specs=out_specs),
        compiler_params=_cp(("arbitrary",), VMEM_CAP),
    )(core, *[v for pair in zip(mines, gots) for v in pair])


def _join_halves(fulls):
    n = len(fulls)

    def body(*refs):
        full = refs[n:2 * n]
        send_sems, recv_sems = refs[2 * n:]
        x, y, c = _place()
        sibling = (x, y, 1 - c)

        def swap(a, half):
            rows = _half_rows(full[a], W_CUTS[a], half)
            return pltpu.make_async_remote_copy(src_ref=rows, dst_ref=rows, send_sem=send_sems.at[a], recv_sem=recv_sems.at[a],
                                                device_id=sibling, device_id_type=MESH)

        sends = [swap(a, c) for a in range(n)]
        for cp in sends:
            cp.start()
        for a, cp in enumerate(sends):
            cp.wait_send()
            swap(a, 1 - c).wait_recv()

    return pl.pallas_call(
        body, name="join_grad_halves", out_shape=[SDS((W_CUTS[a][1], W_CUTS[a][2]), F32) for a in range(n)],
        in_specs=[ANY] * n, out_specs=[ANY] * n,
        scratch_shapes=[pltpu.SemaphoreType.DMA((n,)), pltpu.SemaphoreType.DMA((n,))],
        input_output_aliases={a: a for a in range(n)}, compiler_params=_cp(side=True),
    )(*fulls)


def _ada_forward(c_all, w_ada, b_cols):
    nb, nc = c_all.shape[0], w_ada.shape[1]

    def body(c_ref, w_ref, b_ref, o_ref):
        cv = c_ref[...]
        sc = (cv * _sig(cv)).astype(BF16)
        o_ref[...] = _dot(sc, w_ref[...].astype(BF16)) + b_ref[...]

    return pl.pallas_call(body, name="ada_forward", out_shape=SDS((nb, nc), F32), compiler_params=_cp(vmem=VMEM_CAP // 2))(c_all, w_ada, b_cols)


def _ada_backward(c_all, dmod_cols, dmod_all):
    nb, nc = dmod_cols.shape

    def body(c_ref, d_ref, a_ref, gw_ref, gb_ref):
        cv = c_ref[...]
        sc = (cv * _sig(cv)).astype(BF16)
        gw_ref[...] = _dot_tn(sc, d_ref[...].astype(BF16))
        gb_ref[...] = jnp.sum(a_ref[...], axis=0, keepdims=True)

    return pl.pallas_call(body, name="ada_backward", out_shape=[SDS((D, nc), F32), SDS((1, dmod_all.shape[1]), F32)],
                          compiler_params=_cp(vmem=VMEM_CAP // 2))(c_all, dmod_cols, dmod_all)


def _modulate(x2, sc1p, shift, seq, tm=512):
    t = x2.shape[0]
    spt = seq // tm

    def body(x_ref, sc_ref, sh_ref, h_ref, ht_ref):
        h = x_ref[...] * sc_ref[0] + sh_ref[0]
        h_ref[...] = h.astype(BF16)
        ht_ref[...] = h.T.astype(BF16)

    per_seq = pl.BlockSpec((1, 1, D), lambda i: (i // spt, 0, 0))
    return pl.pallas_call(
        body, name="modulate", out_shape=[HBM_OUT((t, D), BF16), HBM_OUT((D, t), BF16)], grid=(t // tm,),
        in_specs=[pl.BlockSpec((tm, D), lambda i: (i, 0)), per_seq, per_seq],
        out_specs=[pl.BlockSpec((tm, D), lambda i: (i, 0)), pl.BlockSpec((D, tm), lambda i: (0, i))],
        compiler_params=_cp(("parallel",)),
    )(_in_hbm(x2), sc1p, shift)


TW = 256
TPS = NCOL // NCHIP // TW
NT = NCOL // TW
NQKV_T = 3 * QW // TW
N_TILE_SEMS = 2 * 3 * TPS


def _tile_tables():
    tabs = np.zeros((NCHIP, 3, NT), np.int32)
    for me in range(NCHIP):
        tiles = [TPS * (me ^ (s // TPS)) + s % TPS for s in range(NT)]
        tabs[me, 0] = tiles
        for row, (lo, hi) in enumerate(((0, NQKV_T), (NQKV_T, NT))):
            mine = [w - lo if lo <= w < hi else None for w in tiles]
            held = next(m for m in mine if m is not None)
            for s, m in enumerate(mine):
                held = held if m is None else m
                tabs[me, 1 + row, s] = held
    return tabs


def _project_gather(h, fulls, tab):
    t = h.shape[0]
    n = len(fulls)

    def body(tab_ref, h_ref, *rest):
        qkv_ref, g_ref = rest[n], rest[n + 1]
        full = rest[n + 2:2 * n + 2]
        w_buf, tile_sems, send_sems, recv_sems = rest[2 * n + 2:]
        s = pl.program_id(0)
        x, y, c = _place()
        me = 2 * x + y
        peers = [(x, 1 - y), (1 - x, y), (1 - x, 1 - y)]
        sibling = (x, y, 1 - c)

        def hop(a, r, stage, chip, half, to):
            window = _shard_window(full[a], W_CUTS[a], chip, half)
            k = N_TILE_SEMS + 6 * (a - 1) + 2 * r + stage
            return pltpu.make_async_remote_copy(src_ref=window, dst_ref=window, send_sem=send_sems.at[k], recv_sem=recv_sems.at[k],
                                                device_id=to, device_id_type=MESH)

        def tile_hop(q, stage, col_step, half, to):
            col = pl.multiple_of(tab_ref[0, col_step] * TW, TW)
            window = full[0].at[pl.ds(pl.multiple_of(half * (D // 2), 16), D // 2), pl.ds(col, TW)]
            k = 2 * (q - TPS) + stage
            return pltpu.make_async_remote_copy(src_ref=window, dst_ref=window, send_sem=send_sems.at[k], recv_sem=recv_sems.at[k],
                                                device_id=to, device_id_type=MESH)

        def send_tile(r, j):
            return tile_hop(TPS * (r + 1) + j, 0, j, c, (*peers[r], c))

        def pass_on(q, to):
            return tile_hop(3 * TPS + q % TPS, 0, q, c, to)

        def arrive(a, r):
            px, py = peers[r]
            chip = 2 * px + py
            hop(a, r, 0, chip, c, (px, py, c)).wait_recv()
            hop(a, r, 1, chip, c, sibling).start()
            hop(a, r, 1, chip, 1 - c, sibling).wait_recv()

        def tile(step, slot):
            col = pl.multiple_of(tab_ref[0, step] * TW, TW)
            return pltpu.make_async_copy(full[0].at[:, pl.ds(col, TW)], w_buf.at[slot], tile_sems.at[slot])

        @pl.when(s == 0)
        def _():
            for r in range(2):
                for j in range(TPS):
                    send_tile(r, j).start()
            tile(0, 0).start()

        @pl.when((s + 1 >= TPS) & (s + 1 < NT))
        def _():
            tile_hop(s + 1, 1, s + 1, 1 - c, sibling).wait_recv()

        @pl.when(s + 1 < NT)
        def _():
            tile(s + 1, 1 - (s % 2)).start()

        @pl.when((s + 2 >= TPS) & (s + 2 < NT))
        def _():
            tile_hop(s + 2, 0, s + 2, c, sibling).wait_recv()
            tile_hop(s + 2, 1, s + 2, c, sibling).start()

        for r in range(2):
            @pl.when(((s + 2) // TPS == r + 1) & ((s + 2) % 2 == (r + 1 + TPS * (r + 1)) % 2))
            def _(r=r):
                pass_on(s + 2, (*peers[1 - r], c)).start()

        @pl.when(s + 2 == 2 * TPS - 1)
        def _():
            for a in range(1, n):
                for r in range(3):
                    hop(a, r, 0, me, c, (*peers[r], c)).start()

        slot = s % 2
        tile(s, slot).wait()
        is_qkv = tab_ref[0, s] < NQKV_T
        for k in range(2):
            @pl.when(slot == k)
            def _(k=k):
                acc = _dot(h_ref[...], w_buf[k])

                @pl.when(is_qkv)
                def _():
                    qkv_ref[...] = acc.astype(BF16)

                @pl.when(jnp.logical_not(is_qkv))
                def _():
                    g_ref[...] = acc.astype(BF16)

        @pl.when(s == NT - 1)
        def _():
            for a in range(1, n):
                for r in range(3):
                    arrive(a, r)
            for r in range(3):
                for j in range(TPS):
                    send_tile(r, j).wait_send()
                    tile_hop(TPS * (r + 1) + j, 1, TPS * (r + 1) + j, c, sibling).wait_send()
                for a in range(1, n):
                    hop(a, r, 0, me, c, (*peers[r], c)).wait_send()
                    px, py = peers[r]
                    hop(a, r, 1, 2 * px + py, c, sibling).wait_send()

    n_sems = N_TILE_SEMS + 6 * (n - 1)
    outs = pl.pallas_call(
        body, name="project_gather", out_shape=[HBM_OUT((t, 3 * QW), BF16), HBM_OUT((t, NGATE), BF16)] + [SDS(s, BF16) for s in W_FULL],
        grid_spec=pltpu.PrefetchScalarGridSpec(
            num_scalar_prefetch=1, grid=(NT,),
            in_specs=[pl.BlockSpec((t, D), lambda s, tab: (0, 0))] + [ANY] * n,
            out_specs=[pl.BlockSpec((t, TW), lambda s, tab: (0, tab[1, s])), pl.BlockSpec((t, TW), lambda s, tab: (0, tab[2, s]))] + [ANY] * n,
            scratch_shapes=[pltpu.VMEM((2, D, TW), BF16), pltpu.SemaphoreType.DMA((2,)),
                            pltpu.SemaphoreType.DMA((n_sems,)), pltpu.SemaphoreType.DMA((n_sems,))]),
        input_output_aliases={2 + a: 2 + a for a in range(n)},
        compiler_params=_cp(("arbitrary",), VMEM_CAP, side=True),
    )(tab, _in_hbm(h), *fulls)
    return outs[0], outs[1], outs[2:]


def _bias_tables(rel_bias, buckets):
    def body(tab_ref, bk_ref, o_ref):
        a = lax.broadcasted_iota(jnp.int32, (BLK, 2 * BLK), 0)
        b = lax.broadcasted_iota(jnp.int32, (BLK, 2 * BLK), 1)
        steps = a + BLK - b
        valid = (steps >= 0) & (steps <= BLK)
        for g in range(3):
            bk = bk_ref[g]
            for j in range(4):
                def pick(kk, acc, bk=bk, col=4 * g + j):
                    return jnp.where(bk == kk, tab_ref[kk, col], acc)

                acc = lax.fori_loop(0, N_BUCKETS, pick, jnp.zeros((BLK, 2 * BLK), F32))
                o_ref[g, j] = jnp.where(valid, acc, NEG)

    return pl.pallas_call(
        body, name="bias_tables", out_shape=SDS((3, 4, BLK, 2 * BLK), F32),
        in_specs=[pl.BlockSpec(memory_space=pltpu.SMEM), VMEM_SPEC], out_specs=VMEM_SPEC,
    )(rel_bias, buckets)


def _bias_grad(ds_sum, buckets):
    def body(ds_ref, bk_ref, o_ref):
        lane = lax.broadcasted_iota(jnp.int32, (1, 128), 1)
        for g in range(3):
            def bucket(kk, carry, g=g):
                row = jnp.zeros((1, 128), F32)
                for j in range(4):
                    v = jnp.where(bk_ref[g] == kk, ds_ref[g, j], 0.0)
                    v = jnp.sum(v.reshape(BLK // 8, 8, 2 * BLK), axis=0)
                    s = jnp.sum(jnp.sum(v, axis=1, keepdims=True), axis=0, keepdims=True)
                    row = jnp.where(lane == j, s, row)
                o_ref[g, pl.ds(kk, 1), :] = row
                return carry

            lax.fori_loop(0, N_BUCKETS, bucket, 0)

    return pl.pallas_call(body, name="bias_grad", out_shape=SDS((3, N_BUCKETS, 128), F32), in_specs=[VMEM_SPEC, VMEM_SPEC],
                          out_specs=VMEM_SPEC)(ds_sum, buckets)


def _sub_rows(d, r, first, size):
    return pl.ds(first * d + r, size) if d == 1 else pl.ds(first * d + r, size, stride=d)


def _head_spec(seq, g, part):
    return pl.BlockSpec((seq, HD), lambda b, hh: (b, part * (QW // HD) + 4 * g + hh))


def _rows(start, count, stride):
    return pl.ds(start, count) if stride == 1 else pl.ds(start, count, stride=stride)


def _gather_rows(dst, dst0, src, src0, stride, count):
    for first in range(0, count, BLK):
        dst[pl.ds(dst0 + first, BLK), :] = src[_rows(src0 + first * stride, BLK, stride), :].astype(dst.dtype)


def _scatter_rows(dst, dst0, stride, src, src0, count):
    for first in range(0, count, BLK):
        dst[_rows(dst0 + first * stride, BLK, stride), :] = src[pl.ds(src0 + first, BLK), :].astype(dst.dtype)


def _by_subsequence(dst, src, d, wide=None, tmp=None):
    seq = src.shape[0]
    ln = seq // d
    if wide is not None:
        wide[...] = src[...].astype(F32)
        src = wide
    if d <= 4:
        for r in range(d):
            _gather_rows(dst, r * ln, src, r, d, ln)
    else:
        quarter = seq // 4
        for r4 in range(4):
            _gather_rows(tmp, r4 * quarter, src, r4, 4, quarter)
        for r4 in range(4):
            for a in range(d // 4):
                _gather_rows(dst, (4 * a + r4) * ln, tmp, r4 * quarter + a, d // 4, ln)


def _to_sequence(dst, src, d, tmp=None):
    seq = dst.shape[0]
    ln = seq // d
    if d <= 4:
        for r in range(d):
            _scatter_rows(dst, r, d, src, r * ln, ln)
    else:
        quarter = seq // 4
        for r4 in range(4):
            for a in range(d // 4):
                _scatter_rows(tmp, r4 * quarter + a, d // 4, src, (4 * a + r4) * ln, ln)
        for r4 in range(4):
            _scatter_rows(dst, r4, 4, tmp, r4 * quarter, quarter)


def _attn_forward(g, qkv, bias, bsz, seq):
    d = DILATIONS[g]
    ln = seq // d
    units = [(r, n) for r in range(d) for n in range(ln // BLK)]

    def band(n):
        return slice(BLK, 2 * BLK) if n == 0 else slice(0, 2 * BLK)

    def body(q_ref, k_ref, v_ref, b_ref, o_ref, l_ref, *scratch):
        hs = pl.program_id(1)
        s_scr, p_scr = scratch[:2]
        if d == 1:
            qd, kd, vd = q_ref, k_ref, v_ref
        else:
            wide, tmp, qd, kd, vd = scratch[2:7]
            for dst, src in ((qd, q_ref), (kd, k_ref), (vd, v_ref)):
                _by_subsequence(dst, src, d, wide, tmp)
        blk = lambda r, n: pl.ds(r * ln + n * BLK, BLK)
        direct = d <= 4
        out_rows = (lambda r, n: _sub_rows(d, r, n * BLK, BLK)) if direct else blk
        o_dst, l_dst = (o_ref, l_ref) if direct else scratch[7:9]
        for u, (r, n) in enumerate(units):
            s_scr[u, :, BLK:] = _dot_nt(qd[blk(r, n), :], kd[blk(r, n), :])
            if n > 0:
                s_scr[u, :, :BLK] = _dot_nt(qd[blk(r, n), :], kd[blk(r, n - 1), :])
        for u, (r, n) in enumerate(units):
            s = s_scr[u, :, band(n)] * SCALE + b_ref[hs, :, band(n)]
            m = jnp.max(s, axis=1, keepdims=True)
            e = jnp.exp(s - m)
            den = jnp.sum(e, axis=1, keepdims=True)
            p_scr[u, :, band(n)] = (e * (1.0 / den)).astype(BF16)
            l_dst[out_rows(r, n), :] = jnp.broadcast_to(m + jnp.log(den), (BLK, HD))
        for u, (r, n) in enumerate(units):
            acc = _dot(p_scr[u, :, BLK:], vd[blk(r, n), :])
            if n > 0:
                acc = acc + _dot(p_scr[u, :, :BLK], vd[blk(r, n - 1), :])
            o_dst[out_rows(r, n), :] = acc
        if not direct:
            _to_sequence(o_ref, o_dst, d, tmp)
            _to_sequence(l_ref, l_dst, d, tmp)

    rows_f32, rows_bf16 = pltpu.VMEM((seq, HD), F32), pltpu.VMEM((seq, HD), BF16)
    regrouped = [] if d == 1 else [rows_f32] * 2 + [rows_bf16] * 3 + ([] if d <= 4 else [rows_f32] * 2)
    out_spec = pl.BlockSpec((seq, HD), lambda b, hh: (b, hh))
    return pl.pallas_call(
        body, name=f"attn_forward_{g}", out_shape=[HBM_OUT((bsz * seq, AW), F32)] * 2, grid=(bsz, 4),
        in_specs=[_head_spec(seq, g, part) for part in range(3)] + [pl.BlockSpec((4, BLK, 2 * BLK), lambda b, hh: (0, 0, 0))],
        out_specs=[out_spec, out_spec],
        scratch_shapes=[pltpu.VMEM((len(units), BLK, 2 * BLK), F32), pltpu.VMEM((len(units), BLK, 2 * BLK), BF16)] + regrouped,
        compiler_params=_cp(("parallel", "parallel"), VMEM_CAP // 2),
    )(qkv, qkv, qkv, _in_hbm(bias))


def _attn_backward(g, qkv, do, dl, bias, prev_out, bsz, seq):
    d = DILATIONS[g]
    ln = seq // d
    units = [(r, n) for r in range(d) for n in range(ln // BLK)]

    def body(q_ref, k_ref, v_ref, do_ref, dl_ref, b_ref, *rest):
        dq_ref, dk_ref, dv_ref, db_ref = rest[-18:-14]
        wide, tmp, qd, kd, vd, dod, dld, dqd, dkd, dvd, s_scr, dp_scr, p_scr, ds_scr = rest[-14:]
        hs = pl.program_id(1)

        @pl.when((pl.program_id(0) == 0) & (hs == 0))
        def _():
            db_ref[...] = jnp.zeros_like(db_ref)

        for dst, src in ((qd, q_ref), (kd, k_ref), (vd, v_ref)):
            _by_subsequence(dst, src, d, wide, tmp)
        _by_subsequence(dod, do_ref, d, None, tmp)
        _by_subsequence(dld, dl_ref, d, None, tmp)
        dkd[...] = jnp.zeros_like(dkd)
        dvd[...] = jnp.zeros_like(dvd)
        blk = lambda r, n: pl.ds(r * ln + n * BLK, BLK)
        keys = lambda r, n: [(blk(r, n), slice(BLK, 2 * BLK))] + ([(blk(r, n - 1), slice(0, BLK))] if n > 0 else [])
        for u, (r, n) in enumerate(units):
            for rows, band in keys(r, n):
                s_scr[u, :, band] = _dot_nt(qd[blk(r, n), :], kd[rows, :])
                dp_scr[u, :, band] = _dot_nt(dod[blk(r, n), :], vd[rows, :])
        for u, (r, n) in enumerate(units):
            both = dld[blk(r, n), :]
            lse, delta = both[:, 0:1], both[:, 64:65]
            band = slice(BLK, 2 * BLK) if n == 0 else slice(0, 2 * BLK)
            p = jnp.exp(s_scr[u, :, band] * SCALE + b_ref[hs, :, band] - lse)
            ds = p * (dp_scr[u, :, band] - delta)
            p_scr[u, :, band] = p.astype(BF16)
            ds_scr[u, :, band] = ds.astype(BF16)
            db_ref[hs, :, band] += ds
        for u, (r, n) in enumerate(units):
            dq = jnp.zeros((BLK, HD), F32)
            for rows, band in keys(r, n):
                dvd[rows, :] += _dot_tn(p_scr[u, :, band], dod[blk(r, n), :])
                dkd[rows, :] += _dot_tn(ds_scr[u, :, band], qd[blk(r, n), :]) * SCALE
                dq = dq + _dot(ds_scr[u, :, band], kd[rows, :])
            dqd[blk(r, n), :] = dq * SCALE
        for out, acc in ((dq_ref, dqd), (dk_ref, dkd), (dv_ref, dvd)):
            if d == 1:
                out[...] = acc[...].astype(BF16)
            else:
                _to_sequence(wide, acc, d, tmp)
                out[...] = wide[...].astype(BF16)

    qkv_spec = _head_spec(seq, g, 0)
    out_spec = pl.BlockSpec((seq, HD), lambda b, hh: (b, hh))
    band_spec = pl.BlockSpec((4, BLK, 2 * BLK), lambda b, hh: (0, 0, 0))
    ins = [qkv, qkv, qkv, _in_hbm(do), _in_hbm(dl), _in_hbm(bias)]
    in_specs = [_head_spec(seq, g, part) for part in range(3)] + [out_spec, out_spec, band_spec]
    aliases = {}
    if prev_out is not None:
        ins += list(prev_out)
        in_specs += [ANY] * 3
        aliases = {6: 0, 7: 1, 8: 2}
    rows_bf16, rows_f32 = pltpu.VMEM((seq, HD), BF16), pltpu.VMEM((seq, HD), F32)
    staged = [pltpu.VMEM((len(units), BLK, 2 * BLK), F32)] * 2 + [pltpu.VMEM((len(units), BLK, 2 * BLK), BF16)] * 2
    dq, dk, dv, db = pl.pallas_call(
        body, name=f"attn_backward_{g}", out_shape=[HBM_OUT((bsz * seq, QW), BF16)] * 3 + [SDS((4, BLK, 2 * BLK), F32)], grid=(bsz, 4),
        in_specs=in_specs, out_specs=[qkv_spec] * 3 + [band_spec], input_output_aliases=aliases,
        scratch_shapes=[rows_f32] * 2 + [rows_bf16] * 4 + [rows_f32] * 4 + staged,
        compiler_params=_cp(("arbitrary", "arbitrary"), VMEM_CAP // 2),
    )(*ins)
    return (dq, dk, dv), db


def _mix_forward(gates, og, lg, x2, tgt, gate, w_ao, w_co, w_o, conv_w, conv_b, ln_g, ln_b, bsz, seq, tm=256):
    t = x2.shape[0]
    spt = seq // tm

    def body(g_ref, o1, o2, o3, l1, l2, l3, x_ref, t_ref, gate_ref, wao_ref, wco_ref, wo_ref, cw_ref, cb_ref, lng_ref, lnb_ref,
             ain_ref, sin_ref, mrg_ref, dy_ref, aout_ref, sout_ref, yc_ref, o_ref, lj_ref, dxr_ref, vec_ref, dgate_ref, zc_ref):
        b, i = pl.program_id(0), pl.program_id(1)

        @pl.when((b == 0) & (i == 0))
        def _():
            vec_ref[...] = jnp.zeros_like(vec_ref)

        @pl.when(i == 0)
        def _():
            zc_ref[...] = jnp.zeros_like(zc_ref)
            dgate_ref[...] = jnp.zeros_like(dgate_ref)

        g_attn, u, bg, cg, g_conv, m_attn, m_conv = (g_ref[:, lo:hi].astype(F32) for lo, hi in GATE_COLS)
        la, lb, lc = l1[...], l2[...], l3[...]
        mx = jnp.maximum(la, jnp.maximum(lb, lc))
        ea, eb, ec = jnp.exp(la - mx), jnp.exp(lb - mx), jnp.exp(lc - mx)
        den = ea + eb + ec
        o = (ea * o1[...] + eb * o2[...] + ec * o3[...]) / den
        o_ref[...] = o
        lj_ref[...] = mx + jnp.log(den)
        a_in = o * (g_attn * _sig(g_attn))
        ain_ref[...] = a_in.astype(BF16)
        a_out = _dot(a_in.astype(BF16), wao_ref[...])
        aout_ref[...] = a_out.astype(BF16)
        z = cg * u
        rows = lax.broadcasted_iota(jnp.int32, (tm, D), 0)
        c6, c7 = zc_ref[6:7, :], zc_ref[7:8, :]
        z1 = jnp.where(rows == 0, c7, pltpu.roll(z, 1, 0))
        z2 = jnp.where(rows == 0, c6, jnp.where(rows == 1, c7, pltpu.roll(z, 2, 0)))
        zc_ref[...] = z[tm - 8:tm, :]
        y_conv = (cw_ref[0:1, :] * z2 + cw_ref[1:2, :] * z1 + cw_ref[2:3, :] * z) + cb_ref[...]
        yc_ref[...] = y_conv.astype(BF16)
        s_in = bg * y_conv * (g_conv * _sig(g_conv))
        sin_ref[...] = s_in.astype(BF16)
        s_out = _dot(s_in.astype(BF16), wco_ref[...])
        sout_ref[...] = s_out.astype(BF16)
        merged = _sig(m_attn) * a_out + _sig(m_conv) * s_out
        mrg_ref[...] = merged.astype(BF16)
        y = _dot(merged.astype(BF16), wo_ref[...])
        gate1 = 1.0 + gate_ref[0]
        r = ALPHA * x_ref[...] + gate1 * y
        mu = jnp.mean(r, axis=1, keepdims=True)
        rc = r - mu
        rstd = lax.rsqrt(jnp.mean(rc * rc, axis=1, keepdims=True) + LN_EPS)
        xhat = rc * rstd
        diff = (xhat * lng_ref[...] + lnb_ref[...]) - t_ref[...]
        dout = diff * (1.0 / D)
        vec_ref[0:1, :] += jnp.sum(dout * xhat, axis=0, keepdims=True)
        vec_ref[1:2, :] += jnp.sum(dout, axis=0, keepdims=True)
        vec_ref[2:3, :] += jnp.sum(diff * diff, axis=0, keepdims=True)
        dxh = dout * lng_ref[...]
        dr = rstd * (dxh - jnp.mean(dxh, axis=1, keepdims=True) - xhat * jnp.mean(dxh * xhat, axis=1, keepdims=True))
        dxr_ref[...] = ALPHA * dr
        dy_ref[...] = (dr * gate1).astype(BF16)
        dgate_ref[0] += jnp.sum(dr * y, axis=0, keepdims=True)

    tok = lambda w: pl.BlockSpec((tm, w), lambda b, i: (b * spt + i, 0))
    const = lambda s: pl.BlockSpec(s, lambda b, i: (0,) * len(s))
    per_seq = pl.BlockSpec((1, 1, D), lambda b, i: (b, 0, 0))
    outs = pl.pallas_call(
        body, name="mix_forward", grid=(bsz, spt),
        out_shape=[HBM_OUT((t, AW), BF16), HBM_OUT((t, D), BF16), HBM_OUT((t, D), BF16), HBM_OUT((t, D), BF16), HBM_OUT((t, D), BF16),
                   HBM_OUT((t, D), BF16), HBM_OUT((t, D), BF16), HBM_OUT((t, AW), F32), HBM_OUT((t, AW), F32), HBM_OUT((t, D), F32),
                   SDS((8, D), F32), SDS((bsz, 1, D), F32)],
        in_specs=[tok(NGATE)] + [tok(AW)] * 6 + [tok(D), tok(D), per_seq, const((AW, D)), const((D, D)), const((D, D)),
                                                 const((3, D)), const((1, D)), const((1, D)), const((1, D))],
        out_specs=[tok(AW), tok(D), tok(D), tok(D), tok(D), tok(D), tok(D), tok(AW), tok(AW), tok(D), const((8, D)), per_seq],
        scratch_shapes=[pltpu.VMEM((8, D), F32)],
        compiler_params=_cp(("arbitrary", "arbitrary"), VMEM_CAP),
    )(_in_hbm(gates), *map(_in_hbm, og), *map(_in_hbm, lg), _in_hbm(x2), _in_hbm(tgt), gate, w_ao, w_co, w_o, conv_w, conv_b, ln_g, ln_b)
    return outs


def _mix_backward(gates, dy, a_out, s_out, y_conv, o, lj, w_ao, w_co, w_o, conv_w, vec_f, bsz, seq, tm=256):
    t = dy.shape[0]
    spt = seq // tm

    def body(g_ref, dy_ref, aout_ref, sout_ref, yc_ref, o_ref, lj_ref, wao_ref, wco_ref, wo_ref, cw_ref, vecf_ref,
             dg_ref, do_ref, dl_ref, daout_ref, dsout_ref, vec_ref, car_ref):
        b, i = pl.program_id(0), pl.program_id(1)

        @pl.when((b == 0) & (i == 0))
        def _():
            vec_ref[...] = vecf_ref[...]

        @pl.when(i == 0)
        def _():
            car_ref[...] = jnp.zeros_like(car_ref)

        g_attn, u, bg, cg, g_conv, m_attn, m_conv = (g_ref[:, lo:hi].astype(F32) for lo, hi in GATE_COLS)
        dmerged = _dot_nt(dy_ref[...], wo_ref[...])
        sa, sc = _sig(m_attn), _sig(m_conv)
        da_out = (dmerged * sa).astype(BF16)
        ds_out = (dmerged * sc).astype(BF16)
        daout_ref[...] = da_out
        dsout_ref[...] = ds_out
        dg_ref[:, 4608:5632] = (dmerged * aout_ref[...].astype(F32) * (sa * (1.0 - sa))).astype(BF16)
        dg_ref[:, 5632:6656] = (dmerged * sout_ref[...].astype(F32) * (sc * (1.0 - sc))).astype(BF16)
        da_in = _dot_nt(da_out, wao_ref[...])
        ds_in = _dot_nt(ds_out, wco_ref[...])
        sga = _sig(g_attn)
        o = o_ref[...]
        do = da_in * (g_attn * sga)
        do_ref[...] = do
        dg_ref[:, 0:512] = (da_in * o * (sga * (1.0 + g_attn * (1.0 - sga)))).astype(BF16)
        prod = do * o
        lane = lax.broadcasted_iota(jnp.int32, (tm, HD), 1)
        for j in range(4):
            cs = slice(j * HD, (j + 1) * HD)
            delta = jnp.sum(prod[:, cs], axis=1, keepdims=True)
            dl_ref[:, cs] = jnp.where(lane < 64, lj_ref[:, cs], delta)
        sgc = _sig(g_conv)
        silu_c = g_conv * sgc
        yc = yc_ref[...].astype(F32)
        dg_ref[:, 1536:2560] = (ds_in * yc * silu_c).astype(BF16)
        dg_ref[:, 3584:4608] = (ds_in * bg * yc * (sgc * (1.0 + g_conv * (1.0 - sgc)))).astype(BF16)
        dyc = ds_in * bg * silu_c
        rows = lax.broadcasted_iota(jnp.int32, (tm, D), 0)
        c0, c1 = car_ref[0:1, :], car_ref[1:2, :]
        n1 = jnp.where(rows == tm - 1, c0, pltpu.roll(dyc, tm - 1, 0))
        n2 = jnp.where(rows == tm - 2, c0, jnp.where(rows == tm - 1, c1, pltpu.roll(dyc, tm - 2, 0)))
        car_ref[...] = dyc[0:8, :]
        dz = cw_ref[2:3, :] * dyc + cw_ref[1:2, :] * n1 + cw_ref[0:1, :] * n2
        z = cg * u
        dg_ref[:, 512:1536] = (dz * cg).astype(BF16)
        dg_ref[:, 2560:3584] = (dz * u).astype(BF16)
        vec_ref[3:4, :] += jnp.sum(n2 * z, axis=0, keepdims=True)
        vec_ref[4:5, :] += jnp.sum(n1 * z, axis=0, keepdims=True)
        vec_ref[5:6, :] += jnp.sum(dyc * z, axis=0, keepdims=True)
        vec_ref[6:7, :] += jnp.sum(dyc, axis=0, keepdims=True)

    tok = lambda w: pl.BlockSpec((tm, w), lambda b, i: (b * spt + (spt - 1 - i), 0))
    const = lambda s: pl.BlockSpec(s, lambda b, i: (0,) * len(s))
    return pl.pallas_call(
        body, name="mix_backward", grid=(bsz, spt),
        out_shape=[HBM_OUT((t, NGATE), BF16), HBM_OUT((t, AW), F32), HBM_OUT((t, AW), F32), HBM_OUT((t, D), BF16), HBM_OUT((t, D), BF16),
                   SDS((8, D), F32)],
        in_specs=[tok(NGATE), tok(D), tok(D), tok(D), tok(D), tok(AW), tok(AW), const((AW, D)), const((D, D)), const((D, D)), const((3, D)),
                  const((8, D))],
        out_specs=[tok(NGATE), tok(AW), tok(AW), tok(D), tok(D), const((8, D))],
        scratch_shapes=[pltpu.VMEM((8, D), F32)],
        compiler_params=_cp(("arbitrary", "arbitrary"), VMEM_CAP),
    )(*map(_in_hbm, (gates, dy, a_out, s_out, y_conv, o, lj)), w_ao, w_co, w_o, conv_w, vec_f)


def _scatter_copies(src, land, send_sems, recv_sems):
    x, y, c = _place()
    chips = [(1 - x, y), (x, 1 - y), (1 - x, 1 - y)]
    return [pltpu.make_async_remote_copy(src_ref=src[a].at[2 * cx + cy], dst_ref=land[a].at[r], send_sem=send_sems.at[3 * a + r],
                                         recv_sem=recv_sems.at[3 * a + r], device_id=(cx, cy, c), device_id_type=MESH)
            for a in range(len(src)) for r, (cx, cy) in enumerate(chips)]


def _halves_out(a):
    kind, nr, nc = W_CUTS[a]
    shape = (nr // 2, W_FULL[a][1]) if kind == "col" else (NCHIP, nr // 2, nc)
    return [SDS(shape, F32), SDS(shape, BF16)]


def _write_halves(a, acc_ref, c, mine_ref, theirs_ref):
    kind, nr, nc = W_CUTS[a]
    hr = nr // 2
    if kind == "col":
        mine_ref[...] = acc_ref[pl.ds(pl.multiple_of(c * hr, hr), hr), :]
        theirs_ref[...] = acc_ref[pl.ds(pl.multiple_of((1 - c) * hr, hr), hr), :].astype(BF16)
    else:
        for k in range(NCHIP):
            mine_ref[k] = acc_ref[pl.ds(pl.multiple_of(k * nr + c * hr, hr), hr), :]
            theirs_ref[k] = acc_ref[pl.ds(pl.multiple_of(k * nr + (1 - c) * hr, hr), hr), :].astype(BF16)


def _out_weight_grads(a_in, da_out, s_in, ds_out, merged, dy, core, tk=512):
    t = dy.shape[0]
    nt = t // tk

    def body(c_ref, ain_ref, da_ref, sin_ref, ds_ref, m_ref, dy_ref, *rest):
        outs, (gao, gco, go) = rest[:6], rest[6:]

        @pl.when(pl.program_id(0) == 0)
        def _():
            gao[...] = jnp.zeros_like(gao)
            gco[...] = jnp.zeros_like(gco)
            go[...] = jnp.zeros_like(go)

        gao[...] += _dot_tn(ain_ref[...], da_ref[...])
        gco[...] += _dot_tn(sin_ref[...], ds_ref[...])
        go[...] += _dot_tn(m_ref[...], dy_ref[...])

        @pl.when(pl.program_id(0) == nt - 1)
        def _():
            for a, acc in ((1, gao), (2, gco), (3, go)):
                _write_halves(a, acc, c_ref[0], outs[2 * a - 2], outs[2 * a - 1])

    tok = lambda w: pl.BlockSpec((tk, w), lambda i, cr: (i, 0))
    out_shape = _halves_out(1) + _halves_out(2) + _halves_out(3)
    outs = pl.pallas_call(
        body, name="out_weight_grads", out_shape=out_shape,
        grid_spec=pltpu.PrefetchScalarGridSpec(
            num_scalar_prefetch=1, grid=(nt,), in_specs=[tok(AW), tok(D), tok(D), tok(D), tok(D), tok(D)],
            out_specs=[pl.BlockSpec(o.shape, lambda i, cr, nd=len(o.shape): (0,) * nd) for o in out_shape],
            scratch_shapes=[pltpu.VMEM((AW, D), F32), pltpu.VMEM((D, D), F32), pltpu.VMEM((D, D), F32)]),
        compiler_params=_cp(("arbitrary",), VMEM_CAP),
    )(core, a_in, da_out, s_in, ds_out, merged, dy)
    return [(outs[0], outs[1]), (outs[2], outs[3]), (outs[4], outs[5])]


def _input_grad(dq, dk, dv, dgates, w, x2, dxr, sc1p, seq, sums, tm=512):
    t = x2.shape[0]
    nt = t // tm
    spt = seq // tm
    bsz = t // seq
    n = len(sums)
    gblk = NGATE // 4
    nsteps = 3 + 4

    def body(dq_ref, dk_ref, dv_ref, dg_ref, wq_ref, wg_ref, x_ref, dxr_ref, sc_ref, *rest):
        src, (dx_ref, dsh_ref, dsc_ref), land = rest[:n], rest[n:n + 3], rest[n + 3:2 * n + 3]
        acc_ref, send_sems, recv_sems = rest[2 * n + 3:]
        j, i = pl.program_id(0), pl.program_id(1)
        copies = _scatter_copies(src, land, send_sems, recv_sems)
        rows = pl.ds(pl.multiple_of(i * tm, tm), tm)

        @pl.when((i == 0) & (j == 0))
        def _():
            for cp in copies:
                cp.start()

        for k, ref in enumerate((dq_ref, dk_ref, dv_ref)):
            @pl.when(j == k)
            def _(k=k, ref=ref):
                part = _dot_nt(ref[...], wq_ref[...])
                if k == 0:
                    acc_ref[rows, :] = part
                else:
                    acc_ref[rows, :] += part

        @pl.when((j >= 3) & (j < nsteps - 1))
        def _():
            acc_ref[rows, :] += _dot_nt(dg_ref[...], wg_ref[...])

        @pl.when(j == nsteps - 1)
        def _():
            dh = acc_ref[rows, :] + _dot_nt(dg_ref[...], wg_ref[...])
            dx_ref[...] = dh * sc_ref[0] + dxr_ref[...]

            @pl.when(i % spt == 0)
            def _():
                dsh_ref[...] = jnp.zeros_like(dsh_ref)
                dsc_ref[...] = jnp.zeros_like(dsc_ref)

            dsh_ref[0] += jnp.sum(dh, axis=0, keepdims=True)
            dsc_ref[0] += jnp.sum(dh * x_ref[...], axis=0, keepdims=True)

        @pl.when((i == nt - 1) & (j == nsteps - 1))
        def _():
            for cp in copies:
                cp.wait()

    def held(k):
        return lambda j, i: (jnp.where(j == k, i, jnp.where(j < k, 0, nt - 1)), 0)

    last = lambda j, i: (jnp.where(j == nsteps - 1, i, 0), 0)
    outs = pl.pallas_call(
        body, name="input_grad", grid=(nsteps, nt),
        out_shape=[SDS((t, D), F32), SDS((bsz, 1, D), F32), SDS((bsz, 1, D), F32)] + [SDS((3,) + s.shape[1:], BF16) for s in sums],
        in_specs=[pl.BlockSpec((tm, QW), held(0)), pl.BlockSpec((tm, QW), held(1)), pl.BlockSpec((tm, QW), held(2)),
                  pl.BlockSpec((tm, gblk), lambda j, i: (jnp.where(j >= 3, i, 0), jnp.clip(j - 3, 0, 3))),
                  pl.BlockSpec((D, QW), lambda j, i: (0, jnp.minimum(j, 2))),
                  pl.BlockSpec((pl.Element(D), pl.Element(gblk)), lambda j, i: (0, pl.multiple_of(3 * QW + gblk * jnp.clip(j - 3, 0, 3), 128))),
                  pl.BlockSpec((tm, D), last), pl.BlockSpec((tm, D), last),
                  pl.BlockSpec((1, 1, D), lambda j, i: (jnp.where(j == nsteps - 1, i // spt, 0), 0, 0))] + [ANY] * n,
        out_specs=[pl.BlockSpec((tm, D), last),
                   pl.BlockSpec((1, 1, D), lambda j, i: (jnp.where(j == nsteps - 1, i // spt, 0), 0, 0)),
                   pl.BlockSpec((1, 1, D), lambda j, i: (jnp.where(j == nsteps - 1, i // spt, 0), 0, 0))] + [ANY] * n,
        scratch_shapes=[pltpu.VMEM((t, D), F32), pltpu.SemaphoreType.DMA((3 * NCHIP,)), pltpu.SemaphoreType.DMA((3 * NCHIP,))],
        compiler_params=_cp(("arbitrary", "arbitrary"), VMEM_CAP, side=True),
    )(*map(_in_hbm, (dq, dk, dv, dgates, w, w, x2, dxr)), sc1p, *sums)
    return outs[0], outs[1], outs[2], outs[3:]


def _in_weight_grad(ht, dq, dk, dv, dgates, core, sums):
    t = ht.shape[1]
    hr = D // 2
    n = len(sums)

    def body(c_ref, ht_ref, dq_ref, dk_ref, dv_ref, dg_ref, *rest):
        src, (mine_ref, theirs_ref), land = rest[:n], rest[n:n + 2], rest[n + 2:2 * n + 2]
        acc_ref, send_sems, recv_sems = rest[2 * n + 2:]
        j = pl.program_id(0)
        copies = _scatter_copies(src, land, send_sems, recv_sems)

        @pl.when(j == 0)
        def _():
            for cp in copies:
                cp.start()

        for k, ref in enumerate((dq_ref, dk_ref, dv_ref)):
            @pl.when((j >= k * NQT) & (j < (k + 1) * NQT))
            def _(ref=ref):
                acc_ref[...] = _dot(ht_ref[...], ref[...])

        @pl.when(j >= 3 * NQT)
        def _():
            acc_ref[...] = _dot(ht_ref[...], dg_ref[...])

        _write_halves(0, acc_ref, c_ref[0], mine_ref, theirs_ref)

        @pl.when(j == NPT - 1)
        def _():
            for cp in copies:
                cp.wait()

    def part(k):
        return pl.BlockSpec((t, TN), lambda j, cr: (0, jnp.clip(j - k * NQT, 0, NQT - 1)))

    out_spec = pl.BlockSpec((hr, TN), lambda j, cr: (0, j))
    outs = pl.pallas_call(
        body, name="in_weight_grad", out_shape=[SDS((hr, NCOL), F32), SDS((hr, NCOL), BF16)] + [SDS((3,) + v.shape[1:], BF16) for v in sums],
        grid_spec=pltpu.PrefetchScalarGridSpec(
            num_scalar_prefetch=1, grid=(NPT,),
            in_specs=[pl.BlockSpec((D, t), lambda j, cr: (0, 0)), part(0), part(1), part(2),
                      pl.BlockSpec((t, TN), lambda j, cr: (0, jnp.maximum(j - 3 * NQT, 0)))] + [ANY] * n,
            out_specs=[out_spec, out_spec] + [ANY] * n,
            scratch_shapes=[pltpu.VMEM((D, TN), F32), pltpu.SemaphoreType.DMA((3 * NCHIP,)), pltpu.SemaphoreType.DMA((3 * NCHIP,))]),
        compiler_params=_cp(("arbitrary",), VMEM_CAP, side=True),
    )(core, *map(_in_hbm, (ht, dq, dk, dv, dgates)), *sums)
    return outs[0], outs[1], outs[2:]


def _sum_partials(gathered):
    def body(g_ref, o_ref):
        acc = g_ref[0]
        for k in range(1, 8):
            acc = acc + g_ref[k]
        o_ref[...] = acc

    return pl.pallas_call(body, name="sum_partials", out_shape=SDS(gathered.shape[1:], F32), in_specs=[VMEM_SPEC], out_specs=VMEM_SPEC)(gathered)


def _adamw(w, g, m, v, name, tr=256):
    r, cdim = w.shape
    tr = tr if cdim <= D else tr // 2
    tr = tr if (r % tr == 0 and r > tr) else r

    def body(w_ref, g_ref, m_ref, v_ref, go_ref, d_ref, nm_ref, nv_ref):
        gv = g_ref[...]
        go_ref[...] = gv
        nm = B1 * m_ref[...] + (1.0 - B1) * gv
        nv = B2 * v_ref[...] + (1.0 - B2) * (gv * gv)
        m_hat = nm / (1.0 - B1 ** STEP)
        v_hat = nv / (1.0 - B2 ** STEP)
        d_ref[...] = -LR * (m_hat / (jnp.sqrt(v_hat) + EPS) + WD * w_ref[...])
        nm_ref[...] = nm
        nv_ref[...] = nv

    spec = pl.BlockSpec((tr, cdim), lambda i: (i, 0))
    return pl.pallas_call(
        body, name=name, grid=(r // tr,), out_shape=[SDS((r, cdim), F32)] * 4, in_specs=[spec] * 4, out_specs=[spec] * 4,
        compiler_params=_cp(("parallel",), VMEM_CAP // 2),
    )(w, g, m, v)


def _t5_bucket(dist):
    n = jnp.maximum(dist, 1).astype(F32)
    large = MAX_EXACT + (jnp.log(n / MAX_EXACT) / math.log(MAX_DISTANCE / MAX_EXACT) * (N_BUCKETS - MAX_EXACT)).astype(jnp.int32)
    large = jnp.minimum(large, N_BUCKETS - 1)
    return jnp.where(dist < MAX_EXACT, dist, large)


def _band_buckets():
    a = jnp.arange(BLK)[:, None]
    b = jnp.arange(2 * BLK)[None, :]
    steps = jnp.maximum(a + BLK - b, 0)
    return jnp.stack([_t5_bucket(steps * d) for d in DILATIONS]).astype(jnp.int32)


def _pad_rows(a, rows=8):
    return jnp.pad(a, ((0, rows - a.shape[0]), (0, 0)))


def kernel(x, c, w_ada, b_ada, w_in, conv_w, conv_b, rel_bias, w_attn_out, w_conv_out, w_o, ln_g, ln_b, loss_target, m_w_ada, m_b_ada, m_w_in, m_conv_w, m_conv_b, m_rel_bias, m_w_attn_out, m_w_conv_out, m_w_o, m_ln_g, m_ln_b, v_w_ada, v_b_ada, v_w_in, v_conv_w, v_conv_b, v_rel_bias, v_w_attn_out, v_w_conv_out, v_w_o, v_ln_g, v_ln_b):
    bsz, seq, _ = x.shape
    t = bsz * seq
    mx, my, mc = _place()
    chip = 2 * mx + my
    dev = 4 * mx + 2 * my + mc
    x2 = x.reshape(t, D)
    tgt = loss_target.reshape(t, D)

    mine = _to_bf16_windows([w[0] for w in (w_in, w_attn_out, w_conv_out, w_o)])

    n_ada = w_ada.shape[2]
    n_cw = conv_w.shape[2]
    c_and_cw = jnp.concatenate([_pad_rows(c), jnp.pad(conv_w[0], ((0, 5), (0, D - n_cw)))], axis=0)
    firsts = _all_gather8(c_and_cw, "gather_c_conv_w")
    c_all = firsts[:, 0:bsz, :].reshape(8 * bsz, D)
    conv_w_f = firsts[0::2, 8:11, 0:n_cw].transpose(1, 0, 2).reshape(3, D)
    b_cols = lax.dynamic_slice(b_ada, (0, chip * n_ada), (1, n_ada))
    mod_part = _ada_forward(c_all, w_ada[0], b_cols)
    mod_parts = _all_gather8(mod_part, "gather_mod")
    mod_all = mod_parts[0::2].transpose(1, 0, 2).reshape(8 * bsz, 3 * D)
    mod = lax.dynamic_slice(mod_all, (dev * bsz, 0), (bsz, 3 * D))
    shift = mod[:, 0:D].reshape(bsz, 1, D)
    sc1p = 1.0 + mod[:, D:2 * D].reshape(bsz, 1, D)
    gate = mod[:, 2 * D:].reshape(bsz, 1, D)

    h, ht = _modulate(x2, sc1p, shift, seq)
    tab = lax.dynamic_index_in_dim(jnp.asarray(_tile_tables()), chip, 0, keepdims=False)
    qkv, gates, (w_in_f, w_ao_f, w_co_f, w_o_f) = _project_gather(h, mine, tab)
    buckets = _band_buckets()
    bias = _bias_tables(rel_bias, buckets)
    og, lg = [], []
    for g in range(3):
        o_g, l_g = _attn_forward(g, qkv, bias[g], bsz, seq)
        og.append(o_g)
        lg.append(l_g)
    (a_in, s_in, merged, dy, a_out, s_out, y_conv, o, lj, dxr, vec_f, dgate) = _mix_forward(
        gates, og, lg, x2, tgt, gate, w_ao_f, w_co_f, w_o_f, conv_w_f, conv_b, ln_g, ln_b, bsz, seq)

    dgates, do, dl, da_out, ds_out, vec = _mix_backward(gates, dy, a_out, s_out, y_conv, o, lj, w_ao_f, w_co_f, w_o_f, conv_w_f, vec_f, bsz, seq)
    core = jnp.reshape(mc, (1,)).astype(jnp.int32)
    small_grads = _out_weight_grads(a_in, da_out, s_in, ds_out, merged, dy, core)
    got_small = _swap_halves([theirs for _, theirs in small_grads], "swap_small_grad_halves")
    sums_small = _chip_sums([own for own, _ in small_grads], got_small, 1, "chip_sums_small")
    dqkv, dbs = None, []
    for g in range(3):
        dqkv, db = _attn_backward(g, qkv, do, dl, bias[g], dqkv, bsz, seq)
        dbs.append(db)
    dq, dk, dv = dqkv
    drb = _bias_grad(jnp.stack(dbs), buckets)
    drb = drb[:, :, 0:4].transpose(1, 0, 2).reshape(N_BUCKETS, 12)
    g_in_mine, g_in_theirs, landed_small = _in_weight_grad(ht, dq, dk, dv, dgates, core, [bf for _, bf in sums_small])
    got_in = _swap_halves([g_in_theirs], "swap_in_grad_halves")
    sums_in = _chip_sums([g_in_mine], got_in, 0, "chip_sums_in")
    grad_x, dshift, dscale, landed_in = _input_grad(dq, dk, dv, dgates, w_in_f, x2, dxr, sc1p, seq, [bf for _, bf in sums_in])
    halves = _reduce_mine([own for own, _ in sums_in + sums_small], list(landed_in) + list(landed_small))
    gw_in, gw_ao, gw_co, gw_o = _join_halves(halves)

    dmod = jnp.concatenate([dshift, dscale, dgate], axis=2).reshape(bsz * 3, D)
    drb_row = jnp.pad(drb.reshape(1, N_BUCKETS * 12), ((0, 0), (0, D - N_BUCKETS * 12)))
    vec = lax.dynamic_update_slice(vec, drb_row, (7, 0))
    packed = jnp.concatenate([vec, _pad_rows(dmod)], axis=0)
    gathered = _all_gather8(packed, "gather_small")
    small = _sum_partials(gathered)
    g_ln_g, g_ln_b, loss_lanes = small[0:1], small[1:2], small[2:3]
    g_conv_w_full, g_conv_b = small[3:6], small[6:7]
    g_rel_bias = small[7, 0:N_BUCKETS * 12].reshape(N_BUCKETS, 12)
    loss = 0.5 / D * jnp.sum(loss_lanes)
    dmod_all = gathered[:, 8:8 + 3 * bsz, :].reshape(8 * bsz, 3 * D)
    dmod_cols = lax.dynamic_slice(dmod_all, (0, chip * n_ada), (8 * bsz, n_ada))
    gw_ada, gb_ada = _ada_backward(c_all, dmod_cols, dmod_all)
    g_conv_w = lax.dynamic_slice(g_conv_w_full, (0, chip * n_cw), (3, n_cw))

    names = ["w_ada", "b_ada", "w_in", "conv_w", "conv_b", "rel_bias", "w_attn_out", "w_conv_out", "w_o", "ln_g", "ln_b"]
    two_d = lambda a: a.reshape(a.shape[-2:]) if a.ndim == 3 else a
    weights = dict(zip(names, map(two_d, (w_ada, b_ada, w_in, conv_w, conv_b, rel_bias, w_attn_out, w_conv_out, w_o, ln_g, ln_b))))
    ms = dict(zip(names, map(two_d, (m_w_ada, m_b_ada, m_w_in, m_conv_w, m_conv_b, m_rel_bias, m_w_attn_out, m_w_conv_out, m_w_o, m_ln_g, m_ln_b))))
    vs = dict(zip(names, map(two_d, (v_w_ada, v_b_ada, v_w_in, v_conv_w, v_conv_b, v_rel_bias, v_w_attn_out, v_w_conv_out, v_w_o, v_ln_g, v_ln_b))))
    grads = dict(zip(names, (gw_ada, gb_ada, gw_in, g_conv_w, g_conv_b, g_rel_bias, gw_ao, gw_co, gw_o, g_ln_g, g_ln_b)))
    shapes = dict(zip(names, (w_ada, b_ada, w_in, conv_w, conv_b, rel_bias, w_attn_out, w_conv_out, w_o, ln_g, ln_b)))
    grad_out, deltas, new_m, new_v = {}, {}, {}, {}
    for n in names:
        grad_out[n], deltas[n], new_m[n], new_v[n] = _adamw(weights[n], grads[n], ms[n], vs[n], f"adamw_{n}")
    shaped = lambda d: [d[n].reshape(shapes[n].shape) for n in names]
    return (loss, grad_x.reshape(bsz, seq, D), *shaped(grad_out), *shaped(deltas), *shaped(new_m), *shaped(new_v))
```

```python
import math

import numpy as np
import jax
import jax.numpy as jnp
from jax import lax
from jax.experimental import pallas as pl
from jax.experimental.pallas import tpu as pltpu

F32 = jnp.float32
BF16 = jnp.bfloat16
SDS = jax.ShapeDtypeStruct
MESH = pl.DeviceIdType.MESH
HBM_OUT = pltpu.HBM
ANY = pl.BlockSpec(memory_space=pl.ANY)
VMEM_SPEC = pl.BlockSpec(memory_space=pltpu.VMEM)

D = 1024
HD = 128
BLK = 128
QW = 1536
AW = 512
NGATE = 6656
GATE_COLS = ((0, 512), (512, 1536), (1536, 2560), (2560, 3584), (3584, 4608), (4608, 5632), (5632, 6656))
NCOL = 3 * QW + NGATE
TN = 512
NQT = QW // TN
NPT = NCOL // TN
DILATIONS = (1, 4, 16)
N_BUCKETS, MAX_EXACT, MAX_DISTANCE = 32, 16, 2048
ALPHA = 2.0 ** 0.25
LN_EPS = 1e-5
NEG = -1e30
SCALE = HD ** -0.5
LR, B1, B2, EPS, WD, STEP = 0.001, 0.9, 0.999, 1e-08, 0.01, 10
NCHIP = 4
VMEM_CAP = 60 * 2 ** 20


def _cp(sem=None, vmem=None, side=False):
    return pltpu.CompilerParams(dimension_semantics=sem, vmem_limit_bytes=vmem, has_side_effects=side)


def _dot(a, b):
    return jnp.dot(a, b, preferred_element_type=F32)


def _dot_nt(a, b):
    return lax.dot_general(a, b, (((1,), (1,)), ((), ())), preferred_element_type=F32)


def _dot_tn(a, b):
    return lax.dot_general(a, b, (((0,), (0,)), ((), ())), preferred_element_type=F32)


def _sig(x):
    return 1.0 / (1.0 + jnp.exp(-x))


def _in_hbm(a):
    return pltpu.with_memory_space_constraint(a, pltpu.HBM)


def _place():
    x, y, c = lax.axis_index("x"), lax.axis_index("y"), lax.axis_index("c")
    return x, y, c


def _all_gather8(v, name):
    r, cdim = v.shape

    def body(v_ref, out_ref, send_sems, recv_sems, local_sem):
        x, y, c = _place()
        me = 4 * x + 2 * y + c
        peers = [(x, y, 1 - c), (1 - x, y, c), (x, 1 - y, c), (1 - x, 1 - y, c),
                 (1 - x, y, 1 - c), (x, 1 - y, 1 - c), (1 - x, 1 - y, 1 - c)]
        mine = pltpu.make_async_copy(v_ref, out_ref.at[me], local_sem)
        mine.start()

        def copy(k, block, to):
            return pltpu.make_async_remote_copy(src_ref=v_ref, dst_ref=out_ref.at[block], send_sem=send_sems.at[k],
                                                recv_sem=recv_sems.at[k], device_id=to, device_id_type=MESH)

        sends = [copy(k, me, p) for k, p in enumerate(peers)]
        for cp in sends:
            cp.start()
        for k, (px, py, pc) in enumerate(peers):
            copy(k, 4 * px + 2 * py + pc, (px, py, pc)).wait_recv()
        for cp in sends:
            cp.wait_send()
        mine.wait()

    return pl.pallas_call(
        body, name=name, out_shape=SDS((8, r, cdim), v.dtype), in_specs=[VMEM_SPEC], out_specs=VMEM_SPEC,
        scratch_shapes=[pltpu.SemaphoreType.DMA((7,)), pltpu.SemaphoreType.DMA((7,)), pltpu.SemaphoreType.DMA(())],
        compiler_params=_cp(side=True),
    )(v)


W_CUTS = (("col", D, NCOL // NCHIP), ("col", AW, D // NCHIP), ("row", D // NCHIP, D), ("row", D // NCHIP, D))
W_FULL = ((D, NCOL), (AW, D), (D, D), (D, D))


def _shard_window(ref, cut, k, half):
    kind, nr, nc = cut
    hr = nr // 2
    if kind == "col":
        rows = pl.ds(0, nr) if half is None else pl.ds(pl.multiple_of(half * hr, 16), hr)
        return ref.at[rows, pl.ds(pl.multiple_of(k * nc, 128), nc)]
    if half is None:
        return ref.at[pl.ds(pl.multiple_of(k * nr, 16), nr), :]
    return ref.at[pl.ds(pl.multiple_of(k * nr + half * hr, 16), hr), :]


def _half_rows(ref, cut, half):
    hr = cut[1] // 2
    return ref.at[pl.ds(pl.multiple_of(half * hr, 16), hr), :]


def _to_bf16_windows(ws):
    x, y, _ = _place()
    chip = jnp.reshape(2 * x + y, (1,)).astype(jnp.int32)
    tr = 256
    n = len(ws)

    def body(c_ref, *refs):
        src, dst = refs[:n], refs[n:]
        dst[0][...] = src[0][...].astype(BF16)

        @pl.when(pl.program_id(0) == 0)
        def _():
            for a in range(1, n):
                dst[a][...] = src[a][...].astype(BF16)

    in_specs = [pl.BlockSpec((tr, W_CUTS[0][2]), lambda i, cr: (i, 0))]
    out_specs = [pl.BlockSpec((tr, W_CUTS[0][2]), lambda i, cr: (i, cr[0]))]
    for a in range(1, n):
        kind, nr, nc = W_CUTS[a]
        in_specs.append(pl.BlockSpec((nr, nc), lambda i, cr: (0, 0)))
        out_specs.append(pl.BlockSpec((nr, nc), (lambda i, cr: (0, cr[0])) if kind == "col" else (lambda i, cr: (cr[0], 0))))
    return pl.pallas_call(
        body, name="to_bf16", out_shape=[SDS(W_FULL[a], BF16) for a in range(n)],
        grid_spec=pltpu.PrefetchScalarGridSpec(num_scalar_prefetch=1, grid=(D // tr,), in_specs=in_specs, out_specs=out_specs),
        compiler_params=_cp(("arbitrary",)),
    )(chip, *ws)


def _swap_halves(theirs, name):
    n = len(theirs)

    def body(*refs):
        src, land = refs[:n], refs[n:2 * n]
        send_sems, recv_sems = refs[2 * n:]
        x, y, c = _place()
        copies = [pltpu.make_async_remote_copy(src_ref=src[a], dst_ref=land[a], send_sem=send_sems.at[a], recv_sem=recv_sems.at[a],
                                               device_id=(x, y, 1 - c), device_id_type=MESH) for a in range(n)]
        for cp in copies:
            cp.start()
        for cp in copies:
            cp.wait()

    return pl.pallas_call(
        body, name=name, out_shape=[SDS(v.shape, v.dtype) for v in theirs], in_specs=[ANY] * n, out_specs=[ANY] * n,
        scratch_shapes=[pltpu.SemaphoreType.DMA((n,)), pltpu.SemaphoreType.DMA((n,))],
        compiler_params=_cp(side=True),
    )(*theirs)


def _chip_sums(mines, gots, first, name):
    n = len(mines)
    x, y, _ = _place()
    me = jnp.reshape(2 * x + y, (1,)).astype(jnp.int32)

    def body(me_ref, *refs):
        ins, outs = refs[:2 * n], refs[2 * n:]
        for a in range(n):
            hr, nc = W_CUTS[first + a][1] // 2, W_CUTS[first + a][2]
            s = (ins[2 * a][...] + ins[2 * a + 1][...].astype(F32)).reshape(hr, nc)
            outs[2 * a + 1][0] = s.astype(BF16)

            @pl.when(pl.program_id(0) == me_ref[0])
            def _(a=a, s=s):
                outs[2 * a][...] = s

    in_specs, out_specs, out_shape = [], [], []
    for a in range(n):
        kind, nr, nc = W_CUTS[first + a]
        hr = nr // 2
        spec = pl.BlockSpec((hr, nc), lambda k, mr: (0, k)) if kind == "col" else pl.BlockSpec((1, hr, nc), lambda k, mr: (k, 0, 0))
        in_specs += [spec, spec]
        out_specs += [pl.BlockSpec((hr, nc), lambda k, mr: (0, 0)), pl.BlockSpec((1, hr, nc), lambda k, mr: (k, 0, 0))]
        out_shape += [SDS((hr, nc), F32), SDS((NCHIP, hr, nc), BF16)]
    outs = pl.pallas_call(
        body, name=name, out_shape=out_shape,
        grid_spec=pltpu.PrefetchScalarGridSpec(num_scalar_prefetch=1, grid=(NCHIP,), in_specs=in_specs, out_specs=out_specs),
        compiler_params=_cp(("arbitrary",), VMEM_CAP),
    )(me, *[v for pair in zip(mines, gots) for v in pair])
    return [(outs[2 * a], outs[2 * a + 1]) for a in range(n)]


def _reduce_mine(mines, gots):
    n = len(mines)
    _, _, c = _place()
    core = jnp.reshape(c, (1,)).astype(jnp.int32)
    tr = 256
    nsteps = W_CUTS[0][1] // 2 // tr

    def body(c_ref, *refs):
        ins, outs = refs[:2 * n], refs[2 * n:]

        def add(a):
            m_ref, g_ref = ins[2 * a], ins[2 * a + 1]
            outs[a][...] = ((m_ref[...] + g_ref[0].astype(F32)) + g_ref[1].astype(F32)) + g_ref[2].astype(F32)

        add(0)

        @pl.when(pl.program_id(0) == 0)
        def _():
            for a in range(1, n):
                add(a)

    nc0 = W_CUTS[0][2]
    in_specs = [pl.BlockSpec((tr, nc0), lambda i, cr: (i, 0)), pl.BlockSpec((3, tr, nc0), lambda i, cr: (0, i, 0))]
    out_specs = [pl.BlockSpec((tr, nc0), lambda i, cr: (cr[0] * nsteps + i, 0))]
    for a in range(1, n):
        hr, nc = W_CUTS[a][1] // 2, W_CUTS[a][2]
        in_specs += [pl.BlockSpec((hr, nc), lambda i, cr: (0, 0)), pl.BlockSpec((3, hr, nc), lambda i, cr: (0, 0, 0))]
        out_specs.append(pl.BlockSpec((hr, nc), lambda i, cr: (cr[0], 0)))
    return pl.pallas_call(
        body, name="reduce_mine", out_shape=[SDS((W_CUTS[a][1], W_CUTS[a][2]), F32) for a in range(n)],
        grid_spec=pltpu.PrefetchScalarGridSpec(num_scalar_prefetch=1, grid=(nsteps,), in_specs=in_specs, out_specs=out_specs),
        compiler_params=_cp(("arbitrary",), VMEM_CAP),
    )(core, *[v for pair in zip(mines, gots) for v in pair])


def _join_halves(fulls):
    n = len(fulls)

    def body(*refs):
        full = refs[n:2 * n]
        send_sems, recv_sems = refs[2 * n:]
        x, y, c = _place()
        sibling = (x, y, 1 - c)

        def swap(a, half):
            rows = _half_rows(full[a], W_CUTS[a], half)
            return pltpu.make_async_remote_copy(src_ref=rows, dst_ref=rows, send_sem=send_sems.at[a], recv_sem=recv_sems.at[a],
                                                device_id=sibling, device_id_type=MESH)

        sends = [swap(a, c) for a in range(n)]
        for cp in sends:
            cp.start()
        for a, cp in enumerate(sends):
            cp.wait_send()
            swap(a, 1 - c).wait_recv()

    return pl.pallas_call(
        body, name="join_grad_halves", out_shape=[SDS((W_CUTS[a][1], W_CUTS[a][2]), F32) for a in range(n)],
        in_specs=[ANY] * n, out_specs=[ANY] * n,
        scratch_shapes=[pltpu.SemaphoreType.DMA((n,)), pltpu.SemaphoreType.DMA((n,))],
        input_output_aliases={a: a for a in range(n)}, compiler_params=_cp(side=True),
    )(*fulls)


def _ada_forward(c_all, w_ada, b_cols):
    nb, nc = c_all.shape[0], w_ada.shape[1]

    def body(c_ref, w_ref, b_ref, o_ref):
        cv = c_ref[...]
        sc = (cv * _sig(cv)).astype(BF16)
        o_ref[...] = _dot(sc, w_ref[...].astype(BF16)) + b_ref[...]

    return pl.pallas_call(body, name="ada_forward", out_shape=SDS((nb, nc), F32), compiler_params=_cp(vmem=VMEM_CAP // 2))(c_all, w_ada, b_cols)


def _ada_backward(c_all, dmod_cols, dmod_all):
    nb, nc = dmod_cols.shape

    def body(c_ref, d_ref, a_ref, gw_ref, gb_ref):
        cv = c_ref[...]
        sc = (cv * _sig(cv)).astype(BF16)
        gw_ref[...] = _dot_tn(sc, d_ref[...].astype(BF16))
        gb_ref[...] = jnp.sum(a_ref[...], axis=0, keepdims=True)

    return pl.pallas_call(body, name="ada_backward", out_shape=[SDS((D, nc), F32), SDS((1, dmod_all.shape[1]), F32)],
                          compiler_params=_cp(vmem=VMEM_CAP // 2))(c_all, dmod_cols, dmod_all)


def _modulate(x2, sc1p, shift, seq, tm=512):
    t = x2.shape[0]
    spt = seq // tm

    def body(x_ref, sc_ref, sh_ref, h_ref, ht_ref):
        h = x_ref[...] * sc_ref[0] + sh_ref[0]
        h_ref[...] = h.astype(BF16)
        ht_ref[...] = h.T.astype(BF16)

    per_seq = pl.BlockSpec((1, 1, D), lambda i: (i // spt, 0, 0))
    return pl.pallas_call(
        body, name="modulate", out_shape=[HBM_OUT((t, D), BF16), HBM_OUT((D, t), BF16)], grid=(t // tm,),
        in_specs=[pl.BlockSpec((tm, D), lambda i: (i, 0)), per_seq, per_seq],
        out_specs=[pl.BlockSpec((tm, D), lambda i: (i, 0)), pl.BlockSpec((D, tm), lambda i: (0, i))],
        compiler_params=_cp(("parallel",)),
    )(_in_hbm(x2), sc1p, shift)


TW = 256
TPS = NCOL // NCHIP // TW
NT = NCOL // TW
NQKV_T = 3 * QW // TW
N_TILE_SEMS = 2 * 3 * TPS


def _tile_tables():
    tabs = np.zeros((NCHIP, 3, NT), np.int32)
    for me in range(NCHIP):
        tiles = [TPS * (me ^ (s // TPS)) + s % TPS for s in range(NT)]
        tabs[me, 0] = tiles
        for row, (lo, hi) in enumerate(((0, NQKV_T), (NQKV_T, NT))):
            mine = [w - lo if lo <= w < hi else None for w in tiles]
            held = next(m for m in mine if m is not None)
            for s, m in enumerate(mine):
                held = held if m is None else m
                tabs[me, 1 + row, s] = held
    return tabs


def _project_gather(h, fulls, tab):
    t = h.shape[0]
    n = len(fulls)

    def body(tab_ref, h_ref, *rest):
        qkv_ref, g_ref = rest[n], rest[n + 1]
        full = rest[n + 2:2 * n + 2]
        w_buf, tile_sems, send_sems, recv_sems = rest[2 * n + 2:]
        s = pl.program_id(0)
        x, y, c = _place()
        me = 2 * x + y
        peers = [(x, 1 - y), (1 - x, y), (1 - x, 1 - y)]
        sibling = (x, y, 1 - c)

        def hop(a, r, stage, chip, half, to):
            window = _shard_window(full[a], W_CUTS[a], chip, half)
            k = N_TILE_SEMS + 6 * (a - 1) + 2 * r + stage
            return pltpu.make_async_remote_copy(src_ref=window, dst_ref=window, send_sem=send_sems.at[k], recv_sem=recv_sems.at[k],
                                                device_id=to, device_id_type=MESH)

        def tile_hop(q, stage, col_step, half, to):
            col = pl.multiple_of(tab_ref[0, col_step] * TW, TW)
            window = full[0].at[pl.ds(pl.multiple_of(half * (D // 2), 16), D // 2), pl.ds(col, TW)]
            k = 2 * (q - TPS) + stage
            return pltpu.make_async_remote_copy(src_ref=window, dst_ref=window, send_sem=send_sems.at[k], recv_sem=recv_sems.at[k],
                                                device_id=to, device_id_type=MESH)

        def send_tile(r, j):
            return tile_hop(TPS * (r + 1) + j, 0, j, c, (*peers[r], c))

        def pass_on(q, to):
            return tile_hop(3 * TPS + q % TPS, 0, q, c, to)

        def arrive(a, r):
            px, py = peers[r]
            chip = 2 * px + py
            hop(a, r, 0, chip, c, (px, py, c)).wait_recv()
            hop(a, r, 1, chip, c, sibling).start()
            hop(a, r, 1, chip, 1 - c, sibling).wait_recv()

        def tile(step, slot):
            col = pl.multiple_of(tab_ref[0, step] * TW, TW)
            return pltpu.make_async_copy(full[0].at[:, pl.ds(col, TW)], w_buf.at[slot], tile_sems.at[slot])

        @pl.when(s == 0)
        def _():
            for r in range(2):
                for j in range(TPS):
                    send_tile(r, j).start()
            tile(0, 0).start()

        @pl.when((s + 1 >= TPS) & (s + 1 < NT))
        def _():
            tile_hop(s + 1, 1, s + 1, 1 - c, sibling).wait_recv()

        @pl.when(s + 1 < NT)
        def _():
            tile(s + 1, 1 - (s % 2)).start()

        @pl.when((s + 2 >= TPS) & (s + 2 < NT))
        def _():
            tile_hop(s + 2, 0, s + 2, c, sibling).wait_recv()
            tile_hop(s + 2, 1, s + 2, c, sibling).start()

        for r in range(2):
            @pl.when(((s + 2) // TPS == r + 1) & ((s + 2) % 2 == (r + 1 + TPS * (r + 1)) % 2))
            def _(r=r):
                pass_on(s + 2, (*peers[1 - r], c)).start()

        @pl.when(s + 2 == 2 * TPS - 1)
        def _():
            for a in range(1, n):
                for r in range(3):
                    hop(a, r, 0, me, c, (*peers[r], c)).start()

        slot = s % 2
        tile(s, slot).wait()
        is_qkv = tab_ref[0, s] < NQKV_T
        for k in range(2):
            @pl.when(slot == k)
            def _(k=k):
                acc = _dot(h_ref[...], w_buf[k])

                @pl.when(is_qkv)
                def _():
                    qkv_ref[...] = acc.astype(BF16)

                @pl.when(jnp.logical_not(is_qkv))
                def _():
                    g_ref[...] = acc.astype(BF16)

        @pl.when(s == NT - 1)
        def _():
            for a in range(1, n):
                for r in range(3):
                    arrive(a, r)
            for r in range(3):
                for j in range(TPS):
                    send_tile(r, j).wait_send()
                    tile_hop(TPS * (r + 1) + j, 1, TPS * (r + 1) + j, c, sibling).wait_send()
                for a in range(1, n):
                    hop(a, r, 0, me, c, (*peers[r], c)).wait_send()
                    px, py = peers[r]
                    hop(a, r, 1, 2 * px + py, c, sibling).wait_send()

    n_sems = N_TILE_SEMS + 6 * (n - 1)
    outs = pl.pallas_call(
        body, name="project_gather", out_shape=[HBM_OUT((t, 3 * QW), BF16), HBM_OUT((t, NGATE), BF16)] + [SDS(s, BF16) for s in W_FULL],
        grid_spec=pltpu.PrefetchScalarGridSpec(
            num_scalar_prefetch=1, grid=(NT,),
            in_specs=[pl.BlockSpec((t, D), lambda s, tab: (0, 0))] + [ANY] * n,
            out_specs=[pl.BlockSpec((t, TW), lambda s, tab: (0, tab[1, s])), pl.BlockSpec((t, TW), lambda s, tab: (0, tab[2, s]))] + [ANY] * n,
            scratch_shapes=[pltpu.VMEM((2, D, TW), BF16), pltpu.SemaphoreType.DMA((2,)),
                            pltpu.SemaphoreType.DMA((n_sems,)), pltpu.SemaphoreType.DMA((n_sems,))]),
        input_output_aliases={2 + a: 2 + a for a in range(n)},
        compiler_params=_cp(("arbitrary",), VMEM_CAP, side=True),
    )(tab, _in_hbm(h), *fulls)
    return outs[0], outs[1], outs[2:]


def _bias_tables(rel_bias, buckets):
    def body(tab_ref, bk_ref, o_ref):
        a = lax.broadcasted_iota(jnp.int32, (BLK, 2 * BLK), 0)
        b = lax.broadcasted_iota(jnp.int32, (BLK, 2 * BLK), 1)
        steps = a + BLK - b
        valid = (steps >= 0) & (steps <= BLK)
        for g in range(3):
            bk = bk_ref[g]
            for j in range(4):
                def pick(kk, acc, bk=bk, col=4 * g + j):
                    return jnp.where(bk == kk, tab_ref[kk, col], acc)

                acc = lax.fori_loop(0, N_BUCKETS, pick, jnp.zeros((BLK, 2 * BLK), F32))
                o_ref[g, j] = jnp.where(valid, acc, NEG)

    return pl.pallas_call(
        body, name="bias_tables", out_shape=SDS((3, 4, BLK, 2 * BLK), F32),
        in_specs=[pl.BlockSpec(memory_space=pltpu.SMEM), VMEM_SPEC], out_specs=VMEM_SPEC,
    )(rel_bias, buckets)


def _bias_grad(ds_sum, buckets):
    def body(ds_ref, bk_ref, o_ref, part_ref):
        lane = lax.broadcasted_iota(jnp.int32, (N_BUCKETS, 128), 1)
        for g in range(3):
            def bucket(kk, carry, g=g):
                mine = bk_ref[g] == kk
                for j in range(4):
                    v = jnp.sum(jnp.where(mine, ds_ref[g, j], 0.0).reshape(BLK // 8, 8, 2 * BLK), axis=0)
                    part_ref[j, pl.ds(pl.multiple_of(kk * 8, 8), 8), :] = v[:, :BLK] + v[:, BLK:]
                return carry

            lax.fori_loop(0, N_BUCKETS, bucket, 0)
            out = jnp.zeros((N_BUCKETS, 128), F32)
            for j in range(4):
                rows = jnp.sum(part_ref[j], axis=1, keepdims=True)
                out = jnp.where(lane == j, jnp.sum(rows.reshape(N_BUCKETS, 8, 1), axis=1), out)
            o_ref[g] = out

    return pl.pallas_call(body, name="bias_grad", out_shape=SDS((3, N_BUCKETS, 128), F32), in_specs=[VMEM_SPEC, VMEM_SPEC],
                          out_specs=VMEM_SPEC, scratch_shapes=[pltpu.VMEM((4, N_BUCKETS * 8, 128), F32)])(ds_sum, buckets)


def _sub_rows(d, r, first, size):
    return pl.ds(first * d + r, size) if d == 1 else pl.ds(first * d + r, size, stride=d)


def _head_spec(seq, g, part):
    return pl.BlockSpec((seq, HD), lambda b, hh: (b, part * (QW // HD) + 4 * g + hh))


def _rows(start, count, stride):
    return pl.ds(start, count) if stride == 1 else pl.ds(start, count, stride=stride)


def _gather_rows(dst, dst0, src, src0, stride, count):
    for first in range(0, count, BLK):
        dst[pl.ds(dst0 + first, BLK), :] = src[_rows(src0 + first * stride, BLK, stride), :].astype(dst.dtype)


def _scatter_rows(dst, dst0, stride, src, src0, count):
    for first in range(0, count, BLK):
        dst[_rows(dst0 + first * stride, BLK, stride), :] = src[pl.ds(src0 + first, BLK), :].astype(dst.dtype)


def _by_subsequence(dst, src, d, wide=None, tmp=None):
    seq = src.shape[0]
    ln = seq // d
    if wide is not None:
        wide[...] = src[...].astype(F32)
        src = wide
    if d <= 4:
        for r in range(d):
            _gather_rows(dst, r * ln, src, r, d, ln)
    else:
        quarter = seq // 4
        for r4 in range(4):
            _gather_rows(tmp, r4 * quarter, src, r4, 4, quarter)
        for r4 in range(4):
            for a in range(d // 4):
                _gather_rows(dst, (4 * a + r4) * ln, tmp, r4 * quarter + a, d // 4, ln)


def _to_sequence(dst, src, d, tmp=None):
    seq = dst.shape[0]
    ln = seq // d
    if d <= 4:
        for r in range(d):
            _scatter_rows(dst, r, d, src, r * ln, ln)
    else:
        quarter = seq // 4
        for r4 in range(4):
            for a in range(d // 4):
                _scatter_rows(tmp, r4 * quarter + a, d // 4, src, (4 * a + r4) * ln, ln)
        for r4 in range(4):
            _scatter_rows(dst, r4, 4, tmp, r4 * quarter, quarter)


def _attn_forward(g, qkv, bias, bsz, seq):
    d = DILATIONS[g]
    ln = seq // d
    units = [(r, n) for r in range(d) for n in range(ln // BLK)]

    def band(n):
        return slice(BLK, 2 * BLK) if n == 0 else slice(0, 2 * BLK)

    def body(q_ref, k_ref, v_ref, b_ref, o_ref, l_ref, *scratch):
        hs = pl.program_id(1)
        s_scr, p_scr = scratch[:2]
        if d == 1:
            qd, kd, vd = q_ref, k_ref, v_ref
        else:
            wide, tmp, qd, kd, vd = scratch[2:7]
            for dst, src in ((qd, q_ref), (kd, k_ref), (vd, v_ref)):
                _by_subsequence(dst, src, d, wide, tmp)
        blk = lambda r, n: pl.ds(r * ln + n * BLK, BLK)
        direct = d <= 4
        out_rows = (lambda r, n: _sub_rows(d, r, n * BLK, BLK)) if direct else blk
        o_dst, l_dst = (o_ref, l_ref) if direct else scratch[7:9]
        for u, (r, n) in enumerate(units):
            s_scr[u, :, BLK:] = _dot_nt(qd[blk(r, n), :], kd[blk(r, n), :])
            if n > 0:
                s_scr[u, :, :BLK] = _dot_nt(qd[blk(r, n), :], kd[blk(r, n - 1), :])
        for u, (r, n) in enumerate(units):
            s = s_scr[u, :, band(n)] * SCALE + b_ref[hs, :, band(n)]
            m = jnp.max(s, axis=1, keepdims=True)
            e = jnp.exp(s - m)
            den = jnp.sum(e, axis=1, keepdims=True)
            p_scr[u, :, band(n)] = (e * (1.0 / den)).astype(BF16)
            l_dst[out_rows(r, n), :] = jnp.broadcast_to(m + jnp.log(den), (BLK, HD))
        for u, (r, n) in enumerate(units):
            acc = _dot(p_scr[u, :, BLK:], vd[blk(r, n), :])
            if n > 0:
                acc = acc + _dot(p_scr[u, :, :BLK], vd[blk(r, n - 1), :])
            o_dst[out_rows(r, n), :] = acc
        if not direct:
            _to_sequence(o_ref, o_dst, d, tmp)
            _to_sequence(l_ref, l_dst, d, tmp)

    rows_f32, rows_bf16 = pltpu.VMEM((seq, HD), F32), pltpu.VMEM((seq, HD), BF16)
    regrouped = [] if d == 1 else [rows_f32] * 2 + [rows_bf16] * 3 + ([] if d <= 4 else [rows_f32] * 2)
    out_spec = pl.BlockSpec((seq, HD), lambda b, hh: (b, hh))
    return pl.pallas_call(
        body, name=f"attn_forward_{g}", out_shape=[HBM_OUT((bsz * seq, AW), F32)] * 2, grid=(bsz, 4),
        in_specs=[_head_spec(seq, g, part) for part in range(3)] + [pl.BlockSpec((4, BLK, 2 * BLK), lambda b, hh: (0, 0, 0))],
        out_specs=[out_spec, out_spec],
        scratch_shapes=[pltpu.VMEM((len(units), BLK, 2 * BLK), F32), pltpu.VMEM((len(units), BLK, 2 * BLK), BF16)] + regrouped,
        compiler_params=_cp(("parallel", "parallel"), VMEM_CAP // 2),
    )(qkv, qkv, qkv, _in_hbm(bias))


def _attn_backward(g, qkv, do, dl, bias, prev_out, bsz, seq):
    d = DILATIONS[g]
    ln = seq // d
    units = [(r, n) for r in range(d) for n in range(ln // BLK)]

    def body(q_ref, k_ref, v_ref, do_ref, dl_ref, b_ref, *rest):
        dq_ref, dk_ref, dv_ref, db_ref = rest[-18:-14]
        wide, tmp, qd, kd, vd, dod, dld, dqd, dkd, dvd, s_scr, dp_scr, p_scr, ds_scr = rest[-14:]
        hs = pl.program_id(1)

        @pl.when((pl.program_id(0) == 0) & (hs == 0))
        def _():
            db_ref[...] = jnp.zeros_like(db_ref)

        for dst, src in ((qd, q_ref), (kd, k_ref), (vd, v_ref)):
            _by_subsequence(dst, src, d, wide, tmp)
        _by_subsequence(dod, do_ref, d, None, tmp)
        _by_subsequence(dld, dl_ref, d, None, tmp)
        dkd[...] = jnp.zeros_like(dkd)
        dvd[...] = jnp.zeros_like(dvd)
        blk = lambda r, n: pl.ds(r * ln + n * BLK, BLK)
        keys = lambda r, n: [(blk(r, n), slice(BLK, 2 * BLK))] + ([(blk(r, n - 1), slice(0, BLK))] if n > 0 else [])
        for u, (r, n) in enumerate(units):
            for rows, band in keys(r, n):
                s_scr[u, :, band] = _dot_nt(qd[blk(r, n), :], kd[rows, :])
                dp_scr[u, :, band] = _dot_nt(dod[blk(r, n), :], vd[rows, :])
        for u, (r, n) in enumerate(units):
            both = dld[blk(r, n), :]
            lse, delta = both[:, 0:1], both[:, 64:65]
            band = slice(BLK, 2 * BLK) if n == 0 else slice(0, 2 * BLK)
            p = jnp.exp(s_scr[u, :, band] * SCALE + b_ref[hs, :, band] - lse)
            ds = p * (dp_scr[u, :, band] - delta)
            p_scr[u, :, band] = p.astype(BF16)
            ds_scr[u, :, band] = ds.astype(BF16)
            db_ref[hs, :, band] += ds
        for u, (r, n) in enumerate(units):
            dq = jnp.zeros((BLK, HD), F32)
            for rows, band in keys(r, n):
                dvd[rows, :] += _dot_tn(p_scr[u, :, band], dod[blk(r, n), :])
                dkd[rows, :] += _dot_tn(ds_scr[u, :, band], qd[blk(r, n), :]) * SCALE
                dq = dq + _dot(ds_scr[u, :, band], kd[rows, :])
            dqd[blk(r, n), :] = dq * SCALE
        for out, acc in ((dq_ref, dqd), (dk_ref, dkd), (dv_ref, dvd)):
            if d == 1:
                out[...] = acc[...].astype(BF16)
            else:
                _to_sequence(wide, acc, d, tmp)
                out[...] = wide[...].astype(BF16)

    qkv_spec = _head_spec(seq, g, 0)
    out_spec = pl.BlockSpec((seq, HD), lambda b, hh: (b, hh))
    band_spec = pl.BlockSpec((4, BLK, 2 * BLK), lambda b, hh: (0, 0, 0))
    ins = [qkv, qkv, qkv, _in_hbm(do), _in_hbm(dl), _in_hbm(bias)]
    in_specs = [_head_spec(seq, g, part) for part in range(3)] + [out_spec, out_spec, band_spec]
    aliases = {}
    if prev_out is not None:
        ins += list(prev_out)
        in_specs += [ANY] * 3
        aliases = {6: 0, 7: 1, 8: 2}
    rows_bf16, rows_f32 = pltpu.VMEM((seq, HD), BF16), pltpu.VMEM((seq, HD), F32)
    staged = [pltpu.VMEM((len(units), BLK, 2 * BLK), F32)] * 2 + [pltpu.VMEM((len(units), BLK, 2 * BLK), BF16)] * 2
    dq, dk, dv, db = pl.pallas_call(
        body, name=f"attn_backward_{g}", out_shape=[HBM_OUT((bsz * seq, QW), BF16)] * 3 + [SDS((4, BLK, 2 * BLK), F32)], grid=(bsz, 4),
        in_specs=in_specs, out_specs=[qkv_spec] * 3 + [band_spec], input_output_aliases=aliases,
        scratch_shapes=[rows_f32] * 2 + [rows_bf16] * 4 + [rows_f32] * 4 + staged,
        compiler_params=_cp(("arbitrary", "arbitrary"), VMEM_CAP // 2),
    )(*ins)
    return (dq, dk, dv), db


def _mix_forward(gates, og, lg, x2, tgt, gate, w_ao, w_co, w_o, conv_w, conv_b, ln_g, ln_b, bsz, seq, tm=256):
    t = x2.shape[0]
    spt = seq // tm

    def body(g_ref, o1, o2, o3, l1, l2, l3, x_ref, t_ref, gate_ref, wao_ref, wco_ref, wo_ref, cw_ref, cb_ref, lng_ref, lnb_ref,
             ain_ref, sin_ref, mrg_ref, dy_ref, aout_ref, sout_ref, yc_ref, o_ref, lj_ref, dxr_ref, vec_ref, dgate_ref, zc_ref):
        b, i = pl.program_id(0), pl.program_id(1)

        @pl.when((b == 0) & (i == 0))
        def _():
            vec_ref[...] = jnp.zeros_like(vec_ref)

        @pl.when(i == 0)
        def _():
            zc_ref[...] = jnp.zeros_like(zc_ref)
            dgate_ref[...] = jnp.zeros_like(dgate_ref)

        g_attn, u, bg, cg, g_conv, m_attn, m_conv = (g_ref[:, lo:hi].astype(F32) for lo, hi in GATE_COLS)
        la, lb, lc = l1[...], l2[...], l3[...]
        mx = jnp.maximum(la, jnp.maximum(lb, lc))
        ea, eb, ec = jnp.exp(la - mx), jnp.exp(lb - mx), jnp.exp(lc - mx)
        den = ea + eb + ec
        o = (ea * o1[...] + eb * o2[...] + ec * o3[...]) / den
        o_ref[...] = o
        lj_ref[...] = mx + jnp.log(den)
        a_in = o * (g_attn * _sig(g_attn))
        ain_ref[...] = a_in.astype(BF16)
        a_out = _dot(a_in.astype(BF16), wao_ref[...])
        aout_ref[...] = a_out.astype(BF16)
        z = cg * u
        rows = lax.broadcasted_iota(jnp.int32, (tm, D), 0)
        c6, c7 = zc_ref[6:7, :], zc_ref[7:8, :]
        z1 = jnp.where(rows == 0, c7, pltpu.roll(z, 1, 0))
        z2 = jnp.where(rows == 0, c6, jnp.where(rows == 1, c7, pltpu.roll(z, 2, 0)))
        zc_ref[...] = z[tm - 8:tm, :]
        y_conv = (cw_ref[0:1, :] * z2 + cw_ref[1:2, :] * z1 + cw_ref[2:3, :] * z) + cb_ref[...]
        yc_ref[...] = y_conv.astype(BF16)
        s_in = bg * y_conv * (g_conv * _sig(g_conv))
        sin_ref[...] = s_in.astype(BF16)
        s_out = _dot(s_in.astype(BF16), wco_ref[...])
        sout_ref[...] = s_out.astype(BF16)
        merged = _sig(m_attn) * a_out + _sig(m_conv) * s_out
        mrg_ref[...] = merged.astype(BF16)
        y = _dot(merged.astype(BF16), wo_ref[...])
        gate1 = 1.0 + gate_ref[0]
        r = ALPHA * x_ref[...] + gate1 * y
        mu = jnp.mean(r, axis=1, keepdims=True)
        rc = r - mu
        rstd = lax.rsqrt(jnp.mean(rc * rc, axis=1, keepdims=True) + LN_EPS)
        xhat = rc * rstd
        diff = (xhat * lng_ref[...] + lnb_ref[...]) - t_ref[...]
        dout = diff * (1.0 / D)
        vec_ref[0:1, :] += jnp.sum(dout * xhat, axis=0, keepdims=True)
        vec_ref[1:2, :] += jnp.sum(dout, axis=0, keepdims=True)
        vec_ref[2:3, :] += jnp.sum(diff * diff, axis=0, keepdims=True)
        dxh = dout * lng_ref[...]
        dr = rstd * (dxh - jnp.mean(dxh, axis=1, keepdims=True) - xhat * jnp.mean(dxh * xhat, axis=1, keepdims=True))
        dxr_ref[...] = ALPHA * dr
        dy_ref[...] = (dr * gate1).astype(BF16)
        dgate_ref[0] += jnp.sum(dr * y, axis=0, keepdims=True)

    tok = lambda w: pl.BlockSpec((tm, w), lambda b, i: (b * spt + i, 0))
    const = lambda s: pl.BlockSpec(s, lambda b, i: (0,) * len(s))
    per_seq = pl.BlockSpec((1, 1, D), lambda b, i: (b, 0, 0))
    outs = pl.pallas_call(
        body, name="mix_forward", grid=(bsz, spt),
        out_shape=[HBM_OUT((t, AW), BF16), HBM_OUT((t, D), BF16), HBM_OUT((t, D), BF16), HBM_OUT((t, D), BF16), HBM_OUT((t, D), BF16),
                   HBM_OUT((t, D), BF16), HBM_OUT((t, D), BF16), HBM_OUT((t, AW), F32), HBM_OUT((t, AW), F32), HBM_OUT((t, D), F32),
                   SDS((8, D), F32), SDS((bsz, 1, D), F32)],
        in_specs=[tok(NGATE)] + [tok(AW)] * 6 + [tok(D), tok(D), per_seq, const((AW, D)), const((D, D)), const((D, D)),
                                                 const((3, D)), const((1, D)), const((1, D)), const((1, D))],
        out_specs=[tok(AW), tok(D), tok(D), tok(D), tok(D), tok(D), tok(D), tok(AW), tok(AW), tok(D), const((8, D)), per_seq],
        scratch_shapes=[pltpu.VMEM((8, D), F32)],
        compiler_params=_cp(("arbitrary", "arbitrary"), VMEM_CAP),
    )(_in_hbm(gates), *map(_in_hbm, og), *map(_in_hbm, lg), _in_hbm(x2), _in_hbm(tgt), gate, w_ao, w_co, w_o, conv_w, conv_b, ln_g, ln_b)
    return outs


def _mix_backward(gates, dy, a_out, s_out, y_conv, o, lj, w_ao, w_co, w_o, conv_w, vec_f, bsz, seq, tm=256):
    t = dy.shape[0]
    spt = seq // tm

    def body(g_ref, dy_ref, aout_ref, sout_ref, yc_ref, o_ref, lj_ref, wao_ref, wco_ref, wo_ref, cw_ref, vecf_ref,
             dg_ref, do_ref, dl_ref, daout_ref, dsout_ref, vec_ref, car_ref):
        b, i = pl.program_id(0), pl.program_id(1)

        @pl.when((b == 0) & (i == 0))
        def _():
            vec_ref[...] = vecf_ref[...]

        @pl.when(i == 0)
        def _():
            car_ref[...] = jnp.zeros_like(car_ref)

        g_attn, u, bg, cg, g_conv, m_attn, m_conv = (g_ref[:, lo:hi].astype(F32) for lo, hi in GATE_COLS)
        dmerged = _dot_nt(dy_ref[...], wo_ref[...])
        sa, sc = _sig(m_attn), _sig(m_conv)
        da_out = (dmerged * sa).astype(BF16)
        ds_out = (dmerged * sc).astype(BF16)
        daout_ref[...] = da_out
        dsout_ref[...] = ds_out
        dg_ref[:, 4608:5632] = (dmerged * aout_ref[...].astype(F32) * (sa * (1.0 - sa))).astype(BF16)
        dg_ref[:, 5632:6656] = (dmerged * sout_ref[...].astype(F32) * (sc * (1.0 - sc))).astype(BF16)
        da_in = _dot_nt(da_out, wao_ref[...])
        ds_in = _dot_nt(ds_out, wco_ref[...])
        sga = _sig(g_attn)
        o = o_ref[...]
        do = da_in * (g_attn * sga)
        do_ref[...] = do
        dg_ref[:, 0:512] = (da_in * o * (sga * (1.0 + g_attn * (1.0 - sga)))).astype(BF16)
        prod = do * o
        lane = lax.broadcasted_iota(jnp.int32, (tm, HD), 1)
        for j in range(4):
            cs = slice(j * HD, (j + 1) * HD)
            delta = jnp.sum(prod[:, cs], axis=1, keepdims=True)
            dl_ref[:, cs] = jnp.where(lane < 64, lj_ref[:, cs], delta)
        sgc = _sig(g_conv)
        silu_c = g_conv * sgc
        yc = yc_ref[...].astype(F32)
        dg_ref[:, 1536:2560] = (ds_in * yc * silu_c).astype(BF16)
        dg_ref[:, 3584:4608] = (ds_in * bg * yc * (sgc * (1.0 + g_conv * (1.0 - sgc)))).astype(BF16)
        dyc = ds_in * bg * silu_c
        rows = lax.broadcasted_iota(jnp.int32, (tm, D), 0)
        c0, c1 = car_ref[0:1, :], car_ref[1:2, :]
        n1 = jnp.where(rows == tm - 1, c0, pltpu.roll(dyc, tm - 1, 0))
        n2 = jnp.where(rows == tm - 2, c0, jnp.where(rows == tm - 1, c1, pltpu.roll(dyc, tm - 2, 0)))
        car_ref[...] = dyc[0:8, :]
        dz = cw_ref[2:3, :] * dyc + cw_ref[1:2, :] * n1 + cw_ref[0:1, :] * n2
        z = cg * u
        dg_ref[:, 512:1536] = (dz * cg).astype(BF16)
        dg_ref[:, 2560:3584] = (dz * u).astype(BF16)
        vec_ref[3:4, :] += jnp.sum(n2 * z, axis=0, keepdims=True)
        vec_ref[4:5, :] += jnp.sum(n1 * z, axis=0, keepdims=True)
        vec_ref[5:6, :] += jnp.sum(dyc * z, axis=0, keepdims=True)
        vec_ref[6:7, :] += jnp.sum(dyc, axis=0, keepdims=True)

    tok = lambda w: pl.BlockSpec((tm, w), lambda b, i: (b * spt + (spt - 1 - i), 0))
    const = lambda s: pl.BlockSpec(s, lambda b, i: (0,) * len(s))
    return pl.pallas_call(
        body, name="mix_backward", grid=(bsz, spt),
        out_shape=[HBM_OUT((t, NGATE), BF16), HBM_OUT((t, AW), F32), HBM_OUT((t, AW), F32), HBM_OUT((t, D), BF16), HBM_OUT((t, D), BF16),
                   SDS((8, D), F32)],
        in_specs=[tok(NGATE), tok(D), tok(D), tok(D), tok(D), tok(AW), tok(AW), const((AW, D)), const((D, D)), const((D, D)), const((3, D)),
                  const((8, D))],
        out_specs=[tok(NGATE), tok(AW), tok(AW), tok(D), tok(D), const((8, D))],
        scratch_shapes=[pltpu.VMEM((8, D), F32)],
        compiler_params=_cp(("arbitrary", "arbitrary"), VMEM_CAP),
    )(*map(_in_hbm, (gates, dy, a_out, s_out, y_conv, o, lj)), w_ao, w_co, w_o, conv_w, vec_f)


def _scatter_copies(src, land, send_sems, recv_sems):
    x, y, c = _place()
    chips = [(1 - x, y), (x, 1 - y), (1 - x, 1 - y)]
    return [pltpu.make_async_remote_copy(src_ref=src[a].at[2 * cx + cy], dst_ref=land[a].at[r], send_sem=send_sems.at[3 * a + r],
                                         recv_sem=recv_sems.at[3 * a + r], device_id=(cx, cy, c), device_id_type=MESH)
            for a in range(len(src)) for r, (cx, cy) in enumerate(chips)]


def _halves_out(a):
    kind, nr, nc = W_CUTS[a]
    shape = (nr // 2, W_FULL[a][1]) if kind == "col" else (NCHIP, nr // 2, nc)
    return [SDS(shape, F32), SDS(shape, BF16)]


def _write_halves(a, acc_ref, c, mine_ref, theirs_ref):
    kind, nr, nc = W_CUTS[a]
    hr = nr // 2
    if kind == "col":
        mine_ref[...] = acc_ref[pl.ds(pl.multiple_of(c * hr, hr), hr), :]
        theirs_ref[...] = acc_ref[pl.ds(pl.multiple_of((1 - c) * hr, hr), hr), :].astype(BF16)
    else:
        for k in range(NCHIP):
            mine_ref[k] = acc_ref[pl.ds(pl.multiple_of(k * nr + c * hr, hr), hr), :]
            theirs_ref[k] = acc_ref[pl.ds(pl.multiple_of(k * nr + (1 - c) * hr, hr), hr), :].astype(BF16)


def _out_weight_grads(a_in, da_out, s_in, ds_out, merged, dy, core, tk=1024):
    t = dy.shape[0]
    nt = t // tk

    def body(c_ref, ain_ref, da_ref, sin_ref, ds_ref, m_ref, dy_ref, *rest):
        outs, (gao, gco, go) = rest[:6], rest[6:]

        @pl.when(pl.program_id(0) == 0)
        def _():
            gao[...] = jnp.zeros_like(gao)
            gco[...] = jnp.zeros_like(gco)
            go[...] = jnp.zeros_like(go)

        gao[...] += _dot_tn(ain_ref[...], da_ref[...])
        gco[...] += _dot_tn(sin_ref[...], ds_ref[...])
        go[...] += _dot_tn(m_ref[...], dy_ref[...])

        @pl.when(pl.program_id(0) == nt - 1)
        def _():
            for a, acc in ((1, gao), (2, gco), (3, go)):
                _write_halves(a, acc, c_ref[0], outs[2 * a - 2], outs[2 * a - 1])

    tok = lambda w: pl.BlockSpec((tk, w), lambda i, cr: (i, 0))
    out_shape = _halves_out(1) + _halves_out(2) + _halves_out(3)
    outs = pl.pallas_call(
        body, name="out_weight_grads", out_shape=out_shape,
        grid_spec=pltpu.PrefetchScalarGridSpec(
            num_scalar_prefetch=1, grid=(nt,), in_specs=[tok(AW), tok(D), tok(D), tok(D), tok(D), tok(D)],
            out_specs=[pl.BlockSpec(o.shape, lambda i, cr, nd=len(o.shape): (0,) * nd) for o in out_shape],
            scratch_shapes=[pltpu.VMEM((AW, D), F32), pltpu.VMEM((D, D), F32), pltpu.VMEM((D, D), F32)]),
        compiler_params=_cp(("arbitrary",), VMEM_CAP),
    )(core, a_in, da_out, s_in, ds_out, merged, dy)
    return [(outs[0], outs[1]), (outs[2], outs[3]), (outs[4], outs[5])]


def _input_grad(dq, dk, dv, dgates, w, x2, dxr, sc1p, seq, sums, tm=512):
    t = x2.shape[0]
    nt = t // tm
    spt = seq // tm
    bsz = t // seq
    n = len(sums)
    gblk = NGATE // 4
    nsteps = 3 + 4

    def body(dq_ref, dk_ref, dv_ref, dg_ref, wq_ref, wg_ref, x_ref, dxr_ref, sc_ref, *rest):
        src, (dx_ref, dsh_ref, dsc_ref), land = rest[:n], rest[n:n + 3], rest[n + 3:2 * n + 3]
        acc_ref, send_sems, recv_sems = rest[2 * n + 3:]
        j, i = pl.program_id(0), pl.program_id(1)
        copies = _scatter_copies(src, land, send_sems, recv_sems)
        rows = pl.ds(pl.multiple_of(i * tm, tm), tm)

        @pl.when((i == 0) & (j == 0))
        def _():
            for cp in copies:
                cp.start()

        for k, ref in enumerate((dq_ref, dk_ref, dv_ref)):
            @pl.when(j == k)
            def _(k=k, ref=ref):
                part = _dot_nt(ref[...], wq_ref[...])
                if k == 0:
                    acc_ref[rows, :] = part
                else:
                    acc_ref[rows, :] += part

        @pl.when((j >= 3) & (j < nsteps - 1))
        def _():
            acc_ref[rows, :] += _dot_nt(dg_ref[...], wg_ref[...])

        @pl.when(j == nsteps - 1)
        def _():
            dh = acc_ref[rows, :] + _dot_nt(dg_ref[...], wg_ref[...])
            dx_ref[...] = dh * sc_ref[0] + dxr_ref[...]

            @pl.when(i % spt == 0)
            def _():
                dsh_ref[...] = jnp.zeros_like(dsh_ref)
                dsc_ref[...] = jnp.zeros_like(dsc_ref)

            dsh_ref[0] += jnp.sum(dh, axis=0, keepdims=True)
            dsc_ref[0] += jnp.sum(dh * x_ref[...], axis=0, keepdims=True)

        @pl.when((i == nt - 1) & (j == nsteps - 1))
        def _():
            for cp in copies:
                cp.wait()

    def held(k):
        return lambda j, i: (jnp.where(j == k, i, jnp.where(j < k, 0, nt - 1)), 0)

    last = lambda j, i: (jnp.where(j == nsteps - 1, i, 0), 0)
    outs = pl.pallas_call(
        body, name="input_grad", grid=(nsteps, nt),
        out_shape=[SDS((t, D), F32), SDS((bsz, 1, D), F32), SDS((bsz, 1, D), F32)] + [SDS((3,) + s.shape[1:], BF16) for s in sums],
        in_specs=[pl.BlockSpec((tm, QW), held(0)), pl.BlockSpec((tm, QW), held(1)), pl.BlockSpec((tm, QW), held(2)),
                  pl.BlockSpec((tm, gblk), lambda j, i: (jnp.where(j >= 3, i, 0), jnp.clip(j - 3, 0, 3))),
                  pl.BlockSpec((D, QW), lambda j, i: (0, jnp.minimum(j, 2))),
                  pl.BlockSpec((pl.Element(D), pl.Element(gblk)), lambda j, i: (0, pl.multiple_of(3 * QW + gblk * jnp.clip(j - 3, 0, 3), 128))),
                  pl.BlockSpec((tm, D), last), pl.BlockSpec((tm, D), last),
                  pl.BlockSpec((1, 1, D), lambda j, i: (jnp.where(j == nsteps - 1, i // spt, 0), 0, 0))] + [ANY] * n,
        out_specs=[pl.BlockSpec((tm, D), last),
                   pl.BlockSpec((1, 1, D), lambda j, i: (jnp.where(j == nsteps - 1, i // spt, 0), 0, 0)),
                   pl.BlockSpec((1, 1, D), lambda j, i: (jnp.where(j == nsteps - 1, i // spt, 0), 0, 0))] + [ANY] * n,
        scratch_shapes=[pltpu.VMEM((t, D), F32), pltpu.SemaphoreType.DMA((3 * NCHIP,)), pltpu.SemaphoreType.DMA((3 * NCHIP,))],
        compiler_params=_cp(("arbitrary", "arbitrary"), VMEM_CAP, side=True),
    )(*map(_in_hbm, (dq, dk, dv, dgates, w, w, x2, dxr)), sc1p, *sums)
    return outs[0], outs[1], outs[2], outs[3:]


def _in_weight_grad(ht, dq, dk, dv, dgates, core, sums):
    t = ht.shape[1]
    hr = D // 2
    n = len(sums)

    def body(c_ref, ht_ref, dq_ref, dk_ref, dv_ref, dg_ref, *rest):
        src, (mine_ref, theirs_ref), land = rest[:n], rest[n:n + 2], rest[n + 2:2 * n + 2]
        acc_ref, send_sems, recv_sems = rest[2 * n + 2:]
        j = pl.program_id(0)
        copies = _scatter_copies(src, land, send_sems, recv_sems)

        @pl.when(j == 0)
        def _():
            for cp in copies:
                cp.start()

        for k, ref in enumerate((dq_ref, dk_ref, dv_ref)):
            @pl.when((j >= k * NQT) & (j < (k + 1) * NQT))
            def _(ref=ref):
                acc_ref[...] = _dot(ht_ref[...], ref[...])

        @pl.when(j >= 3 * NQT)
        def _():
            acc_ref[...] = _dot(ht_ref[...], dg_ref[...])

        _write_halves(0, acc_ref, c_ref[0], mine_ref, theirs_ref)

        @pl.when(j == NPT - 1)
        def _():
            for cp in copies:
                cp.wait()

    def part(k):
        return pl.BlockSpec((t, TN), lambda j, cr: (0, jnp.clip(j - k * NQT, 0, NQT - 1)))

    out_spec = pl.BlockSpec((hr, TN), lambda j, cr: (0, j))
    outs = pl.pallas_call(
        body, name="in_weight_grad", out_shape=[SDS((hr, NCOL), F32), SDS((hr, NCOL), BF16)] + [SDS((3,) + v.shape[1:], BF16) for v in sums],
        grid_spec=pltpu.PrefetchScalarGridSpec(
            num_scalar_prefetch=1, grid=(NPT,),
            in_specs=[pl.BlockSpec((D, t), lambda j, cr: (0, 0)), part(0), part(1), part(2),
                      pl.BlockSpec((t, TN), lambda j, cr: (0, jnp.maximum(j - 3 * NQT, 0)))] + [ANY] * n,
            out_specs=[out_spec, out_spec] + [ANY] * n,
            scratch_shapes=[pltpu.VMEM((D, TN), F32), pltpu.SemaphoreType.DMA((3 * NCHIP,)), pltpu.SemaphoreType.DMA((3 * NCHIP,))]),
        compiler_params=_cp(("arbitrary",), VMEM_CAP, side=True),
    )(core, *map(_in_hbm, (ht, dq, dk, dv, dgates)), *sums)
    return outs[0], outs[1], outs[2:]


def _sum_partials(gathered):
    def body(g_ref, o_ref):
        acc = g_ref[0]
        for k in range(1, 8):
            acc = acc + g_ref[k]
        o_ref[...] = acc

    return pl.pallas_call(body, name="sum_partials", out_shape=SDS(gathered.shape[1:], F32), in_specs=[VMEM_SPEC], out_specs=VMEM_SPEC)(gathered)


def _adamw(w, g, m, v, name, tr=256):
    r, cdim = w.shape
    tr = tr if cdim <= D else tr // 2
    tr = tr if (r % tr == 0 and r > tr) else r

    def body(w_ref, g_ref, m_ref, v_ref, go_ref, d_ref, nm_ref, nv_ref):
        gv = g_ref[...]
        go_ref[...] = gv
        nm = B1 * m_ref[...] + (1.0 - B1) * gv
        nv = B2 * v_ref[...] + (1.0 - B2) * (gv * gv)
        m_hat = nm / (1.0 - B1 ** STEP)
        v_hat = nv / (1.0 - B2 ** STEP)
        d_ref[...] = -LR * (m_hat / (jnp.sqrt(v_hat) + EPS) + WD * w_ref[...])
        nm_ref[...] = nm
        nv_ref[...] = nv

    spec = pl.BlockSpec((tr, cdim), lambda i: (i, 0))
    return pl.pallas_call(
        body, name=name, grid=(r // tr,), out_shape=[SDS((r, cdim), F32)] * 4, in_specs=[spec] * 4, out_specs=[spec] * 4,
        compiler_params=_cp(("parallel",), VMEM_CAP // 2),
    )(w, g, m, v)


def _t5_bucket(dist):
    n = jnp.maximum(dist, 1).astype(F32)
    large = MAX_EXACT + (jnp.log(n / MAX_EXACT) / math.log(MAX_DISTANCE / MAX_EXACT) * (N_BUCKETS - MAX_EXACT)).astype(jnp.int32)
    large = jnp.minimum(large, N_BUCKETS - 1)
    return jnp.where(dist < MAX_EXACT, dist, large)


def _band_buckets():
    a = jnp.arange(BLK)[:, None]
    b = jnp.arange(2 * BLK)[None, :]
    steps = jnp.maximum(a + BLK - b, 0)
    return jnp.stack([_t5_bucket(steps * d) for d in DILATIONS]).astype(jnp.int32)


def _pad_rows(a, rows=8):
    return jnp.pad(a, ((0, rows - a.shape[0]), (0, 0)))


def kernel(x, c, w_ada, b_ada, w_in, conv_w, conv_b, rel_bias, w_attn_out, w_conv_out, w_o, ln_g, ln_b, loss_target, m_w_ada, m_b_ada, m_w_in, m_conv_w, m_conv_b, m_rel_bias, m_w_attn_out, m_w_conv_out, m_w_o, m_ln_g, m_ln_b, v_w_ada, v_b_ada, v_w_in, v_conv_w, v_conv_b, v_rel_bias, v_w_attn_out, v_w_conv_out, v_w_o, v_ln_g, v_ln_b):
    bsz, seq, _ = x.shape
    t = bsz * seq
    mx, my, mc = _place()
    chip = 2 * mx + my
    dev = 4 * mx + 2 * my + mc
    x2 = x.reshape(t, D)
    tgt = loss_target.reshape(t, D)

    mine = _to_bf16_windows([w[0] for w in (w_in, w_attn_out, w_conv_out, w_o)])

    n_ada = w_ada.shape[2]
    n_cw = conv_w.shape[2]
    c_and_cw = jnp.concatenate([_pad_rows(c), jnp.pad(conv_w[0], ((0, 5), (0, D - n_cw)))], axis=0)
    firsts = _all_gather8(c_and_cw, "gather_c_conv_w")
    c_all = firsts[:, 0:bsz, :].reshape(8 * bsz, D)
    conv_w_f = firsts[0::2, 8:11, 0:n_cw].transpose(1, 0, 2).reshape(3, D)
    b_cols = lax.dynamic_slice(b_ada, (0, chip * n_ada), (1, n_ada))
    mod_part = _ada_forward(c_all, w_ada[0], b_cols)
    mod_parts = _all_gather8(mod_part, "gather_mod")
    mod_all = mod_parts[0::2].transpose(1, 0, 2).reshape(8 * bsz, 3 * D)
    mod = lax.dynamic_slice(mod_all, (dev * bsz, 0), (bsz, 3 * D))
    shift = mod[:, 0:D].reshape(bsz, 1, D)
    sc1p = 1.0 + mod[:, D:2 * D].reshape(bsz, 1, D)
    gate = mod[:, 2 * D:].reshape(bsz, 1, D)

    h, ht = _modulate(x2, sc1p, shift, seq)
    tab = lax.dynamic_index_in_dim(jnp.asarray(_tile_tables()), chip, 0, keepdims=False)
    qkv, gates, (w_in_f, w_ao_f, w_co_f, w_o_f) = _project_gather(h, mine, tab)
    buckets = _band_buckets()
    bias = _bias_tables(rel_bias, buckets)
    og, lg = [], []
    for g in range(3):
        o_g, l_g = _attn_forward(g, qkv, bias[g], bsz, seq)
        og.append(o_g)
        lg.append(l_g)
    (a_in, s_in, merged, dy, a_out, s_out, y_conv, o, lj, dxr, vec_f, dgate) = _mix_forward(
        gates, og, lg, x2, tgt, gate, w_ao_f, w_co_f, w_o_f, conv_w_f, conv_b, ln_g, ln_b, bsz, seq)

    dgates, do, dl, da_out, ds_out, vec = _mix_backward(gates, dy, a_out, s_out, y_conv, o, lj, w_ao_f, w_co_f, w_o_f, conv_w_f, vec_f, bsz, seq)
    core = jnp.reshape(mc, (1,)).astype(jnp.int32)
    small_grads = _out_weight_grads(a_in, da_out, s_in, ds_out, merged, dy, core)
    got_small = _swap_halves([theirs for _, theirs in small_grads], "swap_small_grad_halves")
    sums_small = _chip_sums([own for own, _ in small_grads], got_small, 1, "chip_sums_small")
    dqkv, dbs = None, []
    for g in range(3):
        dqkv, db = _attn_backward(g, qkv, do, dl, bias[g], dqkv, bsz, seq)
        dbs.append(db)
    dq, dk, dv = dqkv
    drb = _bias_grad(jnp.stack(dbs), buckets)
    drb = drb[:, :, 0:4].transpose(1, 0, 2).reshape(N_BUCKETS, 12)
    g_in_mine, g_in_theirs, landed_small = _in_weight_grad(ht, dq, dk, dv, dgates, core, [bf for _, bf in sums_small])
    got_in = _swap_halves([g_in_theirs], "swap_in_grad_halves")
    sums_in = _chip_sums([g_in_mine], got_in, 0, "chip_sums_in")
    grad_x, dshift, dscale, landed_in = _input_grad(dq, dk, dv, dgates, w_in_f, x2, dxr, sc1p, seq, [bf for _, bf in sums_in])
    halves = _reduce_mine([own for own, _ in sums_in + sums_small], list(landed_in) + list(landed_small))
    gw_in, gw_ao, gw_co, gw_o = _join_halves(halves)

    dmod = jnp.concatenate([dshift, dscale, dgate], axis=2).reshape(bsz * 3, D)
    drb_row = jnp.pad(drb.reshape(1, N_BUCKETS * 12), ((0, 0), (0, D - N_BUCKETS * 12)))
    vec = lax.dynamic_update_slice(vec, drb_row, (7, 0))
    packed = jnp.concatenate([vec, _pad_rows(dmod)], axis=0)
    gathered = _all_gather8(packed, "gather_small")
    small = _sum_partials(gathered)
    g_ln_g, g_ln_b, loss_lanes = small[0:1], small[1:2], small[2:3]
    g_conv_w_full, g_conv_b = small[3:6], small[6:7]
    g_rel_bias = small[7, 0:N_BUCKETS * 12].reshape(N_BUCKETS, 12)
    loss = 0.5 / D * jnp.sum(loss_lanes)
    dmod_all = gathered[:, 8:8 + 3 * bsz, :].reshape(8 * bsz, 3 * D)
    dmod_cols = lax.dynamic_slice(dmod_all, (0, chip * n_ada), (8 * bsz, n_ada))
    gw_ada, gb_ada = _ada_backward(c_all, dmod_cols, dmod_all)
    g_conv_w = lax.dynamic_slice(g_conv_w_full, (0, chip * n_cw), (3, n_cw))

    names = ["w_ada", "b_ada", "w_in", "conv_w", "conv_b", "rel_bias", "w_attn_out", "w_conv_out", "w_o", "ln_g", "ln_b"]
    two_d = lambda a: a.reshape(a.shape[-2:]) if a.ndim == 3 else a
    weights = dict(zip(names, map(two_d, (w_ada, b_ada, w_in, conv_w, conv_b, rel_bias, w_attn_out, w_conv_out, w_o, ln_g, ln_b))))
    ms = dict(zip(names, map(two_d, (m_w_ada, m_b_ada, m_w_in, m_conv_w, m_conv_b, m_rel_bias, m_w_attn_out, m_w_conv_out, m_w_o, m_ln_g, m_ln_b))))
    vs = dict(zip(names, map(two_d, (v_w_ada, v_b_ada, v_w_in, v_conv_w, v_conv_b, v_rel_bias, v_w_attn_out, v_w_conv_out, v_w_o, v_ln_g, v_ln_b))))
    grads = dict(zip(names, (gw_ada, gb_ada, gw_in, g_conv_w, g_conv_b, g_rel_bias, gw_ao, gw_co, gw_o, g_ln_g, g_ln_b)))
    shapes = dict(zip(names, (w_ada, b_ada, w_in, conv_w, conv_b, rel_bias, w_attn_out, w_conv_out, w_o, ln_g, ln_b)))
    grad_out, deltas, new_m, new_v = {}, {}, {}, {}
    for n in names:
        grad_out[n], deltas[n], new_m[n], new_v[n] = _adamw(weights[n], grads[n], ms[n], vs[n], f"adamw_{n}")
    shaped = lambda d: [d[n].reshape(shapes[n].shape) for n in names]
    return (loss, grad_x.reshape(bsz, seq, D), *shaped(grad_out), *shaped(deltas), *shaped(new_m), *shaped(new_v))
```

```python
import math

import numpy as np
import jax
import jax.numpy as jnp
from jax import lax
from jax.experimental import pallas as pl
from jax.experimental.pallas import tpu as pltpu

F32 = jnp.float32
BF16 = jnp.bfloat16
SDS = jax.ShapeDtypeStruct
MESH = pl.DeviceIdType.MESH
HBM_OUT = pltpu.HBM
ANY = pl.BlockSpec(memory_space=pl.ANY)
VMEM_SPEC = pl.BlockSpec(memory_space=pltpu.VMEM)

D = 1024
HD = 128
BLK = 128
QW = 1536
AW = 512
NGATE = 6656
GATE_COLS = ((0, 512), (512, 1536), (1536, 2560), (2560, 3584), (3584, 4608), (4608, 5632), (5632, 6656))
NCOL = 3 * QW + NGATE
TN = 512
NQT = QW // TN
NPT = NCOL // TN
DILATIONS = (1, 4, 16)
N_BUCKETS, MAX_EXACT, MAX_DISTANCE = 32, 16, 2048
ALPHA = 2.0 ** 0.25
LN_EPS = 1e-5
NEG = -1e30
SCALE = HD ** -0.5
LR, B1, B2, EPS, WD, STEP = 0.001, 0.9, 0.999, 1e-08, 0.01, 10
NCHIP = 4
VMEM_CAP = 60 * 2 ** 20


def _cp(sem=None, vmem=None, side=False):
    return pltpu.CompilerParams(dimension_semantics=sem, vmem_limit_bytes=vmem, has_side_effects=side)


def _dot(a, b):
    return jnp.dot(a, b, preferred_element_type=F32)


def _dot_nt(a, b):
    return lax.dot_general(a, b, (((1,), (1,)), ((), ())), preferred_element_type=F32)


def _dot_tn(a, b):
    return lax.dot_general(a, b, (((0,), (0,)), ((), ())), preferred_element_type=F32)


def _sig(x):
    return 1.0 / (1.0 + jnp.exp(-x))


def _in_hbm(a):
    return pltpu.with_memory_space_constraint(a, pltpu.HBM)


def _place():
    x, y, c = lax.axis_index("x"), lax.axis_index("y"), lax.axis_index("c")
    return x, y, c


def _all_gather8(v, name):
    r, cdim = v.shape

    def body(v_ref, out_ref, send_sems, recv_sems, local_sem):
        x, y, c = _place()
        me = 4 * x + 2 * y + c
        peers = [(x, y, 1 - c), (1 - x, y, c), (x, 1 - y, c), (1 - x, 1 - y, c),
                 (1 - x, y, 1 - c), (x, 1 - y, 1 - c), (1 - x, 1 - y, 1 - c)]
        mine = pltpu.make_async_copy(v_ref, out_ref.at[me], local_sem)
        mine.start()

        def copy(k, block, to):
            return pltpu.make_async_remote_copy(src_ref=v_ref, dst_ref=out_ref.at[block], send_sem=send_sems.at[k],
                                                recv_sem=recv_sems.at[k], device_id=to, device_id_type=MESH)

        sends = [copy(k, me, p) for k, p in enumerate(peers)]
        for cp in sends:
            cp.start()
        for k, (px, py, pc) in enumerate(peers):
            copy(k, 4 * px + 2 * py + pc, (px, py, pc)).wait_recv()
        for cp in sends:
            cp.wait_send()
        mine.wait()

    return pl.pallas_call(
        body, name=name, out_shape=SDS((8, r, cdim), v.dtype), in_specs=[VMEM_SPEC], out_specs=VMEM_SPEC,
        scratch_shapes=[pltpu.SemaphoreType.DMA((7,)), pltpu.SemaphoreType.DMA((7,)), pltpu.SemaphoreType.DMA(())],
        compiler_params=_cp(side=True),
    )(v)


W_CUTS = (("col", D, NCOL // NCHIP), ("col", AW, D // NCHIP), ("row", D // NCHIP, D), ("row", D // NCHIP, D))
W_FULL = ((D, NCOL), (AW, D), (D, D), (D, D))


def _shard_window(ref, cut, k, half):
    kind, nr, nc = cut
    hr = nr // 2
    if kind == "col":
        rows = pl.ds(0, nr) if half is None else pl.ds(pl.multiple_of(half * hr, 16), hr)
        return ref.at[rows, pl.ds(pl.multiple_of(k * nc, 128), nc)]
    if half is None:
        return ref.at[pl.ds(pl.multiple_of(k * nr, 16), nr), :]
    return ref.at[pl.ds(pl.multiple_of(k * nr + half * hr, 16), hr), :]


def _half_rows(ref, cut, half):
    hr = cut[1] // 2
    return ref.at[pl.ds(pl.multiple_of(half * hr, 16), hr), :]


def _to_bf16_windows(ws):
    x, y, _ = _place()
    chip = jnp.reshape(2 * x + y, (1,)).astype(jnp.int32)
    tr = 256
    n = len(ws)

    def body(c_ref, *refs):
        src, dst = refs[:n], refs[n:]
        dst[0][...] = src[0][...].astype(BF16)

        @pl.when(pl.program_id(0) == 0)
        def _():
            for a in range(1, n):
                dst[a][...] = src[a][...].astype(BF16)

    in_specs = [pl.BlockSpec((tr, W_CUTS[0][2]), lambda i, cr: (i, 0))]
    out_specs = [pl.BlockSpec((tr, W_CUTS[0][2]), lambda i, cr: (i, cr[0]))]
    for a in range(1, n):
        kind, nr, nc = W_CUTS[a]
        in_specs.append(pl.BlockSpec((nr, nc), lambda i, cr: (0, 0)))
        out_specs.append(pl.BlockSpec((nr, nc), (lambda i, cr: (0, cr[0])) if kind == "col" else (lambda i, cr: (cr[0], 0))))
    return pl.pallas_call(
        body, name="to_bf16", out_shape=[SDS(W_FULL[a], BF16) for a in range(n)],
        grid_spec=pltpu.PrefetchScalarGridSpec(num_scalar_prefetch=1, grid=(D // tr,), in_specs=in_specs, out_specs=out_specs),
        compiler_params=_cp(("arbitrary",)),
    )(chip, *ws)


def _swap_halves(theirs, name):
    n = len(theirs)

    def body(*refs):
        src, land = refs[:n], refs[n:2 * n]
        send_sems, recv_sems = refs[2 * n:]
        x, y, c = _place()
        copies = [pltpu.make_async_remote_copy(src_ref=src[a], dst_ref=land[a], send_sem=send_sems.at[a], recv_sem=recv_sems.at[a],
                                               device_id=(x, y, 1 - c), device_id_type=MESH) for a in range(n)]
        for cp in copies:
            cp.start()
        for cp in copies:
            cp.wait()

    return pl.pallas_call(
        body, name=name, out_shape=[SDS(v.shape, v.dtype) for v in theirs], in_specs=[ANY] * n, out_specs=[ANY] * n,
        scratch_shapes=[pltpu.SemaphoreType.DMA((n,)), pltpu.SemaphoreType.DMA((n,))],
        compiler_params=_cp(side=True),
    )(*theirs)


def _chip_sums(mines, gots, first, name):
    n = len(mines)
    x, y, _ = _place()
    me = jnp.reshape(2 * x + y, (1,)).astype(jnp.int32)

    def body(me_ref, *refs):
        ins, outs = refs[:2 * n], refs[2 * n:]
        for a in range(n):
            hr, nc = W_CUTS[first + a][1] // 2, W_CUTS[first + a][2]
            s = (ins[2 * a][...] + ins[2 * a + 1][...].astype(F32)).reshape(hr, nc)
            outs[2 * a + 1][0] = s.astype(BF16)

            @pl.when(pl.program_id(0) == me_ref[0])
            def _(a=a, s=s):
                outs[2 * a][...] = s

    in_specs, out_specs, out_shape = [], [], []
    for a in range(n):
        kind, nr, nc = W_CUTS[first + a]
        hr = nr // 2
        spec = pl.BlockSpec((hr, nc), lambda k, mr: (0, k)) if kind == "col" else pl.BlockSpec((1, hr, nc), lambda k, mr: (k, 0, 0))
        in_specs += [spec, spec]
        out_specs += [pl.BlockSpec((hr, nc), lambda k, mr: (0, 0)), pl.BlockSpec((1, hr, nc), lambda k, mr: (k, 0, 0))]
        out_shape += [SDS((hr, nc), F32), SDS((NCHIP, hr, nc), BF16)]
    outs = pl.pallas_call(
        body, name=name, out_shape=out_shape,
        grid_spec=pltpu.PrefetchScalarGridSpec(num_scalar_prefetch=1, grid=(NCHIP,), in_specs=in_specs, out_specs=out_specs),
        compiler_params=_cp(("arbitrary",), VMEM_CAP),
    )(me, *[v for pair in zip(mines, gots) for v in pair])
    return [(outs[2 * a], outs[2 * a + 1]) for a in range(n)]


def _reduce_mine(mines, gots):
    n = len(mines)
    _, _, c = _place()
    core = jnp.reshape(c, (1,)).astype(jnp.int32)
    tr = 256
    nsteps = W_CUTS[0][1] // 2 // tr

    def body(c_ref, *refs):
        ins, outs = refs[:2 * n], refs[2 * n:]

        def add(a):
            m_ref, g_ref = ins[2 * a], ins[2 * a + 1]
            outs[a][...] = ((m_ref[...] + g_ref[0].astype(F32)) + g_ref[1].astype(F32)) + g_ref[2].astype(F32)

        add(0)

        @pl.when(pl.program_id(0) == 0)
        def _():
            for a in range(1, n):
                add(a)

    nc0 = W_CUTS[0][2]
    in_specs = [pl.BlockSpec((tr, nc0), lambda i, cr: (i, 0)), pl.BlockSpec((3, tr, nc0), lambda i, cr: (0, i, 0))]
    out_specs = [pl.BlockSpec((tr, nc0), lambda i, cr: (cr[0] * nsteps + i, 0))]
    for a in range(1, n):
        hr, nc = W_CUTS[a][1] // 2, W_CUTS[a][2]
        in_specs += [pl.BlockSpec((hr, nc), lambda i, cr: (0, 0)), pl.BlockSpec((3, hr, nc), lambda i, cr: (0, 0, 0))]
        out_specs.append(pl.BlockSpec((hr, nc), lambda i, cr: (cr[0], 0)))
    return pl.pallas_call(
        body, name="reduce_mine", out_shape=[SDS((W_CUTS[a][1], W_CUTS[a][2]), F32) for a in range(n)],
        grid_spec=pltpu.PrefetchScalarGridSpec(num_scalar_prefetch=1, grid=(nsteps,), in_specs=in_specs, out_specs=out_specs),
        compiler_params=_cp(("arbitrary",), VMEM_CAP),
    )(core, *[v for pair in zip(mines, gots) for v in pair])


def _join_halves(fulls):
    n = len(fulls)

    def body(*refs):
        full = refs[n:2 * n]
        send_sems, recv_sems = refs[2 * n:]
        x, y, c = _place()
        sibling = (x, y, 1 - c)

        def swap(a, half):
            rows = _half_rows(full[a], W_CUTS[a], half)
            return pltpu.make_async_remote_copy(src_ref=rows, dst_ref=rows, send_sem=send_sems.at[a], recv_sem=recv_sems.at[a],
                                                device_id=sibling, device_id_type=MESH)

        sends = [swap(a, c) for a in range(n)]
        for cp in sends:
            cp.start()
        for a, cp in enumerate(sends):
            cp.wait_send()
            swap(a, 1 - c).wait_recv()

    return pl.pallas_call(
        body, name="join_grad_halves", out_shape=[SDS((W_CUTS[a][1], W_CUTS[a][2]), F32) for a in range(n)],
        in_specs=[ANY] * n, out_specs=[ANY] * n,
        scratch_shapes=[pltpu.SemaphoreType.DMA((n,)), pltpu.SemaphoreType.DMA((n,))],
        input_output_aliases={a: a for a in range(n)}, compiler_params=_cp(side=True),
    )(*fulls)


def _ada_forward(c_all, w_ada, b_cols):
    nb, nc = c_all.shape[0], w_ada.shape[1]

    def body(c_ref, w_ref, b_ref, o_ref):
        cv = c_ref[...]
        sc = (cv * _sig(cv)).astype(BF16)
        o_ref[...] = _dot(sc, w_ref[...].astype(BF16)) + b_ref[...]

    return pl.pallas_call(body, name="ada_forward", out_shape=SDS((nb, nc), F32), compiler_params=_cp(vmem=VMEM_CAP // 2))(c_all, w_ada, b_cols)


def _ada_backward(c_all, dmod_cols, dmod_all):
    nb, nc = dmod_cols.shape

    def body(c_ref, d_ref, a_ref, gw_ref, gb_ref):
        cv = c_ref[...]
        sc = (cv * _sig(cv)).astype(BF16)
        gw_ref[...] = _dot_tn(sc, d_ref[...].astype(BF16))
        gb_ref[...] = jnp.sum(a_ref[...], axis=0, keepdims=True)

    return pl.pallas_call(body, name="ada_backward", out_shape=[SDS((D, nc), F32), SDS((1, dmod_all.shape[1]), F32)],
                          compiler_params=_cp(vmem=VMEM_CAP // 2))(c_all, dmod_cols, dmod_all)


def _modulate(x2, sc1p, shift, seq, tm=512):
    t = x2.shape[0]
    spt = seq // tm

    def body(x_ref, sc_ref, sh_ref, h_ref, ht_ref):
        h = x_ref[...] * sc_ref[0] + sh_ref[0]
        h_ref[...] = h.astype(BF16)
        ht_ref[...] = h.T.astype(BF16)

    per_seq = pl.BlockSpec((1, 1, D), lambda i: (i // spt, 0, 0))
    return pl.pallas_call(
        body, name="modulate", out_shape=[HBM_OUT((t, D), BF16), HBM_OUT((D, t), BF16)], grid=(t // tm,),
        in_specs=[pl.BlockSpec((tm, D), lambda i: (i, 0)), per_seq, per_seq],
        out_specs=[pl.BlockSpec((tm, D), lambda i: (i, 0)), pl.BlockSpec((D, tm), lambda i: (0, i))],
        compiler_params=_cp(("parallel",)),
    )(_in_hbm(x2), sc1p, shift)


TW = 256
TPS = NCOL // NCHIP // TW
NT = NCOL // TW
NQKV_T = 3 * QW // TW
N_TILE_SEMS = 2 * 3 * TPS


def _tile_tables():
    tabs = np.zeros((NCHIP, 3, NT), np.int32)
    for me in range(NCHIP):
        tiles = [TPS * (me ^ (s // TPS)) + s % TPS for s in range(NT)]
        tabs[me, 0] = tiles
        for row, (lo, hi) in enumerate(((0, NQKV_T), (NQKV_T, NT))):
            mine = [w - lo if lo <= w < hi else None for w in tiles]
            held = next(m for m in mine if m is not None)
            for s, m in enumerate(mine):
                held = held if m is None else m
                tabs[me, 1 + row, s] = held
    return tabs


def _project_gather(h, fulls, tab):
    t = h.shape[0]
    n = len(fulls)

    def body(tab_ref, h_ref, *rest):
        qkv_ref, g_ref = rest[n], rest[n + 1]
        full = rest[n + 2:2 * n + 2]
        w_buf, tile_sems, send_sems, recv_sems = rest[2 * n + 2:]
        s = pl.program_id(0)
        x, y, c = _place()
        me = 2 * x + y
        peers = [(x, 1 - y), (1 - x, y), (1 - x, 1 - y)]
        sibling = (x, y, 1 - c)

        def hop(a, r, stage, chip, half, to):
            window = _shard_window(full[a], W_CUTS[a], chip, half)
            k = N_TILE_SEMS + 6 * (a - 1) + 2 * r + stage
            return pltpu.make_async_remote_copy(src_ref=window, dst_ref=window, send_sem=send_sems.at[k], recv_sem=recv_sems.at[k],
                                                device_id=to, device_id_type=MESH)

        def tile_hop(q, stage, col_step, half, to):
            col = pl.multiple_of(tab_ref[0, col_step] * TW, TW)
            window = full[0].at[pl.ds(pl.multiple_of(half * (D // 2), 16), D // 2), pl.ds(col, TW)]
            k = 2 * (q - TPS) + stage
            return pltpu.make_async_remote_copy(src_ref=window, dst_ref=window, send_sem=send_sems.at[k], recv_sem=recv_sems.at[k],
                                                device_id=to, device_id_type=MESH)

        def send_tile(r, j):
            return tile_hop(TPS * (r + 1) + j, 0, j, c, (*peers[r], c))

        def pass_on(q, to):
            return tile_hop(3 * TPS + q % TPS, 0, q, c, to)

        def arrive(a, r):
            px, py = peers[r]
            chip = 2 * px + py
            hop(a, r, 0, chip, c, (px, py, c)).wait_recv()
            hop(a, r, 1, chip, c, sibling).start()
            hop(a, r, 1, chip, 1 - c, sibling).wait_recv()

        def tile(step, slot):
            col = pl.multiple_of(tab_ref[0, step] * TW, TW)
            return pltpu.make_async_copy(full[0].at[:, pl.ds(col, TW)], w_buf.at[slot], tile_sems.at[slot])

        @pl.when(s == 0)
        def _():
            for r in range(2):
                for j in range(TPS):
                    send_tile(r, j).start()
            tile(0, 0).start()

        @pl.when((s + 1 >= TPS) & (s + 1 < NT))
        def _():
            tile_hop(s + 1, 1, s + 1, 1 - c, sibling).wait_recv()

        @pl.when(s + 1 < NT)
        def _():
            tile(s + 1, 1 - (s % 2)).start()

        @pl.when((s + 2 >= TPS) & (s + 2 < NT))
        def _():
            tile_hop(s + 2, 0, s + 2, c, sibling).wait_recv()
            tile_hop(s + 2, 1, s + 2, c, sibling).start()

        for r in range(2):
            @pl.when(((s + 2) // TPS == r + 1) & ((s + 2) % 2 == (r + 1 + TPS * (r + 1)) % 2))
            def _(r=r):
                pass_on(s + 2, (*peers[1 - r], c)).start()

        @pl.when(s + 2 == 2 * TPS - 1)
        def _():
            for a in range(1, n):
                for r in range(3):
                    hop(a, r, 0, me, c, (*peers[r], c)).start()

        slot = s % 2
        tile(s, slot).wait()
        is_qkv = tab_ref[0, s] < NQKV_T
        for k in range(2):
            @pl.when(slot == k)
            def _(k=k):
                acc = _dot(h_ref[...], w_buf[k])

                @pl.when(is_qkv)
                def _():
                    qkv_ref[...] = acc.astype(BF16)

                @pl.when(jnp.logical_not(is_qkv))
                def _():
                    g_ref[...] = acc.astype(BF16)

        @pl.when(s == NT - 1)
        def _():
            for a in range(1, n):
                for r in range(3):
                    arrive(a, r)
            for r in range(3):
                for j in range(TPS):
                    send_tile(r, j).wait_send()
                    tile_hop(TPS * (r + 1) + j, 1, TPS * (r + 1) + j, c, sibling).wait_send()
                for a in range(1, n):
                    hop(a, r, 0, me, c, (*peers[r], c)).wait_send()
                    px, py = peers[r]
                    hop(a, r, 1, 2 * px + py, c, sibling).wait_send()

    n_sems = N_TILE_SEMS + 6 * (n - 1)
    outs = pl.pallas_call(
        body, name="project_gather", out_shape=[HBM_OUT((t, 3 * QW), BF16), HBM_OUT((t, NGATE), BF16)] + [SDS(s, BF16) for s in W_FULL],
        grid_spec=pltpu.PrefetchScalarGridSpec(
            num_scalar_prefetch=1, grid=(NT,),
            in_specs=[pl.BlockSpec((t, D), lambda s, tab: (0, 0))] + [ANY] * n,
            out_specs=[pl.BlockSpec((t, TW), lambda s, tab: (0, tab[1, s])), pl.BlockSpec((t, TW), lambda s, tab: (0, tab[2, s]))] + [ANY] * n,
            scratch_shapes=[pltpu.VMEM((2, D, TW), BF16), pltpu.SemaphoreType.DMA((2,)),
                            pltpu.SemaphoreType.DMA((n_sems,)), pltpu.SemaphoreType.DMA((n_sems,))]),
        input_output_aliases={2 + a: 2 + a for a in range(n)},
        compiler_params=_cp(("arbitrary",), VMEM_CAP, side=True),
    )(tab, _in_hbm(h), *fulls)
    return outs[0], outs[1], outs[2:]


def _bias_tables(rel_bias, buckets):
    def body(tab_ref, bk_ref, o_ref):
        a = lax.broadcasted_iota(jnp.int32, (BLK, 2 * BLK), 0)
        b = lax.broadcasted_iota(jnp.int32, (BLK, 2 * BLK), 1)
        steps = a + BLK - b
        valid = (steps >= 0) & (steps <= BLK)
        for g in range(3):
            bk = bk_ref[g]
            for j in range(4):
                def pick(kk, acc, bk=bk, col=4 * g + j):
                    return jnp.where(bk == kk, tab_ref[kk, col], acc)

                acc = lax.fori_loop(0, N_BUCKETS, pick, jnp.zeros((BLK, 2 * BLK), F32))
                o_ref[g, j] = jnp.where(valid, acc, NEG)

    return pl.pallas_call(
        body, name="bias_tables", out_shape=SDS((3, 4, BLK, 2 * BLK), F32),
        in_specs=[pl.BlockSpec(memory_space=pltpu.SMEM), VMEM_SPEC], out_specs=VMEM_SPEC,
    )(rel_bias, buckets)


def _bias_grad(ds_sum, buckets):
    def body(ds_ref, bk_ref, o_ref, part_ref):
        lane = lax.broadcasted_iota(jnp.int32, (N_BUCKETS, 128), 1)
        for g in range(3):
            def bucket(kk, carry, g=g):
                mine = bk_ref[g] == kk
                for j in range(4):
                    v = jnp.sum(jnp.where(mine, ds_ref[g, j], 0.0).reshape(BLK // 8, 8, 2 * BLK), axis=0)
                    part_ref[j, pl.ds(pl.multiple_of(kk * 8, 8), 8), :] = v[:, :BLK] + v[:, BLK:]
                return carry

            lax.fori_loop(0, N_BUCKETS, bucket, 0)
            out = jnp.zeros((N_BUCKETS, 128), F32)
            for j in range(4):
                rows = jnp.sum(part_ref[j], axis=1, keepdims=True)
                out = jnp.where(lane == j, jnp.sum(rows.reshape(N_BUCKETS, 8, 1), axis=1), out)
            o_ref[g] = out

    return pl.pallas_call(body, name="bias_grad", out_shape=SDS((3, N_BUCKETS, 128), F32), in_specs=[VMEM_SPEC, VMEM_SPEC],
                          out_specs=VMEM_SPEC, scratch_shapes=[pltpu.VMEM((4, N_BUCKETS * 8, 128), F32)])(ds_sum, buckets)


def _sub_rows(d, r, first, size):
    return pl.ds(first * d + r, size) if d == 1 else pl.ds(first * d + r, size, stride=d)


def _head_spec(seq, g, part):
    return pl.BlockSpec((seq, HD), lambda b, hh: (b, part * (QW // HD) + 4 * g + hh))


def _rows(start, count, stride):
    return pl.ds(start, count) if stride == 1 else pl.ds(start, count, stride=stride)


def _gather_rows(dst, dst0, src, src0, stride, count):
    for first in range(0, count, BLK):
        dst[pl.ds(dst0 + first, BLK), :] = src[_rows(src0 + first * stride, BLK, stride), :].astype(dst.dtype)


def _scatter_rows(dst, dst0, stride, src, src0, count):
    for first in range(0, count, BLK):
        dst[_rows(dst0 + first * stride, BLK, stride), :] = src[pl.ds(src0 + first, BLK), :].astype(dst.dtype)


def _by_subsequence(dst, src, d, wide=None, tmp=None):
    seq = src.shape[0]
    ln = seq // d
    if wide is not None:
        wide[...] = src[...].astype(F32)
        src = wide
    if d <= 4:
        for r in range(d):
            _gather_rows(dst, r * ln, src, r, d, ln)
    else:
        quarter = seq // 4
        for r4 in range(4):
            _gather_rows(tmp, r4 * quarter, src, r4, 4, quarter)
        for r4 in range(4):
            for a in range(d // 4):
                _gather_rows(dst, (4 * a + r4) * ln, tmp, r4 * quarter + a, d // 4, ln)


def _to_sequence(dst, src, d, tmp=None):
    seq = dst.shape[0]
    ln = seq // d
    if d <= 4:
        for r in range(d):
            _scatter_rows(dst, r, d, src, r * ln, ln)
    else:
        quarter = seq // 4
        for r4 in range(4):
            for a in range(d // 4):
                _scatter_rows(tmp, r4 * quarter + a, d // 4, src, (4 * a + r4) * ln, ln)
        for r4 in range(4):
            _scatter_rows(dst, r4, 4, tmp, r4 * quarter, quarter)


def _attn_forward(g, qkv, bias, bsz, seq):
    d = DILATIONS[g]
    ln = seq // d
    units = [(r, n) for r in range(d) for n in range(ln // BLK)]

    def band(n):
        return slice(BLK, 2 * BLK) if n == 0 else slice(0, 2 * BLK)

    def body(q_ref, k_ref, v_ref, b_ref, o_ref, l_ref, *scratch):
        hs = pl.program_id(1)
        s_scr, p_scr = scratch[:2]
        if d == 1:
            qd, kd, vd = q_ref, k_ref, v_ref
        else:
            wide, tmp, qd, kd, vd = scratch[2:7]
            for dst, src in ((qd, q_ref), (kd, k_ref), (vd, v_ref)):
                _by_subsequence(dst, src, d, wide, tmp)
        blk = lambda r, n: pl.ds(r * ln + n * BLK, BLK)
        direct = d <= 4
        out_rows = (lambda r, n: _sub_rows(d, r, n * BLK, BLK)) if direct else blk
        o_dst, l_dst = (o_ref, l_ref) if direct else scratch[7:9]
        for u, (r, n) in enumerate(units):
            s_scr[u, :, BLK:] = _dot_nt(qd[blk(r, n), :], kd[blk(r, n), :])
            if n > 0:
                s_scr[u, :, :BLK] = _dot_nt(qd[blk(r, n), :], kd[blk(r, n - 1), :])
        for u, (r, n) in enumerate(units):
            s = s_scr[u, :, band(n)] * SCALE + b_ref[hs, :, band(n)]
            m = jnp.max(s, axis=1, keepdims=True)
            e = jnp.exp(s - m)
            den = jnp.sum(e, axis=1, keepdims=True)
            p_scr[u, :, band(n)] = (e * (1.0 / den)).astype(BF16)
            l_dst[out_rows(r, n), :] = jnp.broadcast_to(m + jnp.log(den), (BLK, HD))
        for u, (r, n) in enumerate(units):
            acc = _dot(p_scr[u, :, BLK:], vd[blk(r, n), :])
            if n > 0:
                acc = acc + _dot(p_scr[u, :, :BLK], vd[blk(r, n - 1), :])
            o_dst[out_rows(r, n), :] = acc
        if not direct:
            _to_sequence(o_ref, o_dst, d, tmp)
            _to_sequence(l_ref, l_dst, d, tmp)

    rows_f32, rows_bf16 = pltpu.VMEM((seq, HD), F32), pltpu.VMEM((seq, HD), BF16)
    regrouped = [] if d == 1 else [rows_f32] * 2 + [rows_bf16] * 3 + ([] if d <= 4 else [rows_f32] * 2)
    out_spec = pl.BlockSpec((seq, HD), lambda b, hh: (b, hh))
    return pl.pallas_call(
        body, name=f"attn_forward_{g}", out_shape=[HBM_OUT((bsz * seq, AW), F32)] * 2, grid=(bsz, 4),
        in_specs=[_head_spec(seq, g, part) for part in range(3)] + [pl.BlockSpec((4, BLK, 2 * BLK), lambda b, hh: (0, 0, 0))],
        out_specs=[out_spec, out_spec],
        scratch_shapes=[pltpu.VMEM((len(units), BLK, 2 * BLK), F32), pltpu.VMEM((len(units), BLK, 2 * BLK), BF16)] + regrouped,
        compiler_params=_cp(("parallel", "parallel"), VMEM_CAP // 2),
    )(qkv, qkv, qkv, _in_hbm(bias))


def _attn_backward(g, qkv, do, dl, bias, prev_out, bsz, seq):
    d = DILATIONS[g]
    ln = seq // d
    units = [(r, n) for r in range(d) for n in range(ln // BLK)]

    def body(q_ref, k_ref, v_ref, do_ref, dl_ref, b_ref, *rest):
        dq_ref, dk_ref, dv_ref, db_ref = rest[-18:-14]
        wide, tmp, qd, kd, vd, dod, dld, dqd, dkd, dvd, s_scr, dp_scr, p_scr, ds_scr = rest[-14:]
        hs = pl.program_id(1)

        @pl.when((pl.program_id(0) == 0) & (hs == 0))
        def _():
            db_ref[...] = jnp.zeros_like(db_ref)

        for dst, src in ((qd, q_ref), (kd, k_ref), (vd, v_ref)):
            _by_subsequence(dst, src, d, wide, tmp)
        _by_subsequence(dod, do_ref, d, None, tmp)
        _by_subsequence(dld, dl_ref, d, None, tmp)
        dkd[...] = jnp.zeros_like(dkd)
        dvd[...] = jnp.zeros_like(dvd)
        blk = lambda r, n: pl.ds(r * ln + n * BLK, BLK)
        keys = lambda r, n: [(blk(r, n), slice(BLK, 2 * BLK))] + ([(blk(r, n - 1), slice(0, BLK))] if n > 0 else [])
        for u, (r, n) in enumerate(units):
            for rows, band in keys(r, n):
                s_scr[u, :, band] = _dot_nt(qd[blk(r, n), :], kd[rows, :])
                dp_scr[u, :, band] = _dot_nt(dod[blk(r, n), :], vd[rows, :])
        for u, (r, n) in enumerate(units):
            both = dld[blk(r, n), :]
            lse, delta = both[:, 0:1], both[:, 64:65]
            band = slice(BLK, 2 * BLK) if n == 0 else slice(0, 2 * BLK)
            p = jnp.exp(s_scr[u, :, band] * SCALE + b_ref[hs, :, band] - lse)
            ds = p * (dp_scr[u, :, band] - delta)
            p_scr[u, :, band] = p.astype(BF16)
            ds_scr[u, :, band] = ds.astype(BF16)
            db_ref[hs, :, band] += ds
        for u, (r, n) in enumerate(units):
            dq = jnp.zeros((BLK, HD), F32)
            for rows, band in keys(r, n):
                dvd[rows, :] += _dot_tn(p_scr[u, :, band], dod[blk(r, n), :])
                dkd[rows, :] += _dot_tn(ds_scr[u, :, band], qd[blk(r, n), :]) * SCALE
                dq = dq + _dot(ds_scr[u, :, band], kd[rows, :])
            dqd[blk(r, n), :] = dq * SCALE
        for out, acc in ((dq_ref, dqd), (dk_ref, dkd), (dv_ref, dvd)):
            if d == 1:
                out[...] = acc[...].astype(BF16)
            else:
                _to_sequence(wide, acc, d, tmp)
                out[...] = wide[...].astype(BF16)

    qkv_spec = _head_spec(seq, g, 0)
    out_spec = pl.BlockSpec((seq, HD), lambda b, hh: (b, hh))
    band_spec = pl.BlockSpec((4, BLK, 2 * BLK), lambda b, hh: (0, 0, 0))
    ins = [qkv, qkv, qkv, _in_hbm(do), _in_hbm(dl), _in_hbm(bias)]
    in_specs = [_head_spec(seq, g, part) for part in range(3)] + [out_spec, out_spec, band_spec]
    aliases = {}
    if prev_out is not None:
        ins += list(prev_out)
        in_specs += [ANY] * 3
        aliases = {6: 0, 7: 1, 8: 2}
    rows_bf16, rows_f32 = pltpu.VMEM((seq, HD), BF16), pltpu.VMEM((seq, HD), F32)
    staged = [pltpu.VMEM((len(units), BLK, 2 * BLK), F32)] * 2 + [pltpu.VMEM((len(units), BLK, 2 * BLK), BF16)] * 2
    dq, dk, dv, db = pl.pallas_call(
        body, name=f"attn_backward_{g}", out_shape=[HBM_OUT((bsz * seq, QW), BF16)] * 3 + [SDS((4, BLK, 2 * BLK), F32)], grid=(bsz, 4),
        in_specs=in_specs, out_specs=[qkv_spec] * 3 + [band_spec], input_output_aliases=aliases,
        scratch_shapes=[rows_f32] * 2 + [rows_bf16] * 4 + [rows_f32] * 4 + staged,
        compiler_params=_cp(("arbitrary", "arbitrary"), VMEM_CAP // 2),
    )(*ins)
    return (dq, dk, dv), db


def _mix_forward(gates, og, lg, x2, tgt, gate, w_ao, w_co, w_o, conv_w, conv_b, ln_g, ln_b, bsz, seq, tm=256):
    t = x2.shape[0]
    spt = seq // tm

    def body(g_ref, o1, o2, o3, l1, l2, l3, x_ref, t_ref, gate_ref, wao_ref, wco_ref, wo_ref, cw_ref, cb_ref, lng_ref, lnb_ref,
             ain_ref, sin_ref, mrg_ref, dy_ref, aout_ref, sout_ref, yc_ref, o_ref, lj_ref, dxr_ref, vec_ref, dgate_ref, zc_ref):
        b, i = pl.program_id(0), pl.program_id(1)

        @pl.when((b == 0) & (i == 0))
        def _():
            vec_ref[...] = jnp.zeros_like(vec_ref)

        @pl.when(i == 0)
        def _():
            zc_ref[...] = jnp.zeros_like(zc_ref)
            dgate_ref[...] = jnp.zeros_like(dgate_ref)

        g_attn, u, bg, cg, g_conv, m_attn, m_conv = (g_ref[:, lo:hi].astype(F32) for lo, hi in GATE_COLS)
        la, lb, lc = l1[...], l2[...], l3[...]
        mx = jnp.maximum(la, jnp.maximum(lb, lc))
        ea, eb, ec = jnp.exp(la - mx), jnp.exp(lb - mx), jnp.exp(lc - mx)
        den = ea + eb + ec
        o = (ea * o1[...] + eb * o2[...] + ec * o3[...]) / den
        o_ref[...] = o
        lj_ref[...] = mx + jnp.log(den)
        a_in = o * (g_attn * _sig(g_attn))
        ain_ref[...] = a_in.astype(BF16)
        a_out = _dot(a_in.astype(BF16), wao_ref[...])
        aout_ref[...] = a_out.astype(BF16)
        z = cg * u
        rows = lax.broadcasted_iota(jnp.int32, (tm, D), 0)
        c6, c7 = zc_ref[6:7, :], zc_ref[7:8, :]
        z1 = jnp.where(rows == 0, c7, pltpu.roll(z, 1, 0))
        z2 = jnp.where(rows == 0, c6, jnp.where(rows == 1, c7, pltpu.roll(z, 2, 0)))
        zc_ref[...] = z[tm - 8:tm, :]
        y_conv = (cw_ref[0:1, :] * z2 + cw_ref[1:2, :] * z1 + cw_ref[2:3, :] * z) + cb_ref[...]
        yc_ref[...] = y_conv.astype(BF16)
        s_in = bg * y_conv * (g_conv * _sig(g_conv))
        sin_ref[...] = s_in.astype(BF16)
        s_out = _dot(s_in.astype(BF16), wco_ref[...])
        sout_ref[...] = s_out.astype(BF16)
        merged = _sig(m_attn) * a_out + _sig(m_conv) * s_out
        mrg_ref[...] = merged.astype(BF16)
        y = _dot(merged.astype(BF16), wo_ref[...])
        gate1 = 1.0 + gate_ref[0]
        r = ALPHA * x_ref[...] + gate1 * y
        mu = jnp.mean(r, axis=1, keepdims=True)
        rc = r - mu
        rstd = lax.rsqrt(jnp.mean(rc * rc, axis=1, keepdims=True) + LN_EPS)
        xhat = rc * rstd
        diff = (xhat * lng_ref[...] + lnb_ref[...]) - t_ref[...]
        dout = diff * (1.0 / D)
        vec_ref[0:1, :] += jnp.sum(dout * xhat, axis=0, keepdims=True)
        vec_ref[1:2, :] += jnp.sum(dout, axis=0, keepdims=True)
        vec_ref[2:3, :] += jnp.sum(diff * diff, axis=0, keepdims=True)
        dxh = dout * lng_ref[...]
        dr = rstd * (dxh - jnp.mean(dxh, axis=1, keepdims=True) - xhat * jnp.mean(dxh * xhat, axis=1, keepdims=True))
        dxr_ref[...] = ALPHA * dr
        dy_ref[...] = (dr * gate1).astype(BF16)
        dgate_ref[0] += jnp.sum(dr * y, axis=0, keepdims=True)

    tok = lambda w: pl.BlockSpec((tm, w), lambda b, i: (b * spt + i, 0))
    const = lambda s: pl.BlockSpec(s, lambda b, i: (0,) * len(s))
    per_seq = pl.BlockSpec((1, 1, D), lambda b, i: (b, 0, 0))
    outs = pl.pallas_call(
        body, name="mix_forward", grid=(bsz, spt),
        out_shape=[HBM_OUT((t, AW), BF16), HBM_OUT((t, D), BF16), HBM_OUT((t, D), BF16), HBM_OUT((t, D), BF16), HBM_OUT((t, D), BF16),
                   HBM_OUT((t, D), BF16), HBM_OUT((t, D), BF16), HBM_OUT((t, AW), F32), HBM_OUT((t, AW), F32), HBM_OUT((t, D), F32),
                   SDS((8, D), F32), SDS((bsz, 1, D), F32)],
        in_specs=[tok(NGATE)] + [tok(AW)] * 6 + [tok(D), tok(D), per_seq, const((AW, D)), const((D, D)), const((D, D)),
                                                 const((3, D)), const((1, D)), const((1, D)), const((1, D))],
        out_specs=[tok(AW), tok(D), tok(D), tok(D), tok(D), tok(D), tok(D), tok(AW), tok(AW), tok(D), const((8, D)), per_seq],
        scratch_shapes=[pltpu.VMEM((8, D), F32)],
        compiler_params=_cp(("arbitrary", "arbitrary"), VMEM_CAP),
    )(_in_hbm(gates), *map(_in_hbm, og), *map(_in_hbm, lg), _in_hbm(x2), _in_hbm(tgt), gate, w_ao, w_co, w_o, conv_w, conv_b, ln_g, ln_b)
    return outs


def _mix_backward(gates, dy, a_out, s_out, y_conv, o, lj, w_ao, w_co, w_o, conv_w, vec_f, bsz, seq, tm=256):
    t = dy.shape[0]
    spt = seq // tm

    def body(g_ref, dy_ref, aout_ref, sout_ref, yc_ref, o_ref, lj_ref, wao_ref, wco_ref, wo_ref, cw_ref, vecf_ref,
             dg_ref, do_ref, dl_ref, daout_ref, dsout_ref, vec_ref, car_ref):
        b, i = pl.program_id(0), pl.program_id(1)

        @pl.when((b == 0) & (i == 0))
        def _():
            vec_ref[...] = vecf_ref[...]

        @pl.when(i == 0)
        def _():
            car_ref[...] = jnp.zeros_like(car_ref)

        g_attn, u, bg, cg, g_conv, m_attn, m_conv = (g_ref[:, lo:hi].astype(F32) for lo, hi in GATE_COLS)
        dmerged = _dot_nt(dy_ref[...], wo_ref[...])
        sa, sc = _sig(m_attn), _sig(m_conv)
        da_out = (dmerged * sa).astype(BF16)
        ds_out = (dmerged * sc).astype(BF16)
        daout_ref[...] = da_out
        dsout_ref[...] = ds_out
        dg_ref[:, 4608:5632] = (dmerged * aout_ref[...].astype(F32) * (sa * (1.0 - sa))).astype(BF16)
        dg_ref[:, 5632:6656] = (dmerged * sout_ref[...].astype(F32) * (sc * (1.0 - sc))).astype(BF16)
        da_in = _dot_nt(da_out, wao_ref[...])
        ds_in = _dot_nt(ds_out, wco_ref[...])
        sga = _sig(g_attn)
        o = o_ref[...]
        do = da_in * (g_attn * sga)
        do_ref[...] = do
        dg_ref[:, 0:512] = (da_in * o * (sga * (1.0 + g_attn * (1.0 - sga)))).astype(BF16)
        prod = do * o
        lane = lax.broadcasted_iota(jnp.int32, (tm, HD), 1)
        for j in range(4):
            cs = slice(j * HD, (j + 1) * HD)
            delta = jnp.sum(prod[:, cs], axis=1, keepdims=True)
            dl_ref[:, cs] = jnp.where(lane < 64, lj_ref[:, cs], delta)
        sgc = _sig(g_conv)
        silu_c = g_conv * sgc
        yc = yc_ref[...].astype(F32)
        dg_ref[:, 1536:2560] = (ds_in * yc * silu_c).astype(BF16)
        dg_ref[:, 3584:4608] = (ds_in * bg * yc * (sgc * (1.0 + g_conv * (1.0 - sgc)))).astype(BF16)
        dyc = ds_in * bg * silu_c
        rows = lax.broadcasted_iota(jnp.int32, (tm, D), 0)
        c0, c1 = car_ref[0:1, :], car_ref[1:2, :]
        n1 = jnp.where(rows == tm - 1, c0, pltpu.roll(dyc, tm - 1, 0))
        n2 = jnp.where(rows == tm - 2, c0, jnp.where(rows == tm - 1, c1, pltpu.roll(dyc, tm - 2, 0)))
        car_ref[...] = dyc[0:8, :]
        dz = cw_ref[2:3, :] * dyc + cw_ref[1:2, :] * n1 + cw_ref[0:1, :] * n2
        z = cg * u
        dg_ref[:, 512:1536] = (dz * cg).astype(BF16)
        dg_ref[:, 2560:3584] = (dz * u).astype(BF16)
        vec_ref[3:4, :] += jnp.sum(n2 * z, axis=0, keepdims=True)
        vec_ref[4:5, :] += jnp.sum(n1 * z, axis=0, keepdims=True)
        vec_ref[5:6, :] += jnp.sum(dyc * z, axis=0, keepdims=True)
        vec_ref[6:7, :] += jnp.sum(dyc, axis=0, keepdims=True)

    tok = lambda w: pl.BlockSpec((tm, w), lambda b, i: (b * spt + (spt - 1 - i), 0))
    const = lambda s: pl.BlockSpec(s, lambda b, i: (0,) * len(s))
    return pl.pallas_call(
        body, name="mix_backward", grid=(bsz, spt),
        out_shape=[HBM_OUT((t, NGATE), BF16), HBM_OUT((t, AW), F32), HBM_OUT((t, AW), F32), HBM_OUT((t, D), BF16), HBM_OUT((t, D), BF16),
                   SDS((8, D), F32)],
        in_specs=[tok(NGATE), tok(D), tok(D), tok(D), tok(D), tok(AW), tok(AW), const((AW, D)), const((D, D)), const((D, D)), const((3, D)),
                  const((8, D))],
        out_specs=[tok(NGATE), tok(AW), tok(AW), tok(D), tok(D), const((8, D))],
        scratch_shapes=[pltpu.VMEM((8, D), F32)],
        compiler_params=_cp(("arbitrary", "arbitrary"), VMEM_CAP),
    )(*map(_in_hbm, (gates, dy, a_out, s_out, y_conv, o, lj)), w_ao, w_co, w_o, conv_w, vec_f)


def _scatter_copies(src, land, send_sems, recv_sems):
    x, y, c = _place()
    chips = [(1 - x, y), (x, 1 - y), (1 - x, 1 - y)]
    return [pltpu.make_async_remote_copy(src_ref=src[a].at[2 * cx + cy], dst_ref=land[a].at[r], send_sem=send_sems.at[3 * a + r],
                                         recv_sem=recv_sems.at[3 * a + r], device_id=(cx, cy, c), device_id_type=MESH)
            for a in range(len(src)) for r, (cx, cy) in enumerate(chips)]


def _halves_out(a):
    kind, nr, nc = W_CUTS[a]
    shape = (nr // 2, W_FULL[a][1]) if kind == "col" else (NCHIP, nr // 2, nc)
    return [SDS(shape, F32), SDS(shape, BF16)]


def _write_halves(a, acc_ref, c, mine_ref, theirs_ref):
    kind, nr, nc = W_CUTS[a]
    hr = nr // 2
    if kind == "col":
        mine_ref[...] = acc_ref[pl.ds(pl.multiple_of(c * hr, hr), hr), :]
        theirs_ref[...] = acc_ref[pl.ds(pl.multiple_of((1 - c) * hr, hr), hr), :].astype(BF16)
    else:
        for k in range(NCHIP):
            mine_ref[k] = acc_ref[pl.ds(pl.multiple_of(k * nr + c * hr, hr), hr), :]
            theirs_ref[k] = acc_ref[pl.ds(pl.multiple_of(k * nr + (1 - c) * hr, hr), hr), :].astype(BF16)


def _out_weight_grads(a_in, da_out, s_in, ds_out, merged, dy, core, tk=1024):
    t = dy.shape[0]
    nt = t // tk

    def body(c_ref, ain_ref, da_ref, sin_ref, ds_ref, m_ref, dy_ref, *rest):
        outs, (gao, gco, go) = rest[:6], rest[6:]

        @pl.when(pl.program_id(0) == 0)
        def _():
            gao[...] = jnp.zeros_like(gao)
            gco[...] = jnp.zeros_like(gco)
            go[...] = jnp.zeros_like(go)

        gao[...] += _dot_tn(ain_ref[...], da_ref[...])
        gco[...] += _dot_tn(sin_ref[...], ds_ref[...])
        go[...] += _dot_tn(m_ref[...], dy_ref[...])

        @pl.when(pl.program_id(0) == nt - 1)
        def _():
            for a, acc in ((1, gao), (2, gco), (3, go)):
                _write_halves(a, acc, c_ref[0], outs[2 * a - 2], outs[2 * a - 1])

    tok = lambda w: pl.BlockSpec((tk, w), lambda i, cr: (i, 0))
    out_shape = _halves_out(1) + _halves_out(2) + _halves_out(3)
    outs = pl.pallas_call(
        body, name="out_weight_grads", out_shape=out_shape,
        grid_spec=pltpu.PrefetchScalarGridSpec(
            num_scalar_prefetch=1, grid=(nt,), in_specs=[tok(AW), tok(D), tok(D), tok(D), tok(D), tok(D)],
            out_specs=[pl.BlockSpec(o.shape, lambda i, cr, nd=len(o.shape): (0,) * nd) for o in out_shape],
            scratch_shapes=[pltpu.VMEM((AW, D), F32), pltpu.VMEM((D, D), F32), pltpu.VMEM((D, D), F32)]),
        compiler_params=_cp(("arbitrary",), VMEM_CAP),
    )(core, a_in, da_out, s_in, ds_out, merged, dy)
    return [(outs[0], outs[1]), (outs[2], outs[3]), (outs[4], outs[5])]


def _input_grad(dq, dk, dv, dgates, w, x2, dxr, sc1p, seq, sums, tm=512):
    t = x2.shape[0]
    nt = t // tm
    spt = seq // tm
    bsz = t // seq
    n = len(sums)
    gblk = NGATE // 4
    nsteps = 3 + 4

    def body(dq_ref, dk_ref, dv_ref, dg_ref, wq_ref, wg_ref, x_ref, dxr_ref, sc_ref, *rest):
        src, (dx_ref, dsh_ref, dsc_ref), land = rest[:n], rest[n:n + 3], rest[n + 3:2 * n + 3]
        acc_ref, send_sems, recv_sems = rest[2 * n + 3:]
        j, i = pl.program_id(0), pl.program_id(1)
        copies = _scatter_copies(src, land, send_sems, recv_sems)
        rows = pl.ds(pl.multiple_of(i * tm, tm), tm)

        @pl.when((i == 0) & (j == 0))
        def _():
            for cp in copies:
                cp.start()

        for k, ref in enumerate((dq_ref, dk_ref, dv_ref)):
            @pl.when(j == k)
            def _(k=k, ref=ref):
                part = _dot_nt(ref[...], wq_ref[...])
                if k == 0:
                    acc_ref[rows, :] = part
                else:
                    acc_ref[rows, :] += part

        @pl.when((j >= 3) & (j < nsteps - 1))
        def _():
            acc_ref[rows, :] += _dot_nt(dg_ref[...], wg_ref[...])

        @pl.when(j == nsteps - 1)
        def _():
            dh = acc_ref[rows, :] + _dot_nt(dg_ref[...], wg_ref[...])
            dx_ref[...] = dh * sc_ref[0] + dxr_ref[...]

            @pl.when(i % spt == 0)
            def _():
                dsh_ref[...] = jnp.zeros_like(dsh_ref)
                dsc_ref[...] = jnp.zeros_like(dsc_ref)

            dsh_ref[0] += jnp.sum(dh, axis=0, keepdims=True)
            dsc_ref[0] += jnp.sum(dh * x_ref[...], axis=0, keepdims=True)

        @pl.when((i == nt - 1) & (j == nsteps - 1))
        def _():
            for cp in copies:
                cp.wait()

    def held(k):
        return lambda j, i: (jnp.where(j == k, i, jnp.where(j < k, 0, nt - 1)), 0)

    last = lambda j, i: (jnp.where(j == nsteps - 1, i, 0), 0)
    outs = pl.pallas_call(
        body, name="input_grad", grid=(nsteps, nt),
        out_shape=[SDS((t, D), F32), SDS((bsz, 1, D), F32), SDS((bsz, 1, D), F32)] + [SDS((3,) + s.shape[1:], BF16) for s in sums],
        in_specs=[pl.BlockSpec((tm, QW), held(0)), pl.BlockSpec((tm, QW), held(1)), pl.BlockSpec((tm, QW), held(2)),
                  pl.BlockSpec((tm, gblk), lambda j, i: (jnp.where(j >= 3, i, 0), jnp.clip(j - 3, 0, 3))),
                  pl.BlockSpec((D, QW), lambda j, i: (0, jnp.minimum(j, 2))),
                  pl.BlockSpec((pl.Element(D), pl.Element(gblk)), lambda j, i: (0, pl.multiple_of(3 * QW + gblk * jnp.clip(j - 3, 0, 3), 128))),
                  pl.BlockSpec((tm, D), last), pl.BlockSpec((tm, D), last),
                  pl.BlockSpec((1, 1, D), lambda j, i: (jnp.where(j == nsteps - 1, i // spt, 0), 0, 0))] + [ANY] * n,
        out_specs=[pl.BlockSpec((tm, D), last),
                   pl.BlockSpec((1, 1, D), lambda j, i: (jnp.where(j == nsteps - 1, i // spt, 0), 0, 0)),
                   pl.BlockSpec((1, 1, D), lambda j, i: (jnp.where(j == nsteps - 1, i // spt, 0), 0, 0))] + [ANY] * n,
        scratch_shapes=[pltpu.VMEM((t, D), F32), pltpu.SemaphoreType.DMA((3 * NCHIP,)), pltpu.SemaphoreType.DMA((3 * NCHIP,))],
        compiler_params=_cp(("arbitrary", "arbitrary"), VMEM_CAP, side=True),
    )(*map(_in_hbm, (dq, dk, dv, dgates, w, w, x2, dxr)), sc1p, *sums)
    return outs[0], outs[1], outs[2], outs[3:]


def _in_weight_grad(ht, dq, dk, dv, dgates, core, sums):
    t = ht.shape[1]
    hr = D // 2
    n = len(sums)

    def body(c_ref, ht_ref, dq_ref, dk_ref, dv_ref, dg_ref, *rest):
        src, mine_ref, got_ref, land = rest[:n], rest[n], rest[n + 1], rest[n + 2:2 * n + 2]
        acc_ref, their_buf, send_sems, recv_sems, tile_send, tile_recv = rest[2 * n + 2:]
        j = pl.program_id(0)
        slot = j % 2
        px, py, pc = _place()
        copies = _scatter_copies(src, land, send_sems, recv_sems)

        def to_sibling(step, k):
            return pltpu.make_async_remote_copy(src_ref=their_buf.at[k], dst_ref=got_ref.at[:, pl.ds(pl.multiple_of(step * TN, TN), TN)],
                                                send_sem=tile_send.at[k], recv_sem=tile_recv.at[0], device_id=(px, py, 1 - pc),
                                                device_id_type=MESH)

        @pl.when(j == 0)
        def _():
            for cp in copies:
                cp.start()

        @pl.when(j >= 2)
        def _():
            to_sibling(j - 2, slot).wait_send()

        for k, ref in enumerate((dq_ref, dk_ref, dv_ref)):
            @pl.when((j >= k * NQT) & (j < (k + 1) * NQT))
            def _(ref=ref):
                acc_ref[...] = _dot(ht_ref[...], ref[...])

        @pl.when(j >= 3 * NQT)
        def _():
            acc_ref[...] = _dot(ht_ref[...], dg_ref[...])

        _write_halves(0, acc_ref, c_ref[0], mine_ref, their_buf.at[slot])
        to_sibling(j, slot).start()

        @pl.when(j == NPT - 1)
        def _():
            to_sibling(j - 1, 1 - slot).wait_send()
            to_sibling(j, slot).wait_send()
            pltpu.make_async_remote_copy(src_ref=got_ref, dst_ref=got_ref, send_sem=tile_send.at[0], recv_sem=tile_recv.at[0],
                                         device_id=(px, py, 1 - pc), device_id_type=MESH).wait_recv()
            for cp in copies:
                cp.wait()

    def part(k):
        return pl.BlockSpec((t, TN), lambda j, cr: (0, jnp.clip(j - k * NQT, 0, NQT - 1)))

    outs = pl.pallas_call(
        body, name="in_weight_grad", out_shape=[SDS((hr, NCOL), F32), SDS((hr, NCOL), BF16)] + [SDS((3,) + v.shape[1:], BF16) for v in sums],
        grid_spec=pltpu.PrefetchScalarGridSpec(
            num_scalar_prefetch=1, grid=(NPT,),
            in_specs=[pl.BlockSpec((D, t), lambda j, cr: (0, 0)), part(0), part(1), part(2),
                      pl.BlockSpec((t, TN), lambda j, cr: (0, jnp.maximum(j - 3 * NQT, 0)))] + [ANY] * n,
            out_specs=[pl.BlockSpec((hr, TN), lambda j, cr: (0, j)), ANY] + [ANY] * n,
            scratch_shapes=[pltpu.VMEM((D, TN), F32), pltpu.VMEM((2, hr, TN), BF16),
                            pltpu.SemaphoreType.DMA((3 * NCHIP,)), pltpu.SemaphoreType.DMA((3 * NCHIP,)),
                            pltpu.SemaphoreType.DMA((2,)), pltpu.SemaphoreType.DMA((1,))]),
        compiler_params=_cp(("arbitrary",), VMEM_CAP, side=True),
    )(core, *map(_in_hbm, (ht, dq, dk, dv, dgates)), *sums)
    return outs[0], outs[1], outs[2:]


def _sum_partials(gathered):
    def body(g_ref, o_ref):
        acc = g_ref[0]
        for k in range(1, 8):
            acc = acc + g_ref[k]
        o_ref[...] = acc

    return pl.pallas_call(body, name="sum_partials", out_shape=SDS(gathered.shape[1:], F32), in_specs=[VMEM_SPEC], out_specs=VMEM_SPEC)(gathered)


def _adamw(w, g, m, v, name, tr=256):
    r, cdim = w.shape
    tr = tr if cdim <= D else tr // 2
    tr = tr if (r % tr == 0 and r > tr) else r

    def body(w_ref, g_ref, m_ref, v_ref, go_ref, d_ref, nm_ref, nv_ref):
        gv = g_ref[...]
        go_ref[...] = gv
        nm = B1 * m_ref[...] + (1.0 - B1) * gv
        nv = B2 * v_ref[...] + (1.0 - B2) * (gv * gv)
        m_hat = nm / (1.0 - B1 ** STEP)
        v_hat = nv / (1.0 - B2 ** STEP)
        d_ref[...] = -LR * (m_hat / (jnp.sqrt(v_hat) + EPS) + WD * w_ref[...])
        nm_ref[...] = nm
        nv_ref[...] = nv

    spec = pl.BlockSpec((tr, cdim), lambda i: (i, 0))
    return pl.pallas_call(
        body, name=name, grid=(r // tr,), out_shape=[SDS((r, cdim), F32)] * 4, in_specs=[spec] * 4, out_specs=[spec] * 4,
        compiler_params=_cp(("parallel",), VMEM_CAP // 2),
    )(w, g, m, v)


def _t5_bucket(dist):
    n = jnp.maximum(dist, 1).astype(F32)
    large = MAX_EXACT + (jnp.log(n / MAX_EXACT) / math.log(MAX_DISTANCE / MAX_EXACT) * (N_BUCKETS - MAX_EXACT)).astype(jnp.int32)
    large = jnp.minimum(large, N_BUCKETS - 1)
    return jnp.where(dist < MAX_EXACT, dist, large)


def _band_buckets():
    a = jnp.arange(BLK)[:, None]
    b = jnp.arange(2 * BLK)[None, :]
    steps = jnp.maximum(a + BLK - b, 0)
    return jnp.stack([_t5_bucket(steps * d) for d in DILATIONS]).astype(jnp.int32)


def _pad_rows(a, rows=8):
    return jnp.pad(a, ((0, rows - a.shape[0]), (0, 0)))


def kernel(x, c, w_ada, b_ada, w_in, conv_w, conv_b, rel_bias, w_attn_out, w_conv_out, w_o, ln_g, ln_b, loss_target, m_w_ada, m_b_ada, m_w_in, m_conv_w, m_conv_b, m_rel_bias, m_w_attn_out, m_w_conv_out, m_w_o, m_ln_g, m_ln_b, v_w_ada, v_b_ada, v_w_in, v_conv_w, v_conv_b, v_rel_bias, v_w_attn_out, v_w_conv_out, v_w_o, v_ln_g, v_ln_b):
    bsz, seq, _ = x.shape
    t = bsz * seq
    mx, my, mc = _place()
    chip = 2 * mx + my
    dev = 4 * mx + 2 * my + mc
    x2 = x.reshape(t, D)
    tgt = loss_target.reshape(t, D)

    mine = _to_bf16_windows([w[0] for w in (w_in, w_attn_out, w_conv_out, w_o)])

    n_ada = w_ada.shape[2]
    n_cw = conv_w.shape[2]
    c_and_cw = jnp.concatenate([_pad_rows(c), jnp.pad(conv_w[0], ((0, 5), (0, D - n_cw)))], axis=0)
    firsts = _all_gather8(c_and_cw, "gather_c_conv_w")
    c_all = firsts[:, 0:bsz, :].reshape(8 * bsz, D)
    conv_w_f = firsts[0::2, 8:11, 0:n_cw].transpose(1, 0, 2).reshape(3, D)
    b_cols = lax.dynamic_slice(b_ada, (0, chip * n_ada), (1, n_ada))
    mod_part = _ada_forward(c_all, w_ada[0], b_cols)
    mod_parts = _all_gather8(mod_part, "gather_mod")
    mod_all = mod_parts[0::2].transpose(1, 0, 2).reshape(8 * bsz, 3 * D)
    mod = lax.dynamic_slice(mod_all, (dev * bsz, 0), (bsz, 3 * D))
    shift = mod[:, 0:D].reshape(bsz, 1, D)
    sc1p = 1.0 + mod[:, D:2 * D].reshape(bsz, 1, D)
    gate = mod[:, 2 * D:].reshape(bsz, 1, D)

    h, ht = _modulate(x2, sc1p, shift, seq)
    tab = lax.dynamic_index_in_dim(jnp.asarray(_tile_tables()), chip, 0, keepdims=False)
    qkv, gates, (w_in_f, w_ao_f, w_co_f, w_o_f) = _project_gather(h, mine, tab)
    buckets = _band_buckets()
    bias = _bias_tables(rel_bias, buckets)
    og, lg = [], []
    for g in range(3):
        o_g, l_g = _attn_forward(g, qkv, bias[g], bsz, seq)
        og.append(o_g)
        lg.append(l_g)
    (a_in, s_in, merged, dy, a_out, s_out, y_conv, o, lj, dxr, vec_f, dgate) = _mix_forward(
        gates, og, lg, x2, tgt, gate, w_ao_f, w_co_f, w_o_f, conv_w_f, conv_b, ln_g, ln_b, bsz, seq)

    dgates, do, dl, da_out, ds_out, vec = _mix_backward(gates, dy, a_out, s_out, y_conv, o, lj, w_ao_f, w_co_f, w_o_f, conv_w_f, vec_f, bsz, seq)
    core = jnp.reshape(mc, (1,)).astype(jnp.int32)
    small_grads = _out_weight_grads(a_in, da_out, s_in, ds_out, merged, dy, core)
    got_small = _swap_halves([theirs for _, theirs in small_grads], "swap_small_grad_halves")
    sums_small = _chip_sums([own for own, _ in small_grads], got_small, 1, "chip_sums_small")
    dqkv, dbs = None, []
    for g in range(3):
        dqkv, db = _attn_backward(g, qkv, do, dl, bias[g], dqkv, bsz, seq)
        dbs.append(db)
    dq, dk, dv = dqkv
    drb = _bias_grad(jnp.stack(dbs), buckets)
    drb = drb[:, :, 0:4].transpose(1, 0, 2).reshape(N_BUCKETS, 12)
    g_in_mine, got_in, landed_small = _in_weight_grad(ht, dq, dk, dv, dgates, core, [bf for _, bf in sums_small])
    sums_in = _chip_sums([g_in_mine], [got_in], 0, "chip_sums_in")
    grad_x, dshift, dscale, landed_in = _input_grad(dq, dk, dv, dgates, w_in_f, x2, dxr, sc1p, seq, [bf for _, bf in sums_in])
    halves = _reduce_mine([own for own, _ in sums_in + sums_small], list(landed_in) + list(landed_small))
    gw_in, gw_ao, gw_co, gw_o = _join_halves(halves)

    dmod = jnp.concatenate([dshift, dscale, dgate], axis=2).reshape(bsz * 3, D)
    drb_row = jnp.pad(drb.reshape(1, N_BUCKETS * 12), ((0, 0), (0, D - N_BUCKETS * 12)))
    vec = lax.dynamic_update_slice(vec, drb_row, (7, 0))
    packed = jnp.concatenate([vec, _pad_rows(dmod)], axis=0)
    gathered = _all_gather8(packed, "gather_small")
    small = _sum_partials(gathered)
    g_ln_g, g_ln_b, loss_lanes = small[0:1], small[1:2], small[2:3]
    g_conv_w_full, g_conv_b = small[3:6], small[6:7]
    g_rel_bias = small[7, 0:N_BUCKETS * 12].reshape(N_BUCKETS, 12)
    loss = 0.5 / D * jnp.sum(loss_lanes)
    dmod_all = gathered[:, 8:8 + 3 * bsz, :].reshape(8 * bsz, 3 * D)
    dmod_cols = lax.dynamic_slice(dmod_all, (0, chip * n_ada), (8 * bsz, n_ada))
    gw_ada, gb_ada = _ada_backward(c_all, dmod_cols, dmod_all)
    g_conv_w = lax.dynamic_slice(g_conv_w_full, (0, chip * n_cw), (3, n_cw))

    names = ["w_ada", "b_ada", "w_in", "conv_w", "conv_b", "rel_bias", "w_attn_out", "w_conv_out", "w_o", "ln_g", "ln_b"]
    two_d = lambda a: a.reshape(a.shape[-2:]) if a.ndim == 3 else a
    weights = dict(zip(names, map(two_d, (w_ada, b_ada, w_in, conv_w, conv_b, rel_bias, w_attn_out, w_conv_out, w_o, ln_g, ln_b))))
    ms = dict(zip(names, map(two_d, (m_w_ada, m_b_ada, m_w_in, m_conv_w, m_conv_b, m_rel_bias, m_w_attn_out, m_w_conv_out, m_w_o, m_ln_g, m_ln_b))))
    vs = dict(zip(names, map(two_d, (v_w_ada, v_b_ada, v_w_in, v_conv_w, v_conv_b, v_rel_bias, v_w_attn_out, v_w_conv_out, v_w_o, v_ln_g, v_ln_b))))
    grads = dict(zip(names, (gw_ada, gb_ada, gw_in, g_conv_w, g_conv_b, g_rel_bias, gw_ao, gw_co, gw_o, g_ln_g, g_ln_b)))
    shapes = dict(zip(names, (w_ada, b_ada, w_in, conv_w, conv_b, rel_bias, w_attn_out, w_conv_out, w_o, ln_g, ln_b)))
    grad_out, deltas, new_m, new_v = {}, {}, {}, {}
    for n in names:
        grad_out[n], deltas[n], new_m[n], new_v[n] = _adamw(weights[n], grads[n], ms[n], vs[n], f"adamw_{n}")
    shaped = lambda d: [d[n].reshape(shapes[n].shape) for n in names]
    return (loss, grad_x.reshape(bsz, seq, D), *shaped(grad_out), *shaped(deltas), *shaped(new_m), *shaped(new_v))
```

```python
import math

import numpy as np
import jax
import jax.numpy as jnp
from jax import lax
from jax.experimental import pallas as pl
from jax.experimental.pallas import tpu as pltpu

F32 = jnp.float32
BF16 = jnp.bfloat16
SDS = jax.ShapeDtypeStruct
MESH = pl.DeviceIdType.MESH
HBM_OUT = pltpu.HBM
ANY = pl.BlockSpec(memory_space=pl.ANY)
VMEM_SPEC = pl.BlockSpec(memory_space=pltpu.VMEM)

D = 1024
HD = 128
BLK = 128
QW = 1536
AW = 512
NGATE = 6656
GATE_COLS = ((0, 512), (512, 1536), (1536, 2560), (2560, 3584), (3584, 4608), (4608, 5632), (5632, 6656))
NCOL = 3 * QW + NGATE
TN = 512
NQT = QW // TN
NPT = NCOL // TN
DILATIONS = (1, 4, 16)
N_BUCKETS, MAX_EXACT, MAX_DISTANCE = 32, 16, 2048
ALPHA = 2.0 ** 0.25
LN_EPS = 1e-5
NEG = -1e30
SCALE = HD ** -0.5
LR, B1, B2, EPS, WD, STEP = 0.001, 0.9, 0.999, 1e-08, 0.01, 10
NCHIP = 4
VMEM_CAP = 60 * 2 ** 20


def _cp(sem=None, vmem=None, side=False):
    return pltpu.CompilerParams(dimension_semantics=sem, vmem_limit_bytes=vmem, has_side_effects=side)


def _dot(a, b):
    return jnp.dot(a, b, preferred_element_type=F32)


def _dot_nt(a, b):
    return lax.dot_general(a, b, (((1,), (1,)), ((), ())), preferred_element_type=F32)


def _dot_tn(a, b):
    return lax.dot_general(a, b, (((0,), (0,)), ((), ())), preferred_element_type=F32)


def _sig(x):
    return 1.0 / (1.0 + jnp.exp(-x))


def _in_hbm(a):
    return pltpu.with_memory_space_constraint(a, pltpu.HBM)


def _place():
    x, y, c = lax.axis_index("x"), lax.axis_index("y"), lax.axis_index("c")
    return x, y, c


def _all_gather8(v, name):
    r, cdim = v.shape

    def body(v_ref, out_ref, send_sems, recv_sems, local_sem):
        x, y, c = _place()
        me = 4 * x + 2 * y + c
        peers = [(x, y, 1 - c), (1 - x, y, c), (x, 1 - y, c), (1 - x, 1 - y, c),
                 (1 - x, y, 1 - c), (x, 1 - y, 1 - c), (1 - x, 1 - y, 1 - c)]
        mine = pltpu.make_async_copy(v_ref, out_ref.at[me], local_sem)
        mine.start()

        def copy(k, block, to):
            return pltpu.make_async_remote_copy(src_ref=v_ref, dst_ref=out_ref.at[block], send_sem=send_sems.at[k],
                                                recv_sem=recv_sems.at[k], device_id=to, device_id_type=MESH)

        sends = [copy(k, me, p) for k, p in enumerate(peers)]
        for cp in sends:
            cp.start()
        for k, (px, py, pc) in enumerate(peers):
            copy(k, 4 * px + 2 * py + pc, (px, py, pc)).wait_recv()
        for cp in sends:
            cp.wait_send()
        mine.wait()

    return pl.pallas_call(
        body, name=name, out_shape=SDS((8, r, cdim), v.dtype), in_specs=[VMEM_SPEC], out_specs=VMEM_SPEC,
        scratch_shapes=[pltpu.SemaphoreType.DMA((7,)), pltpu.SemaphoreType.DMA((7,)), pltpu.SemaphoreType.DMA(())],
        compiler_params=_cp(side=True),
    )(v)


W_CUTS = (("col", D, NCOL // NCHIP), ("col", AW, D // NCHIP), ("row", D // NCHIP, D), ("row", D // NCHIP, D))
W_FULL = ((D, NCOL), (AW, D), (D, D), (D, D))


def _shard_window(ref, cut, k, half):
    kind, nr, nc = cut
    hr = nr // 2
    if kind == "col":
        rows = pl.ds(0, nr) if half is None else pl.ds(pl.multiple_of(half * hr, 16), hr)
        return ref.at[rows, pl.ds(pl.multiple_of(k * nc, 128), nc)]
    if half is None:
        return ref.at[pl.ds(pl.multiple_of(k * nr, 16), nr), :]
    return ref.at[pl.ds(pl.multiple_of(k * nr + half * hr, 16), hr), :]


def _to_bf16_windows(ws):
    x, y, _ = _place()
    chip = jnp.reshape(2 * x + y, (1,)).astype(jnp.int32)
    tr = 256
    n = len(ws)

    def body(c_ref, *refs):
        src, dst = refs[:n], refs[n:]
        dst[0][...] = src[0][...].astype(BF16)

        @pl.when(pl.program_id(0) == 0)
        def _():
            for a in range(1, n):
                dst[a][...] = src[a][...].astype(BF16)

    in_specs = [pl.BlockSpec((tr, W_CUTS[0][2]), lambda i, cr: (i, 0))]
    out_specs = [pl.BlockSpec((tr, W_CUTS[0][2]), lambda i, cr: (i, cr[0]))]
    for a in range(1, n):
        kind, nr, nc = W_CUTS[a]
        in_specs.append(pl.BlockSpec((nr, nc), lambda i, cr: (0, 0)))
        out_specs.append(pl.BlockSpec((nr, nc), (lambda i, cr: (0, cr[0])) if kind == "col" else (lambda i, cr: (cr[0], 0))))
    return pl.pallas_call(
        body, name="to_bf16", out_shape=[SDS(W_FULL[a], BF16) for a in range(n)],
        grid_spec=pltpu.PrefetchScalarGridSpec(num_scalar_prefetch=1, grid=(D // tr,), in_specs=in_specs, out_specs=out_specs),
        compiler_params=_cp(("arbitrary",)),
    )(chip, *ws)


def _swap_halves(theirs, name):
    n = len(theirs)

    def body(*refs):
        src, land = refs[:n], refs[n:2 * n]
        send_sems, recv_sems = refs[2 * n:]
        x, y, c = _place()
        copies = [pltpu.make_async_remote_copy(src_ref=src[a], dst_ref=land[a], send_sem=send_sems.at[a], recv_sem=recv_sems.at[a],
                                               device_id=(x, y, 1 - c), device_id_type=MESH) for a in range(n)]
        for cp in copies:
            cp.start()
        for cp in copies:
            cp.wait()

    return pl.pallas_call(
        body, name=name, out_shape=[SDS(v.shape, v.dtype) for v in theirs], in_specs=[ANY] * n, out_specs=[ANY] * n,
        scratch_shapes=[pltpu.SemaphoreType.DMA((n,)), pltpu.SemaphoreType.DMA((n,))],
        compiler_params=_cp(side=True),
    )(*theirs)


def _chip_sums(mines, gots, first, name):
    n = len(mines)
    x, y, _ = _place()
    me = jnp.reshape(2 * x + y, (1,)).astype(jnp.int32)

    def body(me_ref, *refs):
        ins, outs = refs[:2 * n], refs[2 * n:]
        for a in range(n):
            hr, nc = W_CUTS[first + a][1] // 2, W_CUTS[first + a][2]
            s = (ins[2 * a][...] + ins[2 * a + 1][...].astype(F32)).reshape(hr, nc)
            outs[2 * a + 1][0] = s.astype(BF16)

            @pl.when(pl.program_id(0) == me_ref[0])
            def _(a=a, s=s):
                outs[2 * a][...] = s

    in_specs, out_specs, out_shape = [], [], []
    for a in range(n):
        kind, nr, nc = W_CUTS[first + a]
        hr = nr // 2
        spec = pl.BlockSpec((hr, nc), lambda k, mr: (0, k)) if kind == "col" else pl.BlockSpec((1, hr, nc), lambda k, mr: (k, 0, 0))
        in_specs += [spec, spec]
        out_specs += [pl.BlockSpec((hr, nc), lambda k, mr: (0, 0)), pl.BlockSpec((1, hr, nc), lambda k, mr: (k, 0, 0))]
        out_shape += [SDS((hr, nc), F32), SDS((NCHIP, hr, nc), BF16)]
    outs = pl.pallas_call(
        body, name=name, out_shape=out_shape,
        grid_spec=pltpu.PrefetchScalarGridSpec(num_scalar_prefetch=1, grid=(NCHIP,), in_specs=in_specs, out_specs=out_specs),
        compiler_params=_cp(("arbitrary",), VMEM_CAP),
    )(me, *[v for pair in zip(mines, gots) for v in pair])
    return [(outs[2 * a], outs[2 * a + 1]) for a in range(n)]


def _reduce_mine(mines, gots):
    n = len(mines)
    tr = 256
    nsteps = W_CUTS[0][1] // 2 // tr

    def body(*refs):
        ins, outs, lands = refs[:2 * n], refs[2 * n:3 * n], refs[3 * n:4 * n]
        stage0, send_sems, recv_sems = refs[4 * n], refs[4 * n + 1], refs[4 * n + 2]
        i = pl.program_id(0)
        px, py, pc = _place()
        sibling = (px, py, 1 - pc)

        def total(a):
            m_ref, g_ref = ins[2 * a], ins[2 * a + 1]
            return ((m_ref[...] + g_ref[0].astype(F32)) + g_ref[1].astype(F32)) + g_ref[2].astype(F32)

        def push(src, dst, k, a):
            return pltpu.make_async_remote_copy(src_ref=src, dst_ref=dst, send_sem=send_sems.at[k], recv_sem=recv_sems.at[a],
                                                device_id=sibling, device_id_type=MESH)

        t0 = total(0)
        outs[0][...] = t0
        for k in range(nsteps):
            @pl.when(i == k)
            def _(k=k):
                stage0[k] = t0
                push(stage0.at[k], lands[0].at[pl.ds(k * tr, tr), :], k, 0).start()

        @pl.when(i == 0)
        def _():
            for a in range(1, n):
                outs[a][...] = total(a)
                push(outs[a], lands[a], nsteps + a - 1, a).start()

        @pl.when(i == nsteps - 1)
        def _():
            for k in range(nsteps):
                push(stage0.at[k], lands[0].at[pl.ds(k * tr, tr), :], k, 0).wait_send()
            for a in range(1, n):
                push(outs[a], lands[a], nsteps + a - 1, a).wait_send()
            for a in range(n):
                push(lands[a], lands[a], 0, a).wait_recv()

    nc0 = W_CUTS[0][2]
    in_specs = [pl.BlockSpec((tr, nc0), lambda i: (i, 0)), pl.BlockSpec((3, tr, nc0), lambda i: (0, i, 0))]
    out_specs = [pl.BlockSpec((tr, nc0), lambda i: (i, 0))]
    for a in range(1, n):
        hr, nc = W_CUTS[a][1] // 2, W_CUTS[a][2]
        in_specs += [pl.BlockSpec((hr, nc), lambda i: (0, 0)), pl.BlockSpec((3, hr, nc), lambda i: (0, 0, 0))]
        out_specs.append(pl.BlockSpec((hr, nc), lambda i: (0, 0)))
    half = [SDS((W_CUTS[a][1] // 2, W_CUTS[a][2]), F32) for a in range(n)]
    outs = pl.pallas_call(
        body, name="reduce_mine", out_shape=half + half, grid=(nsteps,), in_specs=in_specs, out_specs=out_specs + [ANY] * n,
        scratch_shapes=[pltpu.VMEM((nsteps, tr, nc0), F32), pltpu.SemaphoreType.DMA((nsteps + n - 1,)), pltpu.SemaphoreType.DMA((n,))],
        compiler_params=_cp(("arbitrary",), VMEM_CAP, side=True),
    )(*[v for pair in zip(mines, gots) for v in pair])
    return [(outs[a], outs[n + a]) for a in range(n)]


def _ada_forward(c_all, w_ada, b_cols):
    nb, nc = c_all.shape[0], w_ada.shape[1]

    def body(c_ref, w_ref, b_ref, o_ref):
        cv = c_ref[...]
        sc = (cv * _sig(cv)).astype(BF16)
        o_ref[...] = _dot(sc, w_ref[...].astype(BF16)) + b_ref[...]

    return pl.pallas_call(body, name="ada_forward", out_shape=SDS((nb, nc), F32), compiler_params=_cp(vmem=VMEM_CAP // 2))(c_all, w_ada, b_cols)


def _ada_backward(c_all, dmod_cols, dmod_all):
    nb, nc = dmod_cols.shape

    def body(c_ref, d_ref, a_ref, gw_ref, gb_ref):
        cv = c_ref[...]
        sc = (cv * _sig(cv)).astype(BF16)
        gw_ref[...] = _dot_tn(sc, d_ref[...].astype(BF16))
        gb_ref[...] = jnp.sum(a_ref[...], axis=0, keepdims=True)

    return pl.pallas_call(body, name="ada_backward", out_shape=[SDS((D, nc), F32), SDS((1, dmod_all.shape[1]), F32)],
                          compiler_params=_cp(vmem=VMEM_CAP // 2))(c_all, dmod_cols, dmod_all)


def _modulate(x2, sc1p, shift, seq, tm=512):
    t = x2.shape[0]
    spt = seq // tm

    def body(x_ref, sc_ref, sh_ref, h_ref, ht_ref):
        h = x_ref[...] * sc_ref[0] + sh_ref[0]
        h_ref[...] = h.astype(BF16)
        ht_ref[...] = h.T.astype(BF16)

    per_seq = pl.BlockSpec((1, 1, D), lambda i: (i // spt, 0, 0))
    return pl.pallas_call(
        body, name="modulate", out_shape=[HBM_OUT((t, D), BF16), HBM_OUT((D, t), BF16)], grid=(t // tm,),
        in_specs=[pl.BlockSpec((tm, D), lambda i: (i, 0)), per_seq, per_seq],
        out_specs=[pl.BlockSpec((tm, D), lambda i: (i, 0)), pl.BlockSpec((D, tm), lambda i: (0, i))],
        compiler_params=_cp(("parallel",)),
    )(_in_hbm(x2), sc1p, shift)


TW = 256
TPS = NCOL // NCHIP // TW
NT = NCOL // TW
NQKV_T = 3 * QW // TW
N_TILE_SEMS = 2 * 3 * TPS


def _tile_tables():
    tabs = np.zeros((NCHIP, 3, NT), np.int32)
    for me in range(NCHIP):
        tiles = [TPS * (me ^ (s // TPS)) + s % TPS for s in range(NT)]
        tabs[me, 0] = tiles
        for row, (lo, hi) in enumerate(((0, NQKV_T), (NQKV_T, NT))):
            mine = [w - lo if lo <= w < hi else None for w in tiles]
            held = next(m for m in mine if m is not None)
            for s, m in enumerate(mine):
                held = held if m is None else m
                tabs[me, 1 + row, s] = held
    return tabs


def _project_gather(h, fulls, tab):
    t = h.shape[0]
    n = len(fulls)

    def body(tab_ref, h_ref, *rest):
        qkv_ref, g_ref = rest[n], rest[n + 1]
        full = rest[n + 2:2 * n + 2]
        w_buf, tile_sems, send_sems, recv_sems = rest[2 * n + 2:]
        s = pl.program_id(0)
        x, y, c = _place()
        me = 2 * x + y
        peers = [(x, 1 - y), (1 - x, y), (1 - x, 1 - y)]
        sibling = (x, y, 1 - c)

        def hop(a, r, stage, chip, half, to):
            window = _shard_window(full[a], W_CUTS[a], chip, half)
            k = N_TILE_SEMS + 6 * (a - 1) + 2 * r + stage
            return pltpu.make_async_remote_copy(src_ref=window, dst_ref=window, send_sem=send_sems.at[k], recv_sem=recv_sems.at[k],
                                                device_id=to, device_id_type=MESH)

        def tile_hop(q, stage, col_step, half, to):
            col = pl.multiple_of(tab_ref[0, col_step] * TW, TW)
            window = full[0].at[pl.ds(pl.multiple_of(half * (D // 2), 16), D // 2), pl.ds(col, TW)]
            k = 2 * (q - TPS) + stage
            return pltpu.make_async_remote_copy(src_ref=window, dst_ref=window, send_sem=send_sems.at[k], recv_sem=recv_sems.at[k],
                                                device_id=to, device_id_type=MESH)

        def send_tile(r, j):
            return tile_hop(TPS * (r + 1) + j, 0, j, c, (*peers[r], c))

        def pass_on(q, to):
            return tile_hop(3 * TPS + q % TPS, 0, q, c, to)

        def arrive(a, r):
            px, py = peers[r]
            chip = 2 * px + py
            hop(a, r, 0, chip, c, (px, py, c)).wait_recv()
            hop(a, r, 1, chip, c, sibling).start()
            hop(a, r, 1, chip, 1 - c, sibling).wait_recv()

        def tile(step, slot):
            col = pl.multiple_of(tab_ref[0, step] * TW, TW)
            return pltpu.make_async_copy(full[0].at[:, pl.ds(col, TW)], w_buf.at[slot], tile_sems.at[slot])

        @pl.when(s == 0)
        def _():
            for r in range(2):
                for j in range(TPS):
                    send_tile(r, j).start()
            tile(0, 0).start()

        @pl.when((s + 1 >= TPS) & (s + 1 < NT))
        def _():
            tile_hop(s + 1, 1, s + 1, 1 - c, sibling).wait_recv()

        @pl.when(s + 1 < NT)
        def _():
            tile(s + 1, 1 - (s % 2)).start()

        @pl.when((s + 2 >= TPS) & (s + 2 < NT))
        def _():
            tile_hop(s + 2, 0, s + 2, c, sibling).wait_recv()
            tile_hop(s + 2, 1, s + 2, c, sibling).start()

        for r in range(2):
            @pl.when(((s + 2) // TPS == r + 1) & ((s + 2) % 2 == (r + 1 + TPS * (r + 1)) % 2))
            def _(r=r):
                pass_on(s + 2, (*peers[1 - r], c)).start()

        @pl.when(s + 2 == 2 * TPS - 1)
        def _():
            for a in range(1, n):
                for r in range(3):
                    hop(a, r, 0, me, c, (*peers[r], c)).start()

        slot = s % 2
        tile(s, slot).wait()
        is_qkv = tab_ref[0, s] < NQKV_T
        for k in range(2):
            @pl.when(slot == k)
            def _(k=k):
                acc = _dot(h_ref[...], w_buf[k])

                @pl.when(is_qkv)
                def _():
                    qkv_ref[...] = acc.astype(BF16)

                @pl.when(jnp.logical_not(is_qkv))
                def _():
                    g_ref[...] = acc.astype(BF16)

        @pl.when(s == NT - 1)
        def _():
            for a in range(1, n):
                for r in range(3):
                    arrive(a, r)
            for r in range(3):
                for j in range(TPS):
                    send_tile(r, j).wait_send()
                    tile_hop(TPS * (r + 1) + j, 1, TPS * (r + 1) + j, c, sibling).wait_send()
                for a in range(1, n):
                    hop(a, r, 0, me, c, (*peers[r], c)).wait_send()
                    px, py = peers[r]
                    hop(a, r, 1, 2 * px + py, c, sibling).wait_send()

    n_sems = N_TILE_SEMS + 6 * (n - 1)
    outs = pl.pallas_call(
        body, name="project_gather", out_shape=[HBM_OUT((t, 3 * QW), BF16), HBM_OUT((t, NGATE), BF16)] + [SDS(s, BF16) for s in W_FULL],
        grid_spec=pltpu.PrefetchScalarGridSpec(
            num_scalar_prefetch=1, grid=(NT,),
            in_specs=[pl.BlockSpec((t, D), lambda s, tab: (0, 0))] + [ANY] * n,
            out_specs=[pl.BlockSpec((t, TW), lambda s, tab: (0, tab[1, s])), pl.BlockSpec((t, TW), lambda s, tab: (0, tab[2, s]))] + [ANY] * n,
            scratch_shapes=[pltpu.VMEM((2, D, TW), BF16), pltpu.SemaphoreType.DMA((2,)),
                            pltpu.SemaphoreType.DMA((n_sems,)), pltpu.SemaphoreType.DMA((n_sems,))]),
        input_output_aliases={2 + a: 2 + a for a in range(n)},
        compiler_params=_cp(("arbitrary",), VMEM_CAP, side=True),
    )(tab, _in_hbm(h), *fulls)
    return outs[0], outs[1], outs[2:]


def _bias_tables(rel_bias, buckets):
    def body(tab_ref, bk_ref, o_ref):
        a = lax.broadcasted_iota(jnp.int32, (BLK, 2 * BLK), 0)
        b = lax.broadcasted_iota(jnp.int32, (BLK, 2 * BLK), 1)
        steps = a + BLK - b
        valid = (steps >= 0) & (steps <= BLK)
        for g in range(3):
            bk = bk_ref[g]
            for j in range(4):
                def pick(kk, acc, bk=bk, col=4 * g + j):
                    return jnp.where(bk == kk, tab_ref[kk, col], acc)

                acc = lax.fori_loop(0, N_BUCKETS, pick, jnp.zeros((BLK, 2 * BLK), F32))
                o_ref[g, j] = jnp.where(valid, acc, NEG)

    return pl.pallas_call(
        body, name="bias_tables", out_shape=SDS((3, 4, BLK, 2 * BLK), F32),
        in_specs=[pl.BlockSpec(memory_space=pltpu.SMEM), VMEM_SPEC], out_specs=VMEM_SPEC,
    )(rel_bias, buckets)


def _bias_grad(ds_sum, buckets):
    def body(ds_ref, bk_ref, o_ref, part_ref):
        lane = lax.broadcasted_iota(jnp.int32, (N_BUCKETS, 128), 1)
        for g in range(3):
            def bucket(kk, carry, g=g):
                mine = bk_ref[g] == kk
                for j in range(4):
                    v = jnp.sum(jnp.where(mine, ds_ref[g, j], 0.0).reshape(BLK // 8, 8, 2 * BLK), axis=0)
                    part_ref[j, pl.ds(pl.multiple_of(kk * 8, 8), 8), :] = v[:, :BLK] + v[:, BLK:]
                return carry

            lax.fori_loop(0, N_BUCKETS, bucket, 0)
            out = jnp.zeros((N_BUCKETS, 128), F32)
            for j in range(4):
                rows = jnp.sum(part_ref[j], axis=1, keepdims=True)
                out = jnp.where(lane == j, jnp.sum(rows.reshape(N_BUCKETS, 8, 1), axis=1), out)
            o_ref[g] = out

    return pl.pallas_call(body, name="bias_grad", out_shape=SDS((3, N_BUCKETS, 128), F32), in_specs=[VMEM_SPEC, VMEM_SPEC],
                          out_specs=VMEM_SPEC, scratch_shapes=[pltpu.VMEM((4, N_BUCKETS * 8, 128), F32)])(ds_sum, buckets)


def _sub_rows(d, r, first, size):
    return pl.ds(first * d + r, size) if d == 1 else pl.ds(first * d + r, size, stride=d)


def _head_spec(seq, g, part):
    return pl.BlockSpec((seq, HD), lambda b, hh: (b, part * (QW // HD) + 4 * g + hh))


def _rows(start, count, stride):
    return pl.ds(start, count) if stride == 1 else pl.ds(start, count, stride=stride)


def _gather_rows(dst, dst0, src, src0, stride, count):
    for first in range(0, count, BLK):
        dst[pl.ds(dst0 + first, BLK), :] = src[_rows(src0 + first * stride, BLK, stride), :].astype(dst.dtype)


def _scatter_rows(dst, dst0, stride, src, src0, count):
    for first in range(0, count, BLK):
        dst[_rows(dst0 + first * stride, BLK, stride), :] = src[pl.ds(src0 + first, BLK), :].astype(dst.dtype)


def _by_subsequence(dst, src, d, wide=None, tmp=None):
    seq = src.shape[0]
    ln = seq // d
    if wide is not None:
        wide[...] = src[...].astype(F32)
        src = wide
    if d <= 4:
        for r in range(d):
            _gather_rows(dst, r * ln, src, r, d, ln)
    else:
        quarter = seq // 4
        for r4 in range(4):
            _gather_rows(tmp, r4 * quarter, src, r4, 4, quarter)
        for r4 in range(4):
            for a in range(d // 4):
                _gather_rows(dst, (4 * a + r4) * ln, tmp, r4 * quarter + a, d // 4, ln)


def _to_sequence(dst, src, d, tmp=None):
    seq = dst.shape[0]
    ln = seq // d
    if d <= 4:
        for r in range(d):
            _scatter_rows(dst, r, d, src, r * ln, ln)
    else:
        quarter = seq // 4
        for r4 in range(4):
            for a in range(d // 4):
                _scatter_rows(tmp, r4 * quarter + a, d // 4, src, (4 * a + r4) * ln, ln)
        for r4 in range(4):
            _scatter_rows(dst, r4, 4, tmp, r4 * quarter, quarter)


def _attn_forward(g, qkv, bias, bsz, seq):
    d = DILATIONS[g]
    ln = seq // d
    units = [(r, n) for r in range(d) for n in range(ln // BLK)]

    def band(n):
        return slice(BLK, 2 * BLK) if n == 0 else slice(0, 2 * BLK)

    def body(q_ref, k_ref, v_ref, b_ref, o_ref, l_ref, *scratch):
        hs = pl.program_id(1)
        s_scr, p_scr = scratch[:2]
        if d == 1:
            qd, kd, vd = q_ref, k_ref, v_ref
        else:
            wide, tmp, qd, kd, vd = scratch[2:7]
            for dst, src in ((qd, q_ref), (kd, k_ref), (vd, v_ref)):
                _by_subsequence(dst, src, d, wide, tmp)
        blk = lambda r, n: pl.ds(r * ln + n * BLK, BLK)
        direct = d <= 4
        out_rows = (lambda r, n: _sub_rows(d, r, n * BLK, BLK)) if direct else blk
        o_dst, l_dst = (o_ref, l_ref) if direct else scratch[7:9]
        for u, (r, n) in enumerate(units):
            s_scr[u, :, BLK:] = _dot_nt(qd[blk(r, n), :], kd[blk(r, n), :])
            if n > 0:
                s_scr[u, :, :BLK] = _dot_nt(qd[blk(r, n), :], kd[blk(r, n - 1), :])
        for u, (r, n) in enumerate(units):
            s = s_scr[u, :, band(n)] * SCALE + b_ref[hs, :, band(n)]
            m = jnp.max(s, axis=1, keepdims=True)
            e = jnp.exp(s - m)
            den = jnp.sum(e, axis=1, keepdims=True)
            p_scr[u, :, band(n)] = (e * (1.0 / den)).astype(BF16)
            l_dst[out_rows(r, n), :] = jnp.broadcast_to(m + jnp.log(den), (BLK, HD))
        for u, (r, n) in enumerate(units):
            acc = _dot(p_scr[u, :, BLK:], vd[blk(r, n), :])
            if n > 0:
                acc = acc + _dot(p_scr[u, :, :BLK], vd[blk(r, n - 1), :])
            o_dst[out_rows(r, n), :] = acc
        if not direct:
            _to_sequence(o_ref, o_dst, d, tmp)
            _to_sequence(l_ref, l_dst, d, tmp)

    rows_f32, rows_bf16 = pltpu.VMEM((seq, HD), F32), pltpu.VMEM((seq, HD), BF16)
    regrouped = [] if d == 1 else [rows_f32] * 2 + [rows_bf16] * 3 + ([] if d <= 4 else [rows_f32] * 2)
    out_spec = pl.BlockSpec((seq, HD), lambda b, hh: (b, hh))
    return pl.pallas_call(
        body, name=f"attn_forward_{g}", out_shape=[HBM_OUT((bsz * seq, AW), F32)] * 2, grid=(bsz, 4),
        in_specs=[_head_spec(seq, g, part) for part in range(3)] + [pl.BlockSpec((4, BLK, 2 * BLK), lambda b, hh: (0, 0, 0))],
        out_specs=[out_spec, out_spec],
        scratch_shapes=[pltpu.VMEM((len(units), BLK, 2 * BLK), F32), pltpu.VMEM((len(units), BLK, 2 * BLK), BF16)] + regrouped,
        compiler_params=_cp(("parallel", "parallel"), VMEM_CAP // 2),
    )(qkv, qkv, qkv, _in_hbm(bias))


def _attn_backward(g, qkv, do, dl, bias, prev_out, bsz, seq):
    d = DILATIONS[g]
    ln = seq // d
    units = [(r, n) for r in range(d) for n in range(ln // BLK)]

    def body(q_ref, k_ref, v_ref, do_ref, dl_ref, b_ref, *rest):
        dq_ref, dk_ref, dv_ref, db_ref = rest[-18:-14]
        wide, tmp, qd, kd, vd, dod, dld, dqd, dkd, dvd, s_scr, dp_scr, p_scr, ds_scr = rest[-14:]
        hs = pl.program_id(1)

        @pl.when((pl.program_id(0) == 0) & (hs == 0))
        def _():
            db_ref[...] = jnp.zeros_like(db_ref)

        for dst, src in ((qd, q_ref), (kd, k_ref), (vd, v_ref)):
            _by_subsequence(dst, src, d, wide, tmp)
        _by_subsequence(dod, do_ref, d, None, tmp)
        _by_subsequence(dld, dl_ref, d, None, tmp)
        dkd[...] = jnp.zeros_like(dkd)
        dvd[...] = jnp.zeros_like(dvd)
        blk = lambda r, n: pl.ds(r * ln + n * BLK, BLK)
        keys = lambda r, n: [(blk(r, n), slice(BLK, 2 * BLK))] + ([(blk(r, n - 1), slice(0, BLK))] if n > 0 else [])
        for u, (r, n) in enumerate(units):
            for rows, band in keys(r, n):
                s_scr[u, :, band] = _dot_nt(qd[blk(r, n), :], kd[rows, :])
                dp_scr[u, :, band] = _dot_nt(dod[blk(r, n), :], vd[rows, :])
        for u, (r, n) in enumerate(units):
            both = dld[blk(r, n), :]
            lse, delta = both[:, 0:1], both[:, 64:65]
            band = slice(BLK, 2 * BLK) if n == 0 else slice(0, 2 * BLK)
            p = jnp.exp(s_scr[u, :, band] * SCALE + b_ref[hs, :, band] - lse)
            ds = p * (dp_scr[u, :, band] - delta)
            p_scr[u, :, band] = p.astype(BF16)
            ds_scr[u, :, band] = ds.astype(BF16)
            db_ref[hs, :, band] += ds
        for u, (r, n) in enumerate(units):
            dq = jnp.zeros((BLK, HD), F32)
            for rows, band in keys(r, n):
                dvd[rows, :] += _dot_tn(p_scr[u, :, band], dod[blk(r, n), :])
                dkd[rows, :] += _dot_tn(ds_scr[u, :, band], qd[blk(r, n), :]) * SCALE
                dq = dq + _dot(ds_scr[u, :, band], kd[rows, :])
            dqd[blk(r, n), :] = dq * SCALE
        for out, acc in ((dq_ref, dqd), (dk_ref, dkd), (dv_ref, dvd)):
            if d == 1:
                out[...] = acc[...].astype(BF16)
            else:
                _to_sequence(wide, acc, d, tmp)
                out[...] = wide[...].astype(BF16)

    qkv_spec = _head_spec(seq, g, 0)
    out_spec = pl.BlockSpec((seq, HD), lambda b, hh: (b, hh))
    band_spec = pl.BlockSpec((4, BLK, 2 * BLK), lambda b, hh: (0, 0, 0))
    ins = [qkv, qkv, qkv, _in_hbm(do), _in_hbm(dl), _in_hbm(bias)]
    in_specs = [_head_spec(seq, g, part) for part in range(3)] + [out_spec, out_spec, band_spec]
    aliases = {}
    if prev_out is not None:
        ins += list(prev_out)
        in_specs += [ANY] * 3
        aliases = {6: 0, 7: 1, 8: 2}
    rows_bf16, rows_f32 = pltpu.VMEM((seq, HD), BF16), pltpu.VMEM((seq, HD), F32)
    staged = [pltpu.VMEM((len(units), BLK, 2 * BLK), F32)] * 2 + [pltpu.VMEM((len(units), BLK, 2 * BLK), BF16)] * 2
    dq, dk, dv, db = pl.pallas_call(
        body, name=f"attn_backward_{g}", out_shape=[HBM_OUT((bsz * seq, QW), BF16)] * 3 + [SDS((4, BLK, 2 * BLK), F32)], grid=(bsz, 4),
        in_specs=in_specs, out_specs=[qkv_spec] * 3 + [band_spec], input_output_aliases=aliases,
        scratch_shapes=[rows_f32] * 2 + [rows_bf16] * 4 + [rows_f32] * 4 + staged,
        compiler_params=_cp(("arbitrary", "arbitrary"), VMEM_CAP // 2),
    )(*ins)
    return (dq, dk, dv), db


def _mix_forward(gates, og, lg, x2, tgt, gate, w_ao, w_co, w_o, conv_w, conv_b, ln_g, ln_b, bsz, seq, tm=256):
    t = x2.shape[0]
    spt = seq // tm

    def body(g_ref, o1, o2, o3, l1, l2, l3, x_ref, t_ref, gate_ref, wao_ref, wco_ref, wo_ref, cw_ref, cb_ref, lng_ref, lnb_ref,
             ain_ref, sin_ref, mrg_ref, dy_ref, aout_ref, sout_ref, yc_ref, o_ref, lj_ref, dxr_ref, vec_ref, dgate_ref, zc_ref):
        b, i = pl.program_id(0), pl.program_id(1)

        @pl.when((b == 0) & (i == 0))
        def _():
            vec_ref[...] = jnp.zeros_like(vec_ref)

        @pl.when(i == 0)
        def _():
            zc_ref[...] = jnp.zeros_like(zc_ref)
            dgate_ref[...] = jnp.zeros_like(dgate_ref)

        g_attn, u, bg, cg, g_conv, m_attn, m_conv = (g_ref[:, lo:hi].astype(F32) for lo, hi in GATE_COLS)
        la, lb, lc = l1[...], l2[...], l3[...]
        mx = jnp.maximum(la, jnp.maximum(lb, lc))
        ea, eb, ec = jnp.exp(la - mx), jnp.exp(lb - mx), jnp.exp(lc - mx)
        den = ea + eb + ec
        o = (ea * o1[...] + eb * o2[...] + ec * o3[...]) / den
        o_ref[...] = o
        lj_ref[...] = mx + jnp.log(den)
        a_in = o * (g_attn * _sig(g_attn))
        ain_ref[...] = a_in.astype(BF16)
        a_out = _dot(a_in.astype(BF16), wao_ref[...])
        aout_ref[...] = a_out.astype(BF16)
        z = cg * u
        rows = lax.broadcasted_iota(jnp.int32, (tm, D), 0)
        c6, c7 = zc_ref[6:7, :], zc_ref[7:8, :]
        z1 = jnp.where(rows == 0, c7, pltpu.roll(z, 1, 0))
        z2 = jnp.where(rows == 0, c6, jnp.where(rows == 1, c7, pltpu.roll(z, 2, 0)))
        zc_ref[...] = z[tm - 8:tm, :]
        y_conv = (cw_ref[0:1, :] * z2 + cw_ref[1:2, :] * z1 + cw_ref[2:3, :] * z) + cb_ref[...]
        yc_ref[...] = y_conv.astype(BF16)
        s_in = bg * y_conv * (g_conv * _sig(g_conv))
        sin_ref[...] = s_in.astype(BF16)
        s_out = _dot(s_in.astype(BF16), wco_ref[...])
        sout_ref[...] = s_out.astype(BF16)
        merged = _sig(m_attn) * a_out + _sig(m_conv) * s_out
        mrg_ref[...] = merged.astype(BF16)
        y = _dot(merged.astype(BF16), wo_ref[...])
        gate1 = 1.0 + gate_ref[0]
        r = ALPHA * x_ref[...] + gate1 * y
        mu = jnp.mean(r, axis=1, keepdims=True)
        rc = r - mu
        rstd = lax.rsqrt(jnp.mean(rc * rc, axis=1, keepdims=True) + LN_EPS)
        xhat = rc * rstd
        diff = (xhat * lng_ref[...] + lnb_ref[...]) - t_ref[...]
        dout = diff * (1.0 / D)
        vec_ref[0:1, :] += jnp.sum(dout * xhat, axis=0, keepdims=True)
        vec_ref[1:2, :] += jnp.sum(dout, axis=0, keepdims=True)
        vec_ref[2:3, :] += jnp.sum(diff * diff, axis=0, keepdims=True)
        dxh = dout * lng_ref[...]
        dr = rstd * (dxh - jnp.mean(dxh, axis=1, keepdims=True) - xhat * jnp.mean(dxh * xhat, axis=1, keepdims=True))
        dxr_ref[...] = ALPHA * dr
        dy_ref[...] = (dr * gate1).astype(BF16)
        dgate_ref[0] += jnp.sum(dr * y, axis=0, keepdims=True)

    tok = lambda w: pl.BlockSpec((tm, w), lambda b, i: (b * spt + i, 0))
    const = lambda s: pl.BlockSpec(s, lambda b, i: (0,) * len(s))
    per_seq = pl.BlockSpec((1, 1, D), lambda b, i: (b, 0, 0))
    outs = pl.pallas_call(
        body, name="mix_forward", grid=(bsz, spt),
        out_shape=[HBM_OUT((t, AW), BF16), HBM_OUT((t, D), BF16), HBM_OUT((t, D), BF16), HBM_OUT((t, D), BF16), HBM_OUT((t, D), BF16),
                   HBM_OUT((t, D), BF16), HBM_OUT((t, D), BF16), HBM_OUT((t, AW), F32), HBM_OUT((t, AW), F32), HBM_OUT((t, D), F32),
                   SDS((8, D), F32), SDS((bsz, 1, D), F32)],
        in_specs=[tok(NGATE)] + [tok(AW)] * 6 + [tok(D), tok(D), per_seq, const((AW, D)), const((D, D)), const((D, D)),
                                                 const((3, D)), const((1, D)), const((1, D)), const((1, D))],
        out_specs=[tok(AW), tok(D), tok(D), tok(D), tok(D), tok(D), tok(D), tok(AW), tok(AW), tok(D), const((8, D)), per_seq],
        scratch_shapes=[pltpu.VMEM((8, D), F32)],
        compiler_params=_cp(("arbitrary", "arbitrary"), VMEM_CAP),
    )(_in_hbm(gates), *map(_in_hbm, og), *map(_in_hbm, lg), _in_hbm(x2), _in_hbm(tgt), gate, w_ao, w_co, w_o, conv_w, conv_b, ln_g, ln_b)
    return outs


def _mix_backward(gates, dy, a_out, s_out, y_conv, o, lj, w_ao, w_co, w_o, conv_w, vec_f, bsz, seq, tm=256):
    t = dy.shape[0]
    spt = seq // tm

    def body(g_ref, dy_ref, aout_ref, sout_ref, yc_ref, o_ref, lj_ref, wao_ref, wco_ref, wo_ref, cw_ref, vecf_ref,
             dg_ref, do_ref, dl_ref, daout_ref, dsout_ref, vec_ref, car_ref):
        b, i = pl.program_id(0), pl.program_id(1)

        @pl.when((b == 0) & (i == 0))
        def _():
            vec_ref[...] = vecf_ref[...]

        @pl.when(i == 0)
        def _():
            car_ref[...] = jnp.zeros_like(car_ref)

        g_attn, u, bg, cg, g_conv, m_attn, m_conv = (g_ref[:, lo:hi].astype(F32) for lo, hi in GATE_COLS)
        dmerged = _dot_nt(dy_ref[...], wo_ref[...])
        sa, sc = _sig(m_attn), _sig(m_conv)
        da_out = (dmerged * sa).astype(BF16)
        ds_out = (dmerged * sc).astype(BF16)
        daout_ref[...] = da_out
        dsout_ref[...] = ds_out
        dg_ref[:, 4608:5632] = (dmerged * aout_ref[...].astype(F32) * (sa * (1.0 - sa))).astype(BF16)
        dg_ref[:, 5632:6656] = (dmerged * sout_ref[...].astype(F32) * (sc * (1.0 - sc))).astype(BF16)
        da_in = _dot_nt(da_out, wao_ref[...])
        ds_in = _dot_nt(ds_out, wco_ref[...])
        sga = _sig(g_attn)
        o = o_ref[...]
        do = da_in * (g_attn * sga)
        do_ref[...] = do
        dg_ref[:, 0:512] = (da_in * o * (sga * (1.0 + g_attn * (1.0 - sga)))).astype(BF16)
        prod = do * o
        lane = lax.broadcasted_iota(jnp.int32, (tm, HD), 1)
        for j in range(4):
            cs = slice(j * HD, (j + 1) * HD)
            delta = jnp.sum(prod[:, cs], axis=1, keepdims=True)
            dl_ref[:, cs] = jnp.where(lane < 64, lj_ref[:, cs], delta)
        sgc = _sig(g_conv)
        silu_c = g_conv * sgc
        yc = yc_ref[...].astype(F32)
        dg_ref[:, 1536:2560] = (ds_in * yc * silu_c).astype(BF16)
        dg_ref[:, 3584:4608] = (ds_in * bg * yc * (sgc * (1.0 + g_conv * (1.0 - sgc)))).astype(BF16)
        dyc = ds_in * bg * silu_c
        rows = lax.broadcasted_iota(jnp.int32, (tm, D), 0)
        c0, c1 = car_ref[0:1, :], car_ref[1:2, :]
        n1 = jnp.where(rows == tm - 1, c0, pltpu.roll(dyc, tm - 1, 0))
        n2 = jnp.where(rows == tm - 2, c0, jnp.where(rows == tm - 1, c1, pltpu.roll(dyc, tm - 2, 0)))
        car_ref[...] = dyc[0:8, :]
        dz = cw_ref[2:3, :] * dyc + cw_ref[1:2, :] * n1 + cw_ref[0:1, :] * n2
        z = cg * u
        dg_ref[:, 512:1536] = (dz * cg).astype(BF16)
        dg_ref[:, 2560:3584] = (dz * u).astype(BF16)
        vec_ref[3:4, :] += jnp.sum(n2 * z, axis=0, keepdims=True)
        vec_ref[4:5, :] += jnp.sum(n1 * z, axis=0, keepdims=True)
        vec_ref[5:6, :] += jnp.sum(dyc * z, axis=0, keepdims=True)
        vec_ref[6:7, :] += jnp.sum(dyc, axis=0, keepdims=True)

    tok = lambda w: pl.BlockSpec((tm, w), lambda b, i: (b * spt + (spt - 1 - i), 0))
    const = lambda s: pl.BlockSpec(s, lambda b, i: (0,) * len(s))
    return pl.pallas_call(
        body, name="mix_backward", grid=(bsz, spt),
        out_shape=[HBM_OUT((t, NGATE), BF16), HBM_OUT((t, AW), F32), HBM_OUT((t, AW), F32), HBM_OUT((t, D), BF16), HBM_OUT((t, D), BF16),
                   SDS((8, D), F32)],
        in_specs=[tok(NGATE), tok(D), tok(D), tok(D), tok(D), tok(AW), tok(AW), const((AW, D)), const((D, D)), const((D, D)), const((3, D)),
                  const((8, D))],
        out_specs=[tok(NGATE), tok(AW), tok(AW), tok(D), tok(D), const((8, D))],
        scratch_shapes=[pltpu.VMEM((8, D), F32)],
        compiler_params=_cp(("arbitrary", "arbitrary"), VMEM_CAP),
    )(*map(_in_hbm, (gates, dy, a_out, s_out, y_conv, o, lj)), w_ao, w_co, w_o, conv_w, vec_f)


def _scatter_copies(src, land, send_sems, recv_sems):
    x, y, c = _place()
    chips = [(1 - x, y), (x, 1 - y), (1 - x, 1 - y)]
    return [pltpu.make_async_remote_copy(src_ref=src[a].at[2 * cx + cy], dst_ref=land[a].at[r], send_sem=send_sems.at[3 * a + r],
                                         recv_sem=recv_sems.at[3 * a + r], device_id=(cx, cy, c), device_id_type=MESH)
            for a in range(len(src)) for r, (cx, cy) in enumerate(chips)]


def _halves_out(a):
    kind, nr, nc = W_CUTS[a]
    shape = (nr // 2, W_FULL[a][1]) if kind == "col" else (NCHIP, nr // 2, nc)
    return [SDS(shape, F32), SDS(shape, BF16)]


def _write_halves(a, acc_ref, c, mine_ref, theirs_ref):
    kind, nr, nc = W_CUTS[a]
    hr = nr // 2
    if kind == "col":
        mine_ref[...] = acc_ref[pl.ds(pl.multiple_of(c * hr, hr), hr), :]
        theirs_ref[...] = acc_ref[pl.ds(pl.multiple_of((1 - c) * hr, hr), hr), :].astype(BF16)
    else:
        for k in range(NCHIP):
            mine_ref[k] = acc_ref[pl.ds(pl.multiple_of(k * nr + c * hr, hr), hr), :]
            theirs_ref[k] = acc_ref[pl.ds(pl.multiple_of(k * nr + (1 - c) * hr, hr), hr), :].astype(BF16)


def _out_weight_grads(a_in, da_out, s_in, ds_out, merged, dy, core, tk=1024):
    t = dy.shape[0]
    nt = t // tk

    def body(c_ref, ain_ref, da_ref, sin_ref, ds_ref, m_ref, dy_ref, *rest):
        outs, (gao, gco, go) = rest[:6], rest[6:]

        @pl.when(pl.program_id(0) == 0)
        def _():
            gao[...] = jnp.zeros_like(gao)
            gco[...] = jnp.zeros_like(gco)
            go[...] = jnp.zeros_like(go)

        gao[...] += _dot_tn(ain_ref[...], da_ref[...])
        gco[...] += _dot_tn(sin_ref[...], ds_ref[...])
        go[...] += _dot_tn(m_ref[...], dy_ref[...])

        @pl.when(pl.program_id(0) == nt - 1)
        def _():
            for a, acc in ((1, gao), (2, gco), (3, go)):
                _write_halves(a, acc, c_ref[0], outs[2 * a - 2], outs[2 * a - 1])

    tok = lambda w: pl.BlockSpec((tk, w), lambda i, cr: (i, 0))
    out_shape = _halves_out(1) + _halves_out(2) + _halves_out(3)
    outs = pl.pallas_call(
        body, name="out_weight_grads", out_shape=out_shape,
        grid_spec=pltpu.PrefetchScalarGridSpec(
            num_scalar_prefetch=1, grid=(nt,), in_specs=[tok(AW), tok(D), tok(D), tok(D), tok(D), tok(D)],
            out_specs=[pl.BlockSpec(o.shape, lambda i, cr, nd=len(o.shape): (0,) * nd) for o in out_shape],
            scratch_shapes=[pltpu.VMEM((AW, D), F32), pltpu.VMEM((D, D), F32), pltpu.VMEM((D, D), F32)]),
        compiler_params=_cp(("arbitrary",), VMEM_CAP),
    )(core, a_in, da_out, s_in, ds_out, merged, dy)
    return [(outs[0], outs[1]), (outs[2], outs[3]), (outs[4], outs[5])]


def _input_grad(dq, dk, dv, dgates, w, x2, dxr, sc1p, seq, sums, tm=512):
    t = x2.shape[0]
    nt = t // tm
    spt = seq // tm
    bsz = t // seq
    n = len(sums)
    gblk = NGATE // 4
    nsteps = 3 + 4

    def body(dq_ref, dk_ref, dv_ref, dg_ref, wq_ref, wg_ref, x_ref, dxr_ref, sc_ref, *rest):
        src, (dx_ref, dsh_ref, dsc_ref), land = rest[:n], rest[n:n + 3], rest[n + 3:2 * n + 3]
        acc_ref, send_sems, recv_sems = rest[2 * n + 3:]
        j, i = pl.program_id(0), pl.program_id(1)
        copies = _scatter_copies(src, land, send_sems, recv_sems)
        rows = pl.ds(pl.multiple_of(i * tm, tm), tm)

        @pl.when((i == 0) & (j == 0))
        def _():
            for cp in copies:
                cp.start()

        for k, ref in enumerate((dq_ref, dk_ref, dv_ref)):
            @pl.when(j == k)
            def _(k=k, ref=ref):
                part = _dot_nt(ref[...], wq_ref[...])
                if k == 0:
                    acc_ref[rows, :] = part
                else:
                    acc_ref[rows, :] += part

        @pl.when((j >= 3) & (j < nsteps - 1))
        def _():
            acc_ref[rows, :] += _dot_nt(dg_ref[...], wg_ref[...])

        @pl.when(j == nsteps - 1)
        def _():
            dh = acc_ref[rows, :] + _dot_nt(dg_ref[...], wg_ref[...])
            dx_ref[...] = dh * sc_ref[0] + dxr_ref[...]

            @pl.when(i % spt == 0)
            def _():
                dsh_ref[...] = jnp.zeros_like(dsh_ref)
                dsc_ref[...] = jnp.zeros_like(dsc_ref)

            dsh_ref[0] += jnp.sum(dh, axis=0, keepdims=True)
            dsc_ref[0] += jnp.sum(dh * x_ref[...], axis=0, keepdims=True)

        @pl.when((i == nt - 1) & (j == nsteps - 1))
        def _():
            for cp in copies:
                cp.wait()

    def held(k):
        return lambda j, i: (jnp.where(j == k, i, jnp.where(j < k, 0, nt - 1)), 0)

    last = lambda j, i: (jnp.where(j == nsteps - 1, i, 0), 0)
    outs = pl.pallas_call(
        body, name="input_grad", grid=(nsteps, nt),
        out_shape=[SDS((t, D), F32), SDS((bsz, 1, D), F32), SDS((bsz, 1, D), F32)] + [SDS((3,) + s.shape[1:], BF16) for s in sums],
        in_specs=[pl.BlockSpec((tm, QW), held(0)), pl.BlockSpec((tm, QW), held(1)), pl.BlockSpec((tm, QW), held(2)),
                  pl.BlockSpec((tm, gblk), lambda j, i: (jnp.where(j >= 3, i, 0), jnp.clip(j - 3, 0, 3))),
                  pl.BlockSpec((D, QW), lambda j, i: (0, jnp.minimum(j, 2))),
                  pl.BlockSpec((pl.Element(D), pl.Element(gblk)), lambda j, i: (0, pl.multiple_of(3 * QW + gblk * jnp.clip(j - 3, 0, 3), 128))),
                  pl.BlockSpec((tm, D), last), pl.BlockSpec((tm, D), last),
                  pl.BlockSpec((1, 1, D), lambda j, i: (jnp.where(j == nsteps - 1, i // spt, 0), 0, 0))] + [ANY] * n,
        out_specs=[pl.BlockSpec((tm, D), last),
                   pl.BlockSpec((1, 1, D), lambda j, i: (jnp.where(j == nsteps - 1, i // spt, 0), 0, 0)),
                   pl.BlockSpec((1, 1, D), lambda j, i: (jnp.where(j == nsteps - 1, i // spt, 0), 0, 0))] + [ANY] * n,
        scratch_shapes=[pltpu.VMEM((t, D), F32), pltpu.SemaphoreType.DMA((3 * NCHIP,)), pltpu.SemaphoreType.DMA((3 * NCHIP,))],
        compiler_params=_cp(("arbitrary", "arbitrary"), VMEM_CAP, side=True),
    )(*map(_in_hbm, (dq, dk, dv, dgates, w, w, x2, dxr)), sc1p, *sums)
    return outs[0], outs[1], outs[2], outs[3:]


def _in_weight_grad(ht, dq, dk, dv, dgates, core, sums):
    t = ht.shape[1]
    hr = D // 2
    n = len(sums)

    def body(c_ref, ht_ref, dq_ref, dk_ref, dv_ref, dg_ref, *rest):
        src, mine_ref, got_ref, land = rest[:n], rest[n], rest[n + 1], rest[n + 2:2 * n + 2]
        acc_ref, their_buf, send_sems, recv_sems, tile_send, tile_recv = rest[2 * n + 2:]
        j = pl.program_id(0)
        slot = j % 2
        px, py, pc = _place()
        copies = _scatter_copies(src, land, send_sems, recv_sems)

        def to_sibling(step, k):
            return pltpu.make_async_remote_copy(src_ref=their_buf.at[k], dst_ref=got_ref.at[:, pl.ds(pl.multiple_of(step * TN, TN), TN)],
                                                send_sem=tile_send.at[k], recv_sem=tile_recv.at[0], device_id=(px, py, 1 - pc),
                                                device_id_type=MESH)

        @pl.when(j == 0)
        def _():
            for cp in copies:
                cp.start()

        @pl.when(j >= 2)
        def _():
            to_sibling(j - 2, slot).wait_send()

        for k, ref in enumerate((dq_ref, dk_ref, dv_ref)):
            @pl.when((j >= k * NQT) & (j < (k + 1) * NQT))
            def _(ref=ref):
                acc_ref[...] = _dot(ht_ref[...], ref[...])

        @pl.when(j >= 3 * NQT)
        def _():
            acc_ref[...] = _dot(ht_ref[...], dg_ref[...])

        _write_halves(0, acc_ref, c_ref[0], mine_ref, their_buf.at[slot])
        to_sibling(j, slot).start()

        @pl.when(j == NPT - 1)
        def _():
            to_sibling(j - 1, 1 - slot).wait_send()
            to_sibling(j, slot).wait_send()
            pltpu.make_async_remote_copy(src_ref=got_ref, dst_ref=got_ref, send_sem=tile_send.at[0], recv_sem=tile_recv.at[0],
                                         device_id=(px, py, 1 - pc), device_id_type=MESH).wait_recv()
            for cp in copies:
                cp.wait()

    def part(k):
        return pl.BlockSpec((t, TN), lambda j, cr: (0, jnp.clip(j - k * NQT, 0, NQT - 1)))

    outs = pl.pallas_call(
        body, name="in_weight_grad", out_shape=[SDS((hr, NCOL), F32), SDS((hr, NCOL), BF16)] + [SDS((3,) + v.shape[1:], BF16) for v in sums],
        grid_spec=pltpu.PrefetchScalarGridSpec(
            num_scalar_prefetch=1, grid=(NPT,),
            in_specs=[pl.BlockSpec((D, t), lambda j, cr: (0, 0)), part(0), part(1), part(2),
                      pl.BlockSpec((t, TN), lambda j, cr: (0, jnp.maximum(j - 3 * NQT, 0)))] + [ANY] * n,
            out_specs=[pl.BlockSpec((hr, TN), lambda j, cr: (0, j)), ANY] + [ANY] * n,
            scratch_shapes=[pltpu.VMEM((D, TN), F32), pltpu.VMEM((2, hr, TN), BF16),
                            pltpu.SemaphoreType.DMA((3 * NCHIP,)), pltpu.SemaphoreType.DMA((3 * NCHIP,)),
                            pltpu.SemaphoreType.DMA((2,)), pltpu.SemaphoreType.DMA((1,))]),
        compiler_params=_cp(("arbitrary",), VMEM_CAP, side=True),
    )(core, *map(_in_hbm, (ht, dq, dk, dv, dgates)), *sums)
    return outs[0], outs[1], outs[2:]


def _sum_partials(gathered):
    def body(g_ref, o_ref):
        acc = g_ref[0]
        for k in range(1, 8):
            acc = acc + g_ref[k]
        o_ref[...] = acc

    return pl.pallas_call(body, name="sum_partials", out_shape=SDS(gathered.shape[1:], F32), in_specs=[VMEM_SPEC], out_specs=VMEM_SPEC)(gathered)


def _adamw(w, g, m, v, name, tr=256):
    r, cdim = w.shape
    tr = tr if cdim <= D else tr // 2
    tr = tr if (r % tr == 0 and r > tr) else r

    def body(w_ref, g_ref, m_ref, v_ref, go_ref, d_ref, nm_ref, nv_ref):
        gv = g_ref[...]
        go_ref[...] = gv
        nm = B1 * m_ref[...] + (1.0 - B1) * gv
        nv = B2 * v_ref[...] + (1.0 - B2) * (gv * gv)
        m_hat = nm / (1.0 - B1 ** STEP)
        v_hat = nv / (1.0 - B2 ** STEP)
        d_ref[...] = -LR * (m_hat / (jnp.sqrt(v_hat) + EPS) + WD * w_ref[...])
        nm_ref[...] = nm
        nv_ref[...] = nv

    spec = pl.BlockSpec((tr, cdim), lambda i: (i, 0))
    return pl.pallas_call(
        body, name=name, grid=(r // tr,), out_shape=[SDS((r, cdim), F32)] * 4, in_specs=[spec] * 4, out_specs=[spec] * 4,
        compiler_params=_cp(("parallel",), VMEM_CAP // 2),
    )(w, g, m, v)


def _adamw_halves(w, g_mine, g_other, m, v, core, name):
    r, cdim = w.shape
    hr = r // 2
    tr = min(hr, 256 if cdim <= D else 128)
    nh = hr // tr

    def body(c_ref, w_ref, gm_ref, go_ref, m_ref, v_ref, g_ref, d_ref, nm_ref, nv_ref):
        gv = jnp.where(pl.program_id(0) // nh == c_ref[0], gm_ref[...], go_ref[...])
        g_ref[...] = gv
        nm = B1 * m_ref[...] + (1.0 - B1) * gv
        nv = B2 * v_ref[...] + (1.0 - B2) * (gv * gv)
        m_hat = nm / (1.0 - B1 ** STEP)
        v_hat = nv / (1.0 - B2 ** STEP)
        d_ref[...] = -LR * (m_hat / (jnp.sqrt(v_hat) + EPS) + WD * w_ref[...])
        nm_ref[...] = nm
        nv_ref[...] = nv

    spec = pl.BlockSpec((tr, cdim), lambda i, cr: (i, 0))
    mine = pl.BlockSpec((tr, cdim), lambda i, cr: (jnp.clip(i - cr[0] * nh, 0, nh - 1), 0))
    other = pl.BlockSpec((tr, cdim), lambda i, cr: (jnp.clip(i - (1 - cr[0]) * nh, 0, nh - 1), 0))
    return pl.pallas_call(
        body, name=name, out_shape=[SDS((r, cdim), F32)] * 4,
        grid_spec=pltpu.PrefetchScalarGridSpec(num_scalar_prefetch=1, grid=(r // tr,), in_specs=[spec, mine, other, spec, spec],
                                               out_specs=[spec] * 4),
        compiler_params=_cp(("arbitrary",), VMEM_CAP // 2),
    )(core, w, g_mine, g_other, m, v)


def _t5_bucket(dist):
    n = jnp.maximum(dist, 1).astype(F32)
    large = MAX_EXACT + (jnp.log(n / MAX_EXACT) / math.log(MAX_DISTANCE / MAX_EXACT) * (N_BUCKETS - MAX_EXACT)).astype(jnp.int32)
    large = jnp.minimum(large, N_BUCKETS - 1)
    return jnp.where(dist < MAX_EXACT, dist, large)


def _band_buckets():
    a = jnp.arange(BLK)[:, None]
    b = jnp.arange(2 * BLK)[None, :]
    steps = jnp.maximum(a + BLK - b, 0)
    return jnp.stack([_t5_bucket(steps * d) for d in DILATIONS]).astype(jnp.int32)


def _pad_rows(a, rows=8):
    return jnp.pad(a, ((0, rows - a.shape[0]), (0, 0)))


def kernel(x, c, w_ada, b_ada, w_in, conv_w, conv_b, rel_bias, w_attn_out, w_conv_out, w_o, ln_g, ln_b, loss_target, m_w_ada, m_b_ada, m_w_in, m_conv_w, m_conv_b, m_rel_bias, m_w_attn_out, m_w_conv_out, m_w_o, m_ln_g, m_ln_b, v_w_ada, v_b_ada, v_w_in, v_conv_w, v_conv_b, v_rel_bias, v_w_attn_out, v_w_conv_out, v_w_o, v_ln_g, v_ln_b):
    bsz, seq, _ = x.shape
    t = bsz * seq
    mx, my, mc = _place()
    chip = 2 * mx + my
    dev = 4 * mx + 2 * my + mc
    x2 = x.reshape(t, D)
    tgt = loss_target.reshape(t, D)

    mine = _to_bf16_windows([w[0] for w in (w_in, w_attn_out, w_conv_out, w_o)])

    n_ada = w_ada.shape[2]
    n_cw = conv_w.shape[2]
    c_and_cw = jnp.concatenate([_pad_rows(c), jnp.pad(conv_w[0], ((0, 5), (0, D - n_cw)))], axis=0)
    firsts = _all_gather8(c_and_cw, "gather_c_conv_w")
    c_all = firsts[:, 0:bsz, :].reshape(8 * bsz, D)
    conv_w_f = firsts[0::2, 8:11, 0:n_cw].transpose(1, 0, 2).reshape(3, D)
    b_cols = lax.dynamic_slice(b_ada, (0, chip * n_ada), (1, n_ada))
    mod_part = _ada_forward(c_all, w_ada[0], b_cols)
    mod_parts = _all_gather8(mod_part, "gather_mod")
    mod_all = mod_parts[0::2].transpose(1, 0, 2).reshape(8 * bsz, 3 * D)
    mod = lax.dynamic_slice(mod_all, (dev * bsz, 0), (bsz, 3 * D))
    shift = mod[:, 0:D].reshape(bsz, 1, D)
    sc1p = 1.0 + mod[:, D:2 * D].reshape(bsz, 1, D)
    gate = mod[:, 2 * D:].reshape(bsz, 1, D)

    h, ht = _modulate(x2, sc1p, shift, seq)
    tab = lax.dynamic_index_in_dim(jnp.asarray(_tile_tables()), chip, 0, keepdims=False)
    qkv, gates, (w_in_f, w_ao_f, w_co_f, w_o_f) = _project_gather(h, mine, tab)
    buckets = _band_buckets()
    bias = _bias_tables(rel_bias, buckets)
    og, lg = [], []
    for g in range(3):
        o_g, l_g = _attn_forward(g, qkv, bias[g], bsz, seq)
        og.append(o_g)
        lg.append(l_g)
    (a_in, s_in, merged, dy, a_out, s_out, y_conv, o, lj, dxr, vec_f, dgate) = _mix_forward(
        gates, og, lg, x2, tgt, gate, w_ao_f, w_co_f, w_o_f, conv_w_f, conv_b, ln_g, ln_b, bsz, seq)

    dgates, do, dl, da_out, ds_out, vec = _mix_backward(gates, dy, a_out, s_out, y_conv, o, lj, w_ao_f, w_co_f, w_o_f, conv_w_f, vec_f, bsz, seq)
    core = jnp.reshape(mc, (1,)).astype(jnp.int32)
    small_grads = _out_weight_grads(a_in, da_out, s_in, ds_out, merged, dy, core)
    got_small = _swap_halves([theirs for _, theirs in small_grads], "swap_small_grad_halves")
    sums_small = _chip_sums([own for own, _ in small_grads], got_small, 1, "chip_sums_small")
    dqkv, dbs = None, []
    for g in range(3):
        dqkv, db = _attn_backward(g, qkv, do, dl, bias[g], dqkv, bsz, seq)
        dbs.append(db)
    dq, dk, dv = dqkv
    drb = _bias_grad(jnp.stack(dbs), buckets)
    drb = drb[:, :, 0:4].transpose(1, 0, 2).reshape(N_BUCKETS, 12)
    g_in_mine, got_in, landed_small = _in_weight_grad(ht, dq, dk, dv, dgates, core, [bf for _, bf in sums_small])
    sums_in = _chip_sums([g_in_mine], [got_in], 0, "chip_sums_in")
    grad_x, dshift, dscale, landed_in = _input_grad(dq, dk, dv, dgates, w_in_f, x2, dxr, sc1p, seq, [bf for _, bf in sums_in])
    reduced = dict(zip(("w_in", "w_attn_out", "w_conv_out", "w_o"),
                       _reduce_mine([own for own, _ in sums_in + sums_small], list(landed_in) + list(landed_small))))

    dmod = jnp.concatenate([dshift, dscale, dgate], axis=2).reshape(bsz * 3, D)
    drb_row = jnp.pad(drb.reshape(1, N_BUCKETS * 12), ((0, 0), (0, D - N_BUCKETS * 12)))
    vec = lax.dynamic_update_slice(vec, drb_row, (7, 0))
    packed = jnp.concatenate([vec, _pad_rows(dmod)], axis=0)
    gathered = _all_gather8(packed, "gather_small")
    small = _sum_partials(gathered)
    g_ln_g, g_ln_b, loss_lanes = small[0:1], small[1:2], small[2:3]
    g_conv_w_full, g_conv_b = small[3:6], small[6:7]
    g_rel_bias = small[7, 0:N_BUCKETS * 12].reshape(N_BUCKETS, 12)
    loss = 0.5 / D * jnp.sum(loss_lanes)
    dmod_all = gathered[:, 8:8 + 3 * bsz, :].reshape(8 * bsz, 3 * D)
    dmod_cols = lax.dynamic_slice(dmod_all, (0, chip * n_ada), (8 * bsz, n_ada))
    gw_ada, gb_ada = _ada_backward(c_all, dmod_cols, dmod_all)
    g_conv_w = lax.dynamic_slice(g_conv_w_full, (0, chip * n_cw), (3, n_cw))

    names = ["w_ada", "b_ada", "w_in", "conv_w", "conv_b", "rel_bias", "w_attn_out", "w_conv_out", "w_o", "ln_g", "ln_b"]
    two_d = lambda a: a.reshape(a.shape[-2:]) if a.ndim == 3 else a
    weights = dict(zip(names, map(two_d, (w_ada, b_ada, w_in, conv_w, conv_b, rel_bias, w_attn_out, w_conv_out, w_o, ln_g, ln_b))))
    ms = dict(zip(names, map(two_d, (m_w_ada, m_b_ada, m_w_in, m_conv_w, m_conv_b, m_rel_bias, m_w_attn_out, m_w_conv_out, m_w_o, m_ln_g, m_ln_b))))
    vs = dict(zip(names, map(two_d, (v_w_ada, v_b_ada, v_w_in, v_conv_w, v_conv_b, v_rel_bias, v_w_attn_out, v_w_conv_out, v_w_o, v_ln_g, v_ln_b))))
    grads = dict(zip(names, (gw_ada, gb_ada, None, g_conv_w, g_conv_b, g_rel_bias, None, None, None, g_ln_g, g_ln_b)))
    shapes = dict(zip(names, (w_ada, b_ada, w_in, conv_w, conv_b, rel_bias, w_attn_out, w_conv_out, w_o, ln_g, ln_b)))
    grad_out, deltas, new_m, new_v = {}, {}, {}, {}
    for n in names:
        if n in reduced:
            grad_out[n], deltas[n], new_m[n], new_v[n] = _adamw_halves(weights[n], *reduced[n], ms[n], vs[n], core, f"adamw_{n}")
        else:
            grad_out[n], deltas[n], new_m[n], new_v[n] = _adamw(weights[n], grads[n], ms[n], vs[n], f"adamw_{n}")
    shaped = lambda d: [d[n].reshape(shapes[n].shape) for n in names]
    return (loss, grad_x.reshape(bsz, seq, D), *shaped(grad_out), *shaped(deltas), *shaped(new_m), *shaped(new_v))
```

```python
import math

import numpy as np
import jax
import jax.numpy as jnp
from jax import lax
from jax.experimental import pallas as pl
from jax.experimental.pallas import tpu as pltpu

F32 = jnp.float32
BF16 = jnp.bfloat16
SDS = jax.ShapeDtypeStruct
MESH = pl.DeviceIdType.MESH
HBM_OUT = pltpu.HBM
ANY = pl.BlockSpec(memory_space=pl.ANY)
VMEM_SPEC = pl.BlockSpec(memory_space=pltpu.VMEM)

D = 1024
HD = 128
BLK = 128
QW = 1536
AW = 512
NGATE = 6656
GATE_COLS = ((0, 512), (512, 1536), (1536, 2560), (2560, 3584), (3584, 4608), (4608, 5632), (5632, 6656))
NCOL = 3 * QW + NGATE
TN = 512
NQT = QW // TN
NPT = NCOL // TN
DILATIONS = (1, 4, 16)
N_BUCKETS, MAX_EXACT, MAX_DISTANCE = 32, 16, 2048
ALPHA = 2.0 ** 0.25
LN_EPS = 1e-5
NEG = -1e30
SCALE = HD ** -0.5
LR, B1, B2, EPS, WD, STEP = 0.001, 0.9, 0.999, 1e-08, 0.01, 10
NCHIP = 4
VMEM_CAP = 60 * 2 ** 20


def _cp(sem=None, vmem=None, side=False):
    return pltpu.CompilerParams(dimension_semantics=sem, vmem_limit_bytes=vmem, has_side_effects=side)


def _dot(a, b):
    return jnp.dot(a, b, preferred_element_type=F32)


def _dot_nt(a, b):
    return lax.dot_general(a, b, (((1,), (1,)), ((), ())), preferred_element_type=F32)


def _dot_tn(a, b):
    return lax.dot_general(a, b, (((0,), (0,)), ((), ())), preferred_element_type=F32)


def _sig(x):
    return 1.0 / (1.0 + jnp.exp(-x))


def _in_hbm(a):
    return pltpu.with_memory_space_constraint(a, pltpu.HBM)


def _place():
    x, y, c = lax.axis_index("x"), lax.axis_index("y"), lax.axis_index("c")
    return x, y, c


def _all_gather8(v, name):
    r, cdim = v.shape

    def body(v_ref, out_ref, send_sems, recv_sems, local_sem):
        x, y, c = _place()
        me = 4 * x + 2 * y + c
        peers = [(x, y, 1 - c), (1 - x, y, c), (x, 1 - y, c), (1 - x, 1 - y, c),
                 (1 - x, y, 1 - c), (x, 1 - y, 1 - c), (1 - x, 1 - y, 1 - c)]
        mine = pltpu.make_async_copy(v_ref, out_ref.at[me], local_sem)
        mine.start()

        def copy(k, block, to):
            return pltpu.make_async_remote_copy(src_ref=v_ref, dst_ref=out_ref.at[block], send_sem=send_sems.at[k],
                                                recv_sem=recv_sems.at[k], device_id=to, device_id_type=MESH)

        sends = [copy(k, me, p) for k, p in enumerate(peers)]
        for cp in sends:
            cp.start()
        for k, (px, py, pc) in enumerate(peers):
            copy(k, 4 * px + 2 * py + pc, (px, py, pc)).wait_recv()
        for cp in sends:
            cp.wait_send()
        mine.wait()

    return pl.pallas_call(
        body, name=name, out_shape=SDS((8, r, cdim), v.dtype), in_specs=[VMEM_SPEC], out_specs=VMEM_SPEC,
        scratch_shapes=[pltpu.SemaphoreType.DMA((7,)), pltpu.SemaphoreType.DMA((7,)), pltpu.SemaphoreType.DMA(())],
        compiler_params=_cp(side=True),
    )(v)


W_CUTS = (("col", D, NCOL // NCHIP), ("col", AW, D // NCHIP), ("row", D // NCHIP, D), ("row", D // NCHIP, D))
W_FULL = ((D, NCOL), (AW, D), (D, D), (D, D))


def _shard_window(ref, cut, k, half):
    kind, nr, nc = cut
    hr = nr // 2
    if kind == "col":
        rows = pl.ds(0, nr) if half is None else pl.ds(pl.multiple_of(half * hr, 16), hr)
        return ref.at[rows, pl.ds(pl.multiple_of(k * nc, 128), nc)]
    if half is None:
        return ref.at[pl.ds(pl.multiple_of(k * nr, 16), nr), :]
    return ref.at[pl.ds(pl.multiple_of(k * nr + half * hr, 16), hr), :]


def _half_rows(ref, cut, half):
    hr = cut[1] // 2
    return ref.at[pl.ds(pl.multiple_of(half * hr, 16), hr), :]


def _to_bf16_windows(ws):
    x, y, _ = _place()
    chip = jnp.reshape(2 * x + y, (1,)).astype(jnp.int32)
    tr = 256
    n = len(ws)

    def body(c_ref, *refs):
        src, dst = refs[:n], refs[n:]
        dst[0][...] = src[0][...].astype(BF16)

        @pl.when(pl.program_id(0) == 0)
        def _():
            for a in range(1, n):
                dst[a][...] = src[a][...].astype(BF16)

    in_specs = [pl.BlockSpec((tr, W_CUTS[0][2]), lambda i, cr: (i, 0))]
    out_specs = [pl.BlockSpec((tr, W_CUTS[0][2]), lambda i, cr: (i, cr[0]))]
    for a in range(1, n):
        kind, nr, nc = W_CUTS[a]
        in_specs.append(pl.BlockSpec((nr, nc), lambda i, cr: (0, 0)))
        out_specs.append(pl.BlockSpec((nr, nc), (lambda i, cr: (0, cr[0])) if kind == "col" else (lambda i, cr: (cr[0], 0))))
    return pl.pallas_call(
        body, name="to_bf16", out_shape=[SDS(W_FULL[a], BF16) for a in range(n)],
        grid_spec=pltpu.PrefetchScalarGridSpec(num_scalar_prefetch=1, grid=(D // tr,), in_specs=in_specs, out_specs=out_specs),
        compiler_params=_cp(("arbitrary",)),
    )(chip, *ws)


def _swap_halves(theirs, name):
    n = len(theirs)

    def body(*refs):
        src, land = refs[:n], refs[n:2 * n]
        send_sems, recv_sems = refs[2 * n:]
        x, y, c = _place()
        copies = [pltpu.make_async_remote_copy(src_ref=src[a], dst_ref=land[a], send_sem=send_sems.at[a], recv_sem=recv_sems.at[a],
                                               device_id=(x, y, 1 - c), device_id_type=MESH) for a in range(n)]
        for cp in copies:
            cp.start()
        for cp in copies:
            cp.wait()

    return pl.pallas_call(
        body, name=name, out_shape=[SDS(v.shape, v.dtype) for v in theirs], in_specs=[ANY] * n, out_specs=[ANY] * n,
        scratch_shapes=[pltpu.SemaphoreType.DMA((n,)), pltpu.SemaphoreType.DMA((n,))],
        compiler_params=_cp(side=True),
    )(*theirs)


def _chip_sums(mines, gots, first, name):
    n = len(mines)
    x, y, _ = _place()
    me = jnp.reshape(2 * x + y, (1,)).astype(jnp.int32)

    def body(me_ref, *refs):
        ins, outs = refs[:2 * n], refs[2 * n:]
        for a in range(n):
            hr, nc = W_CUTS[first + a][1] // 2, W_CUTS[first + a][2]
            s = (ins[2 * a][...] + ins[2 * a + 1][...].astype(F32)).reshape(hr, nc)
            outs[2 * a + 1][0] = s.astype(BF16)

            @pl.when(pl.program_id(0) == me_ref[0])
            def _(a=a, s=s):
                outs[2 * a][...] = s

    in_specs, out_specs, out_shape = [], [], []
    for a in range(n):
        kind, nr, nc = W_CUTS[first + a]
        hr = nr // 2
        spec = pl.BlockSpec((hr, nc), lambda k, mr: (0, k)) if kind == "col" else pl.BlockSpec((1, hr, nc), lambda k, mr: (k, 0, 0))
        in_specs += [spec, spec]
        out_specs += [pl.BlockSpec((hr, nc), lambda k, mr: (0, 0)), pl.BlockSpec((1, hr, nc), lambda k, mr: (k, 0, 0))]
        out_shape += [SDS((hr, nc), F32), SDS((NCHIP, hr, nc), BF16)]
    outs = pl.pallas_call(
        body, name=name, out_shape=out_shape,
        grid_spec=pltpu.PrefetchScalarGridSpec(num_scalar_prefetch=1, grid=(NCHIP,), in_specs=in_specs, out_specs=out_specs),
        compiler_params=_cp(("arbitrary",), VMEM_CAP),
    )(me, *[v for pair in zip(mines, gots) for v in pair])
    return [(outs[2 * a], outs[2 * a + 1]) for a in range(n)]


def _reduce_mine(mines, gots):
    n = len(mines)
    _, _, c = _place()
    core = jnp.reshape(c, (1,)).astype(jnp.int32)
    tr = 256
    nsteps = W_CUTS[0][1] // 2 // tr

    def body(c_ref, *refs):
        ins, outs = refs[:2 * n], refs[2 * n:]

        def add(a):
            m_ref, g_ref = ins[2 * a], ins[2 * a + 1]
            outs[a][...] = ((m_ref[...] + g_ref[0].astype(F32)) + g_ref[1].astype(F32)) + g_ref[2].astype(F32)

        add(0)

        @pl.when(pl.program_id(0) == 0)
        def _():
            for a in range(1, n):
                add(a)

    nc0 = W_CUTS[0][2]
    in_specs = [pl.BlockSpec((tr, nc0), lambda i, cr: (i, 0)), pl.BlockSpec((3, tr, nc0), lambda i, cr: (0, i, 0))]
    out_specs = [pl.BlockSpec((tr, nc0), lambda i, cr: (cr[0] * nsteps + i, 0))]
    for a in range(1, n):
        hr, nc = W_CUTS[a][1] // 2, W_CUTS[a][2]
        in_specs += [pl.BlockSpec((hr, nc), lambda i, cr: (0, 0)), pl.BlockSpec((3, hr, nc), lambda i, cr: (0, 0, 0))]
        out_specs.append(pl.BlockSpec((hr, nc), lambda i, cr: (cr[0], 0)))
    return pl.pallas_call(
        body, name="reduce_mine", out_shape=[SDS((W_CUTS[a][1], W_CUTS[a][2]), F32) for a in range(n)],
        grid_spec=pltpu.PrefetchScalarGridSpec(num_scalar_prefetch=1, grid=(nsteps,), in_specs=in_specs, out_specs=out_specs),
        compiler_params=_cp(("arbitrary",), VMEM_CAP),
    )(core, *[v for pair in zip(mines, gots) for v in pair])


def _join_halves(fulls):
    n = len(fulls)

    def body(*refs):
        full = refs[n:2 * n]
        send_sems, recv_sems = refs[2 * n:]
        x, y, c = _place()
        sibling = (x, y, 1 - c)

        def swap(a, half):
            rows = _half_rows(full[a], W_CUTS[a], half)
            return pltpu.make_async_remote_copy(src_ref=rows, dst_ref=rows, send_sem=send_sems.at[a], recv_sem=recv_sems.at[a],
                                                device_id=sibling, device_id_type=MESH)

        sends = [swap(a, c) for a in range(n)]
        for cp in sends:
            cp.start()
        for a, cp in enumerate(sends):
            cp.wait_send()
            swap(a, 1 - c).wait_recv()

    return pl.pallas_call(
        body, name="join_grad_halves", out_shape=[SDS((W_CUTS[a][1], W_CUTS[a][2]), F32) for a in range(n)],
        in_specs=[ANY] * n, out_specs=[ANY] * n,
        scratch_shapes=[pltpu.SemaphoreType.DMA((n,)), pltpu.SemaphoreType.DMA((n,))],
        input_output_aliases={a: a for a in range(n)}, compiler_params=_cp(side=True),
    )(*fulls)


def _ada_forward(c_all, w_ada, b_cols):
    nb, nc = c_all.shape[0], w_ada.shape[1]

    def body(c_ref, w_ref, b_ref, o_ref):
        cv = c_ref[...]
        sc = (cv * _sig(cv)).astype(BF16)
        o_ref[...] = _dot(sc, w_ref[...].astype(BF16)) + b_ref[...]

    return pl.pallas_call(body, name="ada_forward", out_shape=SDS((nb, nc), F32), compiler_params=_cp(vmem=VMEM_CAP // 2))(c_all, w_ada, b_cols)


def _ada_backward(c_all, dmod_cols, dmod_all):
    nb, nc = dmod_cols.shape

    def body(c_ref, d_ref, a_ref, gw_ref, gb_ref):
        cv = c_ref[...]
        sc = (cv * _sig(cv)).astype(BF16)
        gw_ref[...] = _dot_tn(sc, d_ref[...].astype(BF16))
        gb_ref[...] = jnp.sum(a_ref[...], axis=0, keepdims=True)

    return pl.pallas_call(body, name="ada_backward", out_shape=[SDS((D, nc), F32), SDS((1, dmod_all.shape[1]), F32)],
                          compiler_params=_cp(vmem=VMEM_CAP // 2))(c_all, dmod_cols, dmod_all)


def _modulate(x2, sc1p, shift, seq, tm=512):
    t = x2.shape[0]
    spt = seq // tm

    def body(x_ref, sc_ref, sh_ref, h_ref, ht_ref):
        h = x_ref[...] * sc_ref[0] + sh_ref[0]
        h_ref[...] = h.astype(BF16)
        ht_ref[...] = h.T.astype(BF16)

    per_seq = pl.BlockSpec((1, 1, D), lambda i: (i // spt, 0, 0))
    return pl.pallas_call(
        body, name="modulate", out_shape=[HBM_OUT((t, D), BF16), HBM_OUT((D, t), BF16)], grid=(t // tm,),
        in_specs=[pl.BlockSpec((tm, D), lambda i: (i, 0)), per_seq, per_seq],
        out_specs=[pl.BlockSpec((tm, D), lambda i: (i, 0)), pl.BlockSpec((D, tm), lambda i: (0, i))],
        compiler_params=_cp(("parallel",)),
    )(_in_hbm(x2), sc1p, shift)


TW = 256
TPS = NCOL // NCHIP // TW
NT = NCOL // TW
NQKV_T = 3 * QW // TW
N_TILE_SEMS = 2 * 3 * TPS


def _tile_tables():
    tabs = np.zeros((NCHIP, 3, NT), np.int32)
    for me in range(NCHIP):
        tiles = [TPS * (me ^ (s // TPS)) + s % TPS for s in range(NT)]
        tabs[me, 0] = tiles
        for row, (lo, hi) in enumerate(((0, NQKV_T), (NQKV_T, NT))):
            mine = [w - lo if lo <= w < hi else None for w in tiles]
            held = next(m for m in mine if m is not None)
            for s, m in enumerate(mine):
                held = held if m is None else m
                tabs[me, 1 + row, s] = held
    return tabs


def _project_gather(h, fulls, tab):
    t = h.shape[0]
    n = len(fulls)

    def body(tab_ref, h_ref, *rest):
        qkv_ref, g_ref = rest[n], rest[n + 1]
        full = rest[n + 2:2 * n + 2]
        w_buf, tile_sems, send_sems, recv_sems = rest[2 * n + 2:]
        s = pl.program_id(0)
        x, y, c = _place()
        me = 2 * x + y
        peers = [(x, 1 - y), (1 - x, y), (1 - x, 1 - y)]
        sibling = (x, y, 1 - c)

        def hop(a, r, stage, chip, half, to):
            window = _shard_window(full[a], W_CUTS[a], chip, half)
            k = N_TILE_SEMS + 6 * (a - 1) + 2 * r + stage
            return pltpu.make_async_remote_copy(src_ref=window, dst_ref=window, send_sem=send_sems.at[k], recv_sem=recv_sems.at[k],
                                                device_id=to, device_id_type=MESH)

        def tile_hop(q, stage, col_step, half, to):
            col = pl.multiple_of(tab_ref[0, col_step] * TW, TW)
            window = full[0].at[pl.ds(pl.multiple_of(half * (D // 2), 16), D // 2), pl.ds(col, TW)]
            k = 2 * (q - TPS) + stage
            return pltpu.make_async_remote_copy(src_ref=window, dst_ref=window, send_sem=send_sems.at[k], recv_sem=recv_sems.at[k],
                                                device_id=to, device_id_type=MESH)

        def send_tile(r, j):
            return tile_hop(TPS * (r + 1) + j, 0, j, c, (*peers[r], c))

        def pass_on(q, to):
            return tile_hop(3 * TPS + q % TPS, 0, q, c, to)

        def arrive(a, r):
            px, py = peers[r]
            chip = 2 * px + py
            hop(a, r, 0, chip, c, (px, py, c)).wait_recv()
            hop(a, r, 1, chip, c, sibling).start()
            hop(a, r, 1, chip, 1 - c, sibling).wait_recv()

        def tile(step, slot):
            col = pl.multiple_of(tab_ref[0, step] * TW, TW)
            return pltpu.make_async_copy(full[0].at[:, pl.ds(col, TW)], w_buf.at[slot], tile_sems.at[slot])

        @pl.when(s == 0)
        def _():
            for r in range(2):
                for j in range(TPS):
                    send_tile(r, j).start()
            tile(0, 0).start()

        @pl.when((s + 1 >= TPS) & (s + 1 < NT))
        def _():
            tile_hop(s + 1, 1, s + 1, 1 - c, sibling).wait_recv()

        @pl.when(s + 1 < NT)
        def _():
            tile(s + 1, 1 - (s % 2)).start()

        @pl.when((s + 2 >= TPS) & (s + 2 < NT))
        def _():
            tile_hop(s + 2, 0, s + 2, c, sibling).wait_recv()
            tile_hop(s + 2, 1, s + 2, c, sibling).start()

        for r in range(2):
            @pl.when(((s + 2) // TPS == r + 1) & ((s + 2) % 2 == (r + 1 + TPS * (r + 1)) % 2))
            def _(r=r):
                pass_on(s + 2, (*peers[1 - r], c)).start()

        @pl.when(s + 2 == 2 * TPS - 1)
        def _():
            for a in range(1, n):
                for r in range(3):
                    hop(a, r, 0, me, c, (*peers[r], c)).start()

        slot = s % 2
        tile(s, slot).wait()
        is_qkv = tab_ref[0, s] < NQKV_T
        for k in range(2):
            @pl.when(slot == k)
            def _(k=k):
                acc = _dot(h_ref[...], w_buf[k])

                @pl.when(is_qkv)
                def _():
                    qkv_ref[...] = acc.astype(BF16)

                @pl.when(jnp.logical_not(is_qkv))
                def _():
                    g_ref[...] = acc.astype(BF16)

        @pl.when(s == NT - 1)
        def _():
            for a in range(1, n):
                for r in range(3):
                    arrive(a, r)
            for r in range(3):
                for j in range(TPS):
                    send_tile(r, j).wait_send()
                    tile_hop(TPS * (r + 1) + j, 1, TPS * (r + 1) + j, c, sibling).wait_send()
                for a in range(1, n):
                    hop(a, r, 0, me, c, (*peers[r], c)).wait_send()
                    px, py = peers[r]
                    hop(a, r, 1, 2 * px + py, c, sibling).wait_send()

    n_sems = N_TILE_SEMS + 6 * (n - 1)
    outs = pl.pallas_call(
        body, name="project_gather", out_shape=[HBM_OUT((t, 3 * QW), BF16), HBM_OUT((t, NGATE), BF16)] + [SDS(s, BF16) for s in W_FULL],
        grid_spec=pltpu.PrefetchScalarGridSpec(
            num_scalar_prefetch=1, grid=(NT,),
            in_specs=[pl.BlockSpec((t, D), lambda s, tab: (0, 0))] + [ANY] * n,
            out_specs=[pl.BlockSpec((t, TW), lambda s, tab: (0, tab[1, s])), pl.BlockSpec((t, TW), lambda s, tab: (0, tab[2, s]))] + [ANY] * n,
            scratch_shapes=[pltpu.VMEM((2, D, TW), BF16), pltpu.SemaphoreType.DMA((2,)),
                            pltpu.SemaphoreType.DMA((n_sems,)), pltpu.SemaphoreType.DMA((n_sems,))]),
        input_output_aliases={2 + a: 2 + a for a in range(n)},
        compiler_params=_cp(("arbitrary",), VMEM_CAP, side=True),
    )(tab, _in_hbm(h), *fulls)
    return outs[0], outs[1], outs[2:]


def _bias_tables(rel_bias, buckets):
    def body(tab_ref, bk_ref, o_ref):
        a = lax.broadcasted_iota(jnp.int32, (BLK, 2 * BLK), 0)
        b = lax.broadcasted_iota(jnp.int32, (BLK, 2 * BLK), 1)
        steps = a + BLK - b
        valid = (steps >= 0) & (steps <= BLK)
        for g in range(3):
            bk = bk_ref[g]
            for j in range(4):
                def pick(kk, acc, bk=bk, col=4 * g + j):
                    return jnp.where(bk == kk, tab_ref[kk, col], acc)

                acc = lax.fori_loop(0, N_BUCKETS, pick, jnp.zeros((BLK, 2 * BLK), F32))
                o_ref[g, j] = jnp.where(valid, acc, NEG)

    return pl.pallas_call(
        body, name="bias_tables", out_shape=SDS((3, 4, BLK, 2 * BLK), F32),
        in_specs=[pl.BlockSpec(memory_space=pltpu.SMEM), VMEM_SPEC], out_specs=VMEM_SPEC,
    )(rel_bias, buckets)


def _bias_grad(ds_sum, buckets):
    def body(ds_ref, bk_ref, o_ref, part_ref):
        lane = lax.broadcasted_iota(jnp.int32, (N_BUCKETS, 128), 1)
        for g in range(3):
            def bucket(kk, carry, g=g):
                mine = bk_ref[g] == kk
                for j in range(4):
                    v = jnp.sum(jnp.where(mine, ds_ref[g, j], 0.0).reshape(BLK // 8, 8, 2 * BLK), axis=0)
                    part_ref[j, pl.ds(pl.multiple_of(kk * 8, 8), 8), :] = v[:, :BLK] + v[:, BLK:]
                return carry

            lax.fori_loop(0, N_BUCKETS, bucket, 0)
            out = jnp.zeros((N_BUCKETS, 128), F32)
            for j in range(4):
                rows = jnp.sum(part_ref[j], axis=1, keepdims=True)
                out = jnp.where(lane == j, jnp.sum(rows.reshape(N_BUCKETS, 8, 1), axis=1), out)
            o_ref[g] = out

    return pl.pallas_call(body, name="bias_grad", out_shape=SDS((3, N_BUCKETS, 128), F32), in_specs=[VMEM_SPEC, VMEM_SPEC],
                          out_specs=VMEM_SPEC, scratch_shapes=[pltpu.VMEM((4, N_BUCKETS * 8, 128), F32)])(ds_sum, buckets)


def _sub_rows(d, r, first, size):
    return pl.ds(first * d + r, size) if d == 1 else pl.ds(first * d + r, size, stride=d)


def _head_spec(seq, g, part):
    return pl.BlockSpec((seq, HD), lambda b, hh: (b, part * (QW // HD) + 4 * g + hh))


def _rows(start, count, stride):
    return pl.ds(start, count) if stride == 1 else pl.ds(start, count, stride=stride)


def _gather_rows(dst, dst0, src, src0, stride, count):
    for first in range(0, count, BLK):
        dst[pl.ds(dst0 + first, BLK), :] = src[_rows(src0 + first * stride, BLK, stride), :].astype(dst.dtype)


def _scatter_rows(dst, dst0, stride, src, src0, count):
    for first in range(0, count, BLK):
        dst[_rows(dst0 + first * stride, BLK, stride), :] = src[pl.ds(src0 + first, BLK), :].astype(dst.dtype)


def _by_subsequence(dst, src, d, wide=None, tmp=None):
    seq = src.shape[0]
    ln = seq // d
    if wide is not None:
        wide[...] = src[...].astype(F32)
        src = wide
    if d <= 4:
        for r in range(d):
            _gather_rows(dst, r * ln, src, r, d, ln)
    else:
        quarter = seq // 4
        for r4 in range(4):
            _gather_rows(tmp, r4 * quarter, src, r4, 4, quarter)
        for r4 in range(4):
            for a in range(d // 4):
                _gather_rows(dst, (4 * a + r4) * ln, tmp, r4 * quarter + a, d // 4, ln)


def _to_sequence(dst, src, d, tmp=None):
    seq = dst.shape[0]
    ln = seq // d
    if d <= 4:
        for r in range(d):
            _scatter_rows(dst, r, d, src, r * ln, ln)
    else:
        quarter = seq // 4
        for r4 in range(4):
            for a in range(d // 4):
                _scatter_rows(tmp, r4 * quarter + a, d // 4, src, (4 * a + r4) * ln, ln)
        for r4 in range(4):
            _scatter_rows(dst, r4, 4, tmp, r4 * quarter, quarter)


def _attn_forward(g, qkv, bias, bsz, seq):
    d = DILATIONS[g]
    ln = seq // d
    units = [(r, n) for r in range(d) for n in range(ln // BLK)]

    def band(n):
        return slice(BLK, 2 * BLK) if n == 0 else slice(0, 2 * BLK)

    def body(q_ref, k_ref, v_ref, b_ref, o_ref, l_ref, *scratch):
        hs = pl.program_id(1)
        s_scr, p_scr = scratch[:2]
        if d == 1:
            qd, kd, vd = q_ref, k_ref, v_ref
        else:
            wide, tmp, qd, kd, vd = scratch[2:7]
            for dst, src in ((qd, q_ref), (kd, k_ref), (vd, v_ref)):
                _by_subsequence(dst, src, d, wide, tmp)
        blk = lambda r, n: pl.ds(r * ln + n * BLK, BLK)
        direct = d <= 4
        out_rows = (lambda r, n: _sub_rows(d, r, n * BLK, BLK)) if direct else blk
        o_dst, l_dst = (o_ref, l_ref) if direct else scratch[7:9]
        for u, (r, n) in enumerate(units):
            s_scr[u, :, BLK:] = _dot_nt(qd[blk(r, n), :], kd[blk(r, n), :])
            if n > 0:
                s_scr[u, :, :BLK] = _dot_nt(qd[blk(r, n), :], kd[blk(r, n - 1), :])
        for u, (r, n) in enumerate(units):
            s = s_scr[u, :, band(n)] * SCALE + b_ref[hs, :, band(n)]
            m = jnp.max(s, axis=1, keepdims=True)
            e = jnp.exp(s - m)
            den = jnp.sum(e, axis=1, keepdims=True)
            p_scr[u, :, band(n)] = (e * (1.0 / den)).astype(BF16)
            l_dst[out_rows(r, n), :] = jnp.broadcast_to(m + jnp.log(den), (BLK, HD))
        for u, (r, n) in enumerate(units):
            acc = _dot(p_scr[u, :, BLK:], vd[blk(r, n), :])
            if n > 0:
                acc = acc + _dot(p_scr[u, :, :BLK], vd[blk(r, n - 1), :])
            o_dst[out_rows(r, n), :] = acc
        if not direct:
            _to_sequence(o_ref, o_dst, d, tmp)
            _to_sequence(l_ref, l_dst, d, tmp)

    rows_f32, rows_bf16 = pltpu.VMEM((seq, HD), F32), pltpu.VMEM((seq, HD), BF16)
    regrouped = [] if d == 1 else [rows_f32] * 2 + [rows_bf16] * 3 + ([] if d <= 4 else [rows_f32] * 2)
    out_spec = pl.BlockSpec((seq, HD), lambda b, hh: (b, hh))
    return pl.pallas_call(
        body, name=f"attn_forward_{g}", out_shape=[HBM_OUT((bsz * seq, AW), F32)] * 2, grid=(bsz, 4),
        in_specs=[_head_spec(seq, g, part) for part in range(3)] + [pl.BlockSpec((4, BLK, 2 * BLK), lambda b, hh: (0, 0, 0))],
        out_specs=[out_spec, out_spec],
        scratch_shapes=[pltpu.VMEM((len(units), BLK, 2 * BLK), F32), pltpu.VMEM((len(units), BLK, 2 * BLK), BF16)] + regrouped,
        compiler_params=_cp(("parallel", "parallel"), VMEM_CAP // 2),
    )(qkv, qkv, qkv, _in_hbm(bias))


def _attn_backward(g, qkv, do, dl, bias, prev_out, bsz, seq):
    d = DILATIONS[g]
    ln = seq // d
    units = [(r, n) for r in range(d) for n in range(ln // BLK)]

    def body(q_ref, k_ref, v_ref, do_ref, dl_ref, b_ref, *rest):
        dq_ref, dk_ref, dv_ref, db_ref = rest[-18:-14]
        wide, tmp, qd, kd, vd, dod, dld, dqd, dkd, dvd, s_scr, dp_scr, p_scr, ds_scr = rest[-14:]
        hs = pl.program_id(1)

        @pl.when((pl.program_id(0) == 0) & (hs == 0))
        def _():
            db_ref[...] = jnp.zeros_like(db_ref)

        for dst, src in ((qd, q_ref), (kd, k_ref), (vd, v_ref)):
            _by_subsequence(dst, src, d, wide, tmp)
        _by_subsequence(dod, do_ref, d, None, tmp)
        _by_subsequence(dld, dl_ref, d, None, tmp)
        dkd[...] = jnp.zeros_like(dkd)
        dvd[...] = jnp.zeros_like(dvd)
        blk = lambda r, n: pl.ds(r * ln + n * BLK, BLK)
        keys = lambda r, n: [(blk(r, n), slice(BLK, 2 * BLK))] + ([(blk(r, n - 1), slice(0, BLK))] if n > 0 else [])
        for u, (r, n) in enumerate(units):
            for rows, band in keys(r, n):
                s_scr[u, :, band] = _dot_nt(qd[blk(r, n), :], kd[rows, :])
                dp_scr[u, :, band] = _dot_nt(dod[blk(r, n), :], vd[rows, :])
        for u, (r, n) in enumerate(units):
            both = dld[blk(r, n), :]
            lse, delta = both[:, 0:1], both[:, 64:65]
            band = slice(BLK, 2 * BLK) if n == 0 else slice(0, 2 * BLK)
            p = jnp.exp(s_scr[u, :, band] * SCALE + b_ref[hs, :, band] - lse)
            ds = p * (dp_scr[u, :, band] - delta)
            p_scr[u, :, band] = p.astype(BF16)
            ds_scr[u, :, band] = ds.astype(BF16)
            db_ref[hs, :, band] += ds
        for u, (r, n) in enumerate(units):
            dq = jnp.zeros((BLK, HD), F32)
            for rows, band in keys(r, n):
                dvd[rows, :] += _dot_tn(p_scr[u, :, band], dod[blk(r, n), :])
                dkd[rows, :] += _dot_tn(ds_scr[u, :, band], qd[blk(r, n), :]) * SCALE
                dq = dq + _dot(ds_scr[u, :, band], kd[rows, :])
            dqd[blk(r, n), :] = dq * SCALE
        for out, acc in ((dq_ref, dqd), (dk_ref, dkd), (dv_ref, dvd)):
            if d == 1:
                out[...] = acc[...].astype(BF16)
            else:
                _to_sequence(wide, acc, d, tmp)
                out[...] = wide[...].astype(BF16)

    qkv_spec = _head_spec(seq, g, 0)
    out_spec = pl.BlockSpec((seq, HD), lambda b, hh: (b, hh))
    band_spec = pl.BlockSpec((4, BLK, 2 * BLK), lambda b, hh: (0, 0, 0))
    ins = [qkv, qkv, qkv, _in_hbm(do), _in_hbm(dl), _in_hbm(bias)]
    in_specs = [_head_spec(seq, g, part) for part in range(3)] + [out_spec, out_spec, band_spec]
    aliases = {}
    if prev_out is not None:
        ins += list(prev_out)
        in_specs += [ANY] * 3
        aliases = {6: 0, 7: 1, 8: 2}
    rows_bf16, rows_f32 = pltpu.VMEM((seq, HD), BF16), pltpu.VMEM((seq, HD), F32)
    staged = [pltpu.VMEM((len(units), BLK, 2 * BLK), F32)] * 2 + [pltpu.VMEM((len(units), BLK, 2 * BLK), BF16)] * 2
    dq, dk, dv, db = pl.pallas_call(
        body, name=f"attn_backward_{g}", out_shape=[HBM_OUT((bsz * seq, QW), BF16)] * 3 + [SDS((4, BLK, 2 * BLK), F32)], grid=(bsz, 4),
        in_specs=in_specs, out_specs=[qkv_spec] * 3 + [band_spec], input_output_aliases=aliases,
        scratch_shapes=[rows_f32] * 2 + [rows_bf16] * 4 + [rows_f32] * 4 + staged,
        compiler_params=_cp(("arbitrary", "arbitrary"), VMEM_CAP // 2),
    )(*ins)
    return (dq, dk, dv), db


def _mix_forward(gates, og, lg, x2, tgt, gate, w_ao, w_co, w_o, conv_w, conv_b, ln_g, ln_b, bsz, seq, tm=256):
    t = x2.shape[0]
    spt = seq // tm

    def body(g_ref, o1, o2, o3, l1, l2, l3, x_ref, t_ref, gate_ref, wao_ref, wco_ref, wo_ref, cw_ref, cb_ref, lng_ref, lnb_ref,
             ain_ref, sin_ref, mrg_ref, dy_ref, aout_ref, sout_ref, yc_ref, o_ref, lj_ref, dxr_ref, vec_ref, dgate_ref, zc_ref):
        b, i = pl.program_id(0), pl.program_id(1)

        @pl.when((b == 0) & (i == 0))
        def _():
            vec_ref[...] = jnp.zeros_like(vec_ref)

        @pl.when(i == 0)
        def _():
            zc_ref[...] = jnp.zeros_like(zc_ref)
            dgate_ref[...] = jnp.zeros_like(dgate_ref)

        g_attn, u, bg, cg, g_conv, m_attn, m_conv = (g_ref[:, lo:hi].astype(F32) for lo, hi in GATE_COLS)
        la, lb, lc = l1[...], l2[...], l3[...]
        mx = jnp.maximum(la, jnp.maximum(lb, lc))
        ea, eb, ec = jnp.exp(la - mx), jnp.exp(lb - mx), jnp.exp(lc - mx)
        den = ea + eb + ec
        o = (ea * o1[...] + eb * o2[...] + ec * o3[...]) / den
        o_ref[...] = o
        lj_ref[...] = mx + jnp.log(den)
        a_in = o * (g_attn * _sig(g_attn))
        ain_ref[...] = a_in.astype(BF16)
        a_out = _dot(a_in.astype(BF16), wao_ref[...])
        aout_ref[...] = a_out.astype(BF16)
        z = cg * u
        rows = lax.broadcasted_iota(jnp.int32, (tm, D), 0)
        c6, c7 = zc_ref[6:7, :], zc_ref[7:8, :]
        z1 = jnp.where(rows == 0, c7, pltpu.roll(z, 1, 0))
        z2 = jnp.where(rows == 0, c6, jnp.where(rows == 1, c7, pltpu.roll(z, 2, 0)))
        zc_ref[...] = z[tm - 8:tm, :]
        y_conv = (cw_ref[0:1, :] * z2 + cw_ref[1:2, :] * z1 + cw_ref[2:3, :] * z) + cb_ref[...]
        yc_ref[...] = y_conv.astype(BF16)
        s_in = bg * y_conv * (g_conv * _sig(g_conv))
        sin_ref[...] = s_in.astype(BF16)
        s_out = _dot(s_in.astype(BF16), wco_ref[...])
        sout_ref[...] = s_out.astype(BF16)
        merged = _sig(m_attn) * a_out + _sig(m_conv) * s_out
        mrg_ref[...] = merged.astype(BF16)
        y = _dot(merged.astype(BF16), wo_ref[...])
        gate1 = 1.0 + gate_ref[0]
        r = ALPHA * x_ref[...] + gate1 * y
        mu = jnp.mean(r, axis=1, keepdims=True)
        rc = r - mu
        rstd = lax.rsqrt(jnp.mean(rc * rc, axis=1, keepdims=True) + LN_EPS)
        xhat = rc * rstd
        diff = (xhat * lng_ref[...] + lnb_ref[...]) - t_ref[...]
        dout = diff * (1.0 / D)
        vec_ref[0:1, :] += jnp.sum(dout * xhat, axis=0, keepdims=True)
        vec_ref[1:2, :] += jnp.sum(dout, axis=0, keepdims=True)
        vec_ref[2:3, :] += jnp.sum(diff * diff, axis=0, keepdims=True)
        dxh = dout * lng_ref[...]
        dr = rstd * (dxh - jnp.mean(dxh, axis=1, keepdims=True) - xhat * jnp.mean(dxh * xhat, axis=1, keepdims=True))
        dxr_ref[...] = ALPHA * dr
        dy_ref[...] = (dr * gate1).astype(BF16)
        dgate_ref[0] += jnp.sum(dr * y, axis=0, keepdims=True)

    tok = lambda w: pl.BlockSpec((tm, w), lambda b, i: (b * spt + i, 0))
    const = lambda s: pl.BlockSpec(s, lambda b, i: (0,) * len(s))
    per_seq = pl.BlockSpec((1, 1, D), lambda b, i: (b, 0, 0))
    outs = pl.pallas_call(
        body, name="mix_forward", grid=(bsz, spt),
        out_shape=[HBM_OUT((t, AW), BF16), HBM_OUT((t, D), BF16), HBM_OUT((t, D), BF16), HBM_OUT((t, D), BF16), HBM_OUT((t, D), BF16),
                   HBM_OUT((t, D), BF16), HBM_OUT((t, D), BF16), HBM_OUT((t, AW), F32), HBM_OUT((t, AW), F32), HBM_OUT((t, D), F32),
                   SDS((8, D), F32), SDS((bsz, 1, D), F32)],
        in_specs=[tok(NGATE)] + [tok(AW)] * 6 + [tok(D), tok(D), per_seq, const((AW, D)), const((D, D)), const((D, D)),
                                                 const((3, D)), const((1, D)), const((1, D)), const((1, D))],
        out_specs=[tok(AW), tok(D), tok(D), tok(D), tok(D), tok(D), tok(D), tok(AW), tok(AW), tok(D), const((8, D)), per_seq],
        scratch_shapes=[pltpu.VMEM((8, D), F32)],
        compiler_params=_cp(("arbitrary", "arbitrary"), VMEM_CAP),
    )(_in_hbm(gates), *map(_in_hbm, og), *map(_in_hbm, lg), _in_hbm(x2), _in_hbm(tgt), gate, w_ao, w_co, w_o, conv_w, conv_b, ln_g, ln_b)
    return outs


def _mix_backward(gates, dy, a_out, s_out, y_conv, o, lj, w_ao, w_co, w_o, conv_w, vec_f, bsz, seq, tm=256):
    t = dy.shape[0]
    spt = seq // tm

    def body(g_ref, dy_ref, aout_ref, sout_ref, yc_ref, o_ref, lj_ref, wao_ref, wco_ref, wo_ref, cw_ref, vecf_ref,
             dg_ref, do_ref, dl_ref, daout_ref, dsout_ref, vec_ref, car_ref):
        b, i = pl.program_id(0), pl.program_id(1)

        @pl.when((b == 0) & (i == 0))
        def _():
            vec_ref[...] = vecf_ref[...]

        @pl.when(i == 0)
        def _():
            car_ref[...] = jnp.zeros_like(car_ref)

        g_attn, u, bg, cg, g_conv, m_attn, m_conv = (g_ref[:, lo:hi].astype(F32) for lo, hi in GATE_COLS)
        dmerged = _dot_nt(dy_ref[...], wo_ref[...])
        sa, sc = _sig(m_attn), _sig(m_conv)
        da_out = (dmerged * sa).astype(BF16)
        ds_out = (dmerged * sc).astype(BF16)
        daout_ref[...] = da_out
        dsout_ref[...] = ds_out
        dg_ref[:, 4608:5632] = (dmerged * aout_ref[...].astype(F32) * (sa * (1.0 - sa))).astype(BF16)
        dg_ref[:, 5632:6656] = (dmerged * sout_ref[...].astype(F32) * (sc * (1.0 - sc))).astype(BF16)
        da_in = _dot_nt(da_out, wao_ref[...])
        ds_in = _dot_nt(ds_out, wco_ref[...])
        sga = _sig(g_attn)
        o = o_ref[...]
        do = da_in * (g_attn * sga)
        do_ref[...] = do
        dg_ref[:, 0:512] = (da_in * o * (sga * (1.0 + g_attn * (1.0 - sga)))).astype(BF16)
        prod = do * o
        lane = lax.broadcasted_iota(jnp.int32, (tm, HD), 1)
        for j in range(4):
            cs = slice(j * HD, (j + 1) * HD)
            delta = jnp.sum(prod[:, cs], axis=1, keepdims=True)
            dl_ref[:, cs] = jnp.where(lane < 64, lj_ref[:, cs], delta)
        sgc = _sig(g_conv)
        silu_c = g_conv * sgc
        yc = yc_ref[...].astype(F32)
        dg_ref[:, 1536:2560] = (ds_in * yc * silu_c).astype(BF16)
        dg_ref[:, 3584:4608] = (ds_in * bg * yc * (sgc * (1.0 + g_conv * (1.0 - sgc)))).astype(BF16)
        dyc = ds_in * bg * silu_c
        rows = lax.broadcasted_iota(jnp.int32, (tm, D), 0)
        c0, c1 = car_ref[0:1, :], car_ref[1:2, :]
        n1 = jnp.where(rows == tm - 1, c0, pltpu.roll(dyc, tm - 1, 0))
        n2 = jnp.where(rows == tm - 2, c0, jnp.where(rows == tm - 1, c1, pltpu.roll(dyc, tm - 2, 0)))
        car_ref[...] = dyc[0:8, :]
        dz = cw_ref[2:3, :] * dyc + cw_ref[1:2, :] * n1 + cw_ref[0:1, :] * n2
        z = cg * u
        dg_ref[:, 512:1536] = (dz * cg).astype(BF16)
        dg_ref[:, 2560:3584] = (dz * u).astype(BF16)
        vec_ref[3:4, :] += jnp.sum(n2 * z, axis=0, keepdims=True)
        vec_ref[4:5, :] += jnp.sum(n1 * z, axis=0, keepdims=True)
        vec_ref[5:6, :] += jnp.sum(dyc * z, axis=0, keepdims=True)
        vec_ref[6:7, :] += jnp.sum(dyc, axis=0, keepdims=True)

    tok = lambda w: pl.BlockSpec((tm, w), lambda b, i: (b * spt + (spt - 1 - i), 0))
    const = lambda s: pl.BlockSpec(s, lambda b, i: (0,) * len(s))
    return pl.pallas_call(
        body, name="mix_backward", grid=(bsz, spt),
        out_shape=[HBM_OUT((t, NGATE), BF16), HBM_OUT((t, AW), F32), HBM_OUT((t, AW), F32), HBM_OUT((t, D), BF16), HBM_OUT((t, D), BF16),
                   SDS((8, D), F32)],
        in_specs=[tok(NGATE), tok(D), tok(D), tok(D), tok(D), tok(AW), tok(AW), const((AW, D)), const((D, D)), const((D, D)), const((3, D)),
                  const((8, D))],
        out_specs=[tok(NGATE), tok(AW), tok(AW), tok(D), tok(D), const((8, D))],
        scratch_shapes=[pltpu.VMEM((8, D), F32)],
        compiler_params=_cp(("arbitrary", "arbitrary"), VMEM_CAP),
    )(*map(_in_hbm, (gates, dy, a_out, s_out, y_conv, o, lj)), w_ao, w_co, w_o, conv_w, vec_f)


def _scatter_copies(src, land, send_sems, recv_sems):
    x, y, c = _place()
    chips = [(1 - x, y), (x, 1 - y), (1 - x, 1 - y)]
    return [pltpu.make_async_remote_copy(src_ref=src[a].at[2 * cx + cy], dst_ref=land[a].at[r], send_sem=send_sems.at[3 * a + r],
                                         recv_sem=recv_sems.at[3 * a + r], device_id=(cx, cy, c), device_id_type=MESH)
            for a in range(len(src)) for r, (cx, cy) in enumerate(chips)]


def _halves_out(a):
    kind, nr, nc = W_CUTS[a]
    shape = (nr // 2, W_FULL[a][1]) if kind == "col" else (NCHIP, nr // 2, nc)
    return [SDS(shape, F32), SDS(shape, BF16)]


def _write_halves(a, acc_ref, c, mine_ref, theirs_ref):
    kind, nr, nc = W_CUTS[a]
    hr = nr // 2
    if kind == "col":
        mine_ref[...] = acc_ref[pl.ds(pl.multiple_of(c * hr, hr), hr), :]
        theirs_ref[...] = acc_ref[pl.ds(pl.multiple_of((1 - c) * hr, hr), hr), :].astype(BF16)
    else:
        for k in range(NCHIP):
            mine_ref[k] = acc_ref[pl.ds(pl.multiple_of(k * nr + c * hr, hr), hr), :]
            theirs_ref[k] = acc_ref[pl.ds(pl.multiple_of(k * nr + (1 - c) * hr, hr), hr), :].astype(BF16)


def _out_weight_grads(a_in, da_out, s_in, ds_out, merged, dy, core, tk=512):
    t = dy.shape[0]
    nt = t // tk

    def body(c_ref, ain_ref, da_ref, sin_ref, ds_ref, m_ref, dy_ref, *rest):
        outs, (gao, gco, go) = rest[:6], rest[6:]

        @pl.when(pl.program_id(0) == 0)
        def _():
            gao[...] = jnp.zeros_like(gao)
            gco[...] = jnp.zeros_like(gco)
            go[...] = jnp.zeros_like(go)

        gao[...] += _dot_tn(ain_ref[...], da_ref[...])
        gco[...] += _dot_tn(sin_ref[...], ds_ref[...])
        go[...] += _dot_tn(m_ref[...], dy_ref[...])

        @pl.when(pl.program_id(0) == nt - 1)
        def _():
            for a, acc in ((1, gao), (2, gco), (3, go)):
                _write_halves(a, acc, c_ref[0], outs[2 * a - 2], outs[2 * a - 1])

    tok = lambda w: pl.BlockSpec((tk, w), lambda i, cr: (i, 0))
    out_shape = _halves_out(1) + _halves_out(2) + _halves_out(3)
    outs = pl.pallas_call(
        body, name="out_weight_grads", out_shape=out_shape,
        grid_spec=pltpu.PrefetchScalarGridSpec(
            num_scalar_prefetch=1, grid=(nt,), in_specs=[tok(AW), tok(D), tok(D), tok(D), tok(D), tok(D)],
            out_specs=[pl.BlockSpec(o.shape, lambda i, cr, nd=len(o.shape): (0,) * nd) for o in out_shape],
            scratch_shapes=[pltpu.VMEM((AW, D), F32), pltpu.VMEM((D, D), F32), pltpu.VMEM((D, D), F32)]),
        compiler_params=_cp(("arbitrary",), VMEM_CAP),
    )(core, a_in, da_out, s_in, ds_out, merged, dy)
    return [(outs[0], outs[1]), (outs[2], outs[3]), (outs[4], outs[5])]


def _input_grad(dq, dk, dv, dgates, w, x2, dxr, sc1p, seq, sums, tm=512):
    t = x2.shape[0]
    nt = t // tm
    spt = seq // tm
    bsz = t // seq
    n = len(sums)
    gblk = NGATE // 4
    nsteps = 3 + 4

    def body(dq_ref, dk_ref, dv_ref, dg_ref, wq_ref, wg_ref, x_ref, dxr_ref, sc_ref, *rest):
        src, (dx_ref, dsh_ref, dsc_ref), land = rest[:n], rest[n:n + 3], rest[n + 3:2 * n + 3]
        acc_ref, send_sems, recv_sems = rest[2 * n + 3:]
        j, i = pl.program_id(0), pl.program_id(1)
        copies = _scatter_copies(src, land, send_sems, recv_sems)
        rows = pl.ds(pl.multiple_of(i * tm, tm), tm)

        @pl.when((i == 0) & (j == 0))
        def _():
            for cp in copies:
                cp.start()

        for k, ref in enumerate((dq_ref, dk_ref, dv_ref)):
            @pl.when(j == k)
            def _(k=k, ref=ref):
                part = _dot_nt(ref[...], wq_ref[...])
                if k == 0:
                    acc_ref[rows, :] = part
                else:
                    acc_ref[rows, :] += part

        @pl.when((j >= 3) & (j < nsteps - 1))
        def _():
            acc_ref[rows, :] += _dot_nt(dg_ref[...], wg_ref[...])

        @pl.when(j == nsteps - 1)
        def _():
            dh = acc_ref[rows, :] + _dot_nt(dg_ref[...], wg_ref[...])
            dx_ref[...] = dh * sc_ref[0] + dxr_ref[...]

            @pl.when(i % spt == 0)
            def _():
                dsh_ref[...] = jnp.zeros_like(dsh_ref)
                dsc_ref[...] = jnp.zeros_like(dsc_ref)

            dsh_ref[0] += jnp.sum(dh, axis=0, keepdims=True)
            dsc_ref[0] += jnp.sum(dh * x_ref[...], axis=0, keepdims=True)

        @pl.when((i == nt - 1) & (j == nsteps - 1))
        def _():
            for cp in copies:
                cp.wait()

    def held(k):
        return lambda j, i: (jnp.where(j == k, i, jnp.where(j < k, 0, nt - 1)), 0)

    last = lambda j, i: (jnp.where(j == nsteps - 1, i, 0), 0)
    outs = pl.pallas_call(
        body, name="input_grad", grid=(nsteps, nt),
        out_shape=[SDS((t, D), F32), SDS((bsz, 1, D), F32), SDS((bsz, 1, D), F32)] + [SDS((3,) + s.shape[1:], BF16) for s in sums],
        in_specs=[pl.BlockSpec((tm, QW), held(0)), pl.BlockSpec((tm, QW), held(1)), pl.BlockSpec((tm, QW), held(2)),
                  pl.BlockSpec((tm, gblk), lambda j, i: (jnp.where(j >= 3, i, 0), jnp.clip(j - 3, 0, 3))),
                  pl.BlockSpec((D, QW), lambda j, i: (0, jnp.minimum(j, 2))),
                  pl.BlockSpec((pl.Element(D), pl.Element(gblk)), lambda j, i: (0, pl.multiple_of(3 * QW + gblk * jnp.clip(j - 3, 0, 3), 128))),
                  pl.BlockSpec((tm, D), last), pl.BlockSpec((tm, D), last),
                  pl.BlockSpec((1, 1, D), lambda j, i: (jnp.where(j == nsteps - 1, i // spt, 0), 0, 0))] + [ANY] * n,
        out_specs=[pl.BlockSpec((tm, D), last),
                   pl.BlockSpec((1, 1, D), lambda j, i: (jnp.where(j == nsteps - 1, i // spt, 0), 0, 0)),
                   pl.BlockSpec((1, 1, D), lambda j, i: (jnp.where(j == nsteps - 1, i // spt, 0), 0, 0))] + [ANY] * n,
        scratch_shapes=[pltpu.VMEM((t, D), F32), pltpu.SemaphoreType.DMA((3 * NCHIP,)), pltpu.SemaphoreType.DMA((3 * NCHIP,))],
        compiler_params=_cp(("arbitrary", "arbitrary"), VMEM_CAP, side=True),
    )(*map(_in_hbm, (dq, dk, dv, dgates, w, w, x2, dxr)), sc1p, *sums)
    return outs[0], outs[1], outs[2], outs[3:]


def _in_weight_grad(ht, dq, dk, dv, dgates, core, sums):
    t = ht.shape[1]
    hr = D // 2
    n = len(sums)

    def body(c_ref, ht_ref, dq_ref, dk_ref, dv_ref, dg_ref, *rest):
        src, mine_ref, got_ref, land = rest[:n], rest[n], rest[n + 1], rest[n + 2:2 * n + 2]
        acc_ref, their_buf, send_sems, recv_sems, tile_send, tile_recv = rest[2 * n + 2:]
        j = pl.program_id(0)
        slot = j % 2
        px, py, pc = _place()
        copies = _scatter_copies(src, land, send_sems, recv_sems)

        def to_sibling(step, k):
            return pltpu.make_async_remote_copy(src_ref=their_buf.at[k], dst_ref=got_ref.at[:, pl.ds(pl.multiple_of(step * TN, TN), TN)],
                                                send_sem=tile_send.at[k], recv_sem=tile_recv.at[0], device_id=(px, py, 1 - pc),
                                                device_id_type=MESH)

        @pl.when(j == 0)
        def _():
            for cp in copies:
                cp.start()

        @pl.when(j >= 2)
        def _():
            to_sibling(j - 2, slot).wait_send()

        for k, ref in enumerate((dq_ref, dk_ref, dv_ref)):
            @pl.when((j >= k * NQT) & (j < (k + 1) * NQT))
            def _(ref=ref):
                acc_ref[...] = _dot(ht_ref[...], ref[...])

        @pl.when(j >= 3 * NQT)
        def _():
            acc_ref[...] = _dot(ht_ref[...], dg_ref[...])

        _write_halves(0, acc_ref, c_ref[0], mine_ref, their_buf.at[slot])
        to_sibling(j, slot).start()

        @pl.when(j == NPT - 1)
        def _():
            to_sibling(j - 1, 1 - slot).wait_send()
            to_sibling(j, slot).wait_send()
            pltpu.make_async_remote_copy(src_ref=got_ref, dst_ref=got_ref, send_sem=tile_send.at[0], recv_sem=tile_recv.at[0],
                                         device_id=(px, py, 1 - pc), device_id_type=MESH).wait_recv()
            for cp in copies:
                cp.wait()

    def part(k):
        return pl.BlockSpec((t, TN), lambda j, cr: (0, jnp.clip(j - k * NQT, 0, NQT - 1)))

    outs = pl.pallas_call(
        body, name="in_weight_grad", out_shape=[SDS((hr, NCOL), F32), SDS((hr, NCOL), BF16)] + [SDS((3,) + v.shape[1:], BF16) for v in sums],
        grid_spec=pltpu.PrefetchScalarGridSpec(
            num_scalar_prefetch=1, grid=(NPT,),
            in_specs=[pl.BlockSpec((D, t), lambda j, cr: (0, 0)), part(0), part(1), part(2),
                      pl.BlockSpec((t, TN), lambda j, cr: (0, jnp.maximum(j - 3 * NQT, 0)))] + [ANY] * n,
            out_specs=[pl.BlockSpec((hr, TN), lambda j, cr: (0, j)), ANY] + [ANY] * n,
            scratch_shapes=[pltpu.VMEM((D, TN), F32), pltpu.VMEM((2, hr, TN), BF16),
                            pltpu.SemaphoreType.DMA((3 * NCHIP,)), pltpu.SemaphoreType.DMA((3 * NCHIP,)),
                            pltpu.SemaphoreType.DMA((2,)), pltpu.SemaphoreType.DMA((1,))]),
        compiler_params=_cp(("arbitrary",), VMEM_CAP, side=True),
    )(core, *map(_in_hbm, (ht, dq, dk, dv, dgates)), *sums)
    return outs[0], outs[1], outs[2:]


def _sum_partials(gathered):
    def body(g_ref, o_ref):
        acc = g_ref[0]
        for k in range(1, 8):
            acc = acc + g_ref[k]
        o_ref[...] = acc

    return pl.pallas_call(body, name="sum_partials", out_shape=SDS(gathered.shape[1:], F32), in_specs=[VMEM_SPEC], out_specs=VMEM_SPEC)(gathered)


def _adamw(w, g, m, v, name, tr=256):
    r, cdim = w.shape
    tr = tr if cdim <= D else tr // 2
    tr = tr if (r % tr == 0 and r > tr) else r

    def body(w_ref, g_ref, m_ref, v_ref, go_ref, d_ref, nm_ref, nv_ref):
        gv = g_ref[...]
        go_ref[...] = gv
        nm = B1 * m_ref[...] + (1.0 - B1) * gv
        nv = B2 * v_ref[...] + (1.0 - B2) * (gv * gv)
        m_hat = nm / (1.0 - B1 ** STEP)
        v_hat = nv / (1.0 - B2 ** STEP)
        d_ref[...] = -LR * (m_hat / (jnp.sqrt(v_hat) + EPS) + WD * w_ref[...])
        nm_ref[...] = nm
        nv_ref[...] = nv

    spec = pl.BlockSpec((tr, cdim), lambda i: (i, 0))
    return pl.pallas_call(
        body, name=name, grid=(r // tr,), out_shape=[SDS((r, cdim), F32)] * 4, in_specs=[spec] * 4, out_specs=[spec] * 4,
        compiler_params=_cp(("parallel",), VMEM_CAP // 2),
    )(w, g, m, v)


def _t5_bucket(dist):
    n = jnp.maximum(dist, 1).astype(F32)
    large = MAX_EXACT + (jnp.log(n / MAX_EXACT) / math.log(MAX_DISTANCE / MAX_EXACT) * (N_BUCKETS - MAX_EXACT)).astype(jnp.int32)
    large = jnp.minimum(large, N_BUCKETS - 1)
    return jnp.where(dist < MAX_EXACT, dist, large)


def _band_buckets():
    a = jnp.arange(BLK)[:, None]
    b = jnp.arange(2 * BLK)[None, :]
    steps = jnp.maximum(a + BLK - b, 0)
    return jnp.stack([_t5_bucket(steps * d) for d in DILATIONS]).astype(jnp.int32)


def _pad_rows(a, rows=8):
    return jnp.pad(a, ((0, rows - a.shape[0]), (0, 0)))


def kernel(x, c, w_ada, b_ada, w_in, conv_w, conv_b, rel_bias, w_attn_out, w_conv_out, w_o, ln_g, ln_b, loss_target, m_w_ada, m_b_ada, m_w_in, m_conv_w, m_conv_b, m_rel_bias, m_w_attn_out, m_w_conv_out, m_w_o, m_ln_g, m_ln_b, v_w_ada, v_b_ada, v_w_in, v_conv_w, v_conv_b, v_rel_bias, v_w_attn_out, v_w_conv_out, v_w_o, v_ln_g, v_ln_b):
    bsz, seq, _ = x.shape
    t = bsz * seq
    mx, my, mc = _place()
    chip = 2 * mx + my
    dev = 4 * mx + 2 * my + mc
    x2 = x.reshape(t, D)
    tgt = loss_target.reshape(t, D)

    mine = _to_bf16_windows([w[0] for w in (w_in, w_attn_out, w_conv_out, w_o)])

    n_ada = w_ada.shape[2]
    n_cw = conv_w.shape[2]
    c_and_cw = jnp.concatenate([_pad_rows(c), jnp.pad(conv_w[0], ((0, 5), (0, D - n_cw)))], axis=0)
    firsts = _all_gather8(c_and_cw, "gather_c_conv_w")
    c_all = firsts[:, 0:bsz, :].reshape(8 * bsz, D)
    conv_w_f = firsts[0::2, 8:11, 0:n_cw].transpose(1, 0, 2).reshape(3, D)
    b_cols = lax.dynamic_slice(b_ada, (0, chip * n_ada), (1, n_ada))
    mod_part = _ada_forward(c_all, w_ada[0], b_cols)
    mod_parts = _all_gather8(mod_part, "gather_mod")
    mod_all = mod_parts[0::2].transpose(1, 0, 2).reshape(8 * bsz, 3 * D)
    mod = lax.dynamic_slice(mod_all, (dev * bsz, 0), (bsz, 3 * D))
    shift = mod[:, 0:D].reshape(bsz, 1, D)
    sc1p = 1.0 + mod[:, D:2 * D].reshape(bsz, 1, D)
    gate = mod[:, 2 * D:].reshape(bsz, 1, D)

    h, ht = _modulate(x2, sc1p, shift, seq)
    tab = lax.dynamic_index_in_dim(jnp.asarray(_tile_tables()), chip, 0, keepdims=False)
    qkv, gates, (w_in_f, w_ao_f, w_co_f, w_o_f) = _project_gather(h, mine, tab)
    buckets = _band_buckets()
    bias = _bias_tables(rel_bias, buckets)
    og, lg = [], []
    for g in range(3):
        o_g, l_g = _attn_forward(g, qkv, bias[g], bsz, seq)
        og.append(o_g)
        lg.append(l_g)
    (a_in, s_in, merged, dy, a_out, s_out, y_conv, o, lj, dxr, vec_f, dgate) = _mix_forward(
        gates, og, lg, x2, tgt, gate, w_ao_f, w_co_f, w_o_f, conv_w_f, conv_b, ln_g, ln_b, bsz, seq)

    dgates, do, dl, da_out, ds_out, vec = _mix_backward(gates, dy, a_out, s_out, y_conv, o, lj, w_ao_f, w_co_f, w_o_f, conv_w_f, vec_f, bsz, seq)
    core = jnp.reshape(mc, (1,)).astype(jnp.int32)
    small_grads = _out_weight_grads(a_in, da_out, s_in, ds_out, merged, dy, core)
    got_small = _swap_halves([theirs for _, theirs in small_grads], "swap_small_grad_halves")
    sums_small = _chip_sums([own for own, _ in small_grads], got_small, 1, "chip_sums_small")
    dqkv, dbs = None, []
    for g in range(3):
        dqkv, db = _attn_backward(g, qkv, do, dl, bias[g], dqkv, bsz, seq)
        dbs.append(db)
    dq, dk, dv = dqkv
    drb = _bias_grad(jnp.stack(dbs), buckets)
    drb = drb[:, :, 0:4].transpose(1, 0, 2).reshape(N_BUCKETS, 12)
    g_in_mine, got_in, landed_small = _in_weight_grad(ht, dq, dk, dv, dgates, core, [bf for _, bf in sums_small])
    sums_in = _chip_sums([g_in_mine], [got_in], 0, "chip_sums_in")
    grad_x, dshift, dscale, landed_in = _input_grad(dq, dk, dv, dgates, w_in_f, x2, dxr, sc1p, seq, [bf for _, bf in sums_in])
    halves = _reduce_mine([own for own, _ in sums_in + sums_small], list(landed_in) + list(landed_small))
    gw_in, gw_ao, gw_co, gw_o = _join_halves(halves)

    dmod = jnp.concatenate([dshift, dscale, dgate], axis=2).reshape(bsz * 3, D)
    drb_row = jnp.pad(drb.reshape(1, N_BUCKETS * 12), ((0, 0), (0, D - N_BUCKETS * 12)))
    vec = lax.dynamic_update_slice(vec, drb_row, (7, 0))
    packed = jnp.concatenate([vec, _pad_rows(dmod)], axis=0)
    gathered = _all_gather8(packed, "gather_small")
    small = _sum_partials(gathered)
    g_ln_g, g_ln_b, loss_lanes = small[0:1], small[1:2], small[2:3]
    g_conv_w_full, g_conv_b = small[3:6], small[6:7]
    g_rel_bias = small[7, 0:N_BUCKETS * 12].reshape(N_BUCKETS, 12)
    loss = 0.5 / D * jnp.sum(loss_lanes)
    dmod_all = gathered[:, 8:8 + 3 * bsz, :].reshape(8 * bsz, 3 * D)
    dmod_cols = lax.dynamic_slice(dmod_all, (0, chip * n_ada), (8 * bsz, n_ada))
    gw_ada, gb_ada = _ada_backward(c_all, dmod_cols, dmod_all)
    g_conv_w = lax.dynamic_slice(g_conv_w_full, (0, chip * n_cw), (3, n_cw))

    names = ["w_ada", "b_ada", "w_in", "conv_w", "conv_b", "rel_bias", "w_attn_out", "w_conv_out", "w_o", "ln_g", "ln_b"]
    two_d = lambda a: a.reshape(a.shape[-2:]) if a.ndim == 3 else a
    weights = dict(zip(names, map(two_d, (w_ada, b_ada, w_in, conv_w, conv_b, rel_bias, w_attn_out, w_conv_out, w_o, ln_g, ln_b))))
    ms = dict(zip(names, map(two_d, (m_w_ada, m_b_ada, m_w_in, m_conv_w, m_conv_b, m_rel_bias, m_w_attn_out, m_w_conv_out, m_w_o, m_ln_g, m_ln_b))))
    vs = dict(zip(names, map(two_d, (v_w_ada, v_b_ada, v_w_in, v_conv_w, v_conv_b, v_rel_bias, v_w_attn_out, v_w_conv_out, v_w_o, v_ln_g, v_ln_b))))
    grads = dict(zip(names, (gw_ada, gb_ada, gw_in, g_conv_w, g_conv_b, g_rel_bias, gw_ao, gw_co, gw_o, g_ln_g, g_ln_b)))
    shapes = dict(zip(names, (w_ada, b_ada, w_in, conv_w, conv_b, rel_bias, w_attn_out, w_conv_out, w_o, ln_g, ln_b)))
    grad_out, deltas, new_m, new_v = {}, {}, {}, {}
    for n in names:
        grad_out[n], deltas[n], new_m[n], new_v[n] = _adamw(weights[n], grads[n], ms[n], vs[n], f"adamw_{n}")
    shaped = lambda d: [d[n].reshape(shapes[n].shape) for n in names]
    return (loss, grad_x.reshape(bsz, seq, D), *shaped(grad_out), *shaped(deltas), *shaped(new_m), *shaped(new_v))
```

```python
import math

import numpy as np
import jax
import jax.numpy as jnp
from jax import lax
from jax.experimental import pallas as pl
from jax.experimental.pallas import tpu as pltpu

F32 = jnp.float32
BF16 = jnp.bfloat16
SDS = jax.ShapeDtypeStruct
MESH = pl.DeviceIdType.MESH
HBM_OUT = pltpu.HBM
ANY = pl.BlockSpec(memory_space=pl.ANY)
VMEM_SPEC = pl.BlockSpec(memory_space=pltpu.VMEM)

D = 1024
HD = 128
BLK = 128
QW = 1536
AW = 512
NGATE = 6656
GATE_COLS = ((0, 512), (512, 1536), (1536, 2560), (2560, 3584), (3584, 4608), (4608, 5632), (5632, 6656))
NCOL = 3 * QW + NGATE
TN = 512
NQT = QW // TN
NPT = NCOL // TN
DILATIONS = (1, 4, 16)
N_BUCKETS, MAX_EXACT, MAX_DISTANCE = 32, 16, 2048
ALPHA = 2.0 ** 0.25
LN_EPS = 1e-5
NEG = -1e30
SCALE = HD ** -0.5
LR, B1, B2, EPS, WD, STEP = 0.001, 0.9, 0.999, 1e-08, 0.01, 10
NCHIP = 4
VMEM_CAP = 60 * 2 ** 20


def _cp(sem=None, vmem=None, side=False):
    return pltpu.CompilerParams(dimension_semantics=sem, vmem_limit_bytes=vmem, has_side_effects=side)


def _dot(a, b):
    return jnp.dot(a, b, preferred_element_type=F32)


def _dot_nt(a, b):
    return lax.dot_general(a, b, (((1,), (1,)), ((), ())), preferred_element_type=F32)


def _dot_tn(a, b):
    return lax.dot_general(a, b, (((0,), (0,)), ((), ())), preferred_element_type=F32)


def _sig(x):
    return 1.0 / (1.0 + jnp.exp(-x))


def _in_hbm(a):
    return pltpu.with_memory_space_constraint(a, pltpu.HBM)


def _place():
    x, y, c = lax.axis_index("x"), lax.axis_index("y"), lax.axis_index("c")
    return x, y, c


def _all_gather8(v, name):
    r, cdim = v.shape

    def body(v_ref, out_ref, send_sems, recv_sems, local_sem):
        x, y, c = _place()
        me = 4 * x + 2 * y + c
        peers = [(x, y, 1 - c), (1 - x, y, c), (x, 1 - y, c), (1 - x, 1 - y, c),
                 (1 - x, y, 1 - c), (x, 1 - y, 1 - c), (1 - x, 1 - y, 1 - c)]
        mine = pltpu.make_async_copy(v_ref, out_ref.at[me], local_sem)
        mine.start()

        def copy(k, block, to):
            return pltpu.make_async_remote_copy(src_ref=v_ref, dst_ref=out_ref.at[block], send_sem=send_sems.at[k],
                                                recv_sem=recv_sems.at[k], device_id=to, device_id_type=MESH)

        sends = [copy(k, me, p) for k, p in enumerate(peers)]
        for cp in sends:
            cp.start()
        for k, (px, py, pc) in enumerate(peers):
            copy(k, 4 * px + 2 * py + pc, (px, py, pc)).wait_recv()
        for cp in sends:
            cp.wait_send()
        mine.wait()

    return pl.pallas_call(
        body, name=name, out_shape=SDS((8, r, cdim), v.dtype), in_specs=[VMEM_SPEC], out_specs=VMEM_SPEC,
        scratch_shapes=[pltpu.SemaphoreType.DMA((7,)), pltpu.SemaphoreType.DMA((7,)), pltpu.SemaphoreType.DMA(())],
        compiler_params=_cp(side=True),
    )(v)


W_CUTS = (("col", D, NCOL // NCHIP), ("col", AW, D // NCHIP), ("row", D // NCHIP, D), ("row", D // NCHIP, D))
W_FULL = ((D, NCOL), (AW, D), (D, D), (D, D))


def _shard_window(ref, cut, k, half):
    kind, nr, nc = cut
    hr = nr // 2
    if kind == "col":
        rows = pl.ds(0, nr) if half is None else pl.ds(pl.multiple_of(half * hr, 16), hr)
        return ref.at[rows, pl.ds(pl.multiple_of(k * nc, 128), nc)]
    if half is None:
        return ref.at[pl.ds(pl.multiple_of(k * nr, 16), nr), :]
    return ref.at[pl.ds(pl.multiple_of(k * nr + half * hr, 16), hr), :]


def _half_rows(ref, cut, half):
    hr = cut[1] // 2
    return ref.at[pl.ds(pl.multiple_of(half * hr, 16), hr), :]


def _to_bf16_windows(ws):
    x, y, _ = _place()
    chip = jnp.reshape(2 * x + y, (1,)).astype(jnp.int32)
    tr = 256
    n = len(ws)

    def body(c_ref, *refs):
        src, dst = refs[:n], refs[n:]
        dst[0][...] = src[0][...].astype(BF16)

        @pl.when(pl.program_id(0) == 0)
        def _():
            for a in range(1, n):
                dst[a][...] = src[a][...].astype(BF16)

    in_specs = [pl.BlockSpec((tr, W_CUTS[0][2]), lambda i, cr: (i, 0))]
    out_specs = [pl.BlockSpec((tr, W_CUTS[0][2]), lambda i, cr: (i, cr[0]))]
    for a in range(1, n):
        kind, nr, nc = W_CUTS[a]
        in_specs.append(pl.BlockSpec((nr, nc), lambda i, cr: (0, 0)))
        out_specs.append(pl.BlockSpec((nr, nc), (lambda i, cr: (0, cr[0])) if kind == "col" else (lambda i, cr: (cr[0], 0))))
    return pl.pallas_call(
        body, name="to_bf16", out_shape=[SDS(W_FULL[a], BF16) for a in range(n)],
        grid_spec=pltpu.PrefetchScalarGridSpec(num_scalar_prefetch=1, grid=(D // tr,), in_specs=in_specs, out_specs=out_specs),
        compiler_params=_cp(("arbitrary",)),
    )(chip, *ws)


def _swap_halves(theirs, name):
    n = len(theirs)

    def body(*refs):
        src, land = refs[:n], refs[n:2 * n]
        send_sems, recv_sems = refs[2 * n:]
        x, y, c = _place()
        copies = [pltpu.make_async_remote_copy(src_ref=src[a], dst_ref=land[a], send_sem=send_sems.at[a], recv_sem=recv_sems.at[a],
                                               device_id=(x, y, 1 - c), device_id_type=MESH) for a in range(n)]
        for cp in copies:
            cp.start()
        for cp in copies:
            cp.wait()

    return pl.pallas_call(
        body, name=name, out_shape=[SDS(v.shape, v.dtype) for v in theirs], in_specs=[ANY] * n, out_specs=[ANY] * n,
        scratch_shapes=[pltpu.SemaphoreType.DMA((n,)), pltpu.SemaphoreType.DMA((n,))],
        compiler_params=_cp(side=True),
    )(*theirs)


def _chip_sums(mines, gots, first, name):
    n = len(mines)
    x, y, _ = _place()
    me = jnp.reshape(2 * x + y, (1,)).astype(jnp.int32)

    def body(me_ref, *refs):
        ins, outs = refs[:2 * n], refs[2 * n:]
        for a in range(n):
            hr, nc = W_CUTS[first + a][1] // 2, W_CUTS[first + a][2]
            s = (ins[2 * a][...] + ins[2 * a + 1][...].astype(F32)).reshape(hr, nc)
            outs[2 * a + 1][0] = s.astype(BF16)

            @pl.when(pl.program_id(0) == me_ref[0])
            def _(a=a, s=s):
                outs[2 * a][...] = s

    in_specs, out_specs, out_shape = [], [], []
    for a in range(n):
        kind, nr, nc = W_CUTS[first + a]
        hr = nr // 2
        spec = pl.BlockSpec((hr, nc), lambda k, mr: (0, k)) if kind == "col" else pl.BlockSpec((1, hr, nc), lambda k, mr: (k, 0, 0))
        in_specs += [spec, spec]
        out_specs += [pl.BlockSpec((hr, nc), lambda k, mr: (0, 0)), pl.BlockSpec((1, hr, nc), lambda k, mr: (k, 0, 0))]
        out_shape += [SDS((hr, nc), F32), SDS((NCHIP, hr, nc), BF16)]
    outs = pl.pallas_call(
        body, name=name, out_shape=out_shape,
        grid_spec=pltpu.PrefetchScalarGridSpec(num_scalar_prefetch=1, grid=(NCHIP,), in_specs=in_specs, out_specs=out_specs),
        compiler_params=_cp(("arbitrary",), VMEM_CAP),
    )(me, *[v for pair in zip(mines, gots) for v in pair])
    return [(outs[2 * a], outs[2 * a + 1]) for a in range(n)]


def _reduce_mine(mines, gots):
    n = len(mines)
    _, _, c = _place()
    core = jnp.reshape(c, (1,)).astype(jnp.int32)
    tr = 256
    nsteps = W_CUTS[0][1] // 2 // tr

    def body(c_ref, *refs):
        ins, outs = refs[:2 * n], refs[2 * n:]

        def add(a):
            m_ref, g_ref = ins[2 * a], ins[2 * a + 1]
            outs[a][...] = ((m_ref[...] + g_ref[0].astype(F32)) + g_ref[1].astype(F32)) + g_ref[2].astype(F32)

        add(0)

        @pl.when(pl.program_id(0) == 0)
        def _():
            for a in range(1, n):
                add(a)

    nc0 = W_CUTS[0][2]
    in_specs = [pl.BlockSpec((tr, nc0), lambda i, cr: (i, 0)), pl.BlockSpec((3, tr, nc0), lambda i, cr: (0, i, 0))]
    out_specs = [pl.BlockSpec((tr, nc0), lambda i, cr: (cr[0] * nsteps + i, 0))]
    for a in range(1, n):
        hr, nc = W_CUTS[a][1] // 2, W_CUTS[a][2]
        in_specs += [pl.BlockSpec((hr, nc), lambda i, cr: (0, 0)), pl.BlockSpec((3, hr, nc), lambda i, cr: (0, 0, 0))]
        out_specs.append(pl.BlockSpec((hr, nc), lambda i, cr: (cr[0], 0)))
    return pl.pallas_call(
        body, name="reduce_mine", out_shape=[SDS((W_CUTS[a][1], W_CUTS[a][2]), F32) for a in range(n)],
        grid_spec=pltpu.PrefetchScalarGridSpec(num_scalar_prefetch=1, grid=(nsteps,), in_specs=in_specs, out_specs=out_specs),
        compiler_params=_cp(("arbitrary",), VMEM_CAP),
    )(core, *[v for pair in zip(mines, gots) for v in pair])


def _join_halves(fulls, packed):
    n = len(fulls)
    r, cdim = packed.shape

    def body(*refs):
        v_ref, full, out_ref = refs[n], refs[n + 1:2 * n + 1], refs[2 * n + 1]
        send_sems, recv_sems, g_send, g_recv, local_sem = refs[2 * n + 2:]
        x, y, c = _place()
        sibling = (x, y, 1 - c)
        me = 4 * x + 2 * y + c
        peers = [(x, y, 1 - c), (1 - x, y, c), (x, 1 - y, c), (1 - x, 1 - y, c),
                 (1 - x, y, 1 - c), (x, 1 - y, 1 - c), (1 - x, 1 - y, 1 - c)]
        mine = pltpu.make_async_copy(v_ref, out_ref.at[me], local_sem)
        mine.start()

        def copy(k, block, to):
            return pltpu.make_async_remote_copy(src_ref=v_ref, dst_ref=out_ref.at[block], send_sem=g_send.at[k],
                                                recv_sem=g_recv.at[k], device_id=to, device_id_type=MESH)

        def swap(a, half):
            rows = _half_rows(full[a], W_CUTS[a], half)
            return pltpu.make_async_remote_copy(src_ref=rows, dst_ref=rows, send_sem=send_sems.at[a], recv_sem=recv_sems.at[a],
                                                device_id=sibling, device_id_type=MESH)

        gathers = [copy(k, me, p) for k, p in enumerate(peers)]
        for cp in gathers:
            cp.start()
        sends = [swap(a, c) for a in range(n)]
        for cp in sends:
            cp.start()
        for a, cp in enumerate(sends):
            cp.wait_send()
            swap(a, 1 - c).wait_recv()
        for k, (px, py, pc) in enumerate(peers):
            copy(k, 4 * px + 2 * py + pc, (px, py, pc)).wait_recv()
        for cp in gathers:
            cp.wait_send()
        mine.wait()

    outs = pl.pallas_call(
        body, name="join_grad_halves", out_shape=[SDS((W_CUTS[a][1], W_CUTS[a][2]), F32) for a in range(n)] + [SDS((8, r, cdim), F32)],
        in_specs=[ANY] * (n + 1), out_specs=[ANY] * (n + 1),
        scratch_shapes=[pltpu.SemaphoreType.DMA((n,)), pltpu.SemaphoreType.DMA((n,)), pltpu.SemaphoreType.DMA((7,)),
                        pltpu.SemaphoreType.DMA((7,)), pltpu.SemaphoreType.DMA(())],
        input_output_aliases={a: a for a in range(n)}, compiler_params=_cp(side=True),
    )(*fulls, packed)
    return outs[:n], outs[n]


def _ada_forward(c_all, w_ada, b_cols):
    nb, nc = c_all.shape[0], w_ada.shape[1]

    def body(c_ref, w_ref, b_ref, o_ref):
        cv = c_ref[...]
        sc = (cv * _sig(cv)).astype(BF16)
        o_ref[...] = _dot(sc, w_ref[...].astype(BF16)) + b_ref[...]

    return pl.pallas_call(body, name="ada_forward", out_shape=SDS((nb, nc), F32), compiler_params=_cp(vmem=VMEM_CAP // 2))(c_all, w_ada, b_cols)


def _ada_backward(c_all, dmod_cols, dmod_all):
    nb, nc = dmod_cols.shape

    def body(c_ref, d_ref, a_ref, gw_ref, gb_ref):
        cv = c_ref[...]
        sc = (cv * _sig(cv)).astype(BF16)
        gw_ref[...] = _dot_tn(sc, d_ref[...].astype(BF16))
        gb_ref[...] = jnp.sum(a_ref[...], axis=0, keepdims=True)

    return pl.pallas_call(body, name="ada_backward", out_shape=[SDS((D, nc), F32), SDS((1, dmod_all.shape[1]), F32)],
                          compiler_params=_cp(vmem=VMEM_CAP // 2))(c_all, dmod_cols, dmod_all)


def _modulate(x2, sc1p, shift, seq, tm=512):
    t = x2.shape[0]
    spt = seq // tm

    def body(x_ref, sc_ref, sh_ref, h_ref, ht_ref):
        h = x_ref[...] * sc_ref[0] + sh_ref[0]
        h_ref[...] = h.astype(BF16)
        ht_ref[...] = h.T.astype(BF16)

    per_seq = pl.BlockSpec((1, 1, D), lambda i: (i // spt, 0, 0))
    return pl.pallas_call(
        body, name="modulate", out_shape=[HBM_OUT((t, D), BF16), HBM_OUT((D, t), BF16)], grid=(t // tm,),
        in_specs=[pl.BlockSpec((tm, D), lambda i: (i, 0)), per_seq, per_seq],
        out_specs=[pl.BlockSpec((tm, D), lambda i: (i, 0)), pl.BlockSpec((D, tm), lambda i: (0, i))],
        compiler_params=_cp(("parallel",)),
    )(_in_hbm(x2), sc1p, shift)


TW = 256
TPS = NCOL // NCHIP // TW
NT = NCOL // TW
NQKV_T = 3 * QW // TW
N_TILE_SEMS = 2 * 3 * TPS


def _tile_tables():
    tabs = np.zeros((NCHIP, 3, NT), np.int32)
    for me in range(NCHIP):
        tiles = [TPS * (me ^ (s // TPS)) + s % TPS for s in range(NT)]
        tabs[me, 0] = tiles
        for row, (lo, hi) in enumerate(((0, NQKV_T), (NQKV_T, NT))):
            mine = [w - lo if lo <= w < hi else None for w in tiles]
            held = next(m for m in mine if m is not None)
            for s, m in enumerate(mine):
                held = held if m is None else m
                tabs[me, 1 + row, s] = held
    return tabs


def _project_gather(h, fulls, tab):
    t = h.shape[0]
    n = len(fulls)

    def body(tab_ref, h_ref, *rest):
        qkv_ref, g_ref = rest[n], rest[n + 1]
        full = rest[n + 2:2 * n + 2]
        w_buf, tile_sems, send_sems, recv_sems = rest[2 * n + 2:]
        s = pl.program_id(0)
        x, y, c = _place()
        me = 2 * x + y
        peers = [(x, 1 - y), (1 - x, y), (1 - x, 1 - y)]
        sibling = (x, y, 1 - c)

        def hop(a, r, stage, chip, half, to):
            window = _shard_window(full[a], W_CUTS[a], chip, half)
            k = N_TILE_SEMS + 6 * (a - 1) + 2 * r + stage
            return pltpu.make_async_remote_copy(src_ref=window, dst_ref=window, send_sem=send_sems.at[k], recv_sem=recv_sems.at[k],
                                                device_id=to, device_id_type=MESH)

        def tile_hop(q, stage, col_step, half, to):
            col = pl.multiple_of(tab_ref[0, col_step] * TW, TW)
            window = full[0].at[pl.ds(pl.multiple_of(half * (D // 2), 16), D // 2), pl.ds(col, TW)]
            k = 2 * (q - TPS) + stage
            return pltpu.make_async_remote_copy(src_ref=window, dst_ref=window, send_sem=send_sems.at[k], recv_sem=recv_sems.at[k],
                                                device_id=to, device_id_type=MESH)

        def send_tile(r, j):
            return tile_hop(TPS * (r + 1) + j, 0, j, c, (*peers[r], c))

        def pass_on(q, to):
            return tile_hop(3 * TPS + q % TPS, 0, q, c, to)

        def arrive(a, r):
            px, py = peers[r]
            chip = 2 * px + py
            hop(a, r, 0, chip, c, (px, py, c)).wait_recv()
            hop(a, r, 1, chip, c, sibling).start()
            hop(a, r, 1, chip, 1 - c, sibling).wait_recv()

        def tile(step, slot):
            col = pl.multiple_of(tab_ref[0, step] * TW, TW)
            return pltpu.make_async_copy(full[0].at[:, pl.ds(col, TW)], w_buf.at[slot], tile_sems.at[slot])

        @pl.when(s == 0)
        def _():
            for r in range(2):
                for j in range(TPS):
                    send_tile(r, j).start()
            tile(0, 0).start()

        @pl.when((s + 1 >= TPS) & (s + 1 < NT))
        def _():
            tile_hop(s + 1, 1, s + 1, 1 - c, sibling).wait_recv()

        @pl.when(s + 1 < NT)
        def _():
            tile(s + 1, 1 - (s % 2)).start()

        @pl.when((s + 2 >= TPS) & (s + 2 < NT))
        def _():
            tile_hop(s + 2, 0, s + 2, c, sibling).wait_recv()
            tile_hop(s + 2, 1, s + 2, c, sibling).start()

        for r in range(2):
            @pl.when(((s + 2) // TPS == r + 1) & ((s + 2) % 2 == (r + 1 + TPS * (r + 1)) % 2))
            def _(r=r):
                pass_on(s + 2, (*peers[1 - r], c)).start()

        @pl.when(s + 2 == 2 * TPS - 1)
        def _():
            for a in range(1, n):
                for r in range(3):
                    hop(a, r, 0, me, c, (*peers[r], c)).start()

        slot = s % 2
        tile(s, slot).wait()
        is_qkv = tab_ref[0, s] < NQKV_T
        for k in range(2):
            @pl.when(slot == k)
            def _(k=k):
                acc = _dot(h_ref[...], w_buf[k])

                @pl.when(is_qkv)
                def _():
                    qkv_ref[...] = acc.astype(BF16)

                @pl.when(jnp.logical_not(is_qkv))
                def _():
                    g_ref[...] = acc.astype(BF16)

        @pl.when(s == NT - 1)
        def _():
            for a in range(1, n):
                for r in range(3):
                    arrive(a, r)
            for r in range(3):
                for j in range(TPS):
                    send_tile(r, j).wait_send()
                    tile_hop(TPS * (r + 1) + j, 1, TPS * (r + 1) + j, c, sibling).wait_send()
                for a in range(1, n):
                    hop(a, r, 0, me, c, (*peers[r], c)).wait_send()
                    px, py = peers[r]
                    hop(a, r, 1, 2 * px + py, c, sibling).wait_send()

    n_sems = N_TILE_SEMS + 6 * (n - 1)
    outs = pl.pallas_call(
        body, name="project_gather", out_shape=[HBM_OUT((t, 3 * QW), BF16), HBM_OUT((t, NGATE), BF16)] + [SDS(s, BF16) for s in W_FULL],
        grid_spec=pltpu.PrefetchScalarGridSpec(
            num_scalar_prefetch=1, grid=(NT,),
            in_specs=[pl.BlockSpec((t, D), lambda s, tab: (0, 0))] + [ANY] * n,
            out_specs=[pl.BlockSpec((t, TW), lambda s, tab: (0, tab[1, s])), pl.BlockSpec((t, TW), lambda s, tab: (0, tab[2, s]))] + [ANY] * n,
            scratch_shapes=[pltpu.VMEM((2, D, TW), BF16), pltpu.SemaphoreType.DMA((2,)),
                            pltpu.SemaphoreType.DMA((n_sems,)), pltpu.SemaphoreType.DMA((n_sems,))]),
        input_output_aliases={2 + a: 2 + a for a in range(n)},
        compiler_params=_cp(("arbitrary",), VMEM_CAP, side=True),
    )(tab, _in_hbm(h), *fulls)
    return outs[0], outs[1], outs[2:]


def _bias_tables(rel_bias, buckets):
    def body(tab_ref, bk_ref, o_ref):
        a = lax.broadcasted_iota(jnp.int32, (BLK, 2 * BLK), 0)
        b = lax.broadcasted_iota(jnp.int32, (BLK, 2 * BLK), 1)
        steps = a + BLK - b
        valid = (steps >= 0) & (steps <= BLK)
        for g in range(3):
            bk = bk_ref[g]
            for j in range(4):
                def pick(kk, acc, bk=bk, col=4 * g + j):
                    return jnp.where(bk == kk, tab_ref[kk, col], acc)

                acc = lax.fori_loop(0, N_BUCKETS, pick, jnp.zeros((BLK, 2 * BLK), F32))
                o_ref[g, j] = jnp.where(valid, acc, NEG)

    return pl.pallas_call(
        body, name="bias_tables", out_shape=SDS((3, 4, BLK, 2 * BLK), F32),
        in_specs=[pl.BlockSpec(memory_space=pltpu.SMEM), VMEM_SPEC], out_specs=VMEM_SPEC,
    )(rel_bias, buckets)


def _bias_grad(ds_sum, buckets):
    def body(ds_ref, bk_ref, o_ref, part_ref):
        lane = lax.broadcasted_iota(jnp.int32, (N_BUCKETS, 128), 1)
        for g in range(3):
            def bucket(kk, carry, g=g):
                mine = bk_ref[g] == kk
                for j in range(4):
                    v = jnp.sum(jnp.where(mine, ds_ref[g, j], 0.0).reshape(BLK // 8, 8, 2 * BLK), axis=0)
                    part_ref[j, pl.ds(pl.multiple_of(kk * 8, 8), 8), :] = v[:, :BLK] + v[:, BLK:]
                return carry

            lax.fori_loop(0, N_BUCKETS, bucket, 0)
            out = jnp.zeros((N_BUCKETS, 128), F32)
            for j in range(4):
                rows = jnp.sum(part_ref[j], axis=1, keepdims=True)
                out = jnp.where(lane == j, jnp.sum(rows.reshape(N_BUCKETS, 8, 1), axis=1), out)
            o_ref[g] = out

    return pl.pallas_call(body, name="bias_grad", out_shape=SDS((3, N_BUCKETS, 128), F32), in_specs=[VMEM_SPEC, VMEM_SPEC],
                          out_specs=VMEM_SPEC, scratch_shapes=[pltpu.VMEM((4, N_BUCKETS * 8, 128), F32)])(ds_sum, buckets)


def _sub_rows(d, r, first, size):
    return pl.ds(first * d + r, size) if d == 1 else pl.ds(first * d + r, size, stride=d)


def _head_spec(seq, g, part):
    return pl.BlockSpec((seq, HD), lambda b, hh: (b, part * (QW // HD) + 4 * g + hh))


def _rows(start, count, stride):
    return pl.ds(start, count) if stride == 1 else pl.ds(start, count, stride=stride)


def _gather_rows(dst, dst0, src, src0, stride, count):
    for first in range(0, count, BLK):
        dst[pl.ds(dst0 + first, BLK), :] = src[_rows(src0 + first * stride, BLK, stride), :].astype(dst.dtype)


def _scatter_rows(dst, dst0, stride, src, src0, count):
    for first in range(0, count, BLK):
        dst[_rows(dst0 + first * stride, BLK, stride), :] = src[pl.ds(src0 + first, BLK), :].astype(dst.dtype)


def _by_subsequence(dst, src, d, wide=None, tmp=None):
    seq = src.shape[0]
    ln = seq // d
    if wide is not None:
        wide[...] = src[...].astype(F32)
        src = wide
    if d <= 4:
        for r in range(d):
            _gather_rows(dst, r * ln, src, r, d, ln)
    else:
        quarter = seq // 4
        for r4 in range(4):
            _gather_rows(tmp, r4 * quarter, src, r4, 4, quarter)
        for r4 in range(4):
            for a in range(d // 4):
                _gather_rows(dst, (4 * a + r4) * ln, tmp, r4 * quarter + a, d // 4, ln)


def _to_sequence(dst, src, d, tmp=None):
    seq = dst.shape[0]
    ln = seq // d
    if d <= 4:
        for r in range(d):
            _scatter_rows(dst, r, d, src, r * ln, ln)
    else:
        quarter = seq // 4
        for r4 in range(4):
            for a in range(d // 4):
                _scatter_rows(tmp, r4 * quarter + a, d // 4, src, (4 * a + r4) * ln, ln)
        for r4 in range(4):
            _scatter_rows(dst, r4, 4, tmp, r4 * quarter, quarter)


def _attn_forward(g, qkv, bias, bsz, seq):
    d = DILATIONS[g]
    ln = seq // d
    units = [(r, n) for r in range(d) for n in range(ln // BLK)]

    def band(n):
        return slice(BLK, 2 * BLK) if n == 0 else slice(0, 2 * BLK)

    def body(q_ref, k_ref, v_ref, b_ref, o_ref, l_ref, *scratch):
        hs = pl.program_id(1)
        s_scr, p_scr = scratch[:2]
        if d == 1:
            qd, kd, vd = q_ref, k_ref, v_ref
        else:
            wide, tmp, qd, kd, vd = scratch[2:7]
            for dst, src in ((qd, q_ref), (kd, k_ref), (vd, v_ref)):
                _by_subsequence(dst, src, d, wide, tmp)
        blk = lambda r, n: pl.ds(r * ln + n * BLK, BLK)
        direct = d <= 4
        out_rows = (lambda r, n: _sub_rows(d, r, n * BLK, BLK)) if direct else blk
        o_dst, l_dst = (o_ref, l_ref) if direct else scratch[7:9]
        for u, (r, n) in enumerate(units):
            s_scr[u, :, BLK:] = _dot_nt(qd[blk(r, n), :], kd[blk(r, n), :])
            if n > 0:
                s_scr[u, :, :BLK] = _dot_nt(qd[blk(r, n), :], kd[blk(r, n - 1), :])
        for u, (r, n) in enumerate(units):
            s = s_scr[u, :, band(n)] * SCALE + b_ref[hs, :, band(n)]
            m = jnp.max(s, axis=1, keepdims=True)
            e = jnp.exp(s - m)
            den = jnp.sum(e, axis=1, keepdims=True)
            p_scr[u, :, band(n)] = (e * (1.0 / den)).astype(BF16)
            l_dst[out_rows(r, n), :] = jnp.broadcast_to(m + jnp.log(den), (BLK, HD))
        for u, (r, n) in enumerate(units):
            acc = _dot(p_scr[u, :, BLK:], vd[blk(r, n), :])
            if n > 0:
                acc = acc + _dot(p_scr[u, :, :BLK], vd[blk(r, n - 1), :])
            o_dst[out_rows(r, n), :] = acc
        if not direct:
            _to_sequence(o_ref, o_dst, d, tmp)
            _to_sequence(l_ref, l_dst, d, tmp)

    rows_f32, rows_bf16 = pltpu.VMEM((seq, HD), F32), pltpu.VMEM((seq, HD), BF16)
    regrouped = [] if d == 1 else [rows_f32] * 2 + [rows_bf16] * 3 + ([] if d <= 4 else [rows_f32] * 2)
    out_spec = pl.BlockSpec((seq, HD), lambda b, hh: (b, hh))
    return pl.pallas_call(
        body, name=f"attn_forward_{g}", out_shape=[HBM_OUT((bsz * seq, AW), F32)] * 2, grid=(bsz, 4),
        in_specs=[_head_spec(seq, g, part) for part in range(3)] + [pl.BlockSpec((4, BLK, 2 * BLK), lambda b, hh: (0, 0, 0))],
        out_specs=[out_spec, out_spec],
        scratch_shapes=[pltpu.VMEM((len(units), BLK, 2 * BLK), F32), pltpu.VMEM((len(units), BLK, 2 * BLK), BF16)] + regrouped,
        compiler_params=_cp(("parallel", "parallel"), VMEM_CAP // 2),
    )(qkv, qkv, qkv, _in_hbm(bias))


def _attn_backward(g, qkv, do, dl, bias, prev_out, bsz, seq):
    d = DILATIONS[g]
    ln = seq // d
    units = [(r, n) for r in range(d) for n in range(ln // BLK)]

    def body(q_ref, k_ref, v_ref, do_ref, dl_ref, b_ref, *rest):
        dq_ref, dk_ref, dv_ref, db_ref = rest[-18:-14]
        wide, tmp, qd, kd, vd, dod, dld, dqd, dkd, dvd, s_scr, dp_scr, p_scr, ds_scr = rest[-14:]
        hs = pl.program_id(1)

        @pl.when((pl.program_id(0) == 0) & (hs == 0))
        def _():
            db_ref[...] = jnp.zeros_like(db_ref)

        for dst, src in ((qd, q_ref), (kd, k_ref), (vd, v_ref)):
            _by_subsequence(dst, src, d, wide, tmp)
        _by_subsequence(dod, do_ref, d, None, tmp)
        _by_subsequence(dld, dl_ref, d, None, tmp)
        dkd[...] = jnp.zeros_like(dkd)
        dvd[...] = jnp.zeros_like(dvd)
        blk = lambda r, n: pl.ds(r * ln + n * BLK, BLK)
        keys = lambda r, n: [(blk(r, n), slice(BLK, 2 * BLK))] + ([(blk(r, n - 1), slice(0, BLK))] if n > 0 else [])
        for u, (r, n) in enumerate(units):
            for rows, band in keys(r, n):
                s_scr[u, :, band] = _dot_nt(qd[blk(r, n), :], kd[rows, :])
                dp_scr[u, :, band] = _dot_nt(dod[blk(r, n), :], vd[rows, :])
        for u, (r, n) in enumerate(units):
            both = dld[blk(r, n), :]
            lse, delta = both[:, 0:1], both[:, 64:65]
            band = slice(BLK, 2 * BLK) if n == 0 else slice(0, 2 * BLK)
            p = jnp.exp(s_scr[u, :, band] * SCALE + b_ref[hs, :, band] - lse)
            ds = p * (dp_scr[u, :, band] - delta)
            p_scr[u, :, band] = p.astype(BF16)
            ds_scr[u, :, band] = ds.astype(BF16)
            db_ref[hs, :, band] += ds
        for u, (r, n) in enumerate(units):
            dq = jnp.zeros((BLK, HD), F32)
            for rows, band in keys(r, n):
                dvd[rows, :] += _dot_tn(p_scr[u, :, band], dod[blk(r, n), :])
                dkd[rows, :] += _dot_tn(ds_scr[u, :, band], qd[blk(r, n), :]) * SCALE
                dq = dq + _dot(ds_scr[u, :, band], kd[rows, :])
            dqd[blk(r, n), :] = dq * SCALE
        for out, acc in ((dq_ref, dqd), (dk_ref, dkd), (dv_ref, dvd)):
            if d == 1:
                out[...] = acc[...].astype(BF16)
            else:
                _to_sequence(wide, acc, d, tmp)
                out[...] = wide[...].astype(BF16)

    qkv_spec = _head_spec(seq, g, 0)
    out_spec = pl.BlockSpec((seq, HD), lambda b, hh: (b, hh))
    band_spec = pl.BlockSpec((4, BLK, 2 * BLK), lambda b, hh: (0, 0, 0))
    ins = [qkv, qkv, qkv, _in_hbm(do), _in_hbm(dl), _in_hbm(bias)]
    in_specs = [_head_spec(seq, g, part) for part in range(3)] + [out_spec, out_spec, band_spec]
    aliases = {}
    if prev_out is not None:
        ins += list(prev_out)
        in_specs += [ANY] * 3
        aliases = {6: 0, 7: 1, 8: 2}
    rows_bf16, rows_f32 = pltpu.VMEM((seq, HD), BF16), pltpu.VMEM((seq, HD), F32)
    staged = [pltpu.VMEM((len(units), BLK, 2 * BLK), F32)] * 2 + [pltpu.VMEM((len(units), BLK, 2 * BLK), BF16)] * 2
    dq, dk, dv, db = pl.pallas_call(
        body, name=f"attn_backward_{g}", out_shape=[HBM_OUT((bsz * seq, QW), BF16)] * 3 + [SDS((4, BLK, 2 * BLK), F32)], grid=(bsz, 4),
        in_specs=in_specs, out_specs=[qkv_spec] * 3 + [band_spec], input_output_aliases=aliases,
        scratch_shapes=[rows_f32] * 2 + [rows_bf16] * 4 + [rows_f32] * 4 + staged,
        compiler_params=_cp(("arbitrary", "arbitrary"), VMEM_CAP // 2),
    )(*ins)
    return (dq, dk, dv), db


def _mix_forward(gates, og, lg, x2, tgt, gate, w_ao, w_co, w_o, conv_w, conv_b, ln_g, ln_b, bsz, seq, tm=256):
    t = x2.shape[0]
    spt = seq // tm

    def body(g_ref, o1, o2, o3, l1, l2, l3, x_ref, t_ref, gate_ref, wao_ref, wco_ref, wo_ref, cw_ref, cb_ref, lng_ref, lnb_ref,
             ain_ref, sin_ref, mrg_ref, dy_ref, aout_ref, sout_ref, yc_ref, o_ref, lj_ref, dxr_ref, vec_ref, dgate_ref, zc_ref):
        b, i = pl.program_id(0), pl.program_id(1)

        @pl.when((b == 0) & (i == 0))
        def _():
            vec_ref[...] = jnp.zeros_like(vec_ref)

        @pl.when(i == 0)
        def _():
            zc_ref[...] = jnp.zeros_like(zc_ref)
            dgate_ref[...] = jnp.zeros_like(dgate_ref)

        g_attn, u, bg, cg, g_conv, m_attn, m_conv = (g_ref[:, lo:hi].astype(F32) for lo, hi in GATE_COLS)
        la, lb, lc = l1[...], l2[...], l3[...]
        mx = jnp.maximum(la, jnp.maximum(lb, lc))
        ea, eb, ec = jnp.exp(la - mx), jnp.exp(lb - mx), jnp.exp(lc - mx)
        den = ea + eb + ec
        o = (ea * o1[...] + eb * o2[...] + ec * o3[...]) / den
        o_ref[...] = o
        lj_ref[...] = mx + jnp.log(den)
        a_in = o * (g_attn * _sig(g_attn))
        ain_ref[...] = a_in.astype(BF16)
        a_out = _dot(a_in.astype(BF16), wao_ref[...])
        aout_ref[...] = a_out.astype(BF16)
        z = cg * u
        rows = lax.broadcasted_iota(jnp.int32, (tm, D), 0)
        c6, c7 = zc_ref[6:7, :], zc_ref[7:8, :]
        z1 = jnp.where(rows == 0, c7, pltpu.roll(z, 1, 0))
        z2 = jnp.where(rows == 0, c6, jnp.where(rows == 1, c7, pltpu.roll(z, 2, 0)))
        zc_ref[...] = z[tm - 8:tm, :]
        y_conv = (cw_ref[0:1, :] * z2 + cw_ref[1:2, :] * z1 + cw_ref[2:3, :] * z) + cb_ref[...]
        yc_ref[...] = y_conv.astype(BF16)
        s_in = bg * y_conv * (g_conv * _sig(g_conv))
        sin_ref[...] = s_in.astype(BF16)
        s_out = _dot(s_in.astype(BF16), wco_ref[...])
        sout_ref[...] = s_out.astype(BF16)
        merged = _sig(m_attn) * a_out + _sig(m_conv) * s_out
        mrg_ref[...] = merged.astype(BF16)
        y = _dot(merged.astype(BF16), wo_ref[...])
        gate1 = 1.0 + gate_ref[0]
        r = ALPHA * x_ref[...] + gate1 * y
        mu = jnp.mean(r, axis=1, keepdims=True)
        rc = r - mu
        rstd = lax.rsqrt(jnp.mean(rc * rc, axis=1, keepdims=True) + LN_EPS)
        xhat = rc * rstd
        diff = (xhat * lng_ref[...] + lnb_ref[...]) - t_ref[...]
        dout = diff * (1.0 / D)
        vec_ref[0:1, :] += jnp.sum(dout * xhat, axis=0, keepdims=True)
        vec_ref[1:2, :] += jnp.sum(dout, axis=0, keepdims=True)
        vec_ref[2:3, :] += jnp.sum(diff * diff, axis=0, keepdims=True)
        dxh = dout * lng_ref[...]
        dr = rstd * (dxh - jnp.mean(dxh, axis=1, keepdims=True) - xhat * jnp.mean(dxh * xhat, axis=1, keepdims=True))
        dxr_ref[...] = ALPHA * dr
        dy_ref[...] = (dr * gate1).astype(BF16)
        dgate_ref[0] += jnp.sum(dr * y, axis=0, keepdims=True)

    tok = lambda w: pl.BlockSpec((tm, w), lambda b, i: (b * spt + i, 0))
    const = lambda s: pl.BlockSpec(s, lambda b, i: (0,) * len(s))
    per_seq = pl.BlockSpec((1, 1, D), lambda b, i: (b, 0, 0))
    outs = pl.pallas_call(
        body, name="mix_forward", grid=(bsz, spt),
        out_shape=[HBM_OUT((t, AW), BF16), HBM_OUT((t, D), BF16), HBM_OUT((t, D), BF16), HBM_OUT((t, D), BF16), HBM_OUT((t, D), BF16),
                   HBM_OUT((t, D), BF16), HBM_OUT((t, D), BF16), HBM_OUT((t, AW), F32), HBM_OUT((t, AW), F32), HBM_OUT((t, D), F32),
                   SDS((8, D), F32), SDS((bsz, 1, D), F32)],
        in_specs=[tok(NGATE)] + [tok(AW)] * 6 + [tok(D), tok(D), per_seq, const((AW, D)), const((D, D)), const((D, D)),
                                                 const((3, D)), const((1, D)), const((1, D)), const((1, D))],
        out_specs=[tok(AW), tok(D), tok(D), tok(D), tok(D), tok(D), tok(D), tok(AW), tok(AW), tok(D), const((8, D)), per_seq],
        scratch_shapes=[pltpu.VMEM((8, D), F32)],
        compiler_params=_cp(("arbitrary", "arbitrary"), VMEM_CAP),
    )(_in_hbm(gates), *map(_in_hbm, og), *map(_in_hbm, lg), _in_hbm(x2), _in_hbm(tgt), gate, w_ao, w_co, w_o, conv_w, conv_b, ln_g, ln_b)
    return outs


def _mix_backward(gates, dy, a_out, s_out, y_conv, o, lj, w_ao, w_co, w_o, conv_w, vec_f, bsz, seq, tm=256):
    t = dy.shape[0]
    spt = seq // tm

    def body(g_ref, dy_ref, aout_ref, sout_ref, yc_ref, o_ref, lj_ref, wao_ref, wco_ref, wo_ref, cw_ref, vecf_ref,
             dg_ref, do_ref, dl_ref, daout_ref, dsout_ref, vec_ref, car_ref):
        b, i = pl.program_id(0), pl.program_id(1)

        @pl.when((b == 0) & (i == 0))
        def _():
            vec_ref[...] = vecf_ref[...]

        @pl.when(i == 0)
        def _():
            car_ref[...] = jnp.zeros_like(car_ref)

        g_attn, u, bg, cg, g_conv, m_attn, m_conv = (g_ref[:, lo:hi].astype(F32) for lo, hi in GATE_COLS)
        dmerged = _dot_nt(dy_ref[...], wo_ref[...])
        sa, sc = _sig(m_attn), _sig(m_conv)
        da_out = (dmerged * sa).astype(BF16)
        ds_out = (dmerged * sc).astype(BF16)
        daout_ref[...] = da_out
        dsout_ref[...] = ds_out
        dg_ref[:, 4608:5632] = (dmerged * aout_ref[...].astype(F32) * (sa * (1.0 - sa))).astype(BF16)
        dg_ref[:, 5632:6656] = (dmerged * sout_ref[...].astype(F32) * (sc * (1.0 - sc))).astype(BF16)
        da_in = _dot_nt(da_out, wao_ref[...])
        ds_in = _dot_nt(ds_out, wco_ref[...])
        sga = _sig(g_attn)
        o = o_ref[...]
        do = da_in * (g_attn * sga)
        do_ref[...] = do
        dg_ref[:, 0:512] = (da_in * o * (sga * (1.0 + g_attn * (1.0 - sga)))).astype(BF16)
        prod = do * o
        lane = lax.broadcasted_iota(jnp.int32, (tm, HD), 1)
        for j in range(4):
            cs = slice(j * HD, (j + 1) * HD)
            delta = jnp.sum(prod[:, cs], axis=1, keepdims=True)
            dl_ref[:, cs] = jnp.where(lane < 64, lj_ref[:, cs], delta)
        sgc = _sig(g_conv)
        silu_c = g_conv * sgc
        yc = yc_ref[...].astype(F32)
        dg_ref[:, 1536:2560] = (ds_in * yc * silu_c).astype(BF16)
        dg_ref[:, 3584:4608] = (ds_in * bg * yc * (sgc * (1.0 + g_conv * (1.0 - sgc)))).astype(BF16)
        dyc = ds_in * bg * silu_c
        rows = lax.broadcasted_iota(jnp.int32, (tm, D), 0)
        c0, c1 = car_ref[0:1, :], car_ref[1:2, :]
        n1 = jnp.where(rows == tm - 1, c0, pltpu.roll(dyc, tm - 1, 0))
        n2 = jnp.where(rows == tm - 2, c0, jnp.where(rows == tm - 1, c1, pltpu.roll(dyc, tm - 2, 0)))
        car_ref[...] = dyc[0:8, :]
        dz = cw_ref[2:3, :] * dyc + cw_ref[1:2, :] * n1 + cw_ref[0:1, :] * n2
        z = cg * u
        dg_ref[:, 512:1536] = (dz * cg).astype(BF16)
        dg_ref[:, 2560:3584] = (dz * u).astype(BF16)
        vec_ref[3:4, :] += jnp.sum(n2 * z, axis=0, keepdims=True)
        vec_ref[4:5, :] += jnp.sum(n1 * z, axis=0, keepdims=True)
        vec_ref[5:6, :] += jnp.sum(dyc * z, axis=0, keepdims=True)
        vec_ref[6:7, :] += jnp.sum(dyc, axis=0, keepdims=True)

    tok = lambda w: pl.BlockSpec((tm, w), lambda b, i: (b * spt + (spt - 1 - i), 0))
    const = lambda s: pl.BlockSpec(s, lambda b, i: (0,) * len(s))
    return pl.pallas_call(
        body, name="mix_backward", grid=(bsz, spt),
        out_shape=[HBM_OUT((t, NGATE), BF16), HBM_OUT((t, AW), F32), HBM_OUT((t, AW), F32), HBM_OUT((t, D), BF16), HBM_OUT((t, D), BF16),
                   SDS((8, D), F32)],
        in_specs=[tok(NGATE), tok(D), tok(D), tok(D), tok(D), tok(AW), tok(AW), const((AW, D)), const((D, D)), const((D, D)), const((3, D)),
                  const((8, D))],
        out_specs=[tok(NGATE), tok(AW), tok(AW), tok(D), tok(D), const((8, D))],
        scratch_shapes=[pltpu.VMEM((8, D), F32)],
        compiler_params=_cp(("arbitrary", "arbitrary"), VMEM_CAP),
    )(*map(_in_hbm, (gates, dy, a_out, s_out, y_conv, o, lj)), w_ao, w_co, w_o, conv_w, vec_f)


def _scatter_copies(src, land, send_sems, recv_sems):
    x, y, c = _place()
    chips = [(1 - x, y), (x, 1 - y), (1 - x, 1 - y)]
    return [pltpu.make_async_remote_copy(src_ref=src[a].at[2 * cx + cy], dst_ref=land[a].at[r], send_sem=send_sems.at[3 * a + r],
                                         recv_sem=recv_sems.at[3 * a + r], device_id=(cx, cy, c), device_id_type=MESH)
            for a in range(len(src)) for r, (cx, cy) in enumerate(chips)]


def _halves_out(a):
    kind, nr, nc = W_CUTS[a]
    shape = (nr // 2, W_FULL[a][1]) if kind == "col" else (NCHIP, nr // 2, nc)
    return [SDS(shape, F32), SDS(shape, BF16)]


def _write_halves(a, acc_ref, c, mine_ref, theirs_ref):
    kind, nr, nc = W_CUTS[a]
    hr = nr // 2
    if kind == "col":
        mine_ref[...] = acc_ref[pl.ds(pl.multiple_of(c * hr, hr), hr), :]
        theirs_ref[...] = acc_ref[pl.ds(pl.multiple_of((1 - c) * hr, hr), hr), :].astype(BF16)
    else:
        for k in range(NCHIP):
            mine_ref[k] = acc_ref[pl.ds(pl.multiple_of(k * nr + c * hr, hr), hr), :]
            theirs_ref[k] = acc_ref[pl.ds(pl.multiple_of(k * nr + (1 - c) * hr, hr), hr), :].astype(BF16)


def _out_weight_grads(a_in, da_out, s_in, ds_out, merged, dy, core, tk=1024):
    t = dy.shape[0]
    nt = t // tk

    def body(c_ref, ain_ref, da_ref, sin_ref, ds_ref, m_ref, dy_ref, *rest):
        outs, (gao, gco, go) = rest[:6], rest[6:]

        @pl.when(pl.program_id(0) == 0)
        def _():
            gao[...] = jnp.zeros_like(gao)
            gco[...] = jnp.zeros_like(gco)
            go[...] = jnp.zeros_like(go)

        gao[...] += _dot_tn(ain_ref[...], da_ref[...])
        gco[...] += _dot_tn(sin_ref[...], ds_ref[...])
        go[...] += _dot_tn(m_ref[...], dy_ref[...])

        @pl.when(pl.program_id(0) == nt - 1)
        def _():
            for a, acc in ((1, gao), (2, gco), (3, go)):
                _write_halves(a, acc, c_ref[0], outs[2 * a - 2], outs[2 * a - 1])

    tok = lambda w: pl.BlockSpec((tk, w), lambda i, cr: (i, 0))
    out_shape = _halves_out(1) + _halves_out(2) + _halves_out(3)
    outs = pl.pallas_call(
        body, name="out_weight_grads", out_shape=out_shape,
        grid_spec=pltpu.PrefetchScalarGridSpec(
            num_scalar_prefetch=1, grid=(nt,), in_specs=[tok(AW), tok(D), tok(D), tok(D), tok(D), tok(D)],
            out_specs=[pl.BlockSpec(o.shape, lambda i, cr, nd=len(o.shape): (0,) * nd) for o in out_shape],
            scratch_shapes=[pltpu.VMEM((AW, D), F32), pltpu.VMEM((D, D), F32), pltpu.VMEM((D, D), F32)]),
        compiler_params=_cp(("arbitrary",), VMEM_CAP),
    )(core, a_in, da_out, s_in, ds_out, merged, dy)
    return [(outs[0], outs[1]), (outs[2], outs[3]), (outs[4], outs[5])]


def _input_grad(dq, dk, dv, dgates, w, x2, dxr, sc1p, seq, sums, tm=512):
    t = x2.shape[0]
    nt = t // tm
    spt = seq // tm
    bsz = t // seq
    n = len(sums)
    gblk = NGATE // 4
    nsteps = 3 + 4

    def body(dq_ref, dk_ref, dv_ref, dg_ref, wq_ref, wg_ref, x_ref, dxr_ref, sc_ref, *rest):
        src, (dx_ref, dsh_ref, dsc_ref), land = rest[:n], rest[n:n + 3], rest[n + 3:2 * n + 3]
        acc_ref, send_sems, recv_sems = rest[2 * n + 3:]
        j, i = pl.program_id(0), pl.program_id(1)
        copies = _scatter_copies(src, land, send_sems, recv_sems)
        rows = pl.ds(pl.multiple_of(i * tm, tm), tm)

        @pl.when((i == 0) & (j == 0))
        def _():
            for cp in copies:
                cp.start()

        for k, ref in enumerate((dq_ref, dk_ref, dv_ref)):
            @pl.when(j == k)
            def _(k=k, ref=ref):
                part = _dot_nt(ref[...], wq_ref[...])
                if k == 0:
                    acc_ref[rows, :] = part
                else:
                    acc_ref[rows, :] += part

        @pl.when((j >= 3) & (j < nsteps - 1))
        def _():
            acc_ref[rows, :] += _dot_nt(dg_ref[...], wg_ref[...])

        @pl.when(j == nsteps - 1)
        def _():
            dh = acc_ref[rows, :] + _dot_nt(dg_ref[...], wg_ref[...])
            dx_ref[...] = dh * sc_ref[0] + dxr_ref[...]

            @pl.when(i % spt == 0)
            def _():
                dsh_ref[...] = jnp.zeros_like(dsh_ref)
                dsc_ref[...] = jnp.zeros_like(dsc_ref)

            dsh_ref[0] += jnp.sum(dh, axis=0, keepdims=True)
            dsc_ref[0] += jnp.sum(dh * x_ref[...], axis=0, keepdims=True)

        @pl.when((i == nt - 1) & (j == nsteps - 1))
        def _():
            for cp in copies:
                cp.wait()

    def held(k):
        return lambda j, i: (jnp.where(j == k, i, jnp.where(j < k, 0, nt - 1)), 0)

    last = lambda j, i: (jnp.where(j == nsteps - 1, i, 0), 0)
    outs = pl.pallas_call(
        body, name="input_grad", grid=(nsteps, nt),
        out_shape=[SDS((t, D), F32), SDS((bsz, 1, D), F32), SDS((bsz, 1, D), F32)] + [SDS((3,) + s.shape[1:], BF16) for s in sums],
        in_specs=[pl.BlockSpec((tm, QW), held(0)), pl.BlockSpec((tm, QW), held(1)), pl.BlockSpec((tm, QW), held(2)),
                  pl.BlockSpec((tm, gblk), lambda j, i: (jnp.where(j >= 3, i, 0), jnp.clip(j - 3, 0, 3))),
                  pl.BlockSpec((D, QW), lambda j, i: (0, jnp.minimum(j, 2))),
                  pl.BlockSpec((pl.Element(D), pl.Element(gblk)), lambda j, i: (0, pl.multiple_of(3 * QW + gblk * jnp.clip(j - 3, 0, 3), 128))),
                  pl.BlockSpec((tm, D), last), pl.BlockSpec((tm, D), last),
                  pl.BlockSpec((1, 1, D), lambda j, i: (jnp.where(j == nsteps - 1, i // spt, 0), 0, 0))] + [ANY] * n,
        out_specs=[pl.BlockSpec((tm, D), last),
                   pl.BlockSpec((1, 1, D), lambda j, i: (jnp.where(j == nsteps - 1, i // spt, 0), 0, 0)),
                   pl.BlockSpec((1, 1, D), lambda j, i: (jnp.where(j == nsteps - 1, i // spt, 0), 0, 0))] + [ANY] * n,
        scratch_shapes=[pltpu.VMEM((t, D), F32), pltpu.SemaphoreType.DMA((3 * NCHIP,)), pltpu.SemaphoreType.DMA((3 * NCHIP,))],
        compiler_params=_cp(("arbitrary", "arbitrary"), VMEM_CAP, side=True),
    )(*map(_in_hbm, (dq, dk, dv, dgates, w, w, x2, dxr)), sc1p, *sums)
    return outs[0], outs[1], outs[2], outs[3:]


def _in_weight_grad(ht, dq, dk, dv, dgates, core, sums):
    t = ht.shape[1]
    hr = D // 2
    n = len(sums)

    def body(c_ref, ht_ref, dq_ref, dk_ref, dv_ref, dg_ref, *rest):
        src, mine_ref, got_ref, land = rest[:n], rest[n], rest[n + 1], rest[n + 2:2 * n + 2]
        acc_ref, their_buf, send_sems, recv_sems, tile_send, tile_recv = rest[2 * n + 2:]
        j = pl.program_id(0)
        slot = j % 2
        px, py, pc = _place()
        copies = _scatter_copies(src, land, send_sems, recv_sems)

        def to_sibling(step, k):
            return pltpu.make_async_remote_copy(src_ref=their_buf.at[k], dst_ref=got_ref.at[:, pl.ds(pl.multiple_of(step * TN, TN), TN)],
                                                send_sem=tile_send.at[k], recv_sem=tile_recv.at[0], device_id=(px, py, 1 - pc),
                                                device_id_type=MESH)

        @pl.when(j == 0)
        def _():
            for cp in copies:
                cp.start()

        @pl.when(j >= 2)
        def _():
            to_sibling(j - 2, slot).wait_send()

        for k, ref in enumerate((dq_ref, dk_ref, dv_ref)):
            @pl.when((j >= k * NQT) & (j < (k + 1) * NQT))
            def _(ref=ref):
                acc_ref[...] = _dot(ht_ref[...], ref[...])

        @pl.when(j >= 3 * NQT)
        def _():
            acc_ref[...] = _dot(ht_ref[...], dg_ref[...])

        _write_halves(0, acc_ref, c_ref[0], mine_ref, their_buf.at[slot])
        to_sibling(j, slot).start()

        @pl.when(j == NPT - 1)
        def _():
            to_sibling(j - 1, 1 - slot).wait_send()
            to_sibling(j, slot).wait_send()
            pltpu.make_async_remote_copy(src_ref=got_ref, dst_ref=got_ref, send_sem=tile_send.at[0], recv_sem=tile_recv.at[0],
                                         device_id=(px, py, 1 - pc), device_id_type=MESH).wait_recv()
            for cp in copies:
                cp.wait()

    def part(k):
        return pl.BlockSpec((t, TN), lambda j, cr: (0, jnp.clip(j - k * NQT, 0, NQT - 1)))

    outs = pl.pallas_call(
        body, name="in_weight_grad", out_shape=[SDS((hr, NCOL), F32), SDS((hr, NCOL), BF16)] + [SDS((3,) + v.shape[1:], BF16) for v in sums],
        grid_spec=pltpu.PrefetchScalarGridSpec(
            num_scalar_prefetch=1, grid=(NPT,),
            in_specs=[pl.BlockSpec((D, t), lambda j, cr: (0, 0)), part(0), part(1), part(2),
                      pl.BlockSpec((t, TN), lambda j, cr: (0, jnp.maximum(j - 3 * NQT, 0)))] + [ANY] * n,
            out_specs=[pl.BlockSpec((hr, TN), lambda j, cr: (0, j)), ANY] + [ANY] * n,
            scratch_shapes=[pltpu.VMEM((D, TN), F32), pltpu.VMEM((2, hr, TN), BF16),
                            pltpu.SemaphoreType.DMA((3 * NCHIP,)), pltpu.SemaphoreType.DMA((3 * NCHIP,)),
                            pltpu.SemaphoreType.DMA((2,)), pltpu.SemaphoreType.DMA((1,))]),
        compiler_params=_cp(("arbitrary",), VMEM_CAP, side=True),
    )(core, *map(_in_hbm, (ht, dq, dk, dv, dgates)), *sums)
    return outs[0], outs[1], outs[2:]


def _sum_partials(gathered):
    def body(g_ref, o_ref):
        acc = g_ref[0]
        for k in range(1, 8):
            acc = acc + g_ref[k]
        o_ref[...] = acc

    return pl.pallas_call(body, name="sum_partials", out_shape=SDS(gathered.shape[1:], F32), in_specs=[VMEM_SPEC], out_specs=VMEM_SPEC)(gathered)


def _adamw(w, g, m, v, name, tr=256):
    r, cdim = w.shape
    tr = tr if cdim <= D else tr // 2
    tr = tr if (r % tr == 0 and r > tr) else r

    def body(w_ref, g_ref, m_ref, v_ref, go_ref, d_ref, nm_ref, nv_ref):
        gv = g_ref[...]
        go_ref[...] = gv
        nm = B1 * m_ref[...] + (1.0 - B1) * gv
        nv = B2 * v_ref[...] + (1.0 - B2) * (gv * gv)
        m_hat = nm / (1.0 - B1 ** STEP)
        v_hat = nv / (1.0 - B2 ** STEP)
        d_ref[...] = -LR * (m_hat / (jnp.sqrt(v_hat) + EPS) + WD * w_ref[...])
        nm_ref[...] = nm
        nv_ref[...] = nv

    spec = pl.BlockSpec((tr, cdim), lambda i: (i, 0))
    return pl.pallas_call(
        body, name=name, grid=(r // tr,), out_shape=[SDS((r, cdim), F32)] * 4, in_specs=[spec] * 4, out_specs=[spec] * 4,
        compiler_params=_cp(("parallel",), VMEM_CAP // 2),
    )(w, g, m, v)


def _t5_bucket(dist):
    n = jnp.maximum(dist, 1).astype(F32)
    large = MAX_EXACT + (jnp.log(n / MAX_EXACT) / math.log(MAX_DISTANCE / MAX_EXACT) * (N_BUCKETS - MAX_EXACT)).astype(jnp.int32)
    large = jnp.minimum(large, N_BUCKETS - 1)
    return jnp.where(dist < MAX_EXACT, dist, large)


def _band_buckets():
    a = jnp.arange(BLK)[:, None]
    b = jnp.arange(2 * BLK)[None, :]
    steps = jnp.maximum(a + BLK - b, 0)
    return jnp.stack([_t5_bucket(steps * d) for d in DILATIONS]).astype(jnp.int32)


def _pad_rows(a, rows=8):
    return jnp.pad(a, ((0, rows - a.shape[0]), (0, 0)))


def kernel(x, c, w_ada, b_ada, w_in, conv_w, conv_b, rel_bias, w_attn_out, w_conv_out, w_o, ln_g, ln_b, loss_target, m_w_ada, m_b_ada, m_w_in, m_conv_w, m_conv_b, m_rel_bias, m_w_attn_out, m_w_conv_out, m_w_o, m_ln_g, m_ln_b, v_w_ada, v_b_ada, v_w_in, v_conv_w, v_conv_b, v_rel_bias, v_w_attn_out, v_w_conv_out, v_w_o, v_ln_g, v_ln_b):
    bsz, seq, _ = x.shape
    t = bsz * seq
    mx, my, mc = _place()
    chip = 2 * mx + my
    dev = 4 * mx + 2 * my + mc
    x2 = x.reshape(t, D)
    tgt = loss_target.reshape(t, D)

    mine = _to_bf16_windows([w[0] for w in (w_in, w_attn_out, w_conv_out, w_o)])

    n_ada = w_ada.shape[2]
    n_cw = conv_w.shape[2]
    c_and_cw = jnp.concatenate([_pad_rows(c), jnp.pad(conv_w[0], ((0, 5), (0, D - n_cw)))], axis=0)
    firsts = _all_gather8(c_and_cw, "gather_c_conv_w")
    c_all = firsts[:, 0:bsz, :].reshape(8 * bsz, D)
    conv_w_f = firsts[0::2, 8:11, 0:n_cw].transpose(1, 0, 2).reshape(3, D)
    b_cols = lax.dynamic_slice(b_ada, (0, chip * n_ada), (1, n_ada))
    mod_part = _ada_forward(c_all, w_ada[0], b_cols)
    mod_parts = _all_gather8(mod_part, "gather_mod")
    mod_all = mod_parts[0::2].transpose(1, 0, 2).reshape(8 * bsz, 3 * D)
    mod = lax.dynamic_slice(mod_all, (dev * bsz, 0), (bsz, 3 * D))
    shift = mod[:, 0:D].reshape(bsz, 1, D)
    sc1p = 1.0 + mod[:, D:2 * D].reshape(bsz, 1, D)
    gate = mod[:, 2 * D:].reshape(bsz, 1, D)

    h, ht = _modulate(x2, sc1p, shift, seq)
    tab = lax.dynamic_index_in_dim(jnp.asarray(_tile_tables()), chip, 0, keepdims=False)
    qkv, gates, (w_in_f, w_ao_f, w_co_f, w_o_f) = _project_gather(h, mine, tab)
    buckets = _band_buckets()
    bias = _bias_tables(rel_bias, buckets)
    og, lg = [], []
    for g in range(3):
        o_g, l_g = _attn_forward(g, qkv, bias[g], bsz, seq)
        og.append(o_g)
        lg.append(l_g)
    (a_in, s_in, merged, dy, a_out, s_out, y_conv, o, lj, dxr, vec_f, dgate) = _mix_forward(
        gates, og, lg, x2, tgt, gate, w_ao_f, w_co_f, w_o_f, conv_w_f, conv_b, ln_g, ln_b, bsz, seq)

    dgates, do, dl, da_out, ds_out, vec = _mix_backward(gates, dy, a_out, s_out, y_conv, o, lj, w_ao_f, w_co_f, w_o_f, conv_w_f, vec_f, bsz, seq)
    core = jnp.reshape(mc, (1,)).astype(jnp.int32)
    small_grads = _out_weight_grads(a_in, da_out, s_in, ds_out, merged, dy, core)
    got_small = _swap_halves([theirs for _, theirs in small_grads], "swap_small_grad_halves")
    sums_small = _chip_sums([own for own, _ in small_grads], got_small, 1, "chip_sums_small")
    dqkv, dbs = None, []
    for g in range(3):
        dqkv, db = _attn_backward(g, qkv, do, dl, bias[g], dqkv, bsz, seq)
        dbs.append(db)
    dq, dk, dv = dqkv
    drb = _bias_grad(jnp.stack(dbs), buckets)
    drb = drb[:, :, 0:4].transpose(1, 0, 2).reshape(N_BUCKETS, 12)
    g_in_mine, got_in, landed_small = _in_weight_grad(ht, dq, dk, dv, dgates, core, [bf for _, bf in sums_small])
    sums_in = _chip_sums([g_in_mine], [got_in], 0, "chip_sums_in")
    grad_x, dshift, dscale, landed_in = _input_grad(dq, dk, dv, dgates, w_in_f, x2, dxr, sc1p, seq, [bf for _, bf in sums_in])
    halves = _reduce_mine([own for own, _ in sums_in + sums_small], list(landed_in) + list(landed_small))

    dmod = jnp.concatenate([dshift, dscale, dgate], axis=2).reshape(bsz * 3, D)
    drb_row = jnp.pad(drb.reshape(1, N_BUCKETS * 12), ((0, 0), (0, D - N_BUCKETS * 12)))
    vec = lax.dynamic_update_slice(vec, drb_row, (7, 0))
    packed = jnp.concatenate([vec, _pad_rows(dmod)], axis=0)
    (gw_in, gw_ao, gw_co, gw_o), gathered = _join_halves(halves, packed)
    small = _sum_partials(gathered)
    g_ln_g, g_ln_b, loss_lanes = small[0:1], small[1:2], small[2:3]
    g_conv_w_full, g_conv_b = small[3:6], small[6:7]
    g_rel_bias = small[7, 0:N_BUCKETS * 12].reshape(N_BUCKETS, 12)
    loss = 0.5 / D * jnp.sum(loss_lanes)
    dmod_all = gathered[:, 8:8 + 3 * bsz, :].reshape(8 * bsz, 3 * D)
    dmod_cols = lax.dynamic_slice(dmod_all, (0, chip * n_ada), (8 * bsz, n_ada))
    gw_ada, gb_ada = _ada_backward(c_all, dmod_cols, dmod_all)
    g_conv_w = lax.dynamic_slice(g_conv_w_full, (0, chip * n_cw), (3, n_cw))

    names = ["w_ada", "b_ada", "w_in", "conv_w", "conv_b", "rel_bias", "w_attn_out", "w_conv_out", "w_o", "ln_g", "ln_b"]
    two_d = lambda a: a.reshape(a.shape[-2:]) if a.ndim == 3 else a
    weights = dict(zip(names, map(two_d, (w_ada, b_ada, w_in, conv_w, conv_b, rel_bias, w_attn_out, w_conv_out, w_o, ln_g, ln_b))))
    ms = dict(zip(names, map(two_d, (m_w_ada, m_b_ada, m_w_in, m_conv_w, m_conv_b, m_rel_bias, m_w_attn_out, m_w_conv_out, m_w_o, m_ln_g, m_ln_b))))
    vs = dict(zip(names, map(two_d, (v_w_ada, v_b_ada, v_w_in, v_conv_w, v_conv_b, v_rel_bias, v_w_attn_out, v_w_conv_out, v_w_o, v_ln_g, v_ln_b))))
    grads = dict(zip(names, (gw_ada, gb_ada, gw_in, g_conv_w, g_conv_b, g_rel_bias, gw_ao, gw_co, gw_o, g_ln_g, g_ln_b)))
    shapes = dict(zip(names, (w_ada, b_ada, w_in, conv_w, conv_b, rel_bias, w_attn_out, w_conv_out, w_o, ln_g, ln_b)))
    grad_out, deltas, new_m, new_v = {}, {}, {}, {}
    for n in names:
        grad_out[n], deltas[n], new_m[n], new_v[n] = _adamw(weights[n], grads[n], ms[n], vs[n], f"adamw_{n}")
    shaped = lambda d: [d[n].reshape(shapes[n].shape) for n in names]
    return (loss, grad_x.reshape(bsz, seq, D), *shaped(grad_out), *shaped(deltas), *shaped(new_m), *shaped(new_v))
```

```python
import math

import numpy as np
import jax
import jax.numpy as jnp
from jax import lax
from jax.experimental import pallas as pl
from jax.experimental.pallas import tpu as pltpu

F32 = jnp.float32
BF16 = jnp.bfloat16
SDS = jax.ShapeDtypeStruct
MESH = pl.DeviceIdType.MESH
HBM_OUT = pltpu.HBM
ANY = pl.BlockSpec(memory_space=pl.ANY)
VMEM_SPEC = pl.BlockSpec(memory_space=pltpu.VMEM)

D = 1024
HD = 128
BLK = 128
QW = 1536
AW = 512
NGATE = 6656
GATE_COLS = ((0, 512), (512, 1536), (1536, 2560), (2560, 3584), (3584, 4608), (4608, 5632), (5632, 6656))
NCOL = 3 * QW + NGATE
TN = 512
NQT = QW // TN
NPT = NCOL // TN
DILATIONS = (1, 4, 16)
N_BUCKETS, MAX_EXACT, MAX_DISTANCE = 32, 16, 2048
ALPHA = 2.0 ** 0.25
LN_EPS = 1e-5
NEG = -1e30
SCALE = HD ** -0.5
LR, B1, B2, EPS, WD, STEP = 0.001, 0.9, 0.999, 1e-08, 0.01, 10
NCHIP = 4
VMEM_CAP = 60 * 2 ** 20


def _cp(sem=None, vmem=None, side=False):
    return pltpu.CompilerParams(dimension_semantics=sem, vmem_limit_bytes=vmem, has_side_effects=side)


def _dot(a, b):
    return jnp.dot(a, b, preferred_element_type=F32)


def _dot_nt(a, b):
    return lax.dot_general(a, b, (((1,), (1,)), ((), ())), preferred_element_type=F32)


def _dot_tn(a, b):
    return lax.dot_general(a, b, (((0,), (0,)), ((), ())), preferred_element_type=F32)


def _sig(x):
    return 1.0 / (1.0 + jnp.exp(-x))


def _in_hbm(a):
    return pltpu.with_memory_space_constraint(a, pltpu.HBM)


def _place():
    x, y, c = lax.axis_index("x"), lax.axis_index("y"), lax.axis_index("c")
    return x, y, c


def _all_gather8(v, name):
    r, cdim = v.shape

    def body(v_ref, out_ref, send_sems, recv_sems, local_sem):
        x, y, c = _place()
        me = 4 * x + 2 * y + c
        peers = [(x, y, 1 - c), (1 - x, y, c), (x, 1 - y, c), (1 - x, 1 - y, c),
                 (1 - x, y, 1 - c), (x, 1 - y, 1 - c), (1 - x, 1 - y, 1 - c)]
        mine = pltpu.make_async_copy(v_ref, out_ref.at[me], local_sem)
        mine.start()

        def copy(k, block, to):
            return pltpu.make_async_remote_copy(src_ref=v_ref, dst_ref=out_ref.at[block], send_sem=send_sems.at[k],
                                                recv_sem=recv_sems.at[k], device_id=to, device_id_type=MESH)

        sends = [copy(k, me, p) for k, p in enumerate(peers)]
        for cp in sends:
            cp.start()
        for k, (px, py, pc) in enumerate(peers):
            copy(k, 4 * px + 2 * py + pc, (px, py, pc)).wait_recv()
        for cp in sends:
            cp.wait_send()
        mine.wait()

    return pl.pallas_call(
        body, name=name, out_shape=SDS((8, r, cdim), v.dtype), in_specs=[VMEM_SPEC], out_specs=VMEM_SPEC,
        scratch_shapes=[pltpu.SemaphoreType.DMA((7,)), pltpu.SemaphoreType.DMA((7,)), pltpu.SemaphoreType.DMA(())],
        compiler_params=_cp(side=True),
    )(v)


W_CUTS = (("col", D, NCOL // NCHIP), ("col", AW, D // NCHIP), ("row", D // NCHIP, D), ("row", D // NCHIP, D))
W_FULL = ((D, NCOL), (AW, D), (D, D), (D, D))


def _shard_window(ref, cut, k, half):
    kind, nr, nc = cut
    hr = nr // 2
    if kind == "col":
        rows = pl.ds(0, nr) if half is None else pl.ds(pl.multiple_of(half * hr, 16), hr)
        return ref.at[rows, pl.ds(pl.multiple_of(k * nc, 128), nc)]
    if half is None:
        return ref.at[pl.ds(pl.multiple_of(k * nr, 16), nr), :]
    return ref.at[pl.ds(pl.multiple_of(k * nr + half * hr, 16), hr), :]


def _half_rows(ref, cut, half):
    hr = cut[1] // 2
    return ref.at[pl.ds(pl.multiple_of(half * hr, 16), hr), :]


def _to_bf16_windows(ws, packed):
    x, y, _ = _place()
    chip = jnp.reshape(2 * x + y, (1,)).astype(jnp.int32)
    tr = 256
    n = len(ws)
    nsteps = D // tr
    r, cdim = packed.shape

    def body(c_ref, *refs):
        src, v_ref, dst, out_ref = refs[:n], refs[n], refs[n + 1:2 * n + 1], refs[2 * n + 1]
        g_send, g_recv, local_sem = refs[2 * n + 2:]
        px, py, pc = _place()
        me = 4 * px + 2 * py + pc
        peers = [(px, py, 1 - pc), (1 - px, py, pc), (px, 1 - py, pc), (1 - px, 1 - py, pc),
                 (1 - px, py, 1 - pc), (px, 1 - py, 1 - pc), (1 - px, 1 - py, 1 - pc)]
        mine = pltpu.make_async_copy(v_ref, out_ref.at[me], local_sem)

        def copy(k, block, to):
            return pltpu.make_async_remote_copy(src_ref=v_ref, dst_ref=out_ref.at[block], send_sem=g_send.at[k],
                                                recv_sem=g_recv.at[k], device_id=to, device_id_type=MESH)

        gathers = [copy(k, me, p) for k, p in enumerate(peers)]

        @pl.when(pl.program_id(0) == 0)
        def _():
            mine.start()
            for cp in gathers:
                cp.start()

        dst[0][...] = src[0][...].astype(BF16)

        @pl.when(pl.program_id(0) == 0)
        def _():
            for a in range(1, n):
                dst[a][...] = src[a][...].astype(BF16)

        @pl.when(pl.program_id(0) == nsteps - 1)
        def _():
            for k, (qx, qy, qc) in enumerate(peers):
                copy(k, 4 * qx + 2 * qy + qc, (qx, qy, qc)).wait_recv()
            for cp in gathers:
                cp.wait_send()
            mine.wait()

    in_specs = [pl.BlockSpec((tr, W_CUTS[0][2]), lambda i, cr: (i, 0))]
    out_specs = [pl.BlockSpec((tr, W_CUTS[0][2]), lambda i, cr: (i, cr[0]))]
    for a in range(1, n):
        kind, nr, nc = W_CUTS[a]
        in_specs.append(pl.BlockSpec((nr, nc), lambda i, cr: (0, 0)))
        out_specs.append(pl.BlockSpec((nr, nc), (lambda i, cr: (0, cr[0])) if kind == "col" else (lambda i, cr: (cr[0], 0))))
    outs = pl.pallas_call(
        body, name="to_bf16", out_shape=[SDS(W_FULL[a], BF16) for a in range(n)] + [SDS((8, r, cdim), F32)],
        grid_spec=pltpu.PrefetchScalarGridSpec(
            num_scalar_prefetch=1, grid=(nsteps,), in_specs=in_specs + [ANY], out_specs=out_specs + [ANY],
            scratch_shapes=[pltpu.SemaphoreType.DMA((7,)), pltpu.SemaphoreType.DMA((7,)), pltpu.SemaphoreType.DMA(())]),
        compiler_params=_cp(("arbitrary",), side=True),
    )(chip, *ws, packed)
    return outs[:n], outs[n]


def _swap_halves(theirs, name):
    n = len(theirs)

    def body(*refs):
        src, land = refs[:n], refs[n:2 * n]
        send_sems, recv_sems = refs[2 * n:]
        x, y, c = _place()
        copies = [pltpu.make_async_remote_copy(src_ref=src[a], dst_ref=land[a], send_sem=send_sems.at[a], recv_sem=recv_sems.at[a],
                                               device_id=(x, y, 1 - c), device_id_type=MESH) for a in range(n)]
        for cp in copies:
            cp.start()
        for cp in copies:
            cp.wait()

    return pl.pallas_call(
        body, name=name, out_shape=[SDS(v.shape, v.dtype) for v in theirs], in_specs=[ANY] * n, out_specs=[ANY] * n,
        scratch_shapes=[pltpu.SemaphoreType.DMA((n,)), pltpu.SemaphoreType.DMA((n,))],
        compiler_params=_cp(side=True),
    )(*theirs)


def _chip_sums(mines, gots, first, name):
    n = len(mines)
    x, y, _ = _place()
    me = jnp.reshape(2 * x + y, (1,)).astype(jnp.int32)

    def body(me_ref, *refs):
        ins, outs = refs[:2 * n], refs[2 * n:]
        for a in range(n):
            hr, nc = W_CUTS[first + a][1] // 2, W_CUTS[first + a][2]
            s = (ins[2 * a][...] + ins[2 * a + 1][...].astype(F32)).reshape(hr, nc)
            outs[2 * a + 1][0] = s.astype(BF16)

            @pl.when(pl.program_id(0) == me_ref[0])
            def _(a=a, s=s):
                outs[2 * a][...] = s

    in_specs, out_specs, out_shape = [], [], []
    for a in range(n):
        kind, nr, nc = W_CUTS[first + a]
        hr = nr // 2
        spec = pl.BlockSpec((hr, nc), lambda k, mr: (0, k)) if kind == "col" else pl.BlockSpec((1, hr, nc), lambda k, mr: (k, 0, 0))
        in_specs += [spec, spec]
        out_specs += [pl.BlockSpec((hr, nc), lambda k, mr: (0, 0)), pl.BlockSpec((1, hr, nc), lambda k, mr: (k, 0, 0))]
        out_shape += [SDS((hr, nc), F32), SDS((NCHIP, hr, nc), BF16)]
    outs = pl.pallas_call(
        body, name=name, out_shape=out_shape,
        grid_spec=pltpu.PrefetchScalarGridSpec(num_scalar_prefetch=1, grid=(NCHIP,), in_specs=in_specs, out_specs=out_specs),
        compiler_params=_cp(("arbitrary",), VMEM_CAP),
    )(me, *[v for pair in zip(mines, gots) for v in pair])
    return [(outs[2 * a], outs[2 * a + 1]) for a in range(n)]


def _reduce_mine(mines, gots):
    n = len(mines)
    _, _, c = _place()
    core = jnp.reshape(c, (1,)).astype(jnp.int32)
    tr = 256
    nsteps = W_CUTS[0][1] // 2 // tr

    def body(c_ref, *refs):
        ins, outs = refs[:2 * n], refs[2 * n:]

        def add(a):
            m_ref, g_ref = ins[2 * a], ins[2 * a + 1]
            outs[a][...] = ((m_ref[...] + g_ref[0].astype(F32)) + g_ref[1].astype(F32)) + g_ref[2].astype(F32)

        add(0)

        @pl.when(pl.program_id(0) == 0)
        def _():
            for a in range(1, n):
                add(a)

    nc0 = W_CUTS[0][2]
    in_specs = [pl.BlockSpec((tr, nc0), lambda i, cr: (i, 0)), pl.BlockSpec((3, tr, nc0), lambda i, cr: (0, i, 0))]
    out_specs = [pl.BlockSpec((tr, nc0), lambda i, cr: (cr[0] * nsteps + i, 0))]
    for a in range(1, n):
        hr, nc = W_CUTS[a][1] // 2, W_CUTS[a][2]
        in_specs += [pl.BlockSpec((hr, nc), lambda i, cr: (0, 0)), pl.BlockSpec((3, hr, nc), lambda i, cr: (0, 0, 0))]
        out_specs.append(pl.BlockSpec((hr, nc), lambda i, cr: (cr[0], 0)))
    return pl.pallas_call(
        body, name="reduce_mine", out_shape=[SDS((W_CUTS[a][1], W_CUTS[a][2]), F32) for a in range(n)],
        grid_spec=pltpu.PrefetchScalarGridSpec(num_scalar_prefetch=1, grid=(nsteps,), in_specs=in_specs, out_specs=out_specs),
        compiler_params=_cp(("arbitrary",), VMEM_CAP),
    )(core, *[v for pair in zip(mines, gots) for v in pair])


def _join_halves(fulls, packed):
    n = len(fulls)
    r, cdim = packed.shape

    def body(*refs):
        v_ref, full, out_ref = refs[n], refs[n + 1:2 * n + 1], refs[2 * n + 1]
        send_sems, recv_sems, g_send, g_recv, local_sem = refs[2 * n + 2:]
        x, y, c = _place()
        sibling = (x, y, 1 - c)
        me = 4 * x + 2 * y + c
        peers = [(x, y, 1 - c), (1 - x, y, c), (x, 1 - y, c), (1 - x, 1 - y, c),
                 (1 - x, y, 1 - c), (x, 1 - y, 1 - c), (1 - x, 1 - y, 1 - c)]
        mine = pltpu.make_async_copy(v_ref, out_ref.at[me], local_sem)
        mine.start()

        def copy(k, block, to):
            return pltpu.make_async_remote_copy(src_ref=v_ref, dst_ref=out_ref.at[block], send_sem=g_send.at[k],
                                                recv_sem=g_recv.at[k], device_id=to, device_id_type=MESH)

        def swap(a, half):
            rows = _half_rows(full[a], W_CUTS[a], half)
            return pltpu.make_async_remote_copy(src_ref=rows, dst_ref=rows, send_sem=send_sems.at[a], recv_sem=recv_sems.at[a],
                                                device_id=sibling, device_id_type=MESH)

        gathers = [copy(k, me, p) for k, p in enumerate(peers)]
        for cp in gathers:
            cp.start()
        sends = [swap(a, c) for a in range(n)]
        for cp in sends:
            cp.start()
        for a, cp in enumerate(sends):
            cp.wait_send()
            swap(a, 1 - c).wait_recv()
        for k, (px, py, pc) in enumerate(peers):
            copy(k, 4 * px + 2 * py + pc, (px, py, pc)).wait_recv()
        for cp in gathers:
            cp.wait_send()
        mine.wait()

    outs = pl.pallas_call(
        body, name="join_grad_halves", out_shape=[SDS((W_CUTS[a][1], W_CUTS[a][2]), F32) for a in range(n)] + [SDS((8, r, cdim), F32)],
        in_specs=[ANY] * (n + 1), out_specs=[ANY] * (n + 1),
        scratch_shapes=[pltpu.SemaphoreType.DMA((n,)), pltpu.SemaphoreType.DMA((n,)), pltpu.SemaphoreType.DMA((7,)),
                        pltpu.SemaphoreType.DMA((7,)), pltpu.SemaphoreType.DMA(())],
        input_output_aliases={a: a for a in range(n)}, compiler_params=_cp(side=True),
    )(*fulls, packed)
    return outs[:n], outs[n]


def _ada_forward(c_all, w_ada, b_cols):
    nb, nc = c_all.shape[0], w_ada.shape[1]

    def body(c_ref, w_ref, b_ref, o_ref):
        cv = c_ref[...]
        sc = (cv * _sig(cv)).astype(BF16)
        o_ref[...] = _dot(sc, w_ref[...].astype(BF16)) + b_ref[...]

    return pl.pallas_call(body, name="ada_forward", out_shape=SDS((nb, nc), F32), compiler_params=_cp(vmem=VMEM_CAP // 2))(c_all, w_ada, b_cols)


def _ada_backward(c_all, dmod_cols, dmod_all):
    nb, nc = dmod_cols.shape

    def body(c_ref, d_ref, a_ref, gw_ref, gb_ref):
        cv = c_ref[...]
        sc = (cv * _sig(cv)).astype(BF16)
        gw_ref[...] = _dot_tn(sc, d_ref[...].astype(BF16))
        gb_ref[...] = jnp.sum(a_ref[...], axis=0, keepdims=True)

    return pl.pallas_call(body, name="ada_backward", out_shape=[SDS((D, nc), F32), SDS((1, dmod_all.shape[1]), F32)],
                          compiler_params=_cp(vmem=VMEM_CAP // 2))(c_all, dmod_cols, dmod_all)


def _modulate(x2, sc1p, shift, seq, tm=512):
    t = x2.shape[0]
    spt = seq // tm

    def body(x_ref, sc_ref, sh_ref, h_ref, ht_ref):
        h = x_ref[...] * sc_ref[0] + sh_ref[0]
        h_ref[...] = h.astype(BF16)
        ht_ref[...] = h.T.astype(BF16)

    per_seq = pl.BlockSpec((1, 1, D), lambda i: (i // spt, 0, 0))
    return pl.pallas_call(
        body, name="modulate", out_shape=[HBM_OUT((t, D), BF16), HBM_OUT((D, t), BF16)], grid=(t // tm,),
        in_specs=[pl.BlockSpec((tm, D), lambda i: (i, 0)), per_seq, per_seq],
        out_specs=[pl.BlockSpec((tm, D), lambda i: (i, 0)), pl.BlockSpec((D, tm), lambda i: (0, i))],
        compiler_params=_cp(("parallel",)),
    )(_in_hbm(x2), sc1p, shift)


TW = 256
TPS = NCOL // NCHIP // TW
NT = NCOL // TW
NQKV_T = 3 * QW // TW
N_TILE_SEMS = 2 * 3 * TPS


def _tile_tables():
    tabs = np.zeros((NCHIP, 3, NT), np.int32)
    for me in range(NCHIP):
        tiles = [TPS * (me ^ (s // TPS)) + s % TPS for s in range(NT)]
        tabs[me, 0] = tiles
        for row, (lo, hi) in enumerate(((0, NQKV_T), (NQKV_T, NT))):
            mine = [w - lo if lo <= w < hi else None for w in tiles]
            held = next(m for m in mine if m is not None)
            for s, m in enumerate(mine):
                held = held if m is None else m
                tabs[me, 1 + row, s] = held
    return tabs


def _project_gather(h, fulls, tab):
    t = h.shape[0]
    n = len(fulls)

    def body(tab_ref, h_ref, *rest):
        qkv_ref, g_ref = rest[n], rest[n + 1]
        full = rest[n + 2:2 * n + 2]
        w_buf, tile_sems, send_sems, recv_sems = rest[2 * n + 2:]
        s = pl.program_id(0)
        x, y, c = _place()
        me = 2 * x + y
        peers = [(x, 1 - y), (1 - x, y), (1 - x, 1 - y)]
        sibling = (x, y, 1 - c)

        def hop(a, r, stage, chip, half, to):
            window = _shard_window(full[a], W_CUTS[a], chip, half)
            k = N_TILE_SEMS + 6 * (a - 1) + 2 * r + stage
            return pltpu.make_async_remote_copy(src_ref=window, dst_ref=window, send_sem=send_sems.at[k], recv_sem=recv_sems.at[k],
                                                device_id=to, device_id_type=MESH)

        def tile_hop(q, stage, col_step, half, to):
            col = pl.multiple_of(tab_ref[0, col_step] * TW, TW)
            window = full[0].at[pl.ds(pl.multiple_of(half * (D // 2), 16), D // 2), pl.ds(col, TW)]
            k = 2 * (q - TPS) + stage
            return pltpu.make_async_remote_copy(src_ref=window, dst_ref=window, send_sem=send_sems.at[k], recv_sem=recv_sems.at[k],
                                                device_id=to, device_id_type=MESH)

        def send_tile(r, j):
            return tile_hop(TPS * (r + 1) + j, 0, j, c, (*peers[r], c))

        def pass_on(q, to):
            return tile_hop(3 * TPS + q % TPS, 0, q, c, to)

        def arrive(a, r):
            px, py = peers[r]
            chip = 2 * px + py
            hop(a, r, 0, chip, c, (px, py, c)).wait_recv()
            hop(a, r, 1, chip, c, sibling).start()
            hop(a, r, 1, chip, 1 - c, sibling).wait_recv()

        def tile(step, slot):
            col = pl.multiple_of(tab_ref[0, step] * TW, TW)
            return pltpu.make_async_copy(full[0].at[:, pl.ds(col, TW)], w_buf.at[slot], tile_sems.at[slot])

        @pl.when(s == 0)
        def _():
            for r in range(2):
                for j in range(TPS):
                    send_tile(r, j).start()
            tile(0, 0).start()

        @pl.when((s + 1 >= TPS) & (s + 1 < NT))
        def _():
            tile_hop(s + 1, 1, s + 1, 1 - c, sibling).wait_recv()

        @pl.when(s + 1 < NT)
        def _():
            tile(s + 1, 1 - (s % 2)).start()

        @pl.when((s + 2 >= TPS) & (s + 2 < NT))
        def _():
            tile_hop(s + 2, 0, s + 2, c, sibling).wait_recv()
            tile_hop(s + 2, 1, s + 2, c, sibling).start()

        for r in range(2):
            @pl.when(((s + 2) // TPS == r + 1) & ((s + 2) % 2 == (r + 1 + TPS * (r + 1)) % 2))
            def _(r=r):
                pass_on(s + 2, (*peers[1 - r], c)).start()

        @pl.when(s + 2 == 2 * TPS - 1)
        def _():
            for a in range(1, n):
                for r in range(3):
                    hop(a, r, 0, me, c, (*peers[r], c)).start()

        slot = s % 2
        tile(s, slot).wait()
        is_qkv = tab_ref[0, s] < NQKV_T
        for k in range(2):
            @pl.when(slot == k)
            def _(k=k):
                acc = _dot(h_ref[...], w_buf[k])

                @pl.when(is_qkv)
                def _():
                    qkv_ref[...] = acc.astype(BF16)

                @pl.when(jnp.logical_not(is_qkv))
                def _():
                    g_ref[...] = acc.astype(BF16)

        @pl.when(s == NT - 1)
        def _():
            for a in range(1, n):
                for r in range(3):
                    arrive(a, r)
            for r in range(3):
                for j in range(TPS):
                    send_tile(r, j).wait_send()
                    tile_hop(TPS * (r + 1) + j, 1, TPS * (r + 1) + j, c, sibling).wait_send()
                for a in range(1, n):
                    hop(a, r, 0, me, c, (*peers[r], c)).wait_send()
                    px, py = peers[r]
                    hop(a, r, 1, 2 * px + py, c, sibling).wait_send()

    n_sems = N_TILE_SEMS + 6 * (n - 1)
    outs = pl.pallas_call(
        body, name="project_gather", out_shape=[HBM_OUT((t, 3 * QW), BF16), HBM_OUT((t, NGATE), BF16)] + [SDS(s, BF16) for s in W_FULL],
        grid_spec=pltpu.PrefetchScalarGridSpec(
            num_scalar_prefetch=1, grid=(NT,),
            in_specs=[pl.BlockSpec((t, D), lambda s, tab: (0, 0))] + [ANY] * n,
            out_specs=[pl.BlockSpec((t, TW), lambda s, tab: (0, tab[1, s])), pl.BlockSpec((t, TW), lambda s, tab: (0, tab[2, s]))] + [ANY] * n,
            scratch_shapes=[pltpu.VMEM((2, D, TW), BF16), pltpu.SemaphoreType.DMA((2,)),
                            pltpu.SemaphoreType.DMA((n_sems,)), pltpu.SemaphoreType.DMA((n_sems,))]),
        input_output_aliases={2 + a: 2 + a for a in range(n)},
        compiler_params=_cp(("arbitrary",), VMEM_CAP, side=True),
    )(tab, _in_hbm(h), *fulls)
    return outs[0], outs[1], outs[2:]


def _bias_tables(rel_bias, buckets):
    def body(tab_ref, bk_ref, o_ref):
        a = lax.broadcasted_iota(jnp.int32, (BLK, 2 * BLK), 0)
        b = lax.broadcasted_iota(jnp.int32, (BLK, 2 * BLK), 1)
        steps = a + BLK - b
        valid = (steps >= 0) & (steps <= BLK)
        for g in range(3):
            bk = bk_ref[g]
            for j in range(4):
                def pick(kk, acc, bk=bk, col=4 * g + j):
                    return jnp.where(bk == kk, tab_ref[kk, col], acc)

                acc = lax.fori_loop(0, N_BUCKETS, pick, jnp.zeros((BLK, 2 * BLK), F32))
                o_ref[g, j] = jnp.where(valid, acc, NEG)

    return pl.pallas_call(
        body, name="bias_tables", out_shape=SDS((3, 4, BLK, 2 * BLK), F32),
        in_specs=[pl.BlockSpec(memory_space=pltpu.SMEM), VMEM_SPEC], out_specs=VMEM_SPEC,
    )(rel_bias, buckets)


def _bias_grad(ds_sum, buckets):
    def body(ds_ref, bk_ref, o_ref, part_ref):
        lane = lax.broadcasted_iota(jnp.int32, (N_BUCKETS, 128), 1)
        for g in range(3):
            def bucket(kk, carry, g=g):
                mine = bk_ref[g] == kk
                for j in range(4):
                    v = jnp.sum(jnp.where(mine, ds_ref[g, j], 0.0).reshape(BLK // 8, 8, 2 * BLK), axis=0)
                    part_ref[j, pl.ds(pl.multiple_of(kk * 8, 8), 8), :] = v[:, :BLK] + v[:, BLK:]
                return carry

            lax.fori_loop(0, N_BUCKETS, bucket, 0)
            out = jnp.zeros((N_BUCKETS, 128), F32)
            for j in range(4):
                rows = jnp.sum(part_ref[j], axis=1, keepdims=True)
                out = jnp.where(lane == j, jnp.sum(rows.reshape(N_BUCKETS, 8, 1), axis=1), out)
            o_ref[g] = out

    return pl.pallas_call(body, name="bias_grad", out_shape=SDS((3, N_BUCKETS, 128), F32), in_specs=[VMEM_SPEC, VMEM_SPEC],
                          out_specs=VMEM_SPEC, scratch_shapes=[pltpu.VMEM((4, N_BUCKETS * 8, 128), F32)])(ds_sum, buckets)


def _sub_rows(d, r, first, size):
    return pl.ds(first * d + r, size) if d == 1 else pl.ds(first * d + r, size, stride=d)


def _head_spec(seq, g, part):
    return pl.BlockSpec((seq, HD), lambda b, hh: (b, part * (QW // HD) + 4 * g + hh))


def _rows(start, count, stride):
    return pl.ds(start, count) if stride == 1 else pl.ds(start, count, stride=stride)


def _gather_rows(dst, dst0, src, src0, stride, count):
    for first in range(0, count, BLK):
        dst[pl.ds(dst0 + first, BLK), :] = src[_rows(src0 + first * stride, BLK, stride), :].astype(dst.dtype)


def _scatter_rows(dst, dst0, stride, src, src0, count):
    for first in range(0, count, BLK):
        dst[_rows(dst0 + first * stride, BLK, stride), :] = src[pl.ds(src0 + first, BLK), :].astype(dst.dtype)


def _by_subsequence(dst, src, d, wide=None, tmp=None):
    seq = src.shape[0]
    ln = seq // d
    if wide is not None:
        wide[...] = src[...].astype(F32)
        src = wide
    if d <= 4:
        for r in range(d):
            _gather_rows(dst, r * ln, src, r, d, ln)
    else:
        quarter = seq // 4
        for r4 in range(4):
            _gather_rows(tmp, r4 * quarter, src, r4, 4, quarter)
        for r4 in range(4):
            for a in range(d // 4):
                _gather_rows(dst, (4 * a + r4) * ln, tmp, r4 * quarter + a, d // 4, ln)


def _to_sequence(dst, src, d, tmp=None):
    seq = dst.shape[0]
    ln = seq // d
    if d <= 4:
        for r in range(d):
            _scatter_rows(dst, r, d, src, r * ln, ln)
    else:
        quarter = seq // 4
        for r4 in range(4):
            for a in range(d // 4):
                _scatter_rows(tmp, r4 * quarter + a, d // 4, src, (4 * a + r4) * ln, ln)
        for r4 in range(4):
            _scatter_rows(dst, r4, 4, tmp, r4 * quarter, quarter)


def _attn_forward(g, qkv, bias, bsz, seq):
    d = DILATIONS[g]
    ln = seq // d
    units = [(r, n) for r in range(d) for n in range(ln // BLK)]

    def band(n):
        return slice(BLK, 2 * BLK) if n == 0 else slice(0, 2 * BLK)

    def body(q_ref, k_ref, v_ref, b_ref, o_ref, l_ref, *scratch):
        hs = pl.program_id(1)
        s_scr, p_scr = scratch[:2]
        if d == 1:
            qd, kd, vd = q_ref, k_ref, v_ref
        else:
            wide, tmp, qd, kd, vd = scratch[2:7]
            for dst, src in ((qd, q_ref), (kd, k_ref), (vd, v_ref)):
                _by_subsequence(dst, src, d, wide, tmp)
        blk = lambda r, n: pl.ds(r * ln + n * BLK, BLK)
        direct = d <= 4
        out_rows = (lambda r, n: _sub_rows(d, r, n * BLK, BLK)) if direct else blk
        o_dst, l_dst = (o_ref, l_ref) if direct else scratch[7:9]
        for u, (r, n) in enumerate(units):
            s_scr[u, :, BLK:] = _dot_nt(qd[blk(r, n), :], kd[blk(r, n), :])
            if n > 0:
                s_scr[u, :, :BLK] = _dot_nt(qd[blk(r, n), :], kd[blk(r, n - 1), :])
        for u, (r, n) in enumerate(units):
            s = s_scr[u, :, band(n)] * SCALE + b_ref[hs, :, band(n)]
            m = jnp.max(s, axis=1, keepdims=True)
            e = jnp.exp(s - m)
            den = jnp.sum(e, axis=1, keepdims=True)
            p_scr[u, :, band(n)] = (e * (1.0 / den)).astype(BF16)
            l_dst[out_rows(r, n), :] = jnp.broadcast_to(m + jnp.log(den), (BLK, HD))
        for u, (r, n) in enumerate(units):
            acc = _dot(p_scr[u, :, BLK:], vd[blk(r, n), :])
            if n > 0:
                acc = acc + _dot(p_scr[u, :, :BLK], vd[blk(r, n - 1), :])
            o_dst[out_rows(r, n), :] = acc
        if not direct:
            _to_sequence(o_ref, o_dst, d, tmp)
            _to_sequence(l_ref, l_dst, d, tmp)

    rows_f32, rows_bf16 = pltpu.VMEM((seq, HD), F32), pltpu.VMEM((seq, HD), BF16)
    regrouped = [] if d == 1 else [rows_f32] * 2 + [rows_bf16] * 3 + ([] if d <= 4 else [rows_f32] * 2)
    out_spec = pl.BlockSpec((seq, HD), lambda b, hh: (b, hh))
    return pl.pallas_call(
        body, name=f"attn_forward_{g}", out_shape=[HBM_OUT((bsz * seq, AW), F32)] * 2, grid=(bsz, 4),
        in_specs=[_head_spec(seq, g, part) for part in range(3)] + [pl.BlockSpec((4, BLK, 2 * BLK), lambda b, hh: (0, 0, 0))],
        out_specs=[out_spec, out_spec],
        scratch_shapes=[pltpu.VMEM((len(units), BLK, 2 * BLK), F32), pltpu.VMEM((len(units), BLK, 2 * BLK), BF16)] + regrouped,
        compiler_params=_cp(("parallel", "parallel"), VMEM_CAP // 2),
    )(qkv, qkv, qkv, _in_hbm(bias))


def _attn_backward(g, qkv, do, dl, bias, prev_out, bsz, seq):
    d = DILATIONS[g]
    ln = seq // d
    units = [(r, n) for r in range(d) for n in range(ln // BLK)]

    def body(q_ref, k_ref, v_ref, do_ref, dl_ref, b_ref, *rest):
        dq_ref, dk_ref, dv_ref, db_ref = rest[-18:-14]
        wide, tmp, qd, kd, vd, dod, dld, dqd, dkd, dvd, s_scr, dp_scr, p_scr, ds_scr = rest[-14:]
        hs = pl.program_id(1)

        @pl.when((pl.program_id(0) == 0) & (hs == 0))
        def _():
            db_ref[...] = jnp.zeros_like(db_ref)

        for dst, src in ((qd, q_ref), (kd, k_ref), (vd, v_ref)):
            _by_subsequence(dst, src, d, wide, tmp)
        _by_subsequence(dod, do_ref, d, None, tmp)
        _by_subsequence(dld, dl_ref, d, None, tmp)
        dkd[...] = jnp.zeros_like(dkd)
        dvd[...] = jnp.zeros_like(dvd)
        blk = lambda r, n: pl.ds(r * ln + n * BLK, BLK)
        keys = lambda r, n: [(blk(r, n), slice(BLK, 2 * BLK))] + ([(blk(r, n - 1), slice(0, BLK))] if n > 0 else [])
        for u, (r, n) in enumerate(units):
            for rows, band in keys(r, n):
                s_scr[u, :, band] = _dot_nt(qd[blk(r, n), :], kd[rows, :])
                dp_scr[u, :, band] = _dot_nt(dod[blk(r, n), :], vd[rows, :])
        for u, (r, n) in enumerate(units):
            both = dld[blk(r, n), :]
            lse, delta = both[:, 0:1], both[:, 64:65]
            band = slice(BLK, 2 * BLK) if n == 0 else slice(0, 2 * BLK)
            p = jnp.exp(s_scr[u, :, band] * SCALE + b_ref[hs, :, band] - lse)
            ds = p * (dp_scr[u, :, band] - delta)
            p_scr[u, :, band] = p.astype(BF16)
            ds_scr[u, :, band] = ds.astype(BF16)
            db_ref[hs, :, band] += ds
        for u, (r, n) in enumerate(units):
            dq = jnp.zeros((BLK, HD), F32)
            for rows, band in keys(r, n):
                dvd[rows, :] += _dot_tn(p_scr[u, :, band], dod[blk(r, n), :])
                dkd[rows, :] += _dot_tn(ds_scr[u, :, band], qd[blk(r, n), :]) * SCALE
                dq = dq + _dot(ds_scr[u, :, band], kd[rows, :])
            dqd[blk(r, n), :] = dq * SCALE
        for out, acc in ((dq_ref, dqd), (dk_ref, dkd), (dv_ref, dvd)):
            if d == 1:
                out[...] = acc[...].astype(BF16)
            else:
                _to_sequence(wide, acc, d, tmp)
                out[...] = wide[...].astype(BF16)

    qkv_spec = _head_spec(seq, g, 0)
    out_spec = pl.BlockSpec((seq, HD), lambda b, hh: (b, hh))
    band_spec = pl.BlockSpec((4, BLK, 2 * BLK), lambda b, hh: (0, 0, 0))
    ins = [qkv, qkv, qkv, _in_hbm(do), _in_hbm(dl), _in_hbm(bias)]
    in_specs = [_head_spec(seq, g, part) for part in range(3)] + [out_spec, out_spec, band_spec]
    aliases = {}
    if prev_out is not None:
        ins += list(prev_out)
        in_specs += [ANY] * 3
        aliases = {6: 0, 7: 1, 8: 2}
    rows_bf16, rows_f32 = pltpu.VMEM((seq, HD), BF16), pltpu.VMEM((seq, HD), F32)
    staged = [pltpu.VMEM((len(units), BLK, 2 * BLK), F32)] * 2 + [pltpu.VMEM((len(units), BLK, 2 * BLK), BF16)] * 2
    dq, dk, dv, db = pl.pallas_call(
        body, name=f"attn_backward_{g}", out_shape=[HBM_OUT((bsz * seq, QW), BF16)] * 3 + [SDS((4, BLK, 2 * BLK), F32)], grid=(bsz, 4),
        in_specs=in_specs, out_specs=[qkv_spec] * 3 + [band_spec], input_output_aliases=aliases,
        scratch_shapes=[rows_f32] * 2 + [rows_bf16] * 4 + [rows_f32] * 4 + staged,
        compiler_params=_cp(("arbitrary", "arbitrary"), VMEM_CAP // 2),
    )(*ins)
    return (dq, dk, dv), db


def _mix_forward(gates, og, lg, x2, tgt, gate, w_ao, w_co, w_o, conv_w, conv_b, ln_g, ln_b, bsz, seq, tm=256):
    t = x2.shape[0]
    spt = seq // tm

    def body(g_ref, o1, o2, o3, l1, l2, l3, x_ref, t_ref, gate_ref, wao_ref, wco_ref, wo_ref, cw_ref, cb_ref, lng_ref, lnb_ref,
             ain_ref, sin_ref, mrg_ref, dy_ref, aout_ref, sout_ref, yc_ref, o_ref, lj_ref, dxr_ref, vec_ref, dgate_ref, zc_ref):
        b, i = pl.program_id(0), pl.program_id(1)

        @pl.when((b == 0) & (i == 0))
        def _():
            vec_ref[...] = jnp.zeros_like(vec_ref)

        @pl.when(i == 0)
        def _():
            zc_ref[...] = jnp.zeros_like(zc_ref)
            dgate_ref[...] = jnp.zeros_like(dgate_ref)

        g_attn, u, bg, cg, g_conv, m_attn, m_conv = (g_ref[:, lo:hi].astype(F32) for lo, hi in GATE_COLS)
        la, lb, lc = l1[...], l2[...], l3[...]
        mx = jnp.maximum(la, jnp.maximum(lb, lc))
        ea, eb, ec = jnp.exp(la - mx), jnp.exp(lb - mx), jnp.exp(lc - mx)
        den = ea + eb + ec
        o = (ea * o1[...] + eb * o2[...] + ec * o3[...]) / den
        o_ref[...] = o
        lj_ref[...] = mx + jnp.log(den)
        a_in = o * (g_attn * _sig(g_attn))
        ain_ref[...] = a_in.astype(BF16)
        a_out = _dot(a_in.astype(BF16), wao_ref[...])
        aout_ref[...] = a_out.astype(BF16)
        z = cg * u
        rows = lax.broadcasted_iota(jnp.int32, (tm, D), 0)
        c6, c7 = zc_ref[6:7, :], zc_ref[7:8, :]
        z1 = jnp.where(rows == 0, c7, pltpu.roll(z, 1, 0))
        z2 = jnp.where(rows == 0, c6, jnp.where(rows == 1, c7, pltpu.roll(z, 2, 0)))
        zc_ref[...] = z[tm - 8:tm, :]
        y_conv = (cw_ref[0:1, :] * z2 + cw_ref[1:2, :] * z1 + cw_ref[2:3, :] * z) + cb_ref[...]
        yc_ref[...] = y_conv.astype(BF16)
        s_in = bg * y_conv * (g_conv * _sig(g_conv))
        sin_ref[...] = s_in.astype(BF16)
        s_out = _dot(s_in.astype(BF16), wco_ref[...])
        sout_ref[...] = s_out.astype(BF16)
        merged = _sig(m_attn) * a_out + _sig(m_conv) * s_out
        mrg_ref[...] = merged.astype(BF16)
        y = _dot(merged.astype(BF16), wo_ref[...])
        gate1 = 1.0 + gate_ref[0]
        r = ALPHA * x_ref[...] + gate1 * y
        mu = jnp.mean(r, axis=1, keepdims=True)
        rc = r - mu
        rstd = lax.rsqrt(jnp.mean(rc * rc, axis=1, keepdims=True) + LN_EPS)
        xhat = rc * rstd
        diff = (xhat * lng_ref[...] + lnb_ref[...]) - t_ref[...]
        dout = diff * (1.0 / D)
        vec_ref[0:1, :] += jnp.sum(dout * xhat, axis=0, keepdims=True)
        vec_ref[1:2, :] += jnp.sum(dout, axis=0, keepdims=True)
        vec_ref[2:3, :] += jnp.sum(diff * diff, axis=0, keepdims=True)
        dxh = dout * lng_ref[...]
        dr = rstd * (dxh - jnp.mean(dxh, axis=1, keepdims=True) - xhat * jnp.mean(dxh * xhat, axis=1, keepdims=True))
        dxr_ref[...] = ALPHA * dr
        dy_ref[...] = (dr * gate1).astype(BF16)
        dgate_ref[0] += jnp.sum(dr * y, axis=0, keepdims=True)

    tok = lambda w: pl.BlockSpec((tm, w), lambda b, i: (b * spt + i, 0))
    const = lambda s: pl.BlockSpec(s, lambda b, i: (0,) * len(s))
    per_seq = pl.BlockSpec((1, 1, D), lambda b, i: (b, 0, 0))
    outs = pl.pallas_call(
        body, name="mix_forward", grid=(bsz, spt),
        out_shape=[HBM_OUT((t, AW), BF16), HBM_OUT((t, D), BF16), HBM_OUT((t, D), BF16), HBM_OUT((t, D), BF16), HBM_OUT((t, D), BF16),
                   HBM_OUT((t, D), BF16), HBM_OUT((t, D), BF16), HBM_OUT((t, AW), F32), HBM_OUT((t, AW), F32), HBM_OUT((t, D), F32),
                   SDS((8, D), F32), SDS((bsz, 1, D), F32)],
        in_specs=[tok(NGATE)] + [tok(AW)] * 6 + [tok(D), tok(D), per_seq, const((AW, D)), const((D, D)), const((D, D)),
                                                 const((3, D)), const((1, D)), const((1, D)), const((1, D))],
        out_specs=[tok(AW), tok(D), tok(D), tok(D), tok(D), tok(D), tok(D), tok(AW), tok(AW), tok(D), const((8, D)), per_seq],
        scratch_shapes=[pltpu.VMEM((8, D), F32)],
        compiler_params=_cp(("arbitrary", "arbitrary"), VMEM_CAP),
    )(_in_hbm(gates), *map(_in_hbm, og), *map(_in_hbm, lg), _in_hbm(x2), _in_hbm(tgt), gate, w_ao, w_co, w_o, conv_w, conv_b, ln_g, ln_b)
    return outs


def _mix_backward(gates, dy, a_out, s_out, y_conv, o, lj, w_ao, w_co, w_o, conv_w, vec_f, bsz, seq, tm=256):
    t = dy.shape[0]
    spt = seq // tm

    def body(g_ref, dy_ref, aout_ref, sout_ref, yc_ref, o_ref, lj_ref, wao_ref, wco_ref, wo_ref, cw_ref, vecf_ref,
             dg_ref, do_ref, dl_ref, daout_ref, dsout_ref, vec_ref, car_ref):
        b, i = pl.program_id(0), pl.program_id(1)

        @pl.when((b == 0) & (i == 0))
        def _():
            vec_ref[...] = vecf_ref[...]

        @pl.when(i == 0)
        def _():
            car_ref[...] = jnp.zeros_like(car_ref)

        g_attn, u, bg, cg, g_conv, m_attn, m_conv = (g_ref[:, lo:hi].astype(F32) for lo, hi in GATE_COLS)
        dmerged = _dot_nt(dy_ref[...], wo_ref[...])
        sa, sc = _sig(m_attn), _sig(m_conv)
        da_out = (dmerged * sa).astype(BF16)
        ds_out = (dmerged * sc).astype(BF16)
        daout_ref[...] = da_out
        dsout_ref[...] = ds_out
        dg_ref[:, 4608:5632] = (dmerged * aout_ref[...].astype(F32) * (sa * (1.0 - sa))).astype(BF16)
        dg_ref[:, 5632:6656] = (dmerged * sout_ref[...].astype(F32) * (sc * (1.0 - sc))).astype(BF16)
        da_in = _dot_nt(da_out, wao_ref[...])
        ds_in = _dot_nt(ds_out, wco_ref[...])
        sga = _sig(g_attn)
        o = o_ref[...]
        do = da_in * (g_attn * sga)
        do_ref[...] = do
        dg_ref[:, 0:512] = (da_in * o * (sga * (1.0 + g_attn * (1.0 - sga)))).astype(BF16)
        prod = do * o
        lane = lax.broadcasted_iota(jnp.int32, (tm, HD), 1)
        for j in range(4):
            cs = slice(j * HD, (j + 1) * HD)
            delta = jnp.sum(prod[:, cs], axis=1, keepdims=True)
            dl_ref[:, cs] = jnp.where(lane < 64, lj_ref[:, cs], delta)
        sgc = _sig(g_conv)
        silu_c = g_conv * sgc
        yc = yc_ref[...].astype(F32)
        dg_ref[:, 1536:2560] = (ds_in * yc * silu_c).astype(BF16)
        dg_ref[:, 3584:4608] = (ds_in * bg * yc * (sgc * (1.0 + g_conv * (1.0 - sgc)))).astype(BF16)
        dyc = ds_in * bg * silu_c
        rows = lax.broadcasted_iota(jnp.int32, (tm, D), 0)
        c0, c1 = car_ref[0:1, :], car_ref[1:2, :]
        n1 = jnp.where(rows == tm - 1, c0, pltpu.roll(dyc, tm - 1, 0))
        n2 = jnp.where(rows == tm - 2, c0, jnp.where(rows == tm - 1, c1, pltpu.roll(dyc, tm - 2, 0)))
        car_ref[...] = dyc[0:8, :]
        dz = cw_ref[2:3, :] * dyc + cw_ref[1:2, :] * n1 + cw_ref[0:1, :] * n2
        z = cg * u
        dg_ref[:, 512:1536] = (dz * cg).astype(BF16)
        dg_ref[:, 2560:3584] = (dz * u).astype(BF16)
        vec_ref[3:4, :] += jnp.sum(n2 * z, axis=0, keepdims=True)
        vec_ref[4:5, :] += jnp.sum(n1 * z, axis=0, keepdims=True)
        vec_ref[5:6, :] += jnp.sum(dyc * z, axis=0, keepdims=True)
        vec_ref[6:7, :] += jnp.sum(dyc, axis=0, keepdims=True)

    tok = lambda w: pl.BlockSpec((tm, w), lambda b, i: (b * spt + (spt - 1 - i), 0))
    const = lambda s: pl.BlockSpec(s, lambda b, i: (0,) * len(s))
    return pl.pallas_call(
        body, name="mix_backward", grid=(bsz, spt),
        out_shape=[HBM_OUT((t, NGATE), BF16), HBM_OUT((t, AW), F32), HBM_OUT((t, AW), F32), HBM_OUT((t, D), BF16), HBM_OUT((t, D), BF16),
                   SDS((8, D), F32)],
        in_specs=[tok(NGATE), tok(D), tok(D), tok(D), tok(D), tok(AW), tok(AW), const((AW, D)), const((D, D)), const((D, D)), const((3, D)),
                  const((8, D))],
        out_specs=[tok(NGATE), tok(AW), tok(AW), tok(D), tok(D), const((8, D))],
        scratch_shapes=[pltpu.VMEM((8, D), F32)],
        compiler_params=_cp(("arbitrary", "arbitrary"), VMEM_CAP),
    )(*map(_in_hbm, (gates, dy, a_out, s_out, y_conv, o, lj)), w_ao, w_co, w_o, conv_w, vec_f)


def _scatter_copies(src, land, send_sems, recv_sems):
    x, y, c = _place()
    chips = [(1 - x, y), (x, 1 - y), (1 - x, 1 - y)]
    return [pltpu.make_async_remote_copy(src_ref=src[a].at[2 * cx + cy], dst_ref=land[a].at[r], send_sem=send_sems.at[3 * a + r],
                                         recv_sem=recv_sems.at[3 * a + r], device_id=(cx, cy, c), device_id_type=MESH)
            for a in range(len(src)) for r, (cx, cy) in enumerate(chips)]


def _halves_out(a):
    kind, nr, nc = W_CUTS[a]
    shape = (nr // 2, W_FULL[a][1]) if kind == "col" else (NCHIP, nr // 2, nc)
    return [SDS(shape, F32), SDS(shape, BF16)]


def _write_halves(a, acc_ref, c, mine_ref, theirs_ref):
    kind, nr, nc = W_CUTS[a]
    hr = nr // 2
    if kind == "col":
        mine_ref[...] = acc_ref[pl.ds(pl.multiple_of(c * hr, hr), hr), :]
        theirs_ref[...] = acc_ref[pl.ds(pl.multiple_of((1 - c) * hr, hr), hr), :].astype(BF16)
    else:
        for k in range(NCHIP):
            mine_ref[k] = acc_ref[pl.ds(pl.multiple_of(k * nr + c * hr, hr), hr), :]
            theirs_ref[k] = acc_ref[pl.ds(pl.multiple_of(k * nr + (1 - c) * hr, hr), hr), :].astype(BF16)


def _out_weight_grads(a_in, da_out, s_in, ds_out, merged, dy, core, tk=1024):
    t = dy.shape[0]
    nt = t // tk

    def body(c_ref, ain_ref, da_ref, sin_ref, ds_ref, m_ref, dy_ref, *rest):
        outs, (gao, gco, go) = rest[:6], rest[6:]

        @pl.when(pl.program_id(0) == 0)
        def _():
            gao[...] = jnp.zeros_like(gao)
            gco[...] = jnp.zeros_like(gco)
            go[...] = jnp.zeros_like(go)

        gao[...] += _dot_tn(ain_ref[...], da_ref[...])
        gco[...] += _dot_tn(sin_ref[...], ds_ref[...])
        go[...] += _dot_tn(m_ref[...], dy_ref[...])

        @pl.when(pl.program_id(0) == nt - 1)
        def _():
            for a, acc in ((1, gao), (2, gco), (3, go)):
                _write_halves(a, acc, c_ref[0], outs[2 * a - 2], outs[2 * a - 1])

    tok = lambda w: pl.BlockSpec((tk, w), lambda i, cr: (i, 0))
    out_shape = _halves_out(1) + _halves_out(2) + _halves_out(3)
    outs = pl.pallas_call(
        body, name="out_weight_grads", out_shape=out_shape,
        grid_spec=pltpu.PrefetchScalarGridSpec(
            num_scalar_prefetch=1, grid=(nt,), in_specs=[tok(AW), tok(D), tok(D), tok(D), tok(D), tok(D)],
            out_specs=[pl.BlockSpec(o.shape, lambda i, cr, nd=len(o.shape): (0,) * nd) for o in out_shape],
            scratch_shapes=[pltpu.VMEM((AW, D), F32), pltpu.VMEM((D, D), F32), pltpu.VMEM((D, D), F32)]),
        compiler_params=_cp(("arbitrary",), VMEM_CAP),
    )(core, a_in, da_out, s_in, ds_out, merged, dy)
    return [(outs[0], outs[1]), (outs[2], outs[3]), (outs[4], outs[5])]


def _input_grad(dq, dk, dv, dgates, w, x2, dxr, sc1p, seq, sums, tm=512):
    t = x2.shape[0]
    nt = t // tm
    spt = seq // tm
    bsz = t // seq
    n = len(sums)
    gblk = NGATE // 4
    nsteps = 3 + 4

    def body(dq_ref, dk_ref, dv_ref, dg_ref, wq_ref, wg_ref, x_ref, dxr_ref, sc_ref, *rest):
        src, (dx_ref, dsh_ref, dsc_ref), land = rest[:n], rest[n:n + 3], rest[n + 3:2 * n + 3]
        acc_ref, send_sems, recv_sems = rest[2 * n + 3:]
        j, i = pl.program_id(0), pl.program_id(1)
        copies = _scatter_copies(src, land, send_sems, recv_sems)
        rows = pl.ds(pl.multiple_of(i * tm, tm), tm)

        @pl.when((i == 0) & (j == 0))
        def _():
            for cp in copies:
                cp.start()

        for k, ref in enumerate((dq_ref, dk_ref, dv_ref)):
            @pl.when(j == k)
            def _(k=k, ref=ref):
                part = _dot_nt(ref[...], wq_ref[...])
                if k == 0:
                    acc_ref[rows, :] = part
                else:
                    acc_ref[rows, :] += part

        @pl.when((j >= 3) & (j < nsteps - 1))
        def _():
            acc_ref[rows, :] += _dot_nt(dg_ref[...], wg_ref[...])

        @pl.when(j == nsteps - 1)
        def _():
            dh = acc_ref[rows, :] + _dot_nt(dg_ref[...], wg_ref[...])
            dx_ref[...] = dh * sc_ref[0] + dxr_ref[...]

            @pl.when(i % spt == 0)
            def _():
                dsh_ref[...] = jnp.zeros_like(dsh_ref)
                dsc_ref[...] = jnp.zeros_like(dsc_ref)

            dsh_ref[0] += jnp.sum(dh, axis=0, keepdims=True)
            dsc_ref[0] += jnp.sum(dh * x_ref[...], axis=0, keepdims=True)

        @pl.when((i == nt - 1) & (j == nsteps - 1))
        def _():
            for cp in copies:
                cp.wait()

    def held(k):
        return lambda j, i: (jnp.where(j == k, i, jnp.where(j < k, 0, nt - 1)), 0)

    last = lambda j, i: (jnp.where(j == nsteps - 1, i, 0), 0)
    outs = pl.pallas_call(
        body, name="input_grad", grid=(nsteps, nt),
        out_shape=[SDS((t, D), F32), SDS((bsz, 1, D), F32), SDS((bsz, 1, D), F32)] + [SDS((3,) + s.shape[1:], BF16) for s in sums],
        in_specs=[pl.BlockSpec((tm, QW), held(0)), pl.BlockSpec((tm, QW), held(1)), pl.BlockSpec((tm, QW), held(2)),
                  pl.BlockSpec((tm, gblk), lambda j, i: (jnp.where(j >= 3, i, 0), jnp.clip(j - 3, 0, 3))),
                  pl.BlockSpec((D, QW), lambda j, i: (0, jnp.minimum(j, 2))),
                  pl.BlockSpec((pl.Element(D), pl.Element(gblk)), lambda j, i: (0, pl.multiple_of(3 * QW + gblk * jnp.clip(j - 3, 0, 3), 128))),
                  pl.BlockSpec((tm, D), last), pl.BlockSpec((tm, D), last),
                  pl.BlockSpec((1, 1, D), lambda j, i: (jnp.where(j == nsteps - 1, i // spt, 0), 0, 0))] + [ANY] * n,
        out_specs=[pl.BlockSpec((tm, D), last),
                   pl.BlockSpec((1, 1, D), lambda j, i: (jnp.where(j == nsteps - 1, i // spt, 0), 0, 0)),
                   pl.BlockSpec((1, 1, D), lambda j, i: (jnp.where(j == nsteps - 1, i // spt, 0), 0, 0))] + [ANY] * n,
        scratch_shapes=[pltpu.VMEM((t, D), F32), pltpu.SemaphoreType.DMA((3 * NCHIP,)), pltpu.SemaphoreType.DMA((3 * NCHIP,))],
        compiler_params=_cp(("arbitrary", "arbitrary"), VMEM_CAP, side=True),
    )(*map(_in_hbm, (dq, dk, dv, dgates, w, w, x2, dxr)), sc1p, *sums)
    return outs[0], outs[1], outs[2], outs[3:]


def _in_weight_grad(ht, dq, dk, dv, dgates, core, sums):
    t = ht.shape[1]
    hr = D // 2
    n = len(sums)

    def body(c_ref, ht_ref, dq_ref, dk_ref, dv_ref, dg_ref, *rest):
        src, mine_ref, got_ref, land = rest[:n], rest[n], rest[n + 1], rest[n + 2:2 * n + 2]
        acc_ref, their_buf, send_sems, recv_sems, tile_send, tile_recv = rest[2 * n + 2:]
        j = pl.program_id(0)
        slot = j % 2
        px, py, pc = _place()
        copies = _scatter_copies(src, land, send_sems, recv_sems)

        def to_sibling(step, k):
            return pltpu.make_async_remote_copy(src_ref=their_buf.at[k], dst_ref=got_ref.at[:, pl.ds(pl.multiple_of(step * TN, TN), TN)],
                                                send_sem=tile_send.at[k], recv_sem=tile_recv.at[0], device_id=(px, py, 1 - pc),
                                                device_id_type=MESH)

        @pl.when(j == 0)
        def _():
            for cp in copies:
                cp.start()

        @pl.when(j >= 2)
        def _():
            to_sibling(j - 2, slot).wait_send()

        for k, ref in enumerate((dq_ref, dk_ref, dv_ref)):
            @pl.when((j >= k * NQT) & (j < (k + 1) * NQT))
            def _(ref=ref):
                acc_ref[...] = _dot(ht_ref[...], ref[...])

        @pl.when(j >= 3 * NQT)
        def _():
            acc_ref[...] = _dot(ht_ref[...], dg_ref[...])

        _write_halves(0, acc_ref, c_ref[0], mine_ref, their_buf.at[slot])
        to_sibling(j, slot).start()

        @pl.when(j == NPT - 1)
        def _():
            to_sibling(j - 1, 1 - slot).wait_send()
            to_sibling(j, slot).wait_send()
            pltpu.make_async_remote_copy(src_ref=got_ref, dst_ref=got_ref, send_sem=tile_send.at[0], recv_sem=tile_recv.at[0],
                                         device_id=(px, py, 1 - pc), device_id_type=MESH).wait_recv()
            for cp in copies:
                cp.wait()

    def part(k):
        return pl.BlockSpec((t, TN), lambda j, cr: (0, jnp.clip(j - k * NQT, 0, NQT - 1)))

    outs = pl.pallas_call(
        body, name="in_weight_grad", out_shape=[SDS((hr, NCOL), F32), SDS((hr, NCOL), BF16)] + [SDS((3,) + v.shape[1:], BF16) for v in sums],
        grid_spec=pltpu.PrefetchScalarGridSpec(
            num_scalar_prefetch=1, grid=(NPT,),
            in_specs=[pl.BlockSpec((D, t), lambda j, cr: (0, 0)), part(0), part(1), part(2),
                      pl.BlockSpec((t, TN), lambda j, cr: (0, jnp.maximum(j - 3 * NQT, 0)))] + [ANY] * n,
            out_specs=[pl.BlockSpec((hr, TN), lambda j, cr: (0, j)), ANY] + [ANY] * n,
            scratch_shapes=[pltpu.VMEM((D, TN), F32), pltpu.VMEM((2, hr, TN), BF16),
                            pltpu.SemaphoreType.DMA((3 * NCHIP,)), pltpu.SemaphoreType.DMA((3 * NCHIP,)),
                            pltpu.SemaphoreType.DMA((2,)), pltpu.SemaphoreType.DMA((1,))]),
        compiler_params=_cp(("arbitrary",), VMEM_CAP, side=True),
    )(core, *map(_in_hbm, (ht, dq, dk, dv, dgates)), *sums)
    return outs[0], outs[1], outs[2:]


def _sum_partials(gathered):
    def body(g_ref, o_ref):
        acc = g_ref[0]
        for k in range(1, 8):
            acc = acc + g_ref[k]
        o_ref[...] = acc

    return pl.pallas_call(body, name="sum_partials", out_shape=SDS(gathered.shape[1:], F32), in_specs=[VMEM_SPEC], out_specs=VMEM_SPEC)(gathered)


def _adamw(w, g, m, v, name, tr=256):
    r, cdim = w.shape
    tr = tr if cdim <= D else tr // 2
    tr = tr if (r % tr == 0 and r > tr) else r

    def body(w_ref, g_ref, m_ref, v_ref, go_ref, d_ref, nm_ref, nv_ref):
        gv = g_ref[...]
        go_ref[...] = gv
        nm = B1 * m_ref[...] + (1.0 - B1) * gv
        nv = B2 * v_ref[...] + (1.0 - B2) * (gv * gv)
        m_hat = nm / (1.0 - B1 ** STEP)
        v_hat = nv / (1.0 - B2 ** STEP)
        d_ref[...] = -LR * (m_hat / (jnp.sqrt(v_hat) + EPS) + WD * w_ref[...])
        nm_ref[...] = nm
        nv_ref[...] = nv

    spec = pl.BlockSpec((tr, cdim), lambda i: (i, 0))
    return pl.pallas_call(
        body, name=name, grid=(r // tr,), out_shape=[SDS((r, cdim), F32)] * 4, in_specs=[spec] * 4, out_specs=[spec] * 4,
        compiler_params=_cp(("parallel",), VMEM_CAP // 2),
    )(w, g, m, v)


def _t5_bucket(dist):
    n = jnp.maximum(dist, 1).astype(F32)
    large = MAX_EXACT + (jnp.log(n / MAX_EXACT) / math.log(MAX_DISTANCE / MAX_EXACT) * (N_BUCKETS - MAX_EXACT)).astype(jnp.int32)
    large = jnp.minimum(large, N_BUCKETS - 1)
    return jnp.where(dist < MAX_EXACT, dist, large)


def _band_buckets():
    a = jnp.arange(BLK)[:, None]
    b = jnp.arange(2 * BLK)[None, :]
    steps = jnp.maximum(a + BLK - b, 0)
    return jnp.stack([_t5_bucket(steps * d) for d in DILATIONS]).astype(jnp.int32)


def _pad_rows(a, rows=8):
    return jnp.pad(a, ((0, rows - a.shape[0]), (0, 0)))


def kernel(x, c, w_ada, b_ada, w_in, conv_w, conv_b, rel_bias, w_attn_out, w_conv_out, w_o, ln_g, ln_b, loss_target, m_w_ada, m_b_ada, m_w_in, m_conv_w, m_conv_b, m_rel_bias, m_w_attn_out, m_w_conv_out, m_w_o, m_ln_g, m_ln_b, v_w_ada, v_b_ada, v_w_in, v_conv_w, v_conv_b, v_rel_bias, v_w_attn_out, v_w_conv_out, v_w_o, v_ln_g, v_ln_b):
    bsz, seq, _ = x.shape
    t = bsz * seq
    mx, my, mc = _place()
    chip = 2 * mx + my
    dev = 4 * mx + 2 * my + mc
    x2 = x.reshape(t, D)
    tgt = loss_target.reshape(t, D)

    n_ada = w_ada.shape[2]
    n_cw = conv_w.shape[2]
    c_and_cw = jnp.concatenate([_pad_rows(c), jnp.pad(conv_w[0], ((0, 5), (0, D - n_cw)))], axis=0)
    mine, firsts = _to_bf16_windows([w[0] for w in (w_in, w_attn_out, w_conv_out, w_o)], c_and_cw)

    c_all = firsts[:, 0:bsz, :].reshape(8 * bsz, D)
    conv_w_f = firsts[0::2, 8:11, 0:n_cw].transpose(1, 0, 2).reshape(3, D)
    b_cols = lax.dynamic_slice(b_ada, (0, chip * n_ada), (1, n_ada))
    mod_part = _ada_forward(c_all, w_ada[0], b_cols)
    mod_parts = _all_gather8(mod_part, "gather_mod")
    mod_all = mod_parts[0::2].transpose(1, 0, 2).reshape(8 * bsz, 3 * D)
    mod = lax.dynamic_slice(mod_all, (dev * bsz, 0), (bsz, 3 * D))
    shift = mod[:, 0:D].reshape(bsz, 1, D)
    sc1p = 1.0 + mod[:, D:2 * D].reshape(bsz, 1, D)
    gate = mod[:, 2 * D:].reshape(bsz, 1, D)

    h, ht = _modulate(x2, sc1p, shift, seq)
    tab = lax.dynamic_index_in_dim(jnp.asarray(_tile_tables()), chip, 0, keepdims=False)
    qkv, gates, (w_in_f, w_ao_f, w_co_f, w_o_f) = _project_gather(h, mine, tab)
    buckets = _band_buckets()
    bias = _bias_tables(rel_bias, buckets)
    og, lg = [], []
    for g in range(3):
        o_g, l_g = _attn_forward(g, qkv, bias[g], bsz, seq)
        og.append(o_g)
        lg.append(l_g)
    (a_in, s_in, merged, dy, a_out, s_out, y_conv, o, lj, dxr, vec_f, dgate) = _mix_forward(
        gates, og, lg, x2, tgt, gate, w_ao_f, w_co_f, w_o_f, conv_w_f, conv_b, ln_g, ln_b, bsz, seq)

    dgates, do, dl, da_out, ds_out, vec = _mix_backward(gates, dy, a_out, s_out, y_conv, o, lj, w_ao_f, w_co_f, w_o_f, conv_w_f, vec_f, bsz, seq)
    core = jnp.reshape(mc, (1,)).astype(jnp.int32)
    small_grads = _out_weight_grads(a_in, da_out, s_in, ds_out, merged, dy, core)
    got_small = _swap_halves([theirs for _, theirs in small_grads], "swap_small_grad_halves")
    sums_small = _chip_sums([own for own, _ in small_grads], got_small, 1, "chip_sums_small")
    dqkv, dbs = None, []
    for g in range(3):
        dqkv, db = _attn_backward(g, qkv, do, dl, bias[g], dqkv, bsz, seq)
        dbs.append(db)
    dq, dk, dv = dqkv
    drb = _bias_grad(jnp.stack(dbs), buckets)
    drb = drb[:, :, 0:4].transpose(1, 0, 2).reshape(N_BUCKETS, 12)
    g_in_mine, got_in, landed_small = _in_weight_grad(ht, dq, dk, dv, dgates, core, [bf for _, bf in sums_small])
    sums_in = _chip_sums([g_in_mine], [got_in], 0, "chip_sums_in")
    grad_x, dshift, dscale, landed_in = _input_grad(dq, dk, dv, dgates, w_in_f, x2, dxr, sc1p, seq, [bf for _, bf in sums_in])
    halves = _reduce_mine([own for own, _ in sums_in + sums_small], list(landed_in) + list(landed_small))

    dmod = jnp.concatenate([dshift, dscale, dgate], axis=2).reshape(bsz * 3, D)
    drb_row = jnp.pad(drb.reshape(1, N_BUCKETS * 12), ((0, 0), (0, D - N_BUCKETS * 12)))
    vec = lax.dynamic_update_slice(vec, drb_row, (7, 0))
    packed = jnp.concatenate([vec, _pad_rows(dmod)], axis=0)
    (gw_in, gw_ao, gw_co, gw_o), gathered = _join_halves(halves, packed)
    small = _sum_partials(gathered)
    g_ln_g, g_ln_b, loss_lanes = small[0:1], small[1:2], small[2:3]
    g_conv_w_full, g_conv_b = small[3:6], small[6:7]
    g_rel_bias = small[7, 0:N_BUCKETS * 12].reshape(N_BUCKETS, 12)
    loss = 0.5 / D * jnp.sum(loss_lanes)
    dmod_all = gathered[:, 8:8 + 3 * bsz, :].reshape(8 * bsz, 3 * D)
    dmod_cols = lax.dynamic_slice(dmod_all, (0, chip * n_ada), (8 * bsz, n_ada))
    gw_ada, gb_ada = _ada_backward(c_all, dmod_cols, dmod_all)
    g_conv_w = lax.dynamic_slice(g_conv_w_full, (0, chip * n_cw), (3, n_cw))

    names = ["w_ada", "b_ada", "w_in", "conv_w", "conv_b", "rel_bias", "w_attn_out", "w_conv_out", "w_o", "ln_g", "ln_b"]
    two_d = lambda a: a.reshape(a.shape[-2:]) if a.ndim == 3 else a
    weights = dict(zip(names, map(two_d, (w_ada, b_ada, w_in, conv_w, conv_b, rel_bias, w_attn_out, w_conv_out, w_o, ln_g, ln_b))))
    ms = dict(zip(names, map(two_d, (m_w_ada, m_b_ada, m_w_in, m_conv_w, m_conv_b, m_rel_bias, m_w_attn_out, m_w_conv_out, m_w_o, m_ln_g, m_ln_b))))
    vs = dict(zip(names, map(two_d, (v_w_ada, v_b_ada, v_w_in, v_conv_w, v_conv_b, v_rel_bias, v_w_attn_out, v_w_conv_out, v_w_o, v_ln_g, v_ln_b))))
    grads = dict(zip(names, (gw_ada, gb_ada, gw_in, g_conv_w, g_conv_b, g_rel_bias, gw_ao, gw_co, gw_o, g_ln_g, g_ln_b)))
    shapes = dict(zip(names, (w_ada, b_ada, w_in, conv_w, conv_b, rel_bias, w_attn_out, w_conv_out, w_o, ln_g, ln_b)))
    grad_out, deltas, new_m, new_v = {}, {}, {}, {}
    for n in names:
        grad_out[n], deltas[n], new_m[n], new_v[n] = _adamw(weights[n], grads[n], ms[n], vs[n], f"adamw_{n}")
    shaped = lambda d: [d[n].reshape(shapes[n].shape) for n in names]
    return (loss, grad_x.reshape(bsz, seq, D), *shaped(grad_out), *shaped(deltas), *shaped(new_m), *shaped(new_v))
```

```python
import math

import numpy as np
import jax
import jax.numpy as jnp
from jax import lax
from jax.experimental import pallas as pl
from jax.experimental.pallas import tpu as pltpu

F32 = jnp.float32
BF16 = jnp.bfloat16
SDS = jax.ShapeDtypeStruct
MESH = pl.DeviceIdType.MESH
HBM_OUT = pltpu.HBM
ANY = pl.BlockSpec(memory_space=pl.ANY)
VMEM_SPEC = pl.BlockSpec(memory_space=pltpu.VMEM)

D = 1024
HD = 128
BLK = 128
QW = 1536
AW = 512
NGATE = 6656
GATE_COLS = ((0, 512), (512, 1536), (1536, 2560), (2560, 3584), (3584, 4608), (4608, 5632), (5632, 6656))
NCOL = 3 * QW + NGATE
TN = 512
NQT = QW // TN
NPT = NCOL // TN
DILATIONS = (1, 4, 16)
N_BUCKETS, MAX_EXACT, MAX_DISTANCE = 32, 16, 2048
ALPHA = 2.0 ** 0.25
LN_EPS = 1e-5
NEG = -1e30
SCALE = HD ** -0.5
LR, B1, B2, EPS, WD, STEP = 0.001, 0.9, 0.999, 1e-08, 0.01, 10
NCHIP = 4
VMEM_CAP = 60 * 2 ** 20


def _cp(sem=None, vmem=None, side=False):
    return pltpu.CompilerParams(dimension_semantics=sem, vmem_limit_bytes=vmem, has_side_effects=side)


def _dot(a, b):
    return jnp.dot(a, b, preferred_element_type=F32)


def _dot_nt(a, b):
    return lax.dot_general(a, b, (((1,), (1,)), ((), ())), preferred_element_type=F32)


def _dot_tn(a, b):
    return lax.dot_general(a, b, (((0,), (0,)), ((), ())), preferred_element_type=F32)


def _sig(x):
    return 1.0 / (1.0 + jnp.exp(-x))


def _in_hbm(a):
    return pltpu.with_memory_space_constraint(a, pltpu.HBM)


def _place():
    x, y, c = lax.axis_index("x"), lax.axis_index("y"), lax.axis_index("c")
    return x, y, c


def _all_gather8(v, name):
    r, cdim = v.shape

    def body(v_ref, out_ref, send_sems, recv_sems, local_sem):
        x, y, c = _place()
        me = 4 * x + 2 * y + c
        peers = [(x, y, 1 - c), (1 - x, y, c), (x, 1 - y, c), (1 - x, 1 - y, c),
                 (1 - x, y, 1 - c), (x, 1 - y, 1 - c), (1 - x, 1 - y, 1 - c)]
        mine = pltpu.make_async_copy(v_ref, out_ref.at[me], local_sem)
        mine.start()

        def copy(k, block, to):
            return pltpu.make_async_remote_copy(src_ref=v_ref, dst_ref=out_ref.at[block], send_sem=send_sems.at[k],
                                                recv_sem=recv_sems.at[k], device_id=to, device_id_type=MESH)

        sends = [copy(k, me, p) for k, p in enumerate(peers)]
        for cp in sends:
            cp.start()
        for k, (px, py, pc) in enumerate(peers):
            copy(k, 4 * px + 2 * py + pc, (px, py, pc)).wait_recv()
        for cp in sends:
            cp.wait_send()
        mine.wait()

    return pl.pallas_call(
        body, name=name, out_shape=SDS((8, r, cdim), v.dtype), in_specs=[VMEM_SPEC], out_specs=VMEM_SPEC,
        scratch_shapes=[pltpu.SemaphoreType.DMA((7,)), pltpu.SemaphoreType.DMA((7,)), pltpu.SemaphoreType.DMA(())],
        compiler_params=_cp(side=True),
    )(v)


W_CUTS = (("col", D, NCOL // NCHIP), ("col", AW, D // NCHIP), ("row", D // NCHIP, D), ("row", D // NCHIP, D))
W_FULL = ((D, NCOL), (AW, D), (D, D), (D, D))


def _shard_window(ref, cut, k, half):
    kind, nr, nc = cut
    hr = nr // 2
    if kind == "col":
        rows = pl.ds(0, nr) if half is None else pl.ds(pl.multiple_of(half * hr, 16), hr)
        return ref.at[rows, pl.ds(pl.multiple_of(k * nc, 128), nc)]
    if half is None:
        return ref.at[pl.ds(pl.multiple_of(k * nr, 16), nr), :]
    return ref.at[pl.ds(pl.multiple_of(k * nr + half * hr, 16), hr), :]


def _half_rows(ref, cut, half):
    hr = cut[1] // 2
    return ref.at[pl.ds(pl.multiple_of(half * hr, 16), hr), :]


def _to_bf16_windows(ws, packed):
    x, y, _ = _place()
    chip = jnp.reshape(2 * x + y, (1,)).astype(jnp.int32)
    tr = 256
    n = len(ws)
    nsteps = D // tr
    r, cdim = packed.shape

    def body(c_ref, *refs):
        src, v_ref, dst, out_ref = refs[:n], refs[n], refs[n + 1:2 * n + 1], refs[2 * n + 1]
        g_send, g_recv, local_sem = refs[2 * n + 2:]
        px, py, pc = _place()
        me = 4 * px + 2 * py + pc
        peers = [(px, py, 1 - pc), (1 - px, py, pc), (px, 1 - py, pc), (1 - px, 1 - py, pc),
                 (1 - px, py, 1 - pc), (px, 1 - py, 1 - pc), (1 - px, 1 - py, 1 - pc)]
        mine = pltpu.make_async_copy(v_ref, out_ref.at[me], local_sem)

        def copy(k, block, to):
            return pltpu.make_async_remote_copy(src_ref=v_ref, dst_ref=out_ref.at[block], send_sem=g_send.at[k],
                                                recv_sem=g_recv.at[k], device_id=to, device_id_type=MESH)

        gathers = [copy(k, me, p) for k, p in enumerate(peers)]

        @pl.when(pl.program_id(0) == 0)
        def _():
            mine.start()
            for cp in gathers:
                cp.start()

        dst[0][...] = src[0][...].astype(BF16)

        @pl.when(pl.program_id(0) == 0)
        def _():
            for a in range(1, n):
                dst[a][...] = src[a][...].astype(BF16)

        @pl.when(pl.program_id(0) == nsteps - 1)
        def _():
            for k, (qx, qy, qc) in enumerate(peers):
                copy(k, 4 * qx + 2 * qy + qc, (qx, qy, qc)).wait_recv()
            for cp in gathers:
                cp.wait_send()
            mine.wait()

    in_specs = [pl.BlockSpec((tr, W_CUTS[0][2]), lambda i, cr: (i, 0))]
    out_specs = [pl.BlockSpec((tr, W_CUTS[0][2]), lambda i, cr: (i, cr[0]))]
    for a in range(1, n):
        kind, nr, nc = W_CUTS[a]
        in_specs.append(pl.BlockSpec((nr, nc), lambda i, cr: (0, 0)))
        out_specs.append(pl.BlockSpec((nr, nc), (lambda i, cr: (0, cr[0])) if kind == "col" else (lambda i, cr: (cr[0], 0))))
    outs = pl.pallas_call(
        body, name="to_bf16", out_shape=[SDS(W_FULL[a], BF16) for a in range(n)] + [SDS((8, r, cdim), F32)],
        grid_spec=pltpu.PrefetchScalarGridSpec(
            num_scalar_prefetch=1, grid=(nsteps,), in_specs=in_specs + [ANY], out_specs=out_specs + [ANY],
            scratch_shapes=[pltpu.SemaphoreType.DMA((7,)), pltpu.SemaphoreType.DMA((7,)), pltpu.SemaphoreType.DMA(())]),
        compiler_params=_cp(("arbitrary",), side=True),
    )(chip, *ws, packed)
    return outs[:n], outs[n]


def _swap_halves(theirs, name):
    n = len(theirs)

    def body(*refs):
        src, land = refs[:n], refs[n:2 * n]
        send_sems, recv_sems = refs[2 * n:]
        x, y, c = _place()
        copies = [pltpu.make_async_remote_copy(src_ref=src[a], dst_ref=land[a], send_sem=send_sems.at[a], recv_sem=recv_sems.at[a],
                                               device_id=(x, y, 1 - c), device_id_type=MESH) for a in range(n)]
        for cp in copies:
            cp.start()
        for cp in copies:
            cp.wait()

    return pl.pallas_call(
        body, name=name, out_shape=[SDS(v.shape, v.dtype) for v in theirs], in_specs=[ANY] * n, out_specs=[ANY] * n,
        scratch_shapes=[pltpu.SemaphoreType.DMA((n,)), pltpu.SemaphoreType.DMA((n,))],
        compiler_params=_cp(side=True),
    )(*theirs)


def _chip_sums(mines, gots, first, name):
    n = len(mines)
    x, y, _ = _place()
    me = jnp.reshape(2 * x + y, (1,)).astype(jnp.int32)

    def body(me_ref, *refs):
        ins, outs = refs[:2 * n], refs[2 * n:]
        for a in range(n):
            hr, nc = W_CUTS[first + a][1] // 2, W_CUTS[first + a][2]
            s = (ins[2 * a][...] + ins[2 * a + 1][...].astype(F32)).reshape(hr, nc)
            outs[2 * a + 1][0] = s.astype(BF16)

            @pl.when(pl.program_id(0) == me_ref[0])
            def _(a=a, s=s):
                outs[2 * a][...] = s

    in_specs, out_specs, out_shape = [], [], []
    for a in range(n):
        kind, nr, nc = W_CUTS[first + a]
        hr = nr // 2
        spec = pl.BlockSpec((hr, nc), lambda k, mr: (0, k)) if kind == "col" else pl.BlockSpec((1, hr, nc), lambda k, mr: (k, 0, 0))
        in_specs += [spec, spec]
        out_specs += [pl.BlockSpec((hr, nc), lambda k, mr: (0, 0)), pl.BlockSpec((1, hr, nc), lambda k, mr: (k, 0, 0))]
        out_shape += [SDS((hr, nc), F32), SDS((NCHIP, hr, nc), BF16)]
    outs = pl.pallas_call(
        body, name=name, out_shape=out_shape,
        grid_spec=pltpu.PrefetchScalarGridSpec(num_scalar_prefetch=1, grid=(NCHIP,), in_specs=in_specs, out_specs=out_specs),
        compiler_params=_cp(("arbitrary",), VMEM_CAP),
    )(me, *[v for pair in zip(mines, gots) for v in pair])
    return [(outs[2 * a], outs[2 * a + 1]) for a in range(n)]


def _reduce_mine(mines, gots):
    n = len(mines)
    _, _, c = _place()
    core = jnp.reshape(c, (1,)).astype(jnp.int32)
    tr = 256
    nsteps = W_CUTS[0][1] // 2 // tr

    def body(c_ref, *refs):
        ins, outs = refs[:2 * n], refs[2 * n:]

        def add(a):
            m_ref, g_ref = ins[2 * a], ins[2 * a + 1]
            outs[a][...] = ((m_ref[...] + g_ref[0].astype(F32)) + g_ref[1].astype(F32)) + g_ref[2].astype(F32)

        add(0)

        @pl.when(pl.program_id(0) == 0)
        def _():
            for a in range(1, n):
                add(a)

    nc0 = W_CUTS[0][2]
    in_specs = [pl.BlockSpec((tr, nc0), lambda i, cr: (i, 0)), pl.BlockSpec((3, tr, nc0), lambda i, cr: (0, i, 0))]
    out_specs = [pl.BlockSpec((tr, nc0), lambda i, cr: (cr[0] * nsteps + i, 0))]
    for a in range(1, n):
        hr, nc = W_CUTS[a][1] // 2, W_CUTS[a][2]
        in_specs += [pl.BlockSpec((hr, nc), lambda i, cr: (0, 0)), pl.BlockSpec((3, hr, nc), lambda i, cr: (0, 0, 0))]
        out_specs.append(pl.BlockSpec((hr, nc), lambda i, cr: (cr[0], 0)))
    return pl.pallas_call(
        body, name="reduce_mine", out_shape=[SDS((W_CUTS[a][1], W_CUTS[a][2]), F32) for a in range(n)],
        grid_spec=pltpu.PrefetchScalarGridSpec(num_scalar_prefetch=1, grid=(nsteps,), in_specs=in_specs, out_specs=out_specs),
        compiler_params=_cp(("arbitrary",), VMEM_CAP),
    )(core, *[v for pair in zip(mines, gots) for v in pair])


def _join_halves(fulls, packed):
    n = len(fulls)
    r, cdim = packed.shape

    def body(*refs):
        v_ref, full, out_ref = refs[n], refs[n + 1:2 * n + 1], refs[2 * n + 1]
        send_sems, recv_sems, g_send, g_recv, local_sem = refs[2 * n + 2:]
        x, y, c = _place()
        sibling = (x, y, 1 - c)
        me = 4 * x + 2 * y + c
        peers = [(x, y, 1 - c), (1 - x, y, c), (x, 1 - y, c), (1 - x, 1 - y, c),
                 (1 - x, y, 1 - c), (x, 1 - y, 1 - c), (1 - x, 1 - y, 1 - c)]
        mine = pltpu.make_async_copy(v_ref, out_ref.at[me], local_sem)
        mine.start()

        def copy(k, block, to):
            return pltpu.make_async_remote_copy(src_ref=v_ref, dst_ref=out_ref.at[block], send_sem=g_send.at[k],
                                                recv_sem=g_recv.at[k], device_id=to, device_id_type=MESH)

        def swap(a, half):
            rows = _half_rows(full[a], W_CUTS[a], half)
            return pltpu.make_async_remote_copy(src_ref=rows, dst_ref=rows, send_sem=send_sems.at[a], recv_sem=recv_sems.at[a],
                                                device_id=sibling, device_id_type=MESH)

        gathers = [copy(k, me, p) for k, p in enumerate(peers)]
        for cp in gathers:
            cp.start()
        sends = [swap(a, c) for a in range(n)]
        for cp in sends:
            cp.start()
        for a, cp in enumerate(sends):
            cp.wait_send()
            swap(a, 1 - c).wait_recv()
        for k, (px, py, pc) in enumerate(peers):
            copy(k, 4 * px + 2 * py + pc, (px, py, pc)).wait_recv()
        for cp in gathers:
            cp.wait_send()
        mine.wait()

    outs = pl.pallas_call(
        body, name="join_grad_halves", out_shape=[SDS((W_CUTS[a][1], W_CUTS[a][2]), F32) for a in range(n)] + [SDS((8, r, cdim), F32)],
        in_specs=[ANY] * (n + 1), out_specs=[ANY] * (n + 1),
        scratch_shapes=[pltpu.SemaphoreType.DMA((n,)), pltpu.SemaphoreType.DMA((n,)), pltpu.SemaphoreType.DMA((7,)),
                        pltpu.SemaphoreType.DMA((7,)), pltpu.SemaphoreType.DMA(())],
        input_output_aliases={a: a for a in range(n)}, compiler_params=_cp(side=True),
    )(*fulls, packed)
    return outs[:n], outs[n]


def _ada_forward(c_all, w_ada, b_cols):
    nb, nc = c_all.shape[0], w_ada.shape[1]

    def body(c_ref, w_ref, b_ref, o_ref):
        cv = c_ref[...]
        sc = (cv * _sig(cv)).astype(BF16)
        o_ref[...] = _dot(sc, w_ref[...].astype(BF16)) + b_ref[...]

    return pl.pallas_call(body, name="ada_forward", out_shape=SDS((nb, nc), F32), compiler_params=_cp(vmem=VMEM_CAP // 2))(c_all, w_ada, b_cols)


def _ada_backward(c_all, dmod_cols, dmod_all):
    nb, nc = dmod_cols.shape

    def body(c_ref, d_ref, a_ref, gw_ref, gb_ref):
        cv = c_ref[...]
        sc = (cv * _sig(cv)).astype(BF16)
        gw_ref[...] = _dot_tn(sc, d_ref[...].astype(BF16))
        gb_ref[...] = jnp.sum(a_ref[...], axis=0, keepdims=True)

    return pl.pallas_call(body, name="ada_backward", out_shape=[SDS((D, nc), F32), SDS((1, dmod_all.shape[1]), F32)],
                          compiler_params=_cp(vmem=VMEM_CAP // 2))(c_all, dmod_cols, dmod_all)


def _modulate(x2, sc1p, shift, seq, tm=512):
    t = x2.shape[0]
    spt = seq // tm

    def body(x_ref, sc_ref, sh_ref, h_ref, ht_ref):
        h = x_ref[...] * sc_ref[0] + sh_ref[0]
        h_ref[...] = h.astype(BF16)
        ht_ref[...] = h.T.astype(BF16)

    per_seq = pl.BlockSpec((1, 1, D), lambda i: (i // spt, 0, 0))
    return pl.pallas_call(
        body, name="modulate", out_shape=[HBM_OUT((t, D), BF16), HBM_OUT((D, t), BF16)], grid=(t // tm,),
        in_specs=[pl.BlockSpec((tm, D), lambda i: (i, 0)), per_seq, per_seq],
        out_specs=[pl.BlockSpec((tm, D), lambda i: (i, 0)), pl.BlockSpec((D, tm), lambda i: (0, i))],
        compiler_params=_cp(("parallel",)),
    )(_in_hbm(x2), sc1p, shift)


TW = 256
TPS = NCOL // NCHIP // TW
NT = NCOL // TW
NQKV_T = 3 * QW // TW
N_TILE_SEMS = 2 * 3 * TPS


def _tile_tables():
    tabs = np.zeros((NCHIP, 3, NT), np.int32)
    for me in range(NCHIP):
        tiles = [TPS * (me ^ (s // TPS)) + s % TPS for s in range(NT)]
        tabs[me, 0] = tiles
        for row, (lo, hi) in enumerate(((0, NQKV_T), (NQKV_T, NT))):
            mine = [w - lo if lo <= w < hi else None for w in tiles]
            held = next(m for m in mine if m is not None)
            for s, m in enumerate(mine):
                held = held if m is None else m
                tabs[me, 1 + row, s] = held
    return tabs


def _project_gather(h, fulls, tab):
    t = h.shape[0]
    n = len(fulls)

    def body(tab_ref, h_ref, *rest):
        qkv_ref, g_ref = rest[n], rest[n + 1]
        full = rest[n + 2:2 * n + 2]
        w_buf, tile_sems, send_sems, recv_sems = rest[2 * n + 2:]
        s = pl.program_id(0)
        x, y, c = _place()
        me = 2 * x + y
        peers = [(x, 1 - y), (1 - x, y), (1 - x, 1 - y)]
        sibling = (x, y, 1 - c)

        def hop(a, r, stage, chip, half, to):
            window = _shard_window(full[a], W_CUTS[a], chip, half)
            k = N_TILE_SEMS + 6 * (a - 1) + 2 * r + stage
            return pltpu.make_async_remote_copy(src_ref=window, dst_ref=window, send_sem=send_sems.at[k], recv_sem=recv_sems.at[k],
                                                device_id=to, device_id_type=MESH)

        def tile_hop(q, stage, col_step, half, to):
            col = pl.multiple_of(tab_ref[0, col_step] * TW, TW)
            window = full[0].at[pl.ds(pl.multiple_of(half * (D // 2), 16), D // 2), pl.ds(col, TW)]
            k = 2 * (q - TPS) + stage
            return pltpu.make_async_remote_copy(src_ref=window, dst_ref=window, send_sem=send_sems.at[k], recv_sem=recv_sems.at[k],
                                                device_id=to, device_id_type=MESH)

        def send_tile(r, j):
            return tile_hop(TPS * (r + 1) + j, 0, j, c, (*peers[r], c))

        def pass_on(q, to):
            return tile_hop(3 * TPS + q % TPS, 0, q, c, to)

        def arrive(a, r):
            px, py = peers[r]
            chip = 2 * px + py
            hop(a, r, 0, chip, c, (px, py, c)).wait_recv()
            hop(a, r, 1, chip, c, sibling).start()
            hop(a, r, 1, chip, 1 - c, sibling).wait_recv()

        def tile(step, slot):
            col = pl.multiple_of(tab_ref[0, step] * TW, TW)
            return pltpu.make_async_copy(full[0].at[:, pl.ds(col, TW)], w_buf.at[slot], tile_sems.at[slot])

        @pl.when(s == 0)
        def _():
            for r in range(2):
                for j in range(TPS):
                    send_tile(r, j).start()
            tile(0, 0).start()

        @pl.when((s + 1 >= TPS) & (s + 1 < NT))
        def _():
            tile_hop(s + 1, 1, s + 1, 1 - c, sibling).wait_recv()

        @pl.when(s + 1 < NT)
        def _():
            tile(s + 1, 1 - (s % 2)).start()

        @pl.when((s + 2 >= TPS) & (s + 2 < NT))
        def _():
            tile_hop(s + 2, 0, s + 2, c, sibling).wait_recv()
            tile_hop(s + 2, 1, s + 2, c, sibling).start()

        for r in range(2):
            @pl.when(((s + 2) // TPS == r + 1) & ((s + 2) % 2 == (r + 1 + TPS * (r + 1)) % 2))
            def _(r=r):
                pass_on(s + 2, (*peers[1 - r], c)).start()

        @pl.when(s + 2 == 2 * TPS - 1)
        def _():
            for a in range(1, n):
                for r in range(3):
                    hop(a, r, 0, me, c, (*peers[r], c)).start()

        slot = s % 2
        tile(s, slot).wait()
        is_qkv = tab_ref[0, s] < NQKV_T
        for k in range(2):
            @pl.when(slot == k)
            def _(k=k):
                acc = _dot(h_ref[...], w_buf[k])

                @pl.when(is_qkv)
                def _():
                    qkv_ref[...] = acc.astype(BF16)

                @pl.when(jnp.logical_not(is_qkv))
                def _():
                    g_ref[...] = acc.astype(BF16)

        @pl.when(s == NT - 1)
        def _():
            for a in range(1, n):
                for r in range(3):
                    arrive(a, r)
            for r in range(3):
                for j in range(TPS):
                    send_tile(r, j).wait_send()
                    tile_hop(TPS * (r + 1) + j, 1, TPS * (r + 1) + j, c, sibling).wait_send()
                for a in range(1, n):
                    hop(a, r, 0, me, c, (*peers[r], c)).wait_send()
                    px, py = peers[r]
                    hop(a, r, 1, 2 * px + py, c, sibling).wait_send()

    n_sems = N_TILE_SEMS + 6 * (n - 1)
    outs = pl.pallas_call(
        body, name="project_gather", out_shape=[HBM_OUT((t, 3 * QW), BF16), HBM_OUT((t, NGATE), BF16)] + [SDS(s, BF16) for s in W_FULL],
        grid_spec=pltpu.PrefetchScalarGridSpec(
            num_scalar_prefetch=1, grid=(NT,),
            in_specs=[pl.BlockSpec((t, D), lambda s, tab: (0, 0))] + [ANY] * n,
            out_specs=[pl.BlockSpec((t, TW), lambda s, tab: (0, tab[1, s])), pl.BlockSpec((t, TW), lambda s, tab: (0, tab[2, s]))] + [ANY] * n,
            scratch_shapes=[pltpu.VMEM((2, D, TW), BF16), pltpu.SemaphoreType.DMA((2,)),
                            pltpu.SemaphoreType.DMA((n_sems,)), pltpu.SemaphoreType.DMA((n_sems,))]),
        input_output_aliases={2 + a: 2 + a for a in range(n)},
        compiler_params=_cp(("arbitrary",), VMEM_CAP, side=True),
    )(tab, _in_hbm(h), *fulls)
    return outs[0], outs[1], outs[2:]


def _bias_tables(rel_bias, buckets, packed):
    r, cdim = packed.shape

    def body(tab_ref, bk_ref, v_ref, o_ref, out_ref, g_send, g_recv, local_sem):
        x, y, c = _place()
        me = 4 * x + 2 * y + c
        peers = [(x, y, 1 - c), (1 - x, y, c), (x, 1 - y, c), (1 - x, 1 - y, c),
                 (1 - x, y, 1 - c), (x, 1 - y, 1 - c), (1 - x, 1 - y, 1 - c)]
        mine = pltpu.make_async_copy(v_ref, out_ref.at[me], local_sem)
        mine.start()

        def copy(k, block, to):
            return pltpu.make_async_remote_copy(src_ref=v_ref, dst_ref=out_ref.at[block], send_sem=g_send.at[k],
                                                recv_sem=g_recv.at[k], device_id=to, device_id_type=MESH)

        gathers = [copy(k, me, p) for k, p in enumerate(peers)]
        for cp in gathers:
            cp.start()
        a = lax.broadcasted_iota(jnp.int32, (BLK, 2 * BLK), 0)
        b = lax.broadcasted_iota(jnp.int32, (BLK, 2 * BLK), 1)
        steps = a + BLK - b
        valid = (steps >= 0) & (steps <= BLK)
        for g in range(3):
            bk = bk_ref[g]
            for j in range(4):
                def pick(kk, acc, bk=bk, col=4 * g + j):
                    return jnp.where(bk == kk, tab_ref[kk, col], acc)

                acc = lax.fori_loop(0, N_BUCKETS, pick, jnp.zeros((BLK, 2 * BLK), F32))
                o_ref[g, j] = jnp.where(valid, acc, NEG)
        for k, (px, py, pc) in enumerate(peers):
            copy(k, 4 * px + 2 * py + pc, (px, py, pc)).wait_recv()
        for cp in gathers:
            cp.wait_send()
        mine.wait()

    return pl.pallas_call(
        body, name="bias_tables", out_shape=[SDS((3, 4, BLK, 2 * BLK), F32), SDS((8, r, cdim), F32)],
        in_specs=[pl.BlockSpec(memory_space=pltpu.SMEM), VMEM_SPEC, ANY], out_specs=[VMEM_SPEC, ANY],
        scratch_shapes=[pltpu.SemaphoreType.DMA((7,)), pltpu.SemaphoreType.DMA((7,)), pltpu.SemaphoreType.DMA(())],
        compiler_params=_cp(side=True),
    )(rel_bias, buckets, packed)


def _bias_grad(ds_sum, buckets):
    def body(ds_ref, bk_ref, o_ref, part_ref):
        lane = lax.broadcasted_iota(jnp.int32, (N_BUCKETS, 128), 1)
        for g in range(3):
            def bucket(kk, carry, g=g):
                mine = bk_ref[g] == kk
                for j in range(4):
                    v = jnp.sum(jnp.where(mine, ds_ref[g, j], 0.0).reshape(BLK // 8, 8, 2 * BLK), axis=0)
                    part_ref[j, pl.ds(pl.multiple_of(kk * 8, 8), 8), :] = v[:, :BLK] + v[:, BLK:]
                return carry

            lax.fori_loop(0, N_BUCKETS, bucket, 0)
            out = jnp.zeros((N_BUCKETS, 128), F32)
            for j in range(4):
                rows = jnp.sum(part_ref[j], axis=1, keepdims=True)
                out = jnp.where(lane == j, jnp.sum(rows.reshape(N_BUCKETS, 8, 1), axis=1), out)
            o_ref[g] = out

    return pl.pallas_call(body, name="bias_grad", out_shape=SDS((3, N_BUCKETS, 128), F32), in_specs=[VMEM_SPEC, VMEM_SPEC],
                          out_specs=VMEM_SPEC, scratch_shapes=[pltpu.VMEM((4, N_BUCKETS * 8, 128), F32)])(ds_sum, buckets)


def _sub_rows(d, r, first, size):
    return pl.ds(first * d + r, size) if d == 1 else pl.ds(first * d + r, size, stride=d)


def _head_spec(seq, g, part):
    return pl.BlockSpec((seq, HD), lambda b, hh: (b, part * (QW // HD) + 4 * g + hh))


def _rows(start, count, stride):
    return pl.ds(start, count) if stride == 1 else pl.ds(start, count, stride=stride)


def _gather_rows(dst, dst0, src, src0, stride, count):
    for first in range(0, count, BLK):
        dst[pl.ds(dst0 + first, BLK), :] = src[_rows(src0 + first * stride, BLK, stride), :].astype(dst.dtype)


def _scatter_rows(dst, dst0, stride, src, src0, count):
    for first in range(0, count, BLK):
        dst[_rows(dst0 + first * stride, BLK, stride), :] = src[pl.ds(src0 + first, BLK), :].astype(dst.dtype)


def _by_subsequence(dst, src, d, wide=None, tmp=None):
    seq = src.shape[0]
    ln = seq // d
    if wide is not None:
        wide[...] = src[...].astype(F32)
        src = wide
    if d <= 4:
        for r in range(d):
            _gather_rows(dst, r * ln, src, r, d, ln)
    else:
        quarter = seq // 4
        for r4 in range(4):
            _gather_rows(tmp, r4 * quarter, src, r4, 4, quarter)
        for r4 in range(4):
            for a in range(d // 4):
                _gather_rows(dst, (4 * a + r4) * ln, tmp, r4 * quarter + a, d // 4, ln)


def _to_sequence(dst, src, d, tmp=None):
    seq = dst.shape[0]
    ln = seq // d
    if d <= 4:
        for r in range(d):
            _scatter_rows(dst, r, d, src, r * ln, ln)
    else:
        quarter = seq // 4
        for r4 in range(4):
            for a in range(d // 4):
                _scatter_rows(tmp, r4 * quarter + a, d // 4, src, (4 * a + r4) * ln, ln)
        for r4 in range(4):
            _scatter_rows(dst, r4, 4, tmp, r4 * quarter, quarter)


def _attn_forward(g, qkv, bias, bsz, seq):
    d = DILATIONS[g]
    ln = seq // d
    units = [(r, n) for r in range(d) for n in range(ln // BLK)]

    def band(n):
        return slice(BLK, 2 * BLK) if n == 0 else slice(0, 2 * BLK)

    def body(q_ref, k_ref, v_ref, b_ref, o_ref, l_ref, *scratch):
        hs = pl.program_id(1)
        s_scr, p_scr = scratch[:2]
        if d == 1:
            qd, kd, vd = q_ref, k_ref, v_ref
        else:
            wide, tmp, qd, kd, vd = scratch[2:7]
            for dst, src in ((qd, q_ref), (kd, k_ref), (vd, v_ref)):
                _by_subsequence(dst, src, d, wide, tmp)
        blk = lambda r, n: pl.ds(r * ln + n * BLK, BLK)
        direct = d <= 4
        out_rows = (lambda r, n: _sub_rows(d, r, n * BLK, BLK)) if direct else blk
        o_dst, l_dst = (o_ref, l_ref) if direct else scratch[7:9]
        for u, (r, n) in enumerate(units):
            s_scr[u, :, BLK:] = _dot_nt(qd[blk(r, n), :], kd[blk(r, n), :])
            if n > 0:
                s_scr[u, :, :BLK] = _dot_nt(qd[blk(r, n), :], kd[blk(r, n - 1), :])
        for u, (r, n) in enumerate(units):
            s = s_scr[u, :, band(n)] * SCALE + b_ref[hs, :, band(n)]
            m = jnp.max(s, axis=1, keepdims=True)
            e = jnp.exp(s - m)
            den = jnp.sum(e, axis=1, keepdims=True)
            p_scr[u, :, band(n)] = (e * (1.0 / den)).astype(BF16)
            l_dst[out_rows(r, n), :] = jnp.broadcast_to(m + jnp.log(den), (BLK, HD))
        for u, (r, n) in enumerate(units):
            acc = _dot(p_scr[u, :, BLK:], vd[blk(r, n), :])
            if n > 0:
                acc = acc + _dot(p_scr[u, :, :BLK], vd[blk(r, n - 1), :])
            o_dst[out_rows(r, n), :] = acc
        if not direct:
            _to_sequence(o_ref, o_dst, d, tmp)
            _to_sequence(l_ref, l_dst, d, tmp)

    rows_f32, rows_bf16 = pltpu.VMEM((seq, HD), F32), pltpu.VMEM((seq, HD), BF16)
    regrouped = [] if d == 1 else [rows_f32] * 2 + [rows_bf16] * 3 + ([] if d <= 4 else [rows_f32] * 2)
    out_spec = pl.BlockSpec((seq, HD), lambda b, hh: (b, hh))
    return pl.pallas_call(
        body, name=f"attn_forward_{g}", out_shape=[HBM_OUT((bsz * seq, AW), F32)] * 2, grid=(bsz, 4),
        in_specs=[_head_spec(seq, g, part) for part in range(3)] + [pl.BlockSpec((4, BLK, 2 * BLK), lambda b, hh: (0, 0, 0))],
        out_specs=[out_spec, out_spec],
        scratch_shapes=[pltpu.VMEM((len(units), BLK, 2 * BLK), F32), pltpu.VMEM((len(units), BLK, 2 * BLK), BF16)] + regrouped,
        compiler_params=_cp(("parallel", "parallel"), VMEM_CAP // 2),
    )(qkv, qkv, qkv, _in_hbm(bias))


def _attn_backward(g, qkv, do, dl, bias, prev_out, bsz, seq):
    d = DILATIONS[g]
    ln = seq // d
    units = [(r, n) for r in range(d) for n in range(ln // BLK)]

    def body(q_ref, k_ref, v_ref, do_ref, dl_ref, b_ref, *rest):
        dq_ref, dk_ref, dv_ref, db_ref = rest[-18:-14]
        wide, tmp, qd, kd, vd, dod, dld, dqd, dkd, dvd, s_scr, dp_scr, p_scr, ds_scr = rest[-14:]
        hs = pl.program_id(1)

        @pl.when((pl.program_id(0) == 0) & (hs == 0))
        def _():
            db_ref[...] = jnp.zeros_like(db_ref)

        for dst, src in ((qd, q_ref), (kd, k_ref), (vd, v_ref)):
            _by_subsequence(dst, src, d, wide, tmp)
        _by_subsequence(dod, do_ref, d, None, tmp)
        _by_subsequence(dld, dl_ref, d, None, tmp)
        dkd[...] = jnp.zeros_like(dkd)
        dvd[...] = jnp.zeros_like(dvd)
        blk = lambda r, n: pl.ds(r * ln + n * BLK, BLK)
        keys = lambda r, n: [(blk(r, n), slice(BLK, 2 * BLK))] + ([(blk(r, n - 1), slice(0, BLK))] if n > 0 else [])
        for u, (r, n) in enumerate(units):
            for rows, band in keys(r, n):
                s_scr[u, :, band] = _dot_nt(qd[blk(r, n), :], kd[rows, :])
                dp_scr[u, :, band] = _dot_nt(dod[blk(r, n), :], vd[rows, :])
        for u, (r, n) in enumerate(units):
            both = dld[blk(r, n), :]
            lse, delta = both[:, 0:1], both[:, 64:65]
            band = slice(BLK, 2 * BLK) if n == 0 else slice(0, 2 * BLK)
            p = jnp.exp(s_scr[u, :, band] * SCALE + b_ref[hs, :, band] - lse)
            ds = p * (dp_scr[u, :, band] - delta)
            p_scr[u, :, band] = p.astype(BF16)
            ds_scr[u, :, band] = ds.astype(BF16)
            db_ref[hs, :, band] += ds
        for u, (r, n) in enumerate(units):
            dq = jnp.zeros((BLK, HD), F32)
            for rows, band in keys(r, n):
                dvd[rows, :] += _dot_tn(p_scr[u, :, band], dod[blk(r, n), :])
                dkd[rows, :] += _dot_tn(ds_scr[u, :, band], qd[blk(r, n), :]) * SCALE
                dq = dq + _dot(ds_scr[u, :, band], kd[rows, :])
            dqd[blk(r, n), :] = dq * SCALE
        for out, acc in ((dq_ref, dqd), (dk_ref, dkd), (dv_ref, dvd)):
            if d == 1:
                out[...] = acc[...].astype(BF16)
            else:
                _to_sequence(wide, acc, d, tmp)
                out[...] = wide[...].astype(BF16)

    qkv_spec = _head_spec(seq, g, 0)
    out_spec = pl.BlockSpec((seq, HD), lambda b, hh: (b, hh))
    band_spec = pl.BlockSpec((4, BLK, 2 * BLK), lambda b, hh: (0, 0, 0))
    ins = [qkv, qkv, qkv, _in_hbm(do), _in_hbm(dl), _in_hbm(bias)]
    in_specs = [_head_spec(seq, g, part) for part in range(3)] + [out_spec, out_spec, band_spec]
    aliases = {}
    if prev_out is not None:
        ins += list(prev_out)
        in_specs += [ANY] * 3
        aliases = {6: 0, 7: 1, 8: 2}
    rows_bf16, rows_f32 = pltpu.VMEM((seq, HD), BF16), pltpu.VMEM((seq, HD), F32)
    staged = [pltpu.VMEM((len(units), BLK, 2 * BLK), F32)] * 2 + [pltpu.VMEM((len(units), BLK, 2 * BLK), BF16)] * 2
    dq, dk, dv, db = pl.pallas_call(
        body, name=f"attn_backward_{g}", out_shape=[HBM_OUT((bsz * seq, QW), BF16)] * 3 + [SDS((4, BLK, 2 * BLK), F32)], grid=(bsz, 4),
        in_specs=in_specs, out_specs=[qkv_spec] * 3 + [band_spec], input_output_aliases=aliases,
        scratch_shapes=[rows_f32] * 2 + [rows_bf16] * 4 + [rows_f32] * 4 + staged,
        compiler_params=_cp(("arbitrary", "arbitrary"), VMEM_CAP // 2),
    )(*ins)
    return (dq, dk, dv), db


def _mix_forward(gates, og, lg, x2, tgt, gate, w_ao, w_co, w_o, conv_w, conv_b, ln_g, ln_b, bsz, seq, tm=256):
    t = x2.shape[0]
    spt = seq // tm

    def body(g_ref, o1, o2, o3, l1, l2, l3, x_ref, t_ref, gate_ref, wao_ref, wco_ref, wo_ref, cw_ref, cb_ref, lng_ref, lnb_ref,
             ain_ref, sin_ref, mrg_ref, dy_ref, aout_ref, sout_ref, yc_ref, o_ref, lj_ref, dxr_ref, vec_ref, dgate_ref, zc_ref):
        b, i = pl.program_id(0), pl.program_id(1)

        @pl.when((b == 0) & (i == 0))
        def _():
            vec_ref[...] = jnp.zeros_like(vec_ref)

        @pl.when(i == 0)
        def _():
            zc_ref[...] = jnp.zeros_like(zc_ref)
            dgate_ref[...] = jnp.zeros_like(dgate_ref)

        g_attn, u, bg, cg, g_conv, m_attn, m_conv = (g_ref[:, lo:hi].astype(F32) for lo, hi in GATE_COLS)
        la, lb, lc = l1[...], l2[...], l3[...]
        mx = jnp.maximum(la, jnp.maximum(lb, lc))
        ea, eb, ec = jnp.exp(la - mx), jnp.exp(lb - mx), jnp.exp(lc - mx)
        den = ea + eb + ec
        o = (ea * o1[...] + eb * o2[...] + ec * o3[...]) / den
        o_ref[...] = o
        lj_ref[...] = mx + jnp.log(den)
        a_in = o * (g_attn * _sig(g_attn))
        ain_ref[...] = a_in.astype(BF16)
        a_out = _dot(a_in.astype(BF16), wao_ref[...])
        aout_ref[...] = a_out.astype(BF16)
        z = cg * u
        rows = lax.broadcasted_iota(jnp.int32, (tm, D), 0)
        c6, c7 = zc_ref[6:7, :], zc_ref[7:8, :]
        z1 = jnp.where(rows == 0, c7, pltpu.roll(z, 1, 0))
        z2 = jnp.where(rows == 0, c6, jnp.where(rows == 1, c7, pltpu.roll(z, 2, 0)))
        zc_ref[...] = z[tm - 8:tm, :]
        y_conv = (cw_ref[0:1, :] * z2 + cw_ref[1:2, :] * z1 + cw_ref[2:3, :] * z) + cb_ref[...]
        yc_ref[...] = y_conv.astype(BF16)
        s_in = bg * y_conv * (g_conv * _sig(g_conv))
        sin_ref[...] = s_in.astype(BF16)
        s_out = _dot(s_in.astype(BF16), wco_ref[...])
        sout_ref[...] = s_out.astype(BF16)
        merged = _sig(m_attn) * a_out + _sig(m_conv) * s_out
        mrg_ref[...] = merged.astype(BF16)
        y = _dot(merged.astype(BF16), wo_ref[...])
        gate1 = 1.0 + gate_ref[0]
        r = ALPHA * x_ref[...] + gate1 * y
        mu = jnp.mean(r, axis=1, keepdims=True)
        rc = r - mu
        rstd = lax.rsqrt(jnp.mean(rc * rc, axis=1, keepdims=True) + LN_EPS)
        xhat = rc * rstd
        diff = (xhat * lng_ref[...] + lnb_ref[...]) - t_ref[...]
        dout = diff * (1.0 / D)
        vec_ref[0:1, :] += jnp.sum(dout * xhat, axis=0, keepdims=True)
        vec_ref[1:2, :] += jnp.sum(dout, axis=0, keepdims=True)
        vec_ref[2:3, :] += jnp.sum(diff * diff, axis=0, keepdims=True)
        dxh = dout * lng_ref[...]
        dr = rstd * (dxh - jnp.mean(dxh, axis=1, keepdims=True) - xhat * jnp.mean(dxh * xhat, axis=1, keepdims=True))
        dxr_ref[...] = ALPHA * dr
        dy_ref[...] = (dr * gate1).astype(BF16)
        dgate_ref[0] += jnp.sum(dr * y, axis=0, keepdims=True)

    tok = lambda w: pl.BlockSpec((tm, w), lambda b, i: (b * spt + i, 0))
    const = lambda s: pl.BlockSpec(s, lambda b, i: (0,) * len(s))
    per_seq = pl.BlockSpec((1, 1, D), lambda b, i: (b, 0, 0))
    outs = pl.pallas_call(
        body, name="mix_forward", grid=(bsz, spt),
        out_shape=[HBM_OUT((t, AW), BF16), HBM_OUT((t, D), BF16), HBM_OUT((t, D), BF16), HBM_OUT((t, D), BF16), HBM_OUT((t, D), BF16),
                   HBM_OUT((t, D), BF16), HBM_OUT((t, D), BF16), HBM_OUT((t, AW), F32), HBM_OUT((t, AW), F32), HBM_OUT((t, D), F32),
                   SDS((8, D), F32), SDS((bsz, 1, D), F32)],
        in_specs=[tok(NGATE)] + [tok(AW)] * 6 + [tok(D), tok(D), per_seq, const((AW, D)), const((D, D)), const((D, D)),
                                                 const((3, D)), const((1, D)), const((1, D)), const((1, D))],
        out_specs=[tok(AW), tok(D), tok(D), tok(D), tok(D), tok(D), tok(D), tok(AW), tok(AW), tok(D), const((8, D)), per_seq],
        scratch_shapes=[pltpu.VMEM((8, D), F32)],
        compiler_params=_cp(("arbitrary", "arbitrary"), VMEM_CAP),
    )(_in_hbm(gates), *map(_in_hbm, og), *map(_in_hbm, lg), _in_hbm(x2), _in_hbm(tgt), gate, w_ao, w_co, w_o, conv_w, conv_b, ln_g, ln_b)
    return outs


def _mix_backward(gates, dy, a_out, s_out, y_conv, o, lj, w_ao, w_co, w_o, conv_w, vec_f, bsz, seq, tm=256):
    t = dy.shape[0]
    spt = seq // tm

    def body(g_ref, dy_ref, aout_ref, sout_ref, yc_ref, o_ref, lj_ref, wao_ref, wco_ref, wo_ref, cw_ref, vecf_ref,
             dg_ref, do_ref, dl_ref, daout_ref, dsout_ref, vec_ref, car_ref):
        b, i = pl.program_id(0), pl.program_id(1)

        @pl.when((b == 0) & (i == 0))
        def _():
            vec_ref[...] = vecf_ref[...]

        @pl.when(i == 0)
        def _():
            car_ref[...] = jnp.zeros_like(car_ref)

        g_attn, u, bg, cg, g_conv, m_attn, m_conv = (g_ref[:, lo:hi].astype(F32) for lo, hi in GATE_COLS)
        dmerged = _dot_nt(dy_ref[...], wo_ref[...])
        sa, sc = _sig(m_attn), _sig(m_conv)
        da_out = (dmerged * sa).astype(BF16)
        ds_out = (dmerged * sc).astype(BF16)
        daout_ref[...] = da_out
        dsout_ref[...] = ds_out
        dg_ref[:, 4608:5632] = (dmerged * aout_ref[...].astype(F32) * (sa * (1.0 - sa))).astype(BF16)
        dg_ref[:, 5632:6656] = (dmerged * sout_ref[...].astype(F32) * (sc * (1.0 - sc))).astype(BF16)
        da_in = _dot_nt(da_out, wao_ref[...])
        ds_in = _dot_nt(ds_out, wco_ref[...])
        sga = _sig(g_attn)
        o = o_ref[...]
        do = da_in * (g_attn * sga)
        do_ref[...] = do
        dg_ref[:, 0:512] = (da_in * o * (sga * (1.0 + g_attn * (1.0 - sga)))).astype(BF16)
        prod = do * o
        lane = lax.broadcasted_iota(jnp.int32, (tm, HD), 1)
        for j in range(4):
            cs = slice(j * HD, (j + 1) * HD)
            delta = jnp.sum(prod[:, cs], axis=1, keepdims=True)
            dl_ref[:, cs] = jnp.where(lane < 64, lj_ref[:, cs], delta)
        sgc = _sig(g_conv)
        silu_c = g_conv * sgc
        yc = yc_ref[...].astype(F32)
        dg_ref[:, 1536:2560] = (ds_in * yc * silu_c).astype(BF16)
        dg_ref[:, 3584:4608] = (ds_in * bg * yc * (sgc * (1.0 + g_conv * (1.0 - sgc)))).astype(BF16)
        dyc = ds_in * bg * silu_c
        rows = lax.broadcasted_iota(jnp.int32, (tm, D), 0)
        c0, c1 = car_ref[0:1, :], car_ref[1:2, :]
        n1 = jnp.where(rows == tm - 1, c0, pltpu.roll(dyc, tm - 1, 0))
        n2 = jnp.where(rows == tm - 2, c0, jnp.where(rows == tm - 1, c1, pltpu.roll(dyc, tm - 2, 0)))
        car_ref[...] = dyc[0:8, :]
        dz = cw_ref[2:3, :] * dyc + cw_ref[1:2, :] * n1 + cw_ref[0:1, :] * n2
        z = cg * u
        dg_ref[:, 512:1536] = (dz * cg).astype(BF16)
        dg_ref[:, 2560:3584] = (dz * u).astype(BF16)
        vec_ref[3:4, :] += jnp.sum(n2 * z, axis=0, keepdims=True)
        vec_ref[4:5, :] += jnp.sum(n1 * z, axis=0, keepdims=True)
        vec_ref[5:6, :] += jnp.sum(dyc * z, axis=0, keepdims=True)
        vec_ref[6:7, :] += jnp.sum(dyc, axis=0, keepdims=True)

    tok = lambda w: pl.BlockSpec((tm, w), lambda b, i: (b * spt + (spt - 1 - i), 0))
    const = lambda s: pl.BlockSpec(s, lambda b, i: (0,) * len(s))
    return pl.pallas_call(
        body, name="mix_backward", grid=(bsz, spt),
        out_shape=[HBM_OUT((t, NGATE), BF16), HBM_OUT((t, AW), F32), HBM_OUT((t, AW), F32), HBM_OUT((t, D), BF16), HBM_OUT((t, D), BF16),
                   SDS((8, D), F32)],
        in_specs=[tok(NGATE), tok(D), tok(D), tok(D), tok(D), tok(AW), tok(AW), const((AW, D)), const((D, D)), const((D, D)), const((3, D)),
                  const((8, D))],
        out_specs=[tok(NGATE), tok(AW), tok(AW), tok(D), tok(D), const((8, D))],
        scratch_shapes=[pltpu.VMEM((8, D), F32)],
        compiler_params=_cp(("arbitrary", "arbitrary"), VMEM_CAP),
    )(*map(_in_hbm, (gates, dy, a_out, s_out, y_conv, o, lj)), w_ao, w_co, w_o, conv_w, vec_f)


def _scatter_copies(src, land, send_sems, recv_sems):
    x, y, c = _place()
    chips = [(1 - x, y), (x, 1 - y), (1 - x, 1 - y)]
    return [pltpu.make_async_remote_copy(src_ref=src[a].at[2 * cx + cy], dst_ref=land[a].at[r], send_sem=send_sems.at[3 * a + r],
                                         recv_sem=recv_sems.at[3 * a + r], device_id=(cx, cy, c), device_id_type=MESH)
            for a in range(len(src)) for r, (cx, cy) in enumerate(chips)]


def _halves_out(a):
    kind, nr, nc = W_CUTS[a]
    shape = (nr // 2, W_FULL[a][1]) if kind == "col" else (NCHIP, nr // 2, nc)
    return [SDS(shape, F32), SDS(shape, BF16)]


def _write_halves(a, acc_ref, c, mine_ref, theirs_ref):
    kind, nr, nc = W_CUTS[a]
    hr = nr // 2
    if kind == "col":
        mine_ref[...] = acc_ref[pl.ds(pl.multiple_of(c * hr, hr), hr), :]
        theirs_ref[...] = acc_ref[pl.ds(pl.multiple_of((1 - c) * hr, hr), hr), :].astype(BF16)
    else:
        for k in range(NCHIP):
            mine_ref[k] = acc_ref[pl.ds(pl.multiple_of(k * nr + c * hr, hr), hr), :]
            theirs_ref[k] = acc_ref[pl.ds(pl.multiple_of(k * nr + (1 - c) * hr, hr), hr), :].astype(BF16)


def _out_weight_grads(a_in, da_out, s_in, ds_out, merged, dy, core, tk=1024):
    t = dy.shape[0]
    nt = t // tk

    def body(c_ref, ain_ref, da_ref, sin_ref, ds_ref, m_ref, dy_ref, *rest):
        outs, (gao, gco, go) = rest[:6], rest[6:]

        @pl.when(pl.program_id(0) == 0)
        def _():
            gao[...] = jnp.zeros_like(gao)
            gco[...] = jnp.zeros_like(gco)
            go[...] = jnp.zeros_like(go)

        gao[...] += _dot_tn(ain_ref[...], da_ref[...])
        gco[...] += _dot_tn(sin_ref[...], ds_ref[...])
        go[...] += _dot_tn(m_ref[...], dy_ref[...])

        @pl.when(pl.program_id(0) == nt - 1)
        def _():
            for a, acc in ((1, gao), (2, gco), (3, go)):
                _write_halves(a, acc, c_ref[0], outs[2 * a - 2], outs[2 * a - 1])

    tok = lambda w: pl.BlockSpec((tk, w), lambda i, cr: (i, 0))
    out_shape = _halves_out(1) + _halves_out(2) + _halves_out(3)
    outs = pl.pallas_call(
        body, name="out_weight_grads", out_shape=out_shape,
        grid_spec=pltpu.PrefetchScalarGridSpec(
            num_scalar_prefetch=1, grid=(nt,), in_specs=[tok(AW), tok(D), tok(D), tok(D), tok(D), tok(D)],
            out_specs=[pl.BlockSpec(o.shape, lambda i, cr, nd=len(o.shape): (0,) * nd) for o in out_shape],
            scratch_shapes=[pltpu.VMEM((AW, D), F32), pltpu.VMEM((D, D), F32), pltpu.VMEM((D, D), F32)]),
        compiler_params=_cp(("arbitrary",), VMEM_CAP),
    )(core, a_in, da_out, s_in, ds_out, merged, dy)
    return [(outs[0], outs[1]), (outs[2], outs[3]), (outs[4], outs[5])]


def _input_grad(dq, dk, dv, dgates, w, x2, dxr, sc1p, seq, sums, tm=512):
    t = x2.shape[0]
    nt = t // tm
    spt = seq // tm
    bsz = t // seq
    n = len(sums)
    gblk = NGATE // 4
    nsteps = 3 + 4

    def body(dq_ref, dk_ref, dv_ref, dg_ref, wq_ref, wg_ref, x_ref, dxr_ref, sc_ref, *rest):
        src, (dx_ref, dsh_ref, dsc_ref), land = rest[:n], rest[n:n + 3], rest[n + 3:2 * n + 3]
        acc_ref, send_sems, recv_sems = rest[2 * n + 3:]
        j, i = pl.program_id(0), pl.program_id(1)
        copies = _scatter_copies(src, land, send_sems, recv_sems)
        rows = pl.ds(pl.multiple_of(i * tm, tm), tm)

        @pl.when((i == 0) & (j == 0))
        def _():
            for cp in copies:
                cp.start()

        for k, ref in enumerate((dq_ref, dk_ref, dv_ref)):
            @pl.when(j == k)
            def _(k=k, ref=ref):
                part = _dot_nt(ref[...], wq_ref[...])
                if k == 0:
                    acc_ref[rows, :] = part
                else:
                    acc_ref[rows, :] += part

        @pl.when((j >= 3) & (j < nsteps - 1))
        def _():
            acc_ref[rows, :] += _dot_nt(dg_ref[...], wg_ref[...])

        @pl.when(j == nsteps - 1)
        def _():
            dh = acc_ref[rows, :] + _dot_nt(dg_ref[...], wg_ref[...])
            dx_ref[...] = dh * sc_ref[0] + dxr_ref[...]

            @pl.when(i % spt == 0)
            def _():
                dsh_ref[...] = jnp.zeros_like(dsh_ref)
                dsc_ref[...] = jnp.zeros_like(dsc_ref)

            dsh_ref[0] += jnp.sum(dh, axis=0, keepdims=True)
            dsc_ref[0] += jnp.sum(dh * x_ref[...], axis=0, keepdims=True)

        @pl.when((i == nt - 1) & (j == nsteps - 1))
        def _():
            for cp in copies:
                cp.wait()

    def held(k):
        return lambda j, i: (jnp.where(j == k, i, jnp.where(j < k, 0, nt - 1)), 0)

    last = lambda j, i: (jnp.where(j == nsteps - 1, i, 0), 0)
    outs = pl.pallas_call(
        body, name="input_grad", grid=(nsteps, nt),
        out_shape=[SDS((t, D), F32), SDS((bsz, 1, D), F32), SDS((bsz, 1, D), F32)] + [SDS((3,) + s.shape[1:], BF16) for s in sums],
        in_specs=[pl.BlockSpec((tm, QW), held(0)), pl.BlockSpec((tm, QW), held(1)), pl.BlockSpec((tm, QW), held(2)),
                  pl.BlockSpec((tm, gblk), lambda j, i: (jnp.where(j >= 3, i, 0), jnp.clip(j - 3, 0, 3))),
                  pl.BlockSpec((D, QW), lambda j, i: (0, jnp.minimum(j, 2))),
                  pl.BlockSpec((pl.Element(D), pl.Element(gblk)), lambda j, i: (0, pl.multiple_of(3 * QW + gblk * jnp.clip(j - 3, 0, 3), 128))),
                  pl.BlockSpec((tm, D), last), pl.BlockSpec((tm, D), last),
                  pl.BlockSpec((1, 1, D), lambda j, i: (jnp.where(j == nsteps - 1, i // spt, 0), 0, 0))] + [ANY] * n,
        out_specs=[pl.BlockSpec((tm, D), last),
                   pl.BlockSpec((1, 1, D), lambda j, i: (jnp.where(j == nsteps - 1, i // spt, 0), 0, 0)),
                   pl.BlockSpec((1, 1, D), lambda j, i: (jnp.where(j == nsteps - 1, i // spt, 0), 0, 0))] + [ANY] * n,
        scratch_shapes=[pltpu.VMEM((t, D), F32), pltpu.SemaphoreType.DMA((3 * NCHIP,)), pltpu.SemaphoreType.DMA((3 * NCHIP,))],
        compiler_params=_cp(("arbitrary", "arbitrary"), VMEM_CAP, side=True),
    )(*map(_in_hbm, (dq, dk, dv, dgates, w, w, x2, dxr)), sc1p, *sums)
    return outs[0], outs[1], outs[2], outs[3:]


def _in_weight_grad(ht, dq, dk, dv, dgates, core, sums):
    t = ht.shape[1]
    hr = D // 2
    n = len(sums)

    def body(c_ref, ht_ref, dq_ref, dk_ref, dv_ref, dg_ref, *rest):
        src, mine_ref, got_ref, land = rest[:n], rest[n], rest[n + 1], rest[n + 2:2 * n + 2]
        acc_ref, their_buf, send_sems, recv_sems, tile_send, tile_recv = rest[2 * n + 2:]
        j = pl.program_id(0)
        slot = j % 2
        px, py, pc = _place()
        copies = _scatter_copies(src, land, send_sems, recv_sems)

        def to_sibling(step, k):
            return pltpu.make_async_remote_copy(src_ref=their_buf.at[k], dst_ref=got_ref.at[:, pl.ds(pl.multiple_of(step * TN, TN), TN)],
                                                send_sem=tile_send.at[k], recv_sem=tile_recv.at[0], device_id=(px, py, 1 - pc),
                                                device_id_type=MESH)

        @pl.when(j == 0)
        def _():
            for cp in copies:
                cp.start()

        @pl.when(j >= 2)
        def _():
            to_sibling(j - 2, slot).wait_send()

        for k, ref in enumerate((dq_ref, dk_ref, dv_ref)):
            @pl.when((j >= k * NQT) & (j < (k + 1) * NQT))
            def _(ref=ref):
                acc_ref[...] = _dot(ht_ref[...], ref[...])

        @pl.when(j >= 3 * NQT)
        def _():
            acc_ref[...] = _dot(ht_ref[...], dg_ref[...])

        _write_halves(0, acc_ref, c_ref[0], mine_ref, their_buf.at[slot])
        to_sibling(j, slot).start()

        @pl.when(j == NPT - 1)
        def _():
            to_sibling(j - 1, 1 - slot).wait_send()
            to_sibling(j, slot).wait_send()
            pltpu.make_async_remote_copy(src_ref=got_ref, dst_ref=got_ref, send_sem=tile_send.at[0], recv_sem=tile_recv.at[0],
                                         device_id=(px, py, 1 - pc), device_id_type=MESH).wait_recv()
            for cp in copies:
                cp.wait()

    def part(k):
        return pl.BlockSpec((t, TN), lambda j, cr: (0, jnp.clip(j - k * NQT, 0, NQT - 1)))

    outs = pl.pallas_call(
        body, name="in_weight_grad", out_shape=[SDS((hr, NCOL), F32), SDS((hr, NCOL), BF16)] + [SDS((3,) + v.shape[1:], BF16) for v in sums],
        grid_spec=pltpu.PrefetchScalarGridSpec(
            num_scalar_prefetch=1, grid=(NPT,),
            in_specs=[pl.BlockSpec((D, t), lambda j, cr: (0, 0)), part(0), part(1), part(2),
                      pl.BlockSpec((t, TN), lambda j, cr: (0, jnp.maximum(j - 3 * NQT, 0)))] + [ANY] * n,
            out_specs=[pl.BlockSpec((hr, TN), lambda j, cr: (0, j)), ANY] + [ANY] * n,
            scratch_shapes=[pltpu.VMEM((D, TN), F32), pltpu.VMEM((2, hr, TN), BF16),
                            pltpu.SemaphoreType.DMA((3 * NCHIP,)), pltpu.SemaphoreType.DMA((3 * NCHIP,)),
                            pltpu.SemaphoreType.DMA((2,)), pltpu.SemaphoreType.DMA((1,))]),
        compiler_params=_cp(("arbitrary",), VMEM_CAP, side=True),
    )(core, *map(_in_hbm, (ht, dq, dk, dv, dgates)), *sums)
    return outs[0], outs[1], outs[2:]


def _sum_partials(gathered):
    def body(g_ref, o_ref):
        acc = g_ref[0]
        for k in range(1, 8):
            acc = acc + g_ref[k]
        o_ref[...] = acc

    return pl.pallas_call(body, name="sum_partials", out_shape=SDS(gathered.shape[1:], F32), in_specs=[VMEM_SPEC], out_specs=VMEM_SPEC)(gathered)


def _adamw(w, g, m, v, name, tr=256):
    r, cdim = w.shape
    tr = tr if cdim <= D else tr // 2
    tr = tr if (r % tr == 0 and r > tr) else r

    def body(w_ref, g_ref, m_ref, v_ref, go_ref, d_ref, nm_ref, nv_ref):
        gv = g_ref[...]
        go_ref[...] = gv
        nm = B1 * m_ref[...] + (1.0 - B1) * gv
        nv = B2 * v_ref[...] + (1.0 - B2) * (gv * gv)
        m_hat = nm / (1.0 - B1 ** STEP)
        v_hat = nv / (1.0 - B2 ** STEP)
        d_ref[...] = -LR * (m_hat / (jnp.sqrt(v_hat) + EPS) + WD * w_ref[...])
        nm_ref[...] = nm
        nv_ref[...] = nv

    spec = pl.BlockSpec((tr, cdim), lambda i: (i, 0))
    return pl.pallas_call(
        body, name=name, grid=(r // tr,), out_shape=[SDS((r, cdim), F32)] * 4, in_specs=[spec] * 4, out_specs=[spec] * 4,
        compiler_params=_cp(("parallel",), VMEM_CAP // 2),
    )(w, g, m, v)


def _t5_bucket(dist):
    n = jnp.maximum(dist, 1).astype(F32)
    large = MAX_EXACT + (jnp.log(n / MAX_EXACT) / math.log(MAX_DISTANCE / MAX_EXACT) * (N_BUCKETS - MAX_EXACT)).astype(jnp.int32)
    large = jnp.minimum(large, N_BUCKETS - 1)
    return jnp.where(dist < MAX_EXACT, dist, large)


def _band_buckets():
    a = jnp.arange(BLK)[:, None]
    b = jnp.arange(2 * BLK)[None, :]
    steps = jnp.maximum(a + BLK - b, 0)
    return jnp.stack([_t5_bucket(steps * d) for d in DILATIONS]).astype(jnp.int32)


def _pad_rows(a, rows=8):
    return jnp.pad(a, ((0, rows - a.shape[0]), (0, 0)))


def kernel(x, c, w_ada, b_ada, w_in, conv_w, conv_b, rel_bias, w_attn_out, w_conv_out, w_o, ln_g, ln_b, loss_target, m_w_ada, m_b_ada, m_w_in, m_conv_w, m_conv_b, m_rel_bias, m_w_attn_out, m_w_conv_out, m_w_o, m_ln_g, m_ln_b, v_w_ada, v_b_ada, v_w_in, v_conv_w, v_conv_b, v_rel_bias, v_w_attn_out, v_w_conv_out, v_w_o, v_ln_g, v_ln_b):
    bsz, seq, _ = x.shape
    t = bsz * seq
    mx, my, mc = _place()
    chip = 2 * mx + my
    dev = 4 * mx + 2 * my + mc
    x2 = x.reshape(t, D)
    tgt = loss_target.reshape(t, D)

    n_ada = w_ada.shape[2]
    n_cw = conv_w.shape[2]
    c_and_cw = jnp.concatenate([_pad_rows(c), jnp.pad(conv_w[0], ((0, 5), (0, D - n_cw)))], axis=0)
    mine, firsts = _to_bf16_windows([w[0] for w in (w_in, w_attn_out, w_conv_out, w_o)], c_and_cw)

    c_all = firsts[:, 0:bsz, :].reshape(8 * bsz, D)
    conv_w_f = firsts[0::2, 8:11, 0:n_cw].transpose(1, 0, 2).reshape(3, D)
    b_cols = lax.dynamic_slice(b_ada, (0, chip * n_ada), (1, n_ada))
    mod_part = _ada_forward(c_all, w_ada[0], b_cols)
    buckets = _band_buckets()
    bias, mod_parts = _bias_tables(rel_bias, buckets, mod_part)
    mod_all = mod_parts[0::2].transpose(1, 0, 2).reshape(8 * bsz, 3 * D)
    mod = lax.dynamic_slice(mod_all, (dev * bsz, 0), (bsz, 3 * D))
    shift = mod[:, 0:D].reshape(bsz, 1, D)
    sc1p = 1.0 + mod[:, D:2 * D].reshape(bsz, 1, D)
    gate = mod[:, 2 * D:].reshape(bsz, 1, D)

    h, ht = _modulate(x2, sc1p, shift, seq)
    tab = lax.dynamic_index_in_dim(jnp.asarray(_tile_tables()), chip, 0, keepdims=False)
    qkv, gates, (w_in_f, w_ao_f, w_co_f, w_o_f) = _project_gather(h, mine, tab)
    og, lg = [], []
    for g in range(3):
        o_g, l_g = _attn_forward(g, qkv, bias[g], bsz, seq)
        og.append(o_g)
        lg.append(l_g)
    (a_in, s_in, merged, dy, a_out, s_out, y_conv, o, lj, dxr, vec_f, dgate) = _mix_forward(
        gates, og, lg, x2, tgt, gate, w_ao_f, w_co_f, w_o_f, conv_w_f, conv_b, ln_g, ln_b, bsz, seq)

    dgates, do, dl, da_out, ds_out, vec = _mix_backward(gates, dy, a_out, s_out, y_conv, o, lj, w_ao_f, w_co_f, w_o_f, conv_w_f, vec_f, bsz, seq)
    core = jnp.reshape(mc, (1,)).astype(jnp.int32)
    small_grads = _out_weight_grads(a_in, da_out, s_in, ds_out, merged, dy, core)
    got_small = _swap_halves([theirs for _, theirs in small_grads], "swap_small_grad_halves")
    sums_small = _chip_sums([own for own, _ in small_grads], got_small, 1, "chip_sums_small")
    dqkv, dbs = None, []
    for g in range(3):
        dqkv, db = _attn_backward(g, qkv, do, dl, bias[g], dqkv, bsz, seq)
        dbs.append(db)
    dq, dk, dv = dqkv
    drb = _bias_grad(jnp.stack(dbs), buckets)
    drb = drb[:, :, 0:4].transpose(1, 0, 2).reshape(N_BUCKETS, 12)
    g_in_mine, got_in, landed_small = _in_weight_grad(ht, dq, dk, dv, dgates, core, [bf for _, bf in sums_small])
    sums_in = _chip_sums([g_in_mine], [got_in], 0, "chip_sums_in")
    grad_x, dshift, dscale, landed_in = _input_grad(dq, dk, dv, dgates, w_in_f, x2, dxr, sc1p, seq, [bf for _, bf in sums_in])
    halves = _reduce_mine([own for own, _ in sums_in + sums_small], list(landed_in) + list(landed_small))

    dmod = jnp.concatenate([dshift, dscale, dgate], axis=2).reshape(bsz * 3, D)
    drb_row = jnp.pad(drb.reshape(1, N_BUCKETS * 12), ((0, 0), (0, D - N_BUCKETS * 12)))
    vec = lax.dynamic_update_slice(vec, drb_row, (7, 0))
    packed = jnp.concatenate([vec, _pad_rows(dmod)], axis=0)
    (gw_in, gw_ao, gw_co, gw_o), gathered = _join_halves(halves, packed)
    small = _sum_partials(gathered)
    g_ln_g, g_ln_b, loss_lanes = small[0:1], small[1:2], small[2:3]
    g_conv_w_full, g_conv_b = small[3:6], small[6:7]
    g_rel_bias = small[7, 0:N_BUCKETS * 12].reshape(N_BUCKETS, 12)
    loss = 0.5 / D * jnp.sum(loss_lanes)
    dmod_all = gathered[:, 8:8 + 3 * bsz, :].reshape(8 * bsz, 3 * D)
    dmod_cols = lax.dynamic_slice(dmod_all, (0, chip * n_ada), (8 * bsz, n_ada))
    gw_ada, gb_ada = _ada_backward(c_all, dmod_cols, dmod_all)
    g_conv_w = lax.dynamic_slice(g_conv_w_full, (0, chip * n_cw), (3, n_cw))

    names = ["w_ada", "b_ada", "w_in", "conv_w", "conv_b", "rel_bias", "w_attn_out", "w_conv_out", "w_o", "ln_g", "ln_b"]
    two_d = lambda a: a.reshape(a.shape[-2:]) if a.ndim == 3 else a
    weights = dict(zip(names, map(two_d, (w_ada, b_ada, w_in, conv_w, conv_b, rel_bias, w_attn_out, w_conv_out, w_o, ln_g, ln_b))))
    ms = dict(zip(names, map(two_d, (m_w_ada, m_b_ada, m_w_in, m_conv_w, m_conv_b, m_rel_bias, m_w_attn_out, m_w_conv_out, m_w_o, m_ln_g, m_ln_b))))
    vs = dict(zip(names, map(two_d, (v_w_ada, v_b_ada, v_w_in, v_conv_w, v_conv_b, v_rel_bias, v_w_attn_out, v_w_conv_out, v_w_o, v_ln_g, v_ln_b))))
    grads = dict(zip(names, (gw_ada, gb_ada, gw_in, g_conv_w, g_conv_b, g_rel_bias, gw_ao, gw_co, gw_o, g_ln_g, g_ln_b)))
    shapes = dict(zip(names, (w_ada, b_ada, w_in, conv_w, conv_b, rel_bias, w_attn_out, w_conv_out, w_o, ln_g, ln_b)))
    grad_out, deltas, new_m, new_v = {}, {}, {}, {}
    for n in names:
        grad_out[n], deltas[n], new_m[n], new_v[n] = _adamw(weights[n], grads[n], ms[n], vs[n], f"adamw_{n}")
    shaped = lambda d: [d[n].reshape(shapes[n].shape) for n in names]
    return (loss, grad_x.reshape(bsz, seq, D), *shaped(grad_out), *shaped(deltas), *shaped(new_m), *shaped(new_v))
```
